```python
import math
import jax, jax.numpy as jnp
from jax import lax
import numpy as np

D_MODEL = 1024
BATCH = 8
SEQ = 16384
DEPTH = 2

N_MIXERS = 2
N_CONV_LAYERS = (DEPTH + N_MIXERS - 1) // N_MIXERS
N_ATTN_LAYERS = DEPTH // N_MIXERS

CONV_KERNEL = 31
CONV_EXPAND = 2

HEAD_DIM = 128
HEADS_PER_GROUP = D_MODEL // HEAD_DIM
DILATED_GROUPS = ((128, 1), (512, 4), (2048, 16))
N_GROUPS = len(DILATED_GROUPS)
N_TOTAL_HEADS = N_GROUPS * HEADS_PER_GROUP
GROUP_WIDTH = HEADS_PER_GROUP * HEAD_DIM
QKV_WIDTH = N_GROUPS * 3 * GROUP_WIDTH
BLOCK = 128
ALIBI_MAX = 8.0

D_FF = 2816
FFN_KERNEL = 3
EPS = 1e-6

kernel_name = "hybrid_conformer_dilated_attn_convffn"


def rmsnorm(x, g):
    xf = x.astype(jnp.float32)
    y = xf * lax.rsqrt(jnp.mean(xf * xf, axis=-1, keepdims=True) + EPS)
    return (y * g.astype(jnp.float32)).astype(x.dtype)


def layernorm(x, g, b):
    xf = x.astype(jnp.float32)
    mu = jnp.mean(xf, axis=-1, keepdims=True)
    var = jnp.mean(jnp.square(xf - mu), axis=-1, keepdims=True)
    y = (xf - mu) * lax.rsqrt(var + EPS)
    return (y * g.astype(jnp.float32) + b.astype(jnp.float32)).astype(x.dtype)


def causal_dwconv(x, w, b):
    k, c = w.shape
    y = lax.conv_general_dilated(
        x, w.astype(x.dtype)[:, None, :], window_strides=(1,), padding=[(k - 1, 0)],
        dimension_numbers=("NWC", "WIO", "NWC"), feature_group_count=c)
    return y + b.astype(x.dtype)


def conformer_conv_module(h, w_in, b_in, dw, dw_b, ln_g, ln_b, w_out, b_out):
    u = h @ w_in + b_in
    a, g = jnp.split(u, 2, axis=-1)
    u = a * jax.nn.sigmoid(g)
    u = causal_dwconv(u, dw, dw_b)
    u = jax.nn.silu(layernorm(u, ln_g, ln_b))
    return u @ w_out + b_out


def dilated_branch(q, k, v, slopes, window, dilation):
    b, s, h, dh = q.shape
    n_back = window // dilation
    L = s // dilation
    nb = -(-L // BLOCK)
    Lp = nb * BLOCK
    n = b * dilation

    def to_sub(t):
        t = t.reshape(b, L, dilation, h, dh).transpose(0, 2, 1, 3, 4).reshape(n, L, h, dh)
        return jnp.pad(t, ((0, 0), (0, Lp - L), (0, 0), (0, 0)))

    def windows(t):
        t = jnp.pad(t, ((0, 0), (BLOCK, 0), (0, 0), (0, 0))).reshape(n, nb + 1, BLOCK, h, dh)
        return jnp.concatenate([t[:, :-1], t[:, 1:]], axis=2)

    qb = to_sub(q).reshape(n, nb, BLOCK, h, dh).astype(jnp.float32)
    kw = windows(to_sub(k)).astype(jnp.float32)
    vw = windows(to_sub(v)).astype(jnp.float32)

    scores = jnp.einsum("nbqhd,nbkhd->nbhqk", qb, kw) * (dh ** -0.5)
    qi = jnp.arange(BLOCK)[:, None]
    ki = jnp.arange(2 * BLOCK)[None, :]
    delta = qi + BLOCK - ki
    key_idx = jnp.arange(nb)[:, None, None] * BLOCK + ki[None] - BLOCK
    valid = (delta >= 0)[None] & (delta <= n_back)[None] & (key_idx >= 0)
    bias = -slopes.astype(jnp.float32)[:, None, None] * (delta * dilation).astype(jnp.float32)[None]
    scores = jnp.where(valid[None, :, None], scores + bias[None, None], -1e30)

    m = jnp.max(scores, axis=-1, keepdims=True)
    p = jnp.exp(scores - m)
    den = jnp.sum(p, axis=-1)
    lse = (m[..., 0] + jnp.log(den)).transpose(0, 1, 3, 2)
    o = jnp.einsum("nbhqk,nbkhd->nbqhd", p, vw) / den.transpose(0, 1, 3, 2)[..., None]

    o = o.reshape(n, Lp, h, dh)[:, :L].reshape(b, dilation, L, h, dh)
    o = o.transpose(0, 2, 1, 3, 4).reshape(b, s, h, dh)
    lse = lse.reshape(n, Lp, h)[:, :L].reshape(b, dilation, L, h)
    lse = lse.transpose(0, 2, 1, 3).reshape(b, s, h)
    return o, lse


def dilated_attention_mixer(h, w_qkv, q_norm, k_norm, w_out):
    b, s, _ = h.shape
    qkv = (h @ w_qkv).reshape(b, s, N_GROUPS, 3, HEADS_PER_GROUP, HEAD_DIM)
    slopes = jnp.asarray(
        2.0 ** (-ALIBI_MAX * (np.arange(N_TOTAL_HEADS, dtype=np.float32) + 1.0) / N_TOTAL_HEADS),
        dtype=jnp.float32).reshape(N_GROUPS, HEADS_PER_GROUP)
    qn = q_norm.reshape(N_GROUPS, HEADS_PER_GROUP, HEAD_DIM)
    kn = k_norm.reshape(N_GROUPS, HEADS_PER_GROUP, HEAD_DIM)
    outs, lses = [], []
    for g, (window, dilation) in enumerate(DILATED_GROUPS):
        q = rmsnorm(qkv[:, :, g, 0], qn[g])
        k = rmsnorm(qkv[:, :, g, 1], kn[g])
        o, lse = dilated_branch(q, k, qkv[:, :, g, 2], slopes[g], window, dilation)
        outs.append(o)
        lses.append(lse)
    wts = jax.nn.softmax(jnp.stack(lses, axis=0), axis=0)
    o = jnp.sum(wts[..., None] * jnp.stack(outs, axis=0), axis=0)
    return o.astype(h.dtype).reshape(b, s, GROUP_WIDTH) @ w_out


def conv_ffn(h, w_up, dw, dw_b, w_down):
    u = causal_dwconv(h @ w_up, dw, dw_b)
    g, v = jnp.split(u, 2, axis=-1)
    return (jax.nn.silu(g) * v) @ w_down


def _fwd_setup_inputs(seed: int = 0) -> dict:
    key = jax.random.key(seed)
    ks = jax.random.split(key, 24)
    f32 = jnp.float32
    nrm = lambda k, shape, scale: jax.random.normal(k, shape, f32) * scale
    D, Lc, La = D_MODEL, N_CONV_LAYERS, N_ATTN_LAYERS
    return {
        "x": jax.random.normal(ks[0], (BATCH, SEQ, D), f32),
        "norm_mix": 1.0 + nrm(ks[1], (DEPTH, D), 0.02),
        "norm_ffn": 1.0 + nrm(ks[2], (DEPTH, D), 0.02),
        "cm_w_in": nrm(ks[3], (Lc, D, CONV_EXPAND * D), D ** -0.5),
        "cm_b_in": nrm(ks[4], (Lc, CONV_EXPAND * D), 0.02),
        "cm_dw": nrm(ks[5], (Lc, CONV_KERNEL, D), CONV_KERNEL ** -0.5),
        "cm_dw_b": nrm(ks[6], (Lc, D), 0.02),
        "cm_ln_g": 1.0 + nrm(ks[7], (Lc, D), 0.02),
        "cm_ln_b": nrm(ks[8], (Lc, D), 0.02),
        "cm_w_out": nrm(ks[9], (Lc, D, D), D ** -0.5),
        "cm_b_out": nrm(ks[10], (Lc, D), 0.02),
        "at_w_qkv": nrm(ks[11], (La, D, QKV_WIDTH), D ** -0.5),
        "at_q_norm": 1.0 + nrm(ks[12], (La, N_TOTAL_HEADS, HEAD_DIM), 0.02),
        "at_k_norm": 1.0 + nrm(ks[13], (La, N_TOTAL_HEADS, HEAD_DIM), 0.02),
        "at_w_out": nrm(ks[14], (La, GROUP_WIDTH, D), GROUP_WIDTH ** -0.5),
        "ff_w_up": nrm(ks[15], (DEPTH, D, 2 * D_FF), D ** -0.5),
        "ff_dw": nrm(ks[16], (DEPTH, FFN_KERNEL, 2 * D_FF), FFN_KERNEL ** -0.5),
        "ff_dw_b": nrm(ks[17], (DEPTH, 2 * D_FF), 0.02),
        "ff_w_down": nrm(ks[18], (DEPTH, D_FF, D), D_FF ** -0.5),
    }


def _fwd_reference(x, norm_mix, norm_ffn, cm_w_in, cm_b_in, cm_dw, cm_dw_b, cm_ln_g, cm_ln_b,
              cm_w_out, cm_b_out, at_w_qkv, at_q_norm, at_k_norm, at_w_out,
              ff_w_up, ff_dw, ff_dw_b, ff_w_down):
    for i in range(DEPTH):
        h = rmsnorm(x, norm_mix[i])
        j = i // N_MIXERS
        if i % N_MIXERS == 0:
            y = conformer_conv_module(h, cm_w_in[j], cm_b_in[j], cm_dw[j], cm_dw_b[j],
                                      cm_ln_g[j], cm_ln_b[j], cm_w_out[j], cm_b_out[j])
        else:
            y = dilated_attention_mixer(h, at_w_qkv[j], at_q_norm[j], at_k_norm[j], at_w_out[j])
        x = x + y
        x = x + conv_ffn(rmsnorm(x, norm_ffn[i]), ff_w_up[i], ff_dw[i], ff_dw_b[i], ff_w_down[i])
    return x


import jax as _jax
import jax.numpy as _jnp

TWIN_FORMAT = 'train_step'
FWD_PARAMS = ['x', 'norm_mix', 'norm_ffn', 'cm_w_in', 'cm_b_in', 'cm_dw', 'cm_dw_b', 'cm_ln_g', 'cm_ln_b', 'cm_w_out', 'cm_b_out', 'at_w_qkv', 'at_q_norm', 'at_k_norm', 'at_w_out', 'ff_w_up', 'ff_dw', 'ff_dw_b', 'ff_w_down']
TWIN_WEIGHTS = ['norm_mix', 'norm_ffn', 'cm_w_in', 'cm_b_in', 'cm_dw', 'cm_dw_b', 'cm_ln_g', 'cm_ln_b', 'cm_w_out', 'cm_b_out', 'at_w_qkv', 'at_q_norm', 'at_k_norm', 'at_w_out', 'ff_w_up', 'ff_dw', 'ff_dw_b', 'ff_w_down']
TWIN_DIFF_INPUT = 'x'
TWIN_INPUTS = ['x', 'norm_mix', 'norm_ffn', 'cm_w_in', 'cm_b_in', 'cm_dw', 'cm_dw_b', 'cm_ln_g', 'cm_ln_b', 'cm_w_out', 'cm_b_out', 'at_w_qkv', 'at_q_norm', 'at_k_norm', 'at_w_out', 'ff_w_up', 'ff_dw', 'ff_dw_b', 'ff_w_down', 'loss_target', 'm_norm_mix', 'm_norm_ffn', 'm_cm_w_in', 'm_cm_b_in', 'm_cm_dw', 'm_cm_dw_b', 'm_cm_ln_g', 'm_cm_ln_b', 'm_cm_w_out', 'm_cm_b_out', 'm_at_w_qkv', 'm_at_q_norm', 'm_at_k_norm', 'm_at_w_out', 'm_ff_w_up', 'm_ff_dw', 'm_ff_dw_b', 'm_ff_w_down', 'v_norm_mix', 'v_norm_ffn', 'v_cm_w_in', 'v_cm_b_in', 'v_cm_dw', 'v_cm_dw_b', 'v_cm_ln_g', 'v_cm_ln_b', 'v_cm_w_out', 'v_cm_b_out', 'v_at_w_qkv', 'v_at_q_norm', 'v_at_k_norm', 'v_at_w_out', 'v_ff_w_up', 'v_ff_dw', 'v_ff_dw_b', 'v_ff_w_down']
TWIN_OUTPUTS = ['loss', 'grad_x', 'grad_norm_mix', 'grad_norm_ffn', 'grad_cm_w_in', 'grad_cm_b_in', 'grad_cm_dw', 'grad_cm_dw_b', 'grad_cm_ln_g', 'grad_cm_ln_b', 'grad_cm_w_out', 'grad_cm_b_out', 'grad_at_w_qkv', 'grad_at_q_norm', 'grad_at_k_norm', 'grad_at_w_out', 'grad_ff_w_up', 'grad_ff_dw', 'grad_ff_dw_b', 'grad_ff_w_down', 'delta_norm_mix', 'delta_norm_ffn', 'delta_cm_w_in', 'delta_cm_b_in', 'delta_cm_dw', 'delta_cm_dw_b', 'delta_cm_ln_g', 'delta_cm_ln_b', 'delta_cm_w_out', 'delta_cm_b_out', 'delta_at_w_qkv', 'delta_at_q_norm', 'delta_at_k_norm', 'delta_at_w_out', 'delta_ff_w_up', 'delta_ff_dw', 'delta_ff_dw_b', 'delta_ff_w_down', 'new_m_norm_mix', 'new_m_norm_ffn', 'new_m_cm_w_in', 'new_m_cm_b_in', 'new_m_cm_dw', 'new_m_cm_dw_b', 'new_m_cm_ln_g', 'new_m_cm_ln_b', 'new_m_cm_w_out', 'new_m_cm_b_out', 'new_m_at_w_qkv', 'new_m_at_q_norm', 'new_m_at_k_norm', 'new_m_at_w_out', 'new_m_ff_w_up', 'new_m_ff_dw', 'new_m_ff_dw_b', 'new_m_ff_w_down', 'new_v_norm_mix', 'new_v_norm_ffn', 'new_v_cm_w_in', 'new_v_cm_b_in', 'new_v_cm_dw', 'new_v_cm_dw_b', 'new_v_cm_ln_g', 'new_v_cm_ln_b', 'new_v_cm_w_out', 'new_v_cm_b_out', 'new_v_at_w_qkv', 'new_v_at_q_norm', 'new_v_at_k_norm', 'new_v_at_w_out', 'new_v_ff_w_up', 'new_v_ff_dw', 'new_v_ff_dw_b', 'new_v_ff_w_down']
TWIN_LEAF_KINDS = {'loss': 'loss', 'grad_x': 'grad_x', 'grad_norm_mix': 'grad_w', 'grad_norm_ffn': 'grad_w', 'grad_cm_w_in': 'grad_w', 'grad_cm_b_in': 'grad_w', 'grad_cm_dw': 'grad_w', 'grad_cm_dw_b': 'grad_w', 'grad_cm_ln_g': 'grad_w', 'grad_cm_ln_b': 'grad_w', 'grad_cm_w_out': 'grad_w', 'grad_cm_b_out': 'grad_w', 'grad_at_w_qkv': 'grad_w', 'grad_at_q_norm': 'grad_w', 'grad_at_k_norm': 'grad_w', 'grad_at_w_out': 'grad_w', 'grad_ff_w_up': 'grad_w', 'grad_ff_dw': 'grad_w', 'grad_ff_dw_b': 'grad_w', 'grad_ff_w_down': 'grad_w', 'delta_norm_mix': 'delta_w', 'delta_norm_ffn': 'delta_w', 'delta_cm_w_in': 'delta_w', 'delta_cm_b_in': 'delta_w', 'delta_cm_dw': 'delta_w', 'delta_cm_dw_b': 'delta_w', 'delta_cm_ln_g': 'delta_w', 'delta_cm_ln_b': 'delta_w', 'delta_cm_w_out': 'delta_w', 'delta_cm_b_out': 'delta_w', 'delta_at_w_qkv': 'delta_w', 'delta_at_q_norm': 'delta_w', 'delta_at_k_norm': 'delta_w', 'delta_at_w_out': 'delta_w', 'delta_ff_w_up': 'delta_w', 'delta_ff_dw': 'delta_w', 'delta_ff_dw_b': 'delta_w', 'delta_ff_w_down': 'delta_w', 'new_m_norm_mix': 'new_m', 'new_m_norm_ffn': 'new_m', 'new_m_cm_w_in': 'new_m', 'new_m_cm_b_in': 'new_m', 'new_m_cm_dw': 'new_m', 'new_m_cm_dw_b': 'new_m', 'new_m_cm_ln_g': 'new_m', 'new_m_cm_ln_b': 'new_m', 'new_m_cm_w_out': 'new_m', 'new_m_cm_b_out': 'new_m', 'new_m_at_w_qkv': 'new_m', 'new_m_at_q_norm': 'new_m', 'new_m_at_k_norm': 'new_m', 'new_m_at_w_out': 'new_m', 'new_m_ff_w_up': 'new_m', 'new_m_ff_dw': 'new_m', 'new_m_ff_dw_b': 'new_m', 'new_m_ff_w_down': 'new_m', 'new_v_norm_mix': 'new_v', 'new_v_norm_ffn': 'new_v', 'new_v_cm_w_in': 'new_v', 'new_v_cm_b_in': 'new_v', 'new_v_cm_dw': 'new_v', 'new_v_cm_dw_b': 'new_v', 'new_v_cm_ln_g': 'new_v', 'new_v_cm_ln_b': 'new_v', 'new_v_cm_w_out': 'new_v', 'new_v_cm_b_out': 'new_v', 'new_v_at_w_qkv': 'new_v', 'new_v_at_q_norm': 'new_v', 'new_v_at_k_norm': 'new_v', 'new_v_at_w_out': 'new_v', 'new_v_ff_w_up': 'new_v', 'new_v_ff_dw': 'new_v', 'new_v_ff_dw_b': 'new_v', 'new_v_ff_w_down': 'new_v'}


def _forward(args):
    return _fwd_reference(*[args[k] for k in FWD_PARAMS])


def _output_shape():
    def fwd():
        inp = _fwd_setup_inputs(0)
        return _fwd_reference(*[inp[k] for k in FWD_PARAMS])
    out = _jax.eval_shape(fwd)
    return out.shape, out.dtype

N_MICROBATCH = 1
ADAM_LR = 0.001
ADAM_B1 = 0.9
ADAM_B2 = 0.999
ADAM_EPS = 1e-08
ADAM_WD = 0.01
ADAM_STEP = 10
PER_EXAMPLE_BATCH_AXIS = {'x': 0, 'loss_target': 0}
SHARED_INPUTS = []
_WEIGHT_DTYPES = {'norm_mix': _jnp.float32, 'norm_ffn': _jnp.float32, 'cm_w_in': _jnp.float32, 'cm_b_in': _jnp.float32, 'cm_dw': _jnp.float32, 'cm_dw_b': _jnp.float32, 'cm_ln_g': _jnp.float32, 'cm_ln_b': _jnp.float32, 'cm_w_out': _jnp.float32, 'cm_b_out': _jnp.float32, 'at_w_qkv': _jnp.float32, 'at_q_norm': _jnp.float32, 'at_k_norm': _jnp.float32, 'at_w_out': _jnp.float32, 'ff_w_up': _jnp.float32, 'ff_dw': _jnp.float32, 'ff_dw_b': _jnp.float32, 'ff_w_down': _jnp.float32}
MOMENT_SCALE = {'norm_mix': 6.158208e+00, 'norm_ffn': 1.037720e+02, 'cm_w_in': 6.730678e-01, 'cm_b_in': 2.583241e+01, 'cm_dw': 2.737283e+00, 'cm_dw_b': 5.758478e+01, 'cm_ln_g': 6.358021e+01, 'cm_ln_b': 5.310364e+01, 'cm_w_out': 1.360960e+01, 'cm_b_out': 7.001819e+01, 'at_w_qkv': 1.115926e+00, 'at_q_norm': 1.473386e+00, 'at_k_norm': 1.472711e+00, 'at_w_out': 3.830319e+00, 'ff_w_up': 2.971839e+00, 'ff_dw': 1.520632e+01, 'ff_dw_b': 1.435237e+01, 'ff_w_down': 1.692659e+00}


def _to_microbatches(a, axis):
    t = _jnp.moveaxis(a, axis, 0)
    t = t.reshape((N_MICROBATCH, t.shape[0] // N_MICROBATCH) + t.shape[1:])
    return _jnp.moveaxis(t, 1, axis + 1)


def setup_inputs(seed: int = 0) -> dict:
    inp = _fwd_setup_inputs(seed)
    key = _jax.random.fold_in(_jax.random.key(seed), 7919)
    shape, _ = _output_shape()
    out = dict(inp)
    out["loss_target"] = _jax.random.normal(_jax.random.fold_in(key, 0), shape, _jnp.float32)
    for i, name in enumerate(TWIN_WEIGHTS):
        w = inp[name].astype(_jnp.float32)
        if MOMENT_SCALE is None:
            s = _jnp.sqrt(_jnp.mean(_jnp.square(w)) + 1e-30)
        else:
            s = MOMENT_SCALE[name]
        km, kv = _jax.random.split(_jax.random.fold_in(key, i + 1))
        out[name] = w
        out["m_" + name] = s * _jax.random.normal(km, w.shape, _jnp.float32)
        out["v_" + name] = (s * s) * _jax.random.uniform(kv, w.shape, _jnp.float32, 0.5, 1.5)
    if N_MICROBATCH > 1:
        for name, axis in PER_EXAMPLE_BATCH_AXIS.items():
            out[name] = _to_microbatches(out[name], axis)
    return {'x': out['x'], 'norm_mix': out['norm_mix'], 'norm_ffn': out['norm_ffn'], 'cm_w_in': out['cm_w_in'], 'cm_b_in': out['cm_b_in'], 'cm_dw': out['cm_dw'], 'cm_dw_b': out['cm_dw_b'], 'cm_ln_g': out['cm_ln_g'], 'cm_ln_b': out['cm_ln_b'], 'cm_w_out': out['cm_w_out'], 'cm_b_out': out['cm_b_out'], 'at_w_qkv': out['at_w_qkv'], 'at_q_norm': out['at_q_norm'], 'at_k_norm': out['at_k_norm'], 'at_w_out': out['at_w_out'], 'ff_w_up': out['ff_w_up'], 'ff_dw': out['ff_dw'], 'ff_dw_b': out['ff_dw_b'], 'ff_w_down': out['ff_w_down'], 'loss_target': out['loss_target'], 'm_norm_mix': out['m_norm_mix'], 'm_norm_ffn': out['m_norm_ffn'], 'm_cm_w_in': out['m_cm_w_in'], 'm_cm_b_in': out['m_cm_b_in'], 'm_cm_dw': out['m_cm_dw'], 'm_cm_dw_b': out['m_cm_dw_b'], 'm_cm_ln_g': out['m_cm_ln_g'], 'm_cm_ln_b': out['m_cm_ln_b'], 'm_cm_w_out': out['m_cm_w_out'], 'm_cm_b_out': out['m_cm_b_out'], 'm_at_w_qkv': out['m_at_w_qkv'], 'm_at_q_norm': out['m_at_q_norm'], 'm_at_k_norm': out['m_at_k_norm'], 'm_at_w_out': out['m_at_w_out'], 'm_ff_w_up': out['m_ff_w_up'], 'm_ff_dw': out['m_ff_dw'], 'm_ff_dw_b': out['m_ff_dw_b'], 'm_ff_w_down': out['m_ff_w_down'], 'v_norm_mix': out['v_norm_mix'], 'v_norm_ffn': out['v_norm_ffn'], 'v_cm_w_in': out['v_cm_w_in'], 'v_cm_b_in': out['v_cm_b_in'], 'v_cm_dw': out['v_cm_dw'], 'v_cm_dw_b': out['v_cm_dw_b'], 'v_cm_ln_g': out['v_cm_ln_g'], 'v_cm_ln_b': out['v_cm_ln_b'], 'v_cm_w_out': out['v_cm_w_out'], 'v_cm_b_out': out['v_cm_b_out'], 'v_at_w_qkv': out['v_at_w_qkv'], 'v_at_q_norm': out['v_at_q_norm'], 'v_at_k_norm': out['v_at_k_norm'], 'v_at_w_out': out['v_at_w_out'], 'v_ff_w_up': out['v_ff_w_up'], 'v_ff_dw': out['v_ff_dw'], 'v_ff_dw_b': out['v_ff_dw_b'], 'v_ff_w_down': out['v_ff_w_down']}


def _loss(weights, diff, rest, loss_target):
    with _jax.named_scope("forward"):
        args = {**rest, TWIN_DIFF_INPUT: diff, **{k: w.astype(_WEIGHT_DTYPES[k]) for k, w in weights.items()}}
        y = _forward(args)
    with _jax.named_scope("loss_head"):
        err = _jnp.square(y.astype(_jnp.float32) - loss_target)
        return 0.5 * _jnp.sum(_jnp.mean(err, axis=-1)) if err.ndim else 0.5 * err


def _adamw(w, g, m, v):
    m = ADAM_B1 * m + (1.0 - ADAM_B1) * g
    v = ADAM_B2 * v + (1.0 - ADAM_B2) * _jnp.square(g)
    m_hat = m / (1.0 - ADAM_B1 ** ADAM_STEP)
    v_hat = v / (1.0 - ADAM_B2 ** ADAM_STEP)
    delta = -ADAM_LR * (m_hat / (_jnp.sqrt(v_hat) + ADAM_EPS) + ADAM_WD * w)
    return delta, m, v


def reference(x, norm_mix, norm_ffn, cm_w_in, cm_b_in, cm_dw, cm_dw_b, cm_ln_g, cm_ln_b, cm_w_out, cm_b_out, at_w_qkv, at_q_norm, at_k_norm, at_w_out, ff_w_up, ff_dw, ff_dw_b, ff_w_down, loss_target, m_norm_mix, m_norm_ffn, m_cm_w_in, m_cm_b_in, m_cm_dw, m_cm_dw_b, m_cm_ln_g, m_cm_ln_b, m_cm_w_out, m_cm_b_out, m_at_w_qkv, m_at_q_norm, m_at_k_norm, m_at_w_out, m_ff_w_up, m_ff_dw, m_ff_dw_b, m_ff_w_down, v_norm_mix, v_norm_ffn, v_cm_w_in, v_cm_b_in, v_cm_dw, v_cm_dw_b, v_cm_ln_g, v_cm_ln_b, v_cm_w_out, v_cm_b_out, v_at_w_qkv, v_at_q_norm, v_at_k_norm, v_at_w_out, v_ff_w_up, v_ff_dw, v_ff_dw_b, v_ff_w_down):
    given = dict(x=x, norm_mix=norm_mix, norm_ffn=norm_ffn, cm_w_in=cm_w_in, cm_b_in=cm_b_in, cm_dw=cm_dw, cm_dw_b=cm_dw_b, cm_ln_g=cm_ln_g, cm_ln_b=cm_ln_b, cm_w_out=cm_w_out, cm_b_out=cm_b_out, at_w_qkv=at_w_qkv, at_q_norm=at_q_norm, at_k_norm=at_k_norm, at_w_out=at_w_out, ff_w_up=ff_w_up, ff_dw=ff_dw, ff_dw_b=ff_dw_b, ff_w_down=ff_w_down, loss_target=loss_target, m_norm_mix=m_norm_mix, m_norm_ffn=m_norm_ffn, m_cm_w_in=m_cm_w_in, m_cm_b_in=m_cm_b_in, m_cm_dw=m_cm_dw, m_cm_dw_b=m_cm_dw_b, m_cm_ln_g=m_cm_ln_g, m_cm_ln_b=m_cm_ln_b, m_cm_w_out=m_cm_w_out, m_cm_b_out=m_cm_b_out, m_at_w_qkv=m_at_w_qkv, m_at_q_norm=m_at_q_norm, m_at_k_norm=m_at_k_norm, m_at_w_out=m_at_w_out, m_ff_w_up=m_ff_w_up, m_ff_dw=m_ff_dw, m_ff_dw_b=m_ff_dw_b, m_ff_w_down=m_ff_w_down, v_norm_mix=v_norm_mix, v_norm_ffn=v_norm_ffn, v_cm_w_in=v_cm_w_in, v_cm_b_in=v_cm_b_in, v_cm_dw=v_cm_dw, v_cm_dw_b=v_cm_dw_b, v_cm_ln_g=v_cm_ln_g, v_cm_ln_b=v_cm_ln_b, v_cm_w_out=v_cm_w_out, v_cm_b_out=v_cm_b_out, v_at_w_qkv=v_at_w_qkv, v_at_q_norm=v_at_q_norm, v_at_k_norm=v_at_k_norm, v_at_w_out=v_at_w_out, v_ff_w_up=v_ff_w_up, v_ff_dw=v_ff_dw, v_ff_dw_b=v_ff_dw_b, v_ff_w_down=v_ff_w_down)
    weights = {n: given[n] for n in TWIN_WEIGHTS}
    shared = {n: given[n] for n in SHARED_INPUTS}
    per_example = {n: given[n] for n in ['x']}
    grad_fn = _jax.value_and_grad(_loss, argnums=(0, 1))

    def one_microbatch(ex, loss_target):
        ex = dict(ex)
        diff = ex.pop(TWIN_DIFF_INPUT)
        return grad_fn(weights, diff, {**shared, **ex}, loss_target)

    if N_MICROBATCH == 1:
        loss, (grad_w, grad_x) = one_microbatch(per_example, given["loss_target"])
    else:
        def body(carry, xs):
            loss_sum, grad_sum = carry
            l_k, (gw_k, gx_k) = one_microbatch(xs[0], xs[1])
            with _jax.named_scope("update"):
                return (loss_sum + l_k, _jax.tree.map(_jnp.add, grad_sum, gw_k)), gx_k

        init = (_jnp.zeros((), _jnp.float32), _jax.tree.map(_jnp.zeros_like, weights))
        (loss, grad_w), grad_x = _jax.lax.scan(body, init, (per_example, given["loss_target"]))
    with _jax.named_scope("update"):
        delta_w, new_m, new_v = {}, {}, {}
        for n in TWIN_WEIGHTS:
            delta_w[n], new_m[n], new_v[n] = _adamw(weights[n], grad_w[n], given["m_" + n], given["v_" + n])
    return (loss, grad_x, *[grad_w[n] for n in TWIN_WEIGHTS], *[delta_w[n] for n in TWIN_WEIGHTS],
            *[new_m[n] for n in TWIN_WEIGHTS], *[new_v[n] for n in TWIN_WEIGHTS])
```

```python
import functools

import jax
import jax.numpy as jnp
import numpy as np
from jax import lax
from jax.experimental import pallas as pl
from jax.experimental.pallas import tpu as pltpu

F32 = jnp.float32
MXU_DTYPE = jnp.bfloat16
EPS = 1e-6
NEG = -1e30
HEAD_DIM = 128
BLOCK = 128
DILATED_GROUPS = ((128, 1), (512, 4), (2048, 16))
ALIBI_MAX = 8.0
CONV_KERNEL = 31
CONV_HALO = 32
FFN_KERNEL = 3
FFN_HALO = 8
ADAM_LR, ADAM_B1, ADAM_B2, ADAM_EPS, ADAM_WD, ADAM_STEP = 0.001, 0.9, 0.999, 1e-08, 0.01, 10
V7X_VMEM_BYTES = 64 * 1024 * 1024
VMEM_LIMIT = V7X_VMEM_BYTES * 3 // 4
N_DEV = 8
MESH = pl.DeviceIdType.MESH


def _pick(n, target, align):
    if n <= target:
        return n
    best = None
    for t in range(align, target + 1, align):
        if n % t == 0:
            best = t
    assert best is not None, (n, target, align)
    return best


def _params(*sem):
    return pltpu.CompilerParams(dimension_semantics=sem, vmem_limit_bytes=VMEM_LIMIT)


def _sigmoid(x):
    return 1.0 / (1.0 + jnp.exp(-x))


_DIMS = {"nn": ((1,), (0,)), "nt": ((1,), (1,)), "tn": ((0,), (0,))}


def _mm(a, b, *, mode, out_dtype, name, tm=1024, tn=1024, tk=None, bias=None, residual=None, b_off=0, b_len=None):
    if mode == "tn":
        K, M = a.shape
    else:
        M, K = a.shape
    if mode == "nt":
        N = b.shape[0]
    else:
        N = b.shape[1]
    if b_len is not None:
        assert mode == "nn" and K == b_len
    tm = _pick(M, tm, 128 if mode == "tn" else 16)
    tn = _pick(N, tn, 128)
    tk = K if tk is None else _pick(K, tk, 128 if mode != "tn" else 16)
    nk = K // tk
    assert b_off % tk == 0
    kb0 = b_off // tk
    if mode == "tn":
        a_spec = pl.BlockSpec((tk, tm), lambda i, j, k: (k, i))
    else:
        a_spec = pl.BlockSpec((tm, tk), lambda i, j, k: (i, k))
    if mode == "nt":
        b_spec = pl.BlockSpec((tn, tk), lambda i, j, k: (j, k))
    else:
        b_spec = pl.BlockSpec((tk, tn), lambda i, j, k: (k + kb0, j))
    in_specs = [a_spec, b_spec]
    args = [a, b]
    if bias is not None:
        in_specs.append(pl.BlockSpec((1, tn), lambda i, j, k: (0, j)))
        args.append(bias)
    if residual is not None:
        in_specs.append(pl.BlockSpec((tm, tn), lambda i, j, k: (i, j)))
        args.append(residual)
    has_bias, has_res = bias is not None, residual is not None

    def body(*refs):
        a_ref, b_ref = refs[0], refs[1]
        pos = 2
        bias_ref = res_ref = None
        if has_bias:
            bias_ref = refs[pos]
            pos += 1
        if has_res:
            res_ref = refs[pos]
            pos += 1
        o_ref = refs[pos]
        acc_ref = refs[pos + 1] if nk > 1 else None

        def finish(acc):
            if has_bias:
                acc = acc + bias_ref[...]
            if has_res:
                acc = acc + res_ref[...]
            o_ref[...] = acc.astype(o_ref.dtype)

        part = lax.dot_general(a_ref[...].astype(MXU_DTYPE), b_ref[...].astype(MXU_DTYPE), (_DIMS[mode], ((), ())),
                               preferred_element_type=F32)
        if nk == 1:
            finish(part)
        else:
            k = pl.program_id(2)

            @pl.when(k == 0)
            def _():
                acc_ref[...] = part

            @pl.when(jnp.logical_and(k > 0, k < nk - 1))
            def _():
                acc_ref[...] += part

            @pl.when(k == nk - 1)
            def _():
                finish(acc_ref[...] + part)

    return pl.pallas_call(
        body, name=name, grid=(M // tm, N // tn, nk), in_specs=in_specs,
        out_specs=pl.BlockSpec((tm, tn), lambda i, j, k: (i, j)),
        out_shape=jax.ShapeDtypeStruct((M, N), out_dtype),
        scratch_shapes=[pltpu.VMEM((tm, tn), F32)] if nk > 1 else [],
        compiler_params=_params("parallel", "parallel", "arbitrary"),
    )(*args)


def _rms_fwd(x, g, *, name):
    S, D = x.shape
    ts = _pick(S, 512, 16)

    def body(x_ref, g_ref, h_ref):
        xv = x_ref[...]
        r = lax.rsqrt(jnp.mean(xv * xv, axis=-1, keepdims=True) + EPS)
        h_ref[...] = (xv * r * g_ref[...]).astype(h_ref.dtype)

    return pl.pallas_call(
        body, name=name, grid=(S // ts,),
        in_specs=[pl.BlockSpec((ts, D), lambda i: (i, 0)), pl.BlockSpec((1, D), lambda i: (0, 0))],
        out_specs=pl.BlockSpec((ts, D), lambda i: (i, 0)),
        out_shape=jax.ShapeDtypeStruct((S, D), MXU_DTYPE),
        compiler_params=_params("parallel"),
    )(x, g)


def _rms_bwd(x, g, dh, dres, *, name):
    S, D = x.shape
    ts = _pick(S, 512, 16)

    def body(x_ref, g_ref, dh_ref, dres_ref, dx_ref, dxb_ref, dg_ref, cs_ref):
        i = pl.program_id(0)
        xv = x_ref[...]
        r = lax.rsqrt(jnp.mean(xv * xv, axis=-1, keepdims=True) + EPS)
        xh = xv * r
        dhv = dh_ref[...].astype(F32)
        gy = dhv * g_ref[...]
        dx = r * (gy - xh * jnp.mean(gy * xh, axis=-1, keepdims=True)) + dres_ref[...]
        dx_ref[...] = dx
        dxb_ref[...] = dx.astype(dxb_ref.dtype)
        dg = jnp.sum(dhv * xh, axis=0, keepdims=True)
        cs = jnp.sum(dx, axis=0, keepdims=True)

        @pl.when(i == 0)
        def _():
            dg_ref[...] = dg
            cs_ref[...] = cs

        @pl.when(i > 0)
        def _():
            dg_ref[...] += dg
            cs_ref[...] += cs

    row = pl.BlockSpec((ts, D), lambda i: (i, 0))
    vec = pl.BlockSpec((1, D), lambda i: (0, 0))
    return pl.pallas_call(
        body, name=name, grid=(S // ts,),
        in_specs=[row, vec, row, row],
        out_specs=[row, row, vec, vec],
        out_shape=[jax.ShapeDtypeStruct((S, D), F32), jax.ShapeDtypeStruct((S, D), MXU_DTYPE),
                   jax.ShapeDtypeStruct((1, D), F32), jax.ShapeDtypeStruct((1, D), F32)],
        compiler_params=_params("arbitrary"),
    )(x, g, dh, dres)


def _cm_fwd(u, dw, dw_b, ln_g, ln_b, *, name):
    S, D2 = u.shape
    D = D2 // 2
    ts = _pick(S, 256, CONV_HALO)
    hb = ts // CONV_HALO

    def body(u_ref, up_ref, dw_ref, dwb_ref, g_ref, b_ref, c_ref, s_ref, ext_ref):
        i = pl.program_id(0)
        prev = up_ref[:, :D] * _sigmoid(up_ref[:, D:])
        ext_ref[0:CONV_HALO, :] = jnp.where(i > 0, prev, 0.0)
        ext_ref[CONV_HALO:CONV_HALO + ts, :] = u_ref[:, :D] * _sigmoid(u_ref[:, D:])
        for cc in range(D // 128):
            sl = slice(cc * 128, (cc + 1) * 128)
            acc = jnp.zeros((ts, 128), F32) + dwb_ref[:, sl]
            for k in range(CONV_KERNEL):
                acc = acc + dw_ref[k:k + 1, sl] * ext_ref[pl.ds(CONV_HALO - (CONV_KERNEL - 1) + k, ts), sl]
            c_ref[:, sl] = acc
        c = c_ref[...]
        mu = jnp.mean(c, axis=-1, keepdims=True)
        xc = c - mu
        rstd = lax.rsqrt(jnp.mean(xc * xc, axis=-1, keepdims=True) + EPS)
        y = xc * rstd * g_ref[...] + b_ref[...]
        s_ref[...] = (y * _sigmoid(y)).astype(s_ref.dtype)

    vec = pl.BlockSpec((1, D), lambda i: (0, 0))
    return pl.pallas_call(
        body, name=name, grid=(S // ts,),
        in_specs=[pl.BlockSpec((ts, D2), lambda i: (i, 0)),
                  pl.BlockSpec((CONV_HALO, D2), lambda i: (jnp.maximum(i * hb - 1, 0), 0)),
                  pl.BlockSpec((CONV_KERNEL, D), lambda i: (0, 0)), vec, vec, vec],
        out_specs=[pl.BlockSpec((ts, D), lambda i: (i, 0)), pl.BlockSpec((ts, D), lambda i: (i, 0))],
        out_shape=[jax.ShapeDtypeStruct((S, D), F32), jax.ShapeDtypeStruct((S, D), MXU_DTYPE)],
        scratch_shapes=[pltpu.VMEM((ts + CONV_HALO, D), F32)],
        compiler_params=_params("parallel"),
    )(u, u, dw, dw_b, ln_g, ln_b)


def _cm_ln_bwd(c, ds, ln_g, ln_b, *, name):
    S, D = c.shape
    ts = _pick(S, 512, 16)

    def body(c_ref, ds_ref, g_ref, b_ref, dc_ref, dg_ref, db_ref):
        i = pl.program_id(0)
        cv = c_ref[...]
        mu = jnp.mean(cv, axis=-1, keepdims=True)
        xc = cv - mu
        rstd = lax.rsqrt(jnp.mean(xc * xc, axis=-1, keepdims=True) + EPS)
        xh = xc * rstd
        y = xh * g_ref[...] + b_ref[...]
        sg = _sigmoid(y)
        dy = ds_ref[...].astype(F32) * (sg * (1.0 + y * (1.0 - sg)))
        gy = dy * g_ref[...]
        dc_ref[...] = rstd * (gy - jnp.mean(gy, axis=-1, keepdims=True) - xh * jnp.mean(gy * xh, axis=-1, keepdims=True))
        dg = jnp.sum(dy * xh, axis=0, keepdims=True)
        db = jnp.sum(dy, axis=0, keepdims=True)

        @pl.when(i == 0)
        def _():
            dg_ref[...] = dg
            db_ref[...] = db

        @pl.when(i > 0)
        def _():
            dg_ref[...] += dg
            db_ref[...] += db

    row = pl.BlockSpec((ts, D), lambda i: (i, 0))
    vec = pl.BlockSpec((1, D), lambda i: (0, 0))
    return pl.pallas_call(
        body, name=name, grid=(S // ts,), in_specs=[row, row, vec, vec], out_specs=[row, vec, vec],
        out_shape=[jax.ShapeDtypeStruct((S, D), F32), jax.ShapeDtypeStruct((1, D), F32), jax.ShapeDtypeStruct((1, D), F32)],
        compiler_params=_params("arbitrary"),
    )(c, ds, ln_g, ln_b)


def _cm_conv_bwd(dc, u, dw, *, name):
    S, D2 = u.shape
    D = D2 // 2
    ts = _pick(S, 256, CONV_HALO)
    hb = ts // CONV_HALO
    n_t = S // ts
    last_h = S // CONV_HALO - 1

    def body(dc_ref, dcn_ref, u_ref, up_ref, dw_ref, du_ref, ddw_ref, ddwb_ref, dbin_ref, dce_ref, ge_ref):
        i = pl.program_id(0)
        dce_ref[0:ts, :] = dc_ref[...]
        dce_ref[ts:ts + CONV_HALO, :] = jnp.where(i < n_t - 1, dcn_ref[...], 0.0)
        prev = up_ref[:, :D] * _sigmoid(up_ref[:, D:])
        ge_ref[0:CONV_HALO, :] = jnp.where(i > 0, prev, 0.0)
        a = u_ref[:, :D]
        sg = _sigmoid(u_ref[:, D:])
        ge_ref[CONV_HALO:CONV_HALO + ts, :] = a * sg

        @pl.when(i == 0)
        def _():
            ddw_ref[...] = jnp.zeros_like(ddw_ref)
            ddwb_ref[...] = jnp.zeros_like(ddwb_ref)
            dbin_ref[...] = jnp.zeros_like(dbin_ref)

        for cc in range(D // 128):
            sl = slice(cc * 128, (cc + 1) * 128)
            dcc = dce_ref[0:ts, sl]
            acc = jnp.zeros((ts, 128), F32)
            for k in range(CONV_KERNEL):
                acc = acc + dw_ref[k:k + 1, sl] * dce_ref[pl.ds(CONV_KERNEL - 1 - k, ts), sl]
                gk = ge_ref[pl.ds(CONV_HALO - (CONV_KERNEL - 1) + k, ts), sl]
                ddw_ref[k:k + 1, sl] += jnp.sum(dcc * gk, axis=0, keepdims=True)
            ddwb_ref[:, sl] += jnp.sum(dcc, axis=0, keepdims=True)
            sgc = sg[:, sl]
            da = acc * sgc
            dg = acc * a[:, sl] * sgc * (1.0 - sgc)
            du_ref[:, sl] = da.astype(du_ref.dtype)
            du_ref[:, D + cc * 128:D + (cc + 1) * 128] = dg.astype(du_ref.dtype)
            dbin_ref[:, sl] += jnp.sum(da, axis=0, keepdims=True)
            dbin_ref[:, D + cc * 128:D + (cc + 1) * 128] += jnp.sum(dg, axis=0, keepdims=True)

    return pl.pallas_call(
        body, name=name, grid=(n_t,),
        in_specs=[pl.BlockSpec((ts, D), lambda i: (i, 0)),
                  pl.BlockSpec((CONV_HALO, D), lambda i: (jnp.minimum((i + 1) * hb, last_h), 0)),
                  pl.BlockSpec((ts, D2), lambda i: (i, 0)),
                  pl.BlockSpec((CONV_HALO, D2), lambda i: (jnp.maximum(i * hb - 1, 0), 0)),
                  pl.BlockSpec((CONV_KERNEL, D), lambda i: (0, 0))],
        out_specs=[pl.BlockSpec((ts, D2), lambda i: (i, 0)), pl.BlockSpec((CONV_HALO, D), lambda i: (0, 0)),
                   pl.BlockSpec((1, D), lambda i: (0, 0)), pl.BlockSpec((1, D2), lambda i: (0, 0))],
        out_shape=[jax.ShapeDtypeStruct((S, D2), MXU_DTYPE), jax.ShapeDtypeStruct((CONV_HALO, D), F32),
                   jax.ShapeDtypeStruct((1, D), F32), jax.ShapeDtypeStruct((1, D2), F32)],
        scratch_shapes=[pltpu.VMEM((ts + CONV_HALO, D), F32), pltpu.VMEM((ts + CONV_HALO, D), F32)],
        compiler_params=_params("arbitrary"),
    )(dc, dc, u, u, dw)


def _ffn_cols(F2):
    return _pick(F2, 1024, 256)


def _ffn_act_fwd(up, dw, dw_b, *, name):
    S, F2 = up.shape
    ts = _pick(S, 512, 16)
    tc = _ffn_cols(F2)
    hb = ts // FFN_HALO

    def body(u_ref, up_ref, w_ref, b_ref, a_ref, ext_ref):
        i = pl.program_id(1)
        ext_ref[0:FFN_HALO, :] = jnp.where(i > 0, up_ref[...].astype(F32), 0.0)
        ext_ref[FFN_HALO:FFN_HALO + ts, :] = u_ref[...].astype(F32)
        for q in range(tc // 256):
            cvs = []
            for half in range(2):
                sl = slice(q * 256 + half * 128, q * 256 + half * 128 + 128)
                cv = b_ref[:, sl] + w_ref[2:3, sl] * ext_ref[pl.ds(FFN_HALO, ts), sl]
                cv = cv + w_ref[1:2, sl] * ext_ref[pl.ds(FFN_HALO - 1, ts), sl]
                cv = cv + w_ref[0:1, sl] * ext_ref[pl.ds(FFN_HALO - 2, ts), sl]
                cvs.append(cv)
            gt, vl = cvs
            a_ref[:, q * 128:(q + 1) * 128] = (gt * _sigmoid(gt) * vl).astype(a_ref.dtype)

    return pl.pallas_call(
        body, name=name, grid=(F2 // tc, S // ts),
        in_specs=[pl.BlockSpec((ts, tc), lambda j, i: (i, j)),
                  pl.BlockSpec((FFN_HALO, tc), lambda j, i: (jnp.maximum(i * hb - 1, 0), j)),
                  pl.BlockSpec((FFN_KERNEL, tc), lambda j, i: (0, j)),
                  pl.BlockSpec((1, tc), lambda j, i: (0, j))],
        out_specs=pl.BlockSpec((ts, tc // 2), lambda j, i: (i, j)),
        out_shape=jax.ShapeDtypeStruct((S, F2 // 2), MXU_DTYPE),
        scratch_shapes=[pltpu.VMEM((ts + FFN_HALO, tc), F32)],
        compiler_params=_params("parallel", "parallel"),
    )(up, up, dw, dw_b)


def _ffn_act_bwd(up, dact, dw, dw_b, *, name):
    S, F2 = up.shape
    ts = _pick(S, 512, 16)
    tc = _ffn_cols(F2)
    hb = ts // FFN_HALO
    n_t = S // ts
    last_h = S // FFN_HALO - 1
    E = ts + FFN_HALO

    def body(u_ref, up_ref, un_ref, da_ref, dan_ref, w_ref, b_ref, dup_ref, ddw_ref, ddb_ref, ue_ref, dae_ref, dcv_ref):
        i = pl.program_id(1)
        ue_ref[0:FFN_HALO, :] = jnp.where(i > 0, up_ref[...].astype(F32), 0.0)
        ue_ref[FFN_HALO:FFN_HALO + ts, :] = u_ref[...].astype(F32)
        ue_ref[FFN_HALO + ts:FFN_HALO + ts + FFN_HALO, :] = jnp.where(i < n_t - 1, un_ref[...].astype(F32), 0.0)
        dae_ref[0:ts, :] = da_ref[...].astype(F32)
        dae_ref[ts:E, :] = jnp.where(i < n_t - 1, dan_ref[...].astype(F32), 0.0)

        @pl.when(i == 0)
        def _():
            ddw_ref[...] = jnp.zeros_like(ddw_ref)
            ddb_ref[...] = jnp.zeros_like(ddb_ref)

        for q in range(tc // 256):
            sls = [slice(q * 256 + half * 128, q * 256 + half * 128 + 128) for half in range(2)]
            cvs = []
            for sl in sls:
                cv = b_ref[:, sl] + w_ref[2:3, sl] * ue_ref[pl.ds(FFN_HALO, E), sl]
                cv = cv + w_ref[1:2, sl] * ue_ref[pl.ds(FFN_HALO - 1, E), sl]
                cv = cv + w_ref[0:1, sl] * ue_ref[pl.ds(FFN_HALO - 2, E), sl]
                cvs.append(cv)
            gt, vl = cvs
            sg = _sigmoid(gt)
            dae = dae_ref[:, q * 128:(q + 1) * 128]
            dcv_ref[:, sls[0]] = dae * vl * (sg * (1.0 + gt * (1.0 - sg)))
            dcv_ref[:, sls[1]] = dae * (gt * sg)
            for sl in sls:
                d0 = dcv_ref[pl.ds(0, ts), sl]
                dup = w_ref[2:3, sl] * d0 + w_ref[1:2, sl] * dcv_ref[pl.ds(1, ts), sl] + w_ref[0:1, sl] * dcv_ref[pl.ds(2, ts), sl]
                dup_ref[:, sl] = dup.astype(dup_ref.dtype)
                for k in range(FFN_KERNEL):
                    ddw_ref[k:k + 1, sl] += jnp.sum(d0 * ue_ref[pl.ds(FFN_HALO - 2 + k, ts), sl], axis=0, keepdims=True)
                ddb_ref[:, sl] += jnp.sum(d0, axis=0, keepdims=True)

    return pl.pallas_call(
        body, name=name, grid=(F2 // tc, n_t),
        in_specs=[pl.BlockSpec((ts, tc), lambda j, i: (i, j)),
                  pl.BlockSpec((FFN_HALO, tc), lambda j, i: (jnp.maximum(i * hb - 1, 0), j)),
                  pl.BlockSpec((FFN_HALO, tc), lambda j, i: (jnp.minimum((i + 1) * hb, last_h), j)),
                  pl.BlockSpec((ts, tc // 2), lambda j, i: (i, j)),
                  pl.BlockSpec((FFN_HALO, tc // 2), lambda j, i: (jnp.minimum((i + 1) * hb, last_h), j)),
                  pl.BlockSpec((FFN_KERNEL, tc), lambda j, i: (0, j)),
                  pl.BlockSpec((1, tc), lambda j, i: (0, j))],
        out_specs=[pl.BlockSpec((ts, tc), lambda j, i: (i, j)), pl.BlockSpec((FFN_HALO, tc), lambda j, i: (0, j)),
                   pl.BlockSpec((1, tc), lambda j, i: (0, j))],
        out_shape=[jax.ShapeDtypeStruct((S, F2), MXU_DTYPE), jax.ShapeDtypeStruct((FFN_HALO, F2), F32),
                   jax.ShapeDtypeStruct((1, F2), F32)],
        scratch_shapes=[pltpu.VMEM((ts + 2 * FFN_HALO, tc), F32), pltpu.VMEM((E, tc // 2), F32), pltpu.VMEM((E, tc), F32)],
        compiler_params=_params("parallel", "arbitrary"),
    )(up, up, up, dact, dact, dw, dw_b)


def _slopes(n_heads_total):
    return np.asarray(2.0 ** (-ALIBI_MAX * (np.arange(n_heads_total, dtype=np.float32) + 1.0) / n_heads_total), np.float32)


def _qk_norm(x, gain):
    r = lax.rsqrt(jnp.mean(x * x, axis=-1, keepdims=True) + EPS)
    xh = x * r
    return xh * gain, xh, r


def _band(b, dil):
    qi = lax.broadcasted_iota(jnp.int32, (BLOCK, 2 * BLOCK), 0)
    ki = lax.broadcasted_iota(jnp.int32, (BLOCK, 2 * BLOCK), 1)
    delta = qi + BLOCK - ki
    valid = (delta >= 0) & (delta <= BLOCK) & ((ki >= BLOCK) | (b > 0))
    return valid, (delta * dil).astype(F32)


def _attn_fwd(qkv, qg, kg, *, grp, name):
    S, W = qkv.shape
    D = W // 9
    H = D // HEAD_DIM
    dil = DILATED_GROUPS[grp][1]
    L = S // dil
    nb = L // BLOCK
    slopes = _slopes(3 * H)[grp * H:(grp + 1) * H]
    scale = HEAD_DIM ** -0.5
    qkv_v = qkv.reshape(L, dil * W)

    def body(q_ref, kp_ref, kc_ref, vp_ref, vc_ref, qg_ref, kg_ref, o_ref, l_ref):
        b = pl.program_id(1)
        valid, dist = _band(b, dil)
        for h in range(H):
            hs = slice(h * HEAD_DIM, (h + 1) * HEAD_DIM)
            qn = _qk_norm(q_ref[:, hs].astype(F32), qg_ref[h:h + 1, :])[0].astype(MXU_DTYPE)
            kp = _qk_norm(kp_ref[:, hs].astype(F32), kg_ref[h:h + 1, :])[0].astype(MXU_DTYPE)
            kc = _qk_norm(kc_ref[:, hs].astype(F32), kg_ref[h:h + 1, :])[0].astype(MXU_DTYPE)
            k2 = jnp.concatenate([kp, kc], axis=0)
            v2 = jnp.concatenate([vp_ref[:, hs], vc_ref[:, hs]], axis=0).astype(MXU_DTYPE)
            s = lax.dot_general(qn, k2, (((1,), (1,)), ((), ())), preferred_element_type=F32) * scale - float(slopes[h]) * dist
            s = jnp.where(valid, s, NEG)
            m = jnp.max(s, axis=-1, keepdims=True)
            p = jnp.exp(s - m)
            den = jnp.sum(p, axis=-1, keepdims=True)
            o = jnp.dot(p.astype(MXU_DTYPE), v2, preferred_element_type=F32) / den
            o_ref[:, hs] = o
            l_ref[:, hs] = jnp.broadcast_to(m + jnp.log(den), (BLOCK, HEAD_DIM))

    def col(j):
        return lambda r, b: (b, r * 9 + grp * 3 + j)

    def colp(j):
        return lambda r, b: (jnp.maximum(b - 1, 0), r * 9 + grp * 3 + j)

    blk = (BLOCK, D)
    gain = pl.BlockSpec((H, HEAD_DIM), lambda r, b: (0, 0))
    o, lse = pl.pallas_call(
        body, name=name, grid=(dil, nb),
        in_specs=[pl.BlockSpec(blk, col(0)), pl.BlockSpec(blk, colp(1)), pl.BlockSpec(blk, col(1)),
                  pl.BlockSpec(blk, colp(2)), pl.BlockSpec(blk, col(2)), gain, gain],
        out_specs=[pl.BlockSpec(blk, lambda r, b: (b, r)), pl.BlockSpec(blk, lambda r, b: (b, r))],
        out_shape=[jax.ShapeDtypeStruct((L, dil * D), F32), jax.ShapeDtypeStruct((L, dil * D), F32)],
        compiler_params=_params("parallel", "parallel"),
    )(qkv_v, qkv_v, qkv_v, qkv_v, qkv_v, qg, kg)
    return o.reshape(S, D), lse.reshape(S, D)


def _attn_merge(os_, ls_, *, name):
    S, D = os_[0].shape
    ts = _pick(S, 512, 16)

    def body(o0, o1, o2, l0, l1, l2, out_ref, outb_ref, lt_ref):
        a, b_, c = l0[...], l1[...], l2[...]
        m = jnp.maximum(jnp.maximum(a, b_), c)
        e0, e1, e2 = jnp.exp(a - m), jnp.exp(b_ - m), jnp.exp(c - m)
        den = e0 + e1 + e2
        out = (e0 * o0[...] + e1 * o1[...] + e2 * o2[...]) / den
        out_ref[...] = out
        outb_ref[...] = out.astype(outb_ref.dtype)
        lt_ref[...] = m + jnp.log(den)

    row = pl.BlockSpec((ts, D), lambda i: (i, 0))
    return pl.pallas_call(
        body, name=name, grid=(S // ts,), in_specs=[row] * 6, out_specs=[row] * 3,
        out_shape=[jax.ShapeDtypeStruct((S, D), F32), jax.ShapeDtypeStruct((S, D), MXU_DTYPE), jax.ShapeDtypeStruct((S, D), F32)],
        compiler_params=_params("parallel"),
    )(*os_, *ls_)


def _attn_delta(do, out, *, name):
    S, D = out.shape
    ts = _pick(S, 512, 16)

    def body(do_ref, o_ref, d_ref):
        prod = do_ref[...].astype(F32) * o_ref[...]
        for h in range(D // HEAD_DIM):
            hs = slice(h * HEAD_DIM, (h + 1) * HEAD_DIM)
            d_ref[:, hs] = jnp.broadcast_to(jnp.sum(prod[:, hs], axis=-1, keepdims=True), (ts, HEAD_DIM))

    row = pl.BlockSpec((ts, D), lambda i: (i, 0))
    return pl.pallas_call(
        body, name=name, grid=(S // ts,), in_specs=[row, row], out_specs=row,
        out_shape=jax.ShapeDtypeStruct((S, D), F32), compiler_params=_params("parallel"),
    )(do, out)


def _attn_bwd(qkv, do, lse, delta, qg, kg, *, grp, name):
    S, W = qkv.shape
    D = W // 9
    H = D // HEAD_DIM
    dil = DILATED_GROUPS[grp][1]
    L = S // dil
    nb = L // BLOCK
    slopes = _slopes(3 * H)[grp * H:(grp + 1) * H]
    scale = HEAD_DIM ** -0.5
    qkv_v = qkv.reshape(L, dil * W)
    do_v, lse_v, delta_v = do.reshape(L, dil * D), lse.reshape(L, dil * D), delta.reshape(L, dil * D)

    def body(q_ref, qp_ref, kp_ref, kc_ref, vp_ref, vc_ref, do_ref, l_ref, dl_ref, qg_ref, kg_ref,
             out_ref, dqg_ref, dkg_ref, cq_ref, ck_ref, cv_ref, nq_ref, nk_ref, nv_ref, pk_ref, pv_ref):
        r = pl.program_id(0)
        b = pl.program_id(1)

        @pl.when(jnp.logical_and(r == 0, b == 0))
        def _():
            dqg_ref[...] = jnp.zeros_like(dqg_ref)
            dkg_ref[...] = jnp.zeros_like(dkg_ref)

        @pl.when(b < nb)
        def _():
            valid, dist = _band(b, dil)
            for h in range(H):
                hs = slice(h * HEAD_DIM, (h + 1) * HEAD_DIM)
                qn = _qk_norm(q_ref[:, hs].astype(F32), qg_ref[h:h + 1, :])[0].astype(MXU_DTYPE)
                kp = _qk_norm(kp_ref[:, hs].astype(F32), kg_ref[h:h + 1, :])[0].astype(MXU_DTYPE)
                kc = _qk_norm(kc_ref[:, hs].astype(F32), kg_ref[h:h + 1, :])[0].astype(MXU_DTYPE)
                k2 = jnp.concatenate([kp, kc], axis=0)
                v2 = jnp.concatenate([vp_ref[:, hs], vc_ref[:, hs]], axis=0).astype(MXU_DTYPE)
                doh = do_ref[:, hs].astype(MXU_DTYPE)
                s = lax.dot_general(qn, k2, (((1,), (1,)), ((), ())), preferred_element_type=F32) * scale - float(slopes[h]) * dist
                s = jnp.where(valid, s, NEG)
                p = jnp.exp(s - l_ref[:, hs][:, 0:1])
                dp = lax.dot_general(doh, v2, (((1,), (1,)), ((), ())), preferred_element_type=F32)
                dsc = (p * (dp - dl_ref[:, hs][:, 0:1]) * scale).astype(MXU_DTYPE)
                nq_ref[:, hs] = jnp.dot(dsc, k2, preferred_element_type=F32)
                dk2 = lax.dot_general(dsc, qn, (((0,), (0,)), ((), ())), preferred_element_type=F32)
                dv2 = lax.dot_general(p.astype(MXU_DTYPE), doh, (((0,), (0,)), ((), ())), preferred_element_type=F32)
                pk_ref[:, hs] = dk2[0:BLOCK]
                nk_ref[:, hs] = dk2[BLOCK:2 * BLOCK]
                pv_ref[:, hs] = dv2[0:BLOCK]
                nv_ref[:, hs] = dv2[BLOCK:2 * BLOCK]

        @pl.when(b == nb)
        def _():
            pk_ref[...] = jnp.zeros_like(pk_ref)
            pv_ref[...] = jnp.zeros_like(pv_ref)

        @pl.when(b > 0)
        def _():
            for h in range(H):
                hs = slice(h * HEAD_DIM, (h + 1) * HEAD_DIM)
                for j, (raw_ref, gain_ref, dgain_ref) in enumerate(((qp_ref, qg_ref, dqg_ref), (kp_ref, kg_ref, dkg_ref))):
                    dy = cq_ref[:, hs] if j == 0 else ck_ref[:, hs] + pk_ref[:, hs]
                    gain = gain_ref[h:h + 1, :]
                    _, xh, rr = _qk_norm(raw_ref[:, hs].astype(F32), gain)
                    gy = dy * gain
                    dx = rr * (gy - xh * jnp.mean(gy * xh, axis=-1, keepdims=True))
                    out_ref[:, j * D + h * HEAD_DIM:j * D + (h + 1) * HEAD_DIM] = dx.astype(out_ref.dtype)
                    dgain_ref[h:h + 1, :] += jnp.sum(dy * xh, axis=0, keepdims=True)
                out_ref[:, 2 * D + h * HEAD_DIM:2 * D + (h + 1) * HEAD_DIM] = (cv_ref[:, hs] + pv_ref[:, hs]).astype(out_ref.dtype)

        @pl.when(b < nb)
        def _():
            cq_ref[...] = nq_ref[...]
            ck_ref[...] = nk_ref[...]
            cv_ref[...] = nv_ref[...]

    def cur(j):
        return lambda r, b: (jnp.minimum(b, nb - 1), r * 9 + grp * 3 + j)

    def prv(j):
        return lambda r, b: (jnp.clip(b - 1, 0, nb - 1), r * 9 + grp * 3 + j)

    blk = (BLOCK, D)
    rowc = pl.BlockSpec(blk, lambda r, b: (jnp.minimum(b, nb - 1), r))
    gain = pl.BlockSpec((H, HEAD_DIM), lambda r, b: (0, 0))
    dqkv, dqg, dkg = pl.pallas_call(
        body, name=name, grid=(dil, nb + 1),
        in_specs=[pl.BlockSpec(blk, cur(0)), pl.BlockSpec(blk, prv(0)), pl.BlockSpec(blk, prv(1)), pl.BlockSpec(blk, cur(1)),
                  pl.BlockSpec(blk, prv(2)), pl.BlockSpec(blk, cur(2)), rowc, rowc, rowc, gain, gain],
        out_specs=[pl.BlockSpec((BLOCK, 3 * D), lambda r, b: (jnp.maximum(b - 1, 0), r)), gain, gain],
        out_shape=[jax.ShapeDtypeStruct((L, dil * 3 * D), MXU_DTYPE), jax.ShapeDtypeStruct((H, HEAD_DIM), F32),
                   jax.ShapeDtypeStruct((H, HEAD_DIM), F32)],
        scratch_shapes=[pltpu.VMEM(blk, F32)] * 8,
        compiler_params=_params("arbitrary", "arbitrary"),
    )(qkv_v, qkv_v, qkv_v, qkv_v, qkv_v, qkv_v, do_v, lse_v, delta_v, qg, kg)
    return dqkv.reshape(S, 3 * D), dqg, dkg


def _loss_head(y, target, *, name):
    S, D = y.shape
    ts = _pick(S, 512, 16)

    def body(y_ref, t_ref, dy_ref, dyb_ref, l_ref, acc_ref):
        i = pl.program_id(0)
        e = y_ref[...] - t_ref[...]
        dy = e * (1.0 / D)
        dy_ref[...] = dy
        dyb_ref[...] = dy.astype(dyb_ref.dtype)
        part = jnp.sum(e * e, axis=0, keepdims=True)

        @pl.when(i == 0)
        def _():
            acc_ref[...] = part

        @pl.when(i > 0)
        def _():
            acc_ref[...] += part

        @pl.when(i == pl.num_programs(0) - 1)
        def _():
            l_ref[...] = jnp.broadcast_to(jnp.sum(acc_ref[...], axis=-1, keepdims=True) * (0.5 / D), l_ref.shape)

    row = pl.BlockSpec((ts, D), lambda i: (i, 0))
    return pl.pallas_call(
        body, name=name, grid=(S // ts,), in_specs=[row, row],
        out_specs=[row, row, pl.BlockSpec((8, 128), lambda i: (0, 0))],
        out_shape=[jax.ShapeDtypeStruct((S, D), F32), jax.ShapeDtypeStruct((S, D), MXU_DTYPE), jax.ShapeDtypeStruct((8, 128), F32)],
        scratch_shapes=[pltpu.VMEM((1, D), F32)],
        compiler_params=_params("arbitrary"),
    )(y, target)


def _adamw(w, m, v, g_parts, *, name):
    R, C = w.shape
    P = g_parts.shape[0]
    tr = _pick(R, 256, 8)
    c1 = 1.0 - ADAM_B1 ** ADAM_STEP
    c2 = 1.0 - ADAM_B2 ** ADAM_STEP

    def body(w_ref, m_ref, v_ref, gp_ref, g_ref, d_ref, nm_ref, nv_ref):
        g = gp_ref[0]
        for k in range(1, P):
            g = g + gp_ref[k]
        mm = ADAM_B1 * m_ref[...] + (1.0 - ADAM_B1) * g
        vv = ADAM_B2 * v_ref[...] + (1.0 - ADAM_B2) * (g * g)
        m_hat = mm / c1
        v_hat = vv / c2
        g_ref[...] = g
        d_ref[...] = -ADAM_LR * (m_hat / (jnp.sqrt(v_hat) + ADAM_EPS) + ADAM_WD * w_ref[...])
        nm_ref[...] = mm
        nv_ref[...] = vv

    row = pl.BlockSpec((tr, C), lambda i: (i, 0))
    return pl.pallas_call(
        body, name=name, grid=(R // tr,),
        in_specs=[row, row, row, pl.BlockSpec((P, tr, C), lambda i: (0, i, 0))],
        out_specs=[row] * 4, out_shape=[jax.ShapeDtypeStruct((R, C), F32)] * 4,
        compiler_params=_params("parallel"),
    )(w, m, v, g_parts)


def _add(a, b, *, name):
    P, R, C = a.shape
    tr = _pick(R, 512, 8)
    blk = pl.BlockSpec((None, tr, C), lambda p, i: (p, i, 0))

    def body(a_ref, b_ref, o_ref):
        o_ref[...] = a_ref[...] + b_ref[...]

    return pl.pallas_call(
        body, name=name, grid=(P, R // tr), in_specs=[blk, blk], out_specs=blk,
        out_shape=jax.ShapeDtypeStruct((P, R, C), F32), compiler_params=_params("parallel", "parallel"),
    )(a, b)


_ANY = pl.BlockSpec(memory_space=pl.ANY)


def _place():
    return lax.axis_index("x"), lax.axis_index("y"), lax.axis_index("c")


def _all_gather(shard, *, name):
    R, C = shard.shape

    def body(x_ref, out_ref, send_sems, recv_sems, local_sem):
        x, y, c = _place()
        me, sibling = (x, y, c), (x, y, 1 - c)
        chips = [(1 - x, y), (x, 1 - y), (1 - x, 1 - y)]

        def slot(px, py, pc):
            return out_ref.at[4 * px + 2 * py + pc]

        def copy(k, block, to, src=None):
            return pltpu.make_async_remote_copy(
                src_ref=slot(*block) if src is None else src, dst_ref=slot(*block),
                send_sem=send_sems.at[k], recv_sem=recv_sems.at[k], device_id=to, device_id_type=MESH)

        mine = pltpu.make_async_copy(x_ref, slot(*me), local_sem)
        mine.start()
        first = [copy(0, me, sibling, src=x_ref)]
        first += [copy(1 + j, me, (*chip, c), src=x_ref) for j, chip in enumerate(chips)]
        for cp in first:
            cp.start()
        passed = [copy(4 + j, (*chip, c), sibling) for j, chip in enumerate(chips)]
        for j, chip in enumerate(chips):
            copy(1 + j, (*chip, c), me).wait_recv()
            passed[j].start()
        copy(0, sibling, me).wait_recv()
        for j, chip in enumerate(chips):
            copy(4 + j, (*chip, 1 - c), me).wait_recv()
        for cp in first + passed:
            cp.wait_send()
        mine.wait()

    return pl.pallas_call(
        body, name=name, in_specs=[_ANY], out_specs=_ANY,
        out_shape=jax.ShapeDtypeStruct((N_DEV, R, C), shard.dtype),
        scratch_shapes=[pltpu.SemaphoreType.DMA((7,)), pltpu.SemaphoreType.DMA((7,)), pltpu.SemaphoreType.DMA],
    )(shard)


def _rs_sibling(g, *, name):
    _, R, C = g.shape

    def body(g_ref, own_ref, sib_ref, send_sems, recv_sems, local_sems):
        x, y, c = _place()
        sibling = (x, y, 1 - c)
        locs, sends = [], []
        for k in range(4):
            locs.append(pltpu.make_async_copy(g_ref.at[2 * k + c], own_ref.at[k], local_sems.at[k]))
            sends.append(pltpu.make_async_remote_copy(
                src_ref=g_ref.at[2 * k + (1 - c)], dst_ref=sib_ref.at[k], send_sem=send_sems.at[k], recv_sem=recv_sems.at[k],
                device_id=sibling, device_id_type=MESH))
        for cp in sends + locs:
            cp.start()
        for cp in sends:
            cp.wait_recv()
        for cp in sends:
            cp.wait_send()
        for cp in locs:
            cp.wait()

    return pl.pallas_call(
        body, name=name, in_specs=[_ANY], out_specs=[_ANY, _ANY],
        out_shape=[jax.ShapeDtypeStruct((4, R, C), g.dtype)] * 2,
        scratch_shapes=[pltpu.SemaphoreType.DMA((4,)), pltpu.SemaphoreType.DMA((4,)), pltpu.SemaphoreType.DMA((4,))],
    )(g)


def _rs_chips(part, *, name):
    _, R, C = part.shape

    def body(p_ref, out_ref, send_sems, recv_sems, local_sem):
        x, y, c = _place()
        chips = [(1 - x, y), (x, 1 - y), (1 - x, 1 - y)]
        mine = pltpu.make_async_copy(p_ref.at[2 * x + y], out_ref.at[0], local_sem)
        mine.start()
        sends = [pltpu.make_async_remote_copy(
            src_ref=p_ref.at[2 * px + py], dst_ref=out_ref.at[1 + j], send_sem=send_sems.at[j], recv_sem=recv_sems.at[j],
            device_id=(px, py, c), device_id_type=MESH) for j, (px, py) in enumerate(chips)]
        for cp in sends:
            cp.start()
        for cp in sends:
            cp.wait_recv()
        for cp in sends:
            cp.wait_send()
        mine.wait()

    return pl.pallas_call(
        body, name=name, in_specs=[_ANY], out_specs=_ANY,
        out_shape=jax.ShapeDtypeStruct((4, R, C), part.dtype),
        scratch_shapes=[pltpu.SemaphoreType.DMA((3,)), pltpu.SemaphoreType.DMA((3,)), pltpu.SemaphoreType.DMA],
    )(part)


def _interleave_rows(wt):
    F2, D = wt.shape
    return wt.reshape(2, F2 // 256, 128, D).transpose(1, 0, 2, 3).reshape(F2, D)


def _deinterleave_rows(wt):
    F2, D = wt.shape
    return wt.reshape(F2 // 256, 2, 128, D).transpose(1, 0, 2, 3).reshape(F2, D)


def _interleave_cols(v):
    k, F2 = v.shape
    return v.reshape(k, 2, F2 // 256, 128).transpose(0, 2, 1, 3).reshape(k, F2)


def _deinterleave_cols(v):
    k, F2 = v.shape
    return v.reshape(k, F2 // 256, 2, 128).transpose(0, 2, 1, 3).reshape(k, F2)


def _pack_rows(parts):
    return jnp.concatenate(parts, axis=0)


def _flat_pack(parts, width):
    flat = jnp.concatenate([p.reshape(-1) for p in parts])
    pad = (-flat.shape[0]) % (8 * width)
    return jnp.pad(flat, (0, pad)).reshape(-1, width)


def _flat_unpack(packed, shapes):
    flat = packed.reshape(-1)
    out, off = [], 0
    for shp in shapes:
        n = int(np.prod(shp))
        out.append(flat[off:off + n].reshape(shp))
        off += n
    return out


def _ffn_forward(x, g_ffn, wupT, wdown, dw_i, dwb_i, tag):
    hf = _rms_fwd(x, g_ffn, name=f"ffn{tag}_rms")
    up = _mm(hf, wupT, mode="nt", out_dtype=F32, name=f"ffn{tag}_up", tn=512)
    act = _ffn_act_fwd(up, dw_i, dwb_i, name=f"ffn{tag}_act")
    y = _mm(act, wdown, mode="nn", out_dtype=F32, name=f"ffn{tag}_down", residual=x)
    return y, (hf, up, act)


def _ffn_backward(x, g_ffn, wupT, wdown, dw_i, dwb_i, saved, dy, dyb, tag):
    hf, up, act = saved
    dact = _mm(dyb, wdown, mode="nt", out_dtype=F32, name=f"ffn{tag}_dact", tm=512, tn=1408)
    d_wdown = _mm(act, dyb, mode="tn", out_dtype=F32, name=f"ffn{tag}_dwdown", tm=1408, tk=2048)
    dup, d_dw_i, d_dwb_i = _ffn_act_bwd(up, dact, dw_i, dwb_i, name=f"ffn{tag}_actbwd")
    dhf = _mm(dup, wupT, mode="nn", out_dtype=F32, name=f"ffn{tag}_dhf", tm=512)
    d_wupT = _mm(dup, hf, mode="tn", out_dtype=F32, name=f"ffn{tag}_dwup", tm=512, tk=2048)
    dx, dxb, dg, cs = _rms_bwd(x, g_ffn, dhf, dy, name=f"ffn{tag}_rmsbwd")
    return dx, dxb, cs, dict(w_upT=d_wupT, w_down=d_wdown, dw=d_dw_i[0:FFN_KERNEL], dw_b=d_dwb_i, norm=dg)


def _local_step(x, target, p):
    S, D = x.shape
    H = D // HEAD_DIM
    h0 = _rms_fwd(x, p["norm_mix"][0:1], name="l0_rms")
    u = _mm(h0, p["w_inT"], mode="nt", out_dtype=F32, name="l0_in", bias=p["cm_b_in"])
    c, s = _cm_fwd(u, p["cm_dw"], p["cm_dw_b"], p["cm_ln_g"], p["cm_ln_b"], name="l0_conv")
    x1 = _mm(s, p["w_out"], mode="nn", out_dtype=F32, name="l0_out", bias=p["cm_b_out"], residual=x)
    x2, sv0 = _ffn_forward(x1, p["norm_ffn"][0:1], p["w_upT"][0], p["w_down"][0], p["ff_dw"][0], p["ff_dw_b"][0:1], 0)
    h1 = _rms_fwd(x2, p["norm_mix"][1:2], name="l1_rms")
    qkv = _mm(h1, p["w_qkvT"], mode="nt", out_dtype=MXU_DTYPE, name="l1_qkv")
    os_, ls_ = [], []
    for g in range(3):
        o, l = _attn_fwd(qkv, p["at_q_norm"][g * H:(g + 1) * H], p["at_k_norm"][g * H:(g + 1) * H], grp=g, name=f"l1_attn{g}")
        os_.append(o)
        ls_.append(l)
    out, outb, lse = _attn_merge(os_, ls_, name="l1_merge")
    x3 = _mm(outb, p["w_o"], mode="nn", out_dtype=F32, name="l1_o", residual=x2)
    x4, sv1 = _ffn_forward(x3, p["norm_ffn"][1:2], p["w_upT"][1], p["w_down"][1], p["ff_dw"][1], p["ff_dw_b"][1:2], 1)
    dx4, dx4b, loss = _loss_head(x4, target, name="loss")
    dx3, dx3b, _, gf1 = _ffn_backward(x3, p["norm_ffn"][1:2], p["w_upT"][1], p["w_down"][1], p["ff_dw"][1], p["ff_dw_b"][1:2],
                                      sv1, dx4, dx4b, 1)
    do = _mm(dx3b, p["w_o"], mode="nt", out_dtype=MXU_DTYPE, name="l1_do")
    d_wo = _mm(outb, dx3b, mode="tn", out_dtype=F32, name="l1_dwo", tk=2048)
    delta = _attn_delta(do, out, name="l1_delta")
    dh1 = None
    d_wqkvT, dqg, dkg = [], [], []
    for g in range(3):
        dqkv_g, a, b_ = _attn_bwd(qkv, do, lse, delta, p["at_q_norm"][g * H:(g + 1) * H], p["at_k_norm"][g * H:(g + 1) * H],
                                  grp=g, name=f"l1_attnbwd{g}")
        dqg.append(a)
        dkg.append(b_)
        d_wqkvT.append(_mm(dqkv_g, h1, mode="tn", out_dtype=F32, name=f"l1_dwqkv{g}", tk=2048))
        dh1 = _mm(dqkv_g, p["w_qkvT"], mode="nn", out_dtype=F32, name=f"l1_dh{g}", b_off=g * 3 * D, b_len=3 * D, residual=dh1)
    dx2, dx2b, dgm1, _ = _rms_bwd(x2, p["norm_mix"][1:2], dh1, dx3, name="l1_rmsbwd")
    dx1, dx1b, cs1, gf0 = _ffn_backward(x1, p["norm_ffn"][0:1], p["w_upT"][0], p["w_down"][0], p["ff_dw"][0], p["ff_dw_b"][0:1],
                                        sv0, dx2, dx2b, 0)
    ds = _mm(dx1b, p["w_out"], mode="nt", out_dtype=F32, name="l0_ds")
    d_wout = _mm(s, dx1b, mode="tn", out_dtype=F32, name="l0_dwout", tk=2048)
    dc, d_lng, d_lnb = _cm_ln_bwd(c, ds, p["cm_ln_g"], p["cm_ln_b"], name="l0_lnbwd")
    du, d_cmdw, d_cmdwb, d_bin = _cm_conv_bwd(dc, u, p["cm_dw"], name="l0_convbwd")
    dh0 = _mm(du, p["w_inT"], mode="nn", out_dtype=F32, name="l0_dh")
    d_winT = _mm(du, h0, mode="tn", out_dtype=F32, name="l0_dwin", tk=2048)
    grad_x, _, dgm0, _ = _rms_bwd(x, p["norm_mix"][0:1], dh0, dx1, name="l0_rmsbwd")
    grads = dict(
        norm_mix=jnp.concatenate([dgm0, dgm1], axis=0),
        norm_ffn=jnp.concatenate([gf0["norm"], gf1["norm"]], axis=0),
        w_inT=d_winT, cm_b_in=d_bin, cm_dw=d_cmdw[0:CONV_KERNEL], cm_dw_b=d_cmdwb, cm_ln_g=d_lng, cm_ln_b=d_lnb,
        w_out=d_wout, cm_b_out=cs1,
        w_qkvT=jnp.concatenate(d_wqkvT, axis=0), at_q_norm=jnp.concatenate(dqg, axis=0), at_k_norm=jnp.concatenate(dkg, axis=0),
        w_o=d_wo,
        w_upT=[gf0["w_upT"], gf1["w_upT"]], w_down=[gf0["w_down"], gf1["w_down"]],
        ff_dw=jnp.stack([gf0["dw"], gf1["dw"]]), ff_dw_b=jnp.concatenate([gf0["dw_b"], gf1["dw_b"]], axis=0),
    )
    return loss, grad_x, grads


_BIG = ("cm_w_in", "cm_w_out", "at_w_qkv", "at_w_out", "ff_w_up", "ff_w_down")
_TRANSPOSED = ("cm_w_in", "at_w_qkv", "ff_w_up")
_SMALL = ("norm_mix", "norm_ffn", "cm_b_in", "cm_dw_b", "cm_ln_g", "cm_ln_b", "cm_b_out", "at_q_norm", "at_k_norm",
          "ff_dw_b", "cm_dw", "ff_dw")
_SMALL_SHARDED = ("cm_dw", "ff_dw")
_ORDER = ("norm_mix", "norm_ffn", "cm_w_in", "cm_b_in", "cm_dw", "cm_dw_b", "cm_ln_g", "cm_ln_b", "cm_w_out", "cm_b_out",
          "at_w_qkv", "at_q_norm", "at_k_norm", "at_w_out", "ff_w_up", "ff_dw", "ff_dw_b", "ff_w_down")


def _big_rows(t):
    parts = []
    for n in _BIG:
        a = t[n]
        mats = [a[l] for l in range(a.shape[0])]
        if n in _TRANSPOSED:
            mats = [m.T for m in mats]
        parts += mats
    return _pack_rows(parts)


def _big_unrows(packed, like):
    out, off = {}, 0
    for n in _BIG:
        a = like[n]
        mats = []
        for l in range(a.shape[0]):
            rows, cols = (a.shape[2], a.shape[1]) if n in _TRANSPOSED else (a.shape[1], a.shape[2])
            m = packed[off:off + rows]
            off += rows
            mats.append(m.T if n in _TRANSPOSED else m)
        out[n] = jnp.stack(mats)
    return out


def kernel(x, norm_mix, norm_ffn, cm_w_in, cm_b_in, cm_dw, cm_dw_b, cm_ln_g, cm_ln_b, cm_w_out, cm_b_out, at_w_qkv, at_q_norm, at_k_norm, at_w_out, ff_w_up, ff_dw, ff_dw_b, ff_w_down, loss_target, m_norm_mix, m_norm_ffn, m_cm_w_in, m_cm_b_in, m_cm_dw, m_cm_dw_b, m_cm_ln_g, m_cm_ln_b, m_cm_w_out, m_cm_b_out, m_at_w_qkv, m_at_q_norm, m_at_k_norm, m_at_w_out, m_ff_w_up, m_ff_dw, m_ff_dw_b, m_ff_w_down, v_norm_mix, v_norm_ffn, v_cm_w_in, v_cm_b_in, v_cm_dw, v_cm_dw_b, v_cm_ln_g, v_cm_ln_b, v_cm_w_out, v_cm_b_out, v_at_w_qkv, v_at_q_norm, v_at_k_norm, v_at_w_out, v_ff_w_up, v_ff_dw, v_ff_dw_b, v_ff_w_down):
    w = dict(norm_mix=norm_mix, norm_ffn=norm_ffn, cm_w_in=cm_w_in, cm_b_in=cm_b_in, cm_dw=cm_dw, cm_dw_b=cm_dw_b, cm_ln_g=cm_ln_g,
             cm_ln_b=cm_ln_b, cm_w_out=cm_w_out, cm_b_out=cm_b_out, at_w_qkv=at_w_qkv, at_q_norm=at_q_norm, at_k_norm=at_k_norm,
             at_w_out=at_w_out, ff_w_up=ff_w_up, ff_dw=ff_dw, ff_dw_b=ff_dw_b, ff_w_down=ff_w_down)
    m = dict(norm_mix=m_norm_mix, norm_ffn=m_norm_ffn, cm_w_in=m_cm_w_in, cm_b_in=m_cm_b_in, cm_dw=m_cm_dw, cm_dw_b=m_cm_dw_b,
             cm_ln_g=m_cm_ln_g, cm_ln_b=m_cm_ln_b, cm_w_out=m_cm_w_out, cm_b_out=m_cm_b_out, at_w_qkv=m_at_w_qkv,
             at_q_norm=m_at_q_norm, at_k_norm=m_at_k_norm, at_w_out=m_at_w_out, ff_w_up=m_ff_w_up, ff_dw=m_ff_dw,
             ff_dw_b=m_ff_dw_b, ff_w_down=m_ff_w_down)
    v = dict(norm_mix=v_norm_mix, norm_ffn=v_norm_ffn, cm_w_in=v_cm_w_in, cm_b_in=v_cm_b_in, cm_dw=v_cm_dw, cm_dw_b=v_cm_dw_b,
             cm_ln_g=v_cm_ln_g, cm_ln_b=v_cm_ln_b, cm_w_out=v_cm_w_out, cm_b_out=v_cm_b_out, at_w_qkv=v_at_w_qkv,
             at_q_norm=v_at_q_norm, at_k_norm=v_at_k_norm, at_w_out=v_at_w_out, ff_w_up=v_ff_w_up, ff_dw=v_ff_dw,
             ff_dw_b=v_ff_dw_b, ff_w_down=v_ff_w_down)
    S, D = x.shape[1], x.shape[2]
    F2 = ff_dw_b.shape[1]
    H3 = at_q_norm.shape[1]
    me = 4 * lax.axis_index("x") + 2 * lax.axis_index("y") + lax.axis_index("c")

    w_rows = _big_rows(w)
    gathered = _all_gather(w_rows.astype(MXU_DTYPE), name="gather_weights")
    full, off = {}, 0
    for n in _BIG:
        a = w[n]
        mats = []
        for l in range(a.shape[0]):
            rows = a.shape[2] if n in _TRANSPOSED else a.shape[1]
            mats.append(gathered[:, off:off + rows, :].reshape(N_DEV * rows, D))
            off += rows
        full[n] = mats
    small_sh = _flat_pack([cm_dw, ff_dw], D)
    small_g = _all_gather(small_sh, name="gather_small")
    cm_dw_full = jnp.concatenate([_flat_unpack(small_g[j], [cm_dw.shape, ff_dw.shape])[0][0] for j in range(N_DEV)], axis=-1)
    ff_dw_full = jnp.concatenate([_flat_unpack(small_g[j], [cm_dw.shape, ff_dw.shape])[1] for j in range(N_DEV)], axis=-1)

    p = dict(
        norm_mix=norm_mix, norm_ffn=norm_ffn, cm_b_in=cm_b_in, cm_dw=cm_dw_full, cm_dw_b=cm_dw_b, cm_ln_g=cm_ln_g, cm_ln_b=cm_ln_b,
        cm_b_out=cm_b_out, at_q_norm=at_q_norm[0], at_k_norm=at_k_norm[0],
        w_inT=full["cm_w_in"][0], w_out=full["cm_w_out"][0], w_qkvT=full["at_w_qkv"][0], w_o=full["at_w_out"][0],
        w_upT=[_interleave_rows(t) for t in full["ff_w_up"]], w_down=full["ff_w_down"],
        ff_dw=jnp.stack([_interleave_cols(ff_dw_full[l]) for l in range(ff_dw_full.shape[0])]),
        ff_dw_b=_interleave_cols(ff_dw_b),
    )
    loss8, grad_x, g = _local_step(x[0], loss_target[0], p)
    loss = lax.psum(loss8[0, 0], ("x", "y", "c"))

    pieces = [g["w_inT"], g["w_out"], g["w_qkvT"], g["w_o"], _deinterleave_rows(g["w_upT"][0]), _deinterleave_rows(g["w_upT"][1]),
              g["w_down"][0], g["w_down"][1]]
    g_rows = jnp.concatenate([t.reshape(N_DEV, t.shape[0] // N_DEV, D) for t in pieces], axis=1)
    own, sib = _rs_sibling(g_rows, name="reduce_sibling")
    part = _add(own, sib, name="reduce_add")
    parts = _rs_chips(part, name="reduce_chips")
    gb, db, mb, vb = _adamw(w_rows, _big_rows(m), _big_rows(v), parts, name="adamw_big")
    big = [_big_unrows(t, w) for t in (gb, db, mb, vb)]

    g_small = dict(g)
    g_small["cm_b_in"] = g["cm_b_in"]
    g_small["at_q_norm"] = g["at_q_norm"][None]
    g_small["at_k_norm"] = g["at_k_norm"][None]
    g_small["ff_dw_b"] = _deinterleave_cols(g["ff_dw_b"])
    g_small["cm_dw"] = g["cm_dw"][None]
    g_small["ff_dw"] = jnp.stack([_deinterleave_cols(g["ff_dw"][l]) for l in range(g["ff_dw"].shape[0])])
    small_shapes = [g_small[n].shape for n in _SMALL]
    gs_parts = _all_gather(_flat_pack([g_small[n] for n in _SMALL], D), name="gather_small_grads")

    def embed(t, n):
        if n not in _SMALL_SHARDED:
            return t
        full_shape = t.shape[:-1] + (t.shape[-1] * N_DEV,)
        return lax.dynamic_update_slice_in_dim(jnp.zeros(full_shape, F32), t, me * t.shape[-1], axis=t.ndim - 1)

    packs = [_flat_pack([embed(tree[n], n) for n in _SMALL], D) for tree in (w, m, v)]
    gs, ds_, ms, vs = _adamw(packs[0], packs[1], packs[2], gs_parts, name="adamw_small")
    small = []
    for t in (gs, ds_, ms, vs):
        un = dict(zip(_SMALL, _flat_unpack(t, small_shapes)))
        for n in _SMALL_SHARDED:
            width = w[n].shape[-1]
            un[n] = lax.dynamic_slice_in_dim(un[n], me * width, width, axis=un[n].ndim - 1)
        small.append({n: un[n].reshape(w[n].shape) for n in _SMALL})

    outs = [loss, grad_x[None]]
    for k in range(4):
        for n in _ORDER:
            outs.append(big[k][n] if n in _BIG else small[k][n])
    return tuple(outs)
```

```python
import functools

import jax
import jax.numpy as jnp
import numpy as np
from jax import lax
from jax.experimental import pallas as pl
from jax.experimental.pallas import tpu as pltpu

F32 = jnp.float32
MXU_DTYPE = jnp.bfloat16
WIRE_DTYPE = jnp.bfloat16
EPS = 1e-6
NEG = -1e30
HEAD_DIM = 128
BLOCK = 128
DILATED_GROUPS = ((128, 1), (512, 4), (2048, 16))
ALIBI_MAX = 8.0
CONV_KERNEL = 31
CONV_HALO = 32
FFN_KERNEL = 3
FFN_HALO = 16
ADAM_LR, ADAM_B1, ADAM_B2, ADAM_EPS, ADAM_WD, ADAM_STEP = 0.001, 0.9, 0.999, 1e-08, 0.01, 10
V7X_VMEM_BYTES = 64 * 1024 * 1024
VMEM_LIMIT = V7X_VMEM_BYTES * 3 // 4
N_DEV = 8
MESH = pl.DeviceIdType.MESH


def _pick(n, target, align):
    if n <= target:
        return n
    best = None
    for t in range(align, target + 1, align):
        if n % t == 0:
            best = t
    assert best is not None, (n, target, align)
    return best


def _params(*sem):
    return pltpu.CompilerParams(dimension_semantics=sem, vmem_limit_bytes=VMEM_LIMIT)


def _sigmoid(x):
    return 1.0 / (1.0 + jnp.exp(-x))


_DIMS = {"nn": ((1,), (0,)), "nt": ((1,), (1,)), "tn": ((0,), (0,))}


def _mm(a, b, *, mode, out_dtype, name, tm=1024, tn=1024, tk=None, bias=None, residual=None, b_off=0, b_len=None):
    if mode == "tn":
        K, M = a.shape
    else:
        M, K = a.shape
    if mode == "nt":
        N = b.shape[0] if b_len is None else b_len
    else:
        N = b.shape[1]
    if b_len is not None:
        assert mode == "nt" or (mode == "nn" and K == b_len)
    tm = _pick(M, tm, 128 if mode == "tn" else 16)
    tn = _pick(N, tn, 128)
    tk = K if tk is None else _pick(K, tk, 128 if mode != "tn" else 16)
    nk = K // tk
    unit = tn if mode == "nt" else tk
    assert b_off % unit == 0
    kb0 = b_off // unit
    if mode == "tn":
        a_spec = pl.BlockSpec((tk, tm), lambda i, j, k: (k, i))
    else:
        a_spec = pl.BlockSpec((tm, tk), lambda i, j, k: (i, k))
    if mode == "nt":
        b_spec = pl.BlockSpec((tn, tk), lambda i, j, k: (j + kb0, k))
    else:
        b_spec = pl.BlockSpec((tk, tn), lambda i, j, k: (k + kb0, j))
    in_specs = [a_spec, b_spec]
    args = [a, b]
    if bias is not None:
        in_specs.append(pl.BlockSpec((1, tn), lambda i, j, k: (0, j)))
        args.append(bias)
    if residual is not None:
        in_specs.append(pl.BlockSpec((tm, tn), lambda i, j, k: (i, j)))
        args.append(residual)
    has_bias, has_res = bias is not None, residual is not None

    def body(*refs):
        a_ref, b_ref = refs[0], refs[1]
        pos = 2
        bias_ref = res_ref = None
        if has_bias:
            bias_ref = refs[pos]
            pos += 1
        if has_res:
            res_ref = refs[pos]
            pos += 1
        o_ref = refs[pos]
        acc_ref = refs[pos + 1] if nk > 1 else None

        def finish(acc):
            if has_bias:
                acc = acc + bias_ref[...]
            if has_res:
                acc = acc + res_ref[...]
            o_ref[...] = acc.astype(o_ref.dtype)

        part = lax.dot_general(a_ref[...].astype(MXU_DTYPE), b_ref[...].astype(MXU_DTYPE), (_DIMS[mode], ((), ())),
                               preferred_element_type=F32)
        if nk == 1:
            finish(part)
        else:
            k = pl.program_id(2)

            @pl.when(k == 0)
            def _():
                acc_ref[...] = part

            @pl.when(jnp.logical_and(k > 0, k < nk - 1))
            def _():
                acc_ref[...] += part

            @pl.when(k == nk - 1)
            def _():
                finish(acc_ref[...] + part)

    return pl.pallas_call(
        body, name=name, grid=(M // tm, N // tn, nk), in_specs=in_specs,
        out_specs=pl.BlockSpec((tm, tn), lambda i, j, k: (i, j)),
        out_shape=jax.ShapeDtypeStruct((M, N), out_dtype),
        scratch_shapes=[pltpu.VMEM((tm, tn), F32)] if nk > 1 else [],
        compiler_params=_params("parallel", "parallel", "arbitrary"),
    )(*args)


def _rms_fwd(x, g, *, name):
    S, D = x.shape
    ts = _pick(S, 512, 16)

    def body(x_ref, g_ref, h_ref):
        xv = x_ref[...]
        r = lax.rsqrt(jnp.mean(xv * xv, axis=-1, keepdims=True) + EPS)
        h_ref[...] = (xv * r * g_ref[...]).astype(h_ref.dtype)

    return pl.pallas_call(
        body, name=name, grid=(S // ts,),
        in_specs=[pl.BlockSpec((ts, D), lambda i: (i, 0)), pl.BlockSpec((1, D), lambda i: (0, 0))],
        out_specs=pl.BlockSpec((ts, D), lambda i: (i, 0)),
        out_shape=jax.ShapeDtypeStruct((S, D), MXU_DTYPE),
        compiler_params=_params("parallel"),
    )(x, g)


def _rms_bwd(x, g, dhs, dres, *, name):
    S, D = x.shape
    ts = _pick(S, 512, 16)
    n_dh = len(dhs)

    def body(*refs):
        x_ref, g_ref = refs[0], refs[1]
        dh_refs = refs[2:2 + n_dh]
        dres_ref, dx_ref, dxb_ref, dg_ref, cs_ref = refs[2 + n_dh:]
        i = pl.program_id(0)
        xv = x_ref[...]
        r = lax.rsqrt(jnp.mean(xv * xv, axis=-1, keepdims=True) + EPS)
        xh = xv * r
        dhv = dh_refs[0][...].astype(F32)
        for t in dh_refs[1:]:
            dhv = dhv + t[...].astype(F32)
        gy = dhv * g_ref[...]
        dx = r * (gy - xh * jnp.mean(gy * xh, axis=-1, keepdims=True)) + dres_ref[...]
        dx_ref[...] = dx
        dxb_ref[...] = dx.astype(dxb_ref.dtype)
        dg = jnp.sum(dhv * xh, axis=0, keepdims=True)
        cs = jnp.sum(dx, axis=0, keepdims=True)

        @pl.when(i == 0)
        def _():
            dg_ref[...] = dg
            cs_ref[...] = cs

        @pl.when(i > 0)
        def _():
            dg_ref[...] += dg
            cs_ref[...] += cs

    row = pl.BlockSpec((ts, D), lambda i: (i, 0))
    vec = pl.BlockSpec((1, D), lambda i: (0, 0))
    return pl.pallas_call(
        body, name=name, grid=(S // ts,),
        in_specs=[row, vec] + [row] * (n_dh + 1),
        out_specs=[row, row, vec, vec],
        out_shape=[jax.ShapeDtypeStruct((S, D), F32), jax.ShapeDtypeStruct((S, D), MXU_DTYPE),
                   jax.ShapeDtypeStruct((1, D), F32), jax.ShapeDtypeStruct((1, D), F32)],
        compiler_params=_params("arbitrary"),
    )(x, g, *dhs, dres)


def _cm_fwd(u, dw, dw_b, ln_g, ln_b, *, name):
    S, D2 = u.shape
    D = D2 // 2
    ts = _pick(S, 256, CONV_HALO)
    hb = ts // CONV_HALO

    def body(u_ref, up_ref, dw_ref, dwb_ref, g_ref, b_ref, c_ref, s_ref, ext_ref):
        i = pl.program_id(0)
        prev = up_ref[:, :D] * _sigmoid(up_ref[:, D:])
        ext_ref[0:CONV_HALO, :] = jnp.where(i > 0, prev, 0.0)
        ext_ref[CONV_HALO:CONV_HALO + ts, :] = u_ref[:, :D] * _sigmoid(u_ref[:, D:])
        for cc in range(D // 128):
            sl = slice(cc * 128, (cc + 1) * 128)
            acc = jnp.zeros((ts, 128), F32) + dwb_ref[:, sl]
            for k in range(CONV_KERNEL):
                acc = acc + dw_ref[k:k + 1, sl] * ext_ref[pl.ds(CONV_HALO - (CONV_KERNEL - 1) + k, ts), sl]
            c_ref[:, sl] = acc
        c = c_ref[...]
        mu = jnp.mean(c, axis=-1, keepdims=True)
        xc = c - mu
        rstd = lax.rsqrt(jnp.mean(xc * xc, axis=-1, keepdims=True) + EPS)
        y = xc * rstd * g_ref[...] + b_ref[...]
        s_ref[...] = (y * _sigmoid(y)).astype(s_ref.dtype)

    vec = pl.BlockSpec((1, D), lambda i: (0, 0))
    return pl.pallas_call(
        body, name=name, grid=(S // ts,),
        in_specs=[pl.BlockSpec((ts, D2), lambda i: (i, 0)),
                  pl.BlockSpec((CONV_HALO, D2), lambda i: (jnp.maximum(i * hb - 1, 0), 0)),
                  pl.BlockSpec((CONV_KERNEL, D), lambda i: (0, 0)), vec, vec, vec],
        out_specs=[pl.BlockSpec((ts, D), lambda i: (i, 0)), pl.BlockSpec((ts, D), lambda i: (i, 0))],
        out_shape=[jax.ShapeDtypeStruct((S, D), F32), jax.ShapeDtypeStruct((S, D), MXU_DTYPE)],
        scratch_shapes=[pltpu.VMEM((ts + CONV_HALO, D), F32)],
        compiler_params=_params("parallel"),
    )(u, u, dw, dw_b, ln_g, ln_b)


def _cm_ln_bwd(c, ds, ln_g, ln_b, *, name):
    S, D = c.shape
    ts = _pick(S, 512, 16)

    def body(c_ref, ds_ref, g_ref, b_ref, dc_ref, dg_ref, db_ref):
        i = pl.program_id(0)
        cv = c_ref[...]
        mu = jnp.mean(cv, axis=-1, keepdims=True)
        xc = cv - mu
        rstd = lax.rsqrt(jnp.mean(xc * xc, axis=-1, keepdims=True) + EPS)
        xh = xc * rstd
        y = xh * g_ref[...] + b_ref[...]
        sg = _sigmoid(y)
        dy = ds_ref[...].astype(F32) * (sg * (1.0 + y * (1.0 - sg)))
        gy = dy * g_ref[...]
        dc_ref[...] = rstd * (gy - jnp.mean(gy, axis=-1, keepdims=True) - xh * jnp.mean(gy * xh, axis=-1, keepdims=True))
        dg = jnp.sum(dy * xh, axis=0, keepdims=True)
        db = jnp.sum(dy, axis=0, keepdims=True)

        @pl.when(i == 0)
        def _():
            dg_ref[...] = dg
            db_ref[...] = db

        @pl.when(i > 0)
        def _():
            dg_ref[...] += dg
            db_ref[...] += db

    row = pl.BlockSpec((ts, D), lambda i: (i, 0))
    vec = pl.BlockSpec((1, D), lambda i: (0, 0))
    return pl.pallas_call(
        body, name=name, grid=(S // ts,), in_specs=[row, row, vec, vec], out_specs=[row, vec, vec],
        out_shape=[jax.ShapeDtypeStruct((S, D), F32), jax.ShapeDtypeStruct((1, D), F32), jax.ShapeDtypeStruct((1, D), F32)],
        compiler_params=_params("arbitrary"),
    )(c, ds, ln_g, ln_b)


def _cm_conv_bwd(dc, u, dw, *, name):
    S, D2 = u.shape
    D = D2 // 2
    ts = _pick(S, 256, CONV_HALO)
    hb = ts // CONV_HALO
    n_t = S // ts
    last_h = S // CONV_HALO - 1

    def body(dc_ref, dcn_ref, u_ref, up_ref, dw_ref, du_ref, ddw_ref, ddwb_ref, dbin_ref, dce_ref, ge_ref):
        i = pl.program_id(0)
        dce_ref[0:ts, :] = dc_ref[...]
        dce_ref[ts:ts + CONV_HALO, :] = jnp.where(i < n_t - 1, dcn_ref[...], 0.0)
        prev = up_ref[:, :D] * _sigmoid(up_ref[:, D:])
        ge_ref[0:CONV_HALO, :] = jnp.where(i > 0, prev, 0.0)
        a = u_ref[:, :D]
        sg = _sigmoid(u_ref[:, D:])
        ge_ref[CONV_HALO:CONV_HALO + ts, :] = a * sg

        @pl.when(i == 0)
        def _():
            ddw_ref[...] = jnp.zeros_like(ddw_ref)
            ddwb_ref[...] = jnp.zeros_like(ddwb_ref)
            dbin_ref[...] = jnp.zeros_like(dbin_ref)

        for cc in range(D // 128):
            sl = slice(cc * 128, (cc + 1) * 128)
            dcc = dce_ref[0:ts, sl]
            acc = jnp.zeros((ts, 128), F32)
            for k in range(CONV_KERNEL):
                acc = acc + dw_ref[k:k + 1, sl] * dce_ref[pl.ds(CONV_KERNEL - 1 - k, ts), sl]
                gk = ge_ref[pl.ds(CONV_HALO - (CONV_KERNEL - 1) + k, ts), sl]
                ddw_ref[k:k + 1, sl] += jnp.sum(dcc * gk, axis=0, keepdims=True)
            ddwb_ref[:, sl] += jnp.sum(dcc, axis=0, keepdims=True)
            sgc = sg[:, sl]
            da = acc * sgc
            dg = acc * a[:, sl] * sgc * (1.0 - sgc)
            du_ref[:, sl] = da.astype(du_ref.dtype)
            du_ref[:, D + cc * 128:D + (cc + 1) * 128] = dg.astype(du_ref.dtype)
            dbin_ref[:, sl] += jnp.sum(da, axis=0, keepdims=True)
            dbin_ref[:, D + cc * 128:D + (cc + 1) * 128] += jnp.sum(dg, axis=0, keepdims=True)

    return pl.pallas_call(
        body, name=name, grid=(n_t,),
        in_specs=[pl.BlockSpec((ts, D), lambda i: (i, 0)),
                  pl.BlockSpec((CONV_HALO, D), lambda i: (jnp.minimum((i + 1) * hb, last_h), 0)),
                  pl.BlockSpec((ts, D2), lambda i: (i, 0)),
                  pl.BlockSpec((CONV_HALO, D2), lambda i: (jnp.maximum(i * hb - 1, 0), 0)),
                  pl.BlockSpec((CONV_KERNEL, D), lambda i: (0, 0))],
        out_specs=[pl.BlockSpec((ts, D2), lambda i: (i, 0)), pl.BlockSpec((CONV_HALO, D), lambda i: (0, 0)),
                   pl.BlockSpec((1, D), lambda i: (0, 0)), pl.BlockSpec((1, D2), lambda i: (0, 0))],
        out_shape=[jax.ShapeDtypeStruct((S, D2), MXU_DTYPE), jax.ShapeDtypeStruct((CONV_HALO, D), F32),
                   jax.ShapeDtypeStruct((1, D), F32), jax.ShapeDtypeStruct((1, D2), F32)],
        scratch_shapes=[pltpu.VMEM((ts + CONV_HALO, D), F32), pltpu.VMEM((ts + CONV_HALO, D), F32)],
        compiler_params=_params("arbitrary"),
    )(dc, dc, u, u, dw)


def _ffn_cols(F2):
    return _pick(F2, 1024, 256)


def _ffn_act_fwd(up, dw, dw_b, *, name):
    S, F2 = up.shape
    ts = _pick(S, 512, 16)
    tc = _ffn_cols(F2)
    hb = ts // FFN_HALO

    def body(u_ref, up_ref, w_ref, b_ref, a_ref, ext_ref):
        i = pl.program_id(1)
        ext_ref[0:FFN_HALO, :] = jnp.where(i > 0, up_ref[...].astype(F32), 0.0)
        ext_ref[FFN_HALO:FFN_HALO + ts, :] = u_ref[...].astype(F32)
        for q in range(tc // 256):
            cvs = []
            for half in range(2):
                sl = slice(q * 256 + half * 128, q * 256 + half * 128 + 128)
                cv = b_ref[:, sl] + w_ref[2:3, sl] * ext_ref[pl.ds(FFN_HALO, ts), sl]
                cv = cv + w_ref[1:2, sl] * ext_ref[pl.ds(FFN_HALO - 1, ts), sl]
                cv = cv + w_ref[0:1, sl] * ext_ref[pl.ds(FFN_HALO - 2, ts), sl]
                cvs.append(cv)
            gt, vl = cvs
            a_ref[:, q * 128:(q + 1) * 128] = (gt * _sigmoid(gt) * vl).astype(a_ref.dtype)

    return pl.pallas_call(
        body, name=name, grid=(F2 // tc, S // ts),
        in_specs=[pl.BlockSpec((ts, tc), lambda j, i: (i, j)),
                  pl.BlockSpec((FFN_HALO, tc), lambda j, i: (jnp.maximum(i * hb - 1, 0), j)),
                  pl.BlockSpec((FFN_KERNEL, tc), lambda j, i: (0, j)),
                  pl.BlockSpec((1, tc), lambda j, i: (0, j))],
        out_specs=pl.BlockSpec((ts, tc // 2), lambda j, i: (i, j)),
        out_shape=jax.ShapeDtypeStruct((S, F2 // 2), MXU_DTYPE),
        scratch_shapes=[pltpu.VMEM((ts + FFN_HALO, tc), F32)],
        compiler_params=_params("parallel", "parallel"),
    )(up, up, dw, dw_b)


def _ffn_act_bwd(up, dact, dw, dw_b, *, name):
    S, F2 = up.shape
    ts = _pick(S, 512, 16)
    tc = _ffn_cols(F2)
    hb = ts // FFN_HALO
    n_t = S // ts
    last_h = S // FFN_HALO - 1
    E = ts + FFN_HALO

    def body(u_ref, up_ref, un_ref, da_ref, dan_ref, w_ref, b_ref, dup_ref, ddw_ref, ddb_ref, ue_ref, dae_ref, dcv_ref):
        i = pl.program_id(1)
        ue_ref[0:FFN_HALO, :] = jnp.where(i > 0, up_ref[...].astype(F32), 0.0)
        ue_ref[FFN_HALO:FFN_HALO + ts, :] = u_ref[...].astype(F32)
        ue_ref[FFN_HALO + ts:FFN_HALO + ts + FFN_HALO, :] = jnp.where(i < n_t - 1, un_ref[...].astype(F32), 0.0)
        dae_ref[0:ts, :] = da_ref[...].astype(F32)
        dae_ref[ts:E, :] = jnp.where(i < n_t - 1, dan_ref[...].astype(F32), 0.0)

        @pl.when(i == 0)
        def _():
            ddw_ref[...] = jnp.zeros_like(ddw_ref)
            ddb_ref[...] = jnp.zeros_like(ddb_ref)

        for q in range(tc // 256):
            sls = [slice(q * 256 + half * 128, q * 256 + half * 128 + 128) for half in range(2)]
            cvs = []
            for sl in sls:
                cv = b_ref[:, sl] + w_ref[2:3, sl] * ue_ref[pl.ds(FFN_HALO, E), sl]
                cv = cv + w_ref[1:2, sl] * ue_ref[pl.ds(FFN_HALO - 1, E), sl]
                cv = cv + w_ref[0:1, sl] * ue_ref[pl.ds(FFN_HALO - 2, E), sl]
                cvs.append(cv)
            gt, vl = cvs
            sg = _sigmoid(gt)
            dae = dae_ref[:, q * 128:(q + 1) * 128]
            dcv_ref[:, sls[0]] = dae * vl * (sg * (1.0 + gt * (1.0 - sg)))
            dcv_ref[:, sls[1]] = dae * (gt * sg)
            for sl in sls:
                d0 = dcv_ref[pl.ds(0, ts), sl]
                dup = w_ref[2:3, sl] * d0 + w_ref[1:2, sl] * dcv_ref[pl.ds(1, ts), sl] + w_ref[0:1, sl] * dcv_ref[pl.ds(2, ts), sl]
                dup_ref[:, sl] = dup.astype(dup_ref.dtype)
                for k in range(FFN_KERNEL):
                    ddw_ref[k:k + 1, sl] += jnp.sum(d0 * ue_ref[pl.ds(FFN_HALO - 2 + k, ts), sl], axis=0, keepdims=True)
                ddb_ref[:, sl] += jnp.sum(d0, axis=0, keepdims=True)

    return pl.pallas_call(
        body, name=name, grid=(F2 // tc, n_t),
        in_specs=[pl.BlockSpec((ts, tc), lambda j, i: (i, j)),
                  pl.BlockSpec((FFN_HALO, tc), lambda j, i: (jnp.maximum(i * hb - 1, 0), j)),
                  pl.BlockSpec((FFN_HALO, tc), lambda j, i: (jnp.minimum((i + 1) * hb, last_h), j)),
                  pl.BlockSpec((ts, tc // 2), lambda j, i: (i, j)),
                  pl.BlockSpec((FFN_HALO, tc // 2), lambda j, i: (jnp.minimum((i + 1) * hb, last_h), j)),
                  pl.BlockSpec((FFN_KERNEL, tc), lambda j, i: (0, j)),
                  pl.BlockSpec((1, tc), lambda j, i: (0, j))],
        out_specs=[pl.BlockSpec((ts, tc), lambda j, i: (i, j)), pl.BlockSpec((FFN_HALO, tc), lambda j, i: (0, j)),
                   pl.BlockSpec((1, tc), lambda j, i: (0, j))],
        out_shape=[jax.ShapeDtypeStruct((S, F2), MXU_DTYPE), jax.ShapeDtypeStruct((FFN_HALO, F2), F32),
                   jax.ShapeDtypeStruct((1, F2), F32)],
        scratch_shapes=[pltpu.VMEM((ts + 2 * FFN_HALO, tc), F32), pltpu.VMEM((E, tc // 2), F32), pltpu.VMEM((E, tc), F32)],
        compiler_params=_params("parallel", "arbitrary"),
    )(up, up, up, dact, dact, dw, dw_b)


def _slopes(n_heads_total):
    return np.asarray(2.0 ** (-ALIBI_MAX * (np.arange(n_heads_total, dtype=np.float32) + 1.0) / n_heads_total), np.float32)


def _qk_norm(x, gain):
    r = lax.rsqrt(jnp.mean(x * x, axis=-1, keepdims=True) + EPS)
    xh = x * r
    return xh * gain, xh, r


def _band(b, dil):
    qi = lax.broadcasted_iota(jnp.int32, (BLOCK, 2 * BLOCK), 0)
    ki = lax.broadcasted_iota(jnp.int32, (BLOCK, 2 * BLOCK), 1)
    delta = qi + BLOCK - ki
    valid = (delta >= 0) & (delta <= BLOCK) & ((ki >= BLOCK) | (b > 0))
    return valid, (delta * dil).astype(F32)


def _permute(x, dil, *, inverse, name):
    if inverse:
        _, L, C = x.shape
        S = dil * L
    else:
        S, C = x.shape
    ts = _pick(S, 512, 16 * dil)
    n = ts // dil
    NC = C // 128
    nat = pl.BlockSpec((ts, C), lambda i: (i, 0))
    sub = pl.BlockSpec((dil, n, C), lambda i: (0, i, 0))

    def body(x_ref, o_ref, nat_ref):
        for c in range(NC):
            sl = slice(c * 128, (c + 1) * 128)
            if inverse:
                for r in range(dil):
                    nat_ref.at[c][pl.ds(r, n, stride=dil), :] = x_ref[r, :, sl].astype(F32)
                o_ref[:, sl] = nat_ref[c].astype(o_ref.dtype)
            else:
                nat_ref[c] = x_ref[:, sl].astype(F32)
                for r in range(dil):
                    o_ref[r, :, sl] = nat_ref.at[c][pl.ds(r, n, stride=dil), :].astype(o_ref.dtype)

    return pl.pallas_call(
        body, name=name, grid=(S // ts,), in_specs=[sub if inverse else nat], out_specs=nat if inverse else sub,
        out_shape=jax.ShapeDtypeStruct((S, C) if inverse else (dil, S // dil, C), x.dtype),
        scratch_shapes=[pltpu.VMEM((NC, ts, 128), F32)], compiler_params=_params("parallel"),
    )(x)


def _to_sub(x, dil, *, name):
    return x if dil == 1 else _permute(x, dil, inverse=False, name=name).reshape(x.shape)


def _to_tok(x, dil, *, name):
    S, C = x.shape
    return x if dil == 1 else _permute(x.reshape(dil, S // dil, C), dil, inverse=True, name=name)


def _attn_fwd(qkv, qg, kg, *, grp, name):
    S, W = qkv.shape
    D = W // 3
    H = D // HEAD_DIM
    dil = DILATED_GROUPS[grp][1]
    L = S // dil
    nb = L // BLOCK
    slopes = _slopes(3 * H)[grp * H:(grp + 1) * H]
    scale = HEAD_DIM ** -0.5

    def body(q_ref, kp_ref, kc_ref, vp_ref, vc_ref, qg_ref, kg_ref, o_ref, l_ref):
        b = pl.program_id(1)
        valid, dist = _band(b, dil)
        l_ref[...] = jnp.zeros_like(l_ref)
        for h in range(H):
            hs = slice(h * HEAD_DIM, (h + 1) * HEAD_DIM)
            qn = _qk_norm(q_ref[:, hs].astype(F32), qg_ref[h:h + 1, :])[0].astype(MXU_DTYPE)
            kp = _qk_norm(kp_ref[:, hs].astype(F32), kg_ref[h:h + 1, :])[0].astype(MXU_DTYPE)
            kc = _qk_norm(kc_ref[:, hs].astype(F32), kg_ref[h:h + 1, :])[0].astype(MXU_DTYPE)
            k2 = jnp.concatenate([kp, kc], axis=0)
            v2 = jnp.concatenate([vp_ref[:, hs], vc_ref[:, hs]], axis=0).astype(MXU_DTYPE)
            s = lax.dot_general(qn, k2, (((1,), (1,)), ((), ())), preferred_element_type=F32) * scale - float(slopes[h]) * dist
            s = jnp.where(valid, s, NEG)
            m = jnp.max(s, axis=-1, keepdims=True)
            p = jnp.exp(s - m)
            den = jnp.sum(p, axis=-1, keepdims=True)
            o = jnp.dot(p.astype(MXU_DTYPE), v2, preferred_element_type=F32) / den
            o_ref[:, hs] = o.astype(o_ref.dtype)
            l_ref[:, h:h + 1] = m + jnp.log(den)

    def cur(j):
        return lambda r, b: (r * nb + b, j)

    def prv(j):
        return lambda r, b: (r * nb + jnp.maximum(b - 1, 0), j)

    blk = (BLOCK, D)
    gain = pl.BlockSpec((H, HEAD_DIM), lambda r, b: (0, 0))
    return pl.pallas_call(
        body, name=name, grid=(dil, nb),
        in_specs=[pl.BlockSpec(blk, cur(0)), pl.BlockSpec(blk, prv(1)), pl.BlockSpec(blk, cur(1)),
                  pl.BlockSpec(blk, prv(2)), pl.BlockSpec(blk, cur(2)), gain, gain],
        out_specs=[pl.BlockSpec(blk, cur(0)), pl.BlockSpec((BLOCK, HEAD_DIM), cur(0))],
        out_shape=[jax.ShapeDtypeStruct((S, D), MXU_DTYPE), jax.ShapeDtypeStruct((S, HEAD_DIM), F32)],
        compiler_params=_params("parallel", "parallel"),
    )(qkv, qkv, qkv, qkv, qkv, qg, kg)


def _attn_merge(os_, ls_, *, name):
    S, D = os_[0].shape
    H = D // HEAD_DIM
    ts = _pick(S, 512, 16)

    def body(o0, o1, o2, l0, l1, l2, outb_ref, lt_ref):
        lt_ref[...] = jnp.zeros_like(lt_ref)
        for h in range(H):
            hs = slice(h * HEAD_DIM, (h + 1) * HEAD_DIM)
            a, b_, c = l0[:, h:h + 1], l1[:, h:h + 1], l2[:, h:h + 1]
            m = jnp.maximum(jnp.maximum(a, b_), c)
            e0, e1, e2 = jnp.exp(a - m), jnp.exp(b_ - m), jnp.exp(c - m)
            den = e0 + e1 + e2
            out = (e0 * o0[:, hs].astype(F32) + e1 * o1[:, hs].astype(F32) + e2 * o2[:, hs].astype(F32)) / den
            outb_ref[:, hs] = out.astype(outb_ref.dtype)
            lt_ref[:, h:h + 1] = m + jnp.log(den)

    row = pl.BlockSpec((ts, D), lambda i: (i, 0))
    lrow = pl.BlockSpec((ts, HEAD_DIM), lambda i: (i, 0))
    return pl.pallas_call(
        body, name=name, grid=(S // ts,), in_specs=[row] * 3 + [lrow] * 3, out_specs=[row, lrow],
        out_shape=[jax.ShapeDtypeStruct((S, D), MXU_DTYPE), jax.ShapeDtypeStruct((S, HEAD_DIM), F32)],
        compiler_params=_params("parallel"),
    )(*os_, *ls_)


def _attn_delta(do, out, *, name):
    S, D = out.shape
    ts = _pick(S, 512, 16)

    def body(do_ref, o_ref, d_ref):
        d_ref[...] = jnp.zeros_like(d_ref)
        for h in range(D // HEAD_DIM):
            hs = slice(h * HEAD_DIM, (h + 1) * HEAD_DIM)
            d_ref[:, h:h + 1] = jnp.sum(do_ref[:, hs].astype(F32) * o_ref[:, hs].astype(F32), axis=-1, keepdims=True)

    row = pl.BlockSpec((ts, D), lambda i: (i, 0))
    return pl.pallas_call(
        body, name=name, grid=(S // ts,), in_specs=[row, row], out_specs=pl.BlockSpec((ts, HEAD_DIM), lambda i: (i, 0)),
        out_shape=jax.ShapeDtypeStruct((S, HEAD_DIM), F32), compiler_params=_params("parallel"),
    )(do, out)


def _attn_bwd(qkv, do, lse, delta, qg, kg, *, grp, name):
    S, W = qkv.shape
    D = W // 3
    H = D // HEAD_DIM
    dil = DILATED_GROUPS[grp][1]
    L = S // dil
    nb = L // BLOCK
    slopes = _slopes(3 * H)[grp * H:(grp + 1) * H]
    scale = HEAD_DIM ** -0.5

    def body(q_ref, qp_ref, kp_ref, kc_ref, vp_ref, vc_ref, do_ref, l_ref, dl_ref, qg_ref, kg_ref,
             out_ref, dqg_ref, dkg_ref, cq_ref, ck_ref, cv_ref, nq_ref, nk_ref, nv_ref, pk_ref, pv_ref):
        r = pl.program_id(0)
        b = pl.program_id(1)

        @pl.when(jnp.logical_and(r == 0, b == 0))
        def _():
            dqg_ref[...] = jnp.zeros_like(dqg_ref)
            dkg_ref[...] = jnp.zeros_like(dkg_ref)

        @pl.when(b < nb)
        def _():
            valid, dist = _band(b, dil)
            for h in range(H):
                hs = slice(h * HEAD_DIM, (h + 1) * HEAD_DIM)
                qn = _qk_norm(q_ref[:, hs].astype(F32), qg_ref[h:h + 1, :])[0].astype(MXU_DTYPE)
                kp = _qk_norm(kp_ref[:, hs].astype(F32), kg_ref[h:h + 1, :])[0].astype(MXU_DTYPE)
                kc = _qk_norm(kc_ref[:, hs].astype(F32), kg_ref[h:h + 1, :])[0].astype(MXU_DTYPE)
                k2 = jnp.concatenate([kp, kc], axis=0)
                v2 = jnp.concatenate([vp_ref[:, hs], vc_ref[:, hs]], axis=0).astype(MXU_DTYPE)
                doh = do_ref[:, hs].astype(MXU_DTYPE)
                s = lax.dot_general(qn, k2, (((1,), (1,)), ((), ())), preferred_element_type=F32) * scale - float(slopes[h]) * dist
                s = jnp.where(valid, s, NEG)
                p = jnp.exp(s - l_ref[:, h:h + 1])
                dp = lax.dot_general(doh, v2, (((1,), (1,)), ((), ())), preferred_element_type=F32)
                dsc = (p * (dp - dl_ref[:, h:h + 1]) * scale).astype(MXU_DTYPE)
                nq_ref[:, hs] = jnp.dot(dsc, k2, preferred_element_type=F32)
                dk2 = lax.dot_general(dsc, qn, (((0,), (0,)), ((), ())), preferred_element_type=F32)
                dv2 = lax.dot_general(p.astype(MXU_DTYPE), doh, (((0,), (0,)), ((), ())), preferred_element_type=F32)
                pk_ref[:, hs] = dk2[0:BLOCK]
                nk_ref[:, hs] = dk2[BLOCK:2 * BLOCK]
                pv_ref[:, hs] = dv2[0:BLOCK]
                nv_ref[:, hs] = dv2[BLOCK:2 * BLOCK]

        @pl.when(b == nb)
        def _():
            pk_ref[...] = jnp.zeros_like(pk_ref)
            pv_ref[...] = jnp.zeros_like(pv_ref)

        @pl.when(b > 0)
        def _():
            for h in range(H):
                hs = slice(h * HEAD_DIM, (h + 1) * HEAD_DIM)
                for j, (raw_ref, gain_ref, dgain_ref) in enumerate(((qp_ref, qg_ref, dqg_ref), (kp_ref, kg_ref, dkg_ref))):
                    dy = cq_ref[:, hs] if j == 0 else ck_ref[:, hs] + pk_ref[:, hs]
                    gain = gain_ref[h:h + 1, :]
                    _, xh, rr = _qk_norm(raw_ref[:, hs].astype(F32), gain)
                    gy = dy * gain
                    dx = rr * (gy - xh * jnp.mean(gy * xh, axis=-1, keepdims=True))
                    out_ref[:, j * D + h * HEAD_DIM:j * D + (h + 1) * HEAD_DIM] = dx.astype(out_ref.dtype)
                    dgain_ref[h:h + 1, :] += jnp.sum(dy * xh, axis=0, keepdims=True)
                out_ref[:, 2 * D + h * HEAD_DIM:2 * D + (h + 1) * HEAD_DIM] = (cv_ref[:, hs] + pv_ref[:, hs]).astype(out_ref.dtype)

        @pl.when(b < nb)
        def _():
            cq_ref[...] = nq_ref[...]
            ck_ref[...] = nk_ref[...]
            cv_ref[...] = nv_ref[...]

    def cur(j):
        return lambda r, b: (r * nb + jnp.minimum(b, nb - 1), j)

    def prv(j):
        return lambda r, b: (r * nb + jnp.clip(b - 1, 0, nb - 1), j)

    blk = (BLOCK, D)
    lblk = pl.BlockSpec((BLOCK, HEAD_DIM), cur(0))
    gain = pl.BlockSpec((H, HEAD_DIM), lambda r, b: (0, 0))
    return pl.pallas_call(
        body, name=name, grid=(dil, nb + 1),
        in_specs=[pl.BlockSpec(blk, cur(0)), pl.BlockSpec(blk, prv(0)), pl.BlockSpec(blk, prv(1)), pl.BlockSpec(blk, cur(1)),
                  pl.BlockSpec(blk, prv(2)), pl.BlockSpec(blk, cur(2)), pl.BlockSpec(blk, cur(0)), lblk, lblk, gain, gain],
        out_specs=[pl.BlockSpec((BLOCK, 3 * D), lambda r, b: (r * nb + jnp.maximum(b - 1, 0), 0)), gain, gain],
        out_shape=[jax.ShapeDtypeStruct((S, 3 * D), MXU_DTYPE), jax.ShapeDtypeStruct((H, HEAD_DIM), F32),
                   jax.ShapeDtypeStruct((H, HEAD_DIM), F32)],
        scratch_shapes=[pltpu.VMEM(blk, F32)] * 8,
        compiler_params=_params("arbitrary", "arbitrary"),
    )(qkv, qkv, qkv, qkv, qkv, qkv, do, lse, delta, qg, kg)


def _loss_head(y, target, *, name):
    S, D = y.shape
    ts = _pick(S, 512, 16)

    def body(y_ref, t_ref, dy_ref, dyb_ref, l_ref, acc_ref):
        i = pl.program_id(0)
        e = y_ref[...] - t_ref[...]
        dy = e * (1.0 / D)
        dy_ref[...] = dy
        dyb_ref[...] = dy.astype(dyb_ref.dtype)
        part = jnp.sum(e * e, axis=0, keepdims=True)

        @pl.when(i == 0)
        def _():
            acc_ref[...] = part

        @pl.when(i > 0)
        def _():
            acc_ref[...] += part

        @pl.when(i == pl.num_programs(0) - 1)
        def _():
            l_ref[...] = jnp.broadcast_to(jnp.sum(acc_ref[...], axis=-1, keepdims=True) * (0.5 / D), l_ref.shape)

    row = pl.BlockSpec((ts, D), lambda i: (i, 0))
    return pl.pallas_call(
        body, name=name, grid=(S // ts,), in_specs=[row, row],
        out_specs=[row, row, pl.BlockSpec((8, 128), lambda i: (0, 0))],
        out_shape=[jax.ShapeDtypeStruct((S, D), F32), jax.ShapeDtypeStruct((S, D), MXU_DTYPE), jax.ShapeDtypeStruct((8, 128), F32)],
        scratch_shapes=[pltpu.VMEM((1, D), F32)],
        compiler_params=_params("arbitrary"),
    )(y, target)


def _adamw(w, m, v, terms, slots, *, name):
    R, C = w.shape
    nt = len(terms)
    tr = _pick(R, 256, 16)
    c1 = 1.0 - ADAM_B1 ** ADAM_STEP
    c2 = 1.0 - ADAM_B2 ** ADAM_STEP

    def body(slot_ref, w_ref, m_ref, v_ref, *rest):
        t_refs = rest[:nt]
        g_ref, d_ref, nm_ref, nv_ref = rest[nt:]
        g = t_refs[0][...].astype(F32)
        for t in t_refs[1:]:
            g = g + t[...].astype(F32)
        mm = ADAM_B1 * m_ref[...] + (1.0 - ADAM_B1) * g
        vv = ADAM_B2 * v_ref[...] + (1.0 - ADAM_B2) * (g * g)
        m_hat = mm / c1
        v_hat = vv / c2
        g_ref[...] = g
        d_ref[...] = -ADAM_LR * (m_hat / (jnp.sqrt(v_hat) + ADAM_EPS) + ADAM_WD * w_ref[...])
        nm_ref[...] = mm
        nv_ref[...] = vv

    row = pl.BlockSpec((tr, C), lambda i, s: (i, 0))
    grid_spec = pltpu.PrefetchScalarGridSpec(
        num_scalar_prefetch=1, grid=(R // tr,),
        in_specs=[row, row, row] + [pl.BlockSpec((None, tr, C), lambda i, s, t=t: (s[t], i, 0)) for t in range(nt)],
        out_specs=[row] * 4)
    return pl.pallas_call(
        body, name=name, grid_spec=grid_spec, out_shape=[jax.ShapeDtypeStruct((R, C), F32)] * 4,
        compiler_params=_params("parallel"),
    )(slots, w, m, v, *terms)


def _chip_partials(g, sib, core, *, name):
    _, R, C = g.shape
    tr = _pick(R, 512, 16)

    def body(core_ref, g_ref, s_ref, o_ref):
        o_ref[...] = (g_ref[...] + s_ref[...].astype(F32)).astype(o_ref.dtype)

    grid_spec = pltpu.PrefetchScalarGridSpec(
        num_scalar_prefetch=1, grid=(4, R // tr),
        in_specs=[pl.BlockSpec((None, tr, C), lambda k, i, c: (2 * k + c[0], i, 0)),
                  pl.BlockSpec((None, tr, C), lambda k, i, c: (k, i, 0))],
        out_specs=pl.BlockSpec((None, tr, C), lambda k, i, c: (k, i, 0)))
    return pl.pallas_call(
        body, name=name, grid_spec=grid_spec, out_shape=jax.ShapeDtypeStruct((4, R, C), sib.dtype),
        compiler_params=_params("parallel", "parallel"),
    )(core, g, sib)


_ANY = pl.BlockSpec(memory_space=pl.ANY)


def _place():
    return lax.axis_index("x"), lax.axis_index("y"), lax.axis_index("c")


def _all_gather(shard, *, name):
    R, C = shard.shape

    def body(x_ref, out_ref, send_sems, recv_sems, local_sem):
        x, y, c = _place()
        me, sibling = (x, y, c), (x, y, 1 - c)
        chips = [(1 - x, y), (x, 1 - y), (1 - x, 1 - y)]

        def slot(px, py, pc):
            return out_ref.at[4 * px + 2 * py + pc]

        def copy(k, block, to, src=None):
            return pltpu.make_async_remote_copy(
                src_ref=slot(*block) if src is None else src, dst_ref=slot(*block),
                send_sem=send_sems.at[k], recv_sem=recv_sems.at[k], device_id=to, device_id_type=MESH)

        mine = pltpu.make_async_copy(x_ref, slot(*me), local_sem)
        mine.start()
        first = [copy(0, me, sibling, src=x_ref)]
        first += [copy(1 + j, me, (*chip, c), src=x_ref) for j, chip in enumerate(chips)]
        for cp in first:
            cp.start()
        passed = [copy(4 + j, (*chip, c), sibling) for j, chip in enumerate(chips)]
        for j, chip in enumerate(chips):
            copy(1 + j, (*chip, c), me).wait_recv()
            passed[j].start()
        copy(0, sibling, me).wait_recv()
        for j, chip in enumerate(chips):
            copy(4 + j, (*chip, 1 - c), me).wait_recv()
        for cp in first + passed:
            cp.wait_send()
        mine.wait()

    return pl.pallas_call(
        body, name=name, in_specs=[_ANY], out_specs=_ANY,
        out_shape=jax.ShapeDtypeStruct((N_DEV, R, C), shard.dtype),
        scratch_shapes=[pltpu.SemaphoreType.DMA((7,)), pltpu.SemaphoreType.DMA((7,)), pltpu.SemaphoreType.DMA],
    )(shard)


def _rs_sibling(g, *, name):
    _, R, C = g.shape

    def body(g_ref, sib_ref, send_sems, recv_sems):
        x, y, c = _place()
        sends = [pltpu.make_async_remote_copy(
            src_ref=g_ref.at[2 * k + (1 - c)], dst_ref=sib_ref.at[k], send_sem=send_sems.at[k], recv_sem=recv_sems.at[k],
            device_id=(x, y, 1 - c), device_id_type=MESH) for k in range(4)]
        for cp in sends:
            cp.start()
        for cp in sends:
            cp.wait_recv()
        for cp in sends:
            cp.wait_send()

    return pl.pallas_call(
        body, name=name, in_specs=[_ANY], out_specs=_ANY, out_shape=jax.ShapeDtypeStruct((4, R, C), g.dtype),
        scratch_shapes=[pltpu.SemaphoreType.DMA((4,)), pltpu.SemaphoreType.DMA((4,))],
    )(g)


def _rs_chips(part, *, name):
    _, R, C = part.shape

    def body(p_ref, out_ref, send_sems, recv_sems):
        x, y, c = _place()
        chips = [(1 - x, y), (x, 1 - y), (1 - x, 1 - y)]
        sends = [pltpu.make_async_remote_copy(
            src_ref=p_ref.at[2 * px + py], dst_ref=out_ref.at[j], send_sem=send_sems.at[j], recv_sem=recv_sems.at[j],
            device_id=(px, py, c), device_id_type=MESH) for j, (px, py) in enumerate(chips)]
        for cp in sends:
            cp.start()
        for cp in sends:
            cp.wait_recv()
        for cp in sends:
            cp.wait_send()

    return pl.pallas_call(
        body, name=name, in_specs=[_ANY], out_specs=_ANY, out_shape=jax.ShapeDtypeStruct((3, R, C), part.dtype),
        scratch_shapes=[pltpu.SemaphoreType.DMA((3,)), pltpu.SemaphoreType.DMA((3,))],
    )(part)


def _interleave_rows(wt):
    F2, D = wt.shape
    return wt.reshape(2, F2 // 256, 128, D).transpose(1, 0, 2, 3).reshape(F2, D)


def _deinterleave_rows(wt):
    F2, D = wt.shape
    return wt.reshape(F2 // 256, 2, 128, D).transpose(1, 0, 2, 3).reshape(F2, D)


def _interleave_cols(v):
    k, F2 = v.shape
    return v.reshape(k, 2, F2 // 256, 128).transpose(0, 2, 1, 3).reshape(k, F2)


def _deinterleave_cols(v):
    k, F2 = v.shape
    return v.reshape(k, F2 // 256, 2, 128).transpose(0, 2, 1, 3).reshape(k, F2)


def _pack_rows(parts):
    return jnp.concatenate(parts, axis=0)


def _flat_pack(parts, width):
    flat = jnp.concatenate([p.reshape(-1) for p in parts])
    pad = (-flat.shape[0]) % (8 * width)
    return jnp.pad(flat, (0, pad)).reshape(-1, width)


def _flat_unpack(packed, shapes):
    flat = packed.reshape(-1)
    out, off = [], 0
    for shp in shapes:
        n = int(np.prod(shp))
        out.append(flat[off:off + n].reshape(shp))
        off += n
    return out


def _ffn_forward(x, g_ffn, wupT, wdown, dw_i, dwb_i, tag):
    hf = _rms_fwd(x, g_ffn, name=f"ffn{tag}_rms")
    up = _mm(hf, wupT, mode="nt", out_dtype=MXU_DTYPE, name=f"ffn{tag}_up", tn=512)
    act = _ffn_act_fwd(up, dw_i, dwb_i, name=f"ffn{tag}_act")
    y = _mm(act, wdown, mode="nn", out_dtype=F32, name=f"ffn{tag}_down", residual=x)
    return y, (hf, up, act)


def _ffn_backward(x, g_ffn, wupT, wdown, dw_i, dwb_i, saved, dy, dyb, tag):
    hf, up, act = saved
    dact = _mm(dyb, wdown, mode="nt", out_dtype=MXU_DTYPE, name=f"ffn{tag}_dact", tm=512, tn=1408)
    d_wdown = _mm(act, dyb, mode="tn", out_dtype=F32, name=f"ffn{tag}_dwdown", tm=1408, tk=2048)
    dup, d_dw_i, d_dwb_i = _ffn_act_bwd(up, dact, dw_i, dwb_i, name=f"ffn{tag}_actbwd")
    dhf = _mm(dup, wupT, mode="nn", out_dtype=F32, name=f"ffn{tag}_dhf", tm=512)
    d_wupT = _mm(dup, hf, mode="tn", out_dtype=F32, name=f"ffn{tag}_dwup", tm=512, tk=2048)
    dx, dxb, dg, cs = _rms_bwd(x, g_ffn, [dhf], dy, name=f"ffn{tag}_rmsbwd")
    return dx, dxb, cs, dict(w_upT=d_wupT, w_down=d_wdown, dw=d_dw_i[0:FFN_KERNEL], dw_b=d_dwb_i, norm=dg)


def _local_step(x, target, p):
    S, D = x.shape
    H = D // HEAD_DIM
    h0 = _rms_fwd(x, p["norm_mix"][0:1], name="l0_rms")
    u = _mm(h0, p["w_inT"], mode="nt", out_dtype=F32, name="l0_in", bias=p["cm_b_in"])
    c, s = _cm_fwd(u, p["cm_dw"], p["cm_dw_b"], p["cm_ln_g"], p["cm_ln_b"], name="l0_conv")
    x1 = _mm(s, p["w_out"], mode="nn", out_dtype=F32, name="l0_out", bias=p["cm_b_out"], residual=x)
    x2, sv0 = _ffn_forward(x1, p["norm_ffn"][0:1], p["w_upT"][0], p["w_down"][0], p["ff_dw"][0], p["ff_dw_b"][0:1], 0)
    h1 = _rms_fwd(x2, p["norm_mix"][1:2], name="l1_rms")
    dils = [dil for _, dil in DILATED_GROUPS]
    h1s, qkvs, os_, ls_ = [], [], [], []
    for g, dil in enumerate(dils):
        h1s.append(_to_sub(h1, dil, name=f"l1_h_sub{g}"))
        qkvs.append(_mm(h1s[g], p["w_qkvT"], mode="nt", out_dtype=MXU_DTYPE, name=f"l1_qkv{g}", b_off=g * 3 * D, b_len=3 * D))
        o, l = _attn_fwd(qkvs[g], p["at_q_norm"][g * H:(g + 1) * H], p["at_k_norm"][g * H:(g + 1) * H], grp=g, name=f"l1_attn{g}")
        os_.append(_to_tok(o, dil, name=f"l1_o_tok{g}"))
        ls_.append(_to_tok(l, dil, name=f"l1_lse_tok{g}"))
    outb, lse = _attn_merge(os_, ls_, name="l1_merge")
    x3 = _mm(outb, p["w_o"], mode="nn", out_dtype=F32, name="l1_o", residual=x2)
    x4, sv1 = _ffn_forward(x3, p["norm_ffn"][1:2], p["w_upT"][1], p["w_down"][1], p["ff_dw"][1], p["ff_dw_b"][1:2], 1)
    dx4, dx4b, loss = _loss_head(x4, target, name="loss")
    dx3, dx3b, _, gf1 = _ffn_backward(x3, p["norm_ffn"][1:2], p["w_upT"][1], p["w_down"][1], p["ff_dw"][1], p["ff_dw_b"][1:2],
                                      sv1, dx4, dx4b, 1)
    do = _mm(dx3b, p["w_o"], mode="nt", out_dtype=MXU_DTYPE, name="l1_do")
    d_wo = _mm(outb, dx3b, mode="tn", out_dtype=F32, name="l1_dwo", tk=2048)
    delta = _attn_delta(do, outb, name="l1_delta")
    dh1s, d_wqkvT, dqg, dkg = [], [], [], []
    for g, dil in enumerate(dils):
        dqkv_g, a, b_ = _attn_bwd(qkvs[g], _to_sub(do, dil, name=f"l1_do_sub{g}"), _to_sub(lse, dil, name=f"l1_lse_sub{g}"),
                                  _to_sub(delta, dil, name=f"l1_delta_sub{g}"), p["at_q_norm"][g * H:(g + 1) * H],
                                  p["at_k_norm"][g * H:(g + 1) * H], grp=g, name=f"l1_attnbwd{g}")
        dqg.append(a)
        dkg.append(b_)
        d_wqkvT.append(_mm(dqkv_g, h1s[g], mode="tn", out_dtype=F32, name=f"l1_dwqkv{g}", tk=2048))
        dh_g = _mm(dqkv_g, p["w_qkvT"], mode="nn", out_dtype=F32, name=f"l1_dh{g}", b_off=g * 3 * D, b_len=3 * D)
        dh1s.append(_to_tok(dh_g, dil, name=f"l1_dh_tok{g}"))
    dx2, dx2b, dgm1, _ = _rms_bwd(x2, p["norm_mix"][1:2], dh1s, dx3, name="l1_rmsbwd")
    dx1, dx1b, cs1, gf0 = _ffn_backward(x1, p["norm_ffn"][0:1], p["w_upT"][0], p["w_down"][0], p["ff_dw"][0], p["ff_dw_b"][0:1],
                                        sv0, dx2, dx2b, 0)
    ds = _mm(dx1b, p["w_out"], mode="nt", out_dtype=F32, name="l0_ds")
    d_wout = _mm(s, dx1b, mode="tn", out_dtype=F32, name="l0_dwout", tk=2048)
    dc, d_lng, d_lnb = _cm_ln_bwd(c, ds, p["cm_ln_g"], p["cm_ln_b"], name="l0_lnbwd")
    du, d_cmdw, d_cmdwb, d_bin = _cm_conv_bwd(dc, u, p["cm_dw"], name="l0_convbwd")
    dh0 = _mm(du, p["w_inT"], mode="nn", out_dtype=F32, name="l0_dh")
    d_winT = _mm(du, h0, mode="tn", out_dtype=F32, name="l0_dwin", tk=2048)
    grad_x, _, dgm0, _ = _rms_bwd(x, p["norm_mix"][0:1], [dh0], dx1, name="l0_rmsbwd")
    grads = dict(
        norm_mix=jnp.concatenate([dgm0, dgm1], axis=0),
        norm_ffn=jnp.concatenate([gf0["norm"], gf1["norm"]], axis=0),
        w_inT=d_winT, cm_b_in=d_bin, cm_dw=d_cmdw[0:CONV_KERNEL], cm_dw_b=d_cmdwb, cm_ln_g=d_lng, cm_ln_b=d_lnb,
        w_out=d_wout, cm_b_out=cs1,
        w_qkvT=jnp.concatenate(d_wqkvT, axis=0), at_q_norm=jnp.concatenate(dqg, axis=0), at_k_norm=jnp.concatenate(dkg, axis=0),
        w_o=d_wo,
        w_upT=[gf0["w_upT"], gf1["w_upT"]], w_down=[gf0["w_down"], gf1["w_down"]],
        ff_dw=jnp.stack([gf0["dw"], gf1["dw"]]), ff_dw_b=jnp.concatenate([gf0["dw_b"], gf1["dw_b"]], axis=0),
    )
    return loss, grad_x, grads


_BIG = ("cm_w_in", "cm_w_out", "at_w_qkv", "at_w_out", "ff_w_up", "ff_w_down")
_TRANSPOSED = ("cm_w_in", "at_w_qkv", "ff_w_up")
_SMALL = ("norm_mix", "norm_ffn", "cm_b_in", "cm_dw_b", "cm_ln_g", "cm_ln_b", "cm_b_out", "at_q_norm", "at_k_norm",
          "ff_dw_b", "cm_dw", "ff_dw")
_SMALL_SHARDED = ("cm_dw", "ff_dw")
_ORDER = ("norm_mix", "norm_ffn", "cm_w_in", "cm_b_in", "cm_dw", "cm_dw_b", "cm_ln_g", "cm_ln_b", "cm_w_out", "cm_b_out",
          "at_w_qkv", "at_q_norm", "at_k_norm", "at_w_out", "ff_w_up", "ff_dw", "ff_dw_b", "ff_w_down")


def _big_rows(t):
    parts = []
    for n in _BIG:
        a = t[n]
        mats = [a[l] for l in range(a.shape[0])]
        if n in _TRANSPOSED:
            mats = [m.T for m in mats]
        parts += mats
    return _pack_rows(parts)


def _big_unrows(packed, like):
    out, off = {}, 0
    for n in _BIG:
        a = like[n]
        mats = []
        for l in range(a.shape[0]):
            rows, cols = (a.shape[2], a.shape[1]) if n in _TRANSPOSED else (a.shape[1], a.shape[2])
            m = packed[off:off + rows]
            off += rows
            mats.append(m.T if n in _TRANSPOSED else m)
        out[n] = jnp.stack(mats)
    return out


def kernel(x, norm_mix, norm_ffn, cm_w_in, cm_b_in, cm_dw, cm_dw_b, cm_ln_g, cm_ln_b, cm_w_out, cm_b_out, at_w_qkv, at_q_norm, at_k_norm, at_w_out, ff_w_up, ff_dw, ff_dw_b, ff_w_down, loss_target, m_norm_mix, m_norm_ffn, m_cm_w_in, m_cm_b_in, m_cm_dw, m_cm_dw_b, m_cm_ln_g, m_cm_ln_b, m_cm_w_out, m_cm_b_out, m_at_w_qkv, m_at_q_norm, m_at_k_norm, m_at_w_out, m_ff_w_up, m_ff_dw, m_ff_dw_b, m_ff_w_down, v_norm_mix, v_norm_ffn, v_cm_w_in, v_cm_b_in, v_cm_dw, v_cm_dw_b, v_cm_ln_g, v_cm_ln_b, v_cm_w_out, v_cm_b_out, v_at_w_qkv, v_at_q_norm, v_at_k_norm, v_at_w_out, v_ff_w_up, v_ff_dw, v_ff_dw_b, v_ff_w_down):
    w = dict(norm_mix=norm_mix, norm_ffn=norm_ffn, cm_w_in=cm_w_in, cm_b_in=cm_b_in, cm_dw=cm_dw, cm_dw_b=cm_dw_b, cm_ln_g=cm_ln_g,
             cm_ln_b=cm_ln_b, cm_w_out=cm_w_out, cm_b_out=cm_b_out, at_w_qkv=at_w_qkv, at_q_norm=at_q_norm, at_k_norm=at_k_norm,
             at_w_out=at_w_out, ff_w_up=ff_w_up, ff_dw=ff_dw, ff_dw_b=ff_dw_b, ff_w_down=ff_w_down)
    m = dict(norm_mix=m_norm_mix, norm_ffn=m_norm_ffn, cm_w_in=m_cm_w_in, cm_b_in=m_cm_b_in, cm_dw=m_cm_dw, cm_dw_b=m_cm_dw_b,
             cm_ln_g=m_cm_ln_g, cm_ln_b=m_cm_ln_b, cm_w_out=m_cm_w_out, cm_b_out=m_cm_b_out, at_w_qkv=m_at_w_qkv,
             at_q_norm=m_at_q_norm, at_k_norm=m_at_k_norm, at_w_out=m_at_w_out, ff_w_up=m_ff_w_up, ff_dw=m_ff_dw,
             ff_dw_b=m_ff_dw_b, ff_w_down=m_ff_w_down)
    v = dict(norm_mix=v_norm_mix, norm_ffn=v_norm_ffn, cm_w_in=v_cm_w_in, cm_b_in=v_cm_b_in, cm_dw=v_cm_dw, cm_dw_b=v_cm_dw_b,
             cm_ln_g=v_cm_ln_g, cm_ln_b=v_cm_ln_b, cm_w_out=v_cm_w_out, cm_b_out=v_cm_b_out, at_w_qkv=v_at_w_qkv,
             at_q_norm=v_at_q_norm, at_k_norm=v_at_k_norm, at_w_out=v_at_w_out, ff_w_up=v_ff_w_up, ff_dw=v_ff_dw,
             ff_dw_b=v_ff_dw_b, ff_w_down=v_ff_w_down)
    S, D = x.shape[1], x.shape[2]
    F2 = ff_dw_b.shape[1]
    H3 = at_q_norm.shape[1]
    me = 4 * lax.axis_index("x") + 2 * lax.axis_index("y") + lax.axis_index("c")

    w_rows = _big_rows(w)
    gathered = _all_gather(w_rows.astype(MXU_DTYPE), name="gather_weights")
    full, off = {}, 0
    for n in _BIG:
        a = w[n]
        mats = []
        for l in range(a.shape[0]):
            rows = a.shape[2] if n in _TRANSPOSED else a.shape[1]
            mats.append(gathered[:, off:off + rows, :].reshape(N_DEV * rows, D))
            off += rows
        full[n] = mats
    small_sh = _flat_pack([cm_dw, ff_dw], D)
    small_g = _all_gather(small_sh, name="gather_small")
    cm_dw_full = jnp.concatenate([_flat_unpack(small_g[j], [cm_dw.shape, ff_dw.shape])[0][0] for j in range(N_DEV)], axis=-1)
    ff_dw_full = jnp.concatenate([_flat_unpack(small_g[j], [cm_dw.shape, ff_dw.shape])[1] for j in range(N_DEV)], axis=-1)

    p = dict(
        norm_mix=norm_mix, norm_ffn=norm_ffn, cm_b_in=cm_b_in, cm_dw=cm_dw_full, cm_dw_b=cm_dw_b, cm_ln_g=cm_ln_g, cm_ln_b=cm_ln_b,
        cm_b_out=cm_b_out, at_q_norm=at_q_norm[0], at_k_norm=at_k_norm[0],
        w_inT=full["cm_w_in"][0], w_out=full["cm_w_out"][0], w_qkvT=full["at_w_qkv"][0], w_o=full["at_w_out"][0],
        w_upT=[_interleave_rows(t) for t in full["ff_w_up"]], w_down=full["ff_w_down"],
        ff_dw=jnp.stack([_interleave_cols(ff_dw_full[l]) for l in range(ff_dw_full.shape[0])]),
        ff_dw_b=_interleave_cols(ff_dw_b),
    )
    loss8, grad_x, g = _local_step(x[0], loss_target[0], p)
    loss = lax.psum(loss8[0, 0], ("x", "y", "c"))

    pieces = [g["w_inT"], g["w_out"], g["w_qkvT"], g["w_o"], _deinterleave_rows(g["w_upT"][0]), _deinterleave_rows(g["w_upT"][1]),
              g["w_down"][0], g["w_down"][1]]
    g_rows = jnp.concatenate([t.reshape(N_DEV, t.shape[0] // N_DEV, D) for t in pieces], axis=1)
    ix, iy, ic = lax.axis_index("x"), lax.axis_index("y"), lax.axis_index("c")
    chip = 2 * ix + iy
    sib = _rs_sibling(g_rows.astype(WIRE_DTYPE), name="reduce_sibling")
    part = _chip_partials(g_rows, sib, jnp.stack([ic]).astype(jnp.int32), name="reduce_add")
    recv = _rs_chips(part, name="reduce_chips")
    slots = jnp.stack([me, chip, 0 * me, 0 * me + 1, 0 * me + 2]).astype(jnp.int32)
    gb, db, mb, vb = _adamw(w_rows, _big_rows(m), _big_rows(v), [g_rows, sib, recv, recv, recv], slots, name="adamw_big")
    big = [_big_unrows(t, w) for t in (gb, db, mb, vb)]

    g_small = dict(g)
    g_small["cm_b_in"] = g["cm_b_in"]
    g_small["at_q_norm"] = g["at_q_norm"][None]
    g_small["at_k_norm"] = g["at_k_norm"][None]
    g_small["ff_dw_b"] = _deinterleave_cols(g["ff_dw_b"])
    g_small["cm_dw"] = g["cm_dw"][None]
    g_small["ff_dw"] = jnp.stack([_deinterleave_cols(g["ff_dw"][l]) for l in range(g["ff_dw"].shape[0])])
    small_shapes = [g_small[n].shape for n in _SMALL]
    gs_parts = _all_gather(_flat_pack([g_small[n] for n in _SMALL], D), name="gather_small_grads")

    def embed(t, n):
        if n not in _SMALL_SHARDED:
            return t
        full_shape = t.shape[:-1] + (t.shape[-1] * N_DEV,)
        return lax.dynamic_update_slice_in_dim(jnp.zeros(full_shape, F32), t, me * t.shape[-1], axis=t.ndim - 1)

    packs = [_flat_pack([embed(tree[n], n) for n in _SMALL], D) for tree in (w, m, v)]
    gs, ds_, ms, vs = _adamw(packs[0], packs[1], packs[2], [gs_parts] * N_DEV, jnp.arange(N_DEV, dtype=jnp.int32),
                             name="adamw_small")
    small = []
    for t in (gs, ds_, ms, vs):
        un = dict(zip(_SMALL, _flat_unpack(t, small_shapes)))
        for n in _SMALL_SHARDED:
            width = w[n].shape[-1]
            un[n] = lax.dynamic_slice_in_dim(un[n], me * width, width, axis=un[n].ndim - 1)
        small.append({n: un[n].reshape(w[n].shape) for n in _SMALL})

    outs = [loss, grad_x[None]]
    for k in range(4):
        for n in _ORDER:
            outs.append(big[k][n] if n in _BIG else small[k][n])
    return tuple(outs)
```

```python
import functools

import jax
import jax.numpy as jnp
import numpy as np
from jax import lax
from jax.experimental import pallas as pl
from jax.experimental.pallas import tpu as pltpu

F32 = jnp.float32
MXU_DTYPE = jnp.bfloat16
WIRE_DTYPE = jnp.bfloat16
EPS = 1e-6
NEG = -1e30
HEAD_DIM = 128
BLOCK = 128
DILATED_GROUPS = ((128, 1), (512, 4), (2048, 16))
ALIBI_MAX = 8.0
CONV_KERNEL = 31
CONV_HALO = 32
CONV_ROWS = 64
FFN_KERNEL = 3
FFN_HALO = 16
FFN_ROWS = 64
ADAM_LR, ADAM_B1, ADAM_B2, ADAM_EPS, ADAM_WD, ADAM_STEP = 0.001, 0.9, 0.999, 1e-08, 0.01, 10
V7X_VMEM_BYTES = 64 * 1024 * 1024
VMEM_LIMIT = V7X_VMEM_BYTES * 3 // 4
N_DEV = 8
MESH = pl.DeviceIdType.MESH


def _pick(n, target, align):
    if n <= target:
        return n
    best = None
    for t in range(align, target + 1, align):
        if n % t == 0:
            best = t
    assert best is not None, (n, target, align)
    return best


def _params(*sem):
    return pltpu.CompilerParams(dimension_semantics=sem, vmem_limit_bytes=VMEM_LIMIT)


def _sigmoid(x):
    return 1.0 / (1.0 + jnp.exp(-x))


_DIMS = {"nn": ((1,), (0,)), "nt": ((1,), (1,)), "tn": ((0,), (0,))}


def _mm(a, b, *, mode, out_dtype, name, tm=1024, tn=1024, tk=None, bias=None, residual=None, b_off=0, b_len=None):
    if mode == "tn":
        K, M = a.shape
    else:
        M, K = a.shape
    if mode == "nt":
        N = b.shape[0] if b_len is None else b_len
    else:
        N = b.shape[1]
    if b_len is not None:
        assert mode == "nt" or (mode == "nn" and K == b_len)
    tm = _pick(M, tm, 128 if mode == "tn" else 16)
    tn = _pick(N, tn, 128)
    tk = K if tk is None else _pick(K, tk, 128 if mode != "tn" else 16)
    nk = K // tk
    unit = tn if mode == "nt" else tk
    assert b_off % unit == 0
    kb0 = b_off // unit
    if mode == "tn":
        a_spec = pl.BlockSpec((tk, tm), lambda i, j, k: (k, i))
    else:
        a_spec = pl.BlockSpec((tm, tk), lambda i, j, k: (i, k))
    if mode == "nt":
        b_spec = pl.BlockSpec((tn, tk), lambda i, j, k: (j + kb0, k))
    else:
        b_spec = pl.BlockSpec((tk, tn), lambda i, j, k: (k + kb0, j))
    in_specs = [a_spec, b_spec]
    args = [a, b]
    if bias is not None:
        in_specs.append(pl.BlockSpec((1, tn), lambda i, j, k: (0, j)))
        args.append(bias)
    if residual is not None:
        in_specs.append(pl.BlockSpec((tm, tn), lambda i, j, k: (i, j)))
        args.append(residual)
    has_bias, has_res = bias is not None, residual is not None

    def body(*refs):
        a_ref, b_ref = refs[0], refs[1]
        pos = 2
        bias_ref = res_ref = None
        if has_bias:
            bias_ref = refs[pos]
            pos += 1
        if has_res:
            res_ref = refs[pos]
            pos += 1
        o_ref = refs[pos]
        acc_ref = refs[pos + 1] if nk > 1 else None

        def finish(acc):
            if has_bias:
                acc = acc + bias_ref[...]
            if has_res:
                acc = acc + res_ref[...]
            o_ref[...] = acc.astype(o_ref.dtype)

        part = lax.dot_general(a_ref[...].astype(MXU_DTYPE), b_ref[...].astype(MXU_DTYPE), (_DIMS[mode], ((), ())),
                               preferred_element_type=F32)
        if nk == 1:
            finish(part)
        else:
            k = pl.program_id(2)

            @pl.when(k == 0)
            def _():
                acc_ref[...] = part

            @pl.when(jnp.logical_and(k > 0, k < nk - 1))
            def _():
                acc_ref[...] += part

            @pl.when(k == nk - 1)
            def _():
                finish(acc_ref[...] + part)

    return pl.pallas_call(
        body, name=name, grid=(M // tm, N // tn, nk), in_specs=in_specs,
        out_specs=pl.BlockSpec((tm, tn), lambda i, j, k: (i, j)),
        out_shape=jax.ShapeDtypeStruct((M, N), out_dtype),
        scratch_shapes=[pltpu.VMEM((tm, tn), F32)] if nk > 1 else [],
        compiler_params=_params("parallel", "parallel", "arbitrary"),
    )(*args)


def _rms_fwd(x, g, *, name):
    S, D = x.shape
    ts = _pick(S, 512, 16)

    def body(x_ref, g_ref, h_ref):
        xv = x_ref[...]
        r = lax.rsqrt(jnp.mean(xv * xv, axis=-1, keepdims=True) + EPS)
        h_ref[...] = (xv * r * g_ref[...]).astype(h_ref.dtype)

    return pl.pallas_call(
        body, name=name, grid=(S // ts,),
        in_specs=[pl.BlockSpec((ts, D), lambda i: (i, 0)), pl.BlockSpec((1, D), lambda i: (0, 0))],
        out_specs=pl.BlockSpec((ts, D), lambda i: (i, 0)),
        out_shape=jax.ShapeDtypeStruct((S, D), MXU_DTYPE),
        compiler_params=_params("parallel"),
    )(x, g)


def _rms_bwd(x, g, dhs, dres, *, name):
    S, D = x.shape
    ts = _pick(S, 512, 16)
    n_dh = len(dhs)

    def body(*refs):
        x_ref, g_ref = refs[0], refs[1]
        dh_refs = refs[2:2 + n_dh]
        dres_ref, dx_ref, dxb_ref, dg_ref, cs_ref = refs[2 + n_dh:]
        i = pl.program_id(0)
        xv = x_ref[...]
        r = lax.rsqrt(jnp.mean(xv * xv, axis=-1, keepdims=True) + EPS)
        xh = xv * r
        dhv = dh_refs[0][...].astype(F32)
        for t in dh_refs[1:]:
            dhv = dhv + t[...].astype(F32)
        gy = dhv * g_ref[...]
        dx = r * (gy - xh * jnp.mean(gy * xh, axis=-1, keepdims=True)) + dres_ref[...]
        dx_ref[...] = dx
        dxb_ref[...] = dx.astype(dxb_ref.dtype)
        dg = jnp.sum(dhv * xh, axis=0, keepdims=True)
        cs = jnp.sum(dx, axis=0, keepdims=True)

        @pl.when(i == 0)
        def _():
            dg_ref[...] = dg
            cs_ref[...] = cs

        @pl.when(i > 0)
        def _():
            dg_ref[...] += dg
            cs_ref[...] += cs

    row = pl.BlockSpec((ts, D), lambda i: (i, 0))
    vec = pl.BlockSpec((1, D), lambda i: (0, 0))
    return pl.pallas_call(
        body, name=name, grid=(S // ts,),
        in_specs=[row, vec] + [row] * (n_dh + 1),
        out_specs=[row, row, vec, vec],
        out_shape=[jax.ShapeDtypeStruct((S, D), F32), jax.ShapeDtypeStruct((S, D), MXU_DTYPE),
                   jax.ShapeDtypeStruct((1, D), F32), jax.ShapeDtypeStruct((1, D), F32)],
        compiler_params=_params("arbitrary"),
    )(x, g, *dhs, dres)


def _conv_phases(ph_ref, ts):
    n = ts + CONV_HALO - 8
    for b in range(1, 8):
        ph_ref[b, 0:n, :] = ph_ref[0, pl.ds(b, n), :]


def _phase_taps(base, step=1):
    groups = {}
    for k in range(CONV_KERNEL):
        a, b = divmod(base + step * k, 8)
        groups.setdefault(b, []).append((a, k))
    out = []
    for b in sorted(groups):
        ak = sorted(groups[b])
        assert [a for a, _ in ak] == list(range(ak[0][0], ak[0][0] + len(ak)))
        out.append((b, ak[0][0], [k for _, k in ak]))
    return out


def _cm_fwd(u, dw, dw_b, ln_g, ln_b, *, name):
    S, D2 = u.shape
    D = D2 // 2
    ts = _pick(S, 256, CONV_HALO)
    hb = ts // CONV_HALO

    def body(u_ref, up_ref, dw_ref, dwb_ref, g_ref, b_ref, c_ref, s_ref, ext_ref):
        i = pl.program_id(0)
        prev = up_ref[:, :D] * _sigmoid(up_ref[:, D:])
        ext_ref[0:CONV_HALO, :] = jnp.where(i > 0, prev, 0.0)
        ext_ref[CONV_HALO:CONV_HALO + ts, :] = u_ref[:, :D] * _sigmoid(u_ref[:, D:])
        for cc in range(D // 128):
            sl = slice(cc * 128, (cc + 1) * 128)
            acc = jnp.zeros((ts, 128), F32) + dwb_ref[:, sl]
            for k in range(CONV_KERNEL):
                acc = acc + dw_ref[k:k + 1, sl] * ext_ref[pl.ds(CONV_HALO - (CONV_KERNEL - 1) + k, ts), sl]
            c_ref[:, sl] = acc
        c = c_ref[...]
        mu = jnp.mean(c, axis=-1, keepdims=True)
        xc = c - mu
        rstd = lax.rsqrt(jnp.mean(xc * xc, axis=-1, keepdims=True) + EPS)
        y = xc * rstd * g_ref[...] + b_ref[...]
        s_ref[...] = (y * _sigmoid(y)).astype(s_ref.dtype)

    vec = pl.BlockSpec((1, D), lambda i: (0, 0))
    return pl.pallas_call(
        body, name=name, grid=(S // ts,),
        in_specs=[pl.BlockSpec((ts, D2), lambda i: (i, 0)),
                  pl.BlockSpec((CONV_HALO, D2), lambda i: (jnp.maximum(i * hb - 1, 0), 0)),
                  pl.BlockSpec((CONV_KERNEL, D), lambda i: (0, 0)), vec, vec, vec],
        out_specs=[pl.BlockSpec((ts, D), lambda i: (i, 0)), pl.BlockSpec((ts, D), lambda i: (i, 0))],
        out_shape=[jax.ShapeDtypeStruct((S, D), F32), jax.ShapeDtypeStruct((S, D), MXU_DTYPE)],
        scratch_shapes=[pltpu.VMEM((ts + CONV_HALO, D), F32)],
        compiler_params=_params("parallel"),
    )(u, u, dw, dw_b, ln_g, ln_b)


def _cm_ln_bwd(c, ds, ln_g, ln_b, *, name):
    S, D = c.shape
    ts = _pick(S, 512, 16)

    def body(c_ref, ds_ref, g_ref, b_ref, dc_ref, dg_ref, db_ref):
        i = pl.program_id(0)
        cv = c_ref[...]
        mu = jnp.mean(cv, axis=-1, keepdims=True)
        xc = cv - mu
        rstd = lax.rsqrt(jnp.mean(xc * xc, axis=-1, keepdims=True) + EPS)
        xh = xc * rstd
        y = xh * g_ref[...] + b_ref[...]
        sg = _sigmoid(y)
        dy = ds_ref[...].astype(F32) * (sg * (1.0 + y * (1.0 - sg)))
        gy = dy * g_ref[...]
        dc_ref[...] = rstd * (gy - jnp.mean(gy, axis=-1, keepdims=True) - xh * jnp.mean(gy * xh, axis=-1, keepdims=True))
        dg = jnp.sum(dy * xh, axis=0, keepdims=True)
        db = jnp.sum(dy, axis=0, keepdims=True)

        @pl.when(i == 0)
        def _():
            dg_ref[...] = dg
            db_ref[...] = db

        @pl.when(i > 0)
        def _():
            dg_ref[...] += dg
            db_ref[...] += db

    row = pl.BlockSpec((ts, D), lambda i: (i, 0))
    vec = pl.BlockSpec((1, D), lambda i: (0, 0))
    return pl.pallas_call(
        body, name=name, grid=(S // ts,), in_specs=[row, row, vec, vec], out_specs=[row, vec, vec],
        out_shape=[jax.ShapeDtypeStruct((S, D), F32), jax.ShapeDtypeStruct((1, D), F32), jax.ShapeDtypeStruct((1, D), F32)],
        compiler_params=_params("arbitrary"),
    )(c, ds, ln_g, ln_b)


def _cm_conv_bwd(dc, u, dw, *, name):
    S, D2 = u.shape
    D = D2 // 2
    ts = _pick(S, 256, CONV_HALO)
    hb = ts // CONV_HALO
    n_t = S // ts
    last_h = S // CONV_HALO - 1

    rc = _pick(ts, CONV_ROWS, 8)

    def fold8(v):
        out = v[0:8]
        for j in range(1, v.shape[0] // 8):
            out = out + v[8 * j:8 * j + 8]
        return out

    def body(dc_ref, dcn_ref, u_ref, up_ref, dw_ref, du_ref, ddw_ref, ddwb_ref, dbin_ref, dph_ref, gph_ref, dgl_ref):
        i = pl.program_id(0)
        dph_ref[0, 0:ts, :] = dc_ref[...]
        dph_ref[0, ts:ts + CONV_HALO, :] = jnp.where(i < n_t - 1, dcn_ref[...], 0.0)
        prev = up_ref[:, :D] * _sigmoid(up_ref[:, D:])
        gph_ref[0, 0:CONV_HALO, :] = jnp.where(i > 0, prev, 0.0)
        gph_ref[0, CONV_HALO:CONV_HALO + ts, :] = u_ref[:, :D] * _sigmoid(u_ref[:, D:])
        _conv_phases(dph_ref, ts)
        _conv_phases(gph_ref, ts)

        @pl.when(i == 0)
        def _():
            ddw_ref[...] = jnp.zeros_like(ddw_ref)
            ddwb_ref[...] = jnp.zeros_like(ddwb_ref)
            dbin_ref[...] = jnp.zeros_like(dbin_ref)

        for cc in range(D // 128):
            sl = slice(cc * 128, (cc + 1) * 128)
            sl2 = slice(D + cc * 128, D + (cc + 1) * 128)
            wk = [dw_ref[k:k + 1, sl] for k in range(CONV_KERNEL)]
            acc_a, acc_g = jnp.zeros((8, 128), F32), jnp.zeros((8, 128), F32)
            dgl = jnp.zeros((ts, 128), F32)
            for b, a0, taps in _phase_taps(CONV_KERNEL - 1, -1):
                for j, k in enumerate(taps):
                    dgl = dgl + wk[k] * dph_ref[b, 8 * (a0 + j):8 * (a0 + j) + ts, sl]
            dgl_ref[...] = dgl
            for r0 in range(0, ts, rc):
                dglu = dgl_ref[r0:r0 + rc, :]
                av = u_ref[r0:r0 + rc, sl]
                sg = _sigmoid(u_ref[r0:r0 + rc, sl2])
                da = dglu * sg
                dg = dglu * av * sg * (1.0 - sg)
                du_ref[r0:r0 + rc, sl] = da.astype(du_ref.dtype)
                du_ref[r0:r0 + rc, sl2] = dg.astype(du_ref.dtype)
                acc_a = acc_a + fold8(da)
                acc_g = acc_g + fold8(dg)
            dbin_ref[:, sl] += jnp.sum(acc_a, axis=0, keepdims=True)
            dbin_ref[:, sl2] += jnp.sum(acc_g, axis=0, keepdims=True)
            for gi, (b, a0, taps) in enumerate(_phase_taps(CONV_HALO - (CONV_KERNEL - 1))):
                accs = [jnp.zeros((8, 128), F32) for _ in taps]
                accb = jnp.zeros((8, 128), F32)
                for r0 in range(0, ts, rc):
                    dcc = dph_ref[0, r0:r0 + rc, sl]
                    win = gph_ref[b, 8 * a0 + r0:8 * (a0 + len(taps) - 1) + r0 + rc, sl]
                    for j in range(len(taps)):
                        accs[j] = accs[j] + fold8(dcc * win[8 * j:8 * j + rc])
                    if gi == 0:
                        accb = accb + fold8(dcc)
                for j, k in enumerate(taps):
                    ddw_ref[k:k + 1, sl] += jnp.sum(accs[j], axis=0, keepdims=True)
                if gi == 0:
                    ddwb_ref[:, sl] += jnp.sum(accb, axis=0, keepdims=True)

    return pl.pallas_call(
        body, name=name, grid=(n_t,),
        in_specs=[pl.BlockSpec((ts, D), lambda i: (i, 0)),
                  pl.BlockSpec((CONV_HALO, D), lambda i: (jnp.minimum((i + 1) * hb, last_h), 0)),
                  pl.BlockSpec((ts, D2), lambda i: (i, 0)),
                  pl.BlockSpec((CONV_HALO, D2), lambda i: (jnp.maximum(i * hb - 1, 0), 0)),
                  pl.BlockSpec((CONV_KERNEL, D), lambda i: (0, 0))],
        out_specs=[pl.BlockSpec((ts, D2), lambda i: (i, 0)), pl.BlockSpec((CONV_HALO, D), lambda i: (0, 0)),
                   pl.BlockSpec((1, D), lambda i: (0, 0)), pl.BlockSpec((1, D2), lambda i: (0, 0))],
        out_shape=[jax.ShapeDtypeStruct((S, D2), MXU_DTYPE), jax.ShapeDtypeStruct((CONV_HALO, D), F32),
                   jax.ShapeDtypeStruct((1, D), F32), jax.ShapeDtypeStruct((1, D2), F32)],
        scratch_shapes=[pltpu.VMEM((8, ts + CONV_HALO, D), F32), pltpu.VMEM((8, ts + CONV_HALO, D), F32),
                        pltpu.VMEM((ts, 128), F32)],
        compiler_params=_params("arbitrary"),
    )(dc, dc, u, u, dw)


def _ffn_cols(F2):
    return _pick(F2, 1024, 256)


def _ffn_act_fwd(up, dw, dw_b, *, name):
    S, F2 = up.shape
    ts = _pick(S, 512, 16)
    tc = _ffn_cols(F2)
    hb = ts // FFN_HALO

    rc = _pick(ts, FFN_ROWS, 16)

    def body(u_ref, up_ref, w_ref, b_ref, a_ref, ext_ref):
        i = pl.program_id(1)
        ext_ref[0:FFN_HALO, :] = jnp.where(i > 0, up_ref[...].astype(F32), 0.0)
        ext_ref[FFN_HALO:FFN_HALO + ts, :] = u_ref[...].astype(F32)
        for q in range(tc // 256):
            sls = [slice(q * 256 + half * 128, q * 256 + half * 128 + 128) for half in range(2)]
            wk = [[w_ref[k:k + 1, sl] for k in range(FFN_KERNEL)] for sl in sls]
            bb = [b_ref[:, sl] for sl in sls]
            for r0 in range(0, ts, rc):
                gt, vl = [bb[h] + sum(wk[h][k] * ext_ref[pl.ds(FFN_HALO + r0 - 2 + k, rc), sls[h]] for k in range(FFN_KERNEL))
                          for h in range(2)]
                a_ref[r0:r0 + rc, q * 128:(q + 1) * 128] = (gt * _sigmoid(gt) * vl).astype(a_ref.dtype)

    return pl.pallas_call(
        body, name=name, grid=(F2 // tc, S // ts),
        in_specs=[pl.BlockSpec((ts, tc), lambda j, i: (i, j)),
                  pl.BlockSpec((FFN_HALO, tc), lambda j, i: (jnp.maximum(i * hb - 1, 0), j)),
                  pl.BlockSpec((FFN_KERNEL, tc), lambda j, i: (0, j)),
                  pl.BlockSpec((1, tc), lambda j, i: (0, j))],
        out_specs=pl.BlockSpec((ts, tc // 2), lambda j, i: (i, j)),
        out_shape=jax.ShapeDtypeStruct((S, F2 // 2), MXU_DTYPE),
        scratch_shapes=[pltpu.VMEM((ts + FFN_HALO, tc), F32)],
        compiler_params=_params("parallel", "parallel"),
    )(up, up, dw, dw_b)


def _ffn_act_bwd(up, dact, dw, dw_b, *, name):
    S, F2 = up.shape
    ts = _pick(S, 512, 16)
    tc = _ffn_cols(F2)
    hb = ts // FFN_HALO
    n_t = S // ts
    last_h = S // FFN_HALO - 1
    E = ts + FFN_HALO

    rc = _pick(ts, FFN_ROWS, 16)

    def fold8(v):
        out = v[0:8]
        for j in range(1, v.shape[0] // 8):
            out = out + v[8 * j:8 * j + 8]
        return out

    def body(u_ref, up_ref, un_ref, da_ref, dan_ref, w_ref, b_ref, dup_ref, ddw_ref, ddb_ref, ue_ref, dcv_ref):
        i = pl.program_id(1)
        ue_ref[0:FFN_HALO, :] = jnp.where(i > 0, up_ref[...].astype(F32), 0.0)
        ue_ref[FFN_HALO:FFN_HALO + ts, :] = u_ref[...].astype(F32)
        ue_ref[FFN_HALO + ts:FFN_HALO + ts + FFN_HALO, :] = jnp.where(i < n_t - 1, un_ref[...].astype(F32), 0.0)

        @pl.when(i == 0)
        def _():
            ddw_ref[...] = jnp.zeros_like(ddw_ref)
            ddb_ref[...] = jnp.zeros_like(ddb_ref)

        for q in range(tc // 256):
            sls = [slice(q * 256 + half * 128, q * 256 + half * 128 + 128) for half in range(2)]
            qs = slice(q * 128, (q + 1) * 128)
            wk = [[w_ref[k:k + 1, sl] for k in range(FFN_KERNEL)] for sl in sls]
            bb = [b_ref[:, sl] for sl in sls]
            acc = [[jnp.zeros((8, 128), F32) for _ in range(FFN_KERNEL)] for _ in range(2)]
            accb = [jnp.zeros((8, 128), F32) for _ in range(2)]
            for r0, rows in [(r, rc) for r in range(0, ts, rc)] + [(ts, FFN_HALO)]:
                xs = [[ue_ref[pl.ds(FFN_HALO + r0 - 2 + k, rows), sls[h]] for k in range(FFN_KERNEL)] for h in range(2)]
                gt, vl = [bb[h] + sum(wk[h][k] * xs[h][k] for k in range(FFN_KERNEL)) for h in range(2)]
                sg = _sigmoid(gt)
                if r0 < ts:
                    dae = da_ref[r0:r0 + rows, qs].astype(F32)
                else:
                    dae = jnp.where(i < n_t - 1, dan_ref[:, qs].astype(F32), 0.0)
                dcv = [dae * vl * (sg * (1.0 + gt * (1.0 - sg))), dae * (gt * sg)]
                for h in range(2):
                    dcv_ref[r0:r0 + rows, sls[h]] = dcv[h]
                    if r0 < ts:
                        for k in range(FFN_KERNEL):
                            acc[h][k] = acc[h][k] + fold8(dcv[h] * xs[h][k])
                        accb[h] = accb[h] + fold8(dcv[h])
            for r0 in range(0, ts, rc):
                for h in range(2):
                    dup = sum(wk[h][2 - j] * dcv_ref[pl.ds(r0 + j, rc), sls[h]] for j in range(FFN_KERNEL))
                    dup_ref[r0:r0 + rc, sls[h]] = dup.astype(dup_ref.dtype)
            for h in range(2):
                for k in range(FFN_KERNEL):
                    ddw_ref[k:k + 1, sls[h]] += jnp.sum(acc[h][k], axis=0, keepdims=True)
                ddb_ref[:, sls[h]] += jnp.sum(accb[h], axis=0, keepdims=True)

    return pl.pallas_call(
        body, name=name, grid=(F2 // tc, n_t),
        in_specs=[pl.BlockSpec((ts, tc), lambda j, i: (i, j)),
                  pl.BlockSpec((FFN_HALO, tc), lambda j, i: (jnp.maximum(i * hb - 1, 0), j)),
                  pl.BlockSpec((FFN_HALO, tc), lambda j, i: (jnp.minimum((i + 1) * hb, last_h), j)),
                  pl.BlockSpec((ts, tc // 2), lambda j, i: (i, j)),
                  pl.BlockSpec((FFN_HALO, tc // 2), lambda j, i: (jnp.minimum((i + 1) * hb, last_h), j)),
                  pl.BlockSpec((FFN_KERNEL, tc), lambda j, i: (0, j)),
                  pl.BlockSpec((1, tc), lambda j, i: (0, j))],
        out_specs=[pl.BlockSpec((ts, tc), lambda j, i: (i, j)), pl.BlockSpec((FFN_HALO, tc), lambda j, i: (0, j)),
                   pl.BlockSpec((1, tc), lambda j, i: (0, j))],
        out_shape=[jax.ShapeDtypeStruct((S, F2), MXU_DTYPE), jax.ShapeDtypeStruct((FFN_HALO, F2), F32),
                   jax.ShapeDtypeStruct((1, F2), F32)],
        scratch_shapes=[pltpu.VMEM((ts + 2 * FFN_HALO, tc), F32), pltpu.VMEM((E, tc), F32)],
        compiler_params=_params("parallel", "arbitrary"),
    )(up, up, up, dact, dact, dw, dw_b)


def _slopes(n_heads_total):
    return np.asarray(2.0 ** (-ALIBI_MAX * (np.arange(n_heads_total, dtype=np.float32) + 1.0) / n_heads_total), np.float32)


def _qk_norm(x, gain):
    r = lax.rsqrt(jnp.mean(x * x, axis=-1, keepdims=True) + EPS)
    xh = x * r
    return xh * gain, xh, r


def _band(b, dil):
    qi = lax.broadcasted_iota(jnp.int32, (BLOCK, 2 * BLOCK), 0)
    ki = lax.broadcasted_iota(jnp.int32, (BLOCK, 2 * BLOCK), 1)
    delta = qi + BLOCK - ki
    valid = (delta >= 0) & (delta <= BLOCK) & ((ki >= BLOCK) | (b > 0))
    return valid, (delta * dil).astype(F32)


def _permute(x, dil, *, inverse, name):
    if inverse:
        _, L, C = x.shape
        S = dil * L
    else:
        S, C = x.shape
    ts = _pick(S, 512, 16 * dil)
    n = ts // dil
    NC = C // 128
    nat = pl.BlockSpec((ts, C), lambda i: (i, 0))
    sub = pl.BlockSpec((dil, n, C), lambda i: (0, i, 0))

    def body(x_ref, o_ref, nat_ref):
        for c in range(NC):
            sl = slice(c * 128, (c + 1) * 128)
            if inverse:
                for r in range(dil):
                    nat_ref.at[c][pl.ds(r, n, stride=dil), :] = x_ref[r, :, sl].astype(F32)
                o_ref[:, sl] = nat_ref[c].astype(o_ref.dtype)
            else:
                nat_ref[c] = x_ref[:, sl].astype(F32)
                for r in range(dil):
                    o_ref[r, :, sl] = nat_ref.at[c][pl.ds(r, n, stride=dil), :].astype(o_ref.dtype)

    return pl.pallas_call(
        body, name=name, grid=(S // ts,), in_specs=[sub if inverse else nat], out_specs=nat if inverse else sub,
        out_shape=jax.ShapeDtypeStruct((S, C) if inverse else (dil, S // dil, C), x.dtype),
        scratch_shapes=[pltpu.VMEM((NC, ts, 128), F32)], compiler_params=_params("parallel"),
    )(x)


def _to_sub(x, dil, *, name):
    return x if dil == 1 else _permute(x, dil, inverse=False, name=name).reshape(x.shape)


def _to_tok(x, dil, *, name):
    S, C = x.shape
    return x if dil == 1 else _permute(x.reshape(dil, S // dil, C), dil, inverse=True, name=name)


def _attn_fwd(qkv, qg, kg, *, grp, name):
    S, W = qkv.shape
    D = W // 3
    H = D // HEAD_DIM
    dil = DILATED_GROUPS[grp][1]
    L = S // dil
    nb = L // BLOCK
    slopes = _slopes(3 * H)[grp * H:(grp + 1) * H]
    scale = HEAD_DIM ** -0.5

    def body(q_ref, kp_ref, kc_ref, vp_ref, vc_ref, qg_ref, kg_ref, o_ref, l_ref):
        b = pl.program_id(1)
        valid, dist = _band(b, dil)
        l_ref[...] = jnp.zeros_like(l_ref)
        for h in range(H):
            hs = slice(h * HEAD_DIM, (h + 1) * HEAD_DIM)
            qn = _qk_norm(q_ref[:, hs].astype(F32), qg_ref[h:h + 1, :])[0].astype(MXU_DTYPE)
            kp = _qk_norm(kp_ref[:, hs].astype(F32), kg_ref[h:h + 1, :])[0].astype(MXU_DTYPE)
            kc = _qk_norm(kc_ref[:, hs].astype(F32), kg_ref[h:h + 1, :])[0].astype(MXU_DTYPE)
            k2 = jnp.concatenate([kp, kc], axis=0)
            v2 = jnp.concatenate([vp_ref[:, hs], vc_ref[:, hs]], axis=0).astype(MXU_DTYPE)
            s = lax.dot_general(qn, k2, (((1,), (1,)), ((), ())), preferred_element_type=F32) * scale - float(slopes[h]) * dist
            s = jnp.where(valid, s, NEG)
            m = jnp.max(s, axis=-1, keepdims=True)
            p = jnp.exp(s - m)
            den = jnp.sum(p, axis=-1, keepdims=True)
            o = jnp.dot(p.astype(MXU_DTYPE), v2, preferred_element_type=F32) / den
            o_ref[:, hs] = o.astype(o_ref.dtype)
            l_ref[:, h:h + 1] = m + jnp.log(den)

    def cur(j):
        return lambda r, b: (r * nb + b, j)

    def prv(j):
        return lambda r, b: (r * nb + jnp.maximum(b - 1, 0), j)

    blk = (BLOCK, D)
    gain = pl.BlockSpec((H, HEAD_DIM), lambda r, b: (0, 0))
    return pl.pallas_call(
        body, name=name, grid=(dil, nb),
        in_specs=[pl.BlockSpec(blk, cur(0)), pl.BlockSpec(blk, prv(1)), pl.BlockSpec(blk, cur(1)),
                  pl.BlockSpec(blk, prv(2)), pl.BlockSpec(blk, cur(2)), gain, gain],
        out_specs=[pl.BlockSpec(blk, cur(0)), pl.BlockSpec((BLOCK, HEAD_DIM), cur(0))],
        out_shape=[jax.ShapeDtypeStruct((S, D), MXU_DTYPE), jax.ShapeDtypeStruct((S, HEAD_DIM), F32)],
        compiler_params=_params("parallel", "parallel"),
    )(qkv, qkv, qkv, qkv, qkv, qg, kg)


def _attn_merge(os_, ls_, *, name):
    S, D = os_[0].shape
    H = D // HEAD_DIM
    ts = _pick(S, 512, 16)

    def body(o0, o1, o2, l0, l1, l2, outb_ref, lt_ref):
        lt_ref[...] = jnp.zeros_like(lt_ref)
        for h in range(H):
            hs = slice(h * HEAD_DIM, (h + 1) * HEAD_DIM)
            a, b_, c = l0[:, h:h + 1], l1[:, h:h + 1], l2[:, h:h + 1]
            m = jnp.maximum(jnp.maximum(a, b_), c)
            e0, e1, e2 = jnp.exp(a - m), jnp.exp(b_ - m), jnp.exp(c - m)
            den = e0 + e1 + e2
            out = (e0 * o0[:, hs].astype(F32) + e1 * o1[:, hs].astype(F32) + e2 * o2[:, hs].astype(F32)) / den
            outb_ref[:, hs] = out.astype(outb_ref.dtype)
            lt_ref[:, h:h + 1] = m + jnp.log(den)

    row = pl.BlockSpec((ts, D), lambda i: (i, 0))
    lrow = pl.BlockSpec((ts, HEAD_DIM), lambda i: (i, 0))
    return pl.pallas_call(
        body, name=name, grid=(S // ts,), in_specs=[row] * 3 + [lrow] * 3, out_specs=[row, lrow],
        out_shape=[jax.ShapeDtypeStruct((S, D), MXU_DTYPE), jax.ShapeDtypeStruct((S, HEAD_DIM), F32)],
        compiler_params=_params("parallel"),
    )(*os_, *ls_)


def _attn_delta(do, out, *, name):
    S, D = out.shape
    ts = _pick(S, 512, 16)

    def body(do_ref, o_ref, d_ref):
        d_ref[...] = jnp.zeros_like(d_ref)
        for h in range(D // HEAD_DIM):
            hs = slice(h * HEAD_DIM, (h + 1) * HEAD_DIM)
            d_ref[:, h:h + 1] = jnp.sum(do_ref[:, hs].astype(F32) * o_ref[:, hs].astype(F32), axis=-1, keepdims=True)

    row = pl.BlockSpec((ts, D), lambda i: (i, 0))
    return pl.pallas_call(
        body, name=name, grid=(S // ts,), in_specs=[row, row], out_specs=pl.BlockSpec((ts, HEAD_DIM), lambda i: (i, 0)),
        out_shape=jax.ShapeDtypeStruct((S, HEAD_DIM), F32), compiler_params=_params("parallel"),
    )(do, out)


def _attn_bwd(qkv, do, lse, delta, qg, kg, *, grp, name):
    S, W = qkv.shape
    D = W // 3
    H = D // HEAD_DIM
    dil = DILATED_GROUPS[grp][1]
    L = S // dil
    nb = L // BLOCK
    slopes = _slopes(3 * H)[grp * H:(grp + 1) * H]
    scale = HEAD_DIM ** -0.5

    def body(q_ref, qp_ref, kp_ref, kc_ref, vp_ref, vc_ref, do_ref, l_ref, dl_ref, qg_ref, kg_ref,
             out_ref, dqg_ref, dkg_ref, cq_ref, ck_ref, cv_ref, nq_ref, nk_ref, nv_ref, pk_ref, pv_ref):
        r = pl.program_id(0)
        b = pl.program_id(1)

        @pl.when(jnp.logical_and(r == 0, b == 0))
        def _():
            dqg_ref[...] = jnp.zeros_like(dqg_ref)
            dkg_ref[...] = jnp.zeros_like(dkg_ref)

        @pl.when(b < nb)
        def _():
            valid, dist = _band(b, dil)
            for h in range(H):
                hs = slice(h * HEAD_DIM, (h + 1) * HEAD_DIM)
                qn = _qk_norm(q_ref[:, hs].astype(F32), qg_ref[h:h + 1, :])[0].astype(MXU_DTYPE)
                kp = _qk_norm(kp_ref[:, hs].astype(F32), kg_ref[h:h + 1, :])[0].astype(MXU_DTYPE)
                kc = _qk_norm(kc_ref[:, hs].astype(F32), kg_ref[h:h + 1, :])[0].astype(MXU_DTYPE)
                k2 = jnp.concatenate([kp, kc], axis=0)
                v2 = jnp.concatenate([vp_ref[:, hs], vc_ref[:, hs]], axis=0).astype(MXU_DTYPE)
                doh = do_ref[:, hs].astype(MXU_DTYPE)
                s = lax.dot_general(qn, k2, (((1,), (1,)), ((), ())), preferred_element_type=F32) * scale - float(slopes[h]) * dist
                s = jnp.where(valid, s, NEG)
                p = jnp.exp(s - l_ref[:, h:h + 1])
                dp = lax.dot_general(doh, v2, (((1,), (1,)), ((), ())), preferred_element_type=F32)
                dsc = (p * (dp - dl_ref[:, h:h + 1]) * scale).astype(MXU_DTYPE)
                nq_ref[:, hs] = jnp.dot(dsc, k2, preferred_element_type=F32)
                dk2 = lax.dot_general(dsc, qn, (((0,), (0,)), ((), ())), preferred_element_type=F32)
                dv2 = lax.dot_general(p.astype(MXU_DTYPE), doh, (((0,), (0,)), ((), ())), preferred_element_type=F32)
                pk_ref[:, hs] = dk2[0:BLOCK]
                nk_ref[:, hs] = dk2[BLOCK:2 * BLOCK]
                pv_ref[:, hs] = dv2[0:BLOCK]
                nv_ref[:, hs] = dv2[BLOCK:2 * BLOCK]

        @pl.when(b == nb)
        def _():
            pk_ref[...] = jnp.zeros_like(pk_ref)
            pv_ref[...] = jnp.zeros_like(pv_ref)

        @pl.when(b > 0)
        def _():
            for h in range(H):
                hs = slice(h * HEAD_DIM, (h + 1) * HEAD_DIM)
                for j, (raw_ref, gain_ref, dgain_ref) in enumerate(((qp_ref, qg_ref, dqg_ref), (kp_ref, kg_ref, dkg_ref))):
                    dy = cq_ref[:, hs] if j == 0 else ck_ref[:, hs] + pk_ref[:, hs]
                    gain = gain_ref[h:h + 1, :]
                    _, xh, rr = _qk_norm(raw_ref[:, hs].astype(F32), gain)
                    gy = dy * gain
                    dx = rr * (gy - xh * jnp.mean(gy * xh, axis=-1, keepdims=True))
                    out_ref[:, j * D + h * HEAD_DIM:j * D + (h + 1) * HEAD_DIM] = dx.astype(out_ref.dtype)
                    dgain_ref[h:h + 1, :] += jnp.sum(dy * xh, axis=0, keepdims=True)
                out_ref[:, 2 * D + h * HEAD_DIM:2 * D + (h + 1) * HEAD_DIM] = (cv_ref[:, hs] + pv_ref[:, hs]).astype(out_ref.dtype)

        @pl.when(b < nb)
        def _():
            cq_ref[...] = nq_ref[...]
            ck_ref[...] = nk_ref[...]
            cv_ref[...] = nv_ref[...]

    def cur(j):
        return lambda r, b: (r * nb + jnp.minimum(b, nb - 1), j)

    def prv(j):
        return lambda r, b: (r * nb + jnp.clip(b - 1, 0, nb - 1), j)

    blk = (BLOCK, D)
    lblk = pl.BlockSpec((BLOCK, HEAD_DIM), cur(0))
    gain = pl.BlockSpec((H, HEAD_DIM), lambda r, b: (0, 0))
    return pl.pallas_call(
        body, name=name, grid=(dil, nb + 1),
        in_specs=[pl.BlockSpec(blk, cur(0)), pl.BlockSpec(blk, prv(0)), pl.BlockSpec(blk, prv(1)), pl.BlockSpec(blk, cur(1)),
                  pl.BlockSpec(blk, prv(2)), pl.BlockSpec(blk, cur(2)), pl.BlockSpec(blk, cur(0)), lblk, lblk, gain, gain],
        out_specs=[pl.BlockSpec((BLOCK, 3 * D), lambda r, b: (r * nb + jnp.maximum(b - 1, 0), 0)), gain, gain],
        out_shape=[jax.ShapeDtypeStruct((S, 3 * D), MXU_DTYPE), jax.ShapeDtypeStruct((H, HEAD_DIM), F32),
                   jax.ShapeDtypeStruct((H, HEAD_DIM), F32)],
        scratch_shapes=[pltpu.VMEM(blk, F32)] * 8,
        compiler_params=_params("arbitrary", "arbitrary"),
    )(qkv, qkv, qkv, qkv, qkv, qkv, do, lse, delta, qg, kg)


def _loss_head(y, target, *, name):
    S, D = y.shape
    ts = _pick(S, 512, 16)

    def body(y_ref, t_ref, dy_ref, dyb_ref, l_ref, acc_ref):
        i = pl.program_id(0)
        e = y_ref[...] - t_ref[...]
        dy = e * (1.0 / D)
        dy_ref[...] = dy
        dyb_ref[...] = dy.astype(dyb_ref.dtype)
        part = jnp.sum(e * e, axis=0, keepdims=True)

        @pl.when(i == 0)
        def _():
            acc_ref[...] = part

        @pl.when(i > 0)
        def _():
            acc_ref[...] += part

        @pl.when(i == pl.num_programs(0) - 1)
        def _():
            l_ref[...] = jnp.broadcast_to(jnp.sum(acc_ref[...], axis=-1, keepdims=True) * (0.5 / D), l_ref.shape)

    row = pl.BlockSpec((ts, D), lambda i: (i, 0))
    return pl.pallas_call(
        body, name=name, grid=(S // ts,), in_specs=[row, row],
        out_specs=[row, row, pl.BlockSpec((8, 128), lambda i: (0, 0))],
        out_shape=[jax.ShapeDtypeStruct((S, D), F32), jax.ShapeDtypeStruct((S, D), MXU_DTYPE), jax.ShapeDtypeStruct((8, 128), F32)],
        scratch_shapes=[pltpu.VMEM((1, D), F32)],
        compiler_params=_params("arbitrary"),
    )(y, target)


def _adamw(w, m, v, terms, slots, *, name):
    R, C = w.shape
    nt = len(terms)
    tr = _pick(R, 256, 16)
    c1 = 1.0 - ADAM_B1 ** ADAM_STEP
    c2 = 1.0 - ADAM_B2 ** ADAM_STEP

    def body(slot_ref, w_ref, m_ref, v_ref, *rest):
        t_refs = rest[:nt]
        g_ref, d_ref, nm_ref, nv_ref = rest[nt:]
        g = t_refs[0][...].astype(F32)
        for t in t_refs[1:]:
            g = g + t[...].astype(F32)
        mm = ADAM_B1 * m_ref[...] + (1.0 - ADAM_B1) * g
        vv = ADAM_B2 * v_ref[...] + (1.0 - ADAM_B2) * (g * g)
        m_hat = mm / c1
        v_hat = vv / c2
        g_ref[...] = g
        d_ref[...] = -ADAM_LR * (m_hat / (jnp.sqrt(v_hat) + ADAM_EPS) + ADAM_WD * w_ref[...])
        nm_ref[...] = mm
        nv_ref[...] = vv

    row = pl.BlockSpec((tr, C), lambda i, s: (i, 0))
    grid_spec = pltpu.PrefetchScalarGridSpec(
        num_scalar_prefetch=1, grid=(R // tr,),
        in_specs=[row, row, row] + [pl.BlockSpec((None, tr, C), lambda i, s, t=t: (s[t], i, 0)) for t in range(nt)],
        out_specs=[row] * 4)
    return pl.pallas_call(
        body, name=name, grid_spec=grid_spec, out_shape=[jax.ShapeDtypeStruct((R, C), F32)] * 4,
        compiler_params=_params("parallel"),
    )(slots, w, m, v, *terms)


def _chip_partials(g, sib, core, *, name):
    _, R, C = g.shape
    tr = _pick(R, 1024, 16)

    def body(core_ref, g_ref, s_ref, o_ref):
        o_ref[...] = (g_ref[...] + s_ref[...].astype(F32)).astype(o_ref.dtype)

    grid_spec = pltpu.PrefetchScalarGridSpec(
        num_scalar_prefetch=1, grid=(4, R // tr),
        in_specs=[pl.BlockSpec((None, tr, C), lambda k, i, c: (2 * k + c[0], i, 0)),
                  pl.BlockSpec((None, tr, C), lambda k, i, c: (k, i, 0))],
        out_specs=pl.BlockSpec((None, tr, C), lambda k, i, c: (k, i, 0)))
    return pl.pallas_call(
        body, name=name, grid_spec=grid_spec, out_shape=jax.ShapeDtypeStruct((4, R, C), sib.dtype),
        compiler_params=_params("parallel", "parallel"),
    )(core, g, sib)


_ANY = pl.BlockSpec(memory_space=pl.ANY)


def _place():
    return lax.axis_index("x"), lax.axis_index("y"), lax.axis_index("c")


def _all_gather(shard, *, name):
    R, C = shard.shape

    def body(x_ref, out_ref, send_sems, recv_sems, local_sem):
        x, y, c = _place()
        me, sibling = (x, y, c), (x, y, 1 - c)
        chips = [(1 - x, y), (x, 1 - y), (1 - x, 1 - y)]

        def slot(px, py, pc):
            return out_ref.at[4 * px + 2 * py + pc]

        def copy(k, block, to, src=None):
            return pltpu.make_async_remote_copy(
                src_ref=slot(*block) if src is None else src, dst_ref=slot(*block),
                send_sem=send_sems.at[k], recv_sem=recv_sems.at[k], device_id=to, device_id_type=MESH)

        mine = pltpu.make_async_copy(x_ref, slot(*me), local_sem)
        mine.start()
        first = [copy(0, me, sibling, src=x_ref)]
        first += [copy(1 + j, me, (*chip, c), src=x_ref) for j, chip in enumerate(chips)]
        for cp in first:
            cp.start()
        passed = [copy(4 + j, (*chip, c), sibling) for j, chip in enumerate(chips)]
        for j, chip in enumerate(chips):
            copy(1 + j, (*chip, c), me).wait_recv()
            passed[j].start()
        copy(0, sibling, me).wait_recv()
        for j, chip in enumerate(chips):
            copy(4 + j, (*chip, 1 - c), me).wait_recv()
        for cp in first + passed:
            cp.wait_send()
        mine.wait()

    return pl.pallas_call(
        body, name=name, in_specs=[_ANY], out_specs=_ANY,
        out_shape=jax.ShapeDtypeStruct((N_DEV, R, C), shard.dtype),
        scratch_shapes=[pltpu.SemaphoreType.DMA((7,)), pltpu.SemaphoreType.DMA((7,)), pltpu.SemaphoreType.DMA],
    )(shard)


def _rs_sibling(g, *, name):
    _, R, C = g.shape

    def body(g_ref, sib_ref, send_sems, recv_sems):
        x, y, c = _place()
        sends = [pltpu.make_async_remote_copy(
            src_ref=g_ref.at[2 * k + (1 - c)], dst_ref=sib_ref.at[k], send_sem=send_sems.at[k], recv_sem=recv_sems.at[k],
            device_id=(x, y, 1 - c), device_id_type=MESH) for k in range(4)]
        for cp in sends:
            cp.start()
        for cp in sends:
            cp.wait_recv()
        for cp in sends:
            cp.wait_send()

    return pl.pallas_call(
        body, name=name, in_specs=[_ANY], out_specs=_ANY, out_shape=jax.ShapeDtypeStruct((4, R, C), g.dtype),
        scratch_shapes=[pltpu.SemaphoreType.DMA((4,)), pltpu.SemaphoreType.DMA((4,))],
    )(g)


def _rs_chips(part, *, name):
    _, R, C = part.shape

    def body(p_ref, out_ref, send_sems, recv_sems):
        x, y, c = _place()
        chips = [(1 - x, y), (x, 1 - y), (1 - x, 1 - y)]
        sends = [pltpu.make_async_remote_copy(
            src_ref=p_ref.at[2 * px + py], dst_ref=out_ref.at[j], send_sem=send_sems.at[j], recv_sem=recv_sems.at[j],
            device_id=(px, py, c), device_id_type=MESH) for j, (px, py) in enumerate(chips)]
        for cp in sends:
            cp.start()
        for cp in sends:
            cp.wait_recv()
        for cp in sends:
            cp.wait_send()

    return pl.pallas_call(
        body, name=name, in_specs=[_ANY], out_specs=_ANY, out_shape=jax.ShapeDtypeStruct((3, R, C), part.dtype),
        scratch_shapes=[pltpu.SemaphoreType.DMA((3,)), pltpu.SemaphoreType.DMA((3,))],
    )(part)


def _interleave_rows(wt):
    F2, D = wt.shape
    return wt.reshape(2, F2 // 256, 128, D).transpose(1, 0, 2, 3).reshape(F2, D)


def _deinterleave_rows(wt):
    F2, D = wt.shape
    return wt.reshape(F2 // 256, 2, 128, D).transpose(1, 0, 2, 3).reshape(F2, D)


def _interleave_cols(v):
    k, F2 = v.shape
    return v.reshape(k, 2, F2 // 256, 128).transpose(0, 2, 1, 3).reshape(k, F2)


def _deinterleave_cols(v):
    k, F2 = v.shape
    return v.reshape(k, F2 // 256, 2, 128).transpose(0, 2, 1, 3).reshape(k, F2)


def _pack_rows(parts):
    return jnp.concatenate(parts, axis=0)


def _flat_pack(parts, width):
    flat = jnp.concatenate([p.reshape(-1) for p in parts])
    pad = (-flat.shape[0]) % (8 * width)
    return jnp.pad(flat, (0, pad)).reshape(-1, width)


def _flat_unpack(packed, shapes):
    flat = packed.reshape(-1)
    out, off = [], 0
    for shp in shapes:
        n = int(np.prod(shp))
        out.append(flat[off:off + n].reshape(shp))
        off += n
    return out


def _ffn_forward(x, g_ffn, wupT, wdown, dw_i, dwb_i, tag):
    hf = _rms_fwd(x, g_ffn, name=f"ffn{tag}_rms")
    up = _mm(hf, wupT, mode="nt", out_dtype=MXU_DTYPE, name=f"ffn{tag}_up", tn=512)
    act = _ffn_act_fwd(up, dw_i, dwb_i, name=f"ffn{tag}_act")
    y = _mm(act, wdown, mode="nn", out_dtype=F32, name=f"ffn{tag}_down", residual=x)
    return y, (hf, up, act)


def _ffn_backward(x, g_ffn, wupT, wdown, dw_i, dwb_i, saved, dy, dyb, tag):
    hf, up, act = saved
    dact = _mm(dyb, wdown, mode="nt", out_dtype=MXU_DTYPE, name=f"ffn{tag}_dact", tm=512, tn=1408)
    d_wdown = _mm(act, dyb, mode="tn", out_dtype=F32, name=f"ffn{tag}_dwdown", tm=1408, tk=2048)
    dup, d_dw_i, d_dwb_i = _ffn_act_bwd(up, dact, dw_i, dwb_i, name=f"ffn{tag}_actbwd")
    dhf = _mm(dup, wupT, mode="nn", out_dtype=F32, name=f"ffn{tag}_dhf", tm=512)
    d_wupT = _mm(dup, hf, mode="tn", out_dtype=F32, name=f"ffn{tag}_dwup", tm=512, tk=2048)
    dx, dxb, dg, cs = _rms_bwd(x, g_ffn, [dhf], dy, name=f"ffn{tag}_rmsbwd")
    return dx, dxb, cs, dict(w_upT=d_wupT, w_down=d_wdown, dw=d_dw_i[0:FFN_KERNEL], dw_b=d_dwb_i, norm=dg)


def _local_step(x, target, p):
    S, D = x.shape
    H = D // HEAD_DIM
    h0 = _rms_fwd(x, p["norm_mix"][0:1], name="l0_rms")
    u = _mm(h0, p["w_inT"], mode="nt", out_dtype=F32, name="l0_in", bias=p["cm_b_in"])
    c, s = _cm_fwd(u, p["cm_dw"], p["cm_dw_b"], p["cm_ln_g"], p["cm_ln_b"], name="l0_conv")
    x1 = _mm(s, p["w_out"], mode="nn", out_dtype=F32, name="l0_out", bias=p["cm_b_out"], residual=x)
    x2, sv0 = _ffn_forward(x1, p["norm_ffn"][0:1], p["w_upT"][0], p["w_down"][0], p["ff_dw"][0], p["ff_dw_b"][0:1], 0)
    h1 = _rms_fwd(x2, p["norm_mix"][1:2], name="l1_rms")
    dils = [dil for _, dil in DILATED_GROUPS]
    h1s, qkvs, os_, ls_ = [], [], [], []
    for g, dil in enumerate(dils):
        h1s.append(_to_sub(h1, dil, name=f"l1_h_sub{g}"))
        qkvs.append(_mm(h1s[g], p["w_qkvT"], mode="nt", out_dtype=MXU_DTYPE, name=f"l1_qkv{g}", b_off=g * 3 * D, b_len=3 * D))
        o, l = _attn_fwd(qkvs[g], p["at_q_norm"][g * H:(g + 1) * H], p["at_k_norm"][g * H:(g + 1) * H], grp=g, name=f"l1_attn{g}")
        os_.append(_to_tok(o, dil, name=f"l1_o_tok{g}"))
        ls_.append(_to_tok(l, dil, name=f"l1_lse_tok{g}"))
    outb, lse = _attn_merge(os_, ls_, name="l1_merge")
    x3 = _mm(outb, p["w_o"], mode="nn", out_dtype=F32, name="l1_o", residual=x2)
    x4, sv1 = _ffn_forward(x3, p["norm_ffn"][1:2], p["w_upT"][1], p["w_down"][1], p["ff_dw"][1], p["ff_dw_b"][1:2], 1)
    dx4, dx4b, loss = _loss_head(x4, target, name="loss")
    dx3, dx3b, _, gf1 = _ffn_backward(x3, p["norm_ffn"][1:2], p["w_upT"][1], p["w_down"][1], p["ff_dw"][1], p["ff_dw_b"][1:2],
                                      sv1, dx4, dx4b, 1)
    do = _mm(dx3b, p["w_o"], mode="nt", out_dtype=MXU_DTYPE, name="l1_do")
    d_wo = _mm(outb, dx3b, mode="tn", out_dtype=F32, name="l1_dwo", tk=2048)
    delta = _attn_delta(do, outb, name="l1_delta")
    dh1s, d_wqkvT, dqg, dkg = [], [], [], []
    for g, dil in enumerate(dils):
        dqkv_g, a, b_ = _attn_bwd(qkvs[g], _to_sub(do, dil, name=f"l1_do_sub{g}"), _to_sub(lse, dil, name=f"l1_lse_sub{g}"),
                                  _to_sub(delta, dil, name=f"l1_delta_sub{g}"), p["at_q_norm"][g * H:(g + 1) * H],
                                  p["at_k_norm"][g * H:(g + 1) * H], grp=g, name=f"l1_attnbwd{g}")
        dqg.append(a)
        dkg.append(b_)
        d_wqkvT.append(_mm(dqkv_g, h1s[g], mode="tn", out_dtype=F32, name=f"l1_dwqkv{g}", tk=2048))
        dh_g = _mm(dqkv_g, p["w_qkvT"], mode="nn", out_dtype=F32, name=f"l1_dh{g}", b_off=g * 3 * D, b_len=3 * D)
        dh1s.append(_to_tok(dh_g, dil, name=f"l1_dh_tok{g}"))
    dx2, dx2b, dgm1, _ = _rms_bwd(x2, p["norm_mix"][1:2], dh1s, dx3, name="l1_rmsbwd")
    dx1, dx1b, cs1, gf0 = _ffn_backward(x1, p["norm_ffn"][0:1], p["w_upT"][0], p["w_down"][0], p["ff_dw"][0], p["ff_dw_b"][0:1],
                                        sv0, dx2, dx2b, 0)
    ds = _mm(dx1b, p["w_out"], mode="nt", out_dtype=F32, name="l0_ds")
    d_wout = _mm(s, dx1b, mode="tn", out_dtype=F32, name="l0_dwout", tk=2048)
    dc, d_lng, d_lnb = _cm_ln_bwd(c, ds, p["cm_ln_g"], p["cm_ln_b"], name="l0_lnbwd")
    du, d_cmdw, d_cmdwb, d_bin = _cm_conv_bwd(dc, u, p["cm_dw"], name="l0_convbwd")
    dh0 = _mm(du, p["w_inT"], mode="nn", out_dtype=F32, name="l0_dh")
    d_winT = _mm(du, h0, mode="tn", out_dtype=F32, name="l0_dwin", tk=2048)
    grad_x, _, dgm0, _ = _rms_bwd(x, p["norm_mix"][0:1], [dh0], dx1, name="l0_rmsbwd")
    grads = dict(
        norm_mix=jnp.concatenate([dgm0, dgm1], axis=0),
        norm_ffn=jnp.concatenate([gf0["norm"], gf1["norm"]], axis=0),
        w_inT=d_winT, cm_b_in=d_bin, cm_dw=d_cmdw[0:CONV_KERNEL], cm_dw_b=d_cmdwb, cm_ln_g=d_lng, cm_ln_b=d_lnb,
        w_out=d_wout, cm_b_out=cs1,
        w_qkvT=jnp.concatenate(d_wqkvT, axis=0), at_q_norm=jnp.concatenate(dqg, axis=0), at_k_norm=jnp.concatenate(dkg, axis=0),
        w_o=d_wo,
        w_upT=[gf0["w_upT"], gf1["w_upT"]], w_down=[gf0["w_down"], gf1["w_down"]],
        ff_dw=jnp.stack([gf0["dw"], gf1["dw"]]), ff_dw_b=jnp.concatenate([gf0["dw_b"], gf1["dw_b"]], axis=0),
    )
    return loss, grad_x, grads


_BIG = ("cm_w_in", "cm_w_out", "at_w_qkv", "at_w_out", "ff_w_up", "ff_w_down")
_TRANSPOSED = ("cm_w_in", "at_w_qkv", "ff_w_up")
_SMALL = ("norm_mix", "norm_ffn", "cm_b_in", "cm_dw_b", "cm_ln_g", "cm_ln_b", "cm_b_out", "at_q_norm", "at_k_norm",
          "ff_dw_b", "cm_dw", "ff_dw")
_SMALL_SHARDED = ("cm_dw", "ff_dw")
_ORDER = ("norm_mix", "norm_ffn", "cm_w_in", "cm_b_in", "cm_dw", "cm_dw_b", "cm_ln_g", "cm_ln_b", "cm_w_out", "cm_b_out",
          "at_w_qkv", "at_q_norm", "at_k_norm", "at_w_out", "ff_w_up", "ff_dw", "ff_dw_b", "ff_w_down")


def _big_rows(t):
    parts = []
    for n in _BIG:
        a = t[n]
        mats = [a[l] for l in range(a.shape[0])]
        if n in _TRANSPOSED:
            mats = [m.T for m in mats]
        parts += mats
    return _pack_rows(parts)


def _big_unrows(packed, like):
    out, off = {}, 0
    for n in _BIG:
        a = like[n]
        mats = []
        for l in range(a.shape[0]):
            rows, cols = (a.shape[2], a.shape[1]) if n in _TRANSPOSED else (a.shape[1], a.shape[2])
            m = packed[off:off + rows]
            off += rows
            mats.append(m.T if n in _TRANSPOSED else m)
        out[n] = jnp.stack(mats)
    return out


def kernel(x, norm_mix, norm_ffn, cm_w_in, cm_b_in, cm_dw, cm_dw_b, cm_ln_g, cm_ln_b, cm_w_out, cm_b_out, at_w_qkv, at_q_norm, at_k_norm, at_w_out, ff_w_up, ff_dw, ff_dw_b, ff_w_down, loss_target, m_norm_mix, m_norm_ffn, m_cm_w_in, m_cm_b_in, m_cm_dw, m_cm_dw_b, m_cm_ln_g, m_cm_ln_b, m_cm_w_out, m_cm_b_out, m_at_w_qkv, m_at_q_norm, m_at_k_norm, m_at_w_out, m_ff_w_up, m_ff_dw, m_ff_dw_b, m_ff_w_down, v_norm_mix, v_norm_ffn, v_cm_w_in, v_cm_b_in, v_cm_dw, v_cm_dw_b, v_cm_ln_g, v_cm_ln_b, v_cm_w_out, v_cm_b_out, v_at_w_qkv, v_at_q_norm, v_at_k_norm, v_at_w_out, v_ff_w_up, v_ff_dw, v_ff_dw_b, v_ff_w_down):
    w = dict(norm_mix=norm_mix, norm_ffn=norm_ffn, cm_w_in=cm_w_in, cm_b_in=cm_b_in, cm_dw=cm_dw, cm_dw_b=cm_dw_b, cm_ln_g=cm_ln_g,
             cm_ln_b=cm_ln_b, cm_w_out=cm_w_out, cm_b_out=cm_b_out, at_w_qkv=at_w_qkv, at_q_norm=at_q_norm, at_k_norm=at_k_norm,
             at_w_out=at_w_out, ff_w_up=ff_w_up, ff_dw=ff_dw, ff_dw_b=ff_dw_b, ff_w_down=ff_w_down)
    m = dict(norm_mix=m_norm_mix, norm_ffn=m_norm_ffn, cm_w_in=m_cm_w_in, cm_b_in=m_cm_b_in, cm_dw=m_cm_dw, cm_dw_b=m_cm_dw_b,
             cm_ln_g=m_cm_ln_g, cm_ln_b=m_cm_ln_b, cm_w_out=m_cm_w_out, cm_b_out=m_cm_b_out, at_w_qkv=m_at_w_qkv,
             at_q_norm=m_at_q_norm, at_k_norm=m_at_k_norm, at_w_out=m_at_w_out, ff_w_up=m_ff_w_up, ff_dw=m_ff_dw,
             ff_dw_b=m_ff_dw_b, ff_w_down=m_ff_w_down)
    v = dict(norm_mix=v_norm_mix, norm_ffn=v_norm_ffn, cm_w_in=v_cm_w_in, cm_b_in=v_cm_b_in, cm_dw=v_cm_dw, cm_dw_b=v_cm_dw_b,
             cm_ln_g=v_cm_ln_g, cm_ln_b=v_cm_ln_b, cm_w_out=v_cm_w_out, cm_b_out=v_cm_b_out, at_w_qkv=v_at_w_qkv,
             at_q_norm=v_at_q_norm, at_k_norm=v_at_k_norm, at_w_out=v_at_w_out, ff_w_up=v_ff_w_up, ff_dw=v_ff_dw,
             ff_dw_b=v_ff_dw_b, ff_w_down=v_ff_w_down)
    S, D = x.shape[1], x.shape[2]
    F2 = ff_dw_b.shape[1]
    H3 = at_q_norm.shape[1]
    me = 4 * lax.axis_index("x") + 2 * lax.axis_index("y") + lax.axis_index("c")

    w_rows = _big_rows(w)
    gathered = _all_gather(w_rows.astype(MXU_DTYPE), name="gather_weights")
    full, off = {}, 0
    for n in _BIG:
        a = w[n]
        mats = []
        for l in range(a.shape[0]):
            rows = a.shape[2] if n in _TRANSPOSED else a.shape[1]
            mats.append(gathered[:, off:off + rows, :].reshape(N_DEV * rows, D))
            off += rows
        full[n] = mats
    small_sh = _flat_pack([cm_dw, ff_dw], D)
    small_g = _all_gather(small_sh, name="gather_small")
    cm_dw_full = jnp.concatenate([_flat_unpack(small_g[j], [cm_dw.shape, ff_dw.shape])[0][0] for j in range(N_DEV)], axis=-1)
    ff_dw_full = jnp.concatenate([_flat_unpack(small_g[j], [cm_dw.shape, ff_dw.shape])[1] for j in range(N_DEV)], axis=-1)

    p = dict(
        norm_mix=norm_mix, norm_ffn=norm_ffn, cm_b_in=cm_b_in, cm_dw=cm_dw_full, cm_dw_b=cm_dw_b, cm_ln_g=cm_ln_g, cm_ln_b=cm_ln_b,
        cm_b_out=cm_b_out, at_q_norm=at_q_norm[0], at_k_norm=at_k_norm[0],
        w_inT=full["cm_w_in"][0], w_out=full["cm_w_out"][0], w_qkvT=full["at_w_qkv"][0], w_o=full["at_w_out"][0],
        w_upT=[_interleave_rows(t) for t in full["ff_w_up"]], w_down=full["ff_w_down"],
        ff_dw=jnp.stack([_interleave_cols(ff_dw_full[l]) for l in range(ff_dw_full.shape[0])]),
        ff_dw_b=_interleave_cols(ff_dw_b),
    )
    loss8, grad_x, g = _local_step(x[0], loss_target[0], p)
    loss = lax.psum(loss8[0, 0], ("x", "y", "c"))

    pieces = [g["w_inT"], g["w_out"], g["w_qkvT"], g["w_o"], _deinterleave_rows(g["w_upT"][0]), _deinterleave_rows(g["w_upT"][1]),
              g["w_down"][0], g["w_down"][1]]
    g_rows = jnp.concatenate([t.reshape(N_DEV, t.shape[0] // N_DEV, D) for t in pieces], axis=1)
    ix, iy, ic = lax.axis_index("x"), lax.axis_index("y"), lax.axis_index("c")
    chip = 2 * ix + iy
    sib = _rs_sibling(g_rows.astype(WIRE_DTYPE), name="reduce_sibling")
    part = _chip_partials(g_rows, sib, jnp.stack([ic]).astype(jnp.int32), name="reduce_add")
    recv = _rs_chips(part, name="reduce_chips")
    slots = jnp.stack([me, chip, 0 * me, 0 * me + 1, 0 * me + 2]).astype(jnp.int32)
    gb, db, mb, vb = _adamw(w_rows, _big_rows(m), _big_rows(v), [g_rows, sib, recv, recv, recv], slots, name="adamw_big")
    big = [_big_unrows(t, w) for t in (gb, db, mb, vb)]

    g_small = dict(g)
    g_small["cm_b_in"] = g["cm_b_in"]
    g_small["at_q_norm"] = g["at_q_norm"][None]
    g_small["at_k_norm"] = g["at_k_norm"][None]
    g_small["ff_dw_b"] = _deinterleave_cols(g["ff_dw_b"])
    g_small["cm_dw"] = g["cm_dw"][None]
    g_small["ff_dw"] = jnp.stack([_deinterleave_cols(g["ff_dw"][l]) for l in range(g["ff_dw"].shape[0])])
    small_shapes = [g_small[n].shape for n in _SMALL]
    gs_parts = _all_gather(_flat_pack([g_small[n] for n in _SMALL], D), name="gather_small_grads")

    def embed(t, n):
        if n not in _SMALL_SHARDED:
            return t
        full_shape = t.shape[:-1] + (t.shape[-1] * N_DEV,)
        return lax.dynamic_update_slice_in_dim(jnp.zeros(full_shape, F32), t, me * t.shape[-1], axis=t.ndim - 1)

    packs = [_flat_pack([embed(tree[n], n) for n in _SMALL], D) for tree in (w, m, v)]
    gs, ds_, ms, vs = _adamw(packs[0], packs[1], packs[2], [gs_parts] * N_DEV, jnp.arange(N_DEV, dtype=jnp.int32),
                             name="adamw_small")
    small = []
    for t in (gs, ds_, ms, vs):
        un = dict(zip(_SMALL, _flat_unpack(t, small_shapes)))
        for n in _SMALL_SHARDED:
            width = w[n].shape[-1]
            un[n] = lax.dynamic_slice_in_dim(un[n], me * width, width, axis=un[n].ndim - 1)
        small.append({n: un[n].reshape(w[n].shape) for n in _SMALL})

    outs = [loss, grad_x[None]]
    for k in range(4):
        for n in _ORDER:
            outs.append(big[k][n] if n in _BIG else small[k][n])
    return tuple(outs)
```

```python
import functools

import jax
import jax.numpy as jnp
import numpy as np
from jax import lax
from jax.experimental import pallas as pl
from jax.experimental.pallas import tpu as pltpu

F32 = jnp.float32
MXU_DTYPE = jnp.bfloat16
WIRE_DTYPE = jnp.bfloat16
EPS = 1e-6
NEG = -1e30
HEAD_DIM = 128
BLOCK = 128
DILATED_GROUPS = ((128, 1), (512, 4), (2048, 16))
ALIBI_MAX = 8.0
CONV_KERNEL = 31
CONV_HALO = 32
CONV_ROWS = 64
FFN_KERNEL = 3
FFN_HALO = 16
FFN_ROWS = 64
ADAM_LR, ADAM_B1, ADAM_B2, ADAM_EPS, ADAM_WD, ADAM_STEP = 0.001, 0.9, 0.999, 1e-08, 0.01, 10
V7X_VMEM_BYTES = 64 * 1024 * 1024
VMEM_LIMIT = V7X_VMEM_BYTES * 3 // 4
N_DEV = 8
MESH = pl.DeviceIdType.MESH


def _pick(n, target, align):
    if n <= target:
        return n
    best = None
    for t in range(align, target + 1, align):
        if n % t == 0:
            best = t
    assert best is not None, (n, target, align)
    return best


def _params(*sem):
    return pltpu.CompilerParams(dimension_semantics=sem, vmem_limit_bytes=VMEM_LIMIT)


def _sigmoid(x):
    return 1.0 / (1.0 + jnp.exp(-x))


_DIMS = {"nn": ((1,), (0,)), "nt": ((1,), (1,)), "tn": ((0,), (0,))}


def _mm(a, b, *, mode, out_dtype, name, tm=1024, tn=1024, tk=None, bias=None, residual=None, b_off=0, b_len=None):
    if mode == "tn":
        K, M = a.shape
    else:
        M, K = a.shape
    if mode == "nt":
        N = b.shape[0] if b_len is None else b_len
    else:
        N = b.shape[1]
    if b_len is not None:
        assert mode == "nt" or (mode == "nn" and K == b_len)
    tm = _pick(M, tm, 128 if mode == "tn" else 16)
    tn = _pick(N, tn, 128)
    tk = K if tk is None else _pick(K, tk, 128 if mode != "tn" else 16)
    nk = K // tk
    unit = tn if mode == "nt" else tk
    assert b_off % unit == 0
    kb0 = b_off // unit
    if mode == "tn":
        a_spec = pl.BlockSpec((tk, tm), lambda i, j, k: (k, i))
    else:
        a_spec = pl.BlockSpec((tm, tk), lambda i, j, k: (i, k))
    if mode == "nt":
        b_spec = pl.BlockSpec((tn, tk), lambda i, j, k: (j + kb0, k))
    else:
        b_spec = pl.BlockSpec((tk, tn), lambda i, j, k: (k + kb0, j))
    in_specs = [a_spec, b_spec]
    args = [a, b]
    if bias is not None:
        in_specs.append(pl.BlockSpec((1, tn), lambda i, j, k: (0, j)))
        args.append(bias)
    if residual is not None:
        in_specs.append(pl.BlockSpec((tm, tn), lambda i, j, k: (i, j)))
        args.append(residual)
    has_bias, has_res = bias is not None, residual is not None

    def body(*refs):
        a_ref, b_ref = refs[0], refs[1]
        pos = 2
        bias_ref = res_ref = None
        if has_bias:
            bias_ref = refs[pos]
            pos += 1
        if has_res:
            res_ref = refs[pos]
            pos += 1
        o_ref = refs[pos]
        acc_ref = refs[pos + 1] if nk > 1 else None

        def finish(acc):
            if has_bias:
                acc = acc + bias_ref[...]
            if has_res:
                acc = acc + res_ref[...]
            o_ref[...] = acc.astype(o_ref.dtype)

        part = lax.dot_general(a_ref[...].astype(MXU_DTYPE), b_ref[...].astype(MXU_DTYPE), (_DIMS[mode], ((), ())),
                               preferred_element_type=F32)
        if nk == 1:
            finish(part)
        else:
            k = pl.program_id(2)

            @pl.when(k == 0)
            def _():
                acc_ref[...] = part

            @pl.when(jnp.logical_and(k > 0, k < nk - 1))
            def _():
                acc_ref[...] += part

            @pl.when(k == nk - 1)
            def _():
                finish(acc_ref[...] + part)

    return pl.pallas_call(
        body, name=name, grid=(M // tm, N // tn, nk), in_specs=in_specs,
        out_specs=pl.BlockSpec((tm, tn), lambda i, j, k: (i, j)),
        out_shape=jax.ShapeDtypeStruct((M, N), out_dtype),
        scratch_shapes=[pltpu.VMEM((tm, tn), F32)] if nk > 1 else [],
        compiler_params=_params("parallel", "parallel", "arbitrary"),
    )(*args)


SUB_TILE = 512


def _sub_spec(dil, ts, cols):
    return pl.BlockSpec((dil, ts // dil, cols), lambda i: (0, i, 0))


def _tok_to_sub(tok_ref, dst_ref, dil):
    nc, ts, _ = tok_ref.shape
    for c in range(nc):
        for r in range(dil):
            dst_ref[r, :, c * 128:(c + 1) * 128] = tok_ref.at[c][pl.ds(r, ts // dil, stride=dil), :].astype(dst_ref.dtype)


def _sub_to_tok(src_ref, tok_ref, dil):
    nc, ts, _ = tok_ref.shape
    for c in range(nc):
        for r in range(dil):
            tok_ref.at[c][pl.ds(r, ts // dil, stride=dil), :] = src_ref[r, :, c * 128:(c + 1) * 128].astype(F32)


def _rms_fwd(x, g, *, name, subs=()):
    S, D = x.shape
    ts = _pick(S, SUB_TILE, 16 * max(subs, default=1))
    NC = D // 128

    def body(x_ref, g_ref, h_ref, *rest):
        xv = x_ref[...]
        r = lax.rsqrt(jnp.mean(xv * xv, axis=-1, keepdims=True) + EPS)
        h = xv * r * g_ref[...]
        h_ref[...] = h.astype(h_ref.dtype)
        if subs:
            tok_ref = rest[-1]
            for c in range(NC):
                tok_ref[c] = h[:, c * 128:(c + 1) * 128]
            for dil, dst_ref in zip(subs, rest):
                _tok_to_sub(tok_ref, dst_ref, dil)

    row = pl.BlockSpec((ts, D), lambda i: (i, 0))
    outs = pl.pallas_call(
        body, name=name, grid=(S // ts,),
        in_specs=[row, pl.BlockSpec((1, D), lambda i: (0, 0))],
        out_specs=[row] + [_sub_spec(dil, ts, D) for dil in subs],
        out_shape=[jax.ShapeDtypeStruct((S, D), MXU_DTYPE)] + [jax.ShapeDtypeStruct((dil, S // dil, D), MXU_DTYPE) for dil in subs],
        scratch_shapes=[pltpu.VMEM((NC, ts, 128), F32)] if subs else [],
        compiler_params=_params("parallel"),
    )(x, g)
    return outs if subs else outs[0]


def _rms_bwd(x, g, dhs, dres, *, name, dh_subs=()):
    S, D = x.shape
    ts = _pick(S, SUB_TILE, 16 * max([dil for _, dil in dh_subs], default=1))
    n_dh, n_sub = len(dhs), len(dh_subs)
    NC = D // 128

    def body(*refs):
        x_ref, g_ref = refs[0], refs[1]
        dh_refs = refs[2:2 + n_dh]
        sub_refs = refs[2 + n_dh:2 + n_dh + n_sub]
        dres_ref, dx_ref, dxb_ref, dg_ref, cs_ref = refs[2 + n_dh + n_sub:7 + n_dh + n_sub]
        i = pl.program_id(0)
        xv = x_ref[...]
        r = lax.rsqrt(jnp.mean(xv * xv, axis=-1, keepdims=True) + EPS)
        xh = xv * r
        dhv = dh_refs[0][...].astype(F32)
        for t in dh_refs[1:]:
            dhv = dhv + t[...].astype(F32)
        for (_, dil), sub_ref in zip(dh_subs, sub_refs):
            tok_ref = refs[-1]
            _sub_to_tok(sub_ref, tok_ref, dil)
            dhv = dhv + jnp.concatenate([tok_ref[c] for c in range(NC)], axis=1)
        gy = dhv * g_ref[...]
        dx = r * (gy - xh * jnp.mean(gy * xh, axis=-1, keepdims=True)) + dres_ref[...]
        dx_ref[...] = dx
        dxb_ref[...] = dx.astype(dxb_ref.dtype)
        dg = jnp.sum(dhv * xh, axis=0, keepdims=True)
        cs = jnp.sum(dx, axis=0, keepdims=True)

        @pl.when(i == 0)
        def _():
            dg_ref[...] = dg
            cs_ref[...] = cs

        @pl.when(i > 0)
        def _():
            dg_ref[...] += dg
            cs_ref[...] += cs

    row = pl.BlockSpec((ts, D), lambda i: (i, 0))
    vec = pl.BlockSpec((1, D), lambda i: (0, 0))
    return pl.pallas_call(
        body, name=name, grid=(S // ts,),
        in_specs=[row, vec] + [row] * n_dh + [_sub_spec(dil, ts, D) for _, dil in dh_subs] + [row],
        out_specs=[row, row, vec, vec],
        out_shape=[jax.ShapeDtypeStruct((S, D), F32), jax.ShapeDtypeStruct((S, D), MXU_DTYPE),
                   jax.ShapeDtypeStruct((1, D), F32), jax.ShapeDtypeStruct((1, D), F32)],
        scratch_shapes=[pltpu.VMEM((NC, ts, 128), F32)] if n_sub else [],
        compiler_params=_params("arbitrary"),
    )(x, g, *dhs, *[a for a, _ in dh_subs], dres)


def _conv_phases(ph_ref, ts):
    n = ts + CONV_HALO - 8
    for b in range(1, 8):
        ph_ref[b, 0:n, :] = ph_ref[0, pl.ds(b, n), :]


def _phase_taps(base, step=1):
    groups = {}
    for k in range(CONV_KERNEL):
        a, b = divmod(base + step * k, 8)
        groups.setdefault(b, []).append((a, k))
    out = []
    for b in sorted(groups):
        ak = sorted(groups[b])
        assert [a for a, _ in ak] == list(range(ak[0][0], ak[0][0] + len(ak)))
        out.append((b, ak[0][0], [k for _, k in ak]))
    return out


def _cm_fwd(u, dw, dw_b, ln_g, ln_b, *, name):
    S, D2 = u.shape
    D = D2 // 2
    ts = _pick(S, 256, CONV_HALO)
    hb = ts // CONV_HALO

    def body(u_ref, up_ref, dw_ref, dwb_ref, g_ref, b_ref, c_ref, s_ref, ext_ref):
        i = pl.program_id(0)
        prev = up_ref[:, :D] * _sigmoid(up_ref[:, D:])
        ext_ref[0:CONV_HALO, :] = jnp.where(i > 0, prev, 0.0)
        ext_ref[CONV_HALO:CONV_HALO + ts, :] = u_ref[:, :D] * _sigmoid(u_ref[:, D:])
        for cc in range(D // 128):
            sl = slice(cc * 128, (cc + 1) * 128)
            acc = jnp.zeros((ts, 128), F32) + dwb_ref[:, sl]
            for k in range(CONV_KERNEL):
                acc = acc + dw_ref[k:k + 1, sl] * ext_ref[pl.ds(CONV_HALO - (CONV_KERNEL - 1) + k, ts), sl]
            c_ref[:, sl] = acc
        c = c_ref[...]
        mu = jnp.mean(c, axis=-1, keepdims=True)
        xc = c - mu
        rstd = lax.rsqrt(jnp.mean(xc * xc, axis=-1, keepdims=True) + EPS)
        y = xc * rstd * g_ref[...] + b_ref[...]
        s_ref[...] = (y * _sigmoid(y)).astype(s_ref.dtype)

    vec = pl.BlockSpec((1, D), lambda i: (0, 0))
    return pl.pallas_call(
        body, name=name, grid=(S // ts,),
        in_specs=[pl.BlockSpec((ts, D2), lambda i: (i, 0)),
                  pl.BlockSpec((CONV_HALO, D2), lambda i: (jnp.maximum(i * hb - 1, 0), 0)),
                  pl.BlockSpec((CONV_KERNEL, D), lambda i: (0, 0)), vec, vec, vec],
        out_specs=[pl.BlockSpec((ts, D), lambda i: (i, 0)), pl.BlockSpec((ts, D), lambda i: (i, 0))],
        out_shape=[jax.ShapeDtypeStruct((S, D), F32), jax.ShapeDtypeStruct((S, D), MXU_DTYPE)],
        scratch_shapes=[pltpu.VMEM((ts + CONV_HALO, D), F32)],
        compiler_params=_params("parallel"),
    )(u, u, dw, dw_b, ln_g, ln_b)


def _cm_ln_bwd(c, ds, ln_g, ln_b, *, name):
    S, D = c.shape
    ts = _pick(S, 512, 16)

    def body(c_ref, ds_ref, g_ref, b_ref, dc_ref, dg_ref, db_ref):
        i = pl.program_id(0)
        cv = c_ref[...]
        mu = jnp.mean(cv, axis=-1, keepdims=True)
        xc = cv - mu
        rstd = lax.rsqrt(jnp.mean(xc * xc, axis=-1, keepdims=True) + EPS)
        xh = xc * rstd
        y = xh * g_ref[...] + b_ref[...]
        sg = _sigmoid(y)
        dy = ds_ref[...].astype(F32) * (sg * (1.0 + y * (1.0 - sg)))
        gy = dy * g_ref[...]
        dc_ref[...] = rstd * (gy - jnp.mean(gy, axis=-1, keepdims=True) - xh * jnp.mean(gy * xh, axis=-1, keepdims=True))
        dg = jnp.sum(dy * xh, axis=0, keepdims=True)
        db = jnp.sum(dy, axis=0, keepdims=True)

        @pl.when(i == 0)
        def _():
            dg_ref[...] = dg
            db_ref[...] = db

        @pl.when(i > 0)
        def _():
            dg_ref[...] += dg
            db_ref[...] += db

    row = pl.BlockSpec((ts, D), lambda i: (i, 0))
    vec = pl.BlockSpec((1, D), lambda i: (0, 0))
    return pl.pallas_call(
        body, name=name, grid=(S // ts,), in_specs=[row, row, vec, vec], out_specs=[row, vec, vec],
        out_shape=[jax.ShapeDtypeStruct((S, D), F32), jax.ShapeDtypeStruct((1, D), F32), jax.ShapeDtypeStruct((1, D), F32)],
        compiler_params=_params("arbitrary"),
    )(c, ds, ln_g, ln_b)


def _cm_conv_bwd(dc, u, dw, *, name):
    S, D2 = u.shape
    D = D2 // 2
    ts = _pick(S, 256, CONV_HALO)
    hb = ts // CONV_HALO
    n_t = S // ts
    last_h = S // CONV_HALO - 1

    rc = _pick(ts, CONV_ROWS, 8)

    def fold8(v):
        out = v[0:8]
        for j in range(1, v.shape[0] // 8):
            out = out + v[8 * j:8 * j + 8]
        return out

    def body(dc_ref, dcn_ref, u_ref, up_ref, dw_ref, du_ref, ddw_ref, ddwb_ref, dbin_ref, dph_ref, gph_ref, dgl_ref):
        i = pl.program_id(0)
        dph_ref[0, 0:ts, :] = dc_ref[...]
        dph_ref[0, ts:ts + CONV_HALO, :] = jnp.where(i < n_t - 1, dcn_ref[...], 0.0)
        prev = up_ref[:, :D] * _sigmoid(up_ref[:, D:])
        gph_ref[0, 0:CONV_HALO, :] = jnp.where(i > 0, prev, 0.0)
        gph_ref[0, CONV_HALO:CONV_HALO + ts, :] = u_ref[:, :D] * _sigmoid(u_ref[:, D:])
        _conv_phases(dph_ref, ts)
        _conv_phases(gph_ref, ts)

        @pl.when(i == 0)
        def _():
            ddw_ref[...] = jnp.zeros_like(ddw_ref)
            ddwb_ref[...] = jnp.zeros_like(ddwb_ref)
            dbin_ref[...] = jnp.zeros_like(dbin_ref)

        for cc in range(D // 128):
            sl = slice(cc * 128, (cc + 1) * 128)
            sl2 = slice(D + cc * 128, D + (cc + 1) * 128)
            wk = [dw_ref[k:k + 1, sl] for k in range(CONV_KERNEL)]
            acc_a, acc_g = jnp.zeros((8, 128), F32), jnp.zeros((8, 128), F32)
            dgl = jnp.zeros((ts, 128), F32)
            for b, a0, taps in _phase_taps(CONV_KERNEL - 1, -1):
                for j, k in enumerate(taps):
                    dgl = dgl + wk[k] * dph_ref[b, 8 * (a0 + j):8 * (a0 + j) + ts, sl]
            dgl_ref[...] = dgl
            for r0 in range(0, ts, rc):
                dglu = dgl_ref[r0:r0 + rc, :]
                av = u_ref[r0:r0 + rc, sl]
                sg = _sigmoid(u_ref[r0:r0 + rc, sl2])
                da = dglu * sg
                dg = dglu * av * sg * (1.0 - sg)
                du_ref[r0:r0 + rc, sl] = da.astype(du_ref.dtype)
                du_ref[r0:r0 + rc, sl2] = dg.astype(du_ref.dtype)
                acc_a = acc_a + fold8(da)
                acc_g = acc_g + fold8(dg)
            dbin_ref[:, sl] += jnp.sum(acc_a, axis=0, keepdims=True)
            dbin_ref[:, sl2] += jnp.sum(acc_g, axis=0, keepdims=True)
            for gi, (b, a0, taps) in enumerate(_phase_taps(CONV_HALO - (CONV_KERNEL - 1))):
                accs = [jnp.zeros((8, 128), F32) for _ in taps]
                accb = jnp.zeros((8, 128), F32)
                for r0 in range(0, ts, rc):
                    dcc = dph_ref[0, r0:r0 + rc, sl]
                    win = gph_ref[b, 8 * a0 + r0:8 * (a0 + len(taps) - 1) + r0 + rc, sl]
                    for j in range(len(taps)):
                        accs[j] = accs[j] + fold8(dcc * win[8 * j:8 * j + rc])
                    if gi == 0:
                        accb = accb + fold8(dcc)
                for j, k in enumerate(taps):
                    ddw_ref[k:k + 1, sl] += jnp.sum(accs[j], axis=0, keepdims=True)
                if gi == 0:
                    ddwb_ref[:, sl] += jnp.sum(accb, axis=0, keepdims=True)

    return pl.pallas_call(
        body, name=name, grid=(n_t,),
        in_specs=[pl.BlockSpec((ts, D), lambda i: (i, 0)),
                  pl.BlockSpec((CONV_HALO, D), lambda i: (jnp.minimum((i + 1) * hb, last_h), 0)),
                  pl.BlockSpec((ts, D2), lambda i: (i, 0)),
                  pl.BlockSpec((CONV_HALO, D2), lambda i: (jnp.maximum(i * hb - 1, 0), 0)),
                  pl.BlockSpec((CONV_KERNEL, D), lambda i: (0, 0))],
        out_specs=[pl.BlockSpec((ts, D2), lambda i: (i, 0)), pl.BlockSpec((CONV_HALO, D), lambda i: (0, 0)),
                   pl.BlockSpec((1, D), lambda i: (0, 0)), pl.BlockSpec((1, D2), lambda i: (0, 0))],
        out_shape=[jax.ShapeDtypeStruct((S, D2), MXU_DTYPE), jax.ShapeDtypeStruct((CONV_HALO, D), F32),
                   jax.ShapeDtypeStruct((1, D), F32), jax.ShapeDtypeStruct((1, D2), F32)],
        scratch_shapes=[pltpu.VMEM((8, ts + CONV_HALO, D), F32), pltpu.VMEM((8, ts + CONV_HALO, D), F32),
                        pltpu.VMEM((ts, 128), F32)],
        compiler_params=_params("arbitrary"),
    )(dc, dc, u, u, dw)


def _ffn_cols(F2):
    return _pick(F2, 1024, 256)


def _ffn_act_fwd(up, dw, dw_b, *, name):
    S, F2 = up.shape
    ts = _pick(S, 512, 16)
    tc = _ffn_cols(F2)
    hb = ts // FFN_HALO

    rc = _pick(ts, FFN_ROWS, 16)

    def body(u_ref, up_ref, w_ref, b_ref, a_ref, ext_ref):
        i = pl.program_id(1)
        ext_ref[0:FFN_HALO, :] = jnp.where(i > 0, up_ref[...].astype(F32), 0.0)
        ext_ref[FFN_HALO:FFN_HALO + ts, :] = u_ref[...].astype(F32)
        for q in range(tc // 256):
            sls = [slice(q * 256 + half * 128, q * 256 + half * 128 + 128) for half in range(2)]
            wk = [[w_ref[k:k + 1, sl] for k in range(FFN_KERNEL)] for sl in sls]
            bb = [b_ref[:, sl] for sl in sls]
            for r0 in range(0, ts, rc):
                gt, vl = [bb[h] + sum(wk[h][k] * ext_ref[pl.ds(FFN_HALO + r0 - 2 + k, rc), sls[h]] for k in range(FFN_KERNEL))
                          for h in range(2)]
                a_ref[r0:r0 + rc, q * 128:(q + 1) * 128] = (gt * _sigmoid(gt) * vl).astype(a_ref.dtype)

    return pl.pallas_call(
        body, name=name, grid=(F2 // tc, S // ts),
        in_specs=[pl.BlockSpec((ts, tc), lambda j, i: (i, j)),
                  pl.BlockSpec((FFN_HALO, tc), lambda j, i: (jnp.maximum(i * hb - 1, 0), j)),
                  pl.BlockSpec((FFN_KERNEL, tc), lambda j, i: (0, j)),
                  pl.BlockSpec((1, tc), lambda j, i: (0, j))],
        out_specs=pl.BlockSpec((ts, tc // 2), lambda j, i: (i, j)),
        out_shape=jax.ShapeDtypeStruct((S, F2 // 2), MXU_DTYPE),
        scratch_shapes=[pltpu.VMEM((ts + FFN_HALO, tc), F32)],
        compiler_params=_params("parallel", "parallel"),
    )(up, up, dw, dw_b)


def _ffn_act_bwd(up, dact, dw, dw_b, *, name):
    S, F2 = up.shape
    ts = _pick(S, 512, 16)
    tc = _ffn_cols(F2)
    hb = ts // FFN_HALO
    n_t = S // ts
    last_h = S // FFN_HALO - 1
    E = ts + FFN_HALO

    rc = _pick(ts, FFN_ROWS, 16)

    def fold8(v):
        out = v[0:8]
        for j in range(1, v.shape[0] // 8):
            out = out + v[8 * j:8 * j + 8]
        return out

    def body(u_ref, up_ref, un_ref, da_ref, dan_ref, w_ref, b_ref, dup_ref, ddw_ref, ddb_ref, ue_ref, dcv_ref):
        i = pl.program_id(1)
        ue_ref[0:FFN_HALO, :] = jnp.where(i > 0, up_ref[...].astype(F32), 0.0)
        ue_ref[FFN_HALO:FFN_HALO + ts, :] = u_ref[...].astype(F32)
        ue_ref[FFN_HALO + ts:FFN_HALO + ts + FFN_HALO, :] = jnp.where(i < n_t - 1, un_ref[...].astype(F32), 0.0)

        @pl.when(i == 0)
        def _():
            ddw_ref[...] = jnp.zeros_like(ddw_ref)
            ddb_ref[...] = jnp.zeros_like(ddb_ref)

        for q in range(tc // 256):
            sls = [slice(q * 256 + half * 128, q * 256 + half * 128 + 128) for half in range(2)]
            qs = slice(q * 128, (q + 1) * 128)
            wk = [[w_ref[k:k + 1, sl] for k in range(FFN_KERNEL)] for sl in sls]
            bb = [b_ref[:, sl] for sl in sls]
            acc = [[jnp.zeros((8, 128), F32) for _ in range(FFN_KERNEL)] for _ in range(2)]
            accb = [jnp.zeros((8, 128), F32) for _ in range(2)]
            for r0, rows in [(r, rc) for r in range(0, ts, rc)] + [(ts, FFN_HALO)]:
                xs = [[ue_ref[pl.ds(FFN_HALO + r0 - 2 + k, rows), sls[h]] for k in range(FFN_KERNEL)] for h in range(2)]
                gt, vl = [bb[h] + sum(wk[h][k] * xs[h][k] for k in range(FFN_KERNEL)) for h in range(2)]
                sg = _sigmoid(gt)
                if r0 < ts:
                    dae = da_ref[r0:r0 + rows, qs].astype(F32)
                else:
                    dae = jnp.where(i < n_t - 1, dan_ref[:, qs].astype(F32), 0.0)
                dcv = [dae * vl * (sg * (1.0 + gt * (1.0 - sg))), dae * (gt * sg)]
                for h in range(2):
                    dcv_ref[r0:r0 + rows, sls[h]] = dcv[h]
                    if r0 < ts:
                        for k in range(FFN_KERNEL):
                            acc[h][k] = acc[h][k] + fold8(dcv[h] * xs[h][k])
                        accb[h] = accb[h] + fold8(dcv[h])
            for r0 in range(0, ts, rc):
                for h in range(2):
                    dup = sum(wk[h][2 - j] * dcv_ref[pl.ds(r0 + j, rc), sls[h]] for j in range(FFN_KERNEL))
                    dup_ref[r0:r0 + rc, sls[h]] = dup.astype(dup_ref.dtype)
            for h in range(2):
                for k in range(FFN_KERNEL):
                    ddw_ref[k:k + 1, sls[h]] += jnp.sum(acc[h][k], axis=0, keepdims=True)
                ddb_ref[:, sls[h]] += jnp.sum(accb[h], axis=0, keepdims=True)

    return pl.pallas_call(
        body, name=name, grid=(F2 // tc, n_t),
        in_specs=[pl.BlockSpec((ts, tc), lambda j, i: (i, j)),
                  pl.BlockSpec((FFN_HALO, tc), lambda j, i: (jnp.maximum(i * hb - 1, 0), j)),
                  pl.BlockSpec((FFN_HALO, tc), lambda j, i: (jnp.minimum((i + 1) * hb, last_h), j)),
                  pl.BlockSpec((ts, tc // 2), lambda j, i: (i, j)),
                  pl.BlockSpec((FFN_HALO, tc // 2), lambda j, i: (jnp.minimum((i + 1) * hb, last_h), j)),
                  pl.BlockSpec((FFN_KERNEL, tc), lambda j, i: (0, j)),
                  pl.BlockSpec((1, tc), lambda j, i: (0, j))],
        out_specs=[pl.BlockSpec((ts, tc), lambda j, i: (i, j)), pl.BlockSpec((FFN_HALO, tc), lambda j, i: (0, j)),
                   pl.BlockSpec((1, tc), lambda j, i: (0, j))],
        out_shape=[jax.ShapeDtypeStruct((S, F2), MXU_DTYPE), jax.ShapeDtypeStruct((FFN_HALO, F2), F32),
                   jax.ShapeDtypeStruct((1, F2), F32)],
        scratch_shapes=[pltpu.VMEM((ts + 2 * FFN_HALO, tc), F32), pltpu.VMEM((E, tc), F32)],
        compiler_params=_params("parallel", "arbitrary"),
    )(up, up, up, dact, dact, dw, dw_b)


def _slopes(n_heads_total):
    return np.asarray(2.0 ** (-ALIBI_MAX * (np.arange(n_heads_total, dtype=np.float32) + 1.0) / n_heads_total), np.float32)


def _qk_norm(x, gain):
    r = lax.rsqrt(jnp.mean(x * x, axis=-1, keepdims=True) + EPS)
    xh = x * r
    return xh * gain, xh, r


def _band(b, dil):
    qi = lax.broadcasted_iota(jnp.int32, (BLOCK, 2 * BLOCK), 0)
    ki = lax.broadcasted_iota(jnp.int32, (BLOCK, 2 * BLOCK), 1)
    delta = qi + BLOCK - ki
    valid = (delta >= 0) & (delta <= BLOCK) & ((ki >= BLOCK) | (b > 0))
    return valid, (delta * dil).astype(F32)


def _attn_fwd(qkv, qg, kg, *, grp, name):
    S, W = qkv.shape
    D = W // 3
    H = D // HEAD_DIM
    dil = DILATED_GROUPS[grp][1]
    L = S // dil
    nb = L // BLOCK
    slopes = _slopes(3 * H)[grp * H:(grp + 1) * H]
    scale = HEAD_DIM ** -0.5

    def body(q_ref, kp_ref, kc_ref, vp_ref, vc_ref, qg_ref, kg_ref, o_ref, l_ref):
        b = pl.program_id(1)
        valid, dist = _band(b, dil)
        l_ref[...] = jnp.zeros_like(l_ref)
        for h in range(H):
            hs = slice(h * HEAD_DIM, (h + 1) * HEAD_DIM)
            qn = _qk_norm(q_ref[:, hs].astype(F32), qg_ref[h:h + 1, :])[0].astype(MXU_DTYPE)
            kp = _qk_norm(kp_ref[:, hs].astype(F32), kg_ref[h:h + 1, :])[0].astype(MXU_DTYPE)
            kc = _qk_norm(kc_ref[:, hs].astype(F32), kg_ref[h:h + 1, :])[0].astype(MXU_DTYPE)
            k2 = jnp.concatenate([kp, kc], axis=0)
            v2 = jnp.concatenate([vp_ref[:, hs], vc_ref[:, hs]], axis=0).astype(MXU_DTYPE)
            s = lax.dot_general(qn, k2, (((1,), (1,)), ((), ())), preferred_element_type=F32) * scale - float(slopes[h]) * dist
            s = jnp.where(valid, s, NEG)
            m = jnp.max(s, axis=-1, keepdims=True)
            p = jnp.exp(s - m)
            den = jnp.sum(p, axis=-1, keepdims=True)
            o = jnp.dot(p.astype(MXU_DTYPE), v2, preferred_element_type=F32) / den
            o_ref[:, hs] = o.astype(o_ref.dtype)
            l_ref[:, h:h + 1] = m + jnp.log(den)

    def cur(j):
        return lambda r, b: (r * nb + b, j)

    def prv(j):
        return lambda r, b: (r * nb + jnp.maximum(b - 1, 0), j)

    blk = (BLOCK, D)
    gain = pl.BlockSpec((H, HEAD_DIM), lambda r, b: (0, 0))
    return pl.pallas_call(
        body, name=name, grid=(dil, nb),
        in_specs=[pl.BlockSpec(blk, cur(0)), pl.BlockSpec(blk, prv(1)), pl.BlockSpec(blk, cur(1)),
                  pl.BlockSpec(blk, prv(2)), pl.BlockSpec(blk, cur(2)), gain, gain],
        out_specs=[pl.BlockSpec(blk, cur(0)), pl.BlockSpec((BLOCK, HEAD_DIM), cur(0))],
        out_shape=[jax.ShapeDtypeStruct((S, D), MXU_DTYPE), jax.ShapeDtypeStruct((S, HEAD_DIM), F32)],
        compiler_params=_params("parallel", "parallel"),
    )(qkv, qkv, qkv, qkv, qkv, qg, kg)


def _attn_merge(os_, ls_, dils, *, name):
    S, D = os_[0].shape
    H = D // HEAD_DIM
    G = len(dils)
    ts = _pick(S, SUB_TILE, 16 * max(dils))
    subs = [g for g in range(G) if dils[g] > 1]

    def body(*refs):
        o_refs, l_refs = refs[0:G], refs[G:2 * G]
        outb_ref = refs[2 * G]
        lt_refs = refs[2 * G + 1:3 * G + 1]
        scratch = refs[3 * G + 1:]
        lt_tok = scratch[0]
        o_tok = {g: scratch[1 + 2 * j] for j, g in enumerate(subs)}
        l_tok = {g: scratch[2 + 2 * j] for j, g in enumerate(subs)}
        for g in subs:
            _sub_to_tok(o_refs[g], o_tok[g], dils[g])
            _sub_to_tok(l_refs[g], l_tok[g], dils[g])
        lt_tok[0] = jnp.zeros((ts, HEAD_DIM), F32)
        for h in range(H):
            hs = slice(h * HEAD_DIM, (h + 1) * HEAD_DIM)
            ls = [l_tok[g][0][:, h:h + 1] if g in subs else l_refs[g][:, h:h + 1] for g in range(G)]
            ov = [o_tok[g][h] if g in subs else o_refs[g][:, hs].astype(F32) for g in range(G)]
            m = functools.reduce(jnp.maximum, ls)
            es = [jnp.exp(l - m) for l in ls]
            den = functools.reduce(lambda a, b: a + b, es)
            out = functools.reduce(lambda a, b: a + b, [e * o for e, o in zip(es, ov)]) / den
            outb_ref[:, hs] = out.astype(outb_ref.dtype)
            lt_tok.at[0][:, h:h + 1] = m + jnp.log(den)
        for g in range(G):
            if g in subs:
                _tok_to_sub(lt_tok, lt_refs[g], dils[g])
            else:
                lt_refs[g][...] = lt_tok[0]

    def spec(g, cols):
        return _sub_spec(dils[g], ts, cols) if g in subs else pl.BlockSpec((ts, cols), lambda i: (i, 0))

    def shape(g, cols, dtype):
        return jax.ShapeDtypeStruct((dils[g], S // dils[g], cols) if g in subs else (S, cols), dtype)

    def view(a, g):
        return a.reshape(dils[g], S // dils[g], a.shape[-1]) if g in subs else a

    outs = pl.pallas_call(
        body, name=name, grid=(S // ts,),
        in_specs=[spec(g, D) for g in range(G)] + [spec(g, HEAD_DIM) for g in range(G)],
        out_specs=[pl.BlockSpec((ts, D), lambda i: (i, 0))] + [spec(g, HEAD_DIM) for g in range(G)],
        out_shape=[jax.ShapeDtypeStruct((S, D), MXU_DTYPE)] + [shape(g, HEAD_DIM, F32) for g in range(G)],
        scratch_shapes=[pltpu.VMEM((1, ts, HEAD_DIM), F32)] + [pltpu.VMEM((H, ts, HEAD_DIM), F32), pltpu.VMEM((1, ts, HEAD_DIM), F32)] * len(subs),
        compiler_params=_params("parallel"),
    )(*[view(o, g) for g, o in enumerate(os_)], *[view(l, g) for g, l in enumerate(ls_)])
    return outs[0], [t.reshape(S, HEAD_DIM) for t in outs[1:]]


def _attn_delta(do, out, dils, *, name):
    S, D = out.shape
    H = D // HEAD_DIM
    G = len(dils)
    ts = _pick(S, SUB_TILE, 16 * max(dils))
    subs = [g for g in range(G) if dils[g] > 1]

    def body(do_ref, o_ref, *rest):
        d_refs = rest[0:G]
        dos_refs = rest[G:G + len(subs)]
        d_tok, do_tok = rest[G + len(subs):]
        d_tok[0] = jnp.zeros((ts, HEAD_DIM), F32)
        for h in range(H):
            hs = slice(h * HEAD_DIM, (h + 1) * HEAD_DIM)
            dov = do_ref[:, hs].astype(F32)
            do_tok[h] = dov
            d_tok.at[0][:, h:h + 1] = jnp.sum(dov * o_ref[:, hs].astype(F32), axis=-1, keepdims=True)
        for g in range(G):
            if g in subs:
                _tok_to_sub(d_tok, d_refs[g], dils[g])
            else:
                d_refs[g][...] = d_tok[0]
        for g, dst in zip(subs, dos_refs):
            _tok_to_sub(do_tok, dst, dils[g])

    def spec(g, cols):
        return _sub_spec(dils[g], ts, cols) if g in subs else pl.BlockSpec((ts, cols), lambda i: (i, 0))

    def shape(g, cols, dtype):
        return jax.ShapeDtypeStruct((dils[g], S // dils[g], cols) if g in subs else (S, cols), dtype)

    row = pl.BlockSpec((ts, D), lambda i: (i, 0))
    outs = pl.pallas_call(
        body, name=name, grid=(S // ts,), in_specs=[row, row],
        out_specs=[spec(g, HEAD_DIM) for g in range(G)] + [spec(g, D) for g in subs],
        out_shape=[shape(g, HEAD_DIM, F32) for g in range(G)] + [shape(g, D, do.dtype) for g in subs],
        scratch_shapes=[pltpu.VMEM((1, ts, HEAD_DIM), F32), pltpu.VMEM((H, ts, HEAD_DIM), F32)],
        compiler_params=_params("parallel"),
    )(do, out)
    deltas = [t.reshape(S, HEAD_DIM) for t in outs[0:G]]
    dos = {g: t.reshape(S, D) for g, t in zip(subs, outs[G:])}
    return deltas, [dos[g] if g in subs else do for g in range(G)]


def _attn_bwd(qkv, do, lse, delta, qg, kg, *, grp, name):
    S, W = qkv.shape
    D = W // 3
    H = D // HEAD_DIM
    dil = DILATED_GROUPS[grp][1]
    L = S // dil
    nb = L // BLOCK
    slopes = _slopes(3 * H)[grp * H:(grp + 1) * H]
    scale = HEAD_DIM ** -0.5

    def body(q_ref, qp_ref, kp_ref, kc_ref, vp_ref, vc_ref, do_ref, l_ref, dl_ref, qg_ref, kg_ref,
             out_ref, dqg_ref, dkg_ref, cq_ref, ck_ref, cv_ref, nq_ref, nk_ref, nv_ref, pk_ref, pv_ref):
        r = pl.program_id(0)
        b = pl.program_id(1)

        @pl.when(jnp.logical_and(r == 0, b == 0))
        def _():
            dqg_ref[...] = jnp.zeros_like(dqg_ref)
            dkg_ref[...] = jnp.zeros_like(dkg_ref)

        @pl.when(b < nb)
        def _():
            valid, dist = _band(b, dil)
            for h in range(H):
                hs = slice(h * HEAD_DIM, (h + 1) * HEAD_DIM)
                qn = _qk_norm(q_ref[:, hs].astype(F32), qg_ref[h:h + 1, :])[0].astype(MXU_DTYPE)
                kp = _qk_norm(kp_ref[:, hs].astype(F32), kg_ref[h:h + 1, :])[0].astype(MXU_DTYPE)
                kc = _qk_norm(kc_ref[:, hs].astype(F32), kg_ref[h:h + 1, :])[0].astype(MXU_DTYPE)
                k2 = jnp.concatenate([kp, kc], axis=0)
                v2 = jnp.concatenate([vp_ref[:, hs], vc_ref[:, hs]], axis=0).astype(MXU_DTYPE)
                doh = do_ref[:, hs].astype(MXU_DTYPE)
                s = lax.dot_general(qn, k2, (((1,), (1,)), ((), ())), preferred_element_type=F32) * scale - float(slopes[h]) * dist
                s = jnp.where(valid, s, NEG)
                p = jnp.exp(s - l_ref[:, h:h + 1])
                dp = lax.dot_general(doh, v2, (((1,), (1,)), ((), ())), preferred_element_type=F32)
                dsc = (p * (dp - dl_ref[:, h:h + 1]) * scale).astype(MXU_DTYPE)
                nq_ref[:, hs] = jnp.dot(dsc, k2, preferred_element_type=F32)
                dk2 = lax.dot_general(dsc, qn, (((0,), (0,)), ((), ())), preferred_element_type=F32)
                dv2 = lax.dot_general(p.astype(MXU_DTYPE), doh, (((0,), (0,)), ((), ())), preferred_element_type=F32)
                pk_ref[:, hs] = dk2[0:BLOCK]
                nk_ref[:, hs] = dk2[BLOCK:2 * BLOCK]
                pv_ref[:, hs] = dv2[0:BLOCK]
                nv_ref[:, hs] = dv2[BLOCK:2 * BLOCK]

        @pl.when(b == nb)
        def _():
            pk_ref[...] = jnp.zeros_like(pk_ref)
            pv_ref[...] = jnp.zeros_like(pv_ref)

        @pl.when(b > 0)
        def _():
            for h in range(H):
                hs = slice(h * HEAD_DIM, (h + 1) * HEAD_DIM)
                for j, (raw_ref, gain_ref, dgain_ref) in enumerate(((qp_ref, qg_ref, dqg_ref), (kp_ref, kg_ref, dkg_ref))):
                    dy = cq_ref[:, hs] if j == 0 else ck_ref[:, hs] + pk_ref[:, hs]
                    gain = gain_ref[h:h + 1, :]
                    _, xh, rr = _qk_norm(raw_ref[:, hs].astype(F32), gain)
                    gy = dy * gain
                    dx = rr * (gy - xh * jnp.mean(gy * xh, axis=-1, keepdims=True))
                    out_ref[:, j * D + h * HEAD_DIM:j * D + (h + 1) * HEAD_DIM] = dx.astype(out_ref.dtype)
                    dgain_ref[h:h + 1, :] += jnp.sum(dy * xh, axis=0, keepdims=True)
                out_ref[:, 2 * D + h * HEAD_DIM:2 * D + (h + 1) * HEAD_DIM] = (cv_ref[:, hs] + pv_ref[:, hs]).astype(out_ref.dtype)

        @pl.when(b < nb)
        def _():
            cq_ref[...] = nq_ref[...]
            ck_ref[...] = nk_ref[...]
            cv_ref[...] = nv_ref[...]

    def cur(j):
        return lambda r, b: (r * nb + jnp.minimum(b, nb - 1), j)

    def prv(j):
        return lambda r, b: (r * nb + jnp.clip(b - 1, 0, nb - 1), j)

    blk = (BLOCK, D)
    lblk = pl.BlockSpec((BLOCK, HEAD_DIM), cur(0))
    gain = pl.BlockSpec((H, HEAD_DIM), lambda r, b: (0, 0))
    return pl.pallas_call(
        body, name=name, grid=(dil, nb + 1),
        in_specs=[pl.BlockSpec(blk, cur(0)), pl.BlockSpec(blk, prv(0)), pl.BlockSpec(blk, prv(1)), pl.BlockSpec(blk, cur(1)),
                  pl.BlockSpec(blk, prv(2)), pl.BlockSpec(blk, cur(2)), pl.BlockSpec(blk, cur(0)), lblk, lblk, gain, gain],
        out_specs=[pl.BlockSpec((BLOCK, 3 * D), lambda r, b: (r * nb + jnp.maximum(b - 1, 0), 0)), gain, gain],
        out_shape=[jax.ShapeDtypeStruct((S, 3 * D), MXU_DTYPE), jax.ShapeDtypeStruct((H, HEAD_DIM), F32),
                   jax.ShapeDtypeStruct((H, HEAD_DIM), F32)],
        scratch_shapes=[pltpu.VMEM(blk, F32)] * 8,
        compiler_params=_params("arbitrary", "arbitrary"),
    )(qkv, qkv, qkv, qkv, qkv, qkv, do, lse, delta, qg, kg)


def _loss_head(y, target, *, name):
    S, D = y.shape
    ts = _pick(S, 512, 16)

    def body(y_ref, t_ref, dy_ref, dyb_ref, l_ref, acc_ref):
        i = pl.program_id(0)
        e = y_ref[...] - t_ref[...]
        dy = e * (1.0 / D)
        dy_ref[...] = dy
        dyb_ref[...] = dy.astype(dyb_ref.dtype)
        part = jnp.sum(e * e, axis=0, keepdims=True)

        @pl.when(i == 0)
        def _():
            acc_ref[...] = part

        @pl.when(i > 0)
        def _():
            acc_ref[...] += part

        @pl.when(i == pl.num_programs(0) - 1)
        def _():
            l_ref[...] = jnp.broadcast_to(jnp.sum(acc_ref[...], axis=-1, keepdims=True) * (0.5 / D), l_ref.shape)

    row = pl.BlockSpec((ts, D), lambda i: (i, 0))
    return pl.pallas_call(
        body, name=name, grid=(S // ts,), in_specs=[row, row],
        out_specs=[row, row, pl.BlockSpec((8, 128), lambda i: (0, 0))],
        out_shape=[jax.ShapeDtypeStruct((S, D), F32), jax.ShapeDtypeStruct((S, D), MXU_DTYPE), jax.ShapeDtypeStruct((8, 128), F32)],
        scratch_shapes=[pltpu.VMEM((1, D), F32)],
        compiler_params=_params("arbitrary"),
    )(y, target)


def _adamw(w, m, v, terms, slots, *, name):
    R, C = w.shape
    nt = len(terms)
    tr = _pick(R, 256, 16)
    c1 = 1.0 - ADAM_B1 ** ADAM_STEP
    c2 = 1.0 - ADAM_B2 ** ADAM_STEP

    def body(slot_ref, w_ref, m_ref, v_ref, *rest):
        t_refs = rest[:nt]
        g_ref, d_ref, nm_ref, nv_ref = rest[nt:]
        g = t_refs[0][...].astype(F32)
        for t in t_refs[1:]:
            g = g + t[...].astype(F32)
        mm = ADAM_B1 * m_ref[...] + (1.0 - ADAM_B1) * g
        vv = ADAM_B2 * v_ref[...] + (1.0 - ADAM_B2) * (g * g)
        m_hat = mm / c1
        v_hat = vv / c2
        g_ref[...] = g
        d_ref[...] = -ADAM_LR * (m_hat / (jnp.sqrt(v_hat) + ADAM_EPS) + ADAM_WD * w_ref[...])
        nm_ref[...] = mm
        nv_ref[...] = vv

    row = pl.BlockSpec((tr, C), lambda i, s: (i, 0))
    grid_spec = pltpu.PrefetchScalarGridSpec(
        num_scalar_prefetch=1, grid=(R // tr,),
        in_specs=[row, row, row] + [pl.BlockSpec((None, tr, C), lambda i, s, t=t: (s[t], i, 0)) for t in range(nt)],
        out_specs=[row] * 4)
    return pl.pallas_call(
        body, name=name, grid_spec=grid_spec, out_shape=[jax.ShapeDtypeStruct((R, C), F32)] * 4,
        compiler_params=_params("parallel"),
    )(slots, w, m, v, *terms)


def _chip_partials(g, sib, core, *, name):
    _, R, C = g.shape
    tr = _pick(R, 1024, 16)

    def body(core_ref, g_ref, s_ref, o_ref):
        o_ref[...] = (g_ref[...] + s_ref[...].astype(F32)).astype(o_ref.dtype)

    grid_spec = pltpu.PrefetchScalarGridSpec(
        num_scalar_prefetch=1, grid=(4, R // tr),
        in_specs=[pl.BlockSpec((None, tr, C), lambda k, i, c: (2 * k + c[0], i, 0)),
                  pl.BlockSpec((None, tr, C), lambda k, i, c: (k, i, 0))],
        out_specs=pl.BlockSpec((None, tr, C), lambda k, i, c: (k, i, 0)))
    return pl.pallas_call(
        body, name=name, grid_spec=grid_spec, out_shape=jax.ShapeDtypeStruct((4, R, C), sib.dtype),
        compiler_params=_params("parallel", "parallel"),
    )(core, g, sib)


_ANY = pl.BlockSpec(memory_space=pl.ANY)


def _place():
    return lax.axis_index("x"), lax.axis_index("y"), lax.axis_index("c")


def _all_gather(shard, *, name):
    R, C = shard.shape

    def body(x_ref, out_ref, send_sems, recv_sems, local_sem):
        x, y, c = _place()
        me, sibling = (x, y, c), (x, y, 1 - c)
        chips = [(1 - x, y), (x, 1 - y), (1 - x, 1 - y)]

        def slot(px, py, pc):
            return out_ref.at[4 * px + 2 * py + pc]

        def copy(k, block, to, src=None):
            return pltpu.make_async_remote_copy(
                src_ref=slot(*block) if src is None else src, dst_ref=slot(*block),
                send_sem=send_sems.at[k], recv_sem=recv_sems.at[k], device_id=to, device_id_type=MESH)

        mine = pltpu.make_async_copy(x_ref, slot(*me), local_sem)
        mine.start()
        first = [copy(0, me, sibling, src=x_ref)]
        first += [copy(1 + j, me, (*chip, c), src=x_ref) for j, chip in enumerate(chips)]
        for cp in first:
            cp.start()
        passed = [copy(4 + j, (*chip, c), sibling) for j, chip in enumerate(chips)]
        for j, chip in enumerate(chips):
            copy(1 + j, (*chip, c), me).wait_recv()
            passed[j].start()
        copy(0, sibling, me).wait_recv()
        for j, chip in enumerate(chips):
            copy(4 + j, (*chip, 1 - c), me).wait_recv()
        for cp in first + passed:
            cp.wait_send()
        mine.wait()

    return pl.pallas_call(
        body, name=name, in_specs=[_ANY], out_specs=_ANY,
        out_shape=jax.ShapeDtypeStruct((N_DEV, R, C), shard.dtype),
        scratch_shapes=[pltpu.SemaphoreType.DMA((7,)), pltpu.SemaphoreType.DMA((7,)), pltpu.SemaphoreType.DMA],
    )(shard)


def _rs_sibling(g, *, name):
    _, R, C = g.shape

    def body(g_ref, sib_ref, send_sems, recv_sems):
        x, y, c = _place()
        sends = [pltpu.make_async_remote_copy(
            src_ref=g_ref.at[2 * k + (1 - c)], dst_ref=sib_ref.at[k], send_sem=send_sems.at[k], recv_sem=recv_sems.at[k],
            device_id=(x, y, 1 - c), device_id_type=MESH) for k in range(4)]
        for cp in sends:
            cp.start()
        for cp in sends:
            cp.wait_recv()
        for cp in sends:
            cp.wait_send()

    return pl.pallas_call(
        body, name=name, in_specs=[_ANY], out_specs=_ANY, out_shape=jax.ShapeDtypeStruct((4, R, C), g.dtype),
        scratch_shapes=[pltpu.SemaphoreType.DMA((4,)), pltpu.SemaphoreType.DMA((4,))],
    )(g)


def _rs_chips(part, *, name):
    _, R, C = part.shape

    def body(p_ref, out_ref, send_sems, recv_sems):
        x, y, c = _place()
        chips = [(1 - x, y), (x, 1 - y), (1 - x, 1 - y)]
        sends = [pltpu.make_async_remote_copy(
            src_ref=p_ref.at[2 * px + py], dst_ref=out_ref.at[j], send_sem=send_sems.at[j], recv_sem=recv_sems.at[j],
            device_id=(px, py, c), device_id_type=MESH) for j, (px, py) in enumerate(chips)]
        for cp in sends:
            cp.start()
        for cp in sends:
            cp.wait_recv()
        for cp in sends:
            cp.wait_send()

    return pl.pallas_call(
        body, name=name, in_specs=[_ANY], out_specs=_ANY, out_shape=jax.ShapeDtypeStruct((3, R, C), part.dtype),
        scratch_shapes=[pltpu.SemaphoreType.DMA((3,)), pltpu.SemaphoreType.DMA((3,))],
    )(part)


def _interleave_rows(wt):
    F2, D = wt.shape
    return wt.reshape(2, F2 // 256, 128, D).transpose(1, 0, 2, 3).reshape(F2, D)


def _deinterleave_rows(wt):
    F2, D = wt.shape
    return wt.reshape(F2 // 256, 2, 128, D).transpose(1, 0, 2, 3).reshape(F2, D)


def _interleave_cols(v):
    k, F2 = v.shape
    return v.reshape(k, 2, F2 // 256, 128).transpose(0, 2, 1, 3).reshape(k, F2)


def _deinterleave_cols(v):
    k, F2 = v.shape
    return v.reshape(k, F2 // 256, 2, 128).transpose(0, 2, 1, 3).reshape(k, F2)


def _pack_rows(parts):
    return jnp.concatenate(parts, axis=0)


def _flat_pack(parts, width):
    flat = jnp.concatenate([p.reshape(-1) for p in parts])
    pad = (-flat.shape[0]) % (8 * width)
    return jnp.pad(flat, (0, pad)).reshape(-1, width)


def _flat_unpack(packed, shapes):
    flat = packed.reshape(-1)
    out, off = [], 0
    for shp in shapes:
        n = int(np.prod(shp))
        out.append(flat[off:off + n].reshape(shp))
        off += n
    return out


def _ffn_forward(x, g_ffn, wupT, wdown, dw_i, dwb_i, tag):
    hf = _rms_fwd(x, g_ffn, name=f"ffn{tag}_rms")
    up = _mm(hf, wupT, mode="nt", out_dtype=MXU_DTYPE, name=f"ffn{tag}_up", tm=2048, tn=512)
    act = _ffn_act_fwd(up, dw_i, dwb_i, name=f"ffn{tag}_act")
    y = _mm(act, wdown, mode="nn", out_dtype=F32, name=f"ffn{tag}_down", residual=x)
    return y, (hf, up, act)


def _ffn_backward(x, g_ffn, wupT, wdown, dw_i, dwb_i, saved, dy, dyb, tag):
    hf, up, act = saved
    dact = _mm(dyb, wdown, mode="nt", out_dtype=MXU_DTYPE, name=f"ffn{tag}_dact", tm=1024, tn=1408)
    d_wdown = _mm(act, dyb, mode="tn", out_dtype=F32, name=f"ffn{tag}_dwdown", tm=1408, tk=2048)
    dup, d_dw_i, d_dwb_i = _ffn_act_bwd(up, dact, dw_i, dwb_i, name=f"ffn{tag}_actbwd")
    dhf = _mm(dup, wupT, mode="nn", out_dtype=F32, name=f"ffn{tag}_dhf", tm=512)
    d_wupT = _mm(dup, hf, mode="tn", out_dtype=F32, name=f"ffn{tag}_dwup", tm=1408, tk=2048)
    dx, dxb, dg, cs = _rms_bwd(x, g_ffn, [dhf], dy, name=f"ffn{tag}_rmsbwd")
    return dx, dxb, cs, dict(w_upT=d_wupT, w_down=d_wdown, dw=d_dw_i[0:FFN_KERNEL], dw_b=d_dwb_i, norm=dg)


def _local_step(x, target, p):
    S, D = x.shape
    H = D // HEAD_DIM
    h0 = _rms_fwd(x, p["norm_mix"][0:1], name="l0_rms")
    u = _mm(h0, p["w_inT"], mode="nt", out_dtype=F32, name="l0_in", bias=p["cm_b_in"])
    c, s = _cm_fwd(u, p["cm_dw"], p["cm_dw_b"], p["cm_ln_g"], p["cm_ln_b"], name="l0_conv")
    x1 = _mm(s, p["w_out"], mode="nn", out_dtype=F32, name="l0_out", bias=p["cm_b_out"], residual=x)
    x2, sv0 = _ffn_forward(x1, p["norm_ffn"][0:1], p["w_upT"][0], p["w_down"][0], p["ff_dw"][0], p["ff_dw_b"][0:1], 0)
    dils = [dil for _, dil in DILATED_GROUPS]
    assert dils[0] == 1
    h1s = [t.reshape(S, D) for t in _rms_fwd(x2, p["norm_mix"][1:2], name="l1_rms", subs=tuple(dils[1:]))]
    qkvs, os_, ls_ = [], [], []
    for g in range(len(dils)):
        qkvs.append(_mm(h1s[g], p["w_qkvT"], mode="nt", out_dtype=MXU_DTYPE, name=f"l1_qkv{g}", b_off=g * 3 * D, b_len=3 * D))
        o, l = _attn_fwd(qkvs[g], p["at_q_norm"][g * H:(g + 1) * H], p["at_k_norm"][g * H:(g + 1) * H], grp=g, name=f"l1_attn{g}")
        os_.append(o)
        ls_.append(l)
    outb, lses = _attn_merge(os_, ls_, dils, name="l1_merge")
    x3 = _mm(outb, p["w_o"], mode="nn", out_dtype=F32, name="l1_o", residual=x2)
    x4, sv1 = _ffn_forward(x3, p["norm_ffn"][1:2], p["w_upT"][1], p["w_down"][1], p["ff_dw"][1], p["ff_dw_b"][1:2], 1)
    dx4, dx4b, loss = _loss_head(x4, target, name="loss")
    dx3, dx3b, _, gf1 = _ffn_backward(x3, p["norm_ffn"][1:2], p["w_upT"][1], p["w_down"][1], p["ff_dw"][1], p["ff_dw_b"][1:2],
                                      sv1, dx4, dx4b, 1)
    do = _mm(dx3b, p["w_o"], mode="nt", out_dtype=MXU_DTYPE, name="l1_do")
    d_wo = _mm(outb, dx3b, mode="tn", out_dtype=F32, name="l1_dwo", tk=2048)
    deltas, dos = _attn_delta(do, outb, dils, name="l1_delta")
    dh1s, d_wqkvT, dqg, dkg = [], [], [], []
    for g, dil in enumerate(dils):
        dqkv_g, a, b_ = _attn_bwd(qkvs[g], dos[g], lses[g], deltas[g], p["at_q_norm"][g * H:(g + 1) * H],
                                  p["at_k_norm"][g * H:(g + 1) * H], grp=g, name=f"l1_attnbwd{g}")
        dqg.append(a)
        dkg.append(b_)
        d_wqkvT.append(_mm(dqkv_g, h1s[g], mode="tn", out_dtype=F32, name=f"l1_dwqkv{g}", tk=2048))
        dh1s.append(_mm(dqkv_g, p["w_qkvT"], mode="nn", out_dtype=F32, name=f"l1_dh{g}", b_off=g * 3 * D, b_len=3 * D))
    dx2, dx2b, dgm1, _ = _rms_bwd(x2, p["norm_mix"][1:2], dh1s[0:1], dx3, name="l1_rmsbwd",
                                  dh_subs=[(dh1s[g].reshape(dils[g], S // dils[g], D), dils[g]) for g in range(1, len(dils))])
    dx1, dx1b, cs1, gf0 = _ffn_backward(x1, p["norm_ffn"][0:1], p["w_upT"][0], p["w_down"][0], p["ff_dw"][0], p["ff_dw_b"][0:1],
                                        sv0, dx2, dx2b, 0)
    ds = _mm(dx1b, p["w_out"], mode="nt", out_dtype=F32, name="l0_ds")
    d_wout = _mm(s, dx1b, mode="tn", out_dtype=F32, name="l0_dwout", tk=2048)
    dc, d_lng, d_lnb = _cm_ln_bwd(c, ds, p["cm_ln_g"], p["cm_ln_b"], name="l0_lnbwd")
    du, d_cmdw, d_cmdwb, d_bin = _cm_conv_bwd(dc, u, p["cm_dw"], name="l0_convbwd")
    dh0 = _mm(du, p["w_inT"], mode="nn", out_dtype=F32, name="l0_dh")
    d_winT = _mm(du, h0, mode="tn", out_dtype=F32, name="l0_dwin", tk=2048)
    grad_x, _, dgm0, _ = _rms_bwd(x, p["norm_mix"][0:1], [dh0], dx1, name="l0_rmsbwd")
    grads = dict(
        norm_mix=jnp.concatenate([dgm0, dgm1], axis=0),
        norm_ffn=jnp.concatenate([gf0["norm"], gf1["norm"]], axis=0),
        w_inT=d_winT, cm_b_in=d_bin, cm_dw=d_cmdw[0:CONV_KERNEL], cm_dw_b=d_cmdwb, cm_ln_g=d_lng, cm_ln_b=d_lnb,
        w_out=d_wout, cm_b_out=cs1,
        w_qkvT=jnp.concatenate(d_wqkvT, axis=0), at_q_norm=jnp.concatenate(dqg, axis=0), at_k_norm=jnp.concatenate(dkg, axis=0),
        w_o=d_wo,
        w_upT=[gf0["w_upT"], gf1["w_upT"]], w_down=[gf0["w_down"], gf1["w_down"]],
        ff_dw=jnp.stack([gf0["dw"], gf1["dw"]]), ff_dw_b=jnp.concatenate([gf0["dw_b"], gf1["dw_b"]], axis=0),
    )
    return loss, grad_x, grads


_BIG = ("cm_w_in", "cm_w_out", "at_w_qkv", "at_w_out", "ff_w_up", "ff_w_down")
_TRANSPOSED = ("cm_w_in", "at_w_qkv", "ff_w_up")
_SMALL = ("norm_mix", "norm_ffn", "cm_b_in", "cm_dw_b", "cm_ln_g", "cm_ln_b", "cm_b_out", "at_q_norm", "at_k_norm",
          "ff_dw_b", "cm_dw", "ff_dw")
_SMALL_SHARDED = ("cm_dw", "ff_dw")
_ORDER = ("norm_mix", "norm_ffn", "cm_w_in", "cm_b_in", "cm_dw", "cm_dw_b", "cm_ln_g", "cm_ln_b", "cm_w_out", "cm_b_out",
          "at_w_qkv", "at_q_norm", "at_k_norm", "at_w_out", "ff_w_up", "ff_dw", "ff_dw_b", "ff_w_down")


def _big_rows(t):
    parts = []
    for n in _BIG:
        a = t[n]
        mats = [a[l] for l in range(a.shape[0])]
        if n in _TRANSPOSED:
            mats = [m.T for m in mats]
        parts += mats
    return _pack_rows(parts)


def _big_unrows(packed, like):
    out, off = {}, 0
    for n in _BIG:
        a = like[n]
        mats = []
        for l in range(a.shape[0]):
            rows, cols = (a.shape[2], a.shape[1]) if n in _TRANSPOSED else (a.shape[1], a.shape[2])
            m = packed[off:off + rows]
            off += rows
            mats.append(m.T if n in _TRANSPOSED else m)
        out[n] = jnp.stack(mats)
    return out


def kernel(x, norm_mix, norm_ffn, cm_w_in, cm_b_in, cm_dw, cm_dw_b, cm_ln_g, cm_ln_b, cm_w_out, cm_b_out, at_w_qkv, at_q_norm, at_k_norm, at_w_out, ff_w_up, ff_dw, ff_dw_b, ff_w_down, loss_target, m_norm_mix, m_norm_ffn, m_cm_w_in, m_cm_b_in, m_cm_dw, m_cm_dw_b, m_cm_ln_g, m_cm_ln_b, m_cm_w_out, m_cm_b_out, m_at_w_qkv, m_at_q_norm, m_at_k_norm, m_at_w_out, m_ff_w_up, m_ff_dw, m_ff_dw_b, m_ff_w_down, v_norm_mix, v_norm_ffn, v_cm_w_in, v_cm_b_in, v_cm_dw, v_cm_dw_b, v_cm_ln_g, v_cm_ln_b, v_cm_w_out, v_cm_b_out, v_at_w_qkv, v_at_q_norm, v_at_k_norm, v_at_w_out, v_ff_w_up, v_ff_dw, v_ff_dw_b, v_ff_w_down):
    w = dict(norm_mix=norm_mix, norm_ffn=norm_ffn, cm_w_in=cm_w_in, cm_b_in=cm_b_in, cm_dw=cm_dw, cm_dw_b=cm_dw_b, cm_ln_g=cm_ln_g,
             cm_ln_b=cm_ln_b, cm_w_out=cm_w_out, cm_b_out=cm_b_out, at_w_qkv=at_w_qkv, at_q_norm=at_q_norm, at_k_norm=at_k_norm,
             at_w_out=at_w_out, ff_w_up=ff_w_up, ff_dw=ff_dw, ff_dw_b=ff_dw_b, ff_w_down=ff_w_down)
    m = dict(norm_mix=m_norm_mix, norm_ffn=m_norm_ffn, cm_w_in=m_cm_w_in, cm_b_in=m_cm_b_in, cm_dw=m_cm_dw, cm_dw_b=m_cm_dw_b,
             cm_ln_g=m_cm_ln_g, cm_ln_b=m_cm_ln_b, cm_w_out=m_cm_w_out, cm_b_out=m_cm_b_out, at_w_qkv=m_at_w_qkv,
             at_q_norm=m_at_q_norm, at_k_norm=m_at_k_norm, at_w_out=m_at_w_out, ff_w_up=m_ff_w_up, ff_dw=m_ff_dw,
             ff_dw_b=m_ff_dw_b, ff_w_down=m_ff_w_down)
    v = dict(norm_mix=v_norm_mix, norm_ffn=v_norm_ffn, cm_w_in=v_cm_w_in, cm_b_in=v_cm_b_in, cm_dw=v_cm_dw, cm_dw_b=v_cm_dw_b,
             cm_ln_g=v_cm_ln_g, cm_ln_b=v_cm_ln_b, cm_w_out=v_cm_w_out, cm_b_out=v_cm_b_out, at_w_qkv=v_at_w_qkv,
             at_q_norm=v_at_q_norm, at_k_norm=v_at_k_norm, at_w_out=v_at_w_out, ff_w_up=v_ff_w_up, ff_dw=v_ff_dw,
             ff_dw_b=v_ff_dw_b, ff_w_down=v_ff_w_down)
    S, D = x.shape[1], x.shape[2]
    F2 = ff_dw_b.shape[1]
    H3 = at_q_norm.shape[1]
    me = 4 * lax.axis_index("x") + 2 * lax.axis_index("y") + lax.axis_index("c")

    w_rows = _big_rows(w)
    gathered = _all_gather(w_rows.astype(MXU_DTYPE), name="gather_weights")
    full, off = {}, 0
    for n in _BIG:
        a = w[n]
        mats = []
        for l in range(a.shape[0]):
            rows = a.shape[2] if n in _TRANSPOSED else a.shape[1]
            mats.append(gathered[:, off:off + rows, :].reshape(N_DEV * rows, D))
            off += rows
        full[n] = mats
    small_sh = _flat_pack([cm_dw, ff_dw], D)
    small_g = _all_gather(small_sh, name="gather_small")
    cm_dw_full = jnp.concatenate([_flat_unpack(small_g[j], [cm_dw.shape, ff_dw.shape])[0][0] for j in range(N_DEV)], axis=-1)
    ff_dw_full = jnp.concatenate([_flat_unpack(small_g[j], [cm_dw.shape, ff_dw.shape])[1] for j in range(N_DEV)], axis=-1)

    p = dict(
        norm_mix=norm_mix, norm_ffn=norm_ffn, cm_b_in=cm_b_in, cm_dw=cm_dw_full, cm_dw_b=cm_dw_b, cm_ln_g=cm_ln_g, cm_ln_b=cm_ln_b,
        cm_b_out=cm_b_out, at_q_norm=at_q_norm[0], at_k_norm=at_k_norm[0],
        w_inT=full["cm_w_in"][0], w_out=full["cm_w_out"][0], w_qkvT=full["at_w_qkv"][0], w_o=full["at_w_out"][0],
        w_upT=[_interleave_rows(t) for t in full["ff_w_up"]], w_down=full["ff_w_down"],
        ff_dw=jnp.stack([_interleave_cols(ff_dw_full[l]) for l in range(ff_dw_full.shape[0])]),
        ff_dw_b=_interleave_cols(ff_dw_b),
    )
    loss8, grad_x, g = _local_step(x[0], loss_target[0], p)
    loss = lax.psum(loss8[0, 0], ("x", "y", "c"))

    pieces = [g["w_inT"], g["w_out"], g["w_qkvT"], g["w_o"], _deinterleave_rows(g["w_upT"][0]), _deinterleave_rows(g["w_upT"][1]),
              g["w_down"][0], g["w_down"][1]]
    g_rows = jnp.concatenate([t.reshape(N_DEV, t.shape[0] // N_DEV, D) for t in pieces], axis=1)
    ix, iy, ic = lax.axis_index("x"), lax.axis_index("y"), lax.axis_index("c")
    chip = 2 * ix + iy
    sib = _rs_sibling(g_rows.astype(WIRE_DTYPE), name="reduce_sibling")
    part = _chip_partials(g_rows, sib, jnp.stack([ic]).astype(jnp.int32), name="reduce_add")
    recv = _rs_chips(part, name="reduce_chips")
    slots = jnp.stack([me, chip, 0 * me, 0 * me + 1, 0 * me + 2]).astype(jnp.int32)
    gb, db, mb, vb = _adamw(w_rows, _big_rows(m), _big_rows(v), [g_rows, sib, recv, recv, recv], slots, name="adamw_big")
    big = [_big_unrows(t, w) for t in (gb, db, mb, vb)]

    g_small = dict(g)
    g_small["cm_b_in"] = g["cm_b_in"]
    g_small["at_q_norm"] = g["at_q_norm"][None]
    g_small["at_k_norm"] = g["at_k_norm"][None]
    g_small["ff_dw_b"] = _deinterleave_cols(g["ff_dw_b"])
    g_small["cm_dw"] = g["cm_dw"][None]
    g_small["ff_dw"] = jnp.stack([_deinterleave_cols(g["ff_dw"][l]) for l in range(g["ff_dw"].shape[0])])
    small_shapes = [g_small[n].shape for n in _SMALL]
    gs_parts = _all_gather(_flat_pack([g_small[n] for n in _SMALL], D), name="gather_small_grads")

    def embed(t, n):
        if n not in _SMALL_SHARDED:
            return t
        full_shape = t.shape[:-1] + (t.shape[-1] * N_DEV,)
        return lax.dynamic_update_slice_in_dim(jnp.zeros(full_shape, F32), t, me * t.shape[-1], axis=t.ndim - 1)

    packs = [_flat_pack([embed(tree[n], n) for n in _SMALL], D) for tree in (w, m, v)]
    gs, ds_, ms, vs = _adamw(packs[0], packs[1], packs[2], [gs_parts] * N_DEV, jnp.arange(N_DEV, dtype=jnp.int32),
                             name="adamw_small")
    small = []
    for t in (gs, ds_, ms, vs):
        un = dict(zip(_SMALL, _flat_unpack(t, small_shapes)))
        for n in _SMALL_SHARDED:
            width = w[n].shape[-1]
            un[n] = lax.dynamic_slice_in_dim(un[n], me * width, width, axis=un[n].ndim - 1)
        small.append({n: un[n].reshape(w[n].shape) for n in _SMALL})

    outs = [loss, grad_x[None]]
    for k in range(4):
        for n in _ORDER:
            outs.append(big[k][n] if n in _BIG else small[k][n])
    return tuple(outs)
```

```python
import functools

import jax
import jax.numpy as jnp
import numpy as np
from jax import lax
from jax.experimental import pallas as pl
from jax.experimental.pallas import tpu as pltpu

F32 = jnp.float32
MXU_DTYPE = jnp.bfloat16
WIRE_DTYPE = jnp.bfloat16
EPS = 1e-6
NEG = -1e30
HEAD_DIM = 128
BLOCK = 128
DILATED_GROUPS = ((128, 1), (512, 4), (2048, 16))
ALIBI_MAX = 8.0
CONV_KERNEL = 31
CONV_HALO = 32
CONV_ROWS = 64
FFN_KERNEL = 3
FFN_HALO = 16
FFN_ROWS = 64
ADAM_LR, ADAM_B1, ADAM_B2, ADAM_EPS, ADAM_WD, ADAM_STEP = 0.001, 0.9, 0.999, 1e-08, 0.01, 10
V7X_VMEM_BYTES = 64 * 1024 * 1024
VMEM_LIMIT = V7X_VMEM_BYTES * 3 // 4
N_DEV = 8
MESH = pl.DeviceIdType.MESH


def _pick(n, target, align):
    if n <= target:
        return n
    best = None
    for t in range(align, target + 1, align):
        if n % t == 0:
            best = t
    assert best is not None, (n, target, align)
    return best


def _params(*sem):
    return pltpu.CompilerParams(dimension_semantics=sem, vmem_limit_bytes=VMEM_LIMIT)


def _sigmoid(x):
    return 1.0 / (1.0 + jnp.exp(-x))


_DIMS = {"nn": ((1,), (0,)), "nt": ((1,), (1,)), "tn": ((0,), (0,))}


def _mm(a, b, *, mode, out_dtype, name, tm=1024, tn=1024, tk=None, bias=None, residual=None, b_off=0, b_len=None):
    if mode == "tn":
        K, M = a.shape
    else:
        M, K = a.shape
    if mode == "nt":
        N = b.shape[0] if b_len is None else b_len
    else:
        N = b.shape[1]
    if b_len is not None:
        assert mode == "nt" or (mode == "nn" and K == b_len)
    tm = _pick(M, tm, 128 if mode == "tn" else 16)
    tn = _pick(N, tn, 128)
    tk = K if tk is None else _pick(K, tk, 128 if mode != "tn" else 16)
    nk = K // tk
    unit = tn if mode == "nt" else tk
    assert b_off % unit == 0
    kb0 = b_off // unit
    if mode == "tn":
        a_spec = pl.BlockSpec((tk, tm), lambda i, j, k: (k, i))
    else:
        a_spec = pl.BlockSpec((tm, tk), lambda i, j, k: (i, k))
    if mode == "nt":
        b_spec = pl.BlockSpec((tn, tk), lambda i, j, k: (j + kb0, k))
    else:
        b_spec = pl.BlockSpec((tk, tn), lambda i, j, k: (k + kb0, j))
    in_specs = [a_spec, b_spec]
    args = [a, b]
    if bias is not None:
        in_specs.append(pl.BlockSpec((1, tn), lambda i, j, k: (0, j)))
        args.append(bias)
    if residual is not None:
        in_specs.append(pl.BlockSpec((tm, tn), lambda i, j, k: (i, j)))
        args.append(residual)
    has_bias, has_res = bias is not None, residual is not None

    def body(*refs):
        a_ref, b_ref = refs[0], refs[1]
        pos = 2
        bias_ref = res_ref = None
        if has_bias:
            bias_ref = refs[pos]
            pos += 1
        if has_res:
            res_ref = refs[pos]
            pos += 1
        o_ref = refs[pos]
        acc_ref = refs[pos + 1] if nk > 1 else None

        def finish(acc):
            if has_bias:
                acc = acc + bias_ref[...]
            if has_res:
                acc = acc + res_ref[...]
            o_ref[...] = acc.astype(o_ref.dtype)

        part = lax.dot_general(a_ref[...].astype(MXU_DTYPE), b_ref[...].astype(MXU_DTYPE), (_DIMS[mode], ((), ())),
                               preferred_element_type=F32)
        if nk == 1:
            finish(part)
        else:
            k = pl.program_id(2)

            @pl.when(k == 0)
            def _():
                acc_ref[...] = part

            @pl.when(jnp.logical_and(k > 0, k < nk - 1))
            def _():
                acc_ref[...] += part

            @pl.when(k == nk - 1)
            def _():
                finish(acc_ref[...] + part)

    return pl.pallas_call(
        body, name=name, grid=(M // tm, N // tn, nk), in_specs=in_specs,
        out_specs=pl.BlockSpec((tm, tn), lambda i, j, k: (i, j)),
        out_shape=jax.ShapeDtypeStruct((M, N), out_dtype),
        scratch_shapes=[pltpu.VMEM((tm, tn), F32)] if nk > 1 else [],
        compiler_params=_params("parallel", "parallel", "arbitrary"),
    )(*args)


SUB_TILE = 512


def _sub_spec(dil, ts, cols):
    return pl.BlockSpec((dil, ts // dil, cols), lambda i: (0, i, 0))


def _tok_to_sub(tok_ref, dst_ref, dil):
    nc, ts, _ = tok_ref.shape
    for c in range(nc):
        for r in range(dil):
            dst_ref[r, :, c * 128:(c + 1) * 128] = tok_ref.at[c][pl.ds(r, ts // dil, stride=dil), :].astype(dst_ref.dtype)


def _sub_to_tok(src_ref, tok_ref, dil):
    nc, ts, _ = tok_ref.shape
    for c in range(nc):
        for r in range(dil):
            tok_ref.at[c][pl.ds(r, ts // dil, stride=dil), :] = src_ref[r, :, c * 128:(c + 1) * 128].astype(F32)


def _rms_fwd(x, g, *, name, subs=()):
    S, D = x.shape
    ts = _pick(S, SUB_TILE, 16 * max(subs, default=1))
    NC = D // 128

    def body(x_ref, g_ref, h_ref, *rest):
        xv = x_ref[...]
        r = lax.rsqrt(jnp.mean(xv * xv, axis=-1, keepdims=True) + EPS)
        h = xv * r * g_ref[...]
        h_ref[...] = h.astype(h_ref.dtype)
        if subs:
            tok_ref = rest[-1]
            for c in range(NC):
                tok_ref[c] = h[:, c * 128:(c + 1) * 128]
            for dil, dst_ref in zip(subs, rest):
                _tok_to_sub(tok_ref, dst_ref, dil)

    row = pl.BlockSpec((ts, D), lambda i: (i, 0))
    outs = pl.pallas_call(
        body, name=name, grid=(S // ts,),
        in_specs=[row, pl.BlockSpec((1, D), lambda i: (0, 0))],
        out_specs=[row] + [_sub_spec(dil, ts, D) for dil in subs],
        out_shape=[jax.ShapeDtypeStruct((S, D), MXU_DTYPE)] + [jax.ShapeDtypeStruct((dil, S // dil, D), MXU_DTYPE) for dil in subs],
        scratch_shapes=[pltpu.VMEM((NC, ts, 128), F32)] if subs else [],
        compiler_params=_params("parallel"),
    )(x, g)
    return outs if subs else outs[0]


def _rms_bwd(x, g, dhs, dres, *, name, dh_subs=()):
    S, D = x.shape
    ts = _pick(S, SUB_TILE, 16 * max([dil for _, dil in dh_subs], default=1))
    n_dh, n_sub = len(dhs), len(dh_subs)
    NC = D // 128

    def body(*refs):
        x_ref, g_ref = refs[0], refs[1]
        dh_refs = refs[2:2 + n_dh]
        sub_refs = refs[2 + n_dh:2 + n_dh + n_sub]
        dres_ref, dx_ref, dxb_ref, dg_ref, cs_ref = refs[2 + n_dh + n_sub:7 + n_dh + n_sub]
        i = pl.program_id(0)
        xv = x_ref[...]
        r = lax.rsqrt(jnp.mean(xv * xv, axis=-1, keepdims=True) + EPS)
        xh = xv * r
        dhv = dh_refs[0][...].astype(F32)
        for t in dh_refs[1:]:
            dhv = dhv + t[...].astype(F32)
        for (_, dil), sub_ref in zip(dh_subs, sub_refs):
            tok_ref = refs[-1]
            _sub_to_tok(sub_ref, tok_ref, dil)
            dhv = dhv + jnp.concatenate([tok_ref[c] for c in range(NC)], axis=1)
        gy = dhv * g_ref[...]
        dx = r * (gy - xh * jnp.mean(gy * xh, axis=-1, keepdims=True)) + dres_ref[...]
        dx_ref[...] = dx
        dxb_ref[...] = dx.astype(dxb_ref.dtype)
        dg = jnp.sum(dhv * xh, axis=0, keepdims=True)
        cs = jnp.sum(dx, axis=0, keepdims=True)

        @pl.when(i == 0)
        def _():
            dg_ref[...] = dg
            cs_ref[...] = cs

        @pl.when(i > 0)
        def _():
            dg_ref[...] += dg
            cs_ref[...] += cs

    row = pl.BlockSpec((ts, D), lambda i: (i, 0))
    vec = pl.BlockSpec((1, D), lambda i: (0, 0))
    return pl.pallas_call(
        body, name=name, grid=(S // ts,),
        in_specs=[row, vec] + [row] * n_dh + [_sub_spec(dil, ts, D) for _, dil in dh_subs] + [row],
        out_specs=[row, row, vec, vec],
        out_shape=[jax.ShapeDtypeStruct((S, D), F32), jax.ShapeDtypeStruct((S, D), MXU_DTYPE),
                   jax.ShapeDtypeStruct((1, D), F32), jax.ShapeDtypeStruct((1, D), F32)],
        scratch_shapes=[pltpu.VMEM((NC, ts, 128), F32)] if n_sub else [],
        compiler_params=_params("arbitrary"),
    )(x, g, *dhs, *[a for a, _ in dh_subs], dres)


def _conv_phases(ph_ref, ts):
    n = ts + CONV_HALO - 8
    for b in range(1, 8):
        ph_ref[b, 0:n, :] = ph_ref[0, pl.ds(b, n), :]


def _phase_taps(base, step=1):
    groups = {}
    for k in range(CONV_KERNEL):
        a, b = divmod(base + step * k, 8)
        groups.setdefault(b, []).append((a, k))
    out = []
    for b in sorted(groups):
        ak = sorted(groups[b])
        assert [a for a, _ in ak] == list(range(ak[0][0], ak[0][0] + len(ak)))
        out.append((b, ak[0][0], [k for _, k in ak]))
    return out


def _cm_fwd(u, dw, dw_b, ln_g, ln_b, *, name):
    S, D2 = u.shape
    D = D2 // 2
    ts = _pick(S, 256, CONV_HALO)
    hb = ts // CONV_HALO

    def body(u_ref, up_ref, dw_ref, dwb_ref, g_ref, b_ref, c_ref, s_ref, ext_ref):
        i = pl.program_id(0)
        prev = up_ref[:, :D] * _sigmoid(up_ref[:, D:])
        ext_ref[0:CONV_HALO, :] = jnp.where(i > 0, prev, 0.0)
        ext_ref[CONV_HALO:CONV_HALO + ts, :] = u_ref[:, :D] * _sigmoid(u_ref[:, D:])
        for cc in range(D // 128):
            sl = slice(cc * 128, (cc + 1) * 128)
            acc = jnp.zeros((ts, 128), F32) + dwb_ref[:, sl]
            for k in range(CONV_KERNEL):
                acc = acc + dw_ref[k:k + 1, sl] * ext_ref[pl.ds(CONV_HALO - (CONV_KERNEL - 1) + k, ts), sl]
            c_ref[:, sl] = acc
        c = c_ref[...]
        mu = jnp.mean(c, axis=-1, keepdims=True)
        xc = c - mu
        rstd = lax.rsqrt(jnp.mean(xc * xc, axis=-1, keepdims=True) + EPS)
        y = xc * rstd * g_ref[...] + b_ref[...]
        s_ref[...] = (y * _sigmoid(y)).astype(s_ref.dtype)

    vec = pl.BlockSpec((1, D), lambda i: (0, 0))
    return pl.pallas_call(
        body, name=name, grid=(S // ts,),
        in_specs=[pl.BlockSpec((ts, D2), lambda i: (i, 0)),
                  pl.BlockSpec((CONV_HALO, D2), lambda i: (jnp.maximum(i * hb - 1, 0), 0)),
                  pl.BlockSpec((CONV_KERNEL, D), lambda i: (0, 0)), vec, vec, vec],
        out_specs=[pl.BlockSpec((ts, D), lambda i: (i, 0)), pl.BlockSpec((ts, D), lambda i: (i, 0))],
        out_shape=[jax.ShapeDtypeStruct((S, D), F32), jax.ShapeDtypeStruct((S, D), MXU_DTYPE)],
        scratch_shapes=[pltpu.VMEM((ts + CONV_HALO, D), F32)],
        compiler_params=_params("parallel"),
    )(u, u, dw, dw_b, ln_g, ln_b)


def _cm_ln_bwd(c, ds, ln_g, ln_b, *, name):
    S, D = c.shape
    ts = _pick(S, 512, 16)

    def body(c_ref, ds_ref, g_ref, b_ref, dc_ref, dg_ref, db_ref):
        i = pl.program_id(0)
        cv = c_ref[...]
        mu = jnp.mean(cv, axis=-1, keepdims=True)
        xc = cv - mu
        rstd = lax.rsqrt(jnp.mean(xc * xc, axis=-1, keepdims=True) + EPS)
        xh = xc * rstd
        y = xh * g_ref[...] + b_ref[...]
        sg = _sigmoid(y)
        dy = ds_ref[...].astype(F32) * (sg * (1.0 + y * (1.0 - sg)))
        gy = dy * g_ref[...]
        dc_ref[...] = rstd * (gy - jnp.mean(gy, axis=-1, keepdims=True) - xh * jnp.mean(gy * xh, axis=-1, keepdims=True))
        dg = jnp.sum(dy * xh, axis=0, keepdims=True)
        db = jnp.sum(dy, axis=0, keepdims=True)

        @pl.when(i == 0)
        def _():
            dg_ref[...] = dg
            db_ref[...] = db

        @pl.when(i > 0)
        def _():
            dg_ref[...] += dg
            db_ref[...] += db

    row = pl.BlockSpec((ts, D), lambda i: (i, 0))
    vec = pl.BlockSpec((1, D), lambda i: (0, 0))
    return pl.pallas_call(
        body, name=name, grid=(S // ts,), in_specs=[row, row, vec, vec], out_specs=[row, vec, vec],
        out_shape=[jax.ShapeDtypeStruct((S, D), F32), jax.ShapeDtypeStruct((1, D), F32), jax.ShapeDtypeStruct((1, D), F32)],
        compiler_params=_params("arbitrary"),
    )(c, ds, ln_g, ln_b)


def _cm_conv_bwd(dc, u, dw, *, name):
    S, D2 = u.shape
    D = D2 // 2
    ts = _pick(S, 256, CONV_HALO)
    hb = ts // CONV_HALO
    n_t = S // ts
    last_h = S // CONV_HALO - 1

    rc = _pick(ts, CONV_ROWS, 8)

    def fold8(v):
        out = v[0:8]
        for j in range(1, v.shape[0] // 8):
            out = out + v[8 * j:8 * j + 8]
        return out

    def body(dc_ref, dcn_ref, u_ref, up_ref, dw_ref, du_ref, ddw_ref, ddwb_ref, dbin_ref, dph_ref, gph_ref, dgl_ref):
        i = pl.program_id(0)
        dph_ref[0, 0:ts, :] = dc_ref[...]
        dph_ref[0, ts:ts + CONV_HALO, :] = jnp.where(i < n_t - 1, dcn_ref[...], 0.0)
        prev = up_ref[:, :D] * _sigmoid(up_ref[:, D:])
        gph_ref[0, 0:CONV_HALO, :] = jnp.where(i > 0, prev, 0.0)
        gph_ref[0, CONV_HALO:CONV_HALO + ts, :] = u_ref[:, :D] * _sigmoid(u_ref[:, D:])
        _conv_phases(dph_ref, ts)
        _conv_phases(gph_ref, ts)

        @pl.when(i == 0)
        def _():
            ddw_ref[...] = jnp.zeros_like(ddw_ref)
            ddwb_ref[...] = jnp.zeros_like(ddwb_ref)
            dbin_ref[...] = jnp.zeros_like(dbin_ref)

        for cc in range(D // 128):
            sl = slice(cc * 128, (cc + 1) * 128)
            sl2 = slice(D + cc * 128, D + (cc + 1) * 128)
            wk = [dw_ref[k:k + 1, sl] for k in range(CONV_KERNEL)]
            acc_a, acc_g = jnp.zeros((8, 128), F32), jnp.zeros((8, 128), F32)
            dgl = jnp.zeros((ts, 128), F32)
            for b, a0, taps in _phase_taps(CONV_KERNEL - 1, -1):
                for j, k in enumerate(taps):
                    dgl = dgl + wk[k] * dph_ref[b, 8 * (a0 + j):8 * (a0 + j) + ts, sl]
            dgl_ref[...] = dgl
            for r0 in range(0, ts, rc):
                dglu = dgl_ref[r0:r0 + rc, :]
                av = u_ref[r0:r0 + rc, sl]
                sg = _sigmoid(u_ref[r0:r0 + rc, sl2])
                da = dglu * sg
                dg = dglu * av * sg * (1.0 - sg)
                du_ref[r0:r0 + rc, sl] = da.astype(du_ref.dtype)
                du_ref[r0:r0 + rc, sl2] = dg.astype(du_ref.dtype)
                acc_a = acc_a + fold8(da)
                acc_g = acc_g + fold8(dg)
            dbin_ref[:, sl] += jnp.sum(acc_a, axis=0, keepdims=True)
            dbin_ref[:, sl2] += jnp.sum(acc_g, axis=0, keepdims=True)
            for gi, (b, a0, taps) in enumerate(_phase_taps(CONV_HALO - (CONV_KERNEL - 1))):
                accs = [jnp.zeros((8, 128), F32) for _ in taps]
                accb = jnp.zeros((8, 128), F32)
                for r0 in range(0, ts, rc):
                    dcc = dph_ref[0, r0:r0 + rc, sl]
                    win = gph_ref[b, 8 * a0 + r0:8 * (a0 + len(taps) - 1) + r0 + rc, sl]
                    for j in range(len(taps)):
                        accs[j] = accs[j] + fold8(dcc * win[8 * j:8 * j + rc])
                    if gi == 0:
                        accb = accb + fold8(dcc)
                for j, k in enumerate(taps):
                    ddw_ref[k:k + 1, sl] += jnp.sum(accs[j], axis=0, keepdims=True)
                if gi == 0:
                    ddwb_ref[:, sl] += jnp.sum(accb, axis=0, keepdims=True)

    return pl.pallas_call(
        body, name=name, grid=(n_t,),
        in_specs=[pl.BlockSpec((ts, D), lambda i: (i, 0)),
                  pl.BlockSpec((CONV_HALO, D), lambda i: (jnp.minimum((i + 1) * hb, last_h), 0)),
                  pl.BlockSpec((ts, D2), lambda i: (i, 0)),
                  pl.BlockSpec((CONV_HALO, D2), lambda i: (jnp.maximum(i * hb - 1, 0), 0)),
                  pl.BlockSpec((CONV_KERNEL, D), lambda i: (0, 0))],
        out_specs=[pl.BlockSpec((ts, D2), lambda i: (i, 0)), pl.BlockSpec((CONV_HALO, D), lambda i: (0, 0)),
                   pl.BlockSpec((1, D), lambda i: (0, 0)), pl.BlockSpec((1, D2), lambda i: (0, 0))],
        out_shape=[jax.ShapeDtypeStruct((S, D2), MXU_DTYPE), jax.ShapeDtypeStruct((CONV_HALO, D), F32),
                   jax.ShapeDtypeStruct((1, D), F32), jax.ShapeDtypeStruct((1, D2), F32)],
        scratch_shapes=[pltpu.VMEM((8, ts + CONV_HALO, D), F32), pltpu.VMEM((8, ts + CONV_HALO, D), F32),
                        pltpu.VMEM((ts, 128), F32)],
        compiler_params=_params("arbitrary"),
    )(dc, dc, u, u, dw)


def _ffn_cols(F2):
    return _pick(F2, 1024, 256)


def _ffn_act_fwd(up, dw, dw_b, *, name):
    S, F2 = up.shape
    ts = _pick(S, 512, 16)
    tc = _ffn_cols(F2)
    hb = ts // FFN_HALO

    rc = _pick(ts, FFN_ROWS, 16)

    def body(u_ref, up_ref, w_ref, b_ref, a_ref, ext_ref):
        i = pl.program_id(1)
        ext_ref[0:FFN_HALO, :] = jnp.where(i > 0, up_ref[...].astype(F32), 0.0)
        ext_ref[FFN_HALO:FFN_HALO + ts, :] = u_ref[...].astype(F32)
        for q in range(tc // 256):
            sls = [slice(q * 256 + half * 128, q * 256 + half * 128 + 128) for half in range(2)]
            wk = [[w_ref[k:k + 1, sl] for k in range(FFN_KERNEL)] for sl in sls]
            bb = [b_ref[:, sl] for sl in sls]
            for r0 in range(0, ts, rc):
                gt, vl = [bb[h] + sum(wk[h][k] * ext_ref[pl.ds(FFN_HALO + r0 - 2 + k, rc), sls[h]] for k in range(FFN_KERNEL))
                          for h in range(2)]
                a_ref[r0:r0 + rc, q * 128:(q + 1) * 128] = (gt * _sigmoid(gt) * vl).astype(a_ref.dtype)

    return pl.pallas_call(
        body, name=name, grid=(F2 // tc, S // ts),
        in_specs=[pl.BlockSpec((ts, tc), lambda j, i: (i, j)),
                  pl.BlockSpec((FFN_HALO, tc), lambda j, i: (jnp.maximum(i * hb - 1, 0), j)),
                  pl.BlockSpec((FFN_KERNEL, tc), lambda j, i: (0, j)),
                  pl.BlockSpec((1, tc), lambda j, i: (0, j))],
        out_specs=pl.BlockSpec((ts, tc // 2), lambda j, i: (i, j)),
        out_shape=jax.ShapeDtypeStruct((S, F2 // 2), MXU_DTYPE),
        scratch_shapes=[pltpu.VMEM((ts + FFN_HALO, tc), F32)],
        compiler_params=_params("parallel", "parallel"),
    )(up, up, dw, dw_b)


def _ffn_act_bwd(up, dact, dw, dw_b, *, name):
    S, F2 = up.shape
    ts = _pick(S, 512, 16)
    tc = _ffn_cols(F2)
    hb = ts // FFN_HALO
    n_t = S // ts
    last_h = S // FFN_HALO - 1
    E = ts + FFN_HALO

    rc = _pick(ts, FFN_ROWS, 16)

    def fold8(v):
        out = v[0:8]
        for j in range(1, v.shape[0] // 8):
            out = out + v[8 * j:8 * j + 8]
        return out

    def body(u_ref, up_ref, un_ref, da_ref, dan_ref, w_ref, b_ref, dup_ref, ddw_ref, ddb_ref, ue_ref, dcv_ref):
        i = pl.program_id(1)
        ue_ref[0:FFN_HALO, :] = jnp.where(i > 0, up_ref[...].astype(F32), 0.0)
        ue_ref[FFN_HALO:FFN_HALO + ts, :] = u_ref[...].astype(F32)
        ue_ref[FFN_HALO + ts:FFN_HALO + ts + FFN_HALO, :] = jnp.where(i < n_t - 1, un_ref[...].astype(F32), 0.0)

        @pl.when(i == 0)
        def _():
            ddw_ref[...] = jnp.zeros_like(ddw_ref)
            ddb_ref[...] = jnp.zeros_like(ddb_ref)

        for q in range(tc // 256):
            sls = [slice(q * 256 + half * 128, q * 256 + half * 128 + 128) for half in range(2)]
            qs = slice(q * 128, (q + 1) * 128)
            wk = [[w_ref[k:k + 1, sl] for k in range(FFN_KERNEL)] for sl in sls]
            bb = [b_ref[:, sl] for sl in sls]
            acc = [[jnp.zeros((8, 128), F32) for _ in range(FFN_KERNEL)] for _ in range(2)]
            accb = [jnp.zeros((8, 128), F32) for _ in range(2)]
            for r0, rows in [(r, rc) for r in range(0, ts, rc)] + [(ts, FFN_HALO)]:
                xs = [[ue_ref[pl.ds(FFN_HALO + r0 - 2 + k, rows), sls[h]] for k in range(FFN_KERNEL)] for h in range(2)]
                gt, vl = [bb[h] + sum(wk[h][k] * xs[h][k] for k in range(FFN_KERNEL)) for h in range(2)]
                sg = _sigmoid(gt)
                if r0 < ts:
                    dae = da_ref[r0:r0 + rows, qs].astype(F32)
                else:
                    dae = jnp.where(i < n_t - 1, dan_ref[:, qs].astype(F32), 0.0)
                dcv = [dae * vl * (sg * (1.0 + gt * (1.0 - sg))), dae * (gt * sg)]
                for h in range(2):
                    dcv_ref[r0:r0 + rows, sls[h]] = dcv[h]
                    if r0 < ts:
                        for k in range(FFN_KERNEL):
                            acc[h][k] = acc[h][k] + fold8(dcv[h] * xs[h][k])
                        accb[h] = accb[h] + fold8(dcv[h])
            for r0 in range(0, ts, rc):
                for h in range(2):
                    dup = sum(wk[h][2 - j] * dcv_ref[pl.ds(r0 + j, rc), sls[h]] for j in range(FFN_KERNEL))
                    dup_ref[r0:r0 + rc, sls[h]] = dup.astype(dup_ref.dtype)
            for h in range(2):
                for k in range(FFN_KERNEL):
                    ddw_ref[k:k + 1, sls[h]] += jnp.sum(acc[h][k], axis=0, keepdims=True)
                ddb_ref[:, sls[h]] += jnp.sum(accb[h], axis=0, keepdims=True)

    return pl.pallas_call(
        body, name=name, grid=(F2 // tc, n_t),
        in_specs=[pl.BlockSpec((ts, tc), lambda j, i: (i, j)),
                  pl.BlockSpec((FFN_HALO, tc), lambda j, i: (jnp.maximum(i * hb - 1, 0), j)),
                  pl.BlockSpec((FFN_HALO, tc), lambda j, i: (jnp.minimum((i + 1) * hb, last_h), j)),
                  pl.BlockSpec((ts, tc // 2), lambda j, i: (i, j)),
                  pl.BlockSpec((FFN_HALO, tc // 2), lambda j, i: (jnp.minimum((i + 1) * hb, last_h), j)),
                  pl.BlockSpec((FFN_KERNEL, tc), lambda j, i: (0, j)),
                  pl.BlockSpec((1, tc), lambda j, i: (0, j))],
        out_specs=[pl.BlockSpec((ts, tc), lambda j, i: (i, j)), pl.BlockSpec((FFN_HALO, tc), lambda j, i: (0, j)),
                   pl.BlockSpec((1, tc), lambda j, i: (0, j))],
        out_shape=[jax.ShapeDtypeStruct((S, F2), MXU_DTYPE), jax.ShapeDtypeStruct((FFN_HALO, F2), F32),
                   jax.ShapeDtypeStruct((1, F2), F32)],
        scratch_shapes=[pltpu.VMEM((ts + 2 * FFN_HALO, tc), F32), pltpu.VMEM((E, tc), F32)],
        compiler_params=_params("parallel", "arbitrary"),
    )(up, up, up, dact, dact, dw, dw_b)


def _slopes(n_heads_total):
    return np.asarray(2.0 ** (-ALIBI_MAX * (np.arange(n_heads_total, dtype=np.float32) + 1.0) / n_heads_total), np.float32)


def _qkv_proj(h, w_qkvT, *, grp, name):
    S, D = h.shape
    H = D // HEAD_DIM
    tm = _pick(S, 1024, 16)

    def body(a_ref, b_ref, o_ref, r_ref):
        j = pl.program_id(1)
        acc = lax.dot_general(a_ref[...].astype(MXU_DTYPE), b_ref[...].astype(MXU_DTYPE), (_DIMS["nt"], ((), ())),
                              preferred_element_type=F32)
        r_ref[...] = jnp.zeros_like(r_ref)

        @pl.when(j < 2)
        def _():
            for hd in range(H):
                hs = slice(hd * HEAD_DIM, (hd + 1) * HEAD_DIM)
                xv = acc[:, hs]
                r = lax.rsqrt(jnp.mean(xv * xv, axis=-1, keepdims=True) + EPS)
                o_ref[:, hs] = (xv * r).astype(o_ref.dtype)
                r_ref[:, hd:hd + 1] = r

        @pl.when(j == 2)
        def _():
            o_ref[...] = acc.astype(o_ref.dtype)

    return pl.pallas_call(
        body, name=name, grid=(S // tm, 3),
        in_specs=[pl.BlockSpec((tm, D), lambda i, j: (i, 0)), pl.BlockSpec((D, D), lambda i, j: (grp * 3 + j, 0))],
        out_specs=[pl.BlockSpec((tm, D), lambda i, j: (i, j)), pl.BlockSpec((tm, HEAD_DIM), lambda i, j: (i, j))],
        out_shape=[jax.ShapeDtypeStruct((S, 3 * D), MXU_DTYPE), jax.ShapeDtypeStruct((S, 3 * HEAD_DIM), F32)],
        compiler_params=_params("parallel", "parallel"),
    )(h, w_qkvT)


def _band(b, dil):
    qi = lax.broadcasted_iota(jnp.int32, (BLOCK, 2 * BLOCK), 0)
    ki = lax.broadcasted_iota(jnp.int32, (BLOCK, 2 * BLOCK), 1)
    delta = qi + BLOCK - ki
    valid = (delta >= 0) & (delta <= BLOCK) & ((ki >= BLOCK) | (b > 0))
    return valid, (delta * dil).astype(F32)


def _attn_fwd(qkv, qg, kg, *, grp, name):
    S, W = qkv.shape
    D = W // 3
    H = D // HEAD_DIM
    dil = DILATED_GROUPS[grp][1]
    L = S // dil
    nb = L // BLOCK
    slopes = _slopes(3 * H)[grp * H:(grp + 1) * H]
    scale = HEAD_DIM ** -0.5

    def body(q_ref, kp_ref, kc_ref, vp_ref, vc_ref, qg_ref, kg_ref, o_ref, l_ref):
        b = pl.program_id(1)
        valid, dist = _band(b, dil)
        l_ref[...] = jnp.zeros_like(l_ref)
        ss = []
        for h in range(H):
            hs = slice(h * HEAD_DIM, (h + 1) * HEAD_DIM)
            qn = (q_ref[:, hs].astype(F32) * qg_ref[h:h + 1, :]).astype(MXU_DTYPE)
            kp = (kp_ref[:, hs].astype(F32) * kg_ref[h:h + 1, :]).astype(MXU_DTYPE)
            kc = (kc_ref[:, hs].astype(F32) * kg_ref[h:h + 1, :]).astype(MXU_DTYPE)
            ss.append(lax.dot_general(qn, jnp.concatenate([kp, kc], axis=0), (((1,), (1,)), ((), ())), preferred_element_type=F32))
        ps = []
        for h in range(H):
            s = jnp.where(valid, ss[h] * scale - float(slopes[h]) * dist, NEG)
            m = jnp.max(s, axis=-1, keepdims=True)
            p = jnp.exp(s - m)
            den = jnp.sum(p, axis=-1, keepdims=True)
            l_ref[:, h:h + 1] = m + jnp.log(den)
            ps.append((p.astype(MXU_DTYPE), den))
        for h in range(H):
            hs = slice(h * HEAD_DIM, (h + 1) * HEAD_DIM)
            pb, den = ps[h]
            v2 = jnp.concatenate([vp_ref[:, hs], vc_ref[:, hs]], axis=0).astype(MXU_DTYPE)
            o_ref[:, hs] = (jnp.dot(pb, v2, preferred_element_type=F32) / den).astype(o_ref.dtype)

    def cur(j):
        return lambda r, b: (r * nb + b, j)

    def prv(j):
        return lambda r, b: (r * nb + jnp.maximum(b - 1, 0), j)

    blk = (BLOCK, D)
    gain = pl.BlockSpec((H, HEAD_DIM), lambda r, b: (0, 0))
    return pl.pallas_call(
        body, name=name, grid=(dil, nb),
        in_specs=[pl.BlockSpec(blk, cur(0)), pl.BlockSpec(blk, prv(1)), pl.BlockSpec(blk, cur(1)),
                  pl.BlockSpec(blk, prv(2)), pl.BlockSpec(blk, cur(2)), gain, gain],
        out_specs=[pl.BlockSpec(blk, cur(0)), pl.BlockSpec((BLOCK, HEAD_DIM), cur(0))],
        out_shape=[jax.ShapeDtypeStruct((S, D), MXU_DTYPE), jax.ShapeDtypeStruct((S, HEAD_DIM), F32)],
        compiler_params=_params("parallel", "parallel"),
    )(qkv, qkv, qkv, qkv, qkv, qg, kg)


def _attn_merge(os_, ls_, dils, *, name):
    S, D = os_[0].shape
    H = D // HEAD_DIM
    G = len(dils)
    ts = _pick(S, SUB_TILE, 16 * max(dils))
    subs = [g for g in range(G) if dils[g] > 1]

    def body(*refs):
        o_refs, l_refs = refs[0:G], refs[G:2 * G]
        outb_ref = refs[2 * G]
        lt_refs = refs[2 * G + 1:3 * G + 1]
        scratch = refs[3 * G + 1:]
        lt_tok = scratch[0]
        o_tok = {g: scratch[1 + 2 * j] for j, g in enumerate(subs)}
        l_tok = {g: scratch[2 + 2 * j] for j, g in enumerate(subs)}
        for g in subs:
            _sub_to_tok(o_refs[g], o_tok[g], dils[g])
            _sub_to_tok(l_refs[g], l_tok[g], dils[g])
        lt_tok[0] = jnp.zeros((ts, HEAD_DIM), F32)
        for h in range(H):
            hs = slice(h * HEAD_DIM, (h + 1) * HEAD_DIM)
            ls = [l_tok[g][0][:, h:h + 1] if g in subs else l_refs[g][:, h:h + 1] for g in range(G)]
            ov = [o_tok[g][h] if g in subs else o_refs[g][:, hs].astype(F32) for g in range(G)]
            m = functools.reduce(jnp.maximum, ls)
            es = [jnp.exp(l - m) for l in ls]
            den = functools.reduce(lambda a, b: a + b, es)
            out = functools.reduce(lambda a, b: a + b, [e * o for e, o in zip(es, ov)]) / den
            outb_ref[:, hs] = out.astype(outb_ref.dtype)
            lt_tok.at[0][:, h:h + 1] = m + jnp.log(den)
        for g in range(G):
            if g in subs:
                _tok_to_sub(lt_tok, lt_refs[g], dils[g])
            else:
                lt_refs[g][...] = lt_tok[0]

    def spec(g, cols):
        return _sub_spec(dils[g], ts, cols) if g in subs else pl.BlockSpec((ts, cols), lambda i: (i, 0))

    def shape(g, cols, dtype):
        return jax.ShapeDtypeStruct((dils[g], S // dils[g], cols) if g in subs else (S, cols), dtype)

    def view(a, g):
        return a.reshape(dils[g], S // dils[g], a.shape[-1]) if g in subs else a

    outs = pl.pallas_call(
        body, name=name, grid=(S // ts,),
        in_specs=[spec(g, D) for g in range(G)] + [spec(g, HEAD_DIM) for g in range(G)],
        out_specs=[pl.BlockSpec((ts, D), lambda i: (i, 0))] + [spec(g, HEAD_DIM) for g in range(G)],
        out_shape=[jax.ShapeDtypeStruct((S, D), MXU_DTYPE)] + [shape(g, HEAD_DIM, F32) for g in range(G)],
        scratch_shapes=[pltpu.VMEM((1, ts, HEAD_DIM), F32)] + [pltpu.VMEM((H, ts, HEAD_DIM), F32), pltpu.VMEM((1, ts, HEAD_DIM), F32)] * len(subs),
        compiler_params=_params("parallel"),
    )(*[view(o, g) for g, o in enumerate(os_)], *[view(l, g) for g, l in enumerate(ls_)])
    return outs[0], [t.reshape(S, HEAD_DIM) for t in outs[1:]]


def _attn_delta(do, out, dils, *, name):
    S, D = out.shape
    H = D // HEAD_DIM
    G = len(dils)
    ts = _pick(S, SUB_TILE, 16 * max(dils))
    subs = [g for g in range(G) if dils[g] > 1]

    def body(do_ref, o_ref, *rest):
        d_refs = rest[0:G]
        dos_refs = rest[G:G + len(subs)]
        d_tok, do_tok = rest[G + len(subs):]
        d_tok[0] = jnp.zeros((ts, HEAD_DIM), F32)
        for h in range(H):
            hs = slice(h * HEAD_DIM, (h + 1) * HEAD_DIM)
            dov = do_ref[:, hs].astype(F32)
            do_tok[h] = dov
            d_tok.at[0][:, h:h + 1] = jnp.sum(dov * o_ref[:, hs].astype(F32), axis=-1, keepdims=True)
        for g in range(G):
            if g in subs:
                _tok_to_sub(d_tok, d_refs[g], dils[g])
            else:
                d_refs[g][...] = d_tok[0]
        for g, dst in zip(subs, dos_refs):
            _tok_to_sub(do_tok, dst, dils[g])

    def spec(g, cols):
        return _sub_spec(dils[g], ts, cols) if g in subs else pl.BlockSpec((ts, cols), lambda i: (i, 0))

    def shape(g, cols, dtype):
        return jax.ShapeDtypeStruct((dils[g], S // dils[g], cols) if g in subs else (S, cols), dtype)

    row = pl.BlockSpec((ts, D), lambda i: (i, 0))
    outs = pl.pallas_call(
        body, name=name, grid=(S // ts,), in_specs=[row, row],
        out_specs=[spec(g, HEAD_DIM) for g in range(G)] + [spec(g, D) for g in subs],
        out_shape=[shape(g, HEAD_DIM, F32) for g in range(G)] + [shape(g, D, do.dtype) for g in subs],
        scratch_shapes=[pltpu.VMEM((1, ts, HEAD_DIM), F32), pltpu.VMEM((H, ts, HEAD_DIM), F32)],
        compiler_params=_params("parallel"),
    )(do, out)
    deltas = [t.reshape(S, HEAD_DIM) for t in outs[0:G]]
    dos = {g: t.reshape(S, D) for g, t in zip(subs, outs[G:])}
    return deltas, [dos[g] if g in subs else do for g in range(G)]


def _attn_bwd(qkv, rqk, do, lse, delta, qg, kg, *, grp, name):
    S, W = qkv.shape
    D = W // 3
    H = D // HEAD_DIM
    dil = DILATED_GROUPS[grp][1]
    L = S // dil
    nb = L // BLOCK
    slopes = _slopes(3 * H)[grp * H:(grp + 1) * H]
    scale = HEAD_DIM ** -0.5

    def body(q_ref, qp_ref, kp_ref, kc_ref, vp_ref, vc_ref, do_ref, l_ref, dl_ref, rq_ref, rk_ref, qg_ref, kg_ref,
             out_ref, dqg_ref, dkg_ref, cq_ref, ck_ref, cv_ref, nq_ref, nk_ref, nv_ref, pk_ref, pv_ref):
        r = pl.program_id(0)
        b = pl.program_id(1)

        @pl.when(jnp.logical_and(r == 0, b == 0))
        def _():
            dqg_ref[...] = jnp.zeros_like(dqg_ref)
            dkg_ref[...] = jnp.zeros_like(dkg_ref)

        @pl.when(b < nb)
        def _():
            valid, dist = _band(b, dil)

            def operands(h):
                hs = slice(h * HEAD_DIM, (h + 1) * HEAD_DIM)
                qn = (q_ref[:, hs].astype(F32) * qg_ref[h:h + 1, :]).astype(MXU_DTYPE)
                kp = (kp_ref[:, hs].astype(F32) * kg_ref[h:h + 1, :]).astype(MXU_DTYPE)
                kc = (kc_ref[:, hs].astype(F32) * kg_ref[h:h + 1, :]).astype(MXU_DTYPE)
                k2 = jnp.concatenate([kp, kc], axis=0)
                v2 = jnp.concatenate([vp_ref[:, hs], vc_ref[:, hs]], axis=0).astype(MXU_DTYPE)
                return hs, qn, k2, v2, do_ref[:, hs].astype(MXU_DTYPE)

            sdp = []
            for h in range(H):
                hs, qn, k2, v2, doh = operands(h)
                s = lax.dot_general(qn, k2, (((1,), (1,)), ((), ())), preferred_element_type=F32)
                dp = lax.dot_general(doh, v2, (((1,), (1,)), ((), ())), preferred_element_type=F32)
                sdp.append((s, dp))
            pds = []
            for h in range(H):
                s, dp = sdp[h]
                s = jnp.where(valid, s * scale - float(slopes[h]) * dist, NEG)
                p = jnp.exp(s - l_ref[:, h:h + 1])
                pds.append((p.astype(MXU_DTYPE), (p * (dp - dl_ref[:, h:h + 1]) * scale).astype(MXU_DTYPE)))
            for h in range(H):
                hs, qn, k2, v2, doh = operands(h)
                pb, dsc = pds[h]
                nq_ref[:, hs] = jnp.dot(dsc, k2, preferred_element_type=F32)
                dk2 = lax.dot_general(dsc, qn, (((0,), (0,)), ((), ())), preferred_element_type=F32)
                dv2 = lax.dot_general(pb, doh, (((0,), (0,)), ((), ())), preferred_element_type=F32)
                pk_ref[:, hs] = dk2[0:BLOCK]
                nk_ref[:, hs] = dk2[BLOCK:2 * BLOCK]
                pv_ref[:, hs] = dv2[0:BLOCK]
                nv_ref[:, hs] = dv2[BLOCK:2 * BLOCK]

        @pl.when(b == nb)
        def _():
            pk_ref[...] = jnp.zeros_like(pk_ref)
            pv_ref[...] = jnp.zeros_like(pv_ref)

        @pl.when(b > 0)
        def _():
            for h in range(H):
                hs = slice(h * HEAD_DIM, (h + 1) * HEAD_DIM)
                for j, (xh_ref, r_ref, gain_ref, dgain_ref) in enumerate(((qp_ref, rq_ref, qg_ref, dqg_ref),
                                                                          (kp_ref, rk_ref, kg_ref, dkg_ref))):
                    dy = cq_ref[:, hs] if j == 0 else ck_ref[:, hs] + pk_ref[:, hs]
                    gain = gain_ref[h:h + 1, :]
                    xh = xh_ref[:, hs].astype(F32)
                    rr = r_ref[:, h:h + 1]
                    gy = dy * gain
                    dx = rr * (gy - xh * jnp.mean(gy * xh, axis=-1, keepdims=True))
                    out_ref[:, j * D + h * HEAD_DIM:j * D + (h + 1) * HEAD_DIM] = dx.astype(out_ref.dtype)
                    dgain_ref[h:h + 1, :] += jnp.sum(dy * xh, axis=0, keepdims=True)
                out_ref[:, 2 * D + h * HEAD_DIM:2 * D + (h + 1) * HEAD_DIM] = (cv_ref[:, hs] + pv_ref[:, hs]).astype(out_ref.dtype)

        @pl.when(b < nb)
        def _():
            cq_ref[...] = nq_ref[...]
            ck_ref[...] = nk_ref[...]
            cv_ref[...] = nv_ref[...]

    def cur(j):
        return lambda r, b: (r * nb + jnp.minimum(b, nb - 1), j)

    def prv(j):
        return lambda r, b: (r * nb + jnp.clip(b - 1, 0, nb - 1), j)

    blk = (BLOCK, D)
    lblk = pl.BlockSpec((BLOCK, HEAD_DIM), cur(0))
    gain = pl.BlockSpec((H, HEAD_DIM), lambda r, b: (0, 0))
    return pl.pallas_call(
        body, name=name, grid=(dil, nb + 1),
        in_specs=[pl.BlockSpec(blk, cur(0)), pl.BlockSpec(blk, prv(0)), pl.BlockSpec(blk, prv(1)), pl.BlockSpec(blk, cur(1)),
                  pl.BlockSpec(blk, prv(2)), pl.BlockSpec(blk, cur(2)), pl.BlockSpec(blk, cur(0)), lblk, lblk,
                  pl.BlockSpec((BLOCK, HEAD_DIM), prv(0)), pl.BlockSpec((BLOCK, HEAD_DIM), prv(1)), gain, gain],
        out_specs=[pl.BlockSpec((BLOCK, 3 * D), lambda r, b: (r * nb + jnp.maximum(b - 1, 0), 0)), gain, gain],
        out_shape=[jax.ShapeDtypeStruct((S, 3 * D), MXU_DTYPE), jax.ShapeDtypeStruct((H, HEAD_DIM), F32),
                   jax.ShapeDtypeStruct((H, HEAD_DIM), F32)],
        scratch_shapes=[pltpu.VMEM(blk, F32)] * 8,
        compiler_params=_params("arbitrary", "arbitrary"),
    )(qkv, qkv, qkv, qkv, qkv, qkv, do, lse, delta, rqk, rqk, qg, kg)


def _loss_head(y, target, *, name):
    S, D = y.shape
    ts = _pick(S, 512, 16)

    def body(y_ref, t_ref, dy_ref, dyb_ref, l_ref, acc_ref):
        i = pl.program_id(0)
        e = y_ref[...] - t_ref[...]
        dy = e * (1.0 / D)
        dy_ref[...] = dy
        dyb_ref[...] = dy.astype(dyb_ref.dtype)
        part = jnp.sum(e * e, axis=0, keepdims=True)

        @pl.when(i == 0)
        def _():
            acc_ref[...] = part

        @pl.when(i > 0)
        def _():
            acc_ref[...] += part

        @pl.when(i == pl.num_programs(0) - 1)
        def _():
            l_ref[...] = jnp.broadcast_to(jnp.sum(acc_ref[...], axis=-1, keepdims=True) * (0.5 / D), l_ref.shape)

    row = pl.BlockSpec((ts, D), lambda i: (i, 0))
    return pl.pallas_call(
        body, name=name, grid=(S // ts,), in_specs=[row, row],
        out_specs=[row, row, pl.BlockSpec((8, 128), lambda i: (0, 0))],
        out_shape=[jax.ShapeDtypeStruct((S, D), F32), jax.ShapeDtypeStruct((S, D), MXU_DTYPE), jax.ShapeDtypeStruct((8, 128), F32)],
        scratch_shapes=[pltpu.VMEM((1, D), F32)],
        compiler_params=_params("arbitrary"),
    )(y, target)


def _adamw(w, m, v, terms, slots, *, name):
    R, C = w.shape
    nt = len(terms)
    tr = _pick(R, 256, 16)
    c1 = 1.0 - ADAM_B1 ** ADAM_STEP
    c2 = 1.0 - ADAM_B2 ** ADAM_STEP

    def body(slot_ref, w_ref, m_ref, v_ref, *rest):
        t_refs = rest[:nt]
        g_ref, d_ref, nm_ref, nv_ref = rest[nt:]
        g = t_refs[0][...].astype(F32)
        for t in t_refs[1:]:
            g = g + t[...].astype(F32)
        mm = ADAM_B1 * m_ref[...] + (1.0 - ADAM_B1) * g
        vv = ADAM_B2 * v_ref[...] + (1.0 - ADAM_B2) * (g * g)
        m_hat = mm / c1
        v_hat = vv / c2
        g_ref[...] = g
        d_ref[...] = -ADAM_LR * (m_hat / (jnp.sqrt(v_hat) + ADAM_EPS) + ADAM_WD * w_ref[...])
        nm_ref[...] = mm
        nv_ref[...] = vv

    row = pl.BlockSpec((tr, C), lambda i, s: (i, 0))
    grid_spec = pltpu.PrefetchScalarGridSpec(
        num_scalar_prefetch=1, grid=(R // tr,),
        in_specs=[row, row, row] + [pl.BlockSpec((None, tr, C), lambda i, s, t=t: (s[t], i, 0)) for t in range(nt)],
        out_specs=[row] * 4)
    return pl.pallas_call(
        body, name=name, grid_spec=grid_spec, out_shape=[jax.ShapeDtypeStruct((R, C), F32)] * 4,
        compiler_params=_params("parallel"),
    )(slots, w, m, v, *terms)


def _chip_partials(g, sib, core, *, name):
    _, R, C = g.shape
    tr = _pick(R, 1024, 16)

    def body(core_ref, g_ref, s_ref, o_ref):
        o_ref[...] = (g_ref[...] + s_ref[...].astype(F32)).astype(o_ref.dtype)

    grid_spec = pltpu.PrefetchScalarGridSpec(
        num_scalar_prefetch=1, grid=(4, R // tr),
        in_specs=[pl.BlockSpec((None, tr, C), lambda k, i, c: (2 * k + c[0], i, 0)),
                  pl.BlockSpec((None, tr, C), lambda k, i, c: (k, i, 0))],
        out_specs=pl.BlockSpec((None, tr, C), lambda k, i, c: (k, i, 0)))
    return pl.pallas_call(
        body, name=name, grid_spec=grid_spec, out_shape=jax.ShapeDtypeStruct((4, R, C), sib.dtype),
        compiler_params=_params("parallel", "parallel"),
    )(core, g, sib)


_ANY = pl.BlockSpec(memory_space=pl.ANY)


def _place():
    return lax.axis_index("x"), lax.axis_index("y"), lax.axis_index("c")


def _all_gather(shard, *, name):
    R, C = shard.shape

    def body(x_ref, out_ref, send_sems, recv_sems, local_sem):
        x, y, c = _place()
        me, sibling = (x, y, c), (x, y, 1 - c)
        chips = [(1 - x, y), (x, 1 - y), (1 - x, 1 - y)]

        def slot(px, py, pc):
            return out_ref.at[4 * px + 2 * py + pc]

        def copy(k, block, to, src=None):
            return pltpu.make_async_remote_copy(
                src_ref=slot(*block) if src is None else src, dst_ref=slot(*block),
                send_sem=send_sems.at[k], recv_sem=recv_sems.at[k], device_id=to, device_id_type=MESH)

        mine = pltpu.make_async_copy(x_ref, slot(*me), local_sem)
        mine.start()
        first = [copy(0, me, sibling, src=x_ref)]
        first += [copy(1 + j, me, (*chip, c), src=x_ref) for j, chip in enumerate(chips)]
        for cp in first:
            cp.start()
        passed = [copy(4 + j, (*chip, c), sibling) for j, chip in enumerate(chips)]
        for j, chip in enumerate(chips):
            copy(1 + j, (*chip, c), me).wait_recv()
            passed[j].start()
        copy(0, sibling, me).wait_recv()
        for j, chip in enumerate(chips):
            copy(4 + j, (*chip, 1 - c), me).wait_recv()
        for cp in first + passed:
            cp.wait_send()
        mine.wait()

    return pl.pallas_call(
        body, name=name, in_specs=[_ANY], out_specs=_ANY,
        out_shape=jax.ShapeDtypeStruct((N_DEV, R, C), shard.dtype),
        scratch_shapes=[pltpu.SemaphoreType.DMA((7,)), pltpu.SemaphoreType.DMA((7,)), pltpu.SemaphoreType.DMA],
    )(shard)


def _rs_sibling(g, *, name):
    _, R, C = g.shape

    def body(g_ref, sib_ref, send_sems, recv_sems):
        x, y, c = _place()
        sends = [pltpu.make_async_remote_copy(
            src_ref=g_ref.at[2 * k + (1 - c)], dst_ref=sib_ref.at[k], send_sem=send_sems.at[k], recv_sem=recv_sems.at[k],
            device_id=(x, y, 1 - c), device_id_type=MESH) for k in range(4)]
        for cp in sends:
            cp.start()
        for cp in sends:
            cp.wait_recv()
        for cp in sends:
            cp.wait_send()

    return pl.pallas_call(
        body, name=name, in_specs=[_ANY], out_specs=_ANY, out_shape=jax.ShapeDtypeStruct((4, R, C), g.dtype),
        scratch_shapes=[pltpu.SemaphoreType.DMA((4,)), pltpu.SemaphoreType.DMA((4,))],
    )(g)


def _rs_chips(part, *, name):
    _, R, C = part.shape

    def body(p_ref, out_ref, send_sems, recv_sems):
        x, y, c = _place()
        chips = [(1 - x, y), (x, 1 - y), (1 - x, 1 - y)]
        sends = [pltpu.make_async_remote_copy(
            src_ref=p_ref.at[2 * px + py], dst_ref=out_ref.at[j], send_sem=send_sems.at[j], recv_sem=recv_sems.at[j],
            device_id=(px, py, c), device_id_type=MESH) for j, (px, py) in enumerate(chips)]
        for cp in sends:
            cp.start()
        for cp in sends:
            cp.wait_recv()
        for cp in sends:
            cp.wait_send()

    return pl.pallas_call(
        body, name=name, in_specs=[_ANY], out_specs=_ANY, out_shape=jax.ShapeDtypeStruct((3, R, C), part.dtype),
        scratch_shapes=[pltpu.SemaphoreType.DMA((3,)), pltpu.SemaphoreType.DMA((3,))],
    )(part)


def _interleave_rows(wt):
    F2, D = wt.shape
    return wt.reshape(2, F2 // 256, 128, D).transpose(1, 0, 2, 3).reshape(F2, D)


def _deinterleave_rows(wt):
    F2, D = wt.shape
    return wt.reshape(F2 // 256, 2, 128, D).transpose(1, 0, 2, 3).reshape(F2, D)


def _interleave_cols(v):
    k, F2 = v.shape
    return v.reshape(k, 2, F2 // 256, 128).transpose(0, 2, 1, 3).reshape(k, F2)


def _deinterleave_cols(v):
    k, F2 = v.shape
    return v.reshape(k, F2 // 256, 2, 128).transpose(0, 2, 1, 3).reshape(k, F2)


def _pack_rows(parts):
    return jnp.concatenate(parts, axis=0)


def _flat_pack(parts, width):
    flat = jnp.concatenate([p.reshape(-1) for p in parts])
    pad = (-flat.shape[0]) % (8 * width)
    return jnp.pad(flat, (0, pad)).reshape(-1, width)


def _flat_unpack(packed, shapes):
    flat = packed.reshape(-1)
    out, off = [], 0
    for shp in shapes:
        n = int(np.prod(shp))
        out.append(flat[off:off + n].reshape(shp))
        off += n
    return out


def _ffn_forward(x, g_ffn, wupT, wdown, dw_i, dwb_i, tag):
    hf = _rms_fwd(x, g_ffn, name=f"ffn{tag}_rms")
    up = _mm(hf, wupT, mode="nt", out_dtype=MXU_DTYPE, name=f"ffn{tag}_up", tm=2048, tn=512)
    act = _ffn_act_fwd(up, dw_i, dwb_i, name=f"ffn{tag}_act")
    y = _mm(act, wdown, mode="nn", out_dtype=F32, name=f"ffn{tag}_down", residual=x)
    return y, (hf, up, act)


def _ffn_backward(x, g_ffn, wupT, wdown, dw_i, dwb_i, saved, dy, dyb, tag):
    hf, up, act = saved
    dact = _mm(dyb, wdown, mode="nt", out_dtype=MXU_DTYPE, name=f"ffn{tag}_dact", tm=1024, tn=1408)
    d_wdown = _mm(act, dyb, mode="tn", out_dtype=F32, name=f"ffn{tag}_dwdown", tm=1408, tk=2048)
    dup, d_dw_i, d_dwb_i = _ffn_act_bwd(up, dact, dw_i, dwb_i, name=f"ffn{tag}_actbwd")
    dhf = _mm(dup, wupT, mode="nn", out_dtype=F32, name=f"ffn{tag}_dhf", tm=512)
    d_wupT = _mm(dup, hf, mode="tn", out_dtype=F32, name=f"ffn{tag}_dwup", tm=1408, tk=2048)
    dx, dxb, dg, cs = _rms_bwd(x, g_ffn, [dhf], dy, name=f"ffn{tag}_rmsbwd")
    return dx, dxb, cs, dict(w_upT=d_wupT, w_down=d_wdown, dw=d_dw_i[0:FFN_KERNEL], dw_b=d_dwb_i, norm=dg)


def _local_step(x, target, p):
    S, D = x.shape
    H = D // HEAD_DIM
    h0 = _rms_fwd(x, p["norm_mix"][0:1], name="l0_rms")
    u = _mm(h0, p["w_inT"], mode="nt", out_dtype=F32, name="l0_in", bias=p["cm_b_in"])
    c, s = _cm_fwd(u, p["cm_dw"], p["cm_dw_b"], p["cm_ln_g"], p["cm_ln_b"], name="l0_conv")
    x1 = _mm(s, p["w_out"], mode="nn", out_dtype=F32, name="l0_out", bias=p["cm_b_out"], residual=x)
    x2, sv0 = _ffn_forward(x1, p["norm_ffn"][0:1], p["w_upT"][0], p["w_down"][0], p["ff_dw"][0], p["ff_dw_b"][0:1], 0)
    dils = [dil for _, dil in DILATED_GROUPS]
    assert dils[0] == 1
    h1s = [t.reshape(S, D) for t in _rms_fwd(x2, p["norm_mix"][1:2], name="l1_rms", subs=tuple(dils[1:]))]
    qkvs, rqks, os_, ls_ = [], [], [], []
    for g in range(len(dils)):
        qkv_g, r_g = _qkv_proj(h1s[g], p["w_qkvT"], grp=g, name=f"l1_qkv{g}")
        qkvs.append(qkv_g)
        rqks.append(r_g)
        o, l = _attn_fwd(qkvs[g], p["at_q_norm"][g * H:(g + 1) * H], p["at_k_norm"][g * H:(g + 1) * H], grp=g, name=f"l1_attn{g}")
        os_.append(o)
        ls_.append(l)
    outb, lses = _attn_merge(os_, ls_, dils, name="l1_merge")
    x3 = _mm(outb, p["w_o"], mode="nn", out_dtype=F32, name="l1_o", residual=x2)
    x4, sv1 = _ffn_forward(x3, p["norm_ffn"][1:2], p["w_upT"][1], p["w_down"][1], p["ff_dw"][1], p["ff_dw_b"][1:2], 1)
    dx4, dx4b, loss = _loss_head(x4, target, name="loss")
    dx3, dx3b, _, gf1 = _ffn_backward(x3, p["norm_ffn"][1:2], p["w_upT"][1], p["w_down"][1], p["ff_dw"][1], p["ff_dw_b"][1:2],
                                      sv1, dx4, dx4b, 1)
    do = _mm(dx3b, p["w_o"], mode="nt", out_dtype=MXU_DTYPE, name="l1_do")
    d_wo = _mm(outb, dx3b, mode="tn", out_dtype=F32, name="l1_dwo", tk=2048)
    deltas, dos = _attn_delta(do, outb, dils, name="l1_delta")
    dh1s, d_wqkvT, dqg, dkg = [], [], [], []
    for g, dil in enumerate(dils):
        dqkv_g, a, b_ = _attn_bwd(qkvs[g], rqks[g], dos[g], lses[g], deltas[g], p["at_q_norm"][g * H:(g + 1) * H],
                                  p["at_k_norm"][g * H:(g + 1) * H], grp=g, name=f"l1_attnbwd{g}")
        dqg.append(a)
        dkg.append(b_)
        d_wqkvT.append(_mm(dqkv_g, h1s[g], mode="tn", out_dtype=F32, name=f"l1_dwqkv{g}", tk=2048))
        dh1s.append(_mm(dqkv_g, p["w_qkvT"], mode="nn", out_dtype=F32, name=f"l1_dh{g}", b_off=g * 3 * D, b_len=3 * D))
    dx2, dx2b, dgm1, _ = _rms_bwd(x2, p["norm_mix"][1:2], dh1s[0:1], dx3, name="l1_rmsbwd",
                                  dh_subs=[(dh1s[g].reshape(dils[g], S // dils[g], D), dils[g]) for g in range(1, len(dils))])
    dx1, dx1b, cs1, gf0 = _ffn_backward(x1, p["norm_ffn"][0:1], p["w_upT"][0], p["w_down"][0], p["ff_dw"][0], p["ff_dw_b"][0:1],
                                        sv0, dx2, dx2b, 0)
    ds = _mm(dx1b, p["w_out"], mode="nt", out_dtype=F32, name="l0_ds")
    d_wout = _mm(s, dx1b, mode="tn", out_dtype=F32, name="l0_dwout", tk=2048)
    dc, d_lng, d_lnb = _cm_ln_bwd(c, ds, p["cm_ln_g"], p["cm_ln_b"], name="l0_lnbwd")
    du, d_cmdw, d_cmdwb, d_bin = _cm_conv_bwd(dc, u, p["cm_dw"], name="l0_convbwd")
    dh0 = _mm(du, p["w_inT"], mode="nn", out_dtype=F32, name="l0_dh")
    d_winT = _mm(du, h0, mode="tn", out_dtype=F32, name="l0_dwin", tk=2048)
    grad_x, _, dgm0, _ = _rms_bwd(x, p["norm_mix"][0:1], [dh0], dx1, name="l0_rmsbwd")
    grads = dict(
        norm_mix=jnp.concatenate([dgm0, dgm1], axis=0),
        norm_ffn=jnp.concatenate([gf0["norm"], gf1["norm"]], axis=0),
        w_inT=d_winT, cm_b_in=d_bin, cm_dw=d_cmdw[0:CONV_KERNEL], cm_dw_b=d_cmdwb, cm_ln_g=d_lng, cm_ln_b=d_lnb,
        w_out=d_wout, cm_b_out=cs1,
        w_qkvT=jnp.concatenate(d_wqkvT, axis=0), at_q_norm=jnp.concatenate(dqg, axis=0), at_k_norm=jnp.concatenate(dkg, axis=0),
        w_o=d_wo,
        w_upT=[gf0["w_upT"], gf1["w_upT"]], w_down=[gf0["w_down"], gf1["w_down"]],
        ff_dw=jnp.stack([gf0["dw"], gf1["dw"]]), ff_dw_b=jnp.concatenate([gf0["dw_b"], gf1["dw_b"]], axis=0),
    )
    return loss, grad_x, grads


_BIG = ("cm_w_in", "cm_w_out", "at_w_qkv", "at_w_out", "ff_w_up", "ff_w_down")
_TRANSPOSED = ("cm_w_in", "at_w_qkv", "ff_w_up")
_SMALL = ("norm_mix", "norm_ffn", "cm_b_in", "cm_dw_b", "cm_ln_g", "cm_ln_b", "cm_b_out", "at_q_norm", "at_k_norm",
          "ff_dw_b", "cm_dw", "ff_dw")
_SMALL_SHARDED = ("cm_dw", "ff_dw")
_ORDER = ("norm_mix", "norm_ffn", "cm_w_in", "cm_b_in", "cm_dw", "cm_dw_b", "cm_ln_g", "cm_ln_b", "cm_w_out", "cm_b_out",
          "at_w_qkv", "at_q_norm", "at_k_norm", "at_w_out", "ff_w_up", "ff_dw", "ff_dw_b", "ff_w_down")


def _big_rows(t):
    parts = []
    for n in _BIG:
        a = t[n]
        mats = [a[l] for l in range(a.shape[0])]
        if n in _TRANSPOSED:
            mats = [m.T for m in mats]
        parts += mats
    return _pack_rows(parts)


def _big_unrows(packed, like):
    out, off = {}, 0
    for n in _BIG:
        a = like[n]
        mats = []
        for l in range(a.shape[0]):
            rows, cols = (a.shape[2], a.shape[1]) if n in _TRANSPOSED else (a.shape[1], a.shape[2])
            m = packed[off:off + rows]
            off += rows
            mats.append(m.T if n in _TRANSPOSED else m)
        out[n] = jnp.stack(mats)
    return out


def kernel(x, norm_mix, norm_ffn, cm_w_in, cm_b_in, cm_dw, cm_dw_b, cm_ln_g, cm_ln_b, cm_w_out, cm_b_out, at_w_qkv, at_q_norm, at_k_norm, at_w_out, ff_w_up, ff_dw, ff_dw_b, ff_w_down, loss_target, m_norm_mix, m_norm_ffn, m_cm_w_in, m_cm_b_in, m_cm_dw, m_cm_dw_b, m_cm_ln_g, m_cm_ln_b, m_cm_w_out, m_cm_b_out, m_at_w_qkv, m_at_q_norm, m_at_k_norm, m_at_w_out, m_ff_w_up, m_ff_dw, m_ff_dw_b, m_ff_w_down, v_norm_mix, v_norm_ffn, v_cm_w_in, v_cm_b_in, v_cm_dw, v_cm_dw_b, v_cm_ln_g, v_cm_ln_b, v_cm_w_out, v_cm_b_out, v_at_w_qkv, v_at_q_norm, v_at_k_norm, v_at_w_out, v_ff_w_up, v_ff_dw, v_ff_dw_b, v_ff_w_down):
    w = dict(norm_mix=norm_mix, norm_ffn=norm_ffn, cm_w_in=cm_w_in, cm_b_in=cm_b_in, cm_dw=cm_dw, cm_dw_b=cm_dw_b, cm_ln_g=cm_ln_g,
             cm_ln_b=cm_ln_b, cm_w_out=cm_w_out, cm_b_out=cm_b_out, at_w_qkv=at_w_qkv, at_q_norm=at_q_norm, at_k_norm=at_k_norm,
             at_w_out=at_w_out, ff_w_up=ff_w_up, ff_dw=ff_dw, ff_dw_b=ff_dw_b, ff_w_down=ff_w_down)
    m = dict(norm_mix=m_norm_mix, norm_ffn=m_norm_ffn, cm_w_in=m_cm_w_in, cm_b_in=m_cm_b_in, cm_dw=m_cm_dw, cm_dw_b=m_cm_dw_b,
             cm_ln_g=m_cm_ln_g, cm_ln_b=m_cm_ln_b, cm_w_out=m_cm_w_out, cm_b_out=m_cm_b_out, at_w_qkv=m_at_w_qkv,
             at_q_norm=m_at_q_norm, at_k_norm=m_at_k_norm, at_w_out=m_at_w_out, ff_w_up=m_ff_w_up, ff_dw=m_ff_dw,
             ff_dw_b=m_ff_dw_b, ff_w_down=m_ff_w_down)
    v = dict(norm_mix=v_norm_mix, norm_ffn=v_norm_ffn, cm_w_in=v_cm_w_in, cm_b_in=v_cm_b_in, cm_dw=v_cm_dw, cm_dw_b=v_cm_dw_b,
             cm_ln_g=v_cm_ln_g, cm_ln_b=v_cm_ln_b, cm_w_out=v_cm_w_out, cm_b_out=v_cm_b_out, at_w_qkv=v_at_w_qkv,
             at_q_norm=v_at_q_norm, at_k_norm=v_at_k_norm, at_w_out=v_at_w_out, ff_w_up=v_ff_w_up, ff_dw=v_ff_dw,
             ff_dw_b=v_ff_dw_b, ff_w_down=v_ff_w_down)
    S, D = x.shape[1], x.shape[2]
    F2 = ff_dw_b.shape[1]
    H3 = at_q_norm.shape[1]
    me = 4 * lax.axis_index("x") + 2 * lax.axis_index("y") + lax.axis_index("c")

    w_rows = _big_rows(w)
    gathered = _all_gather(w_rows.astype(MXU_DTYPE), name="gather_weights")
    full, off = {}, 0
    for n in _BIG:
        a = w[n]
        mats = []
        for l in range(a.shape[0]):
            rows = a.shape[2] if n in _TRANSPOSED else a.shape[1]
            mats.append(gathered[:, off:off + rows, :].reshape(N_DEV * rows, D))
            off += rows
        full[n] = mats
    small_sh = _flat_pack([cm_dw, ff_dw], D)
    small_g = _all_gather(small_sh, name="gather_small")
    cm_dw_full = jnp.concatenate([_flat_unpack(small_g[j], [cm_dw.shape, ff_dw.shape])[0][0] for j in range(N_DEV)], axis=-1)
    ff_dw_full = jnp.concatenate([_flat_unpack(small_g[j], [cm_dw.shape, ff_dw.shape])[1] for j in range(N_DEV)], axis=-1)

    p = dict(
        norm_mix=norm_mix, norm_ffn=norm_ffn, cm_b_in=cm_b_in, cm_dw=cm_dw_full, cm_dw_b=cm_dw_b, cm_ln_g=cm_ln_g, cm_ln_b=cm_ln_b,
        cm_b_out=cm_b_out, at_q_norm=at_q_norm[0], at_k_norm=at_k_norm[0],
        w_inT=full["cm_w_in"][0], w_out=full["cm_w_out"][0], w_qkvT=full["at_w_qkv"][0], w_o=full["at_w_out"][0],
        w_upT=[_interleave_rows(t) for t in full["ff_w_up"]], w_down=full["ff_w_down"],
        ff_dw=jnp.stack([_interleave_cols(ff_dw_full[l]) for l in range(ff_dw_full.shape[0])]),
        ff_dw_b=_interleave_cols(ff_dw_b),
    )
    loss8, grad_x, g = _local_step(x[0], loss_target[0], p)
    loss = lax.psum(loss8[0, 0], ("x", "y", "c"))

    pieces = [g["w_inT"], g["w_out"], g["w_qkvT"], g["w_o"], _deinterleave_rows(g["w_upT"][0]), _deinterleave_rows(g["w_upT"][1]),
              g["w_down"][0], g["w_down"][1]]
    g_rows = jnp.concatenate([t.reshape(N_DEV, t.shape[0] // N_DEV, D) for t in pieces], axis=1)
    ix, iy, ic = lax.axis_index("x"), lax.axis_index("y"), lax.axis_index("c")
    chip = 2 * ix + iy
    sib = _rs_sibling(g_rows.astype(WIRE_DTYPE), name="reduce_sibling")
    part = _chip_partials(g_rows, sib, jnp.stack([ic]).astype(jnp.int32), name="reduce_add")
    recv = _rs_chips(part, name="reduce_chips")
    slots = jnp.stack([me, chip, 0 * me, 0 * me + 1, 0 * me + 2]).astype(jnp.int32)
    gb, db, mb, vb = _adamw(w_rows, _big_rows(m), _big_rows(v), [g_rows, sib, recv, recv, recv], slots, name="adamw_big")
    big = [_big_unrows(t, w) for t in (gb, db, mb, vb)]

    g_small = dict(g)
    g_small["cm_b_in"] = g["cm_b_in"]
    g_small["at_q_norm"] = g["at_q_norm"][None]
    g_small["at_k_norm"] = g["at_k_norm"][None]
    g_small["ff_dw_b"] = _deinterleave_cols(g["ff_dw_b"])
    g_small["cm_dw"] = g["cm_dw"][None]
    g_small["ff_dw"] = jnp.stack([_deinterleave_cols(g["ff_dw"][l]) for l in range(g["ff_dw"].shape[0])])
    small_shapes = [g_small[n].shape for n in _SMALL]
    gs_parts = _all_gather(_flat_pack([g_small[n] for n in _SMALL], D), name="gather_small_grads")

    def embed(t, n):
        if n not in _SMALL_SHARDED:
            return t
        full_shape = t.shape[:-1] + (t.shape[-1] * N_DEV,)
        return lax.dynamic_update_slice_in_dim(jnp.zeros(full_shape, F32), t, me * t.shape[-1], axis=t.ndim - 1)

    packs = [_flat_pack([embed(tree[n], n) for n in _SMALL], D) for tree in (w, m, v)]
    gs, ds_, ms, vs = _adamw(packs[0], packs[1], packs[2], [gs_parts] * N_DEV, jnp.arange(N_DEV, dtype=jnp.int32),
                             name="adamw_small")
    small = []
    for t in (gs, ds_, ms, vs):
        un = dict(zip(_SMALL, _flat_unpack(t, small_shapes)))
        for n in _SMALL_SHARDED:
            width = w[n].shape[-1]
            un[n] = lax.dynamic_slice_in_dim(un[n], me * width, width, axis=un[n].ndim - 1)
        small.append({n: un[n].reshape(w[n].shape) for n in _SMALL})

    outs = [loss, grad_x[None]]
    for k in range(4):
        for n in _ORDER:
            outs.append(big[k][n] if n in _BIG else small[k][n])
    return tuple(outs)
```

```python
import functools

import jax
import jax.numpy as jnp
import numpy as np
from jax import lax
from jax.experimental import pallas as pl
from jax.experimental.pallas import tpu as pltpu

F32 = jnp.float32
MXU_DTYPE = jnp.bfloat16
WIRE_DTYPE = jnp.bfloat16
EPS = 1e-6
NEG = -1e30
HEAD_DIM = 128
BLOCK = 128
DILATED_GROUPS = ((128, 1), (512, 4), (2048, 16))
ALIBI_MAX = 8.0
CONV_KERNEL = 31
CONV_HALO = 32
CONV_ROWS = 64
FFN_KERNEL = 3
FFN_HALO = 16
FFN_ROWS = 64
ADAM_LR, ADAM_B1, ADAM_B2, ADAM_EPS, ADAM_WD, ADAM_STEP = 0.001, 0.9, 0.999, 1e-08, 0.01, 10
V7X_VMEM_BYTES = 64 * 1024 * 1024
VMEM_LIMIT = V7X_VMEM_BYTES * 3 // 4
N_DEV = 8
MESH = pl.DeviceIdType.MESH


def _pick(n, target, align):
    if n <= target:
        return n
    best = None
    for t in range(align, target + 1, align):
        if n % t == 0:
            best = t
    assert best is not None, (n, target, align)
    return best


def _params(*sem):
    return pltpu.CompilerParams(dimension_semantics=sem, vmem_limit_bytes=VMEM_LIMIT)


def _sigmoid(x):
    return 1.0 / (1.0 + jnp.exp(-x))


_DIMS = {"nn": ((1,), (0,)), "nt": ((1,), (1,)), "tn": ((0,), (0,))}


def _mm(a, b, *, mode, out_dtype, name, tm=1024, tn=1024, tk=None, bias=None, residual=None, b_off=0, b_len=None):
    if mode == "tn":
        K, M = a.shape
    else:
        M, K = a.shape
    if mode == "nt":
        N = b.shape[0] if b_len is None else b_len
    else:
        N = b.shape[1]
    if b_len is not None:
        assert mode == "nt" or (mode == "nn" and K == b_len)
    tm = _pick(M, tm, 128 if mode == "tn" else 16)
    tn = _pick(N, tn, 128)
    tk = K if tk is None else _pick(K, tk, 128 if mode != "tn" else 16)
    nk = K // tk
    unit = tn if mode == "nt" else tk
    assert b_off % unit == 0
    kb0 = b_off // unit
    if mode == "tn":
        a_spec = pl.BlockSpec((tk, tm), lambda i, j, k: (k, i))
    else:
        a_spec = pl.BlockSpec((tm, tk), lambda i, j, k: (i, k))
    if mode == "nt":
        b_spec = pl.BlockSpec((tn, tk), lambda i, j, k: (j + kb0, k))
    else:
        b_spec = pl.BlockSpec((tk, tn), lambda i, j, k: (k + kb0, j))
    in_specs = [a_spec, b_spec]
    args = [a, b]
    if bias is not None:
        in_specs.append(pl.BlockSpec((1, tn), lambda i, j, k: (0, j)))
        args.append(bias)
    if residual is not None:
        in_specs.append(pl.BlockSpec((tm, tn), lambda i, j, k: (i, j)))
        args.append(residual)
    has_bias, has_res = bias is not None, residual is not None

    def body(*refs):
        a_ref, b_ref = refs[0], refs[1]
        pos = 2
        bias_ref = res_ref = None
        if has_bias:
            bias_ref = refs[pos]
            pos += 1
        if has_res:
            res_ref = refs[pos]
            pos += 1
        o_ref = refs[pos]
        acc_ref = refs[pos + 1] if nk > 1 else None

        def finish(acc):
            if has_bias:
                acc = acc + bias_ref[...]
            if has_res:
                acc = acc + res_ref[...]
            o_ref[...] = acc.astype(o_ref.dtype)

        part = lax.dot_general(a_ref[...].astype(MXU_DTYPE), b_ref[...].astype(MXU_DTYPE), (_DIMS[mode], ((), ())),
                               preferred_element_type=F32)
        if nk == 1:
            finish(part)
        else:
            k = pl.program_id(2)

            @pl.when(k == 0)
            def _():
                acc_ref[...] = part

            @pl.when(jnp.logical_and(k > 0, k < nk - 1))
            def _():
                acc_ref[...] += part

            @pl.when(k == nk - 1)
            def _():
                finish(acc_ref[...] + part)

    return pl.pallas_call(
        body, name=name, grid=(M // tm, N // tn, nk), in_specs=in_specs,
        out_specs=pl.BlockSpec((tm, tn), lambda i, j, k: (i, j)),
        out_shape=jax.ShapeDtypeStruct((M, N), out_dtype),
        scratch_shapes=[pltpu.VMEM((tm, tn), F32)] if nk > 1 else [],
        compiler_params=_params("parallel", "parallel", "arbitrary"),
    )(*args)


SUB_TILE = 512


def _sub_spec(dil, ts, cols):
    return pl.BlockSpec((dil, ts // dil, cols), lambda i: (0, i, 0))


def _tok_to_sub(tok_ref, dst_ref, dil):
    nc, ts, _ = tok_ref.shape
    for c in range(nc):
        for r in range(dil):
            dst_ref[r, :, c * 128:(c + 1) * 128] = tok_ref.at[c][pl.ds(r, ts // dil, stride=dil), :].astype(dst_ref.dtype)


def _sub_to_tok(src_ref, tok_ref, dil):
    nc, ts, _ = tok_ref.shape
    for c in range(nc):
        for r in range(dil):
            tok_ref.at[c][pl.ds(r, ts // dil, stride=dil), :] = src_ref[r, :, c * 128:(c + 1) * 128].astype(F32)


def _rms_fwd(x, g, *, name, subs=()):
    S, D = x.shape
    ts = _pick(S, SUB_TILE, 16 * max(subs, default=1))
    NC = D // 128

    def body(x_ref, g_ref, h_ref, *rest):
        xv = x_ref[...]
        r = lax.rsqrt(jnp.mean(xv * xv, axis=-1, keepdims=True) + EPS)
        h = xv * r * g_ref[...]
        h_ref[...] = h.astype(h_ref.dtype)
        if subs:
            tok_ref = rest[-1]
            for c in range(NC):
                tok_ref[c] = h[:, c * 128:(c + 1) * 128]
            for dil, dst_ref in zip(subs, rest):
                _tok_to_sub(tok_ref, dst_ref, dil)

    row = pl.BlockSpec((ts, D), lambda i: (i, 0))
    outs = pl.pallas_call(
        body, name=name, grid=(S // ts,),
        in_specs=[row, pl.BlockSpec((1, D), lambda i: (0, 0))],
        out_specs=[row] + [_sub_spec(dil, ts, D) for dil in subs],
        out_shape=[jax.ShapeDtypeStruct((S, D), MXU_DTYPE)] + [jax.ShapeDtypeStruct((dil, S // dil, D), MXU_DTYPE) for dil in subs],
        scratch_shapes=[pltpu.VMEM((NC, ts, 128), F32)] if subs else [],
        compiler_params=_params("parallel"),
    )(x, g)
    return outs if subs else outs[0]


def _rms_bwd(x, g, dhs, dres, *, name, dh_subs=()):
    S, D = x.shape
    ts = _pick(S, SUB_TILE, 16 * max([dil for _, dil in dh_subs], default=1))
    n_dh, n_sub = len(dhs), len(dh_subs)
    NC = D // 128

    def body(*refs):
        x_ref, g_ref = refs[0], refs[1]
        dh_refs = refs[2:2 + n_dh]
        sub_refs = refs[2 + n_dh:2 + n_dh + n_sub]
        dres_ref, dx_ref, dxb_ref, dg_ref, cs_ref = refs[2 + n_dh + n_sub:7 + n_dh + n_sub]
        i = pl.program_id(0)
        xv = x_ref[...]
        r = lax.rsqrt(jnp.mean(xv * xv, axis=-1, keepdims=True) + EPS)
        xh = xv * r
        dhv = dh_refs[0][...].astype(F32)
        for t in dh_refs[1:]:
            dhv = dhv + t[...].astype(F32)
        for (_, dil), sub_ref in zip(dh_subs, sub_refs):
            tok_ref = refs[-1]
            _sub_to_tok(sub_ref, tok_ref, dil)
            dhv = dhv + jnp.concatenate([tok_ref[c] for c in range(NC)], axis=1)
        gy = dhv * g_ref[...]
        dx = r * (gy - xh * jnp.mean(gy * xh, axis=-1, keepdims=True)) + dres_ref[...]
        dx_ref[...] = dx
        dxb_ref[...] = dx.astype(dxb_ref.dtype)
        dg = jnp.sum(dhv * xh, axis=0, keepdims=True)
        cs = jnp.sum(dx, axis=0, keepdims=True)

        @pl.when(i == 0)
        def _():
            dg_ref[...] = dg
            cs_ref[...] = cs

        @pl.when(i > 0)
        def _():
            dg_ref[...] += dg
            cs_ref[...] += cs

    row = pl.BlockSpec((ts, D), lambda i: (i, 0))
    vec = pl.BlockSpec((1, D), lambda i: (0, 0))
    return pl.pallas_call(
        body, name=name, grid=(S // ts,),
        in_specs=[row, vec] + [row] * n_dh + [_sub_spec(dil, ts, D) for _, dil in dh_subs] + [row],
        out_specs=[row, row, vec, vec],
        out_shape=[jax.ShapeDtypeStruct((S, D), F32), jax.ShapeDtypeStruct((S, D), MXU_DTYPE),
                   jax.ShapeDtypeStruct((1, D), F32), jax.ShapeDtypeStruct((1, D), F32)],
        scratch_shapes=[pltpu.VMEM((NC, ts, 128), F32)] if n_sub else [],
        compiler_params=_params("arbitrary"),
    )(x, g, *dhs, *[a for a, _ in dh_subs], dres)


def _conv_phases(ph_ref, ts):
    n = ts + CONV_HALO - 8
    for b in range(1, 8):
        ph_ref[b, 0:n, :] = ph_ref[0, pl.ds(b, n), :]


def _phase_taps(base, step=1):
    groups = {}
    for k in range(CONV_KERNEL):
        a, b = divmod(base + step * k, 8)
        groups.setdefault(b, []).append((a, k))
    out = []
    for b in sorted(groups):
        ak = sorted(groups[b])
        assert [a for a, _ in ak] == list(range(ak[0][0], ak[0][0] + len(ak)))
        out.append((b, ak[0][0], [k for _, k in ak]))
    return out


def _cm_fwd(u, dw, dw_b, ln_g, ln_b, *, name, exchange=None):
    S, D2 = u.shape
    D = D2 // 2
    ts = _pick(S, 256, CONV_HALO)
    hb = ts // CONV_HALO
    ex_in, ex_out, ex_scr = ([], [], []) if exchange is None else (exchange.operands, exchange.out_shapes, exchange.scratch)

    def body(u_ref, up_ref, dw_ref, dwb_ref, g_ref, b_ref, *rest):
        xi = rest[:len(ex_in)]
        c_ref, s_ref = rest[len(ex_in):len(ex_in) + 2]
        xo = rest[len(ex_in) + 2:len(ex_in) + 2 + len(ex_out)]
        ext_ref = rest[len(ex_in) + 2 + len(ex_out)]
        i = pl.program_id(0)
        if exchange is not None:
            exchange.emit(i, S // ts, xi, xo, rest[len(ex_in) + 3 + len(ex_out):])
        prev = up_ref[:, :D] * _sigmoid(up_ref[:, D:])
        ext_ref[0:CONV_HALO, :] = jnp.where(i > 0, prev, 0.0)
        ext_ref[CONV_HALO:CONV_HALO + ts, :] = u_ref[:, :D] * _sigmoid(u_ref[:, D:])
        for cc in range(D // 128):
            sl = slice(cc * 128, (cc + 1) * 128)
            acc = jnp.zeros((ts, 128), F32) + dwb_ref[:, sl]
            for k in range(CONV_KERNEL):
                acc = acc + dw_ref[k:k + 1, sl] * ext_ref[pl.ds(CONV_HALO - (CONV_KERNEL - 1) + k, ts), sl]
            c_ref[:, sl] = acc
        c = c_ref[...]
        mu = jnp.mean(c, axis=-1, keepdims=True)
        xc = c - mu
        rstd = lax.rsqrt(jnp.mean(xc * xc, axis=-1, keepdims=True) + EPS)
        y = xc * rstd * g_ref[...] + b_ref[...]
        s_ref[...] = (y * _sigmoid(y)).astype(s_ref.dtype)

    vec = pl.BlockSpec((1, D), lambda i: (0, 0))
    return pl.pallas_call(
        body, name=name, grid=(S // ts,),
        in_specs=[pl.BlockSpec((ts, D2), lambda i: (i, 0)),
                  pl.BlockSpec((CONV_HALO, D2), lambda i: (jnp.maximum(i * hb - 1, 0), 0)),
                  pl.BlockSpec((CONV_KERNEL, D), lambda i: (0, 0)), vec, vec, vec] + [_ANY] * len(ex_in),
        out_specs=[pl.BlockSpec((ts, D), lambda i: (i, 0)), pl.BlockSpec((ts, D), lambda i: (i, 0))] + [_ANY] * len(ex_out),
        out_shape=[jax.ShapeDtypeStruct((S, D), F32), jax.ShapeDtypeStruct((S, D), MXU_DTYPE)] + list(ex_out),
        scratch_shapes=[pltpu.VMEM((ts + CONV_HALO, D), F32)] + list(ex_scr),
        compiler_params=_params("parallel" if exchange is None else "arbitrary"),
    )(u, u, dw, dw_b, ln_g, ln_b, *ex_in)


def _cm_ln_bwd(c, ds, ln_g, ln_b, *, name):
    S, D = c.shape
    ts = _pick(S, 512, 16)

    def body(c_ref, ds_ref, g_ref, b_ref, dc_ref, dg_ref, db_ref):
        i = pl.program_id(0)
        cv = c_ref[...]
        mu = jnp.mean(cv, axis=-1, keepdims=True)
        xc = cv - mu
        rstd = lax.rsqrt(jnp.mean(xc * xc, axis=-1, keepdims=True) + EPS)
        xh = xc * rstd
        y = xh * g_ref[...] + b_ref[...]
        sg = _sigmoid(y)
        dy = ds_ref[...].astype(F32) * (sg * (1.0 + y * (1.0 - sg)))
        gy = dy * g_ref[...]
        dc_ref[...] = rstd * (gy - jnp.mean(gy, axis=-1, keepdims=True) - xh * jnp.mean(gy * xh, axis=-1, keepdims=True))
        dg = jnp.sum(dy * xh, axis=0, keepdims=True)
        db = jnp.sum(dy, axis=0, keepdims=True)

        @pl.when(i == 0)
        def _():
            dg_ref[...] = dg
            db_ref[...] = db

        @pl.when(i > 0)
        def _():
            dg_ref[...] += dg
            db_ref[...] += db

    row = pl.BlockSpec((ts, D), lambda i: (i, 0))
    vec = pl.BlockSpec((1, D), lambda i: (0, 0))
    return pl.pallas_call(
        body, name=name, grid=(S // ts,), in_specs=[row, row, vec, vec], out_specs=[row, vec, vec],
        out_shape=[jax.ShapeDtypeStruct((S, D), F32), jax.ShapeDtypeStruct((1, D), F32), jax.ShapeDtypeStruct((1, D), F32)],
        compiler_params=_params("arbitrary"),
    )(c, ds, ln_g, ln_b)


def _cm_conv_bwd(dc, u, dw, *, name):
    S, D2 = u.shape
    D = D2 // 2
    ts = _pick(S, 256, CONV_HALO)
    hb = ts // CONV_HALO
    n_t = S // ts
    last_h = S // CONV_HALO - 1

    rc = _pick(ts, CONV_ROWS, 8)

    def fold8(v):
        out = v[0:8]
        for j in range(1, v.shape[0] // 8):
            out = out + v[8 * j:8 * j + 8]
        return out

    def body(dc_ref, dcn_ref, u_ref, up_ref, dw_ref, du_ref, ddw_ref, ddwb_ref, dbin_ref, dph_ref, gph_ref, dgl_ref):
        i = pl.program_id(0)
        dph_ref[0, 0:ts, :] = dc_ref[...]
        dph_ref[0, ts:ts + CONV_HALO, :] = jnp.where(i < n_t - 1, dcn_ref[...], 0.0)
        prev = up_ref[:, :D] * _sigmoid(up_ref[:, D:])
        gph_ref[0, 0:CONV_HALO, :] = jnp.where(i > 0, prev, 0.0)
        gph_ref[0, CONV_HALO:CONV_HALO + ts, :] = u_ref[:, :D] * _sigmoid(u_ref[:, D:])
        _conv_phases(dph_ref, ts)
        _conv_phases(gph_ref, ts)

        @pl.when(i == 0)
        def _():
            ddw_ref[...] = jnp.zeros_like(ddw_ref)
            ddwb_ref[...] = jnp.zeros_like(ddwb_ref)
            dbin_ref[...] = jnp.zeros_like(dbin_ref)

        for cc in range(D // 128):
            sl = slice(cc * 128, (cc + 1) * 128)
            sl2 = slice(D + cc * 128, D + (cc + 1) * 128)
            wk = [dw_ref[k:k + 1, sl] for k in range(CONV_KERNEL)]
            acc_a, acc_g = jnp.zeros((8, 128), F32), jnp.zeros((8, 128), F32)
            dgl = jnp.zeros((ts, 128), F32)
            for b, a0, taps in _phase_taps(CONV_KERNEL - 1, -1):
                for j, k in enumerate(taps):
                    dgl = dgl + wk[k] * dph_ref[b, 8 * (a0 + j):8 * (a0 + j) + ts, sl]
            dgl_ref[...] = dgl
            for r0 in range(0, ts, rc):
                dglu = dgl_ref[r0:r0 + rc, :]
                av = u_ref[r0:r0 + rc, sl]
                sg = _sigmoid(u_ref[r0:r0 + rc, sl2])
                da = dglu * sg
                dg = dglu * av * sg * (1.0 - sg)
                du_ref[r0:r0 + rc, sl] = da.astype(du_ref.dtype)
                du_ref[r0:r0 + rc, sl2] = dg.astype(du_ref.dtype)
                acc_a = acc_a + fold8(da)
                acc_g = acc_g + fold8(dg)
            dbin_ref[:, sl] += jnp.sum(acc_a, axis=0, keepdims=True)
            dbin_ref[:, sl2] += jnp.sum(acc_g, axis=0, keepdims=True)
            for gi, (b, a0, taps) in enumerate(_phase_taps(CONV_HALO - (CONV_KERNEL - 1))):
                accs = [jnp.zeros((8, 128), F32) for _ in taps]
                accb = jnp.zeros((8, 128), F32)
                for r0 in range(0, ts, rc):
                    dcc = dph_ref[0, r0:r0 + rc, sl]
                    win = gph_ref[b, 8 * a0 + r0:8 * (a0 + len(taps) - 1) + r0 + rc, sl]
                    for j in range(len(taps)):
                        accs[j] = accs[j] + fold8(dcc * win[8 * j:8 * j + rc])
                    if gi == 0:
                        accb = accb + fold8(dcc)
                for j, k in enumerate(taps):
                    ddw_ref[k:k + 1, sl] += jnp.sum(accs[j], axis=0, keepdims=True)
                if gi == 0:
                    ddwb_ref[:, sl] += jnp.sum(accb, axis=0, keepdims=True)

    return pl.pallas_call(
        body, name=name, grid=(n_t,),
        in_specs=[pl.BlockSpec((ts, D), lambda i: (i, 0)),
                  pl.BlockSpec((CONV_HALO, D), lambda i: (jnp.minimum((i + 1) * hb, last_h), 0)),
                  pl.BlockSpec((ts, D2), lambda i: (i, 0)),
                  pl.BlockSpec((CONV_HALO, D2), lambda i: (jnp.maximum(i * hb - 1, 0), 0)),
                  pl.BlockSpec((CONV_KERNEL, D), lambda i: (0, 0))],
        out_specs=[pl.BlockSpec((ts, D2), lambda i: (i, 0)), pl.BlockSpec((CONV_HALO, D), lambda i: (0, 0)),
                   pl.BlockSpec((1, D), lambda i: (0, 0)), pl.BlockSpec((1, D2), lambda i: (0, 0))],
        out_shape=[jax.ShapeDtypeStruct((S, D2), MXU_DTYPE), jax.ShapeDtypeStruct((CONV_HALO, D), F32),
                   jax.ShapeDtypeStruct((1, D), F32), jax.ShapeDtypeStruct((1, D2), F32)],
        scratch_shapes=[pltpu.VMEM((8, ts + CONV_HALO, D), F32), pltpu.VMEM((8, ts + CONV_HALO, D), F32),
                        pltpu.VMEM((ts, 128), F32)],
        compiler_params=_params("arbitrary"),
    )(dc, dc, u, u, dw)


def _ffn_cols(F2):
    return _pick(F2, 1024, 256)


def _ffn_act_fwd(up, dw, dw_b, *, name):
    S, F2 = up.shape
    ts = _pick(S, 512, 16)
    tc = _ffn_cols(F2)
    hb = ts // FFN_HALO

    rc = _pick(ts, FFN_ROWS, 16)

    def body(u_ref, up_ref, w_ref, b_ref, a_ref, ext_ref):
        i = pl.program_id(1)
        ext_ref[0:FFN_HALO, :] = jnp.where(i > 0, up_ref[...].astype(F32), 0.0)
        ext_ref[FFN_HALO:FFN_HALO + ts, :] = u_ref[...].astype(F32)
        for q in range(tc // 256):
            sls = [slice(q * 256 + half * 128, q * 256 + half * 128 + 128) for half in range(2)]
            wk = [[w_ref[k:k + 1, sl] for k in range(FFN_KERNEL)] for sl in sls]
            bb = [b_ref[:, sl] for sl in sls]
            for r0 in range(0, ts, rc):
                gt, vl = [bb[h] + sum(wk[h][k] * ext_ref[pl.ds(FFN_HALO + r0 - 2 + k, rc), sls[h]] for k in range(FFN_KERNEL))
                          for h in range(2)]
                a_ref[r0:r0 + rc, q * 128:(q + 1) * 128] = (gt * _sigmoid(gt) * vl).astype(a_ref.dtype)

    return pl.pallas_call(
        body, name=name, grid=(F2 // tc, S // ts),
        in_specs=[pl.BlockSpec((ts, tc), lambda j, i: (i, j)),
                  pl.BlockSpec((FFN_HALO, tc), lambda j, i: (jnp.maximum(i * hb - 1, 0), j)),
                  pl.BlockSpec((FFN_KERNEL, tc), lambda j, i: (0, j)),
                  pl.BlockSpec((1, tc), lambda j, i: (0, j))],
        out_specs=pl.BlockSpec((ts, tc // 2), lambda j, i: (i, j)),
        out_shape=jax.ShapeDtypeStruct((S, F2 // 2), MXU_DTYPE),
        scratch_shapes=[pltpu.VMEM((ts + FFN_HALO, tc), F32)],
        compiler_params=_params("parallel", "parallel"),
    )(up, up, dw, dw_b)


def _ffn_act_bwd(up, dact, dw, dw_b, *, name, exchange=None):
    ex_in, ex_out, ex_scr = ([], [], []) if exchange is None else (exchange.operands, exchange.out_shapes, exchange.scratch)
    S, F2 = up.shape
    ts = _pick(S, 512, 16)
    tc = _ffn_cols(F2)
    hb = ts // FFN_HALO
    n_t = S // ts
    last_h = S // FFN_HALO - 1
    E = ts + FFN_HALO

    rc = _pick(ts, FFN_ROWS, 16)

    def fold8(v):
        out = v[0:8]
        for j in range(1, v.shape[0] // 8):
            out = out + v[8 * j:8 * j + 8]
        return out

    def body(u_ref, up_ref, un_ref, da_ref, dan_ref, w_ref, b_ref, *rest):
        xi = rest[:len(ex_in)]
        dup_ref, ddw_ref, ddb_ref = rest[len(ex_in):len(ex_in) + 3]
        xo = rest[len(ex_in) + 3:len(ex_in) + 3 + len(ex_out)]
        ue_ref, dcv_ref = rest[len(ex_in) + 3 + len(ex_out):len(ex_in) + 5 + len(ex_out)]
        i = pl.program_id(1)
        if exchange is not None:
            exchange.emit(pl.program_id(0) * n_t + i, (F2 // tc) * n_t, xi, xo, rest[len(ex_in) + 5 + len(ex_out):])
        ue_ref[0:FFN_HALO, :] = jnp.where(i > 0, up_ref[...].astype(F32), 0.0)
        ue_ref[FFN_HALO:FFN_HALO + ts, :] = u_ref[...].astype(F32)
        ue_ref[FFN_HALO + ts:FFN_HALO + ts + FFN_HALO, :] = jnp.where(i < n_t - 1, un_ref[...].astype(F32), 0.0)

        @pl.when(i == 0)
        def _():
            ddw_ref[...] = jnp.zeros_like(ddw_ref)
            ddb_ref[...] = jnp.zeros_like(ddb_ref)

        for q in range(tc // 256):
            sls = [slice(q * 256 + half * 128, q * 256 + half * 128 + 128) for half in range(2)]
            qs = slice(q * 128, (q + 1) * 128)
            wk = [[w_ref[k:k + 1, sl] for k in range(FFN_KERNEL)] for sl in sls]
            bb = [b_ref[:, sl] for sl in sls]
            acc = [[jnp.zeros((8, 128), F32) for _ in range(FFN_KERNEL)] for _ in range(2)]
            accb = [jnp.zeros((8, 128), F32) for _ in range(2)]
            for r0, rows in [(r, rc) for r in range(0, ts, rc)] + [(ts, FFN_HALO)]:
                xs = [[ue_ref[pl.ds(FFN_HALO + r0 - 2 + k, rows), sls[h]] for k in range(FFN_KERNEL)] for h in range(2)]
                gt, vl = [bb[h] + sum(wk[h][k] * xs[h][k] for k in range(FFN_KERNEL)) for h in range(2)]
                sg = _sigmoid(gt)
                if r0 < ts:
                    dae = da_ref[r0:r0 + rows, qs].astype(F32)
                else:
                    dae = jnp.where(i < n_t - 1, dan_ref[:, qs].astype(F32), 0.0)
                dcv = [dae * vl * (sg * (1.0 + gt * (1.0 - sg))), dae * (gt * sg)]
                for h in range(2):
                    dcv_ref[r0:r0 + rows, sls[h]] = dcv[h]
                    if r0 < ts:
                        for k in range(FFN_KERNEL):
                            acc[h][k] = acc[h][k] + fold8(dcv[h] * xs[h][k])
                        accb[h] = accb[h] + fold8(dcv[h])
            for r0 in range(0, ts, rc):
                for h in range(2):
                    dup = sum(wk[h][2 - j] * dcv_ref[pl.ds(r0 + j, rc), sls[h]] for j in range(FFN_KERNEL))
                    dup_ref[r0:r0 + rc, sls[h]] = dup.astype(dup_ref.dtype)
            for h in range(2):
                for k in range(FFN_KERNEL):
                    ddw_ref[k:k + 1, sls[h]] += jnp.sum(acc[h][k], axis=0, keepdims=True)
                ddb_ref[:, sls[h]] += jnp.sum(accb[h], axis=0, keepdims=True)

    return pl.pallas_call(
        body, name=name, grid=(F2 // tc, n_t),
        in_specs=[pl.BlockSpec((ts, tc), lambda j, i: (i, j)),
                  pl.BlockSpec((FFN_HALO, tc), lambda j, i: (jnp.maximum(i * hb - 1, 0), j)),
                  pl.BlockSpec((FFN_HALO, tc), lambda j, i: (jnp.minimum((i + 1) * hb, last_h), j)),
                  pl.BlockSpec((ts, tc // 2), lambda j, i: (i, j)),
                  pl.BlockSpec((FFN_HALO, tc // 2), lambda j, i: (jnp.minimum((i + 1) * hb, last_h), j)),
                  pl.BlockSpec((FFN_KERNEL, tc), lambda j, i: (0, j)),
                  pl.BlockSpec((1, tc), lambda j, i: (0, j))] + [_ANY] * len(ex_in),
        out_specs=[pl.BlockSpec((ts, tc), lambda j, i: (i, j)), pl.BlockSpec((FFN_HALO, tc), lambda j, i: (0, j)),
                   pl.BlockSpec((1, tc), lambda j, i: (0, j))] + [_ANY] * len(ex_out),
        out_shape=[jax.ShapeDtypeStruct((S, F2), MXU_DTYPE), jax.ShapeDtypeStruct((FFN_HALO, F2), F32),
                   jax.ShapeDtypeStruct((1, F2), F32)] + list(ex_out),
        scratch_shapes=[pltpu.VMEM((ts + 2 * FFN_HALO, tc), F32), pltpu.VMEM((E, tc), F32)] + list(ex_scr),
        compiler_params=_params("parallel" if exchange is None else "arbitrary", "arbitrary"),
    )(up, up, up, dact, dact, dw, dw_b, *ex_in)


def _slopes(n_heads_total):
    return np.asarray(2.0 ** (-ALIBI_MAX * (np.arange(n_heads_total, dtype=np.float32) + 1.0) / n_heads_total), np.float32)


def _qkv_proj(h, w_qkvT, *, grp, name):
    S, D = h.shape
    H = D // HEAD_DIM
    tm = _pick(S, 1024, 16)

    rows = _pick(tm, 256, 16)

    def body(a_ref, b_ref, o_ref, r_ref):
        j = pl.program_id(1)
        r_ref[...] = jnp.zeros_like(r_ref)

        def product(c):
            return lax.dot_general(a_ref[c * rows:(c + 1) * rows, :].astype(MXU_DTYPE), b_ref[...].astype(MXU_DTYPE),
                                   (_DIMS["nt"], ((), ())), preferred_element_type=F32)

        @pl.when(j < 2)
        def _():
            acc = product(0)
            for c in range(tm // rows):
                nxt = product(c + 1) if c + 1 < tm // rows else None
                rs = slice(c * rows, (c + 1) * rows)
                for hd in range(H):
                    hs = slice(hd * HEAD_DIM, (hd + 1) * HEAD_DIM)
                    xv = acc[:, hs]
                    r = lax.rsqrt(jnp.mean(xv * xv, axis=-1, keepdims=True) + EPS)
                    o_ref[rs, hs] = (xv * r).astype(o_ref.dtype)
                    r_ref[rs, hd:hd + 1] = r
                acc = nxt

        @pl.when(j == 2)
        def _():
            o_ref[...] = lax.dot_general(a_ref[...].astype(MXU_DTYPE), b_ref[...].astype(MXU_DTYPE), (_DIMS["nt"], ((), ())),
                                         preferred_element_type=F32).astype(o_ref.dtype)

    return pl.pallas_call(
        body, name=name, grid=(S // tm, 3),
        in_specs=[pl.BlockSpec((tm, D), lambda i, j: (i, 0)), pl.BlockSpec((D, D), lambda i, j: (grp * 3 + j, 0))],
        out_specs=[pl.BlockSpec((tm, D), lambda i, j: (i, j)), pl.BlockSpec((tm, HEAD_DIM), lambda i, j: (i, j))],
        out_shape=[jax.ShapeDtypeStruct((S, 3 * D), MXU_DTYPE), jax.ShapeDtypeStruct((S, 3 * HEAD_DIM), F32)],
        compiler_params=_params("parallel", "parallel"),
    )(h, w_qkvT)


def _band(b, dil):
    qi = lax.broadcasted_iota(jnp.int32, (BLOCK, 2 * BLOCK), 0)
    ki = lax.broadcasted_iota(jnp.int32, (BLOCK, 2 * BLOCK), 1)
    delta = qi + BLOCK - ki
    valid = (delta >= 0) & (delta <= BLOCK) & ((ki >= BLOCK) | (b > 0))
    return valid, (delta * dil).astype(F32)


def _attn_fwd(qkv, qg, kg, *, grp, name):
    S, W = qkv.shape
    D = W // 3
    H = D // HEAD_DIM
    dil = DILATED_GROUPS[grp][1]
    L = S // dil
    nb = L // BLOCK
    slopes = _slopes(3 * H)[grp * H:(grp + 1) * H]
    scale = HEAD_DIM ** -0.5

    def body(q_ref, kp_ref, kc_ref, vp_ref, vc_ref, qg_ref, kg_ref, o_ref, l_ref):
        b = pl.program_id(1)
        valid, dist = _band(b, dil)
        l_ref[...] = jnp.zeros_like(l_ref)
        ss = []
        for h in range(H):
            hs = slice(h * HEAD_DIM, (h + 1) * HEAD_DIM)
            qn = (q_ref[:, hs].astype(F32) * qg_ref[h:h + 1, :]).astype(MXU_DTYPE)
            kp = (kp_ref[:, hs].astype(F32) * kg_ref[h:h + 1, :]).astype(MXU_DTYPE)
            kc = (kc_ref[:, hs].astype(F32) * kg_ref[h:h + 1, :]).astype(MXU_DTYPE)
            ss.append(lax.dot_general(qn, jnp.concatenate([kp, kc], axis=0), (((1,), (1,)), ((), ())), preferred_element_type=F32))
        ps = []
        for h in range(H):
            s = jnp.where(valid, ss[h] * scale - float(slopes[h]) * dist, NEG)
            m = jnp.max(s, axis=-1, keepdims=True)
            p = jnp.exp(s - m)
            den = jnp.sum(p, axis=-1, keepdims=True)
            l_ref[:, h:h + 1] = m + jnp.log(den)
            ps.append((p.astype(MXU_DTYPE), den))
        for h in range(H):
            hs = slice(h * HEAD_DIM, (h + 1) * HEAD_DIM)
            pb, den = ps[h]
            v2 = jnp.concatenate([vp_ref[:, hs], vc_ref[:, hs]], axis=0).astype(MXU_DTYPE)
            o_ref[:, hs] = (jnp.dot(pb, v2, preferred_element_type=F32) / den).astype(o_ref.dtype)

    def cur(j):
        return lambda r, b: (r * nb + b, j)

    def prv(j):
        return lambda r, b: (r * nb + jnp.maximum(b - 1, 0), j)

    blk = (BLOCK, D)
    gain = pl.BlockSpec((H, HEAD_DIM), lambda r, b: (0, 0))
    return pl.pallas_call(
        body, name=name, grid=(dil, nb),
        in_specs=[pl.BlockSpec(blk, cur(0)), pl.BlockSpec(blk, prv(1)), pl.BlockSpec(blk, cur(1)),
                  pl.BlockSpec(blk, prv(2)), pl.BlockSpec(blk, cur(2)), gain, gain],
        out_specs=[pl.BlockSpec(blk, cur(0)), pl.BlockSpec((BLOCK, HEAD_DIM), cur(0))],
        out_shape=[jax.ShapeDtypeStruct((S, D), MXU_DTYPE), jax.ShapeDtypeStruct((S, HEAD_DIM), F32)],
        compiler_params=_params("parallel", "parallel"),
    )(qkv, qkv, qkv, qkv, qkv, qg, kg)


def _attn_merge(os_, ls_, dils, *, name):
    S, D = os_[0].shape
    H = D // HEAD_DIM
    G = len(dils)
    ts = _pick(S, SUB_TILE, 16 * max(dils))
    subs = [g for g in range(G) if dils[g] > 1]

    def body(*refs):
        o_refs, l_refs = refs[0:G], refs[G:2 * G]
        outb_ref = refs[2 * G]
        lt_refs = refs[2 * G + 1:3 * G + 1]
        scratch = refs[3 * G + 1:]
        lt_tok = scratch[0]
        o_tok = {g: scratch[1 + 2 * j] for j, g in enumerate(subs)}
        l_tok = {g: scratch[2 + 2 * j] for j, g in enumerate(subs)}
        for g in subs:
            _sub_to_tok(o_refs[g], o_tok[g], dils[g])
            _sub_to_tok(l_refs[g], l_tok[g], dils[g])
        lt_tok[0] = jnp.zeros((ts, HEAD_DIM), F32)
        for h in range(H):
            hs = slice(h * HEAD_DIM, (h + 1) * HEAD_DIM)
            ls = [l_tok[g][0][:, h:h + 1] if g in subs else l_refs[g][:, h:h + 1] for g in range(G)]
            ov = [o_tok[g][h] if g in subs else o_refs[g][:, hs].astype(F32) for g in range(G)]
            m = functools.reduce(jnp.maximum, ls)
            es = [jnp.exp(l - m) for l in ls]
            den = functools.reduce(lambda a, b: a + b, es)
            out = functools.reduce(lambda a, b: a + b, [e * o for e, o in zip(es, ov)]) / den
            outb_ref[:, hs] = out.astype(outb_ref.dtype)
            lt_tok.at[0][:, h:h + 1] = m + jnp.log(den)
        for g in range(G):
            if g in subs:
                _tok_to_sub(lt_tok, lt_refs[g], dils[g])
            else:
                lt_refs[g][...] = lt_tok[0]

    def spec(g, cols):
        return _sub_spec(dils[g], ts, cols) if g in subs else pl.BlockSpec((ts, cols), lambda i: (i, 0))

    def shape(g, cols, dtype):
        return jax.ShapeDtypeStruct((dils[g], S // dils[g], cols) if g in subs else (S, cols), dtype)

    def view(a, g):
        return a.reshape(dils[g], S // dils[g], a.shape[-1]) if g in subs else a

    outs = pl.pallas_call(
        body, name=name, grid=(S // ts,),
        in_specs=[spec(g, D) for g in range(G)] + [spec(g, HEAD_DIM) for g in range(G)],
        out_specs=[pl.BlockSpec((ts, D), lambda i: (i, 0))] + [spec(g, HEAD_DIM) for g in range(G)],
        out_shape=[jax.ShapeDtypeStruct((S, D), MXU_DTYPE)] + [shape(g, HEAD_DIM, F32) for g in range(G)],
        scratch_shapes=[pltpu.VMEM((1, ts, HEAD_DIM), F32)] + [pltpu.VMEM((H, ts, HEAD_DIM), F32), pltpu.VMEM((1, ts, HEAD_DIM), F32)] * len(subs),
        compiler_params=_params("parallel"),
    )(*[view(o, g) for g, o in enumerate(os_)], *[view(l, g) for g, l in enumerate(ls_)])
    return outs[0], [t.reshape(S, HEAD_DIM) for t in outs[1:]]


def _attn_delta(do, out, dils, *, name):
    S, D = out.shape
    H = D // HEAD_DIM
    G = len(dils)
    ts = _pick(S, SUB_TILE, 16 * max(dils))
    subs = [g for g in range(G) if dils[g] > 1]

    def body(do_ref, o_ref, *rest):
        d_refs = rest[0:G]
        dos_refs = rest[G:G + len(subs)]
        d_tok, do_tok = rest[G + len(subs):]
        d_tok[0] = jnp.zeros((ts, HEAD_DIM), F32)
        for h in range(H):
            hs = slice(h * HEAD_DIM, (h + 1) * HEAD_DIM)
            dov = do_ref[:, hs].astype(F32)
            do_tok[h] = dov
            d_tok.at[0][:, h:h + 1] = jnp.sum(dov * o_ref[:, hs].astype(F32), axis=-1, keepdims=True)
        for g in range(G):
            if g in subs:
                _tok_to_sub(d_tok, d_refs[g], dils[g])
            else:
                d_refs[g][...] = d_tok[0]
        for g, dst in zip(subs, dos_refs):
            _tok_to_sub(do_tok, dst, dils[g])

    def spec(g, cols):
        return _sub_spec(dils[g], ts, cols) if g in subs else pl.BlockSpec((ts, cols), lambda i: (i, 0))

    def shape(g, cols, dtype):
        return jax.ShapeDtypeStruct((dils[g], S // dils[g], cols) if g in subs else (S, cols), dtype)

    row = pl.BlockSpec((ts, D), lambda i: (i, 0))
    outs = pl.pallas_call(
        body, name=name, grid=(S // ts,), in_specs=[row, row],
        out_specs=[spec(g, HEAD_DIM) for g in range(G)] + [spec(g, D) for g in subs],
        out_shape=[shape(g, HEAD_DIM, F32) for g in range(G)] + [shape(g, D, do.dtype) for g in subs],
        scratch_shapes=[pltpu.VMEM((1, ts, HEAD_DIM), F32), pltpu.VMEM((H, ts, HEAD_DIM), F32)],
        compiler_params=_params("parallel"),
    )(do, out)
    deltas = [t.reshape(S, HEAD_DIM) for t in outs[0:G]]
    dos = {g: t.reshape(S, D) for g, t in zip(subs, outs[G:])}
    return deltas, [dos[g] if g in subs else do for g in range(G)]


def _attn_bwd(qkv, rqk, do, lse, delta, qg, kg, *, grp, name):
    S, W = qkv.shape
    D = W // 3
    H = D // HEAD_DIM
    dil = DILATED_GROUPS[grp][1]
    L = S // dil
    nb = L // BLOCK
    slopes = _slopes(3 * H)[grp * H:(grp + 1) * H]
    scale = HEAD_DIM ** -0.5

    def body(q_ref, qp_ref, kp_ref, kc_ref, vp_ref, vc_ref, do_ref, l_ref, dl_ref, rq_ref, rk_ref, qg_ref, kg_ref,
             out_ref, dqg_ref, dkg_ref, cq_ref, ck_ref, cv_ref, nq_ref, nk_ref, nv_ref, pk_ref, pv_ref):
        r = pl.program_id(0)
        b = pl.program_id(1)

        @pl.when(jnp.logical_and(r == 0, b == 0))
        def _():
            dqg_ref[...] = jnp.zeros_like(dqg_ref)
            dkg_ref[...] = jnp.zeros_like(dkg_ref)

        @pl.when(b < nb)
        def _():
            valid, dist = _band(b, dil)

            def operands(h):
                hs = slice(h * HEAD_DIM, (h + 1) * HEAD_DIM)
                qn = (q_ref[:, hs].astype(F32) * qg_ref[h:h + 1, :]).astype(MXU_DTYPE)
                kp = (kp_ref[:, hs].astype(F32) * kg_ref[h:h + 1, :]).astype(MXU_DTYPE)
                kc = (kc_ref[:, hs].astype(F32) * kg_ref[h:h + 1, :]).astype(MXU_DTYPE)
                k2 = jnp.concatenate([kp, kc], axis=0)
                v2 = jnp.concatenate([vp_ref[:, hs], vc_ref[:, hs]], axis=0).astype(MXU_DTYPE)
                return hs, qn, k2, v2, do_ref[:, hs].astype(MXU_DTYPE)

            sdp = []
            for h in range(H):
                hs, qn, k2, v2, doh = operands(h)
                s = lax.dot_general(qn, k2, (((1,), (1,)), ((), ())), preferred_element_type=F32)
                dp = lax.dot_general(doh, v2, (((1,), (1,)), ((), ())), preferred_element_type=F32)
                sdp.append((s, dp))
            pds = []
            for h in range(H):
                s, dp = sdp[h]
                s = jnp.where(valid, s * scale - float(slopes[h]) * dist, NEG)
                p = jnp.exp(s - l_ref[:, h:h + 1])
                pds.append((p.astype(MXU_DTYPE), (p * (dp - dl_ref[:, h:h + 1]) * scale).astype(MXU_DTYPE)))
            for h in range(H):
                hs, qn, k2, v2, doh = operands(h)
                pb, dsc = pds[h]
                nq_ref[:, hs] = jnp.dot(dsc, k2, preferred_element_type=F32)
                dk2 = lax.dot_general(dsc, qn, (((0,), (0,)), ((), ())), preferred_element_type=F32)
                dv2 = lax.dot_general(pb, doh, (((0,), (0,)), ((), ())), preferred_element_type=F32)
                pk_ref[:, hs] = dk2[0:BLOCK]
                nk_ref[:, hs] = dk2[BLOCK:2 * BLOCK]
                pv_ref[:, hs] = dv2[0:BLOCK]
                nv_ref[:, hs] = dv2[BLOCK:2 * BLOCK]

        @pl.when(b == nb)
        def _():
            pk_ref[...] = jnp.zeros_like(pk_ref)
            pv_ref[...] = jnp.zeros_like(pv_ref)

        @pl.when(b > 0)
        def _():
            for h in range(H):
                hs = slice(h * HEAD_DIM, (h + 1) * HEAD_DIM)
                for j, (xh_ref, r_ref, gain_ref, dgain_ref) in enumerate(((qp_ref, rq_ref, qg_ref, dqg_ref),
                                                                          (kp_ref, rk_ref, kg_ref, dkg_ref))):
                    dy = cq_ref[:, hs] if j == 0 else ck_ref[:, hs] + pk_ref[:, hs]
                    gain = gain_ref[h:h + 1, :]
                    xh = xh_ref[:, hs].astype(F32)
                    rr = r_ref[:, h:h + 1]
                    gy = dy * gain
                    dx = rr * (gy - xh * jnp.mean(gy * xh, axis=-1, keepdims=True))
                    out_ref[:, j * D + h * HEAD_DIM:j * D + (h + 1) * HEAD_DIM] = dx.astype(out_ref.dtype)
                    dgain_ref[h:h + 1, :] += jnp.sum(dy * xh, axis=0, keepdims=True)
                out_ref[:, 2 * D + h * HEAD_DIM:2 * D + (h + 1) * HEAD_DIM] = (cv_ref[:, hs] + pv_ref[:, hs]).astype(out_ref.dtype)

        @pl.when(b < nb)
        def _():
            cq_ref[...] = nq_ref[...]
            ck_ref[...] = nk_ref[...]
            cv_ref[...] = nv_ref[...]

    def cur(j):
        return lambda r, b: (r * nb + jnp.minimum(b, nb - 1), j)

    def prv(j):
        return lambda r, b: (r * nb + jnp.clip(b - 1, 0, nb - 1), j)

    blk = (BLOCK, D)
    lblk = pl.BlockSpec((BLOCK, HEAD_DIM), cur(0))
    gain = pl.BlockSpec((H, HEAD_DIM), lambda r, b: (0, 0))
    return pl.pallas_call(
        body, name=name, grid=(dil, nb + 1),
        in_specs=[pl.BlockSpec(blk, cur(0)), pl.BlockSpec(blk, prv(0)), pl.BlockSpec(blk, prv(1)), pl.BlockSpec(blk, cur(1)),
                  pl.BlockSpec(blk, prv(2)), pl.BlockSpec(blk, cur(2)), pl.BlockSpec(blk, cur(0)), lblk, lblk,
                  pl.BlockSpec((BLOCK, HEAD_DIM), prv(0)), pl.BlockSpec((BLOCK, HEAD_DIM), prv(1)), gain, gain],
        out_specs=[pl.BlockSpec((BLOCK, 3 * D), lambda r, b: (r * nb + jnp.maximum(b - 1, 0), 0)), gain, gain],
        out_shape=[jax.ShapeDtypeStruct((S, 3 * D), MXU_DTYPE), jax.ShapeDtypeStruct((H, HEAD_DIM), F32),
                   jax.ShapeDtypeStruct((H, HEAD_DIM), F32)],
        scratch_shapes=[pltpu.VMEM(blk, F32)] * 8,
        compiler_params=_params("arbitrary", "arbitrary"),
    )(qkv, qkv, qkv, qkv, qkv, qkv, do, lse, delta, rqk, rqk, qg, kg)


def _loss_head(y, target, *, name):
    S, D = y.shape
    ts = _pick(S, 512, 16)

    def body(y_ref, t_ref, dy_ref, dyb_ref, l_ref, acc_ref):
        i = pl.program_id(0)
        e = y_ref[...] - t_ref[...]
        dy = e * (1.0 / D)
        dy_ref[...] = dy
        dyb_ref[...] = dy.astype(dyb_ref.dtype)
        part = jnp.sum(e * e, axis=0, keepdims=True)

        @pl.when(i == 0)
        def _():
            acc_ref[...] = part

        @pl.when(i > 0)
        def _():
            acc_ref[...] += part

        @pl.when(i == pl.num_programs(0) - 1)
        def _():
            l_ref[...] = jnp.broadcast_to(jnp.sum(acc_ref[...], axis=-1, keepdims=True) * (0.5 / D), l_ref.shape)

    row = pl.BlockSpec((ts, D), lambda i: (i, 0))
    return pl.pallas_call(
        body, name=name, grid=(S // ts,), in_specs=[row, row],
        out_specs=[row, row, pl.BlockSpec((8, 128), lambda i: (0, 0))],
        out_shape=[jax.ShapeDtypeStruct((S, D), F32), jax.ShapeDtypeStruct((S, D), MXU_DTYPE), jax.ShapeDtypeStruct((8, 128), F32)],
        scratch_shapes=[pltpu.VMEM((1, D), F32)],
        compiler_params=_params("arbitrary"),
    )(y, target)


def _adamw(w, m, v, terms, slots, *, name):
    R, C = w.shape
    nt = len(terms)
    tr = _pick(R, 256, 16)
    c1 = 1.0 - ADAM_B1 ** ADAM_STEP
    c2 = 1.0 - ADAM_B2 ** ADAM_STEP

    def body(slot_ref, w_ref, m_ref, v_ref, *rest):
        t_refs = rest[:nt]
        g_ref, d_ref, nm_ref, nv_ref = rest[nt:]
        g = t_refs[0][...].astype(F32)
        for t in t_refs[1:]:
            g = g + t[...].astype(F32)
        mm = ADAM_B1 * m_ref[...] + (1.0 - ADAM_B1) * g
        vv = ADAM_B2 * v_ref[...] + (1.0 - ADAM_B2) * (g * g)
        m_hat = mm / c1
        v_hat = vv / c2
        g_ref[...] = g
        d_ref[...] = -ADAM_LR * (m_hat / (jnp.sqrt(v_hat) + ADAM_EPS) + ADAM_WD * w_ref[...])
        nm_ref[...] = mm
        nv_ref[...] = vv

    row = pl.BlockSpec((tr, C), lambda i, s: (i, 0))
    grid_spec = pltpu.PrefetchScalarGridSpec(
        num_scalar_prefetch=1, grid=(R // tr,),
        in_specs=[row, row, row] + [pl.BlockSpec((None, tr, C), lambda i, s, t=t: (s[t], i, 0)) for t in range(nt)],
        out_specs=[row] * 4)
    return pl.pallas_call(
        body, name=name, grid_spec=grid_spec, out_shape=[jax.ShapeDtypeStruct((R, C), F32)] * 4,
        compiler_params=_params("parallel"),
    )(slots, w, m, v, *terms)


def _chip_partials(g, sib, core, *, name):
    _, R, C = g.shape
    tr = _pick(R, 1200, 16)

    def body(core_ref, g_ref, s_ref, o_ref):
        o_ref[...] = (g_ref[...] + s_ref[...].astype(F32)).astype(o_ref.dtype)

    grid_spec = pltpu.PrefetchScalarGridSpec(
        num_scalar_prefetch=1, grid=(4, R // tr),
        in_specs=[pl.BlockSpec((None, tr, C), lambda k, i, c: (2 * k + c[0], i, 0)),
                  pl.BlockSpec((None, tr, C), lambda k, i, c: (k, i, 0))],
        out_specs=pl.BlockSpec((None, tr, C), lambda k, i, c: (k, i, 0)))
    return pl.pallas_call(
        body, name=name, grid_spec=grid_spec, out_shape=jax.ShapeDtypeStruct((4, R, C), sib.dtype),
        compiler_params=_params("parallel", "parallel"),
    )(core, g, sib)


_ANY = pl.BlockSpec(memory_space=pl.ANY)


def _place():
    return lax.axis_index("x"), lax.axis_index("y"), lax.axis_index("c")


class _Exchange:
    def __init__(self, operands, out_shapes, scratch, emit):
        self.operands, self.out_shapes, self.scratch, self.emit = operands, out_shapes, scratch, emit


def _run_exchange(ex, *, name):
    n_in, n_out = len(ex.operands), len(ex.out_shapes)

    def body(*refs):
        ex.emit(0, 1, refs[:n_in], refs[n_in:n_in + n_out], refs[n_in + n_out:])

    return pl.pallas_call(body, name=name, in_specs=[_ANY] * n_in, out_specs=[_ANY] * n_out, out_shape=ex.out_shapes,
                          scratch_shapes=ex.scratch)(*ex.operands)


def _gather_exchange(shard):
    R, C = shard.shape

    def emit(step, n, ins, outs, sems):
        x_ref, out_ref = ins[0], outs[0]
        send_sems, recv_sems, local_sem = sems
        x, y, c = _place()
        me, sibling = (x, y, c), (x, y, 1 - c)
        chips = [(1 - x, y), (x, 1 - y), (1 - x, 1 - y)]

        def slot(px, py, pc):
            return out_ref.at[4 * px + 2 * py + pc]

        def copy(k, block, to, src=None):
            return pltpu.make_async_remote_copy(
                src_ref=slot(*block) if src is None else src, dst_ref=slot(*block),
                send_sem=send_sems.at[k], recv_sem=recv_sems.at[k], device_id=to, device_id_type=MESH)

        mine = pltpu.make_async_copy(x_ref, slot(*me), local_sem)
        first = [copy(0, me, sibling, src=x_ref)] + [copy(1 + j, me, (*chip, c), src=x_ref) for j, chip in enumerate(chips)]
        passed = [copy(4 + j, (*chip, c), sibling) for j, chip in enumerate(chips)]

        @pl.when(step == 0)
        def _():
            mine.start()
            for cp in first:
                cp.start()

        for j, chip in enumerate(chips):
            @pl.when(step == max(n - 2 * (len(chips) - j), 0))
            def _(j=j, chip=chip):
                copy(1 + j, (*chip, c), me).wait_recv()
                passed[j].start()

        @pl.when(step == n - 1)
        def _():
            copy(0, sibling, me).wait_recv()
            for j, chip in enumerate(chips):
                copy(4 + j, (*chip, 1 - c), me).wait_recv()
            for cp in first + passed:
                cp.wait_send()
            mine.wait()

    return _Exchange([shard], [jax.ShapeDtypeStruct((N_DEV, R, C), shard.dtype)],
                     [pltpu.SemaphoreType.DMA((7,)), pltpu.SemaphoreType.DMA((7,)), pltpu.SemaphoreType.DMA], emit)


def _all_gather(shard, *, name):
    return _run_exchange(_gather_exchange(shard), name=name)[0]


def _rs_sibling(g, *, name):
    _, R, C = g.shape

    def body(g_ref, sib_ref, send_sems, recv_sems):
        x, y, c = _place()
        sends = [pltpu.make_async_remote_copy(
            src_ref=g_ref.at[2 * k + (1 - c)], dst_ref=sib_ref.at[k], send_sem=send_sems.at[k], recv_sem=recv_sems.at[k],
            device_id=(x, y, 1 - c), device_id_type=MESH) for k in range(4)]
        for cp in sends:
            cp.start()
        for cp in sends:
            cp.wait_recv()
        for cp in sends:
            cp.wait_send()

    return pl.pallas_call(
        body, name=name, in_specs=[_ANY], out_specs=_ANY, out_shape=jax.ShapeDtypeStruct((4, R, C), g.dtype),
        scratch_shapes=[pltpu.SemaphoreType.DMA((4,)), pltpu.SemaphoreType.DMA((4,))],
    )(g)


def _chips_exchange(part):
    _, R, C = part.shape

    def emit(step, n, ins, outs, sems):
        p_ref, out_ref = ins[0], outs[0]
        send_sems, recv_sems = sems
        x, y, c = _place()
        chips = [(1 - x, y), (x, 1 - y), (1 - x, 1 - y)]
        sends = [pltpu.make_async_remote_copy(
            src_ref=p_ref.at[2 * px + py], dst_ref=out_ref.at[j], send_sem=send_sems.at[j], recv_sem=recv_sems.at[j],
            device_id=(px, py, c), device_id_type=MESH) for j, (px, py) in enumerate(chips)]

        @pl.when(step == 0)
        def _():
            for cp in sends:
                cp.start()

        @pl.when(step == n - 1)
        def _():
            for cp in sends:
                cp.wait_recv()
            for cp in sends:
                cp.wait_send()

    return _Exchange([part], [jax.ShapeDtypeStruct((3, R, C), part.dtype)],
                     [pltpu.SemaphoreType.DMA((3,)), pltpu.SemaphoreType.DMA((3,))], emit)


def _rs_chips(part, *, name):
    return _run_exchange(_chips_exchange(part), name=name)[0]


def _interleave_rows(wt):
    F2, D = wt.shape
    return wt.reshape(2, F2 // 256, 128, D).transpose(1, 0, 2, 3).reshape(F2, D)


def _deinterleave_rows(wt):
    F2, D = wt.shape
    return wt.reshape(F2 // 256, 2, 128, D).transpose(1, 0, 2, 3).reshape(F2, D)


def _interleave_cols(v):
    k, F2 = v.shape
    return v.reshape(k, 2, F2 // 256, 128).transpose(0, 2, 1, 3).reshape(k, F2)


def _deinterleave_cols(v):
    k, F2 = v.shape
    return v.reshape(k, F2 // 256, 2, 128).transpose(0, 2, 1, 3).reshape(k, F2)


def _pack_rows(parts):
    return jnp.concatenate(parts, axis=0)


def _flat_pack(parts, width):
    flat = jnp.concatenate([p.reshape(-1) for p in parts])
    pad = (-flat.shape[0]) % (8 * width)
    return jnp.pad(flat, (0, pad)).reshape(-1, width)


def _flat_unpack(packed, shapes):
    flat = packed.reshape(-1)
    out, off = [], 0
    for shp in shapes:
        n = int(np.prod(shp))
        out.append(flat[off:off + n].reshape(shp))
        off += n
    return out


def _ffn_forward(x, g_ffn, wupT, wdown, dw_i, dwb_i, tag):
    hf = _rms_fwd(x, g_ffn, name=f"ffn{tag}_rms")
    up = _mm(hf, wupT, mode="nt", out_dtype=MXU_DTYPE, name=f"ffn{tag}_up", tm=2048, tn=512)
    act = _ffn_act_fwd(up, dw_i, dwb_i, name=f"ffn{tag}_act")
    y = _mm(act, wdown, mode="nn", out_dtype=F32, name=f"ffn{tag}_down", residual=x)
    return y, (hf, up, act)


def _ffn_backward(x, g_ffn, wupT, wdown, dw_i, dwb_i, saved, dy, dyb, tag, exchange=None):
    hf, up, act = saved
    dact = _mm(dyb, wdown, mode="nt", out_dtype=MXU_DTYPE, name=f"ffn{tag}_dact", tm=1024, tn=1408)
    d_wdown = _mm(act, dyb, mode="tn", out_dtype=F32, name=f"ffn{tag}_dwdown", tm=1408, tk=2048)
    dup, d_dw_i, d_dwb_i, *carried = _ffn_act_bwd(up, dact, dw_i, dwb_i, name=f"ffn{tag}_actbwd", exchange=exchange)
    dhf = _mm(dup, wupT, mode="nn", out_dtype=F32, name=f"ffn{tag}_dhf", tm=512)
    d_wupT = _mm(dup, hf, mode="tn", out_dtype=F32, name=f"ffn{tag}_dwup", tm=1408, tk=2048)
    dx, dxb, dg, cs = _rms_bwd(x, g_ffn, [dhf], dy, name=f"ffn{tag}_rmsbwd")
    return dx, dxb, cs, dict(w_upT=d_wupT, w_down=d_wdown, dw=d_dw_i[0:FFN_KERNEL], dw_b=d_dwb_i, norm=dg), carried


def _local_step(x, target, p, late_weights=None, early_reduce=None):
    S, D = x.shape
    H = D // HEAD_DIM
    h0 = _rms_fwd(x, p["norm_mix"][0:1], name="l0_rms")
    u = _mm(h0, p["w_inT"], mode="nt", out_dtype=F32, name="l0_in", bias=p["cm_b_in"])
    c, s, *carried = _cm_fwd(u, p["cm_dw"], p["cm_dw_b"], p["cm_ln_g"], p["cm_ln_b"], name="l0_conv",
                             exchange=None if late_weights is None else late_weights[0])
    if late_weights is not None:
        p = {**p, **late_weights[1](carried)}
    x1 = _mm(s, p["w_out"], mode="nn", out_dtype=F32, name="l0_out", bias=p["cm_b_out"], residual=x)
    x2, sv0 = _ffn_forward(x1, p["norm_ffn"][0:1], p["w_upT"][0], p["w_down"][0], p["ff_dw"][0], p["ff_dw_b"][0:1], 0)
    dils = [dil for _, dil in DILATED_GROUPS]
    assert dils[0] == 1
    h1s = [t.reshape(S, D) for t in _rms_fwd(x2, p["norm_mix"][1:2], name="l1_rms", subs=tuple(dils[1:]))]
    qkvs, rqks, os_, ls_ = [], [], [], []
    for g in range(len(dils)):
        qkv_g, r_g = _qkv_proj(h1s[g], p["w_qkvT"], grp=g, name=f"l1_qkv{g}")
        qkvs.append(qkv_g)
        rqks.append(r_g)
        o, l = _attn_fwd(qkvs[g], p["at_q_norm"][g * H:(g + 1) * H], p["at_k_norm"][g * H:(g + 1) * H], grp=g, name=f"l1_attn{g}")
        os_.append(o)
        ls_.append(l)
    outb, lses = _attn_merge(os_, ls_, dils, name="l1_merge")
    x3 = _mm(outb, p["w_o"], mode="nn", out_dtype=F32, name="l1_o", residual=x2)
    x4, sv1 = _ffn_forward(x3, p["norm_ffn"][1:2], p["w_upT"][1], p["w_down"][1], p["ff_dw"][1], p["ff_dw_b"][1:2], 1)
    dx4, dx4b, loss = _loss_head(x4, target, name="loss")
    dx3, dx3b, _, gf1, _ = _ffn_backward(x3, p["norm_ffn"][1:2], p["w_upT"][1], p["w_down"][1], p["ff_dw"][1], p["ff_dw_b"][1:2],
                                      sv1, dx4, dx4b, 1)
    do = _mm(dx3b, p["w_o"], mode="nt", out_dtype=MXU_DTYPE, name="l1_do")
    d_wo = _mm(outb, dx3b, mode="tn", out_dtype=F32, name="l1_dwo", tk=2048)
    deltas, dos = _attn_delta(do, outb, dils, name="l1_delta")
    dh1s, d_wqkvT, dqg, dkg = [], [], [], []
    for g, dil in enumerate(dils):
        dqkv_g, a, b_ = _attn_bwd(qkvs[g], rqks[g], dos[g], lses[g], deltas[g], p["at_q_norm"][g * H:(g + 1) * H],
                                  p["at_k_norm"][g * H:(g + 1) * H], grp=g, name=f"l1_attnbwd{g}")
        dqg.append(a)
        dkg.append(b_)
        d_wqkvT.append(_mm(dqkv_g, h1s[g], mode="tn", out_dtype=F32, name=f"l1_dwqkv{g}", tk=2048))
        dh1s.append(_mm(dqkv_g, p["w_qkvT"], mode="nn", out_dtype=F32, name=f"l1_dh{g}", b_off=g * 3 * D, b_len=3 * D))
    dx2, dx2b, dgm1, _ = _rms_bwd(x2, p["norm_mix"][1:2], dh1s[0:1], dx3, name="l1_rmsbwd",
                                  dh_subs=[(dh1s[g].reshape(dils[g], S // dils[g], D), dils[g]) for g in range(1, len(dils))])
    ex, finish = (None, None) if early_reduce is None else early_reduce(
        dict(w_qkvT=jnp.concatenate(d_wqkvT, axis=0), w_o=d_wo, w_upT=gf1["w_upT"], w_down=gf1["w_down"]))
    dx1, dx1b, cs1, gf0, carried = _ffn_backward(x1, p["norm_ffn"][0:1], p["w_upT"][0], p["w_down"][0], p["ff_dw"][0],
                                                 p["ff_dw_b"][0:1], sv0, dx2, dx2b, 0, exchange=ex)
    reduced = None if finish is None else finish(carried)
    ds = _mm(dx1b, p["w_out"], mode="nt", out_dtype=F32, name="l0_ds")
    d_wout = _mm(s, dx1b, mode="tn", out_dtype=F32, name="l0_dwout", tk=2048)
    dc, d_lng, d_lnb = _cm_ln_bwd(c, ds, p["cm_ln_g"], p["cm_ln_b"], name="l0_lnbwd")
    du, d_cmdw, d_cmdwb, d_bin = _cm_conv_bwd(dc, u, p["cm_dw"], name="l0_convbwd")
    dh0 = _mm(du, p["w_inT"], mode="nn", out_dtype=F32, name="l0_dh")
    d_winT = _mm(du, h0, mode="tn", out_dtype=F32, name="l0_dwin", tk=2048)
    grad_x, _, dgm0, _ = _rms_bwd(x, p["norm_mix"][0:1], [dh0], dx1, name="l0_rmsbwd")
    grads = dict(
        norm_mix=jnp.concatenate([dgm0, dgm1], axis=0),
        norm_ffn=jnp.concatenate([gf0["norm"], gf1["norm"]], axis=0),
        w_inT=d_winT, cm_b_in=d_bin, cm_dw=d_cmdw[0:CONV_KERNEL], cm_dw_b=d_cmdwb, cm_ln_g=d_lng, cm_ln_b=d_lnb,
        w_out=d_wout, cm_b_out=cs1,
        w_qkvT=jnp.concatenate(d_wqkvT, axis=0), at_q_norm=jnp.concatenate(dqg, axis=0), at_k_norm=jnp.concatenate(dkg, axis=0),
        w_o=d_wo,
        w_upT=[gf0["w_upT"], gf1["w_upT"]], w_down=[gf0["w_down"], gf1["w_down"]],
        ff_dw=jnp.stack([gf0["dw"], gf1["dw"]]), ff_dw_b=jnp.concatenate([gf0["dw_b"], gf1["dw_b"]], axis=0),
    )
    return loss, grad_x, grads, reduced


_BIG = ("cm_w_in", "cm_w_out", "at_w_qkv", "at_w_out", "ff_w_up", "ff_w_down")
_TRANSPOSED = ("cm_w_in", "at_w_qkv", "ff_w_up")
_SMALL = ("norm_mix", "norm_ffn", "cm_b_in", "cm_dw_b", "cm_ln_g", "cm_ln_b", "cm_b_out", "at_q_norm", "at_k_norm",
          "ff_dw_b", "cm_dw", "ff_dw")
_SMALL_SHARDED = ("cm_dw", "ff_dw")
_ORDER = ("norm_mix", "norm_ffn", "cm_w_in", "cm_b_in", "cm_dw", "cm_dw_b", "cm_ln_g", "cm_ln_b", "cm_w_out", "cm_b_out",
          "at_w_qkv", "at_q_norm", "at_k_norm", "at_w_out", "ff_w_up", "ff_dw", "ff_dw_b", "ff_w_down")


_UNITS = (("cm_w_in", 0), ("cm_w_out", 0), ("ff_w_up", 0), ("ff_w_down", 0),
          ("at_w_qkv", 0), ("at_w_out", 0), ("ff_w_up", 1), ("ff_w_down", 1))
_N_FIRST = 2
_N_LAYER0 = 4


def _unit_rows(t, n, l):
    return t[n].shape[2] if n in _TRANSPOSED else t[n].shape[1]


def _big_rows(t, units=_UNITS):
    return _pack_rows([t[n][l].T if n in _TRANSPOSED else t[n][l] for n, l in units])


def _big_unrows(packed, like):
    mats, off = {}, 0
    for n, l in _UNITS:
        rows = _unit_rows(like, n, l)
        m = packed[off:off + rows]
        off += rows
        mats[(n, l)] = m.T if n in _TRANSPOSED else m
    return {n: jnp.stack([mats[(n, l)] for l in range(like[n].shape[0])]) for n in _BIG}


def kernel(x, norm_mix, norm_ffn, cm_w_in, cm_b_in, cm_dw, cm_dw_b, cm_ln_g, cm_ln_b, cm_w_out, cm_b_out, at_w_qkv, at_q_norm, at_k_norm, at_w_out, ff_w_up, ff_dw, ff_dw_b, ff_w_down, loss_target, m_norm_mix, m_norm_ffn, m_cm_w_in, m_cm_b_in, m_cm_dw, m_cm_dw_b, m_cm_ln_g, m_cm_ln_b, m_cm_w_out, m_cm_b_out, m_at_w_qkv, m_at_q_norm, m_at_k_norm, m_at_w_out, m_ff_w_up, m_ff_dw, m_ff_dw_b, m_ff_w_down, v_norm_mix, v_norm_ffn, v_cm_w_in, v_cm_b_in, v_cm_dw, v_cm_dw_b, v_cm_ln_g, v_cm_ln_b, v_cm_w_out, v_cm_b_out, v_at_w_qkv, v_at_q_norm, v_at_k_norm, v_at_w_out, v_ff_w_up, v_ff_dw, v_ff_dw_b, v_ff_w_down):
    w = dict(norm_mix=norm_mix, norm_ffn=norm_ffn, cm_w_in=cm_w_in, cm_b_in=cm_b_in, cm_dw=cm_dw, cm_dw_b=cm_dw_b, cm_ln_g=cm_ln_g,
             cm_ln_b=cm_ln_b, cm_w_out=cm_w_out, cm_b_out=cm_b_out, at_w_qkv=at_w_qkv, at_q_norm=at_q_norm, at_k_norm=at_k_norm,
             at_w_out=at_w_out, ff_w_up=ff_w_up, ff_dw=ff_dw, ff_dw_b=ff_dw_b, ff_w_down=ff_w_down)
    m = dict(norm_mix=m_norm_mix, norm_ffn=m_norm_ffn, cm_w_in=m_cm_w_in, cm_b_in=m_cm_b_in, cm_dw=m_cm_dw, cm_dw_b=m_cm_dw_b,
             cm_ln_g=m_cm_ln_g, cm_ln_b=m_cm_ln_b, cm_w_out=m_cm_w_out, cm_b_out=m_cm_b_out, at_w_qkv=m_at_w_qkv,
             at_q_norm=m_at_q_norm, at_k_norm=m_at_k_norm, at_w_out=m_at_w_out, ff_w_up=m_ff_w_up, ff_dw=m_ff_dw,
             ff_dw_b=m_ff_dw_b, ff_w_down=m_ff_w_down)
    v = dict(norm_mix=v_norm_mix, norm_ffn=v_norm_ffn, cm_w_in=v_cm_w_in, cm_b_in=v_cm_b_in, cm_dw=v_cm_dw, cm_dw_b=v_cm_dw_b,
             cm_ln_g=v_cm_ln_g, cm_ln_b=v_cm_ln_b, cm_w_out=v_cm_w_out, cm_b_out=v_cm_b_out, at_w_qkv=v_at_w_qkv,
             at_q_norm=v_at_q_norm, at_k_norm=v_at_k_norm, at_w_out=v_at_w_out, ff_w_up=v_ff_w_up, ff_dw=v_ff_dw,
             ff_dw_b=v_ff_dw_b, ff_w_down=v_ff_w_down)
    S, D = x.shape[1], x.shape[2]
    F2 = ff_dw_b.shape[1]
    H3 = at_q_norm.shape[1]
    me = 4 * lax.axis_index("x") + 2 * lax.axis_index("y") + lax.axis_index("c")

    ix, iy, ic = lax.axis_index("x"), lax.axis_index("y"), lax.axis_index("c")
    chip = 2 * ix + iy
    core = jnp.stack([ic]).astype(jnp.int32)
    w_rows = _big_rows(w)
    unit_rows = [_unit_rows(w, n, l) for n, l in _UNITS]
    n_first = sum(unit_rows[:_N_FIRST])
    n_layer0 = sum(unit_rows[:_N_LAYER0])
    w_wire = w_rows.astype(MXU_DTYPE)

    def unpack(gathered, units, rows):
        full, off = {}, 0
        for (n, l), r in zip(units, rows):
            full[(n, l)] = gathered[:, off:off + r, :].reshape(N_DEV * r, D)
            off += r
        out = {}
        if ("cm_w_in", 0) in full:
            out.update(w_inT=full[("cm_w_in", 0)], w_out=full[("cm_w_out", 0)])
        if ("at_w_qkv", 0) in full:
            out.update(w_qkvT=full[("at_w_qkv", 0)], w_o=full[("at_w_out", 0)],
                       w_upT=[_interleave_rows(full[("ff_w_up", l)]) for l in range(2)],
                       w_down=[full[("ff_w_down", l)] for l in range(2)])
        return out

    first = _all_gather(w_wire[:n_first], name="gather_first")
    late_weights = (_gather_exchange(w_wire[n_first:]),
                    lambda carried: unpack(carried[0], _UNITS[_N_FIRST:], unit_rows[_N_FIRST:]))
    small_sh = _flat_pack([cm_dw, ff_dw], D)
    small_g = _all_gather(small_sh, name="gather_small")
    cm_dw_full = jnp.concatenate([_flat_unpack(small_g[j], [cm_dw.shape, ff_dw.shape])[0][0] for j in range(N_DEV)], axis=-1)
    ff_dw_full = jnp.concatenate([_flat_unpack(small_g[j], [cm_dw.shape, ff_dw.shape])[1] for j in range(N_DEV)], axis=-1)

    p = dict(
        norm_mix=norm_mix, norm_ffn=norm_ffn, cm_b_in=cm_b_in, cm_dw=cm_dw_full, cm_dw_b=cm_dw_b, cm_ln_g=cm_ln_g, cm_ln_b=cm_ln_b,
        cm_b_out=cm_b_out, at_q_norm=at_q_norm[0], at_k_norm=at_k_norm[0],
        ff_dw=jnp.stack([_interleave_cols(ff_dw_full[l]) for l in range(ff_dw_full.shape[0])]),
        ff_dw_b=_interleave_cols(ff_dw_b),
        **unpack(first, _UNITS[:_N_FIRST], unit_rows[:_N_FIRST]),
    )

    def pack(pieces):
        return jnp.concatenate([t.reshape(N_DEV, t.shape[0] // N_DEV, D) for t in pieces], axis=1)

    def reduce_start(pieces, tag):
        g_rows = pack(pieces)
        sib = _rs_sibling(g_rows.astype(WIRE_DTYPE), name=f"reduce{tag}_sibling")
        return g_rows, sib, _chip_partials(g_rows, sib, core, name=f"reduce{tag}_add")

    def early_reduce(gd):
        g_rows, sib, part = reduce_start([gd["w_qkvT"], gd["w_o"], _deinterleave_rows(gd["w_upT"]), gd["w_down"]], 1)
        return _chips_exchange(part), lambda carried: (g_rows, sib, carried[0])

    loss8, grad_x, g, reduced1 = _local_step(x[0], loss_target[0], p, late_weights, early_reduce)
    loss = lax.psum(loss8[0, 0], ("x", "y", "c"))
    g_rows0, sib0, part0 = reduce_start([g["w_inT"], g["w_out"], _deinterleave_rows(g["w_upT"][0]), g["w_down"][0]], 0)
    reduced0 = (g_rows0, sib0, _rs_chips(part0, name="reduce0_chips"))
    slots = jnp.stack([me, chip, 0 * me, 0 * me + 1, 0 * me + 2]).astype(jnp.int32)
    m_rows, v_rows = _big_rows(m), _big_rows(v)
    updated = []
    for tag, (g_rows, sib, recv), rows in ((0, reduced0, slice(0, n_layer0)), (1, reduced1, slice(n_layer0, None))):
        updated.append(_adamw(w_rows[rows], m_rows[rows], v_rows[rows], [g_rows, sib, recv, recv, recv], slots, name=f"adamw_big{tag}"))
    big = [_big_unrows(jnp.concatenate([updated[0][k], updated[1][k]], axis=0), w) for k in range(4)]

    g_small = dict(g)
    g_small["cm_b_in"] = g["cm_b_in"]
    g_small["at_q_norm"] = g["at_q_norm"][None]
    g_small["at_k_norm"] = g["at_k_norm"][None]
    g_small["ff_dw_b"] = _deinterleave_cols(g["ff_dw_b"])
    g_small["cm_dw"] = g["cm_dw"][None]
    g_small["ff_dw"] = jnp.stack([_deinterleave_cols(g["ff_dw"][l]) for l in range(g["ff_dw"].shape[0])])
    small_shapes = [g_small[n].shape for n in _SMALL]
    gs_parts = _all_gather(_flat_pack([g_small[n] for n in _SMALL], D), name="gather_small_grads")

    def embed(t, n):
        if n not in _SMALL_SHARDED:
            return t
        full_shape = t.shape[:-1] + (t.shape[-1] * N_DEV,)
        return lax.dynamic_update_slice_in_dim(jnp.zeros(full_shape, F32), t, me * t.shape[-1], axis=t.ndim - 1)

    packs = [_flat_pack([embed(tree[n], n) for n in _SMALL], D) for tree in (w, m, v)]
    gs, ds_, ms, vs = _adamw(packs[0], packs[1], packs[2], [gs_parts] * N_DEV, jnp.arange(N_DEV, dtype=jnp.int32),
                             name="adamw_small")
    small = []
    for t in (gs, ds_, ms, vs):
        un = dict(zip(_SMALL, _flat_unpack(t, small_shapes)))
        for n in _SMALL_SHARDED:
            width = w[n].shape[-1]
            un[n] = lax.dynamic_slice_in_dim(un[n], me * width, width, axis=un[n].ndim - 1)
        small.append({n: un[n].reshape(w[n].shape) for n in _SMALL})

    outs = [loss, grad_x[None]]
    for k in range(4):
        for n in _ORDER:
            outs.append(big[k][n] if n in _BIG else small[k][n])
    return tuple(outs)
```

```python
import functools

import jax
import jax.numpy as jnp
import numpy as np
from jax import lax
from jax.experimental import pallas as pl
from jax.experimental.pallas import tpu as pltpu

F32 = jnp.float32
MXU_DTYPE = jnp.bfloat16
WIRE_DTYPE = jnp.bfloat16
EPS = 1e-6
NEG = -1e30
HEAD_DIM = 128
BLOCK = 128
DILATED_GROUPS = ((128, 1), (512, 4), (2048, 16))
ALIBI_MAX = 8.0
CONV_KERNEL = 31
CONV_HALO = 32
CONV_ROWS = 64
FFN_KERNEL = 3
FFN_HALO = 16
FFN_ROWS = 64
ADAM_LR, ADAM_B1, ADAM_B2, ADAM_EPS, ADAM_WD, ADAM_STEP = 0.001, 0.9, 0.999, 1e-08, 0.01, 10
V7X_VMEM_BYTES = 64 * 1024 * 1024
VMEM_LIMIT = V7X_VMEM_BYTES * 3 // 4
N_DEV = 8
MESH = pl.DeviceIdType.MESH


def _pick(n, target, align):
    if n <= target:
        return n
    best = None
    for t in range(align, target + 1, align):
        if n % t == 0:
            best = t
    assert best is not None, (n, target, align)
    return best


def _params(*sem):
    return pltpu.CompilerParams(dimension_semantics=sem, vmem_limit_bytes=VMEM_LIMIT)


def _sigmoid(x):
    return 1.0 / (1.0 + jnp.exp(-x))


_DIMS = {"nn": ((1,), (0,)), "nt": ((1,), (1,)), "tn": ((0,), (0,))}


def _mm(a, b, *, mode, out_dtype, name, tm=1024, tn=1024, tk=None, bias=None, residual=None, b_off=0, b_len=None):
    if mode == "tn":
        K, M = a.shape
    else:
        M, K = a.shape
    if mode == "nt":
        N = b.shape[0] if b_len is None else b_len
    else:
        N = b.shape[1]
    if b_len is not None:
        assert mode == "nt" or (mode == "nn" and K == b_len)
    tm = _pick(M, tm, 128 if mode == "tn" else 16)
    tn = _pick(N, tn, 128)
    tk = K if tk is None else _pick(K, tk, 128 if mode != "tn" else 16)
    nk = K // tk
    unit = tn if mode == "nt" else tk
    assert b_off % unit == 0
    kb0 = b_off // unit
    if mode == "tn":
        a_spec = pl.BlockSpec((tk, tm), lambda i, j, k: (k, i))
    else:
        a_spec = pl.BlockSpec((tm, tk), lambda i, j, k: (i, k))
    if mode == "nt":
        b_spec = pl.BlockSpec((tn, tk), lambda i, j, k: (j + kb0, k))
    else:
        b_spec = pl.BlockSpec((tk, tn), lambda i, j, k: (k + kb0, j))
    in_specs = [a_spec, b_spec]
    args = [a, b]
    if bias is not None:
        in_specs.append(pl.BlockSpec((1, tn), lambda i, j, k: (0, j)))
        args.append(bias)
    if residual is not None:
        in_specs.append(pl.BlockSpec((tm, tn), lambda i, j, k: (i, j)))
        args.append(residual)
    has_bias, has_res = bias is not None, residual is not None

    def body(*refs):
        a_ref, b_ref = refs[0], refs[1]
        pos = 2
        bias_ref = res_ref = None
        if has_bias:
            bias_ref = refs[pos]
            pos += 1
        if has_res:
            res_ref = refs[pos]
            pos += 1
        o_ref = refs[pos]
        acc_ref = refs[pos + 1] if nk > 1 else None

        def finish(acc):
            if has_bias:
                acc = acc + bias_ref[...]
            if has_res:
                acc = acc + res_ref[...]
            o_ref[...] = acc.astype(o_ref.dtype)

        part = lax.dot_general(a_ref[...].astype(MXU_DTYPE), b_ref[...].astype(MXU_DTYPE), (_DIMS[mode], ((), ())),
                               preferred_element_type=F32)
        if nk == 1:
            finish(part)
        else:
            k = pl.program_id(2)

            @pl.when(k == 0)
            def _():
                acc_ref[...] = part

            @pl.when(jnp.logical_and(k > 0, k < nk - 1))
            def _():
                acc_ref[...] += part

            @pl.when(k == nk - 1)
            def _():
                finish(acc_ref[...] + part)

    return pl.pallas_call(
        body, name=name, grid=(M // tm, N // tn, nk), in_specs=in_specs,
        out_specs=pl.BlockSpec((tm, tn), lambda i, j, k: (i, j)),
        out_shape=jax.ShapeDtypeStruct((M, N), out_dtype),
        scratch_shapes=[pltpu.VMEM((tm, tn), F32)] if nk > 1 else [],
        compiler_params=_params("parallel", "parallel", "arbitrary"),
    )(*args)


SUB_TILE = 512


def _sub_spec(dil, ts, cols):
    return pl.BlockSpec((dil, ts // dil, cols), lambda i: (0, i, 0))


def _tok_to_sub(tok_ref, dst_ref, dil):
    nc, ts, _ = tok_ref.shape
    for c in range(nc):
        for r in range(dil):
            dst_ref[r, :, c * 128:(c + 1) * 128] = tok_ref.at[c][pl.ds(r, ts // dil, stride=dil), :].astype(dst_ref.dtype)


def _sub_to_tok(src_ref, tok_ref, dil):
    nc, ts, _ = tok_ref.shape
    for c in range(nc):
        for r in range(dil):
            tok_ref.at[c][pl.ds(r, ts // dil, stride=dil), :] = src_ref[r, :, c * 128:(c + 1) * 128].astype(F32)


def _rms_fwd(x, g, *, name, subs=()):
    S, D = x.shape
    ts = _pick(S, SUB_TILE, 16 * max(subs, default=1))
    NC = D // 128

    def body(x_ref, g_ref, h_ref, *rest):
        xv = x_ref[...]
        r = lax.rsqrt(jnp.mean(xv * xv, axis=-1, keepdims=True) + EPS)
        h = xv * r * g_ref[...]
        h_ref[...] = h.astype(h_ref.dtype)
        if subs:
            tok_ref = rest[-1]
            for c in range(NC):
                tok_ref[c] = h[:, c * 128:(c + 1) * 128]
            for dil, dst_ref in zip(subs, rest):
                _tok_to_sub(tok_ref, dst_ref, dil)

    row = pl.BlockSpec((ts, D), lambda i: (i, 0))
    outs = pl.pallas_call(
        body, name=name, grid=(S // ts,),
        in_specs=[row, pl.BlockSpec((1, D), lambda i: (0, 0))],
        out_specs=[row] + [_sub_spec(dil, ts, D) for dil in subs],
        out_shape=[jax.ShapeDtypeStruct((S, D), MXU_DTYPE)] + [jax.ShapeDtypeStruct((dil, S // dil, D), MXU_DTYPE) for dil in subs],
        scratch_shapes=[pltpu.VMEM((NC, ts, 128), F32)] if subs else [],
        compiler_params=_params("parallel"),
    )(x, g)
    return outs if subs else outs[0]


def _rms_bwd(x, g, dhs, dres, *, name, dh_subs=()):
    S, D = x.shape
    ts = _pick(S, SUB_TILE, 16 * max([dil for _, dil in dh_subs], default=1))
    n_dh, n_sub = len(dhs), len(dh_subs)
    NC = D // 128

    def body(*refs):
        x_ref, g_ref = refs[0], refs[1]
        dh_refs = refs[2:2 + n_dh]
        sub_refs = refs[2 + n_dh:2 + n_dh + n_sub]
        dres_ref, dx_ref, dxb_ref, dg_ref, cs_ref = refs[2 + n_dh + n_sub:7 + n_dh + n_sub]
        i = pl.program_id(0)
        xv = x_ref[...]
        r = lax.rsqrt(jnp.mean(xv * xv, axis=-1, keepdims=True) + EPS)
        xh = xv * r
        dhv = dh_refs[0][...].astype(F32)
        for t in dh_refs[1:]:
            dhv = dhv + t[...].astype(F32)
        for (_, dil), sub_ref in zip(dh_subs, sub_refs):
            tok_ref = refs[-1]
            _sub_to_tok(sub_ref, tok_ref, dil)
            dhv = dhv + jnp.concatenate([tok_ref[c] for c in range(NC)], axis=1)
        gy = dhv * g_ref[...]
        dx = r * (gy - xh * jnp.mean(gy * xh, axis=-1, keepdims=True)) + dres_ref[...]
        dx_ref[...] = dx
        dxb_ref[...] = dx.astype(dxb_ref.dtype)
        dg = jnp.sum(dhv * xh, axis=0, keepdims=True)
        cs = jnp.sum(dx, axis=0, keepdims=True)

        @pl.when(i == 0)
        def _():
            dg_ref[...] = dg
            cs_ref[...] = cs

        @pl.when(i > 0)
        def _():
            dg_ref[...] += dg
            cs_ref[...] += cs

    row = pl.BlockSpec((ts, D), lambda i: (i, 0))
    vec = pl.BlockSpec((1, D), lambda i: (0, 0))
    return pl.pallas_call(
        body, name=name, grid=(S // ts,),
        in_specs=[row, vec] + [row] * n_dh + [_sub_spec(dil, ts, D) for _, dil in dh_subs] + [row],
        out_specs=[row, row, vec, vec],
        out_shape=[jax.ShapeDtypeStruct((S, D), F32), jax.ShapeDtypeStruct((S, D), MXU_DTYPE),
                   jax.ShapeDtypeStruct((1, D), F32), jax.ShapeDtypeStruct((1, D), F32)],
        scratch_shapes=[pltpu.VMEM((NC, ts, 128), F32)] if n_sub else [],
        compiler_params=_params("arbitrary"),
    )(x, g, *dhs, *[a for a, _ in dh_subs], dres)


def _conv_phases(ph_ref, ts):
    n = ts + CONV_HALO - 8
    for b in range(1, 8):
        ph_ref[b, 0:n, :] = ph_ref[0, pl.ds(b, n), :]


def _phase_taps(base, step=1):
    groups = {}
    for k in range(CONV_KERNEL):
        a, b = divmod(base + step * k, 8)
        groups.setdefault(b, []).append((a, k))
    out = []
    for b in sorted(groups):
        ak = sorted(groups[b])
        assert [a for a, _ in ak] == list(range(ak[0][0], ak[0][0] + len(ak)))
        out.append((b, ak[0][0], [k for _, k in ak]))
    return out


def _cm_fwd(u, dw, dw_b, ln_g, ln_b, *, name, exchange=None):
    S, D2 = u.shape
    D = D2 // 2
    ts = _pick(S, 256, CONV_HALO)
    hb = ts // CONV_HALO
    ex_in, ex_out, ex_scr = ([], [], []) if exchange is None else (exchange.operands, exchange.out_shapes, exchange.scratch)

    def body(u_ref, up_ref, dw_ref, dwb_ref, g_ref, b_ref, *rest):
        xi = rest[:len(ex_in)]
        c_ref, s_ref = rest[len(ex_in):len(ex_in) + 2]
        xo = rest[len(ex_in) + 2:len(ex_in) + 2 + len(ex_out)]
        ext_ref = rest[len(ex_in) + 2 + len(ex_out)]
        i = pl.program_id(0)
        if exchange is not None:
            exchange.emit(i, S // ts, xi, xo, rest[len(ex_in) + 3 + len(ex_out):])
        prev = up_ref[:, :D] * _sigmoid(up_ref[:, D:])
        ext_ref[0:CONV_HALO, :] = jnp.where(i > 0, prev, 0.0)
        ext_ref[CONV_HALO:CONV_HALO + ts, :] = u_ref[:, :D] * _sigmoid(u_ref[:, D:])
        for cc in range(D // 128):
            sl = slice(cc * 128, (cc + 1) * 128)
            acc = jnp.zeros((ts, 128), F32) + dwb_ref[:, sl]
            for k in range(CONV_KERNEL):
                acc = acc + dw_ref[k:k + 1, sl] * ext_ref[pl.ds(CONV_HALO - (CONV_KERNEL - 1) + k, ts), sl]
            c_ref[:, sl] = acc
        c = c_ref[...]
        mu = jnp.mean(c, axis=-1, keepdims=True)
        xc = c - mu
        rstd = lax.rsqrt(jnp.mean(xc * xc, axis=-1, keepdims=True) + EPS)
        y = xc * rstd * g_ref[...] + b_ref[...]
        s_ref[...] = (y * _sigmoid(y)).astype(s_ref.dtype)

    vec = pl.BlockSpec((1, D), lambda i: (0, 0))
    return pl.pallas_call(
        body, name=name, grid=(S // ts,),
        in_specs=[pl.BlockSpec((ts, D2), lambda i: (i, 0)),
                  pl.BlockSpec((CONV_HALO, D2), lambda i: (jnp.maximum(i * hb - 1, 0), 0)),
                  pl.BlockSpec((CONV_KERNEL, D), lambda i: (0, 0)), vec, vec, vec] + [_ANY] * len(ex_in),
        out_specs=[pl.BlockSpec((ts, D), lambda i: (i, 0)), pl.BlockSpec((ts, D), lambda i: (i, 0))] + [_ANY] * len(ex_out),
        out_shape=[jax.ShapeDtypeStruct((S, D), F32), jax.ShapeDtypeStruct((S, D), MXU_DTYPE)] + list(ex_out),
        scratch_shapes=[pltpu.VMEM((ts + CONV_HALO, D), F32)] + list(ex_scr),
        compiler_params=_params("parallel" if exchange is None else "arbitrary"),
    )(u, u, dw, dw_b, ln_g, ln_b, *ex_in)


def _cm_ln_bwd(c, ds, ln_g, ln_b, *, name):
    S, D = c.shape
    ts = _pick(S, 512, 16)

    def body(c_ref, ds_ref, g_ref, b_ref, dc_ref, dg_ref, db_ref):
        i = pl.program_id(0)
        cv = c_ref[...]
        mu = jnp.mean(cv, axis=-1, keepdims=True)
        xc = cv - mu
        rstd = lax.rsqrt(jnp.mean(xc * xc, axis=-1, keepdims=True) + EPS)
        xh = xc * rstd
        y = xh * g_ref[...] + b_ref[...]
        sg = _sigmoid(y)
        dy = ds_ref[...].astype(F32) * (sg * (1.0 + y * (1.0 - sg)))
        gy = dy * g_ref[...]
        dc_ref[...] = rstd * (gy - jnp.mean(gy, axis=-1, keepdims=True) - xh * jnp.mean(gy * xh, axis=-1, keepdims=True))
        dg = jnp.sum(dy * xh, axis=0, keepdims=True)
        db = jnp.sum(dy, axis=0, keepdims=True)

        @pl.when(i == 0)
        def _():
            dg_ref[...] = dg
            db_ref[...] = db

        @pl.when(i > 0)
        def _():
            dg_ref[...] += dg
            db_ref[...] += db

    row = pl.BlockSpec((ts, D), lambda i: (i, 0))
    vec = pl.BlockSpec((1, D), lambda i: (0, 0))
    return pl.pallas_call(
        body, name=name, grid=(S // ts,), in_specs=[row, row, vec, vec], out_specs=[row, vec, vec],
        out_shape=[jax.ShapeDtypeStruct((S, D), F32), jax.ShapeDtypeStruct((1, D), F32), jax.ShapeDtypeStruct((1, D), F32)],
        compiler_params=_params("arbitrary"),
    )(c, ds, ln_g, ln_b)


def _cm_conv_bwd(dc, u, dw, *, name, exchange=None):
    ex_in, ex_out, ex_scr = ([], [], []) if exchange is None else (exchange.operands, exchange.out_shapes, exchange.scratch)
    S, D2 = u.shape
    D = D2 // 2
    ts = _pick(S, 256, CONV_HALO)
    hb = ts // CONV_HALO
    n_t = S // ts
    last_h = S // CONV_HALO - 1

    rc = _pick(ts, CONV_ROWS, 8)

    def fold8(v):
        out = v[0:8]
        for j in range(1, v.shape[0] // 8):
            out = out + v[8 * j:8 * j + 8]
        return out

    def body(dc_ref, dcn_ref, u_ref, up_ref, dw_ref, *rest):
        xi = rest[:len(ex_in)]
        du_ref, ddw_ref, ddwb_ref, dbin_ref = rest[len(ex_in):len(ex_in) + 4]
        xo = rest[len(ex_in) + 4:len(ex_in) + 4 + len(ex_out)]
        dph_ref, gph_ref, dgl_ref = rest[len(ex_in) + 4 + len(ex_out):len(ex_in) + 7 + len(ex_out)]
        i = pl.program_id(0)
        if exchange is not None:
            exchange.emit(i, n_t, xi, xo, rest[len(ex_in) + 7 + len(ex_out):])
        dph_ref[0, 0:ts, :] = dc_ref[...]
        dph_ref[0, ts:ts + CONV_HALO, :] = jnp.where(i < n_t - 1, dcn_ref[...], 0.0)
        prev = up_ref[:, :D] * _sigmoid(up_ref[:, D:])
        gph_ref[0, 0:CONV_HALO, :] = jnp.where(i > 0, prev, 0.0)
        gph_ref[0, CONV_HALO:CONV_HALO + ts, :] = u_ref[:, :D] * _sigmoid(u_ref[:, D:])
        _conv_phases(dph_ref, ts)
        _conv_phases(gph_ref, ts)

        @pl.when(i == 0)
        def _():
            ddw_ref[...] = jnp.zeros_like(ddw_ref)
            ddwb_ref[...] = jnp.zeros_like(ddwb_ref)
            dbin_ref[...] = jnp.zeros_like(dbin_ref)

        for cc in range(D // 128):
            sl = slice(cc * 128, (cc + 1) * 128)
            sl2 = slice(D + cc * 128, D + (cc + 1) * 128)
            wk = [dw_ref[k:k + 1, sl] for k in range(CONV_KERNEL)]
            acc_a, acc_g = jnp.zeros((8, 128), F32), jnp.zeros((8, 128), F32)
            dgl = jnp.zeros((ts, 128), F32)
            for b, a0, taps in _phase_taps(CONV_KERNEL - 1, -1):
                for j, k in enumerate(taps):
                    dgl = dgl + wk[k] * dph_ref[b, 8 * (a0 + j):8 * (a0 + j) + ts, sl]
            dgl_ref[...] = dgl
            for r0 in range(0, ts, rc):
                dglu = dgl_ref[r0:r0 + rc, :]
                av = u_ref[r0:r0 + rc, sl]
                sg = _sigmoid(u_ref[r0:r0 + rc, sl2])
                da = dglu * sg
                dg = dglu * av * sg * (1.0 - sg)
                du_ref[r0:r0 + rc, sl] = da.astype(du_ref.dtype)
                du_ref[r0:r0 + rc, sl2] = dg.astype(du_ref.dtype)
                acc_a = acc_a + fold8(da)
                acc_g = acc_g + fold8(dg)
            dbin_ref[:, sl] += jnp.sum(acc_a, axis=0, keepdims=True)
            dbin_ref[:, sl2] += jnp.sum(acc_g, axis=0, keepdims=True)
            for gi, (b, a0, taps) in enumerate(_phase_taps(CONV_HALO - (CONV_KERNEL - 1))):
                accs = [jnp.zeros((8, 128), F32) for _ in taps]
                accb = jnp.zeros((8, 128), F32)
                for r0 in range(0, ts, rc):
                    dcc = dph_ref[0, r0:r0 + rc, sl]
                    win = gph_ref[b, 8 * a0 + r0:8 * (a0 + len(taps) - 1) + r0 + rc, sl]
                    for j in range(len(taps)):
                        accs[j] = accs[j] + fold8(dcc * win[8 * j:8 * j + rc])
                    if gi == 0:
                        accb = accb + fold8(dcc)
                for j, k in enumerate(taps):
                    ddw_ref[k:k + 1, sl] += jnp.sum(accs[j], axis=0, keepdims=True)
                if gi == 0:
                    ddwb_ref[:, sl] += jnp.sum(accb, axis=0, keepdims=True)

    return pl.pallas_call(
        body, name=name, grid=(n_t,),
        in_specs=[pl.BlockSpec((ts, D), lambda i: (i, 0)),
                  pl.BlockSpec((CONV_HALO, D), lambda i: (jnp.minimum((i + 1) * hb, last_h), 0)),
                  pl.BlockSpec((ts, D2), lambda i: (i, 0)),
                  pl.BlockSpec((CONV_HALO, D2), lambda i: (jnp.maximum(i * hb - 1, 0), 0)),
                  pl.BlockSpec((CONV_KERNEL, D), lambda i: (0, 0))] + [_ANY] * len(ex_in),
        out_specs=[pl.BlockSpec((ts, D2), lambda i: (i, 0)), pl.BlockSpec((CONV_HALO, D), lambda i: (0, 0)),
                   pl.BlockSpec((1, D), lambda i: (0, 0)), pl.BlockSpec((1, D2), lambda i: (0, 0))] + [_ANY] * len(ex_out),
        out_shape=[jax.ShapeDtypeStruct((S, D2), MXU_DTYPE), jax.ShapeDtypeStruct((CONV_HALO, D), F32),
                   jax.ShapeDtypeStruct((1, D), F32), jax.ShapeDtypeStruct((1, D2), F32)] + list(ex_out),
        scratch_shapes=[pltpu.VMEM((8, ts + CONV_HALO, D), F32), pltpu.VMEM((8, ts + CONV_HALO, D), F32),
                        pltpu.VMEM((ts, 128), F32)] + list(ex_scr),
        compiler_params=_params("arbitrary"),
    )(dc, dc, u, u, dw, *ex_in)


def _ffn_cols(F2):
    return _pick(F2, 1024, 256)


def _ffn_act_fwd(up, dw, dw_b, *, name):
    S, F2 = up.shape
    ts = _pick(S, 512, 16)
    tc = _ffn_cols(F2)
    hb = ts // FFN_HALO

    rc = _pick(ts, FFN_ROWS, 16)

    def body(u_ref, up_ref, w_ref, b_ref, a_ref, ext_ref):
        i = pl.program_id(1)
        ext_ref[0:FFN_HALO, :] = jnp.where(i > 0, up_ref[...].astype(F32), 0.0)
        ext_ref[FFN_HALO:FFN_HALO + ts, :] = u_ref[...].astype(F32)
        for q in range(tc // 256):
            sls = [slice(q * 256 + half * 128, q * 256 + half * 128 + 128) for half in range(2)]
            wk = [[w_ref[k:k + 1, sl] for k in range(FFN_KERNEL)] for sl in sls]
            bb = [b_ref[:, sl] for sl in sls]
            for r0 in range(0, ts, rc):
                gt, vl = [bb[h] + sum(wk[h][k] * ext_ref[pl.ds(FFN_HALO + r0 - 2 + k, rc), sls[h]] for k in range(FFN_KERNEL))
                          for h in range(2)]
                a_ref[r0:r0 + rc, q * 128:(q + 1) * 128] = (gt * _sigmoid(gt) * vl).astype(a_ref.dtype)

    return pl.pallas_call(
        body, name=name, grid=(F2 // tc, S // ts),
        in_specs=[pl.BlockSpec((ts, tc), lambda j, i: (i, j)),
                  pl.BlockSpec((FFN_HALO, tc), lambda j, i: (jnp.maximum(i * hb - 1, 0), j)),
                  pl.BlockSpec((FFN_KERNEL, tc), lambda j, i: (0, j)),
                  pl.BlockSpec((1, tc), lambda j, i: (0, j))],
        out_specs=pl.BlockSpec((ts, tc // 2), lambda j, i: (i, j)),
        out_shape=jax.ShapeDtypeStruct((S, F2 // 2), MXU_DTYPE),
        scratch_shapes=[pltpu.VMEM((ts + FFN_HALO, tc), F32)],
        compiler_params=_params("parallel", "parallel"),
    )(up, up, dw, dw_b)


def _ffn_act_bwd(up, dact, dw, dw_b, *, name, exchange=None):
    ex_in, ex_out, ex_scr = ([], [], []) if exchange is None else (exchange.operands, exchange.out_shapes, exchange.scratch)
    S, F2 = up.shape
    ts = _pick(S, 512, 16)
    tc = _ffn_cols(F2)
    hb = ts // FFN_HALO
    n_t = S // ts
    last_h = S // FFN_HALO - 1
    E = ts + FFN_HALO

    rc = _pick(ts, FFN_ROWS, 16)

    def fold8(v):
        out = v[0:8]
        for j in range(1, v.shape[0] // 8):
            out = out + v[8 * j:8 * j + 8]
        return out

    def body(u_ref, up_ref, un_ref, da_ref, dan_ref, w_ref, b_ref, *rest):
        xi = rest[:len(ex_in)]
        dup_ref, ddw_ref, ddb_ref = rest[len(ex_in):len(ex_in) + 3]
        xo = rest[len(ex_in) + 3:len(ex_in) + 3 + len(ex_out)]
        ue_ref, dcv_ref = rest[len(ex_in) + 3 + len(ex_out):len(ex_in) + 5 + len(ex_out)]
        i = pl.program_id(1)
        if exchange is not None:
            exchange.emit(pl.program_id(0) * n_t + i, (F2 // tc) * n_t, xi, xo, rest[len(ex_in) + 5 + len(ex_out):])
        ue_ref[0:FFN_HALO, :] = jnp.where(i > 0, up_ref[...].astype(F32), 0.0)
        ue_ref[FFN_HALO:FFN_HALO + ts, :] = u_ref[...].astype(F32)
        ue_ref[FFN_HALO + ts:FFN_HALO + ts + FFN_HALO, :] = jnp.where(i < n_t - 1, un_ref[...].astype(F32), 0.0)

        @pl.when(i == 0)
        def _():
            ddw_ref[...] = jnp.zeros_like(ddw_ref)
            ddb_ref[...] = jnp.zeros_like(ddb_ref)

        for q in range(tc // 256):
            sls = [slice(q * 256 + half * 128, q * 256 + half * 128 + 128) for half in range(2)]
            qs = slice(q * 128, (q + 1) * 128)
            wk = [[w_ref[k:k + 1, sl] for k in range(FFN_KERNEL)] for sl in sls]
            bb = [b_ref[:, sl] for sl in sls]
            acc = [[jnp.zeros((8, 128), F32) for _ in range(FFN_KERNEL)] for _ in range(2)]
            accb = [jnp.zeros((8, 128), F32) for _ in range(2)]
            for r0, rows in [(r, rc) for r in range(0, ts, rc)] + [(ts, FFN_HALO)]:
                xs = [[ue_ref[pl.ds(FFN_HALO + r0 - 2 + k, rows), sls[h]] for k in range(FFN_KERNEL)] for h in range(2)]
                gt, vl = [bb[h] + sum(wk[h][k] * xs[h][k] for k in range(FFN_KERNEL)) for h in range(2)]
                sg = _sigmoid(gt)
                if r0 < ts:
                    dae = da_ref[r0:r0 + rows, qs].astype(F32)
                else:
                    dae = jnp.where(i < n_t - 1, dan_ref[:, qs].astype(F32), 0.0)
                dcv = [dae * vl * (sg * (1.0 + gt * (1.0 - sg))), dae * (gt * sg)]
                for h in range(2):
                    dcv_ref[r0:r0 + rows, sls[h]] = dcv[h]
                    if r0 < ts:
                        for k in range(FFN_KERNEL):
                            acc[h][k] = acc[h][k] + fold8(dcv[h] * xs[h][k])
                        accb[h] = accb[h] + fold8(dcv[h])
            for r0 in range(0, ts, rc):
                for h in range(2):
                    dup = sum(wk[h][2 - j] * dcv_ref[pl.ds(r0 + j, rc), sls[h]] for j in range(FFN_KERNEL))
                    dup_ref[r0:r0 + rc, sls[h]] = dup.astype(dup_ref.dtype)
            for h in range(2):
                for k in range(FFN_KERNEL):
                    ddw_ref[k:k + 1, sls[h]] += jnp.sum(acc[h][k], axis=0, keepdims=True)
                ddb_ref[:, sls[h]] += jnp.sum(accb[h], axis=0, keepdims=True)

    return pl.pallas_call(
        body, name=name, grid=(F2 // tc, n_t),
        in_specs=[pl.BlockSpec((ts, tc), lambda j, i: (i, j)),
                  pl.BlockSpec((FFN_HALO, tc), lambda j, i: (jnp.maximum(i * hb - 1, 0), j)),
                  pl.BlockSpec((FFN_HALO, tc), lambda j, i: (jnp.minimum((i + 1) * hb, last_h), j)),
                  pl.BlockSpec((ts, tc // 2), lambda j, i: (i, j)),
                  pl.BlockSpec((FFN_HALO, tc // 2), lambda j, i: (jnp.minimum((i + 1) * hb, last_h), j)),
                  pl.BlockSpec((FFN_KERNEL, tc), lambda j, i: (0, j)),
                  pl.BlockSpec((1, tc), lambda j, i: (0, j))] + [_ANY] * len(ex_in),
        out_specs=[pl.BlockSpec((ts, tc), lambda j, i: (i, j)), pl.BlockSpec((FFN_HALO, tc), lambda j, i: (0, j)),
                   pl.BlockSpec((1, tc), lambda j, i: (0, j))] + [_ANY] * len(ex_out),
        out_shape=[jax.ShapeDtypeStruct((S, F2), MXU_DTYPE), jax.ShapeDtypeStruct((FFN_HALO, F2), F32),
                   jax.ShapeDtypeStruct((1, F2), F32)] + list(ex_out),
        scratch_shapes=[pltpu.VMEM((ts + 2 * FFN_HALO, tc), F32), pltpu.VMEM((E, tc), F32)] + list(ex_scr),
        compiler_params=_params("parallel" if exchange is None else "arbitrary", "arbitrary"),
    )(up, up, up, dact, dact, dw, dw_b, *ex_in)


def _slopes(n_heads_total):
    return np.asarray(2.0 ** (-ALIBI_MAX * (np.arange(n_heads_total, dtype=np.float32) + 1.0) / n_heads_total), np.float32)


def _qkv_proj(h, w_qkvT, *, grp, name):
    S, D = h.shape
    H = D // HEAD_DIM
    tm = _pick(S, 1024, 16)

    rows = _pick(tm, 256, 16)

    def body(a_ref, b_ref, o_ref, r_ref):
        j = pl.program_id(1)
        r_ref[...] = jnp.zeros_like(r_ref)

        def product(c):
            return lax.dot_general(a_ref[c * rows:(c + 1) * rows, :].astype(MXU_DTYPE), b_ref[...].astype(MXU_DTYPE),
                                   (_DIMS["nt"], ((), ())), preferred_element_type=F32)

        @pl.when(j < 2)
        def _():
            acc = product(0)
            for c in range(tm // rows):
                nxt = product(c + 1) if c + 1 < tm // rows else None
                rs = slice(c * rows, (c + 1) * rows)
                for hd in range(H):
                    hs = slice(hd * HEAD_DIM, (hd + 1) * HEAD_DIM)
                    xv = acc[:, hs]
                    r = lax.rsqrt(jnp.mean(xv * xv, axis=-1, keepdims=True) + EPS)
                    o_ref[rs, hs] = (xv * r).astype(o_ref.dtype)
                    r_ref[rs, hd:hd + 1] = r
                acc = nxt

        @pl.when(j == 2)
        def _():
            o_ref[...] = lax.dot_general(a_ref[...].astype(MXU_DTYPE), b_ref[...].astype(MXU_DTYPE), (_DIMS["nt"], ((), ())),
                                         preferred_element_type=F32).astype(o_ref.dtype)

    return pl.pallas_call(
        body, name=name, grid=(S // tm, 3),
        in_specs=[pl.BlockSpec((tm, D), lambda i, j: (i, 0)), pl.BlockSpec((D, D), lambda i, j: (grp * 3 + j, 0))],
        out_specs=[pl.BlockSpec((tm, D), lambda i, j: (i, j)), pl.BlockSpec((tm, HEAD_DIM), lambda i, j: (i, j))],
        out_shape=[jax.ShapeDtypeStruct((S, 3 * D), MXU_DTYPE), jax.ShapeDtypeStruct((S, 3 * HEAD_DIM), F32)],
        compiler_params=_params("parallel", "parallel"),
    )(h, w_qkvT)


def _band(b, dil):
    qi = lax.broadcasted_iota(jnp.int32, (BLOCK, 2 * BLOCK), 0)
    ki = lax.broadcasted_iota(jnp.int32, (BLOCK, 2 * BLOCK), 1)
    delta = qi + BLOCK - ki
    valid = (delta >= 0) & (delta <= BLOCK) & ((ki >= BLOCK) | (b > 0))
    return valid, (delta * dil).astype(F32)


def _attn_fwd(qkv, qg, kg, *, grp, name):
    S, W = qkv.shape
    D = W // 3
    H = D // HEAD_DIM
    dil = DILATED_GROUPS[grp][1]
    L = S // dil
    nb = L // BLOCK
    slopes = _slopes(3 * H)[grp * H:(grp + 1) * H]
    scale = HEAD_DIM ** -0.5

    def body(q_ref, kp_ref, kc_ref, vp_ref, vc_ref, qg_ref, kg_ref, o_ref, l_ref):
        b = pl.program_id(1)
        valid, dist = _band(b, dil)
        l_ref[...] = jnp.zeros_like(l_ref)
        ss = []
        for h in range(H):
            hs = slice(h * HEAD_DIM, (h + 1) * HEAD_DIM)
            qn = (q_ref[:, hs].astype(F32) * qg_ref[h:h + 1, :]).astype(MXU_DTYPE)
            kp = (kp_ref[:, hs].astype(F32) * kg_ref[h:h + 1, :]).astype(MXU_DTYPE)
            kc = (kc_ref[:, hs].astype(F32) * kg_ref[h:h + 1, :]).astype(MXU_DTYPE)
            ss.append(lax.dot_general(qn, jnp.concatenate([kp, kc], axis=0), (((1,), (1,)), ((), ())), preferred_element_type=F32))
        ps = []
        for h in range(H):
            s = jnp.where(valid, ss[h] * scale - float(slopes[h]) * dist, NEG)
            m = jnp.max(s, axis=-1, keepdims=True)
            p = jnp.exp(s - m)
            den = jnp.sum(p, axis=-1, keepdims=True)
            l_ref[:, h:h + 1] = m + jnp.log(den)
            ps.append((p.astype(MXU_DTYPE), den))
        for h in range(H):
            hs = slice(h * HEAD_DIM, (h + 1) * HEAD_DIM)
            pb, den = ps[h]
            v2 = jnp.concatenate([vp_ref[:, hs], vc_ref[:, hs]], axis=0).astype(MXU_DTYPE)
            o_ref[:, hs] = (jnp.dot(pb, v2, preferred_element_type=F32) / den).astype(o_ref.dtype)

    def cur(j):
        return lambda r, b: (r * nb + b, j)

    def prv(j):
        return lambda r, b: (r * nb + jnp.maximum(b - 1, 0), j)

    blk = (BLOCK, D)
    gain = pl.BlockSpec((H, HEAD_DIM), lambda r, b: (0, 0))
    return pl.pallas_call(
        body, name=name, grid=(dil, nb),
        in_specs=[pl.BlockSpec(blk, cur(0)), pl.BlockSpec(blk, prv(1)), pl.BlockSpec(blk, cur(1)),
                  pl.BlockSpec(blk, prv(2)), pl.BlockSpec(blk, cur(2)), gain, gain],
        out_specs=[pl.BlockSpec(blk, cur(0)), pl.BlockSpec((BLOCK, HEAD_DIM), cur(0))],
        out_shape=[jax.ShapeDtypeStruct((S, D), MXU_DTYPE), jax.ShapeDtypeStruct((S, HEAD_DIM), F32)],
        compiler_params=_params("parallel", "parallel"),
    )(qkv, qkv, qkv, qkv, qkv, qg, kg)


def _attn_merge(os_, ls_, dils, *, name):
    S, D = os_[0].shape
    H = D // HEAD_DIM
    G = len(dils)
    ts = _pick(S, SUB_TILE, 16 * max(dils))
    subs = [g for g in range(G) if dils[g] > 1]

    def body(*refs):
        o_refs, l_refs = refs[0:G], refs[G:2 * G]
        outb_ref = refs[2 * G]
        lt_refs = refs[2 * G + 1:3 * G + 1]
        scratch = refs[3 * G + 1:]
        lt_tok = scratch[0]
        o_tok = {g: scratch[1 + 2 * j] for j, g in enumerate(subs)}
        l_tok = {g: scratch[2 + 2 * j] for j, g in enumerate(subs)}
        for g in subs:
            _sub_to_tok(o_refs[g], o_tok[g], dils[g])
            _sub_to_tok(l_refs[g], l_tok[g], dils[g])
        lt_tok[0] = jnp.zeros((ts, HEAD_DIM), F32)
        for h in range(H):
            hs = slice(h * HEAD_DIM, (h + 1) * HEAD_DIM)
            ls = [l_tok[g][0][:, h:h + 1] if g in subs else l_refs[g][:, h:h + 1] for g in range(G)]
            ov = [o_tok[g][h] if g in subs else o_refs[g][:, hs].astype(F32) for g in range(G)]
            m = functools.reduce(jnp.maximum, ls)
            es = [jnp.exp(l - m) for l in ls]
            den = functools.reduce(lambda a, b: a + b, es)
            out = functools.reduce(lambda a, b: a + b, [e * o for e, o in zip(es, ov)]) / den
            outb_ref[:, hs] = out.astype(outb_ref.dtype)
            lt_tok.at[0][:, h:h + 1] = m + jnp.log(den)
        for g in range(G):
            if g in subs:
                _tok_to_sub(lt_tok, lt_refs[g], dils[g])
            else:
                lt_refs[g][...] = lt_tok[0]

    def spec(g, cols):
        return _sub_spec(dils[g], ts, cols) if g in subs else pl.BlockSpec((ts, cols), lambda i: (i, 0))

    def shape(g, cols, dtype):
        return jax.ShapeDtypeStruct((dils[g], S // dils[g], cols) if g in subs else (S, cols), dtype)

    def view(a, g):
        return a.reshape(dils[g], S // dils[g], a.shape[-1]) if g in subs else a

    outs = pl.pallas_call(
        body, name=name, grid=(S // ts,),
        in_specs=[spec(g, D) for g in range(G)] + [spec(g, HEAD_DIM) for g in range(G)],
        out_specs=[pl.BlockSpec((ts, D), lambda i: (i, 0))] + [spec(g, HEAD_DIM) for g in range(G)],
        out_shape=[jax.ShapeDtypeStruct((S, D), MXU_DTYPE)] + [shape(g, HEAD_DIM, F32) for g in range(G)],
        scratch_shapes=[pltpu.VMEM((1, ts, HEAD_DIM), F32)] + [pltpu.VMEM((H, ts, HEAD_DIM), F32), pltpu.VMEM((1, ts, HEAD_DIM), F32)] * len(subs),
        compiler_params=_params("parallel"),
    )(*[view(o, g) for g, o in enumerate(os_)], *[view(l, g) for g, l in enumerate(ls_)])
    return outs[0], [t.reshape(S, HEAD_DIM) for t in outs[1:]]


def _attn_delta(do, out, dils, *, name):
    S, D = out.shape
    H = D // HEAD_DIM
    G = len(dils)
    ts = _pick(S, SUB_TILE, 16 * max(dils))
    subs = [g for g in range(G) if dils[g] > 1]

    def body(do_ref, o_ref, *rest):
        d_refs = rest[0:G]
        dos_refs = rest[G:G + len(subs)]
        d_tok, do_tok = rest[G + len(subs):]
        d_tok[0] = jnp.zeros((ts, HEAD_DIM), F32)
        for h in range(H):
            hs = slice(h * HEAD_DIM, (h + 1) * HEAD_DIM)
            dov = do_ref[:, hs].astype(F32)
            do_tok[h] = dov
            d_tok.at[0][:, h:h + 1] = jnp.sum(dov * o_ref[:, hs].astype(F32), axis=-1, keepdims=True)
        for g in range(G):
            if g in subs:
                _tok_to_sub(d_tok, d_refs[g], dils[g])
            else:
                d_refs[g][...] = d_tok[0]
        for g, dst in zip(subs, dos_refs):
            _tok_to_sub(do_tok, dst, dils[g])

    def spec(g, cols):
        return _sub_spec(dils[g], ts, cols) if g in subs else pl.BlockSpec((ts, cols), lambda i: (i, 0))

    def shape(g, cols, dtype):
        return jax.ShapeDtypeStruct((dils[g], S // dils[g], cols) if g in subs else (S, cols), dtype)

    row = pl.BlockSpec((ts, D), lambda i: (i, 0))
    outs = pl.pallas_call(
        body, name=name, grid=(S // ts,), in_specs=[row, row],
        out_specs=[spec(g, HEAD_DIM) for g in range(G)] + [spec(g, D) for g in subs],
        out_shape=[shape(g, HEAD_DIM, F32) for g in range(G)] + [shape(g, D, do.dtype) for g in subs],
        scratch_shapes=[pltpu.VMEM((1, ts, HEAD_DIM), F32), pltpu.VMEM((H, ts, HEAD_DIM), F32)],
        compiler_params=_params("parallel"),
    )(do, out)
    deltas = [t.reshape(S, HEAD_DIM) for t in outs[0:G]]
    dos = {g: t.reshape(S, D) for g, t in zip(subs, outs[G:])}
    return deltas, [dos[g] if g in subs else do for g in range(G)]


def _attn_bwd(qkv, rqk, do, lse, delta, qg, kg, *, grp, name):
    S, W = qkv.shape
    D = W // 3
    H = D // HEAD_DIM
    dil = DILATED_GROUPS[grp][1]
    L = S // dil
    nb = L // BLOCK
    slopes = _slopes(3 * H)[grp * H:(grp + 1) * H]
    scale = HEAD_DIM ** -0.5

    def body(q_ref, qp_ref, kp_ref, kc_ref, vp_ref, vc_ref, do_ref, l_ref, dl_ref, rq_ref, rk_ref, qg_ref, kg_ref,
             out_ref, dqg_ref, dkg_ref, cq_ref, ck_ref, cv_ref, nq_ref, nk_ref, nv_ref, pk_ref, pv_ref):
        r = pl.program_id(0)
        b = pl.program_id(1)

        @pl.when(jnp.logical_and(r == 0, b == 0))
        def _():
            dqg_ref[...] = jnp.zeros_like(dqg_ref)
            dkg_ref[...] = jnp.zeros_like(dkg_ref)

        @pl.when(b < nb)
        def _():
            valid, dist = _band(b, dil)

            def operands(h):
                hs = slice(h * HEAD_DIM, (h + 1) * HEAD_DIM)
                qn = (q_ref[:, hs].astype(F32) * qg_ref[h:h + 1, :]).astype(MXU_DTYPE)
                kp = (kp_ref[:, hs].astype(F32) * kg_ref[h:h + 1, :]).astype(MXU_DTYPE)
                kc = (kc_ref[:, hs].astype(F32) * kg_ref[h:h + 1, :]).astype(MXU_DTYPE)
                k2 = jnp.concatenate([kp, kc], axis=0)
                v2 = jnp.concatenate([vp_ref[:, hs], vc_ref[:, hs]], axis=0).astype(MXU_DTYPE)
                return hs, qn, k2, v2, do_ref[:, hs].astype(MXU_DTYPE)

            sdp = []
            for h in range(H):
                hs, qn, k2, v2, doh = operands(h)
                s = lax.dot_general(qn, k2, (((1,), (1,)), ((), ())), preferred_element_type=F32)
                dp = lax.dot_general(doh, v2, (((1,), (1,)), ((), ())), preferred_element_type=F32)
                sdp.append((s, dp))
            pds = []
            for h in range(H):
                s, dp = sdp[h]
                s = jnp.where(valid, s * scale - float(slopes[h]) * dist, NEG)
                p = jnp.exp(s - l_ref[:, h:h + 1])
                pds.append((p.astype(MXU_DTYPE), (p * (dp - dl_ref[:, h:h + 1]) * scale).astype(MXU_DTYPE)))
            for h in range(H):
                hs, qn, k2, v2, doh = operands(h)
                pb, dsc = pds[h]
                nq_ref[:, hs] = jnp.dot(dsc, k2, preferred_element_type=F32)
                dk2 = lax.dot_general(dsc, qn, (((0,), (0,)), ((), ())), preferred_element_type=F32)
                dv2 = lax.dot_general(pb, doh, (((0,), (0,)), ((), ())), preferred_element_type=F32)
                pk_ref[:, hs] = dk2[0:BLOCK]
                nk_ref[:, hs] = dk2[BLOCK:2 * BLOCK]
                pv_ref[:, hs] = dv2[0:BLOCK]
                nv_ref[:, hs] = dv2[BLOCK:2 * BLOCK]

        @pl.when(b == nb)
        def _():
            pk_ref[...] = jnp.zeros_like(pk_ref)
            pv_ref[...] = jnp.zeros_like(pv_ref)

        @pl.when(b > 0)
        def _():
            for h in range(H):
                hs = slice(h * HEAD_DIM, (h + 1) * HEAD_DIM)
                for j, (xh_ref, r_ref, gain_ref, dgain_ref) in enumerate(((qp_ref, rq_ref, qg_ref, dqg_ref),
                                                                          (kp_ref, rk_ref, kg_ref, dkg_ref))):
                    dy = cq_ref[:, hs] if j == 0 else ck_ref[:, hs] + pk_ref[:, hs]
                    gain = gain_ref[h:h + 1, :]
                    xh = xh_ref[:, hs].astype(F32)
                    rr = r_ref[:, h:h + 1]
                    gy = dy * gain
                    dx = rr * (gy - xh * jnp.mean(gy * xh, axis=-1, keepdims=True))
                    out_ref[:, j * D + h * HEAD_DIM:j * D + (h + 1) * HEAD_DIM] = dx.astype(out_ref.dtype)
                    dgain_ref[h:h + 1, :] += jnp.sum(dy * xh, axis=0, keepdims=True)
                out_ref[:, 2 * D + h * HEAD_DIM:2 * D + (h + 1) * HEAD_DIM] = (cv_ref[:, hs] + pv_ref[:, hs]).astype(out_ref.dtype)

        @pl.when(b < nb)
        def _():
            cq_ref[...] = nq_ref[...]
            ck_ref[...] = nk_ref[...]
            cv_ref[...] = nv_ref[...]

    def cur(j):
        return lambda r, b: (r * nb + jnp.minimum(b, nb - 1), j)

    def prv(j):
        return lambda r, b: (r * nb + jnp.clip(b - 1, 0, nb - 1), j)

    blk = (BLOCK, D)
    lblk = pl.BlockSpec((BLOCK, HEAD_DIM), cur(0))
    gain = pl.BlockSpec((H, HEAD_DIM), lambda r, b: (0, 0))
    return pl.pallas_call(
        body, name=name, grid=(dil, nb + 1),
        in_specs=[pl.BlockSpec(blk, cur(0)), pl.BlockSpec(blk, prv(0)), pl.BlockSpec(blk, prv(1)), pl.BlockSpec(blk, cur(1)),
                  pl.BlockSpec(blk, prv(2)), pl.BlockSpec(blk, cur(2)), pl.BlockSpec(blk, cur(0)), lblk, lblk,
                  pl.BlockSpec((BLOCK, HEAD_DIM), prv(0)), pl.BlockSpec((BLOCK, HEAD_DIM), prv(1)), gain, gain],
        out_specs=[pl.BlockSpec((BLOCK, 3 * D), lambda r, b: (r * nb + jnp.maximum(b - 1, 0), 0)), gain, gain],
        out_shape=[jax.ShapeDtypeStruct((S, 3 * D), MXU_DTYPE), jax.ShapeDtypeStruct((H, HEAD_DIM), F32),
                   jax.ShapeDtypeStruct((H, HEAD_DIM), F32)],
        scratch_shapes=[pltpu.VMEM(blk, F32)] * 8,
        compiler_params=_params("arbitrary", "arbitrary"),
    )(qkv, qkv, qkv, qkv, qkv, qkv, do, lse, delta, rqk, rqk, qg, kg)


def _loss_head(y, target, *, name):
    S, D = y.shape
    ts = _pick(S, 512, 16)

    def body(y_ref, t_ref, dy_ref, dyb_ref, l_ref, acc_ref):
        i = pl.program_id(0)
        e = y_ref[...] - t_ref[...]
        dy = e * (1.0 / D)
        dy_ref[...] = dy
        dyb_ref[...] = dy.astype(dyb_ref.dtype)
        part = jnp.sum(e * e, axis=0, keepdims=True)

        @pl.when(i == 0)
        def _():
            acc_ref[...] = part

        @pl.when(i > 0)
        def _():
            acc_ref[...] += part

        @pl.when(i == pl.num_programs(0) - 1)
        def _():
            l_ref[...] = jnp.broadcast_to(jnp.sum(acc_ref[...], axis=-1, keepdims=True) * (0.5 / D), l_ref.shape)

    row = pl.BlockSpec((ts, D), lambda i: (i, 0))
    return pl.pallas_call(
        body, name=name, grid=(S // ts,), in_specs=[row, row],
        out_specs=[row, row, pl.BlockSpec((8, 128), lambda i: (0, 0))],
        out_shape=[jax.ShapeDtypeStruct((S, D), F32), jax.ShapeDtypeStruct((S, D), MXU_DTYPE), jax.ShapeDtypeStruct((8, 128), F32)],
        scratch_shapes=[pltpu.VMEM((1, D), F32)],
        compiler_params=_params("arbitrary"),
    )(y, target)


def _adamw(w, m, v, terms, slots, *, name):
    R, C = w.shape
    nt = len(terms)
    tr = _pick(R, 256, 16)
    c1 = 1.0 - ADAM_B1 ** ADAM_STEP
    c2 = 1.0 - ADAM_B2 ** ADAM_STEP

    def body(slot_ref, w_ref, m_ref, v_ref, *rest):
        t_refs = rest[:nt]
        g_ref, d_ref, nm_ref, nv_ref = rest[nt:]
        g = t_refs[0][...].astype(F32)
        for t in t_refs[1:]:
            g = g + t[...].astype(F32)
        mm = ADAM_B1 * m_ref[...] + (1.0 - ADAM_B1) * g
        vv = ADAM_B2 * v_ref[...] + (1.0 - ADAM_B2) * (g * g)
        m_hat = mm / c1
        v_hat = vv / c2
        g_ref[...] = g
        d_ref[...] = -ADAM_LR * (m_hat / (jnp.sqrt(v_hat) + ADAM_EPS) + ADAM_WD * w_ref[...])
        nm_ref[...] = mm
        nv_ref[...] = vv

    row = pl.BlockSpec((tr, C), lambda i, s: (i, 0))
    grid_spec = pltpu.PrefetchScalarGridSpec(
        num_scalar_prefetch=1, grid=(R // tr,),
        in_specs=[row, row, row] + [pl.BlockSpec((None, tr, C), lambda i, s, t=t: (s[t], i, 0)) for t in range(nt)],
        out_specs=[row] * 4)
    return pl.pallas_call(
        body, name=name, grid_spec=grid_spec, out_shape=[jax.ShapeDtypeStruct((R, C), F32)] * 4,
        compiler_params=_params("parallel"),
    )(slots, w, m, v, *terms)


def _chip_partials(g, sib, core, *, name):
    _, R, C = g.shape
    tr = _pick(R, 1200, 16)

    def body(core_ref, g_ref, s_ref, o_ref):
        o_ref[...] = (g_ref[...] + s_ref[...].astype(F32)).astype(o_ref.dtype)

    grid_spec = pltpu.PrefetchScalarGridSpec(
        num_scalar_prefetch=1, grid=(4, R // tr),
        in_specs=[pl.BlockSpec((None, tr, C), lambda k, i, c: (2 * k + c[0], i, 0)),
                  pl.BlockSpec((None, tr, C), lambda k, i, c: (k, i, 0))],
        out_specs=pl.BlockSpec((None, tr, C), lambda k, i, c: (k, i, 0)))
    return pl.pallas_call(
        body, name=name, grid_spec=grid_spec, out_shape=jax.ShapeDtypeStruct((4, R, C), sib.dtype),
        compiler_params=_params("parallel", "parallel"),
    )(core, g, sib)


_ANY = pl.BlockSpec(memory_space=pl.ANY)


def _place():
    return lax.axis_index("x"), lax.axis_index("y"), lax.axis_index("c")


class _Exchange:
    def __init__(self, operands, out_shapes, scratch, emit):
        self.operands, self.out_shapes, self.scratch, self.emit = operands, out_shapes, scratch, emit


def _run_exchange(ex, *, name):
    n_in, n_out = len(ex.operands), len(ex.out_shapes)

    def body(*refs):
        ex.emit(0, 1, refs[:n_in], refs[n_in:n_in + n_out], refs[n_in + n_out:])

    return pl.pallas_call(body, name=name, in_specs=[_ANY] * n_in, out_specs=[_ANY] * n_out, out_shape=ex.out_shapes,
                          scratch_shapes=ex.scratch)(*ex.operands)


def _gather_exchange(shard):
    R, C = shard.shape

    def emit(step, n, ins, outs, sems):
        x_ref, out_ref = ins[0], outs[0]
        send_sems, recv_sems, local_sem = sems
        x, y, c = _place()
        me, sibling = (x, y, c), (x, y, 1 - c)
        chips = [(1 - x, y), (x, 1 - y), (1 - x, 1 - y)]

        def slot(px, py, pc):
            return out_ref.at[4 * px + 2 * py + pc]

        def copy(k, block, to, src=None):
            return pltpu.make_async_remote_copy(
                src_ref=slot(*block) if src is None else src, dst_ref=slot(*block),
                send_sem=send_sems.at[k], recv_sem=recv_sems.at[k], device_id=to, device_id_type=MESH)

        mine = pltpu.make_async_copy(x_ref, slot(*me), local_sem)
        first = [copy(0, me, sibling, src=x_ref)] + [copy(1 + j, me, (*chip, c), src=x_ref) for j, chip in enumerate(chips)]
        passed = [copy(4 + j, (*chip, c), sibling) for j, chip in enumerate(chips)]

        @pl.when(step == 0)
        def _():
            mine.start()
            for cp in first:
                cp.start()

        for j, chip in enumerate(chips):
            @pl.when(step == max(n - 2 * (len(chips) - j), 0))
            def _(j=j, chip=chip):
                copy(1 + j, (*chip, c), me).wait_recv()
                passed[j].start()

        @pl.when(step == n - 1)
        def _():
            copy(0, sibling, me).wait_recv()
            for j, chip in enumerate(chips):
                copy(4 + j, (*chip, 1 - c), me).wait_recv()
            for cp in first + passed:
                cp.wait_send()
            mine.wait()

    return _Exchange([shard], [jax.ShapeDtypeStruct((N_DEV, R, C), shard.dtype)],
                     [pltpu.SemaphoreType.DMA((7,)), pltpu.SemaphoreType.DMA((7,)), pltpu.SemaphoreType.DMA], emit)


def _all_gather(shard, *, name):
    return _run_exchange(_gather_exchange(shard), name=name)[0]


def _rs_sibling(g, *, name):
    _, R, C = g.shape

    def body(g_ref, sib_ref, send_sems, recv_sems):
        x, y, c = _place()
        sends = [pltpu.make_async_remote_copy(
            src_ref=g_ref.at[2 * k + (1 - c)], dst_ref=sib_ref.at[k], send_sem=send_sems.at[k], recv_sem=recv_sems.at[k],
            device_id=(x, y, 1 - c), device_id_type=MESH) for k in range(4)]
        for cp in sends:
            cp.start()
        for cp in sends:
            cp.wait_recv()
        for cp in sends:
            cp.wait_send()

    return pl.pallas_call(
        body, name=name, in_specs=[_ANY], out_specs=_ANY, out_shape=jax.ShapeDtypeStruct((4, R, C), g.dtype),
        scratch_shapes=[pltpu.SemaphoreType.DMA((4,)), pltpu.SemaphoreType.DMA((4,))],
    )(g)


def _chips_exchange(part):
    _, R, C = part.shape

    def emit(step, n, ins, outs, sems):
        p_ref, out_ref = ins[0], outs[0]
        send_sems, recv_sems = sems
        x, y, c = _place()
        chips = [(1 - x, y), (x, 1 - y), (1 - x, 1 - y)]
        sends = [pltpu.make_async_remote_copy(
            src_ref=p_ref.at[2 * px + py], dst_ref=out_ref.at[j], send_sem=send_sems.at[j], recv_sem=recv_sems.at[j],
            device_id=(px, py, c), device_id_type=MESH) for j, (px, py) in enumerate(chips)]

        @pl.when(step == 0)
        def _():
            for cp in sends:
                cp.start()

        @pl.when(step == n - 1)
        def _():
            for cp in sends:
                cp.wait_recv()
            for cp in sends:
                cp.wait_send()

    return _Exchange([part], [jax.ShapeDtypeStruct((3, R, C), part.dtype)],
                     [pltpu.SemaphoreType.DMA((3,)), pltpu.SemaphoreType.DMA((3,))], emit)


def _rs_chips(part, *, name):
    return _run_exchange(_chips_exchange(part), name=name)[0]


def _interleave_rows(wt):
    F2, D = wt.shape
    return wt.reshape(2, F2 // 256, 128, D).transpose(1, 0, 2, 3).reshape(F2, D)


def _deinterleave_rows(wt):
    F2, D = wt.shape
    return wt.reshape(F2 // 256, 2, 128, D).transpose(1, 0, 2, 3).reshape(F2, D)


def _interleave_cols(v):
    k, F2 = v.shape
    return v.reshape(k, 2, F2 // 256, 128).transpose(0, 2, 1, 3).reshape(k, F2)


def _deinterleave_cols(v):
    k, F2 = v.shape
    return v.reshape(k, F2 // 256, 2, 128).transpose(0, 2, 1, 3).reshape(k, F2)


def _pack_rows(parts):
    return jnp.concatenate(parts, axis=0)


def _flat_pack(parts, width):
    flat = jnp.concatenate([p.reshape(-1) for p in parts])
    pad = (-flat.shape[0]) % (8 * width)
    return jnp.pad(flat, (0, pad)).reshape(-1, width)


def _flat_unpack(packed, shapes):
    flat = packed.reshape(-1)
    out, off = [], 0
    for shp in shapes:
        n = int(np.prod(shp))
        out.append(flat[off:off + n].reshape(shp))
        off += n
    return out


def _ffn_forward(x, g_ffn, wupT, wdown, dw_i, dwb_i, tag):
    hf = _rms_fwd(x, g_ffn, name=f"ffn{tag}_rms")
    up = _mm(hf, wupT, mode="nt", out_dtype=MXU_DTYPE, name=f"ffn{tag}_up", tm=2048, tn=512)
    act = _ffn_act_fwd(up, dw_i, dwb_i, name=f"ffn{tag}_act")
    y = _mm(act, wdown, mode="nn", out_dtype=F32, name=f"ffn{tag}_down", residual=x)
    return y, (hf, up, act)


def _ffn_backward(x, g_ffn, wupT, wdown, dw_i, dwb_i, saved, dy, dyb, tag, exchange=None):
    hf, up, act = saved
    dact = _mm(dyb, wdown, mode="nt", out_dtype=MXU_DTYPE, name=f"ffn{tag}_dact", tm=1024, tn=1408)
    d_wdown = _mm(act, dyb, mode="tn", out_dtype=F32, name=f"ffn{tag}_dwdown", tm=1408, tk=2048)
    dup, d_dw_i, d_dwb_i, *carried = _ffn_act_bwd(up, dact, dw_i, dwb_i, name=f"ffn{tag}_actbwd", exchange=exchange)
    dhf = _mm(dup, wupT, mode="nn", out_dtype=F32, name=f"ffn{tag}_dhf", tm=512)
    d_wupT = _mm(dup, hf, mode="tn", out_dtype=F32, name=f"ffn{tag}_dwup", tm=1408, tk=2048)
    dx, dxb, dg, cs = _rms_bwd(x, g_ffn, [dhf], dy, name=f"ffn{tag}_rmsbwd")
    return dx, dxb, cs, dict(w_upT=d_wupT, w_down=d_wdown, dw=d_dw_i[0:FFN_KERNEL], dw_b=d_dwb_i, norm=dg), carried


def _local_step(x, target, p, late_weights=None, early_reduce=None):
    S, D = x.shape
    H = D // HEAD_DIM
    h0 = _rms_fwd(x, p["norm_mix"][0:1], name="l0_rms")
    u = _mm(h0, p["w_inT"], mode="nt", out_dtype=F32, name="l0_in", bias=p["cm_b_in"])
    c, s, *carried = _cm_fwd(u, p["cm_dw"], p["cm_dw_b"], p["cm_ln_g"], p["cm_ln_b"], name="l0_conv",
                             exchange=None if late_weights is None else late_weights[0])
    if late_weights is not None:
        p = {**p, **late_weights[1](carried)}
    x1 = _mm(s, p["w_out"], mode="nn", out_dtype=F32, name="l0_out", bias=p["cm_b_out"], residual=x)
    x2, sv0 = _ffn_forward(x1, p["norm_ffn"][0:1], p["w_upT"][0], p["w_down"][0], p["ff_dw"][0], p["ff_dw_b"][0:1], 0)
    dils = [dil for _, dil in DILATED_GROUPS]
    assert dils[0] == 1
    h1s = [t.reshape(S, D) for t in _rms_fwd(x2, p["norm_mix"][1:2], name="l1_rms", subs=tuple(dils[1:]))]
    qkvs, rqks, os_, ls_ = [], [], [], []
    for g in range(len(dils)):
        qkv_g, r_g = _qkv_proj(h1s[g], p["w_qkvT"], grp=g, name=f"l1_qkv{g}")
        qkvs.append(qkv_g)
        rqks.append(r_g)
        o, l = _attn_fwd(qkvs[g], p["at_q_norm"][g * H:(g + 1) * H], p["at_k_norm"][g * H:(g + 1) * H], grp=g, name=f"l1_attn{g}")
        os_.append(o)
        ls_.append(l)
    outb, lses = _attn_merge(os_, ls_, dils, name="l1_merge")
    x3 = _mm(outb, p["w_o"], mode="nn", out_dtype=F32, name="l1_o", residual=x2)
    x4, sv1 = _ffn_forward(x3, p["norm_ffn"][1:2], p["w_upT"][1], p["w_down"][1], p["ff_dw"][1], p["ff_dw_b"][1:2], 1)
    dx4, dx4b, loss = _loss_head(x4, target, name="loss")
    dx3, dx3b, _, gf1, _ = _ffn_backward(x3, p["norm_ffn"][1:2], p["w_upT"][1], p["w_down"][1], p["ff_dw"][1], p["ff_dw_b"][1:2],
                                      sv1, dx4, dx4b, 1)
    do = _mm(dx3b, p["w_o"], mode="nt", out_dtype=MXU_DTYPE, name="l1_do")
    d_wo = _mm(outb, dx3b, mode="tn", out_dtype=F32, name="l1_dwo", tk=2048)
    deltas, dos = _attn_delta(do, outb, dils, name="l1_delta")
    dh1s, d_wqkvT, dqg, dkg = [], [], [], []
    for g, dil in enumerate(dils):
        dqkv_g, a, b_ = _attn_bwd(qkvs[g], rqks[g], dos[g], lses[g], deltas[g], p["at_q_norm"][g * H:(g + 1) * H],
                                  p["at_k_norm"][g * H:(g + 1) * H], grp=g, name=f"l1_attnbwd{g}")
        dqg.append(a)
        dkg.append(b_)
        d_wqkvT.append(_mm(dqkv_g, h1s[g], mode="tn", out_dtype=F32, name=f"l1_dwqkv{g}", tk=2048))
        dh1s.append(_mm(dqkv_g, p["w_qkvT"], mode="nn", out_dtype=F32, name=f"l1_dh{g}", b_off=g * 3 * D, b_len=3 * D))
    dx2, dx2b, dgm1, _ = _rms_bwd(x2, p["norm_mix"][1:2], dh1s[0:1], dx3, name="l1_rmsbwd",
                                  dh_subs=[(dh1s[g].reshape(dils[g], S // dils[g], D), dils[g]) for g in range(1, len(dils))])
    ex, finish = (None, None) if early_reduce is None else early_reduce(
        dict(w_qkvT=jnp.concatenate(d_wqkvT, axis=0), w_o=d_wo, w_upT=gf1["w_upT"], w_down=gf1["w_down"]))
    dx1, dx1b, cs1, gf0, carried = _ffn_backward(x1, p["norm_ffn"][0:1], p["w_upT"][0], p["w_down"][0], p["ff_dw"][0],
                                                 p["ff_dw_b"][0:1], sv0, dx2, dx2b, 0, exchange=ex)
    reduced = [] if finish is None else [finish(carried)]
    ex, finish = (None, None) if early_reduce is None else early_reduce(dict(w_upT=gf0["w_upT"], w_down=gf0["w_down"]))
    ds = _mm(dx1b, p["w_out"], mode="nt", out_dtype=F32, name="l0_ds")
    d_wout = _mm(s, dx1b, mode="tn", out_dtype=F32, name="l0_dwout", tk=2048)
    dc, d_lng, d_lnb = _cm_ln_bwd(c, ds, p["cm_ln_g"], p["cm_ln_b"], name="l0_lnbwd")
    du, d_cmdw, d_cmdwb, d_bin, *carried = _cm_conv_bwd(dc, u, p["cm_dw"], name="l0_convbwd", exchange=ex)
    if finish is not None:
        reduced.append(finish(carried))
    dh0 = _mm(du, p["w_inT"], mode="nn", out_dtype=F32, name="l0_dh")
    d_winT = _mm(du, h0, mode="tn", out_dtype=F32, name="l0_dwin", tk=2048)
    grad_x, _, dgm0, _ = _rms_bwd(x, p["norm_mix"][0:1], [dh0], dx1, name="l0_rmsbwd")
    grads = dict(
        norm_mix=jnp.concatenate([dgm0, dgm1], axis=0),
        norm_ffn=jnp.concatenate([gf0["norm"], gf1["norm"]], axis=0),
        w_inT=d_winT, cm_b_in=d_bin, cm_dw=d_cmdw[0:CONV_KERNEL], cm_dw_b=d_cmdwb, cm_ln_g=d_lng, cm_ln_b=d_lnb,
        w_out=d_wout, cm_b_out=cs1,
        w_qkvT=jnp.concatenate(d_wqkvT, axis=0), at_q_norm=jnp.concatenate(dqg, axis=0), at_k_norm=jnp.concatenate(dkg, axis=0),
        w_o=d_wo,
        w_upT=[gf0["w_upT"], gf1["w_upT"]], w_down=[gf0["w_down"], gf1["w_down"]],
        ff_dw=jnp.stack([gf0["dw"], gf1["dw"]]), ff_dw_b=jnp.concatenate([gf0["dw_b"], gf1["dw_b"]], axis=0),
    )
    return loss, grad_x, grads, reduced


_BIG = ("cm_w_in", "cm_w_out", "at_w_qkv", "at_w_out", "ff_w_up", "ff_w_down")
_TRANSPOSED = ("cm_w_in", "at_w_qkv", "ff_w_up")
_SMALL = ("norm_mix", "norm_ffn", "cm_b_in", "cm_dw_b", "cm_ln_g", "cm_ln_b", "cm_b_out", "at_q_norm", "at_k_norm",
          "ff_dw_b", "cm_dw", "ff_dw")
_SMALL_SHARDED = ("cm_dw", "ff_dw")
_ORDER = ("norm_mix", "norm_ffn", "cm_w_in", "cm_b_in", "cm_dw", "cm_dw_b", "cm_ln_g", "cm_ln_b", "cm_w_out", "cm_b_out",
          "at_w_qkv", "at_q_norm", "at_k_norm", "at_w_out", "ff_w_up", "ff_dw", "ff_dw_b", "ff_w_down")


_UNITS = (("cm_w_in", 0), ("cm_w_out", 0), ("ff_w_up", 0), ("ff_w_down", 0),
          ("at_w_qkv", 0), ("at_w_out", 0), ("ff_w_up", 1), ("ff_w_down", 1))
_N_FIRST = 2
_N_LAYER0 = 4


def _unit_rows(t, n, l):
    return t[n].shape[2] if n in _TRANSPOSED else t[n].shape[1]


def _big_rows(t, units=_UNITS):
    return _pack_rows([t[n][l].T if n in _TRANSPOSED else t[n][l] for n, l in units])


def _big_unrows(packed, like):
    mats, off = {}, 0
    for n, l in _UNITS:
        rows = _unit_rows(like, n, l)
        m = packed[off:off + rows]
        off += rows
        mats[(n, l)] = m.T if n in _TRANSPOSED else m
    return {n: jnp.stack([mats[(n, l)] for l in range(like[n].shape[0])]) for n in _BIG}


def kernel(x, norm_mix, norm_ffn, cm_w_in, cm_b_in, cm_dw, cm_dw_b, cm_ln_g, cm_ln_b, cm_w_out, cm_b_out, at_w_qkv, at_q_norm, at_k_norm, at_w_out, ff_w_up, ff_dw, ff_dw_b, ff_w_down, loss_target, m_norm_mix, m_norm_ffn, m_cm_w_in, m_cm_b_in, m_cm_dw, m_cm_dw_b, m_cm_ln_g, m_cm_ln_b, m_cm_w_out, m_cm_b_out, m_at_w_qkv, m_at_q_norm, m_at_k_norm, m_at_w_out, m_ff_w_up, m_ff_dw, m_ff_dw_b, m_ff_w_down, v_norm_mix, v_norm_ffn, v_cm_w_in, v_cm_b_in, v_cm_dw, v_cm_dw_b, v_cm_ln_g, v_cm_ln_b, v_cm_w_out, v_cm_b_out, v_at_w_qkv, v_at_q_norm, v_at_k_norm, v_at_w_out, v_ff_w_up, v_ff_dw, v_ff_dw_b, v_ff_w_down):
    w = dict(norm_mix=norm_mix, norm_ffn=norm_ffn, cm_w_in=cm_w_in, cm_b_in=cm_b_in, cm_dw=cm_dw, cm_dw_b=cm_dw_b, cm_ln_g=cm_ln_g,
             cm_ln_b=cm_ln_b, cm_w_out=cm_w_out, cm_b_out=cm_b_out, at_w_qkv=at_w_qkv, at_q_norm=at_q_norm, at_k_norm=at_k_norm,
             at_w_out=at_w_out, ff_w_up=ff_w_up, ff_dw=ff_dw, ff_dw_b=ff_dw_b, ff_w_down=ff_w_down)
    m = dict(norm_mix=m_norm_mix, norm_ffn=m_norm_ffn, cm_w_in=m_cm_w_in, cm_b_in=m_cm_b_in, cm_dw=m_cm_dw, cm_dw_b=m_cm_dw_b,
             cm_ln_g=m_cm_ln_g, cm_ln_b=m_cm_ln_b, cm_w_out=m_cm_w_out, cm_b_out=m_cm_b_out, at_w_qkv=m_at_w_qkv,
             at_q_norm=m_at_q_norm, at_k_norm=m_at_k_norm, at_w_out=m_at_w_out, ff_w_up=m_ff_w_up, ff_dw=m_ff_dw,
             ff_dw_b=m_ff_dw_b, ff_w_down=m_ff_w_down)
    v = dict(norm_mix=v_norm_mix, norm_ffn=v_norm_ffn, cm_w_in=v_cm_w_in, cm_b_in=v_cm_b_in, cm_dw=v_cm_dw, cm_dw_b=v_cm_dw_b,
             cm_ln_g=v_cm_ln_g, cm_ln_b=v_cm_ln_b, cm_w_out=v_cm_w_out, cm_b_out=v_cm_b_out, at_w_qkv=v_at_w_qkv,
             at_q_norm=v_at_q_norm, at_k_norm=v_at_k_norm, at_w_out=v_at_w_out, ff_w_up=v_ff_w_up, ff_dw=v_ff_dw,
             ff_dw_b=v_ff_dw_b, ff_w_down=v_ff_w_down)
    S, D = x.shape[1], x.shape[2]
    F2 = ff_dw_b.shape[1]
    H3 = at_q_norm.shape[1]
    me = 4 * lax.axis_index("x") + 2 * lax.axis_index("y") + lax.axis_index("c")

    ix, iy, ic = lax.axis_index("x"), lax.axis_index("y"), lax.axis_index("c")
    chip = 2 * ix + iy
    core = jnp.stack([ic]).astype(jnp.int32)
    w_rows = _big_rows(w)
    unit_rows = [_unit_rows(w, n, l) for n, l in _UNITS]
    n_first = sum(unit_rows[:_N_FIRST])
    n_layer0 = sum(unit_rows[:_N_LAYER0])
    w_wire = w_rows.astype(MXU_DTYPE)

    def unpack(gathered, units, rows):
        full, off = {}, 0
        for (n, l), r in zip(units, rows):
            full[(n, l)] = gathered[:, off:off + r, :].reshape(N_DEV * r, D)
            off += r
        out = {}
        if ("cm_w_in", 0) in full:
            out.update(w_inT=full[("cm_w_in", 0)], w_out=full[("cm_w_out", 0)])
        if ("at_w_qkv", 0) in full:
            out.update(w_qkvT=full[("at_w_qkv", 0)], w_o=full[("at_w_out", 0)],
                       w_upT=[_interleave_rows(full[("ff_w_up", l)]) for l in range(2)],
                       w_down=[full[("ff_w_down", l)] for l in range(2)])
        return out

    first = _all_gather(w_wire[:n_first], name="gather_first")
    late_weights = (_gather_exchange(w_wire[n_first:]),
                    lambda carried: unpack(carried[0], _UNITS[_N_FIRST:], unit_rows[_N_FIRST:]))
    small_sh = _flat_pack([cm_dw, ff_dw], D)
    small_g = _all_gather(small_sh, name="gather_small")
    cm_dw_full = jnp.concatenate([_flat_unpack(small_g[j], [cm_dw.shape, ff_dw.shape])[0][0] for j in range(N_DEV)], axis=-1)
    ff_dw_full = jnp.concatenate([_flat_unpack(small_g[j], [cm_dw.shape, ff_dw.shape])[1] for j in range(N_DEV)], axis=-1)

    p = dict(
        norm_mix=norm_mix, norm_ffn=norm_ffn, cm_b_in=cm_b_in, cm_dw=cm_dw_full, cm_dw_b=cm_dw_b, cm_ln_g=cm_ln_g, cm_ln_b=cm_ln_b,
        cm_b_out=cm_b_out, at_q_norm=at_q_norm[0], at_k_norm=at_k_norm[0],
        ff_dw=jnp.stack([_interleave_cols(ff_dw_full[l]) for l in range(ff_dw_full.shape[0])]),
        ff_dw_b=_interleave_cols(ff_dw_b),
        **unpack(first, _UNITS[:_N_FIRST], unit_rows[:_N_FIRST]),
    )

    def pack(pieces):
        return jnp.concatenate([t.reshape(N_DEV, t.shape[0] // N_DEV, D) for t in pieces], axis=1)

    def reduce_start(pieces, tag):
        g_rows = pack(pieces)
        sib = _rs_sibling(g_rows.astype(WIRE_DTYPE), name=f"reduce{tag}_sibling")
        return g_rows, sib, _chip_partials(g_rows, sib, core, name=f"reduce{tag}_add")

    def early_reduce(gd):
        ffn = [_deinterleave_rows(gd["w_upT"]), gd["w_down"]]
        tag, pieces = (2, [gd["w_qkvT"], gd["w_o"]] + ffn) if "w_qkvT" in gd else (1, ffn)
        g_rows, sib, part = reduce_start(pieces, tag)
        return _chips_exchange(part), lambda carried: (g_rows, sib, carried[0])

    loss8, grad_x, g, (reduced2, reduced1) = _local_step(x[0], loss_target[0], p, late_weights, early_reduce)
    loss = lax.psum(loss8[0, 0], ("x", "y", "c"))
    g_rows0, sib0, part0 = reduce_start([g["w_inT"], g["w_out"]], 0)
    reduced0 = (g_rows0, sib0, _rs_chips(part0, name="reduce0_chips"))
    slots = jnp.stack([me, chip, 0 * me, 0 * me + 1, 0 * me + 2]).astype(jnp.int32)
    m_rows, v_rows = _big_rows(m), _big_rows(v)
    updated = []
    for tag, (g_rows, sib, recv), rows in ((0, reduced0, slice(0, n_first)), (1, reduced1, slice(n_first, n_layer0)),
                                           (2, reduced2, slice(n_layer0, None))):
        updated.append(_adamw(w_rows[rows], m_rows[rows], v_rows[rows], [g_rows, sib, recv, recv, recv], slots, name=f"adamw_big{tag}"))
    big = [_big_unrows(jnp.concatenate([u[k] for u in updated], axis=0), w) for k in range(4)]

    g_small = dict(g)
    g_small["cm_b_in"] = g["cm_b_in"]
    g_small["at_q_norm"] = g["at_q_norm"][None]
    g_small["at_k_norm"] = g["at_k_norm"][None]
    g_small["ff_dw_b"] = _deinterleave_cols(g["ff_dw_b"])
    g_small["cm_dw"] = g["cm_dw"][None]
    g_small["ff_dw"] = jnp.stack([_deinterleave_cols(g["ff_dw"][l]) for l in range(g["ff_dw"].shape[0])])
    small_shapes = [g_small[n].shape for n in _SMALL]
    gs_parts = _all_gather(_flat_pack([g_small[n] for n in _SMALL], D), name="gather_small_grads")

    def embed(t, n):
        if n not in _SMALL_SHARDED:
            return t
        full_shape = t.shape[:-1] + (t.shape[-1] * N_DEV,)
        return lax.dynamic_update_slice_in_dim(jnp.zeros(full_shape, F32), t, me * t.shape[-1], axis=t.ndim - 1)

    packs = [_flat_pack([embed(tree[n], n) for n in _SMALL], D) for tree in (w, m, v)]
    gs, ds_, ms, vs = _adamw(packs[0], packs[1], packs[2], [gs_parts] * N_DEV, jnp.arange(N_DEV, dtype=jnp.int32),
                             name="adamw_small")
    small = []
    for t in (gs, ds_, ms, vs):
        un = dict(zip(_SMALL, _flat_unpack(t, small_shapes)))
        for n in _SMALL_SHARDED:
            width = w[n].shape[-1]
            un[n] = lax.dynamic_slice_in_dim(un[n], me * width, width, axis=un[n].ndim - 1)
        small.append({n: un[n].reshape(w[n].shape) for n in _SMALL})

    outs = [loss, grad_x[None]]
    for k in range(4):
        for n in _ORDER:
            outs.append(big[k][n] if n in _BIG else small[k][n])
    return tuple(outs)
```

```python
import functools

import jax
import jax.numpy as jnp
import numpy as np
from jax import lax
from jax.experimental import pallas as pl
from jax.experimental.pallas import tpu as pltpu

F32 = jnp.float32
MXU_DTYPE = jnp.bfloat16
WIRE_DTYPE = jnp.bfloat16
EPS = 1e-6
NEG = -1e30
HEAD_DIM = 128
BLOCK = 128
DILATED_GROUPS = ((128, 1), (512, 4), (2048, 16))
ALIBI_MAX = 8.0
CONV_KERNEL = 31
CONV_HALO = 32
CONV_ROWS = 64
FFN_KERNEL = 3
FFN_HALO = 16
FFN_ROWS = 64
ADAM_LR, ADAM_B1, ADAM_B2, ADAM_EPS, ADAM_WD, ADAM_STEP = 0.001, 0.9, 0.999, 1e-08, 0.01, 10
V7X_VMEM_BYTES = 64 * 1024 * 1024
VMEM_LIMIT = V7X_VMEM_BYTES * 3 // 4
N_DEV = 8
MESH = pl.DeviceIdType.MESH


def _pick(n, target, align):
    if n <= target:
        return n
    best = None
    for t in range(align, target + 1, align):
        if n % t == 0:
            best = t
    assert best is not None, (n, target, align)
    return best


def _params(*sem):
    return pltpu.CompilerParams(dimension_semantics=sem, vmem_limit_bytes=VMEM_LIMIT)


def _sigmoid(x):
    return 1.0 / (1.0 + jnp.exp(-x))


_DIMS = {"nn": ((1,), (0,)), "nt": ((1,), (1,)), "tn": ((0,), (0,))}


def _mm(a, b, *, mode, out_dtype, name, tm=1024, tn=1024, tk=None, bias=None, residual=None, b_off=0, b_len=None,
        post=None, keep_main=True):
    if mode == "tn":
        K, M = a.shape
    else:
        M, K = a.shape
    if mode == "nt":
        N = b.shape[0] if b_len is None else b_len
    else:
        N = b.shape[1]
    if b_len is not None:
        assert mode == "nt" or (mode == "nn" and K == b_len)
    tm = _pick(M, tm, 128 if mode == "tn" else 16)
    tn = _pick(N, tn, 128)
    tk = K if tk is None else _pick(K, tk, 128 if mode != "tn" else 16)
    nk = K // tk
    unit = tn if mode == "nt" else tk
    assert b_off % unit == 0
    kb0 = b_off // unit
    if mode == "tn":
        a_spec = pl.BlockSpec((tk, tm), lambda i, j, k: (k, i))
    else:
        a_spec = pl.BlockSpec((tm, tk), lambda i, j, k: (i, k))
    if mode == "nt":
        b_spec = pl.BlockSpec((tn, tk), lambda i, j, k: (j + kb0, k))
    else:
        b_spec = pl.BlockSpec((tk, tn), lambda i, j, k: (k + kb0, j))
    in_specs = [a_spec, b_spec]
    args = [a, b]
    if bias is not None:
        in_specs.append(pl.BlockSpec((1, tn), lambda i, j, k: (0, j)))
        args.append(bias)
    if residual is not None:
        in_specs.append(pl.BlockSpec((tm, tn), lambda i, j, k: (i, j)))
        args.append(residual)
    has_bias, has_res = bias is not None, residual is not None
    kinds = {"tile": ((tm, tn), lambda i, j, k: (i, j)), "row": ((1, tn), lambda i, j, k: (0, j)),
             "lanes": ((8, 128), lambda i, j, k: (0, 0))}
    post_in = [] if post is None else post.ins
    post_out = [] if post is None else post.outs
    if post is not None:
        assert tn == N
        for arr, kind in post_in:
            in_specs.append(pl.BlockSpec(*kinds[kind]))
            args.append(arr)
    out_specs = [pl.BlockSpec((tm, tn), lambda i, j, k: (i, j))] if keep_main else []
    out_shape = [jax.ShapeDtypeStruct((M, N), out_dtype)] if keep_main else []
    for kind, dtype in post_out:
        out_specs.append(pl.BlockSpec(*kinds[kind]))
        out_shape.append(jax.ShapeDtypeStruct({"tile": (M, N), "row": (1, N), "lanes": (8, 128)}[kind], dtype))
    accumulates = any(kind != "tile" for kind, _ in post_out)

    def body(*refs):
        a_ref, b_ref = refs[0], refs[1]
        pos = 2
        bias_ref = res_ref = None
        if has_bias:
            bias_ref = refs[pos]
            pos += 1
        if has_res:
            res_ref = refs[pos]
            pos += 1
        pin_refs = refs[pos:pos + len(post_in)]
        pos += len(post_in)
        o_ref = refs[pos] if keep_main else None
        pos += 1 if keep_main else 0
        pout_refs = refs[pos:pos + len(post_out)]
        pos += len(post_out)
        acc_ref = refs[pos] if nk > 1 else None

        def finish(acc):
            if has_bias:
                acc = acc + bias_ref[...]
            if has_res:
                acc = acc + res_ref[...]
            if keep_main:
                o_ref[...] = acc.astype(o_ref.dtype)
            if post is not None:
                post.fn(acc, pin_refs, pout_refs, pl.program_id(0) == 0)

        part = lax.dot_general(a_ref[...].astype(MXU_DTYPE), b_ref[...].astype(MXU_DTYPE), (_DIMS[mode], ((), ())),
                               preferred_element_type=F32)
        if nk == 1:
            finish(part)
        else:
            k = pl.program_id(2)

            @pl.when(k == 0)
            def _():
                acc_ref[...] = part

            @pl.when(jnp.logical_and(k > 0, k < nk - 1))
            def _():
                acc_ref[...] += part

            @pl.when(k == nk - 1)
            def _():
                finish(acc_ref[...] + part)

    outs = pl.pallas_call(
        body, name=name, grid=(M // tm, N // tn, nk), in_specs=in_specs, out_specs=out_specs, out_shape=out_shape,
        scratch_shapes=[pltpu.VMEM((tm, tn), F32)] if nk > 1 else [],
        compiler_params=_params("arbitrary" if accumulates else "parallel", "parallel", "arbitrary"),
    )(*args)
    return outs[0] if post is None else outs


class _Post:
    def __init__(self, ins, outs, fn):
        self.ins, self.outs, self.fn = ins, outs, fn


def _accumulate(ref, value, first):
    @pl.when(first)
    def _():
        ref[...] = value

    @pl.when(jnp.logical_not(first))
    def _():
        ref[...] += value


def _post_rms(g):
    def fn(acc, ins, outs, first):
        r = lax.rsqrt(jnp.mean(acc * acc, axis=-1, keepdims=True) + EPS)
        outs[0][...] = (acc * r * ins[0][...]).astype(outs[0].dtype)

    return _Post([(g, "row")], [("tile", MXU_DTYPE)], fn)


def _post_loss(target):
    def fn(acc, ins, outs, first):
        e = acc - ins[0][...]
        dy = e * (1.0 / acc.shape[-1])
        outs[0][...] = dy
        outs[1][...] = dy.astype(outs[1].dtype)
        part = jnp.sum(jnp.sum(e * e, axis=0, keepdims=True), axis=1, keepdims=True) * (0.5 / acc.shape[-1])
        _accumulate(outs[2], jnp.broadcast_to(part, outs[2].shape), first)

    return _Post([(target, "tile")], [("tile", F32), ("tile", MXU_DTYPE), ("lanes", F32)], fn)


def _post_rms_bwd(x, g, dres):
    def fn(acc, ins, outs, first):
        xv = ins[0][...]
        r = lax.rsqrt(jnp.mean(xv * xv, axis=-1, keepdims=True) + EPS)
        xh = xv * r
        gy = acc * ins[1][...]
        dx = r * (gy - xh * jnp.mean(gy * xh, axis=-1, keepdims=True)) + ins[2][...]
        outs[0][...] = dx
        outs[1][...] = dx.astype(outs[1].dtype)
        _accumulate(outs[2], jnp.sum(acc * xh, axis=0, keepdims=True), first)
        _accumulate(outs[3], jnp.sum(dx, axis=0, keepdims=True), first)

    return _Post([(x, "tile"), (g, "row"), (dres, "tile")], [("tile", F32), ("tile", MXU_DTYPE), ("row", F32), ("row", F32)], fn)


SUB_TILE = 512


def _sub_spec(dil, ts, cols):
    return pl.BlockSpec((dil, ts // dil, cols), lambda i: (0, i, 0))


def _tok_to_sub(tok_ref, dst_ref, dil):
    nc, ts, _ = tok_ref.shape
    for c in range(nc):
        for r in range(dil):
            dst_ref[r, :, c * 128:(c + 1) * 128] = tok_ref.at[c][pl.ds(r, ts // dil, stride=dil), :].astype(dst_ref.dtype)


def _sub_to_tok(src_ref, tok_ref, dil):
    nc, ts, _ = tok_ref.shape
    for c in range(nc):
        for r in range(dil):
            tok_ref.at[c][pl.ds(r, ts // dil, stride=dil), :] = src_ref[r, :, c * 128:(c + 1) * 128].astype(F32)


def _rms_fwd(x, g, *, name, subs=()):
    S, D = x.shape
    ts = _pick(S, SUB_TILE, 16 * max(subs, default=1))
    NC = D // 128

    def body(x_ref, g_ref, h_ref, *rest):
        xv = x_ref[...]
        r = lax.rsqrt(jnp.mean(xv * xv, axis=-1, keepdims=True) + EPS)
        h = xv * r * g_ref[...]
        h_ref[...] = h.astype(h_ref.dtype)
        if subs:
            tok_ref = rest[-1]
            for c in range(NC):
                tok_ref[c] = h[:, c * 128:(c + 1) * 128]
            for dil, dst_ref in zip(subs, rest):
                _tok_to_sub(tok_ref, dst_ref, dil)

    row = pl.BlockSpec((ts, D), lambda i: (i, 0))
    outs = pl.pallas_call(
        body, name=name, grid=(S // ts,),
        in_specs=[row, pl.BlockSpec((1, D), lambda i: (0, 0))],
        out_specs=[row] + [_sub_spec(dil, ts, D) for dil in subs],
        out_shape=[jax.ShapeDtypeStruct((S, D), MXU_DTYPE)] + [jax.ShapeDtypeStruct((dil, S // dil, D), MXU_DTYPE) for dil in subs],
        scratch_shapes=[pltpu.VMEM((NC, ts, 128), F32)] if subs else [],
        compiler_params=_params("parallel"),
    )(x, g)
    return outs if subs else outs[0]


def _rms_bwd(x, g, dhs, dres, *, name, dh_subs=()):
    S, D = x.shape
    ts = _pick(S, SUB_TILE, 16 * max([dil for _, dil in dh_subs], default=1))
    n_dh, n_sub = len(dhs), len(dh_subs)
    NC = D // 128

    def body(*refs):
        x_ref, g_ref = refs[0], refs[1]
        dh_refs = refs[2:2 + n_dh]
        sub_refs = refs[2 + n_dh:2 + n_dh + n_sub]
        dres_ref, dx_ref, dxb_ref, dg_ref, cs_ref = refs[2 + n_dh + n_sub:7 + n_dh + n_sub]
        i = pl.program_id(0)
        xv = x_ref[...]
        r = lax.rsqrt(jnp.mean(xv * xv, axis=-1, keepdims=True) + EPS)
        xh = xv * r
        dhv = dh_refs[0][...].astype(F32)
        for t in dh_refs[1:]:
            dhv = dhv + t[...].astype(F32)
        for (_, dil), sub_ref in zip(dh_subs, sub_refs):
            tok_ref = refs[-1]
            _sub_to_tok(sub_ref, tok_ref, dil)
            dhv = dhv + jnp.concatenate([tok_ref[c] for c in range(NC)], axis=1)
        gy = dhv * g_ref[...]
        dx = r * (gy - xh * jnp.mean(gy * xh, axis=-1, keepdims=True)) + dres_ref[...]
        dx_ref[...] = dx
        dxb_ref[...] = dx.astype(dxb_ref.dtype)
        dg = jnp.sum(dhv * xh, axis=0, keepdims=True)
        cs = jnp.sum(dx, axis=0, keepdims=True)

        @pl.when(i == 0)
        def _():
            dg_ref[...] = dg
            cs_ref[...] = cs

        @pl.when(i > 0)
        def _():
            dg_ref[...] += dg
            cs_ref[...] += cs

    row = pl.BlockSpec((ts, D), lambda i: (i, 0))
    vec = pl.BlockSpec((1, D), lambda i: (0, 0))
    return pl.pallas_call(
        body, name=name, grid=(S // ts,),
        in_specs=[row, vec] + [row] * n_dh + [_sub_spec(dil, ts, D) for _, dil in dh_subs] + [row],
        out_specs=[row, row, vec, vec],
        out_shape=[jax.ShapeDtypeStruct((S, D), F32), jax.ShapeDtypeStruct((S, D), MXU_DTYPE),
                   jax.ShapeDtypeStruct((1, D), F32), jax.ShapeDtypeStruct((1, D), F32)],
        scratch_shapes=[pltpu.VMEM((NC, ts, 128), F32)] if n_sub else [],
        compiler_params=_params("arbitrary"),
    )(x, g, *dhs, *[a for a, _ in dh_subs], dres)


def _conv_phases(ph_ref, ts):
    n = ts + CONV_HALO - 8
    for b in range(1, 8):
        ph_ref[b, 0:n, :] = ph_ref[0, pl.ds(b, n), :]


def _phase_taps(base, step=1):
    groups = {}
    for k in range(CONV_KERNEL):
        a, b = divmod(base + step * k, 8)
        groups.setdefault(b, []).append((a, k))
    out = []
    for b in sorted(groups):
        ak = sorted(groups[b])
        assert [a for a, _ in ak] == list(range(ak[0][0], ak[0][0] + len(ak)))
        out.append((b, ak[0][0], [k for _, k in ak]))
    return out


def _cm_fwd(u, dw, dw_b, ln_g, ln_b, *, name, exchange=None):
    S, D2 = u.shape
    D = D2 // 2
    ts = _pick(S, 256, CONV_HALO)
    hb = ts // CONV_HALO
    ex_in, ex_out, ex_scr = ([], [], []) if exchange is None else (exchange.operands, exchange.out_shapes, exchange.scratch)

    def body(u_ref, up_ref, dw_ref, dwb_ref, g_ref, b_ref, *rest):
        xi = rest[:len(ex_in)]
        c_ref, s_ref = rest[len(ex_in):len(ex_in) + 2]
        xo = rest[len(ex_in) + 2:len(ex_in) + 2 + len(ex_out)]
        ext_ref = rest[len(ex_in) + 2 + len(ex_out)]
        i = pl.program_id(0)
        if exchange is not None:
            exchange.emit(i, S // ts, xi, xo, rest[len(ex_in) + 3 + len(ex_out):])
        prev = up_ref[:, :D] * _sigmoid(up_ref[:, D:])
        ext_ref[0:CONV_HALO, :] = jnp.where(i > 0, prev, 0.0)
        ext_ref[CONV_HALO:CONV_HALO + ts, :] = u_ref[:, :D] * _sigmoid(u_ref[:, D:])
        for cc in range(D // 128):
            sl = slice(cc * 128, (cc + 1) * 128)
            acc = jnp.zeros((ts, 128), F32) + dwb_ref[:, sl]
            for k in range(CONV_KERNEL):
                acc = acc + dw_ref[k:k + 1, sl] * ext_ref[pl.ds(CONV_HALO - (CONV_KERNEL - 1) + k, ts), sl]
            c_ref[:, sl] = acc
        c = c_ref[...]
        mu = jnp.mean(c, axis=-1, keepdims=True)
        xc = c - mu
        rstd = lax.rsqrt(jnp.mean(xc * xc, axis=-1, keepdims=True) + EPS)
        y = xc * rstd * g_ref[...] + b_ref[...]
        s_ref[...] = (y * _sigmoid(y)).astype(s_ref.dtype)

    vec = pl.BlockSpec((1, D), lambda i: (0, 0))
    return pl.pallas_call(
        body, name=name, grid=(S // ts,),
        in_specs=[pl.BlockSpec((ts, D2), lambda i: (i, 0)),
                  pl.BlockSpec((CONV_HALO, D2), lambda i: (jnp.maximum(i * hb - 1, 0), 0)),
                  pl.BlockSpec((CONV_KERNEL, D), lambda i: (0, 0)), vec, vec, vec] + [_ANY] * len(ex_in),
        out_specs=[pl.BlockSpec((ts, D), lambda i: (i, 0)), pl.BlockSpec((ts, D), lambda i: (i, 0))] + [_ANY] * len(ex_out),
        out_shape=[jax.ShapeDtypeStruct((S, D), F32), jax.ShapeDtypeStruct((S, D), MXU_DTYPE)] + list(ex_out),
        scratch_shapes=[pltpu.VMEM((ts + CONV_HALO, D), F32)] + list(ex_scr),
        compiler_params=_params("parallel" if exchange is None else "arbitrary"),
    )(u, u, dw, dw_b, ln_g, ln_b, *ex_in)


def _cm_ln_bwd(c, ds, ln_g, ln_b, *, name):
    S, D = c.shape
    ts = _pick(S, 512, 16)

    def body(c_ref, ds_ref, g_ref, b_ref, dc_ref, dg_ref, db_ref):
        i = pl.program_id(0)
        cv = c_ref[...]
        mu = jnp.mean(cv, axis=-1, keepdims=True)
        xc = cv - mu
        rstd = lax.rsqrt(jnp.mean(xc * xc, axis=-1, keepdims=True) + EPS)
        xh = xc * rstd
        y = xh * g_ref[...] + b_ref[...]
        sg = _sigmoid(y)
        dy = ds_ref[...].astype(F32) * (sg * (1.0 + y * (1.0 - sg)))
        gy = dy * g_ref[...]
        dc_ref[...] = rstd * (gy - jnp.mean(gy, axis=-1, keepdims=True) - xh * jnp.mean(gy * xh, axis=-1, keepdims=True))
        dg = jnp.sum(dy * xh, axis=0, keepdims=True)
        db = jnp.sum(dy, axis=0, keepdims=True)

        @pl.when(i == 0)
        def _():
            dg_ref[...] = dg
            db_ref[...] = db

        @pl.when(i > 0)
        def _():
            dg_ref[...] += dg
            db_ref[...] += db

    row = pl.BlockSpec((ts, D), lambda i: (i, 0))
    vec = pl.BlockSpec((1, D), lambda i: (0, 0))
    return pl.pallas_call(
        body, name=name, grid=(S // ts,), in_specs=[row, row, vec, vec], out_specs=[row, vec, vec],
        out_shape=[jax.ShapeDtypeStruct((S, D), F32), jax.ShapeDtypeStruct((1, D), F32), jax.ShapeDtypeStruct((1, D), F32)],
        compiler_params=_params("arbitrary"),
    )(c, ds, ln_g, ln_b)


def _cm_conv_bwd(dc, u, dw, *, name, exchange=None):
    ex_in, ex_out, ex_scr = ([], [], []) if exchange is None else (exchange.operands, exchange.out_shapes, exchange.scratch)
    S, D2 = u.shape
    D = D2 // 2
    ts = _pick(S, 256, CONV_HALO)
    hb = ts // CONV_HALO
    n_t = S // ts
    last_h = S // CONV_HALO - 1

    rc = _pick(ts, CONV_ROWS, 8)

    def fold8(v):
        out = v[0:8]
        for j in range(1, v.shape[0] // 8):
            out = out + v[8 * j:8 * j + 8]
        return out

    def body(dc_ref, dcn_ref, u_ref, up_ref, dw_ref, *rest):
        xi = rest[:len(ex_in)]
        du_ref, ddw_ref, ddwb_ref, dbin_ref = rest[len(ex_in):len(ex_in) + 4]
        xo = rest[len(ex_in) + 4:len(ex_in) + 4 + len(ex_out)]
        dph_ref, gph_ref, dgl_ref = rest[len(ex_in) + 4 + len(ex_out):len(ex_in) + 7 + len(ex_out)]
        i = pl.program_id(0)
        if exchange is not None:
            exchange.emit(i, n_t, xi, xo, rest[len(ex_in) + 7 + len(ex_out):])
        dph_ref[0, 0:ts, :] = dc_ref[...]
        dph_ref[0, ts:ts + CONV_HALO, :] = jnp.where(i < n_t - 1, dcn_ref[...], 0.0)
        prev = up_ref[:, :D] * _sigmoid(up_ref[:, D:])
        gph_ref[0, 0:CONV_HALO, :] = jnp.where(i > 0, prev, 0.0)
        gph_ref[0, CONV_HALO:CONV_HALO + ts, :] = u_ref[:, :D] * _sigmoid(u_ref[:, D:])
        _conv_phases(dph_ref, ts)
        _conv_phases(gph_ref, ts)

        @pl.when(i == 0)
        def _():
            ddw_ref[...] = jnp.zeros_like(ddw_ref)
            ddwb_ref[...] = jnp.zeros_like(ddwb_ref)
            dbin_ref[...] = jnp.zeros_like(dbin_ref)

        for cc in range(D // 128):
            sl = slice(cc * 128, (cc + 1) * 128)
            sl2 = slice(D + cc * 128, D + (cc + 1) * 128)
            wk = [dw_ref[k:k + 1, sl] for k in range(CONV_KERNEL)]
            acc_a, acc_g = jnp.zeros((8, 128), F32), jnp.zeros((8, 128), F32)
            dgl = jnp.zeros((ts, 128), F32)
            for b, a0, taps in _phase_taps(CONV_KERNEL - 1, -1):
                for j, k in enumerate(taps):
                    dgl = dgl + wk[k] * dph_ref[b, 8 * (a0 + j):8 * (a0 + j) + ts, sl]
            dgl_ref[...] = dgl
            for r0 in range(0, ts, rc):
                dglu = dgl_ref[r0:r0 + rc, :]
                av = u_ref[r0:r0 + rc, sl]
                sg = _sigmoid(u_ref[r0:r0 + rc, sl2])
                da = dglu * sg
                dg = dglu * av * sg * (1.0 - sg)
                du_ref[r0:r0 + rc, sl] = da.astype(du_ref.dtype)
                du_ref[r0:r0 + rc, sl2] = dg.astype(du_ref.dtype)
                acc_a = acc_a + fold8(da)
                acc_g = acc_g + fold8(dg)
            dbin_ref[:, sl] += jnp.sum(acc_a, axis=0, keepdims=True)
            dbin_ref[:, sl2] += jnp.sum(acc_g, axis=0, keepdims=True)
            for gi, (b, a0, taps) in enumerate(_phase_taps(CONV_HALO - (CONV_KERNEL - 1))):
                accs = [jnp.zeros((8, 128), F32) for _ in taps]
                accb = jnp.zeros((8, 128), F32)
                for r0 in range(0, ts, rc):
                    dcc = dph_ref[0, r0:r0 + rc, sl]
                    win = gph_ref[b, 8 * a0 + r0:8 * (a0 + len(taps) - 1) + r0 + rc, sl]
                    for j in range(len(taps)):
                        accs[j] = accs[j] + fold8(dcc * win[8 * j:8 * j + rc])
                    if gi == 0:
                        accb = accb + fold8(dcc)
                for j, k in enumerate(taps):
                    ddw_ref[k:k + 1, sl] += jnp.sum(accs[j], axis=0, keepdims=True)
                if gi == 0:
                    ddwb_ref[:, sl] += jnp.sum(accb, axis=0, keepdims=True)

    return pl.pallas_call(
        body, name=name, grid=(n_t,),
        in_specs=[pl.BlockSpec((ts, D), lambda i: (i, 0)),
                  pl.BlockSpec((CONV_HALO, D), lambda i: (jnp.minimum((i + 1) * hb, last_h), 0)),
                  pl.BlockSpec((ts, D2), lambda i: (i, 0)),
                  pl.BlockSpec((CONV_HALO, D2), lambda i: (jnp.maximum(i * hb - 1, 0), 0)),
                  pl.BlockSpec((CONV_KERNEL, D), lambda i: (0, 0))] + [_ANY] * len(ex_in),
        out_specs=[pl.BlockSpec((ts, D2), lambda i: (i, 0)), pl.BlockSpec((CONV_HALO, D), lambda i: (0, 0)),
                   pl.BlockSpec((1, D), lambda i: (0, 0)), pl.BlockSpec((1, D2), lambda i: (0, 0))] + [_ANY] * len(ex_out),
        out_shape=[jax.ShapeDtypeStruct((S, D2), MXU_DTYPE), jax.ShapeDtypeStruct((CONV_HALO, D), F32),
                   jax.ShapeDtypeStruct((1, D), F32), jax.ShapeDtypeStruct((1, D2), F32)] + list(ex_out),
        scratch_shapes=[pltpu.VMEM((8, ts + CONV_HALO, D), F32), pltpu.VMEM((8, ts + CONV_HALO, D), F32),
                        pltpu.VMEM((ts, 128), F32)] + list(ex_scr),
        compiler_params=_params("arbitrary"),
    )(dc, dc, u, u, dw, *ex_in)


def _ffn_cols(F2):
    return _pick(F2, 1024, 256)


def _ffn_act_fwd(up, dw, dw_b, *, name):
    S, F2 = up.shape
    ts = _pick(S, 512, 16)
    tc = _ffn_cols(F2)
    hb = ts // FFN_HALO

    rc = _pick(ts, FFN_ROWS, 16)

    def body(u_ref, up_ref, w_ref, b_ref, a_ref, ext_ref):
        i = pl.program_id(1)
        ext_ref[0:FFN_HALO, :] = jnp.where(i > 0, up_ref[...].astype(F32), 0.0)
        ext_ref[FFN_HALO:FFN_HALO + ts, :] = u_ref[...].astype(F32)
        for q in range(tc // 256):
            sls = [slice(q * 256 + half * 128, q * 256 + half * 128 + 128) for half in range(2)]
            wk = [[w_ref[k:k + 1, sl] for k in range(FFN_KERNEL)] for sl in sls]
            bb = [b_ref[:, sl] for sl in sls]
            for r0 in range(0, ts, rc):
                gt, vl = [bb[h] + sum(wk[h][k] * ext_ref[pl.ds(FFN_HALO + r0 - 2 + k, rc), sls[h]] for k in range(FFN_KERNEL))
                          for h in range(2)]
                a_ref[r0:r0 + rc, q * 128:(q + 1) * 128] = (gt * _sigmoid(gt) * vl).astype(a_ref.dtype)

    return pl.pallas_call(
        body, name=name, grid=(F2 // tc, S // ts),
        in_specs=[pl.BlockSpec((ts, tc), lambda j, i: (i, j)),
                  pl.BlockSpec((FFN_HALO, tc), lambda j, i: (jnp.maximum(i * hb - 1, 0), j)),
                  pl.BlockSpec((FFN_KERNEL, tc), lambda j, i: (0, j)),
                  pl.BlockSpec((1, tc), lambda j, i: (0, j))],
        out_specs=pl.BlockSpec((ts, tc // 2), lambda j, i: (i, j)),
        out_shape=jax.ShapeDtypeStruct((S, F2 // 2), MXU_DTYPE),
        scratch_shapes=[pltpu.VMEM((ts + FFN_HALO, tc), F32)],
        compiler_params=_params("parallel", "parallel"),
    )(up, up, dw, dw_b)


def _ffn_act_bwd(up, dact, dw, dw_b, *, name, exchange=None):
    ex_in, ex_out, ex_scr = ([], [], []) if exchange is None else (exchange.operands, exchange.out_shapes, exchange.scratch)
    S, F2 = up.shape
    ts = _pick(S, 512, 16)
    tc = _ffn_cols(F2)
    hb = ts // FFN_HALO
    n_t = S // ts
    last_h = S // FFN_HALO - 1
    E = ts + FFN_HALO

    rc = _pick(ts, FFN_ROWS, 16)

    def fold8(v):
        out = v[0:8]
        for j in range(1, v.shape[0] // 8):
            out = out + v[8 * j:8 * j + 8]
        return out

    def body(u_ref, up_ref, un_ref, da_ref, dan_ref, w_ref, b_ref, *rest):
        xi = rest[:len(ex_in)]
        dup_ref, ddw_ref, ddb_ref = rest[len(ex_in):len(ex_in) + 3]
        xo = rest[len(ex_in) + 3:len(ex_in) + 3 + len(ex_out)]
        ue_ref, dcv_ref = rest[len(ex_in) + 3 + len(ex_out):len(ex_in) + 5 + len(ex_out)]
        i = pl.program_id(1)
        if exchange is not None:
            exchange.emit(pl.program_id(0) * n_t + i, (F2 // tc) * n_t, xi, xo, rest[len(ex_in) + 5 + len(ex_out):])
        ue_ref[0:FFN_HALO, :] = jnp.where(i > 0, up_ref[...].astype(F32), 0.0)
        ue_ref[FFN_HALO:FFN_HALO + ts, :] = u_ref[...].astype(F32)
        ue_ref[FFN_HALO + ts:FFN_HALO + ts + FFN_HALO, :] = jnp.where(i < n_t - 1, un_ref[...].astype(F32), 0.0)

        @pl.when(i == 0)
        def _():
            ddw_ref[...] = jnp.zeros_like(ddw_ref)
            ddb_ref[...] = jnp.zeros_like(ddb_ref)

        for q in range(tc // 256):
            sls = [slice(q * 256 + half * 128, q * 256 + half * 128 + 128) for half in range(2)]
            qs = slice(q * 128, (q + 1) * 128)
            wk = [[w_ref[k:k + 1, sl] for k in range(FFN_KERNEL)] for sl in sls]
            bb = [b_ref[:, sl] for sl in sls]
            acc = [[jnp.zeros((8, 128), F32) for _ in range(FFN_KERNEL)] for _ in range(2)]
            accb = [jnp.zeros((8, 128), F32) for _ in range(2)]
            for r0, rows in [(r, rc) for r in range(0, ts, rc)] + [(ts, FFN_HALO)]:
                xs = [[ue_ref[pl.ds(FFN_HALO + r0 - 2 + k, rows), sls[h]] for k in range(FFN_KERNEL)] for h in range(2)]
                gt, vl = [bb[h] + sum(wk[h][k] * xs[h][k] for k in range(FFN_KERNEL)) for h in range(2)]
                sg = _sigmoid(gt)
                if r0 < ts:
                    dae = da_ref[r0:r0 + rows, qs].astype(F32)
                else:
                    dae = jnp.where(i < n_t - 1, dan_ref[:, qs].astype(F32), 0.0)
                dcv = [dae * vl * (sg * (1.0 + gt * (1.0 - sg))), dae * (gt * sg)]
                for h in range(2):
                    dcv_ref[r0:r0 + rows, sls[h]] = dcv[h]
                    if r0 < ts:
                        for k in range(FFN_KERNEL):
                            acc[h][k] = acc[h][k] + fold8(dcv[h] * xs[h][k])
                        accb[h] = accb[h] + fold8(dcv[h])
            for r0 in range(0, ts, rc):
                for h in range(2):
                    dup = sum(wk[h][2 - j] * dcv_ref[pl.ds(r0 + j, rc), sls[h]] for j in range(FFN_KERNEL))
                    dup_ref[r0:r0 + rc, sls[h]] = dup.astype(dup_ref.dtype)
            for h in range(2):
                for k in range(FFN_KERNEL):
                    ddw_ref[k:k + 1, sls[h]] += jnp.sum(acc[h][k], axis=0, keepdims=True)
                ddb_ref[:, sls[h]] += jnp.sum(accb[h], axis=0, keepdims=True)

    return pl.pallas_call(
        body, name=name, grid=(F2 // tc, n_t),
        in_specs=[pl.BlockSpec((ts, tc), lambda j, i: (i, j)),
                  pl.BlockSpec((FFN_HALO, tc), lambda j, i: (jnp.maximum(i * hb - 1, 0), j)),
                  pl.BlockSpec((FFN_HALO, tc), lambda j, i: (jnp.minimum((i + 1) * hb, last_h), j)),
                  pl.BlockSpec((ts, tc // 2), lambda j, i: (i, j)),
                  pl.BlockSpec((FFN_HALO, tc // 2), lambda j, i: (jnp.minimum((i + 1) * hb, last_h), j)),
                  pl.BlockSpec((FFN_KERNEL, tc), lambda j, i: (0, j)),
                  pl.BlockSpec((1, tc), lambda j, i: (0, j))] + [_ANY] * len(ex_in),
        out_specs=[pl.BlockSpec((ts, tc), lambda j, i: (i, j)), pl.BlockSpec((FFN_HALO, tc), lambda j, i: (0, j)),
                   pl.BlockSpec((1, tc), lambda j, i: (0, j))] + [_ANY] * len(ex_out),
        out_shape=[jax.ShapeDtypeStruct((S, F2), MXU_DTYPE), jax.ShapeDtypeStruct((FFN_HALO, F2), F32),
                   jax.ShapeDtypeStruct((1, F2), F32)] + list(ex_out),
        scratch_shapes=[pltpu.VMEM((ts + 2 * FFN_HALO, tc), F32), pltpu.VMEM((E, tc), F32)] + list(ex_scr),
        compiler_params=_params("parallel" if exchange is None else "arbitrary", "arbitrary"),
    )(up, up, up, dact, dact, dw, dw_b, *ex_in)


def _slopes(n_heads_total):
    return np.asarray(2.0 ** (-ALIBI_MAX * (np.arange(n_heads_total, dtype=np.float32) + 1.0) / n_heads_total), np.float32)


def _qkv_proj(h, w_qkvT, *, grp, name):
    S, D = h.shape
    H = D // HEAD_DIM
    tm = _pick(S, 1024, 16)

    rows = _pick(tm, 256, 16)

    def body(a_ref, b_ref, o_ref, r_ref):
        j = pl.program_id(1)
        r_ref[...] = jnp.zeros_like(r_ref)

        def product(c):
            return lax.dot_general(a_ref[c * rows:(c + 1) * rows, :].astype(MXU_DTYPE), b_ref[...].astype(MXU_DTYPE),
                                   (_DIMS["nt"], ((), ())), preferred_element_type=F32)

        @pl.when(j < 2)
        def _():
            acc = product(0)
            for c in range(tm // rows):
                nxt = product(c + 1) if c + 1 < tm // rows else None
                rs = slice(c * rows, (c + 1) * rows)
                for hd in range(H):
                    hs = slice(hd * HEAD_DIM, (hd + 1) * HEAD_DIM)
                    xv = acc[:, hs]
                    r = lax.rsqrt(jnp.mean(xv * xv, axis=-1, keepdims=True) + EPS)
                    o_ref[rs, hs] = (xv * r).astype(o_ref.dtype)
                    r_ref[rs, hd:hd + 1] = r
                acc = nxt

        @pl.when(j == 2)
        def _():
            o_ref[...] = lax.dot_general(a_ref[...].astype(MXU_DTYPE), b_ref[...].astype(MXU_DTYPE), (_DIMS["nt"], ((), ())),
                                         preferred_element_type=F32).astype(o_ref.dtype)

    return pl.pallas_call(
        body, name=name, grid=(S // tm, 3),
        in_specs=[pl.BlockSpec((tm, D), lambda i, j: (i, 0)), pl.BlockSpec((D, D), lambda i, j: (grp * 3 + j, 0))],
        out_specs=[pl.BlockSpec((tm, D), lambda i, j: (i, j)), pl.BlockSpec((tm, HEAD_DIM), lambda i, j: (i, j))],
        out_shape=[jax.ShapeDtypeStruct((S, 3 * D), MXU_DTYPE), jax.ShapeDtypeStruct((S, 3 * HEAD_DIM), F32)],
        compiler_params=_params("parallel", "parallel"),
    )(h, w_qkvT)


def _band(b, dil):
    qi = lax.broadcasted_iota(jnp.int32, (BLOCK, 2 * BLOCK), 0)
    ki = lax.broadcasted_iota(jnp.int32, (BLOCK, 2 * BLOCK), 1)
    delta = qi + BLOCK - ki
    valid = (delta >= 0) & (delta <= BLOCK) & ((ki >= BLOCK) | (b > 0))
    return valid, (delta * dil).astype(F32)


def _attn_fwd(qkv, qg, kg, *, grp, name):
    S, W = qkv.shape
    D = W // 3
    H = D // HEAD_DIM
    dil = DILATED_GROUPS[grp][1]
    L = S // dil
    nb = L // BLOCK
    slopes = _slopes(3 * H)[grp * H:(grp + 1) * H]
    scale = HEAD_DIM ** -0.5

    def body(q_ref, kp_ref, kc_ref, vp_ref, vc_ref, qg_ref, kg_ref, o_ref, l_ref):
        b = pl.program_id(1)
        valid, dist = _band(b, dil)
        l_ref[...] = jnp.zeros_like(l_ref)
        ss = []
        for h in range(H):
            hs = slice(h * HEAD_DIM, (h + 1) * HEAD_DIM)
            qn = (q_ref[:, hs].astype(F32) * qg_ref[h:h + 1, :]).astype(MXU_DTYPE)
            kp = (kp_ref[:, hs].astype(F32) * kg_ref[h:h + 1, :]).astype(MXU_DTYPE)
            kc = (kc_ref[:, hs].astype(F32) * kg_ref[h:h + 1, :]).astype(MXU_DTYPE)
            ss.append(lax.dot_general(qn, jnp.concatenate([kp, kc], axis=0), (((1,), (1,)), ((), ())), preferred_element_type=F32))
        ps = []
        for h in range(H):
            s = jnp.where(valid, ss[h] * scale - float(slopes[h]) * dist, NEG)
            m = jnp.max(s, axis=-1, keepdims=True)
            p = jnp.exp(s - m)
            den = jnp.sum(p, axis=-1, keepdims=True)
            l_ref[:, h:h + 1] = m + jnp.log(den)
            ps.append((p.astype(MXU_DTYPE), den))
        for h in range(H):
            hs = slice(h * HEAD_DIM, (h + 1) * HEAD_DIM)
            pb, den = ps[h]
            v2 = jnp.concatenate([vp_ref[:, hs], vc_ref[:, hs]], axis=0).astype(MXU_DTYPE)
            o_ref[:, hs] = (jnp.dot(pb, v2, preferred_element_type=F32) / den).astype(o_ref.dtype)

    def cur(j):
        return lambda r, b: (r * nb + b, j)

    def prv(j):
        return lambda r, b: (r * nb + jnp.maximum(b - 1, 0), j)

    blk = (BLOCK, D)
    gain = pl.BlockSpec((H, HEAD_DIM), lambda r, b: (0, 0))
    return pl.pallas_call(
        body, name=name, grid=(dil, nb),
        in_specs=[pl.BlockSpec(blk, cur(0)), pl.BlockSpec(blk, prv(1)), pl.BlockSpec(blk, cur(1)),
                  pl.BlockSpec(blk, prv(2)), pl.BlockSpec(blk, cur(2)), gain, gain],
        out_specs=[pl.BlockSpec(blk, cur(0)), pl.BlockSpec((BLOCK, HEAD_DIM), cur(0))],
        out_shape=[jax.ShapeDtypeStruct((S, D), MXU_DTYPE), jax.ShapeDtypeStruct((S, HEAD_DIM), F32)],
        compiler_params=_params("parallel", "parallel"),
    )(qkv, qkv, qkv, qkv, qkv, qg, kg)


def _attn_merge(os_, ls_, dils, *, name):
    S, D = os_[0].shape
    H = D // HEAD_DIM
    G = len(dils)
    ts = _pick(S, SUB_TILE, 16 * max(dils))
    subs = [g for g in range(G) if dils[g] > 1]

    def body(*refs):
        o_refs, l_refs = refs[0:G], refs[G:2 * G]
        outb_ref = refs[2 * G]
        lt_refs = refs[2 * G + 1:3 * G + 1]
        scratch = refs[3 * G + 1:]
        lt_tok = scratch[0]
        o_tok = {g: scratch[1 + 2 * j] for j, g in enumerate(subs)}
        l_tok = {g: scratch[2 + 2 * j] for j, g in enumerate(subs)}
        for g in subs:
            _sub_to_tok(o_refs[g], o_tok[g], dils[g])
            _sub_to_tok(l_refs[g], l_tok[g], dils[g])
        lt_tok[0] = jnp.zeros((ts, HEAD_DIM), F32)
        for h in range(H):
            hs = slice(h * HEAD_DIM, (h + 1) * HEAD_DIM)
            ls = [l_tok[g][0][:, h:h + 1] if g in subs else l_refs[g][:, h:h + 1] for g in range(G)]
            ov = [o_tok[g][h] if g in subs else o_refs[g][:, hs].astype(F32) for g in range(G)]
            m = functools.reduce(jnp.maximum, ls)
            es = [jnp.exp(l - m) for l in ls]
            den = functools.reduce(lambda a, b: a + b, es)
            out = functools.reduce(lambda a, b: a + b, [e * o for e, o in zip(es, ov)]) / den
            outb_ref[:, hs] = out.astype(outb_ref.dtype)
            lt_tok.at[0][:, h:h + 1] = m + jnp.log(den)
        for g in range(G):
            if g in subs:
                _tok_to_sub(lt_tok, lt_refs[g], dils[g])
            else:
                lt_refs[g][...] = lt_tok[0]

    def spec(g, cols):
        return _sub_spec(dils[g], ts, cols) if g in subs else pl.BlockSpec((ts, cols), lambda i: (i, 0))

    def shape(g, cols, dtype):
        return jax.ShapeDtypeStruct((dils[g], S // dils[g], cols) if g in subs else (S, cols), dtype)

    def view(a, g):
        return a.reshape(dils[g], S // dils[g], a.shape[-1]) if g in subs else a

    outs = pl.pallas_call(
        body, name=name, grid=(S // ts,),
        in_specs=[spec(g, D) for g in range(G)] + [spec(g, HEAD_DIM) for g in range(G)],
        out_specs=[pl.BlockSpec((ts, D), lambda i: (i, 0))] + [spec(g, HEAD_DIM) for g in range(G)],
        out_shape=[jax.ShapeDtypeStruct((S, D), MXU_DTYPE)] + [shape(g, HEAD_DIM, F32) for g in range(G)],
        scratch_shapes=[pltpu.VMEM((1, ts, HEAD_DIM), F32)] + [pltpu.VMEM((H, ts, HEAD_DIM), F32), pltpu.VMEM((1, ts, HEAD_DIM), F32)] * len(subs),
        compiler_params=_params("parallel"),
    )(*[view(o, g) for g, o in enumerate(os_)], *[view(l, g) for g, l in enumerate(ls_)])
    return outs[0], [t.reshape(S, HEAD_DIM) for t in outs[1:]]


def _attn_delta(do, out, dils, *, name):
    S, D = out.shape
    H = D // HEAD_DIM
    G = len(dils)
    ts = _pick(S, SUB_TILE, 16 * max(dils))
    subs = [g for g in range(G) if dils[g] > 1]

    def body(do_ref, o_ref, *rest):
        d_refs = rest[0:G]
        dos_refs = rest[G:G + len(subs)]
        d_tok, do_tok = rest[G + len(subs):]
        d_tok[0] = jnp.zeros((ts, HEAD_DIM), F32)
        for h in range(H):
            hs = slice(h * HEAD_DIM, (h + 1) * HEAD_DIM)
            dov = do_ref[:, hs].astype(F32)
            do_tok[h] = dov
            d_tok.at[0][:, h:h + 1] = jnp.sum(dov * o_ref[:, hs].astype(F32), axis=-1, keepdims=True)
        for g in range(G):
            if g in subs:
                _tok_to_sub(d_tok, d_refs[g], dils[g])
            else:
                d_refs[g][...] = d_tok[0]
        for g, dst in zip(subs, dos_refs):
            _tok_to_sub(do_tok, dst, dils[g])

    def spec(g, cols):
        return _sub_spec(dils[g], ts, cols) if g in subs else pl.BlockSpec((ts, cols), lambda i: (i, 0))

    def shape(g, cols, dtype):
        return jax.ShapeDtypeStruct((dils[g], S // dils[g], cols) if g in subs else (S, cols), dtype)

    row = pl.BlockSpec((ts, D), lambda i: (i, 0))
    outs = pl.pallas_call(
        body, name=name, grid=(S // ts,), in_specs=[row, row],
        out_specs=[spec(g, HEAD_DIM) for g in range(G)] + [spec(g, D) for g in subs],
        out_shape=[shape(g, HEAD_DIM, F32) for g in range(G)] + [shape(g, D, do.dtype) for g in subs],
        scratch_shapes=[pltpu.VMEM((1, ts, HEAD_DIM), F32), pltpu.VMEM((H, ts, HEAD_DIM), F32)],
        compiler_params=_params("parallel"),
    )(do, out)
    deltas = [t.reshape(S, HEAD_DIM) for t in outs[0:G]]
    dos = {g: t.reshape(S, D) for g, t in zip(subs, outs[G:])}
    return deltas, [dos[g] if g in subs else do for g in range(G)]


def _attn_bwd(qkv, rqk, do, lse, delta, qg, kg, *, grp, name):
    S, W = qkv.shape
    D = W // 3
    H = D // HEAD_DIM
    dil = DILATED_GROUPS[grp][1]
    L = S // dil
    nb = L // BLOCK
    slopes = _slopes(3 * H)[grp * H:(grp + 1) * H]
    scale = HEAD_DIM ** -0.5

    def body(q_ref, qp_ref, kp_ref, kc_ref, vp_ref, vc_ref, do_ref, l_ref, dl_ref, rq_ref, rk_ref, qg_ref, kg_ref,
             out_ref, dqg_ref, dkg_ref, cq_ref, ck_ref, cv_ref, nq_ref, nk_ref, nv_ref, pk_ref, pv_ref):
        r = pl.program_id(0)
        b = pl.program_id(1)

        @pl.when(jnp.logical_and(r == 0, b == 0))
        def _():
            dqg_ref[...] = jnp.zeros_like(dqg_ref)
            dkg_ref[...] = jnp.zeros_like(dkg_ref)

        @pl.when(b < nb)
        def _():
            valid, dist = _band(b, dil)

            def operands(h):
                hs = slice(h * HEAD_DIM, (h + 1) * HEAD_DIM)
                qn = (q_ref[:, hs].astype(F32) * qg_ref[h:h + 1, :]).astype(MXU_DTYPE)
                kp = (kp_ref[:, hs].astype(F32) * kg_ref[h:h + 1, :]).astype(MXU_DTYPE)
                kc = (kc_ref[:, hs].astype(F32) * kg_ref[h:h + 1, :]).astype(MXU_DTYPE)
                k2 = jnp.concatenate([kp, kc], axis=0)
                v2 = jnp.concatenate([vp_ref[:, hs], vc_ref[:, hs]], axis=0).astype(MXU_DTYPE)
                return hs, qn, k2, v2, do_ref[:, hs].astype(MXU_DTYPE)

            sdp = []
            for h in range(H):
                hs, qn, k2, v2, doh = operands(h)
                s = lax.dot_general(qn, k2, (((1,), (1,)), ((), ())), preferred_element_type=F32)
                dp = lax.dot_general(doh, v2, (((1,), (1,)), ((), ())), preferred_element_type=F32)
                sdp.append((s, dp))
            pds = []
            for h in range(H):
                s, dp = sdp[h]
                s = jnp.where(valid, s * scale - float(slopes[h]) * dist, NEG)
                p = jnp.exp(s - l_ref[:, h:h + 1])
                pds.append((p.astype(MXU_DTYPE), (p * (dp - dl_ref[:, h:h + 1]) * scale).astype(MXU_DTYPE)))
            for h in range(H):
                hs, qn, k2, v2, doh = operands(h)
                pb, dsc = pds[h]
                nq_ref[:, hs] = jnp.dot(dsc, k2, preferred_element_type=F32)
                dk2 = lax.dot_general(dsc, qn, (((0,), (0,)), ((), ())), preferred_element_type=F32)
                dv2 = lax.dot_general(pb, doh, (((0,), (0,)), ((), ())), preferred_element_type=F32)
                pk_ref[:, hs] = dk2[0:BLOCK]
                nk_ref[:, hs] = dk2[BLOCK:2 * BLOCK]
                pv_ref[:, hs] = dv2[0:BLOCK]
                nv_ref[:, hs] = dv2[BLOCK:2 * BLOCK]

        @pl.when(b == nb)
        def _():
            pk_ref[...] = jnp.zeros_like(pk_ref)
            pv_ref[...] = jnp.zeros_like(pv_ref)

        @pl.when(b > 0)
        def _():
            for h in range(H):
                hs = slice(h * HEAD_DIM, (h + 1) * HEAD_DIM)
                for j, (xh_ref, r_ref, gain_ref, dgain_ref) in enumerate(((qp_ref, rq_ref, qg_ref, dqg_ref),
                                                                          (kp_ref, rk_ref, kg_ref, dkg_ref))):
                    dy = cq_ref[:, hs] if j == 0 else ck_ref[:, hs] + pk_ref[:, hs]
                    gain = gain_ref[h:h + 1, :]
                    xh = xh_ref[:, hs].astype(F32)
                    rr = r_ref[:, h:h + 1]
                    gy = dy * gain
                    dx = rr * (gy - xh * jnp.mean(gy * xh, axis=-1, keepdims=True))
                    out_ref[:, j * D + h * HEAD_DIM:j * D + (h + 1) * HEAD_DIM] = dx.astype(out_ref.dtype)
                    dgain_ref[h:h + 1, :] += jnp.sum(dy * xh, axis=0, keepdims=True)
                out_ref[:, 2 * D + h * HEAD_DIM:2 * D + (h + 1) * HEAD_DIM] = (cv_ref[:, hs] + pv_ref[:, hs]).astype(out_ref.dtype)

        @pl.when(b < nb)
        def _():
            cq_ref[...] = nq_ref[...]
            ck_ref[...] = nk_ref[...]
            cv_ref[...] = nv_ref[...]

    def cur(j):
        return lambda r, b: (r * nb + jnp.minimum(b, nb - 1), j)

    def prv(j):
        return lambda r, b: (r * nb + jnp.clip(b - 1, 0, nb - 1), j)

    blk = (BLOCK, D)
    lblk = pl.BlockSpec((BLOCK, HEAD_DIM), cur(0))
    gain = pl.BlockSpec((H, HEAD_DIM), lambda r, b: (0, 0))
    return pl.pallas_call(
        body, name=name, grid=(dil, nb + 1),
        in_specs=[pl.BlockSpec(blk, cur(0)), pl.BlockSpec(blk, prv(0)), pl.BlockSpec(blk, prv(1)), pl.BlockSpec(blk, cur(1)),
                  pl.BlockSpec(blk, prv(2)), pl.BlockSpec(blk, cur(2)), pl.BlockSpec(blk, cur(0)), lblk, lblk,
                  pl.BlockSpec((BLOCK, HEAD_DIM), prv(0)), pl.BlockSpec((BLOCK, HEAD_DIM), prv(1)), gain, gain],
        out_specs=[pl.BlockSpec((BLOCK, 3 * D), lambda r, b: (r * nb + jnp.maximum(b - 1, 0), 0)), gain, gain],
        out_shape=[jax.ShapeDtypeStruct((S, 3 * D), MXU_DTYPE), jax.ShapeDtypeStruct((H, HEAD_DIM), F32),
                   jax.ShapeDtypeStruct((H, HEAD_DIM), F32)],
        scratch_shapes=[pltpu.VMEM(blk, F32)] * 8,
        compiler_params=_params("arbitrary", "arbitrary"),
    )(qkv, qkv, qkv, qkv, qkv, qkv, do, lse, delta, rqk, rqk, qg, kg)


def _adamw(w, m, v, terms, slots, *, name):
    R, C = w.shape
    nt = len(terms)
    tr = _pick(R, 256, 16)
    c1 = 1.0 - ADAM_B1 ** ADAM_STEP
    c2 = 1.0 - ADAM_B2 ** ADAM_STEP

    def body(slot_ref, w_ref, m_ref, v_ref, *rest):
        t_refs = rest[:nt]
        g_ref, d_ref, nm_ref, nv_ref = rest[nt:]
        g = t_refs[0][...].astype(F32)
        for t in t_refs[1:]:
            g = g + t[...].astype(F32)
        mm = ADAM_B1 * m_ref[...] + (1.0 - ADAM_B1) * g
        vv = ADAM_B2 * v_ref[...] + (1.0 - ADAM_B2) * (g * g)
        m_hat = mm / c1
        v_hat = vv / c2
        g_ref[...] = g
        d_ref[...] = -ADAM_LR * (m_hat / (jnp.sqrt(v_hat) + ADAM_EPS) + ADAM_WD * w_ref[...])
        nm_ref[...] = mm
        nv_ref[...] = vv

    row = pl.BlockSpec((tr, C), lambda i, s: (i, 0))
    grid_spec = pltpu.PrefetchScalarGridSpec(
        num_scalar_prefetch=1, grid=(R // tr,),
        in_specs=[row, row, row] + [pl.BlockSpec((None, tr, C), lambda i, s, t=t: (s[t], i, 0)) for t in range(nt)],
        out_specs=[row] * 4)
    return pl.pallas_call(
        body, name=name, grid_spec=grid_spec, out_shape=[jax.ShapeDtypeStruct((R, C), F32)] * 4,
        compiler_params=_params("parallel"),
    )(slots, w, m, v, *terms)


def _chip_partials(g, sib, core, *, name):
    _, R, C = g.shape
    tr = _pick(R, 1200, 16)

    def body(core_ref, g_ref, s_ref, o_ref):
        o_ref[...] = (g_ref[...] + s_ref[...].astype(F32)).astype(o_ref.dtype)

    grid_spec = pltpu.PrefetchScalarGridSpec(
        num_scalar_prefetch=1, grid=(4, R // tr),
        in_specs=[pl.BlockSpec((None, tr, C), lambda k, i, c: (2 * k + c[0], i, 0)),
                  pl.BlockSpec((None, tr, C), lambda k, i, c: (k, i, 0))],
        out_specs=pl.BlockSpec((None, tr, C), lambda k, i, c: (k, i, 0)))
    return pl.pallas_call(
        body, name=name, grid_spec=grid_spec, out_shape=jax.ShapeDtypeStruct((4, R, C), sib.dtype),
        compiler_params=_params("parallel", "parallel"),
    )(core, g, sib)


_ANY = pl.BlockSpec(memory_space=pl.ANY)


def _place():
    return lax.axis_index("x"), lax.axis_index("y"), lax.axis_index("c")


class _Exchange:
    def __init__(self, operands, out_shapes, scratch, emit):
        self.operands, self.out_shapes, self.scratch, self.emit = operands, out_shapes, scratch, emit


def _run_exchange(ex, *, name):
    n_in, n_out = len(ex.operands), len(ex.out_shapes)

    def body(*refs):
        ex.emit(0, 1, refs[:n_in], refs[n_in:n_in + n_out], refs[n_in + n_out:])

    return pl.pallas_call(body, name=name, in_specs=[_ANY] * n_in, out_specs=[_ANY] * n_out, out_shape=ex.out_shapes,
                          scratch_shapes=ex.scratch)(*ex.operands)


def _gather_exchange(shard):
    R, C = shard.shape

    def emit(step, n, ins, outs, sems):
        x_ref, out_ref = ins[0], outs[0]
        send_sems, recv_sems, local_sem = sems
        x, y, c = _place()
        me, sibling = (x, y, c), (x, y, 1 - c)
        chips = [(1 - x, y), (x, 1 - y), (1 - x, 1 - y)]

        def slot(px, py, pc):
            return out_ref.at[4 * px + 2 * py + pc]

        def copy(k, block, to, src=None):
            return pltpu.make_async_remote_copy(
                src_ref=slot(*block) if src is None else src, dst_ref=slot(*block),
                send_sem=send_sems.at[k], recv_sem=recv_sems.at[k], device_id=to, device_id_type=MESH)

        mine = pltpu.make_async_copy(x_ref, slot(*me), local_sem)
        first = [copy(0, me, sibling, src=x_ref)] + [copy(1 + j, me, (*chip, c), src=x_ref) for j, chip in enumerate(chips)]
        passed = [copy(4 + j, (*chip, c), sibling) for j, chip in enumerate(chips)]

        @pl.when(step == 0)
        def _():
            mine.start()
            for cp in first:
                cp.start()

        for j, chip in enumerate(chips):
            @pl.when(step == max(n - 2 * (len(chips) - j), 0))
            def _(j=j, chip=chip):
                copy(1 + j, (*chip, c), me).wait_recv()
                passed[j].start()

        @pl.when(step == n - 1)
        def _():
            copy(0, sibling, me).wait_recv()
            for j, chip in enumerate(chips):
                copy(4 + j, (*chip, 1 - c), me).wait_recv()
            for cp in first + passed:
                cp.wait_send()
            mine.wait()

    return _Exchange([shard], [jax.ShapeDtypeStruct((N_DEV, R, C), shard.dtype)],
                     [pltpu.SemaphoreType.DMA((7,)), pltpu.SemaphoreType.DMA((7,)), pltpu.SemaphoreType.DMA], emit)


def _all_gather(shard, *, name):
    return _run_exchange(_gather_exchange(shard), name=name)[0]


def _rs_sibling(g, *, name):
    _, R, C = g.shape

    def body(g_ref, sib_ref, send_sems, recv_sems):
        x, y, c = _place()
        sends = [pltpu.make_async_remote_copy(
            src_ref=g_ref.at[2 * k + (1 - c)], dst_ref=sib_ref.at[k], send_sem=send_sems.at[k], recv_sem=recv_sems.at[k],
            device_id=(x, y, 1 - c), device_id_type=MESH) for k in range(4)]
        for cp in sends:
            cp.start()
        for cp in sends:
            cp.wait_recv()
        for cp in sends:
            cp.wait_send()

    return pl.pallas_call(
        body, name=name, in_specs=[_ANY], out_specs=_ANY, out_shape=jax.ShapeDtypeStruct((4, R, C), g.dtype),
        scratch_shapes=[pltpu.SemaphoreType.DMA((4,)), pltpu.SemaphoreType.DMA((4,))],
    )(g)


def _chips_exchange(part):
    _, R, C = part.shape

    def emit(step, n, ins, outs, sems):
        p_ref, out_ref = ins[0], outs[0]
        send_sems, recv_sems = sems
        x, y, c = _place()
        chips = [(1 - x, y), (x, 1 - y), (1 - x, 1 - y)]
        sends = [pltpu.make_async_remote_copy(
            src_ref=p_ref.at[2 * px + py], dst_ref=out_ref.at[j], send_sem=send_sems.at[j], recv_sem=recv_sems.at[j],
            device_id=(px, py, c), device_id_type=MESH) for j, (px, py) in enumerate(chips)]

        @pl.when(step == 0)
        def _():
            for cp in sends:
                cp.start()

        @pl.when(step == n - 1)
        def _():
            for cp in sends:
                cp.wait_recv()
            for cp in sends:
                cp.wait_send()

    return _Exchange([part], [jax.ShapeDtypeStruct((3, R, C), part.dtype)],
                     [pltpu.SemaphoreType.DMA((3,)), pltpu.SemaphoreType.DMA((3,))], emit)


def _rs_chips(part, *, name):
    return _run_exchange(_chips_exchange(part), name=name)[0]


def _interleave_rows(wt):
    F2, D = wt.shape
    return wt.reshape(2, F2 // 256, 128, D).transpose(1, 0, 2, 3).reshape(F2, D)


def _deinterleave_rows(wt):
    F2, D = wt.shape
    return wt.reshape(F2 // 256, 2, 128, D).transpose(1, 0, 2, 3).reshape(F2, D)


def _interleave_cols(v):
    k, F2 = v.shape
    return v.reshape(k, 2, F2 // 256, 128).transpose(0, 2, 1, 3).reshape(k, F2)


def _deinterleave_cols(v):
    k, F2 = v.shape
    return v.reshape(k, F2 // 256, 2, 128).transpose(0, 2, 1, 3).reshape(k, F2)


def _pack_rows(parts):
    return jnp.concatenate(parts, axis=0)


def _flat_pack(parts, width):
    flat = jnp.concatenate([p.reshape(-1) for p in parts])
    pad = (-flat.shape[0]) % (8 * width)
    return jnp.pad(flat, (0, pad)).reshape(-1, width)


def _flat_unpack(packed, shapes):
    flat = packed.reshape(-1)
    out, off = [], 0
    for shp in shapes:
        n = int(np.prod(shp))
        out.append(flat[off:off + n].reshape(shp))
        off += n
    return out


def _ffn_forward(x, hf, wupT, wdown, dw_i, dwb_i, tag, loss_target=None):
    up = _mm(hf, wupT, mode="nt", out_dtype=MXU_DTYPE, name=f"ffn{tag}_up", tm=2048, tn=512)
    act = _ffn_act_fwd(up, dw_i, dwb_i, name=f"ffn{tag}_act")
    if loss_target is None:
        y = _mm(act, wdown, mode="nn", out_dtype=F32, name=f"ffn{tag}_down", residual=x)
    else:
        y = _mm(act, wdown, mode="nn", out_dtype=F32, name=f"ffn{tag}_down", residual=x, tm=512, post=_post_loss(loss_target),
                keep_main=False)
    return y, (hf, up, act)


def _ffn_backward(x, g_ffn, wupT, wdown, dw_i, dwb_i, saved, dy, dyb, tag, exchange=None):
    hf, up, act = saved
    dact = _mm(dyb, wdown, mode="nt", out_dtype=MXU_DTYPE, name=f"ffn{tag}_dact", tm=1024, tn=1408)
    d_wdown = _mm(act, dyb, mode="tn", out_dtype=F32, name=f"ffn{tag}_dwdown", tm=1408, tk=2048)
    dup, d_dw_i, d_dwb_i, *carried = _ffn_act_bwd(up, dact, dw_i, dwb_i, name=f"ffn{tag}_actbwd", exchange=exchange)
    dx, dxb, dg, cs = _mm(dup, wupT, mode="nn", out_dtype=F32, name=f"ffn{tag}_dhf", tm=512, tk=2816,
                          post=_post_rms_bwd(x, g_ffn, dy), keep_main=False)
    d_wupT = _mm(dup, hf, mode="tn", out_dtype=F32, name=f"ffn{tag}_dwup", tm=1408, tk=2048)
    return dx, dxb, cs, dict(w_upT=d_wupT, w_down=d_wdown, dw=d_dw_i[0:FFN_KERNEL], dw_b=d_dwb_i, norm=dg), carried


def _local_step(x, target, p, late_weights=None, early_reduce=None):
    S, D = x.shape
    H = D // HEAD_DIM
    h0 = _rms_fwd(x, p["norm_mix"][0:1], name="l0_rms")
    u = _mm(h0, p["w_inT"], mode="nt", out_dtype=F32, name="l0_in", bias=p["cm_b_in"])
    c, s, *carried = _cm_fwd(u, p["cm_dw"], p["cm_dw_b"], p["cm_ln_g"], p["cm_ln_b"], name="l0_conv",
                             exchange=None if late_weights is None else late_weights[0])
    if late_weights is not None:
        p = {**p, **late_weights[1](carried)}
    x1, hf0 = _mm(s, p["w_out"], mode="nn", out_dtype=F32, name="l0_out", bias=p["cm_b_out"], residual=x,
                  post=_post_rms(p["norm_ffn"][0:1]))
    x2, sv0 = _ffn_forward(x1, hf0, p["w_upT"][0], p["w_down"][0], p["ff_dw"][0], p["ff_dw_b"][0:1], 0)
    dils = [dil for _, dil in DILATED_GROUPS]
    assert dils[0] == 1
    h1s = [t.reshape(S, D) for t in _rms_fwd(x2, p["norm_mix"][1:2], name="l1_rms", subs=tuple(dils[1:]))]
    qkvs, rqks, os_, ls_ = [], [], [], []
    for g in range(len(dils)):
        qkv_g, r_g = _qkv_proj(h1s[g], p["w_qkvT"], grp=g, name=f"l1_qkv{g}")
        qkvs.append(qkv_g)
        rqks.append(r_g)
        o, l = _attn_fwd(qkvs[g], p["at_q_norm"][g * H:(g + 1) * H], p["at_k_norm"][g * H:(g + 1) * H], grp=g, name=f"l1_attn{g}")
        os_.append(o)
        ls_.append(l)
    outb, lses = _attn_merge(os_, ls_, dils, name="l1_merge")
    x3, hf1 = _mm(outb, p["w_o"], mode="nn", out_dtype=F32, name="l1_o", residual=x2, post=_post_rms(p["norm_ffn"][1:2]))
    (dx4, dx4b, loss), sv1 = _ffn_forward(x3, hf1, p["w_upT"][1], p["w_down"][1], p["ff_dw"][1], p["ff_dw_b"][1:2], 1,
                                          loss_target=target)
    dx3, dx3b, _, gf1, _ = _ffn_backward(x3, p["norm_ffn"][1:2], p["w_upT"][1], p["w_down"][1], p["ff_dw"][1], p["ff_dw_b"][1:2],
                                      sv1, dx4, dx4b, 1)
    do = _mm(dx3b, p["w_o"], mode="nt", out_dtype=MXU_DTYPE, name="l1_do")
    d_wo = _mm(outb, dx3b, mode="tn", out_dtype=F32, name="l1_dwo", tk=2048)
    deltas, dos = _attn_delta(do, outb, dils, name="l1_delta")
    dh1s, d_wqkvT, dqg, dkg = [], [], [], []
    for g, dil in enumerate(dils):
        dqkv_g, a, b_ = _attn_bwd(qkvs[g], rqks[g], dos[g], lses[g], deltas[g], p["at_q_norm"][g * H:(g + 1) * H],
                                  p["at_k_norm"][g * H:(g + 1) * H], grp=g, name=f"l1_attnbwd{g}")
        dqg.append(a)
        dkg.append(b_)
        d_wqkvT.append(_mm(dqkv_g, h1s[g], mode="tn", out_dtype=F32, name=f"l1_dwqkv{g}", tk=2048))
        dh1s.append(_mm(dqkv_g, p["w_qkvT"], mode="nn", out_dtype=F32, name=f"l1_dh{g}", b_off=g * 3 * D, b_len=3 * D))
    dx2, dx2b, dgm1, _ = _rms_bwd(x2, p["norm_mix"][1:2], dh1s[0:1], dx3, name="l1_rmsbwd",
                                  dh_subs=[(dh1s[g].reshape(dils[g], S // dils[g], D), dils[g]) for g in range(1, len(dils))])
    ex, finish = (None, None) if early_reduce is None else early_reduce(
        dict(w_qkvT=jnp.concatenate(d_wqkvT, axis=0), w_o=d_wo, w_upT=gf1["w_upT"], w_down=gf1["w_down"]))
    dx1, dx1b, cs1, gf0, carried = _ffn_backward(x1, p["norm_ffn"][0:1], p["w_upT"][0], p["w_down"][0], p["ff_dw"][0],
                                                 p["ff_dw_b"][0:1], sv0, dx2, dx2b, 0, exchange=ex)
    reduced = [] if finish is None else [finish(carried)]
    ex, finish = (None, None) if early_reduce is None else early_reduce(dict(w_upT=gf0["w_upT"], w_down=gf0["w_down"]))
    ds = _mm(dx1b, p["w_out"], mode="nt", out_dtype=F32, name="l0_ds")
    d_wout = _mm(s, dx1b, mode="tn", out_dtype=F32, name="l0_dwout", tk=2048)
    dc, d_lng, d_lnb = _cm_ln_bwd(c, ds, p["cm_ln_g"], p["cm_ln_b"], name="l0_lnbwd")
    du, d_cmdw, d_cmdwb, d_bin, *carried = _cm_conv_bwd(dc, u, p["cm_dw"], name="l0_convbwd", exchange=ex)
    if finish is not None:
        reduced.append(finish(carried))
    grad_x, _, dgm0, _ = _mm(du, p["w_inT"], mode="nn", out_dtype=F32, name="l0_dh", tm=512,
                             post=_post_rms_bwd(x, p["norm_mix"][0:1], dx1), keep_main=False)
    d_winT = _mm(du, h0, mode="tn", out_dtype=F32, name="l0_dwin", tk=2048)
    grads = dict(
        norm_mix=jnp.concatenate([dgm0, dgm1], axis=0),
        norm_ffn=jnp.concatenate([gf0["norm"], gf1["norm"]], axis=0),
        w_inT=d_winT, cm_b_in=d_bin, cm_dw=d_cmdw[0:CONV_KERNEL], cm_dw_b=d_cmdwb, cm_ln_g=d_lng, cm_ln_b=d_lnb,
        w_out=d_wout, cm_b_out=cs1,
        w_qkvT=jnp.concatenate(d_wqkvT, axis=0), at_q_norm=jnp.concatenate(dqg, axis=0), at_k_norm=jnp.concatenate(dkg, axis=0),
        w_o=d_wo,
        w_upT=[gf0["w_upT"], gf1["w_upT"]], w_down=[gf0["w_down"], gf1["w_down"]],
        ff_dw=jnp.stack([gf0["dw"], gf1["dw"]]), ff_dw_b=jnp.concatenate([gf0["dw_b"], gf1["dw_b"]], axis=0),
    )
    return loss, grad_x, grads, reduced


_BIG = ("cm_w_in", "cm_w_out", "at_w_qkv", "at_w_out", "ff_w_up", "ff_w_down")
_TRANSPOSED = ("cm_w_in", "at_w_qkv", "ff_w_up")
_SMALL = ("norm_mix", "norm_ffn", "cm_b_in", "cm_dw_b", "cm_ln_g", "cm_ln_b", "cm_b_out", "at_q_norm", "at_k_norm",
          "ff_dw_b", "cm_dw", "ff_dw")
_SMALL_SHARDED = ("cm_dw", "ff_dw")
_ORDER = ("norm_mix", "norm_ffn", "cm_w_in", "cm_b_in", "cm_dw", "cm_dw_b", "cm_ln_g", "cm_ln_b", "cm_w_out", "cm_b_out",
          "at_w_qkv", "at_q_norm", "at_k_norm", "at_w_out", "ff_w_up", "ff_dw", "ff_dw_b", "ff_w_down")


_UNITS = (("cm_w_in", 0), ("cm_w_out", 0), ("ff_w_up", 0), ("ff_w_down", 0),
          ("at_w_qkv", 0), ("at_w_out", 0), ("ff_w_up", 1), ("ff_w_down", 1))
_N_FIRST = 2
_N_LAYER0 = 4


def _unit_rows(t, n, l):
    return t[n].shape[2] if n in _TRANSPOSED else t[n].shape[1]


def _big_rows(t, units=_UNITS):
    return _pack_rows([t[n][l].T if n in _TRANSPOSED else t[n][l] for n, l in units])


def _big_unrows(packed, like):
    mats, off = {}, 0
    for n, l in _UNITS:
        rows = _unit_rows(like, n, l)
        m = packed[off:off + rows]
        off += rows
        mats[(n, l)] = m.T if n in _TRANSPOSED else m
    return {n: jnp.stack([mats[(n, l)] for l in range(like[n].shape[0])]) for n in _BIG}


def kernel(x, norm_mix, norm_ffn, cm_w_in, cm_b_in, cm_dw, cm_dw_b, cm_ln_g, cm_ln_b, cm_w_out, cm_b_out, at_w_qkv, at_q_norm, at_k_norm, at_w_out, ff_w_up, ff_dw, ff_dw_b, ff_w_down, loss_target, m_norm_mix, m_norm_ffn, m_cm_w_in, m_cm_b_in, m_cm_dw, m_cm_dw_b, m_cm_ln_g, m_cm_ln_b, m_cm_w_out, m_cm_b_out, m_at_w_qkv, m_at_q_norm, m_at_k_norm, m_at_w_out, m_ff_w_up, m_ff_dw, m_ff_dw_b, m_ff_w_down, v_norm_mix, v_norm_ffn, v_cm_w_in, v_cm_b_in, v_cm_dw, v_cm_dw_b, v_cm_ln_g, v_cm_ln_b, v_cm_w_out, v_cm_b_out, v_at_w_qkv, v_at_q_norm, v_at_k_norm, v_at_w_out, v_ff_w_up, v_ff_dw, v_ff_dw_b, v_ff_w_down):
    w = dict(norm_mix=norm_mix, norm_ffn=norm_ffn, cm_w_in=cm_w_in, cm_b_in=cm_b_in, cm_dw=cm_dw, cm_dw_b=cm_dw_b, cm_ln_g=cm_ln_g,
             cm_ln_b=cm_ln_b, cm_w_out=cm_w_out, cm_b_out=cm_b_out, at_w_qkv=at_w_qkv, at_q_norm=at_q_norm, at_k_norm=at_k_norm,
             at_w_out=at_w_out, ff_w_up=ff_w_up, ff_dw=ff_dw, ff_dw_b=ff_dw_b, ff_w_down=ff_w_down)
    m = dict(norm_mix=m_norm_mix, norm_ffn=m_norm_ffn, cm_w_in=m_cm_w_in, cm_b_in=m_cm_b_in, cm_dw=m_cm_dw, cm_dw_b=m_cm_dw_b,
             cm_ln_g=m_cm_ln_g, cm_ln_b=m_cm_ln_b, cm_w_out=m_cm_w_out, cm_b_out=m_cm_b_out, at_w_qkv=m_at_w_qkv,
             at_q_norm=m_at_q_norm, at_k_norm=m_at_k_norm, at_w_out=m_at_w_out, ff_w_up=m_ff_w_up, ff_dw=m_ff_dw,
             ff_dw_b=m_ff_dw_b, ff_w_down=m_ff_w_down)
    v = dict(norm_mix=v_norm_mix, norm_ffn=v_norm_ffn, cm_w_in=v_cm_w_in, cm_b_in=v_cm_b_in, cm_dw=v_cm_dw, cm_dw_b=v_cm_dw_b,
             cm_ln_g=v_cm_ln_g, cm_ln_b=v_cm_ln_b, cm_w_out=v_cm_w_out, cm_b_out=v_cm_b_out, at_w_qkv=v_at_w_qkv,
             at_q_norm=v_at_q_norm, at_k_norm=v_at_k_norm, at_w_out=v_at_w_out, ff_w_up=v_ff_w_up, ff_dw=v_ff_dw,
             ff_dw_b=v_ff_dw_b, ff_w_down=v_ff_w_down)
    S, D = x.shape[1], x.shape[2]
    F2 = ff_dw_b.shape[1]
    H3 = at_q_norm.shape[1]
    me = 4 * lax.axis_index("x") + 2 * lax.axis_index("y") + lax.axis_index("c")

    ix, iy, ic = lax.axis_index("x"), lax.axis_index("y"), lax.axis_index("c")
    chip = 2 * ix + iy
    core = jnp.stack([ic]).astype(jnp.int32)
    w_rows = _big_rows(w)
    unit_rows = [_unit_rows(w, n, l) for n, l in _UNITS]
    n_first = sum(unit_rows[:_N_FIRST])
    n_layer0 = sum(unit_rows[:_N_LAYER0])
    w_wire = w_rows.astype(MXU_DTYPE)

    def unpack(gathered, units, rows):
        full, off = {}, 0
        for (n, l), r in zip(units, rows):
            full[(n, l)] = gathered[:, off:off + r, :].reshape(N_DEV * r, D)
            off += r
        out = {}
        if ("cm_w_in", 0) in full:
            out.update(w_inT=full[("cm_w_in", 0)], w_out=full[("cm_w_out", 0)])
        if ("at_w_qkv", 0) in full:
            out.update(w_qkvT=full[("at_w_qkv", 0)], w_o=full[("at_w_out", 0)],
                       w_upT=[_interleave_rows(full[("ff_w_up", l)]) for l in range(2)],
                       w_down=[full[("ff_w_down", l)] for l in range(2)])
        return out

    first = _all_gather(w_wire[:n_first], name="gather_first")
    late_weights = (_gather_exchange(w_wire[n_first:]),
                    lambda carried: unpack(carried[0], _UNITS[_N_FIRST:], unit_rows[_N_FIRST:]))
    small_sh = _flat_pack([cm_dw, ff_dw], D)
    small_g = _all_gather(small_sh, name="gather_small")
    cm_dw_full = jnp.concatenate([_flat_unpack(small_g[j], [cm_dw.shape, ff_dw.shape])[0][0] for j in range(N_DEV)], axis=-1)
    ff_dw_full = jnp.concatenate([_flat_unpack(small_g[j], [cm_dw.shape, ff_dw.shape])[1] for j in range(N_DEV)], axis=-1)

    p = dict(
        norm_mix=norm_mix, norm_ffn=norm_ffn, cm_b_in=cm_b_in, cm_dw=cm_dw_full, cm_dw_b=cm_dw_b, cm_ln_g=cm_ln_g, cm_ln_b=cm_ln_b,
        cm_b_out=cm_b_out, at_q_norm=at_q_norm[0], at_k_norm=at_k_norm[0],
        ff_dw=jnp.stack([_interleave_cols(ff_dw_full[l]) for l in range(ff_dw_full.shape[0])]),
        ff_dw_b=_interleave_cols(ff_dw_b),
        **unpack(first, _UNITS[:_N_FIRST], unit_rows[:_N_FIRST]),
    )

    def pack(pieces):
        return jnp.concatenate([t.reshape(N_DEV, t.shape[0] // N_DEV, D) for t in pieces], axis=1)

    def reduce_start(pieces, tag):
        g_rows = pack(pieces)
        sib = _rs_sibling(g_rows.astype(WIRE_DTYPE), name=f"reduce{tag}_sibling")
        return g_rows, sib, _chip_partials(g_rows, sib, core, name=f"reduce{tag}_add")

    def early_reduce(gd):
        ffn = [_deinterleave_rows(gd["w_upT"]), gd["w_down"]]
        tag, pieces = (2, [gd["w_qkvT"], gd["w_o"]] + ffn) if "w_qkvT" in gd else (1, ffn)
        g_rows, sib, part = reduce_start(pieces, tag)
        return _chips_exchange(part), lambda carried: (g_rows, sib, carried[0])

    loss8, grad_x, g, (reduced2, reduced1) = _local_step(x[0], loss_target[0], p, late_weights, early_reduce)
    loss = lax.psum(loss8[0, 0], ("x", "y", "c"))
    g_rows0, sib0, part0 = reduce_start([g["w_inT"], g["w_out"]], 0)
    reduced0 = (g_rows0, sib0, _rs_chips(part0, name="reduce0_chips"))
    slots = jnp.stack([me, chip, 0 * me, 0 * me + 1, 0 * me + 2]).astype(jnp.int32)
    m_rows, v_rows = _big_rows(m), _big_rows(v)
    updated = []
    for tag, (g_rows, sib, recv), rows in ((0, reduced0, slice(0, n_first)), (1, reduced1, slice(n_first, n_layer0)),
                                           (2, reduced2, slice(n_layer0, None))):
        updated.append(_adamw(w_rows[rows], m_rows[rows], v_rows[rows], [g_rows, sib, recv, recv, recv], slots, name=f"adamw_big{tag}"))
    big = [_big_unrows(jnp.concatenate([u[k] for u in updated], axis=0), w) for k in range(4)]

    g_small = dict(g)
    g_small["cm_b_in"] = g["cm_b_in"]
    g_small["at_q_norm"] = g["at_q_norm"][None]
    g_small["at_k_norm"] = g["at_k_norm"][None]
    g_small["ff_dw_b"] = _deinterleave_cols(g["ff_dw_b"])
    g_small["cm_dw"] = g["cm_dw"][None]
    g_small["ff_dw"] = jnp.stack([_deinterleave_cols(g["ff_dw"][l]) for l in range(g["ff_dw"].shape[0])])
    small_shapes = [g_small[n].shape for n in _SMALL]
    gs_parts = _all_gather(_flat_pack([g_small[n] for n in _SMALL], D), name="gather_small_grads")

    def embed(t, n):
        if n not in _SMALL_SHARDED:
            return t
        full_shape = t.shape[:-1] + (t.shape[-1] * N_DEV,)
        return lax.dynamic_update_slice_in_dim(jnp.zeros(full_shape, F32), t, me * t.shape[-1], axis=t.ndim - 1)

    packs = [_flat_pack([embed(tree[n], n) for n in _SMALL], D) for tree in (w, m, v)]
    gs, ds_, ms, vs = _adamw(packs[0], packs[1], packs[2], [gs_parts] * N_DEV, jnp.arange(N_DEV, dtype=jnp.int32),
                             name="adamw_small")
    small = []
    for t in (gs, ds_, ms, vs):
        un = dict(zip(_SMALL, _flat_unpack(t, small_shapes)))
        for n in _SMALL_SHARDED:
            width = w[n].shape[-1]
            un[n] = lax.dynamic_slice_in_dim(un[n], me * width, width, axis=un[n].ndim - 1)
        small.append({n: un[n].reshape(w[n].shape) for n in _SMALL})

    outs = [loss, grad_x[None]]
    for k in range(4):
        for n in _ORDER:
            outs.append(big[k][n] if n in _BIG else small[k][n])
    return tuple(outs)
```

```python
import functools

import jax
import jax.numpy as jnp
import numpy as np
from jax import lax
from jax.experimental import pallas as pl
from jax.experimental.pallas import tpu as pltpu

F32 = jnp.float32
MXU_DTYPE = jnp.bfloat16
WIRE_DTYPE = jnp.bfloat16
EPS = 1e-6
NEG = -1e30
HEAD_DIM = 128
BLOCK = 128
DILATED_GROUPS = ((128, 1), (512, 4), (2048, 16))
ALIBI_MAX = 8.0
CONV_KERNEL = 31
CONV_HALO = 32
CONV_ROWS = 64
FFN_KERNEL = 3
FFN_HALO = 16
FFN_ROWS = 64
ADAM_LR, ADAM_B1, ADAM_B2, ADAM_EPS, ADAM_WD, ADAM_STEP = 0.001, 0.9, 0.999, 1e-08, 0.01, 10
V7X_VMEM_BYTES = 64 * 1024 * 1024
VMEM_LIMIT = V7X_VMEM_BYTES * 3 // 4
N_DEV = 8
MESH = pl.DeviceIdType.MESH


def _pick(n, target, align):
    if n <= target:
        return n
    best = None
    for t in range(align, target + 1, align):
        if n % t == 0:
            best = t
    assert best is not None, (n, target, align)
    return best


def _params(*sem, vmem=VMEM_LIMIT):
    return pltpu.CompilerParams(dimension_semantics=sem, vmem_limit_bytes=vmem)


def _sigmoid(x):
    return 1.0 / (1.0 + jnp.exp(-x))


_DIMS = {"nn": ((1,), (0,)), "nt": ((1,), (1,)), "tn": ((0,), (0,))}


def _mm(a, b, *, mode, out_dtype, name, tm=1024, tn=1024, tk=None, bias=None, residual=None, b_off=0, b_len=None,
        post=None, keep_main=True):
    if mode == "tn":
        K, M = a.shape
    else:
        M, K = a.shape
    if mode == "nt":
        N = b.shape[0] if b_len is None else b_len
    else:
        N = b.shape[1]
    if b_len is not None:
        assert mode == "nt" or (mode == "nn" and K == b_len)
    tm = _pick(M, tm, 128 if mode == "tn" else 16)
    tn = _pick(N, tn, 128)
    tk = K if tk is None else _pick(K, tk, 128 if mode != "tn" else 16)
    nk = K // tk
    unit = tn if mode == "nt" else tk
    assert b_off % unit == 0
    kb0 = b_off // unit
    if mode == "tn":
        a_spec = pl.BlockSpec((tk, tm), lambda i, j, k: (k, i))
    else:
        a_spec = pl.BlockSpec((tm, tk), lambda i, j, k: (i, k))
    if mode == "nt":
        b_spec = pl.BlockSpec((tn, tk), lambda i, j, k: (j + kb0, k))
    else:
        b_spec = pl.BlockSpec((tk, tn), lambda i, j, k: (k + kb0, j))
    in_specs = [a_spec, b_spec]
    args = [a, b]
    if bias is not None:
        in_specs.append(pl.BlockSpec((1, tn), lambda i, j, k: (0, j)))
        args.append(bias)
    if residual is not None:
        in_specs.append(pl.BlockSpec((tm, tn), lambda i, j, k: (i, j)))
        args.append(residual)
    has_bias, has_res = bias is not None, residual is not None
    kinds = {"tile": ((tm, tn), lambda i, j, k: (i, j)), "row": ((1, tn), lambda i, j, k: (0, j)),
             "lanes": ((8, 128), lambda i, j, k: (0, 0))}
    post_in = [] if post is None else post.ins
    post_out = [] if post is None else post.outs
    if post is not None:
        assert tn == N
        for arr, kind in post_in:
            in_specs.append(pl.BlockSpec(*kinds[kind]))
            args.append(arr)
    out_specs = [pl.BlockSpec((tm, tn), lambda i, j, k: (i, j))] if keep_main else []
    out_shape = [jax.ShapeDtypeStruct((M, N), out_dtype)] if keep_main else []
    for kind, dtype in post_out:
        out_specs.append(pl.BlockSpec(*kinds[kind]))
        out_shape.append(jax.ShapeDtypeStruct({"tile": (M, N), "row": (1, N), "lanes": (8, 128)}[kind], dtype))
    accumulates = any(kind != "tile" for kind, _ in post_out)

    def body(*refs):
        a_ref, b_ref = refs[0], refs[1]
        pos = 2
        bias_ref = res_ref = None
        if has_bias:
            bias_ref = refs[pos]
            pos += 1
        if has_res:
            res_ref = refs[pos]
            pos += 1
        pin_refs = refs[pos:pos + len(post_in)]
        pos += len(post_in)
        o_ref = refs[pos] if keep_main else None
        pos += 1 if keep_main else 0
        pout_refs = refs[pos:pos + len(post_out)]
        pos += len(post_out)
        acc_ref = refs[pos] if nk > 1 else None

        def finish(acc):
            if has_bias:
                acc = acc + bias_ref[...]
            if has_res:
                acc = acc + res_ref[...]
            if keep_main:
                o_ref[...] = acc.astype(o_ref.dtype)
            if post is not None:
                post.fn(acc, pin_refs, pout_refs, pl.program_id(0) == 0)

        part = lax.dot_general(a_ref[...].astype(MXU_DTYPE), b_ref[...].astype(MXU_DTYPE), (_DIMS[mode], ((), ())),
                               preferred_element_type=F32)
        if nk == 1:
            finish(part)
        else:
            k = pl.program_id(2)

            @pl.when(k == 0)
            def _():
                acc_ref[...] = part

            @pl.when(jnp.logical_and(k > 0, k < nk - 1))
            def _():
                acc_ref[...] += part

            @pl.when(k == nk - 1)
            def _():
                finish(acc_ref[...] + part)

    outs = pl.pallas_call(
        body, name=name, grid=(M // tm, N // tn, nk), in_specs=in_specs, out_specs=out_specs, out_shape=out_shape,
        scratch_shapes=[pltpu.VMEM((tm, tn), F32)] if nk > 1 else [],
        compiler_params=_params("arbitrary" if accumulates else "parallel", "parallel", "arbitrary"),
    )(*args)
    return outs[0] if post is None else outs


def _mm_post(a, b, post, *, name, tm=512, rows=256):
    M, K = a.shape
    N = b.shape[1]
    tm = _pick(M, tm, 16)
    rows = _pick(tm, rows, 16)
    kinds = {"tile": ((tm, N), lambda i: (i, 0)), "row": ((1, N), lambda i: (0, 0)), "lanes": ((8, 128), lambda i: (0, 0))}
    n_in = len(post.ins)

    def body(a_ref, b_ref, *rest):
        first = pl.program_id(0) == 0

        def product(c):
            return jnp.dot(a_ref[c * rows:(c + 1) * rows, :].astype(MXU_DTYPE), b_ref[...].astype(MXU_DTYPE),
                           preferred_element_type=F32)

        def chunk(refs, specs, c):
            return [r.at[pl.ds(c * rows, rows), :] if kind == "tile" else r for r, kind in zip(refs, specs)]

        acc = product(0)
        for c in range(tm // rows):
            nxt = product(c + 1) if c + 1 < tm // rows else None
            post.fn(acc, chunk(rest[:n_in], [k for _, k in post.ins], c), chunk(rest[n_in:], [k for k, _ in post.outs], c),
                    jnp.logical_and(first, c == 0))
            acc = nxt

    return pl.pallas_call(
        body, name=name, grid=(M // tm,),
        in_specs=[pl.BlockSpec((tm, K), lambda i: (i, 0)), pl.BlockSpec((K, N), lambda i: (0, 0), pipeline_mode=pl.Buffered(1))]
        + [pl.BlockSpec(*kinds[kind]) for _, kind in post.ins],
        out_specs=[pl.BlockSpec(*kinds[kind]) for kind, _ in post.outs],
        out_shape=[jax.ShapeDtypeStruct({"tile": (M, N), "row": (1, N), "lanes": (8, 128)}[kind], dtype) for kind, dtype in post.outs],
        compiler_params=_params("arbitrary", vmem=V7X_VMEM_BYTES * 7 // 8),
    )(a, b, *[arr for arr, _ in post.ins])


class _Post:
    def __init__(self, ins, outs, fn):
        self.ins, self.outs, self.fn = ins, outs, fn


def _accumulate(ref, value, first):
    @pl.when(first)
    def _():
        ref[...] = value

    @pl.when(jnp.logical_not(first))
    def _():
        ref[...] += value


def _post_rms(g):
    def fn(acc, ins, outs, first):
        r = lax.rsqrt(jnp.mean(acc * acc, axis=-1, keepdims=True) + EPS)
        outs[0][...] = (acc * r * ins[0][...]).astype(outs[0].dtype)

    return _Post([(g, "row")], [("tile", MXU_DTYPE)], fn)


def _post_loss(target):
    def fn(acc, ins, outs, first):
        e = acc - ins[0][...]
        dy = e * (1.0 / acc.shape[-1])
        outs[0][...] = dy
        outs[1][...] = dy.astype(outs[1].dtype)
        part = jnp.sum(jnp.sum(e * e, axis=0, keepdims=True), axis=1, keepdims=True) * (0.5 / acc.shape[-1])
        _accumulate(outs[2], jnp.broadcast_to(part, outs[2].shape), first)

    return _Post([(target, "tile")], [("tile", F32), ("tile", MXU_DTYPE), ("lanes", F32)], fn)


def _post_rms_bwd(x, g, dres):
    def fn(acc, ins, outs, first):
        xv = ins[0][...]
        r = lax.rsqrt(jnp.mean(xv * xv, axis=-1, keepdims=True) + EPS)
        xh = xv * r
        gy = acc * ins[1][...]
        dx = r * (gy - xh * jnp.mean(gy * xh, axis=-1, keepdims=True)) + ins[2][...]
        outs[0][...] = dx
        outs[1][...] = dx.astype(outs[1].dtype)
        _accumulate(outs[2], jnp.sum(acc * xh, axis=0, keepdims=True), first)
        _accumulate(outs[3], jnp.sum(dx, axis=0, keepdims=True), first)

    return _Post([(x, "tile"), (g, "row"), (dres, "tile")], [("tile", F32), ("tile", MXU_DTYPE), ("row", F32), ("row", F32)], fn)


SUB_TILE = 512


def _sub_spec(dil, ts, cols):
    return pl.BlockSpec((dil, ts // dil, cols), lambda i: (0, i, 0))


def _tok_to_sub(tok_ref, dst_ref, dil):
    nc, ts, _ = tok_ref.shape
    for c in range(nc):
        for r in range(dil):
            dst_ref[r, :, c * 128:(c + 1) * 128] = tok_ref.at[c][pl.ds(r, ts // dil, stride=dil), :].astype(dst_ref.dtype)


def _sub_to_tok(src_ref, tok_ref, dil):
    nc, ts, _ = tok_ref.shape
    for c in range(nc):
        for r in range(dil):
            tok_ref.at[c][pl.ds(r, ts // dil, stride=dil), :] = src_ref[r, :, c * 128:(c + 1) * 128].astype(F32)


def _rms_fwd(x, g, *, name, subs=()):
    S, D = x.shape
    ts = _pick(S, SUB_TILE, 16 * max(subs, default=1))
    NC = D // 128

    def body(x_ref, g_ref, h_ref, *rest):
        xv = x_ref[...]
        r = lax.rsqrt(jnp.mean(xv * xv, axis=-1, keepdims=True) + EPS)
        h = xv * r * g_ref[...]
        h_ref[...] = h.astype(h_ref.dtype)
        if subs:
            tok_ref = rest[-1]
            for c in range(NC):
                tok_ref[c] = h[:, c * 128:(c + 1) * 128]
            for dil, dst_ref in zip(subs, rest):
                _tok_to_sub(tok_ref, dst_ref, dil)

    row = pl.BlockSpec((ts, D), lambda i: (i, 0))
    outs = pl.pallas_call(
        body, name=name, grid=(S // ts,),
        in_specs=[row, pl.BlockSpec((1, D), lambda i: (0, 0))],
        out_specs=[row] + [_sub_spec(dil, ts, D) for dil in subs],
        out_shape=[jax.ShapeDtypeStruct((S, D), MXU_DTYPE)] + [jax.ShapeDtypeStruct((dil, S // dil, D), MXU_DTYPE) for dil in subs],
        scratch_shapes=[pltpu.VMEM((NC, ts, 128), F32)] if subs else [],
        compiler_params=_params("parallel"),
    )(x, g)
    return outs if subs else outs[0]


def _rms_bwd(x, g, dhs, dres, *, name, dh_subs=()):
    S, D = x.shape
    ts = _pick(S, SUB_TILE, 16 * max([dil for _, dil in dh_subs], default=1))
    n_dh, n_sub = len(dhs), len(dh_subs)
    NC = D // 128

    def body(*refs):
        x_ref, g_ref = refs[0], refs[1]
        dh_refs = refs[2:2 + n_dh]
        sub_refs = refs[2 + n_dh:2 + n_dh + n_sub]
        dres_ref, dx_ref, dxb_ref, dg_ref, cs_ref = refs[2 + n_dh + n_sub:7 + n_dh + n_sub]
        i = pl.program_id(0)
        xv = x_ref[...]
        r = lax.rsqrt(jnp.mean(xv * xv, axis=-1, keepdims=True) + EPS)
        xh = xv * r
        dhv = dh_refs[0][...].astype(F32)
        for t in dh_refs[1:]:
            dhv = dhv + t[...].astype(F32)
        for (_, dil), sub_ref in zip(dh_subs, sub_refs):
            tok_ref = refs[-1]
            _sub_to_tok(sub_ref, tok_ref, dil)
            dhv = dhv + jnp.concatenate([tok_ref[c] for c in range(NC)], axis=1)
        gy = dhv * g_ref[...]
        dx = r * (gy - xh * jnp.mean(gy * xh, axis=-1, keepdims=True)) + dres_ref[...]
        dx_ref[...] = dx
        dxb_ref[...] = dx.astype(dxb_ref.dtype)
        dg = jnp.sum(dhv * xh, axis=0, keepdims=True)
        cs = jnp.sum(dx, axis=0, keepdims=True)

        @pl.when(i == 0)
        def _():
            dg_ref[...] = dg
            cs_ref[...] = cs

        @pl.when(i > 0)
        def _():
            dg_ref[...] += dg
            cs_ref[...] += cs

    row = pl.BlockSpec((ts, D), lambda i: (i, 0))
    vec = pl.BlockSpec((1, D), lambda i: (0, 0))
    return pl.pallas_call(
        body, name=name, grid=(S // ts,),
        in_specs=[row, vec] + [row] * n_dh + [_sub_spec(dil, ts, D) for _, dil in dh_subs] + [row],
        out_specs=[row, row, vec, vec],
        out_shape=[jax.ShapeDtypeStruct((S, D), F32), jax.ShapeDtypeStruct((S, D), MXU_DTYPE),
                   jax.ShapeDtypeStruct((1, D), F32), jax.ShapeDtypeStruct((1, D), F32)],
        scratch_shapes=[pltpu.VMEM((NC, ts, 128), F32)] if n_sub else [],
        compiler_params=_params("arbitrary"),
    )(x, g, *dhs, *[a for a, _ in dh_subs], dres)


def _conv_phases(ph_ref, ts):
    n = ts + CONV_HALO - 8
    for b in range(1, 8):
        ph_ref[b, 0:n, :] = ph_ref[0, pl.ds(b, n), :]


def _phase_taps(base, step=1):
    groups = {}
    for k in range(CONV_KERNEL):
        a, b = divmod(base + step * k, 8)
        groups.setdefault(b, []).append((a, k))
    out = []
    for b in sorted(groups):
        ak = sorted(groups[b])
        assert [a for a, _ in ak] == list(range(ak[0][0], ak[0][0] + len(ak)))
        out.append((b, ak[0][0], [k for _, k in ak]))
    return out


def _cm_fwd(u, dw, dw_b, ln_g, ln_b, *, name, exchange=None):
    S, D2 = u.shape
    D = D2 // 2
    ts = _pick(S, 256, CONV_HALO)
    hb = ts // CONV_HALO
    ex_in, ex_out, ex_scr = ([], [], []) if exchange is None else (exchange.operands, exchange.out_shapes, exchange.scratch)

    def body(u_ref, up_ref, dw_ref, dwb_ref, g_ref, b_ref, *rest):
        xi = rest[:len(ex_in)]
        c_ref, s_ref = rest[len(ex_in):len(ex_in) + 2]
        xo = rest[len(ex_in) + 2:len(ex_in) + 2 + len(ex_out)]
        ext_ref = rest[len(ex_in) + 2 + len(ex_out)]
        i = pl.program_id(0)
        if exchange is not None:
            exchange.emit(i, S // ts, xi, xo, rest[len(ex_in) + 3 + len(ex_out):])
        prev = up_ref[:, :D] * _sigmoid(up_ref[:, D:])
        ext_ref[0:CONV_HALO, :] = jnp.where(i > 0, prev, 0.0)
        ext_ref[CONV_HALO:CONV_HALO + ts, :] = u_ref[:, :D] * _sigmoid(u_ref[:, D:])
        for cc in range(D // 128):
            sl = slice(cc * 128, (cc + 1) * 128)
            acc = jnp.zeros((ts, 128), F32) + dwb_ref[:, sl]
            for k in range(CONV_KERNEL):
                acc = acc + dw_ref[k:k + 1, sl] * ext_ref[pl.ds(CONV_HALO - (CONV_KERNEL - 1) + k, ts), sl]
            c_ref[:, sl] = acc
        c = c_ref[...]
        mu = jnp.mean(c, axis=-1, keepdims=True)
        xc = c - mu
        rstd = lax.rsqrt(jnp.mean(xc * xc, axis=-1, keepdims=True) + EPS)
        y = xc * rstd * g_ref[...] + b_ref[...]
        s_ref[...] = (y * _sigmoid(y)).astype(s_ref.dtype)

    vec = pl.BlockSpec((1, D), lambda i: (0, 0))
    return pl.pallas_call(
        body, name=name, grid=(S // ts,),
        in_specs=[pl.BlockSpec((ts, D2), lambda i: (i, 0)),
                  pl.BlockSpec((CONV_HALO, D2), lambda i: (jnp.maximum(i * hb - 1, 0), 0)),
                  pl.BlockSpec((CONV_KERNEL, D), lambda i: (0, 0)), vec, vec, vec] + [_ANY] * len(ex_in),
        out_specs=[pl.BlockSpec((ts, D), lambda i: (i, 0)), pl.BlockSpec((ts, D), lambda i: (i, 0))] + [_ANY] * len(ex_out),
        out_shape=[jax.ShapeDtypeStruct((S, D), F32), jax.ShapeDtypeStruct((S, D), MXU_DTYPE)] + list(ex_out),
        scratch_shapes=[pltpu.VMEM((ts + CONV_HALO, D), F32)] + list(ex_scr),
        compiler_params=_params("parallel" if exchange is None else "arbitrary"),
    )(u, u, dw, dw_b, ln_g, ln_b, *ex_in)


def _cm_ln_bwd(c, ds, ln_g, ln_b, *, name):
    S, D = c.shape
    ts = _pick(S, 512, 16)

    def body(c_ref, ds_ref, g_ref, b_ref, dc_ref, dg_ref, db_ref):
        i = pl.program_id(0)
        cv = c_ref[...]
        mu = jnp.mean(cv, axis=-1, keepdims=True)
        xc = cv - mu
        rstd = lax.rsqrt(jnp.mean(xc * xc, axis=-1, keepdims=True) + EPS)
        xh = xc * rstd
        y = xh * g_ref[...] + b_ref[...]
        sg = _sigmoid(y)
        dy = ds_ref[...].astype(F32) * (sg * (1.0 + y * (1.0 - sg)))
        gy = dy * g_ref[...]
        dc_ref[...] = rstd * (gy - jnp.mean(gy, axis=-1, keepdims=True) - xh * jnp.mean(gy * xh, axis=-1, keepdims=True))
        dg = jnp.sum(dy * xh, axis=0, keepdims=True)
        db = jnp.sum(dy, axis=0, keepdims=True)

        @pl.when(i == 0)
        def _():
            dg_ref[...] = dg
            db_ref[...] = db

        @pl.when(i > 0)
        def _():
            dg_ref[...] += dg
            db_ref[...] += db

    row = pl.BlockSpec((ts, D), lambda i: (i, 0))
    vec = pl.BlockSpec((1, D), lambda i: (0, 0))
    return pl.pallas_call(
        body, name=name, grid=(S // ts,), in_specs=[row, row, vec, vec], out_specs=[row, vec, vec],
        out_shape=[jax.ShapeDtypeStruct((S, D), F32), jax.ShapeDtypeStruct((1, D), F32), jax.ShapeDtypeStruct((1, D), F32)],
        compiler_params=_params("arbitrary"),
    )(c, ds, ln_g, ln_b)


def _cm_conv_bwd(dc, u, dw, *, name, exchange=None):
    ex_in, ex_out, ex_scr = ([], [], []) if exchange is None else (exchange.operands, exchange.out_shapes, exchange.scratch)
    S, D2 = u.shape
    D = D2 // 2
    ts = _pick(S, 256, CONV_HALO)
    hb = ts // CONV_HALO
    n_t = S // ts
    last_h = S // CONV_HALO - 1

    rc = _pick(ts, CONV_ROWS, 8)

    def fold8(v):
        out = v[0:8]
        for j in range(1, v.shape[0] // 8):
            out = out + v[8 * j:8 * j + 8]
        return out

    def body(dc_ref, dcn_ref, u_ref, up_ref, dw_ref, *rest):
        xi = rest[:len(ex_in)]
        du_ref, ddw_ref, ddwb_ref, dbin_ref = rest[len(ex_in):len(ex_in) + 4]
        xo = rest[len(ex_in) + 4:len(ex_in) + 4 + len(ex_out)]
        dph_ref, gph_ref, dgl_ref = rest[len(ex_in) + 4 + len(ex_out):len(ex_in) + 7 + len(ex_out)]
        i = pl.program_id(0)
        if exchange is not None:
            exchange.emit(i, n_t, xi, xo, rest[len(ex_in) + 7 + len(ex_out):])
        dph_ref[0, 0:ts, :] = dc_ref[...]
        dph_ref[0, ts:ts + CONV_HALO, :] = jnp.where(i < n_t - 1, dcn_ref[...], 0.0)
        prev = up_ref[:, :D] * _sigmoid(up_ref[:, D:])
        gph_ref[0, 0:CONV_HALO, :] = jnp.where(i > 0, prev, 0.0)
        gph_ref[0, CONV_HALO:CONV_HALO + ts, :] = u_ref[:, :D] * _sigmoid(u_ref[:, D:])
        _conv_phases(dph_ref, ts)
        _conv_phases(gph_ref, ts)

        @pl.when(i == 0)
        def _():
            ddw_ref[...] = jnp.zeros_like(ddw_ref)
            ddwb_ref[...] = jnp.zeros_like(ddwb_ref)
            dbin_ref[...] = jnp.zeros_like(dbin_ref)

        for cc in range(D // 128):
            sl = slice(cc * 128, (cc + 1) * 128)
            sl2 = slice(D + cc * 128, D + (cc + 1) * 128)
            wk = [dw_ref[k:k + 1, sl] for k in range(CONV_KERNEL)]
            acc_a, acc_g = jnp.zeros((8, 128), F32), jnp.zeros((8, 128), F32)
            dgl = jnp.zeros((ts, 128), F32)
            for b, a0, taps in _phase_taps(CONV_KERNEL - 1, -1):
                for j, k in enumerate(taps):
                    dgl = dgl + wk[k] * dph_ref[b, 8 * (a0 + j):8 * (a0 + j) + ts, sl]
            dgl_ref[...] = dgl
            for r0 in range(0, ts, rc):
                dglu = dgl_ref[r0:r0 + rc, :]
                av = u_ref[r0:r0 + rc, sl]
                sg = _sigmoid(u_ref[r0:r0 + rc, sl2])
                da = dglu * sg
                dg = dglu * av * sg * (1.0 - sg)
                du_ref[r0:r0 + rc, sl] = da.astype(du_ref.dtype)
                du_ref[r0:r0 + rc, sl2] = dg.astype(du_ref.dtype)
                acc_a = acc_a + fold8(da)
                acc_g = acc_g + fold8(dg)
            dbin_ref[:, sl] += jnp.sum(acc_a, axis=0, keepdims=True)
            dbin_ref[:, sl2] += jnp.sum(acc_g, axis=0, keepdims=True)
            for gi, (b, a0, taps) in enumerate(_phase_taps(CONV_HALO - (CONV_KERNEL - 1))):
                accs = [jnp.zeros((8, 128), F32) for _ in taps]
                accb = jnp.zeros((8, 128), F32)
                for r0 in range(0, ts, rc):
                    dcc = dph_ref[0, r0:r0 + rc, sl]
                    win = gph_ref[b, 8 * a0 + r0:8 * (a0 + len(taps) - 1) + r0 + rc, sl]
                    for j in range(len(taps)):
                        accs[j] = accs[j] + fold8(dcc * win[8 * j:8 * j + rc])
                    if gi == 0:
                        accb = accb + fold8(dcc)
                for j, k in enumerate(taps):
                    ddw_ref[k:k + 1, sl] += jnp.sum(accs[j], axis=0, keepdims=True)
                if gi == 0:
                    ddwb_ref[:, sl] += jnp.sum(accb, axis=0, keepdims=True)

    return pl.pallas_call(
        body, name=name, grid=(n_t,),
        in_specs=[pl.BlockSpec((ts, D), lambda i: (i, 0)),
                  pl.BlockSpec((CONV_HALO, D), lambda i: (jnp.minimum((i + 1) * hb, last_h), 0)),
                  pl.BlockSpec((ts, D2), lambda i: (i, 0)),
                  pl.BlockSpec((CONV_HALO, D2), lambda i: (jnp.maximum(i * hb - 1, 0), 0)),
                  pl.BlockSpec((CONV_KERNEL, D), lambda i: (0, 0))] + [_ANY] * len(ex_in),
        out_specs=[pl.BlockSpec((ts, D2), lambda i: (i, 0)), pl.BlockSpec((CONV_HALO, D), lambda i: (0, 0)),
                   pl.BlockSpec((1, D), lambda i: (0, 0)), pl.BlockSpec((1, D2), lambda i: (0, 0))] + [_ANY] * len(ex_out),
        out_shape=[jax.ShapeDtypeStruct((S, D2), MXU_DTYPE), jax.ShapeDtypeStruct((CONV_HALO, D), F32),
                   jax.ShapeDtypeStruct((1, D), F32), jax.ShapeDtypeStruct((1, D2), F32)] + list(ex_out),
        scratch_shapes=[pltpu.VMEM((8, ts + CONV_HALO, D), F32), pltpu.VMEM((8, ts + CONV_HALO, D), F32),
                        pltpu.VMEM((ts, 128), F32)] + list(ex_scr),
        compiler_params=_params("arbitrary"),
    )(dc, dc, u, u, dw, *ex_in)


def _ffn_cols(F2):
    return _pick(F2, 1024, 256)


def _ffn_act_fwd(up, dw, dw_b, *, name):
    S, F2 = up.shape
    ts = _pick(S, 512, 16)
    tc = _ffn_cols(F2)
    hb = ts // FFN_HALO

    rc = _pick(ts, FFN_ROWS, 16)

    def body(u_ref, up_ref, w_ref, b_ref, a_ref, ext_ref):
        i = pl.program_id(1)
        ext_ref[0:FFN_HALO, :] = jnp.where(i > 0, up_ref[...].astype(F32), 0.0)
        ext_ref[FFN_HALO:FFN_HALO + ts, :] = u_ref[...].astype(F32)
        for q in range(tc // 256):
            sls = [slice(q * 256 + half * 128, q * 256 + half * 128 + 128) for half in range(2)]
            wk = [[w_ref[k:k + 1, sl] for k in range(FFN_KERNEL)] for sl in sls]
            bb = [b_ref[:, sl] for sl in sls]
            for r0 in range(0, ts, rc):
                gt, vl = [bb[h] + sum(wk[h][k] * ext_ref[pl.ds(FFN_HALO + r0 - 2 + k, rc), sls[h]] for k in range(FFN_KERNEL))
                          for h in range(2)]
                a_ref[r0:r0 + rc, q * 128:(q + 1) * 128] = (gt * _sigmoid(gt) * vl).astype(a_ref.dtype)

    return pl.pallas_call(
        body, name=name, grid=(F2 // tc, S // ts),
        in_specs=[pl.BlockSpec((ts, tc), lambda j, i: (i, j)),
                  pl.BlockSpec((FFN_HALO, tc), lambda j, i: (jnp.maximum(i * hb - 1, 0), j)),
                  pl.BlockSpec((FFN_KERNEL, tc), lambda j, i: (0, j)),
                  pl.BlockSpec((1, tc), lambda j, i: (0, j))],
        out_specs=pl.BlockSpec((ts, tc // 2), lambda j, i: (i, j)),
        out_shape=jax.ShapeDtypeStruct((S, F2 // 2), MXU_DTYPE),
        scratch_shapes=[pltpu.VMEM((ts + FFN_HALO, tc), F32)],
        compiler_params=_params("parallel", "parallel"),
    )(up, up, dw, dw_b)


def _ffn_act_bwd(up, dact, dw, dw_b, *, name, exchange=None):
    ex_in, ex_out, ex_scr = ([], [], []) if exchange is None else (exchange.operands, exchange.out_shapes, exchange.scratch)
    S, F2 = up.shape
    ts = _pick(S, 512, 16)
    tc = _ffn_cols(F2)
    hb = ts // FFN_HALO
    n_t = S // ts
    last_h = S // FFN_HALO - 1
    E = ts + FFN_HALO

    rc = _pick(ts, FFN_ROWS, 16)

    def fold8(v):
        out = v[0:8]
        for j in range(1, v.shape[0] // 8):
            out = out + v[8 * j:8 * j + 8]
        return out

    def body(u_ref, up_ref, un_ref, da_ref, dan_ref, w_ref, b_ref, *rest):
        xi = rest[:len(ex_in)]
        dup_ref, ddw_ref, ddb_ref = rest[len(ex_in):len(ex_in) + 3]
        xo = rest[len(ex_in) + 3:len(ex_in) + 3 + len(ex_out)]
        ue_ref, dcv_ref = rest[len(ex_in) + 3 + len(ex_out):len(ex_in) + 5 + len(ex_out)]
        i = pl.program_id(1)
        if exchange is not None:
            exchange.emit(pl.program_id(0) * n_t + i, (F2 // tc) * n_t, xi, xo, rest[len(ex_in) + 5 + len(ex_out):])
        ue_ref[0:FFN_HALO, :] = jnp.where(i > 0, up_ref[...].astype(F32), 0.0)
        ue_ref[FFN_HALO:FFN_HALO + ts, :] = u_ref[...].astype(F32)
        ue_ref[FFN_HALO + ts:FFN_HALO + ts + FFN_HALO, :] = jnp.where(i < n_t - 1, un_ref[...].astype(F32), 0.0)

        @pl.when(i == 0)
        def _():
            ddw_ref[...] = jnp.zeros_like(ddw_ref)
            ddb_ref[...] = jnp.zeros_like(ddb_ref)

        for q in range(tc // 256):
            sls = [slice(q * 256 + half * 128, q * 256 + half * 128 + 128) for half in range(2)]
            qs = slice(q * 128, (q + 1) * 128)
            wk = [[w_ref[k:k + 1, sl] for k in range(FFN_KERNEL)] for sl in sls]
            bb = [b_ref[:, sl] for sl in sls]
            acc = [[jnp.zeros((8, 128), F32) for _ in range(FFN_KERNEL)] for _ in range(2)]
            accb = [jnp.zeros((8, 128), F32) for _ in range(2)]
            for r0, rows in [(r, rc) for r in range(0, ts, rc)] + [(ts, FFN_HALO)]:
                xs = [[ue_ref[pl.ds(FFN_HALO + r0 - 2 + k, rows), sls[h]] for k in range(FFN_KERNEL)] for h in range(2)]
                gt, vl = [bb[h] + sum(wk[h][k] * xs[h][k] for k in range(FFN_KERNEL)) for h in range(2)]
                sg = _sigmoid(gt)
                if r0 < ts:
                    dae = da_ref[r0:r0 + rows, qs].astype(F32)
                else:
                    dae = jnp.where(i < n_t - 1, dan_ref[:, qs].astype(F32), 0.0)
                dcv = [dae * vl * (sg * (1.0 + gt * (1.0 - sg))), dae * (gt * sg)]
                for h in range(2):
                    dcv_ref[r0:r0 + rows, sls[h]] = dcv[h]
                    if r0 < ts:
                        for k in range(FFN_KERNEL):
                            acc[h][k] = acc[h][k] + fold8(dcv[h] * xs[h][k])
                        accb[h] = accb[h] + fold8(dcv[h])
            for r0 in range(0, ts, rc):
                for h in range(2):
                    dup = sum(wk[h][2 - j] * dcv_ref[pl.ds(r0 + j, rc), sls[h]] for j in range(FFN_KERNEL))
                    dup_ref[r0:r0 + rc, sls[h]] = dup.astype(dup_ref.dtype)
            for h in range(2):
                for k in range(FFN_KERNEL):
                    ddw_ref[k:k + 1, sls[h]] += jnp.sum(acc[h][k], axis=0, keepdims=True)
                ddb_ref[:, sls[h]] += jnp.sum(accb[h], axis=0, keepdims=True)

    return pl.pallas_call(
        body, name=name, grid=(F2 // tc, n_t),
        in_specs=[pl.BlockSpec((ts, tc), lambda j, i: (i, j)),
                  pl.BlockSpec((FFN_HALO, tc), lambda j, i: (jnp.maximum(i * hb - 1, 0), j)),
                  pl.BlockSpec((FFN_HALO, tc), lambda j, i: (jnp.minimum((i + 1) * hb, last_h), j)),
                  pl.BlockSpec((ts, tc // 2), lambda j, i: (i, j)),
                  pl.BlockSpec((FFN_HALO, tc // 2), lambda j, i: (jnp.minimum((i + 1) * hb, last_h), j)),
                  pl.BlockSpec((FFN_KERNEL, tc), lambda j, i: (0, j)),
                  pl.BlockSpec((1, tc), lambda j, i: (0, j))] + [_ANY] * len(ex_in),
        out_specs=[pl.BlockSpec((ts, tc), lambda j, i: (i, j)), pl.BlockSpec((FFN_HALO, tc), lambda j, i: (0, j)),
                   pl.BlockSpec((1, tc), lambda j, i: (0, j))] + [_ANY] * len(ex_out),
        out_shape=[jax.ShapeDtypeStruct((S, F2), MXU_DTYPE), jax.ShapeDtypeStruct((FFN_HALO, F2), F32),
                   jax.ShapeDtypeStruct((1, F2), F32)] + list(ex_out),
        scratch_shapes=[pltpu.VMEM((ts + 2 * FFN_HALO, tc), F32), pltpu.VMEM((E, tc), F32)] + list(ex_scr),
        compiler_params=_params("parallel" if exchange is None else "arbitrary", "arbitrary"),
    )(up, up, up, dact, dact, dw, dw_b, *ex_in)


def _slopes(n_heads_total):
    return np.asarray(2.0 ** (-ALIBI_MAX * (np.arange(n_heads_total, dtype=np.float32) + 1.0) / n_heads_total), np.float32)


def _qkv_proj(h, w_qkvT, *, grp, name):
    S, D = h.shape
    H = D // HEAD_DIM
    tm = _pick(S, 1024, 16)

    rows = _pick(tm, 256, 16)

    def body(a_ref, b_ref, o_ref, r_ref):
        j = pl.program_id(1)
        r_ref[...] = jnp.zeros_like(r_ref)

        def product(c):
            return lax.dot_general(a_ref[c * rows:(c + 1) * rows, :].astype(MXU_DTYPE), b_ref[...].astype(MXU_DTYPE),
                                   (_DIMS["nt"], ((), ())), preferred_element_type=F32)

        @pl.when(j < 2)
        def _():
            acc = product(0)
            for c in range(tm // rows):
                nxt = product(c + 1) if c + 1 < tm // rows else None
                rs = slice(c * rows, (c + 1) * rows)
                for hd in range(H):
                    hs = slice(hd * HEAD_DIM, (hd + 1) * HEAD_DIM)
                    xv = acc[:, hs]
                    r = lax.rsqrt(jnp.mean(xv * xv, axis=-1, keepdims=True) + EPS)
                    o_ref[rs, hs] = (xv * r).astype(o_ref.dtype)
                    r_ref[rs, hd:hd + 1] = r
                acc = nxt

        @pl.when(j == 2)
        def _():
            o_ref[...] = lax.dot_general(a_ref[...].astype(MXU_DTYPE), b_ref[...].astype(MXU_DTYPE), (_DIMS["nt"], ((), ())),
                                         preferred_element_type=F32).astype(o_ref.dtype)

    return pl.pallas_call(
        body, name=name, grid=(S // tm, 3),
        in_specs=[pl.BlockSpec((tm, D), lambda i, j: (i, 0)), pl.BlockSpec((D, D), lambda i, j: (grp * 3 + j, 0))],
        out_specs=[pl.BlockSpec((tm, D), lambda i, j: (i, j)), pl.BlockSpec((tm, HEAD_DIM), lambda i, j: (i, j))],
        out_shape=[jax.ShapeDtypeStruct((S, 3 * D), MXU_DTYPE), jax.ShapeDtypeStruct((S, 3 * HEAD_DIM), F32)],
        compiler_params=_params("parallel", "parallel"),
    )(h, w_qkvT)


def _band(b, dil):
    qi = lax.broadcasted_iota(jnp.int32, (BLOCK, 2 * BLOCK), 0)
    ki = lax.broadcasted_iota(jnp.int32, (BLOCK, 2 * BLOCK), 1)
    delta = qi + BLOCK - ki
    valid = (delta >= 0) & (delta <= BLOCK) & ((ki >= BLOCK) | (b > 0))
    return valid, (delta * dil).astype(F32)


def _attn_fwd(qkv, qg, kg, *, grp, name):
    S, W = qkv.shape
    D = W // 3
    H = D // HEAD_DIM
    dil = DILATED_GROUPS[grp][1]
    L = S // dil
    nb = L // BLOCK
    slopes = _slopes(3 * H)[grp * H:(grp + 1) * H]
    scale = HEAD_DIM ** -0.5

    def body(q_ref, kp_ref, kc_ref, vp_ref, vc_ref, qg_ref, kg_ref, o_ref, l_ref):
        b = pl.program_id(1)
        valid, dist = _band(b, dil)
        l_ref[...] = jnp.zeros_like(l_ref)
        ss = []
        for h in range(H):
            hs = slice(h * HEAD_DIM, (h + 1) * HEAD_DIM)
            qn = (q_ref[:, hs].astype(F32) * qg_ref[h:h + 1, :]).astype(MXU_DTYPE)
            kp = (kp_ref[:, hs].astype(F32) * kg_ref[h:h + 1, :]).astype(MXU_DTYPE)
            kc = (kc_ref[:, hs].astype(F32) * kg_ref[h:h + 1, :]).astype(MXU_DTYPE)
            ss.append(lax.dot_general(qn, jnp.concatenate([kp, kc], axis=0), (((1,), (1,)), ((), ())), preferred_element_type=F32))
        ps = []
        for h in range(H):
            s = jnp.where(valid, ss[h] * scale - float(slopes[h]) * dist, NEG)
            m = jnp.max(s, axis=-1, keepdims=True)
            p = jnp.exp(s - m)
            den = jnp.sum(p, axis=-1, keepdims=True)
            l_ref[:, h:h + 1] = m + jnp.log(den)
            ps.append((p.astype(MXU_DTYPE), den))
        for h in range(H):
            hs = slice(h * HEAD_DIM, (h + 1) * HEAD_DIM)
            pb, den = ps[h]
            v2 = jnp.concatenate([vp_ref[:, hs], vc_ref[:, hs]], axis=0).astype(MXU_DTYPE)
            o_ref[:, hs] = (jnp.dot(pb, v2, preferred_element_type=F32) / den).astype(o_ref.dtype)

    def cur(j):
        return lambda r, b: (r * nb + b, j)

    def prv(j):
        return lambda r, b: (r * nb + jnp.maximum(b - 1, 0), j)

    blk = (BLOCK, D)
    gain = pl.BlockSpec((H, HEAD_DIM), lambda r, b: (0, 0))
    return pl.pallas_call(
        body, name=name, grid=(dil, nb),
        in_specs=[pl.BlockSpec(blk, cur(0)), pl.BlockSpec(blk, prv(1)), pl.BlockSpec(blk, cur(1)),
                  pl.BlockSpec(blk, prv(2)), pl.BlockSpec(blk, cur(2)), gain, gain],
        out_specs=[pl.BlockSpec(blk, cur(0)), pl.BlockSpec((BLOCK, HEAD_DIM), cur(0))],
        out_shape=[jax.ShapeDtypeStruct((S, D), MXU_DTYPE), jax.ShapeDtypeStruct((S, HEAD_DIM), F32)],
        compiler_params=_params("parallel", "parallel"),
    )(qkv, qkv, qkv, qkv, qkv, qg, kg)


def _attn_merge(os_, ls_, dils, *, name):
    S, D = os_[0].shape
    H = D // HEAD_DIM
    G = len(dils)
    ts = _pick(S, SUB_TILE, 16 * max(dils))
    subs = [g for g in range(G) if dils[g] > 1]

    def body(*refs):
        o_refs, l_refs = refs[0:G], refs[G:2 * G]
        outb_ref = refs[2 * G]
        lt_refs = refs[2 * G + 1:3 * G + 1]
        scratch = refs[3 * G + 1:]
        lt_tok = scratch[0]
        o_tok = {g: scratch[1 + 2 * j] for j, g in enumerate(subs)}
        l_tok = {g: scratch[2 + 2 * j] for j, g in enumerate(subs)}
        for g in subs:
            _sub_to_tok(o_refs[g], o_tok[g], dils[g])
            _sub_to_tok(l_refs[g], l_tok[g], dils[g])
        lt_tok[0] = jnp.zeros((ts, HEAD_DIM), F32)
        for h in range(H):
            hs = slice(h * HEAD_DIM, (h + 1) * HEAD_DIM)
            ls = [l_tok[g][0][:, h:h + 1] if g in subs else l_refs[g][:, h:h + 1] for g in range(G)]
            ov = [o_tok[g][h] if g in subs else o_refs[g][:, hs].astype(F32) for g in range(G)]
            m = functools.reduce(jnp.maximum, ls)
            es = [jnp.exp(l - m) for l in ls]
            den = functools.reduce(lambda a, b: a + b, es)
            out = functools.reduce(lambda a, b: a + b, [e * o for e, o in zip(es, ov)]) / den
            outb_ref[:, hs] = out.astype(outb_ref.dtype)
            lt_tok.at[0][:, h:h + 1] = m + jnp.log(den)
        for g in range(G):
            if g in subs:
                _tok_to_sub(lt_tok, lt_refs[g], dils[g])
            else:
                lt_refs[g][...] = lt_tok[0]

    def spec(g, cols):
        return _sub_spec(dils[g], ts, cols) if g in subs else pl.BlockSpec((ts, cols), lambda i: (i, 0))

    def shape(g, cols, dtype):
        return jax.ShapeDtypeStruct((dils[g], S // dils[g], cols) if g in subs else (S, cols), dtype)

    def view(a, g):
        return a.reshape(dils[g], S // dils[g], a.shape[-1]) if g in subs else a

    outs = pl.pallas_call(
        body, name=name, grid=(S // ts,),
        in_specs=[spec(g, D) for g in range(G)] + [spec(g, HEAD_DIM) for g in range(G)],
        out_specs=[pl.BlockSpec((ts, D), lambda i: (i, 0))] + [spec(g, HEAD_DIM) for g in range(G)],
        out_shape=[jax.ShapeDtypeStruct((S, D), MXU_DTYPE)] + [shape(g, HEAD_DIM, F32) for g in range(G)],
        scratch_shapes=[pltpu.VMEM((1, ts, HEAD_DIM), F32)] + [pltpu.VMEM((H, ts, HEAD_DIM), F32), pltpu.VMEM((1, ts, HEAD_DIM), F32)] * len(subs),
        compiler_params=_params("parallel"),
    )(*[view(o, g) for g, o in enumerate(os_)], *[view(l, g) for g, l in enumerate(ls_)])
    return outs[0], [t.reshape(S, HEAD_DIM) for t in outs[1:]]


def _attn_delta(do, out, dils, *, name):
    S, D = out.shape
    H = D // HEAD_DIM
    G = len(dils)
    ts = _pick(S, SUB_TILE, 16 * max(dils))
    subs = [g for g in range(G) if dils[g] > 1]

    def body(do_ref, o_ref, *rest):
        d_refs = rest[0:G]
        dos_refs = rest[G:G + len(subs)]
        d_tok, do_tok = rest[G + len(subs):]
        d_tok[0] = jnp.zeros((ts, HEAD_DIM), F32)
        for h in range(H):
            hs = slice(h * HEAD_DIM, (h + 1) * HEAD_DIM)
            dov = do_ref[:, hs].astype(F32)
            do_tok[h] = dov
            d_tok.at[0][:, h:h + 1] = jnp.sum(dov * o_ref[:, hs].astype(F32), axis=-1, keepdims=True)
        for g in range(G):
            if g in subs:
                _tok_to_sub(d_tok, d_refs[g], dils[g])
            else:
                d_refs[g][...] = d_tok[0]
        for g, dst in zip(subs, dos_refs):
            _tok_to_sub(do_tok, dst, dils[g])

    def spec(g, cols):
        return _sub_spec(dils[g], ts, cols) if g in subs else pl.BlockSpec((ts, cols), lambda i: (i, 0))

    def shape(g, cols, dtype):
        return jax.ShapeDtypeStruct((dils[g], S // dils[g], cols) if g in subs else (S, cols), dtype)

    row = pl.BlockSpec((ts, D), lambda i: (i, 0))
    outs = pl.pallas_call(
        body, name=name, grid=(S // ts,), in_specs=[row, row],
        out_specs=[spec(g, HEAD_DIM) for g in range(G)] + [spec(g, D) for g in subs],
        out_shape=[shape(g, HEAD_DIM, F32) for g in range(G)] + [shape(g, D, do.dtype) for g in subs],
        scratch_shapes=[pltpu.VMEM((1, ts, HEAD_DIM), F32), pltpu.VMEM((H, ts, HEAD_DIM), F32)],
        compiler_params=_params("parallel"),
    )(do, out)
    deltas = [t.reshape(S, HEAD_DIM) for t in outs[0:G]]
    dos = {g: t.reshape(S, D) for g, t in zip(subs, outs[G:])}
    return deltas, [dos[g] if g in subs else do for g in range(G)]


def _attn_bwd(qkv, rqk, do, lse, delta, qg, kg, *, grp, name):
    S, W = qkv.shape
    D = W // 3
    H = D // HEAD_DIM
    dil = DILATED_GROUPS[grp][1]
    L = S // dil
    nb = L // BLOCK
    slopes = _slopes(3 * H)[grp * H:(grp + 1) * H]
    scale = HEAD_DIM ** -0.5

    def body(q_ref, qp_ref, kp_ref, kc_ref, vp_ref, vc_ref, do_ref, l_ref, dl_ref, rq_ref, rk_ref, qg_ref, kg_ref,
             out_ref, dqg_ref, dkg_ref, cq_ref, ck_ref, cv_ref, nq_ref, nk_ref, nv_ref, pk_ref, pv_ref):
        r = pl.program_id(0)
        b = pl.program_id(1)

        @pl.when(jnp.logical_and(r == 0, b == 0))
        def _():
            dqg_ref[...] = jnp.zeros_like(dqg_ref)
            dkg_ref[...] = jnp.zeros_like(dkg_ref)

        @pl.when(b < nb)
        def _():
            valid, dist = _band(b, dil)

            def operands(h):
                hs = slice(h * HEAD_DIM, (h + 1) * HEAD_DIM)
                qn = (q_ref[:, hs].astype(F32) * qg_ref[h:h + 1, :]).astype(MXU_DTYPE)
                kp = (kp_ref[:, hs].astype(F32) * kg_ref[h:h + 1, :]).astype(MXU_DTYPE)
                kc = (kc_ref[:, hs].astype(F32) * kg_ref[h:h + 1, :]).astype(MXU_DTYPE)
                k2 = jnp.concatenate([kp, kc], axis=0)
                v2 = jnp.concatenate([vp_ref[:, hs], vc_ref[:, hs]], axis=0).astype(MXU_DTYPE)
                return hs, qn, k2, v2, do_ref[:, hs].astype(MXU_DTYPE)

            sdp = []
            for h in range(H):
                hs, qn, k2, v2, doh = operands(h)
                s = lax.dot_general(qn, k2, (((1,), (1,)), ((), ())), preferred_element_type=F32)
                dp = lax.dot_general(doh, v2, (((1,), (1,)), ((), ())), preferred_element_type=F32)
                sdp.append((s, dp))
            pds = []
            for h in range(H):
                s, dp = sdp[h]
                s = jnp.where(valid, s * scale - float(slopes[h]) * dist, NEG)
                p = jnp.exp(s - l_ref[:, h:h + 1])
                pds.append((p.astype(MXU_DTYPE), (p * (dp - dl_ref[:, h:h + 1]) * scale).astype(MXU_DTYPE)))
            for h in range(H):
                hs, qn, k2, v2, doh = operands(h)
                pb, dsc = pds[h]
                nq_ref[:, hs] = jnp.dot(dsc, k2, preferred_element_type=F32)
                dk2 = lax.dot_general(dsc, qn, (((0,), (0,)), ((), ())), preferred_element_type=F32)
                dv2 = lax.dot_general(pb, doh, (((0,), (0,)), ((), ())), preferred_element_type=F32)
                pk_ref[:, hs] = dk2[0:BLOCK]
                nk_ref[:, hs] = dk2[BLOCK:2 * BLOCK]
                pv_ref[:, hs] = dv2[0:BLOCK]
                nv_ref[:, hs] = dv2[BLOCK:2 * BLOCK]

        @pl.when(b == nb)
        def _():
            pk_ref[...] = jnp.zeros_like(pk_ref)
            pv_ref[...] = jnp.zeros_like(pv_ref)

        @pl.when(b > 0)
        def _():
            for h in range(H):
                hs = slice(h * HEAD_DIM, (h + 1) * HEAD_DIM)
                for j, (xh_ref, r_ref, gain_ref, dgain_ref) in enumerate(((qp_ref, rq_ref, qg_ref, dqg_ref),
                                                                          (kp_ref, rk_ref, kg_ref, dkg_ref))):
                    dy = cq_ref[:, hs] if j == 0 else ck_ref[:, hs] + pk_ref[:, hs]
                    gain = gain_ref[h:h + 1, :]
                    xh = xh_ref[:, hs].astype(F32)
                    rr = r_ref[:, h:h + 1]
                    gy = dy * gain
                    dx = rr * (gy - xh * jnp.mean(gy * xh, axis=-1, keepdims=True))
                    out_ref[:, j * D + h * HEAD_DIM:j * D + (h + 1) * HEAD_DIM] = dx.astype(out_ref.dtype)
                    dgain_ref[h:h + 1, :] += jnp.sum(dy * xh, axis=0, keepdims=True)
                out_ref[:, 2 * D + h * HEAD_DIM:2 * D + (h + 1) * HEAD_DIM] = (cv_ref[:, hs] + pv_ref[:, hs]).astype(out_ref.dtype)

        @pl.when(b < nb)
        def _():
            cq_ref[...] = nq_ref[...]
            ck_ref[...] = nk_ref[...]
            cv_ref[...] = nv_ref[...]

    def cur(j):
        return lambda r, b: (r * nb + jnp.minimum(b, nb - 1), j)

    def prv(j):
        return lambda r, b: (r * nb + jnp.clip(b - 1, 0, nb - 1), j)

    blk = (BLOCK, D)
    lblk = pl.BlockSpec((BLOCK, HEAD_DIM), cur(0))
    gain = pl.BlockSpec((H, HEAD_DIM), lambda r, b: (0, 0))
    return pl.pallas_call(
        body, name=name, grid=(dil, nb + 1),
        in_specs=[pl.BlockSpec(blk, cur(0)), pl.BlockSpec(blk, prv(0)), pl.BlockSpec(blk, prv(1)), pl.BlockSpec(blk, cur(1)),
                  pl.BlockSpec(blk, prv(2)), pl.BlockSpec(blk, cur(2)), pl.BlockSpec(blk, cur(0)), lblk, lblk,
                  pl.BlockSpec((BLOCK, HEAD_DIM), prv(0)), pl.BlockSpec((BLOCK, HEAD_DIM), prv(1)), gain, gain],
        out_specs=[pl.BlockSpec((BLOCK, 3 * D), lambda r, b: (r * nb + jnp.maximum(b - 1, 0), 0)), gain, gain],
        out_shape=[jax.ShapeDtypeStruct((S, 3 * D), MXU_DTYPE), jax.ShapeDtypeStruct((H, HEAD_DIM), F32),
                   jax.ShapeDtypeStruct((H, HEAD_DIM), F32)],
        scratch_shapes=[pltpu.VMEM(blk, F32)] * 8,
        compiler_params=_params("arbitrary", "arbitrary"),
    )(qkv, qkv, qkv, qkv, qkv, qkv, do, lse, delta, rqk, rqk, qg, kg)


def _adamw(w, m, v, terms, slots, *, name):
    R, C = w.shape
    nt = len(terms)
    tr = _pick(R, 256, 16)
    c1 = 1.0 - ADAM_B1 ** ADAM_STEP
    c2 = 1.0 - ADAM_B2 ** ADAM_STEP

    def body(slot_ref, w_ref, m_ref, v_ref, *rest):
        t_refs = rest[:nt]
        g_ref, d_ref, nm_ref, nv_ref = rest[nt:]
        g = t_refs[0][...].astype(F32)
        for t in t_refs[1:]:
            g = g + t[...].astype(F32)
        mm = ADAM_B1 * m_ref[...] + (1.0 - ADAM_B1) * g
        vv = ADAM_B2 * v_ref[...] + (1.0 - ADAM_B2) * (g * g)
        m_hat = mm / c1
        v_hat = vv / c2
        g_ref[...] = g
        d_ref[...] = -ADAM_LR * (m_hat / (jnp.sqrt(v_hat) + ADAM_EPS) + ADAM_WD * w_ref[...])
        nm_ref[...] = mm
        nv_ref[...] = vv

    row = pl.BlockSpec((tr, C), lambda i, s: (i, 0))
    grid_spec = pltpu.PrefetchScalarGridSpec(
        num_scalar_prefetch=1, grid=(R // tr,),
        in_specs=[row, row, row] + [pl.BlockSpec((None, tr, C), lambda i, s, t=t: (s[t], i, 0)) for t in range(nt)],
        out_specs=[row] * 4)
    return pl.pallas_call(
        body, name=name, grid_spec=grid_spec, out_shape=[jax.ShapeDtypeStruct((R, C), F32)] * 4,
        compiler_params=_params("parallel"),
    )(slots, w, m, v, *terms)


def _chip_partials(g, sib, core, *, name):
    _, R, C = g.shape
    tr = _pick(R, 1200, 16)

    def body(core_ref, g_ref, s_ref, o_ref):
        o_ref[...] = (g_ref[...] + s_ref[...].astype(F32)).astype(o_ref.dtype)

    grid_spec = pltpu.PrefetchScalarGridSpec(
        num_scalar_prefetch=1, grid=(4, R // tr),
        in_specs=[pl.BlockSpec((None, tr, C), lambda k, i, c: (2 * k + c[0], i, 0)),
                  pl.BlockSpec((None, tr, C), lambda k, i, c: (k, i, 0))],
        out_specs=pl.BlockSpec((None, tr, C), lambda k, i, c: (k, i, 0)))
    return pl.pallas_call(
        body, name=name, grid_spec=grid_spec, out_shape=jax.ShapeDtypeStruct((4, R, C), sib.dtype),
        compiler_params=_params("parallel", "parallel"),
    )(core, g, sib)


_ANY = pl.BlockSpec(memory_space=pl.ANY)


def _place():
    return lax.axis_index("x"), lax.axis_index("y"), lax.axis_index("c")


class _Exchange:
    def __init__(self, operands, out_shapes, scratch, emit):
        self.operands, self.out_shapes, self.scratch, self.emit = operands, out_shapes, scratch, emit


def _run_exchange(ex, *, name):
    n_in, n_out = len(ex.operands), len(ex.out_shapes)

    def body(*refs):
        ex.emit(0, 1, refs[:n_in], refs[n_in:n_in + n_out], refs[n_in + n_out:])

    return pl.pallas_call(body, name=name, in_specs=[_ANY] * n_in, out_specs=[_ANY] * n_out, out_shape=ex.out_shapes,
                          scratch_shapes=ex.scratch)(*ex.operands)


def _gather_exchange(shard):
    R, C = shard.shape

    def emit(step, n, ins, outs, sems):
        x_ref, out_ref = ins[0], outs[0]
        send_sems, recv_sems, local_sem = sems
        x, y, c = _place()
        me, sibling = (x, y, c), (x, y, 1 - c)
        chips = [(1 - x, y), (x, 1 - y), (1 - x, 1 - y)]

        def slot(px, py, pc):
            return out_ref.at[4 * px + 2 * py + pc]

        def copy(k, block, to, src=None):
            return pltpu.make_async_remote_copy(
                src_ref=slot(*block) if src is None else src, dst_ref=slot(*block),
                send_sem=send_sems.at[k], recv_sem=recv_sems.at[k], device_id=to, device_id_type=MESH)

        mine = pltpu.make_async_copy(x_ref, slot(*me), local_sem)
        first = [copy(0, me, sibling, src=x_ref)] + [copy(1 + j, me, (*chip, c), src=x_ref) for j, chip in enumerate(chips)]
        passed = [copy(4 + j, (*chip, c), sibling) for j, chip in enumerate(chips)]

        @pl.when(step == 0)
        def _():
            mine.start()
            for cp in first:
                cp.start()

        for j, chip in enumerate(chips):
            @pl.when(step == max(n - 2 * (len(chips) - j), 0))
            def _(j=j, chip=chip):
                copy(1 + j, (*chip, c), me).wait_recv()
                passed[j].start()

        @pl.when(step == n - 1)
        def _():
            copy(0, sibling, me).wait_recv()
            for j, chip in enumerate(chips):
                copy(4 + j, (*chip, 1 - c), me).wait_recv()
            for cp in first + passed:
                cp.wait_send()
            mine.wait()

    return _Exchange([shard], [jax.ShapeDtypeStruct((N_DEV, R, C), shard.dtype)],
                     [pltpu.SemaphoreType.DMA((7,)), pltpu.SemaphoreType.DMA((7,)), pltpu.SemaphoreType.DMA], emit)


def _all_gather(shard, *, name):
    return _run_exchange(_gather_exchange(shard), name=name)[0]


def _rs_sibling(g, *, name):
    _, R, C = g.shape

    def body(g_ref, sib_ref, send_sems, recv_sems):
        x, y, c = _place()
        sends = [pltpu.make_async_remote_copy(
            src_ref=g_ref.at[2 * k + (1 - c)], dst_ref=sib_ref.at[k], send_sem=send_sems.at[k], recv_sem=recv_sems.at[k],
            device_id=(x, y, 1 - c), device_id_type=MESH) for k in range(4)]
        for cp in sends:
            cp.start()
        for cp in sends:
            cp.wait_recv()
        for cp in sends:
            cp.wait_send()

    return pl.pallas_call(
        body, name=name, in_specs=[_ANY], out_specs=_ANY, out_shape=jax.ShapeDtypeStruct((4, R, C), g.dtype),
        scratch_shapes=[pltpu.SemaphoreType.DMA((4,)), pltpu.SemaphoreType.DMA((4,))],
    )(g)


def _chips_exchange(part):
    _, R, C = part.shape

    def emit(step, n, ins, outs, sems):
        p_ref, out_ref = ins[0], outs[0]
        send_sems, recv_sems = sems
        x, y, c = _place()
        chips = [(1 - x, y), (x, 1 - y), (1 - x, 1 - y)]
        sends = [pltpu.make_async_remote_copy(
            src_ref=p_ref.at[2 * px + py], dst_ref=out_ref.at[j], send_sem=send_sems.at[j], recv_sem=recv_sems.at[j],
            device_id=(px, py, c), device_id_type=MESH) for j, (px, py) in enumerate(chips)]

        @pl.when(step == 0)
        def _():
            for cp in sends:
                cp.start()

        @pl.when(step == n - 1)
        def _():
            for cp in sends:
                cp.wait_recv()
            for cp in sends:
                cp.wait_send()

    return _Exchange([part], [jax.ShapeDtypeStruct((3, R, C), part.dtype)],
                     [pltpu.SemaphoreType.DMA((3,)), pltpu.SemaphoreType.DMA((3,))], emit)


def _rs_chips(part, *, name):
    return _run_exchange(_chips_exchange(part), name=name)[0]


def _interleave_rows(wt):
    F2, D = wt.shape
    return wt.reshape(2, F2 // 256, 128, D).transpose(1, 0, 2, 3).reshape(F2, D)


def _deinterleave_rows(wt):
    F2, D = wt.shape
    return wt.reshape(F2 // 256, 2, 128, D).transpose(1, 0, 2, 3).reshape(F2, D)


def _interleave_cols(v):
    k, F2 = v.shape
    return v.reshape(k, 2, F2 // 256, 128).transpose(0, 2, 1, 3).reshape(k, F2)


def _deinterleave_cols(v):
    k, F2 = v.shape
    return v.reshape(k, F2 // 256, 2, 128).transpose(0, 2, 1, 3).reshape(k, F2)


def _pack_rows(parts):
    return jnp.concatenate(parts, axis=0)


def _flat_pack(parts, width):
    flat = jnp.concatenate([p.reshape(-1) for p in parts])
    pad = (-flat.shape[0]) % (8 * width)
    return jnp.pad(flat, (0, pad)).reshape(-1, width)


def _flat_unpack(packed, shapes):
    flat = packed.reshape(-1)
    out, off = [], 0
    for shp in shapes:
        n = int(np.prod(shp))
        out.append(flat[off:off + n].reshape(shp))
        off += n
    return out


def _ffn_forward(x, hf, wupT, wdown, dw_i, dwb_i, tag, loss_target=None):
    up = _mm(hf, wupT, mode="nt", out_dtype=MXU_DTYPE, name=f"ffn{tag}_up", tm=2048, tn=512)
    act = _ffn_act_fwd(up, dw_i, dwb_i, name=f"ffn{tag}_act")
    if loss_target is None:
        y = _mm(act, wdown, mode="nn", out_dtype=F32, name=f"ffn{tag}_down", residual=x)
    else:
        y = _mm(act, wdown, mode="nn", out_dtype=F32, name=f"ffn{tag}_down", residual=x, tm=512, post=_post_loss(loss_target),
                keep_main=False)
    return y, (hf, up, act)


def _ffn_backward(x, g_ffn, wupT, wdown, dw_i, dwb_i, saved, dy, dyb, tag, exchange=None):
    hf, up, act = saved
    dact = _mm(dyb, wdown, mode="nt", out_dtype=MXU_DTYPE, name=f"ffn{tag}_dact", tm=1024, tn=1408)
    d_wdown = _mm(act, dyb, mode="tn", out_dtype=F32, name=f"ffn{tag}_dwdown", tm=1408, tk=2048)
    dup, d_dw_i, d_dwb_i, *carried = _ffn_act_bwd(up, dact, dw_i, dwb_i, name=f"ffn{tag}_actbwd", exchange=exchange)
    dx, dxb, dg, cs = _mm_post(dup, wupT, _post_rms_bwd(x, g_ffn, dy), name=f"ffn{tag}_dhf")
    d_wupT = _mm(dup, hf, mode="tn", out_dtype=F32, name=f"ffn{tag}_dwup", tm=1408, tk=2048)
    return dx, dxb, cs, dict(w_upT=d_wupT, w_down=d_wdown, dw=d_dw_i[0:FFN_KERNEL], dw_b=d_dwb_i, norm=dg), carried


def _local_step(x, target, p, late_weights=None, early_reduce=None):
    S, D = x.shape
    H = D // HEAD_DIM
    h0 = _rms_fwd(x, p["norm_mix"][0:1], name="l0_rms")
    u = _mm(h0, p["w_inT"], mode="nt", out_dtype=F32, name="l0_in", bias=p["cm_b_in"])
    c, s, *carried = _cm_fwd(u, p["cm_dw"], p["cm_dw_b"], p["cm_ln_g"], p["cm_ln_b"], name="l0_conv",
                             exchange=None if late_weights is None else late_weights[0])
    if late_weights is not None:
        p = {**p, **late_weights[1](carried)}
    x1, hf0 = _mm(s, p["w_out"], mode="nn", out_dtype=F32, name="l0_out", bias=p["cm_b_out"], residual=x,
                  post=_post_rms(p["norm_ffn"][0:1]))
    x2, sv0 = _ffn_forward(x1, hf0, p["w_upT"][0], p["w_down"][0], p["ff_dw"][0], p["ff_dw_b"][0:1], 0)
    dils = [dil for _, dil in DILATED_GROUPS]
    assert dils[0] == 1
    h1s = [t.reshape(S, D) for t in _rms_fwd(x2, p["norm_mix"][1:2], name="l1_rms", subs=tuple(dils[1:]))]
    qkvs, rqks, os_, ls_ = [], [], [], []
    for g in range(len(dils)):
        qkv_g, r_g = _qkv_proj(h1s[g], p["w_qkvT"], grp=g, name=f"l1_qkv{g}")
        qkvs.append(qkv_g)
        rqks.append(r_g)
        o, l = _attn_fwd(qkvs[g], p["at_q_norm"][g * H:(g + 1) * H], p["at_k_norm"][g * H:(g + 1) * H], grp=g, name=f"l1_attn{g}")
        os_.append(o)
        ls_.append(l)
    outb, lses = _attn_merge(os_, ls_, dils, name="l1_merge")
    x3, hf1 = _mm(outb, p["w_o"], mode="nn", out_dtype=F32, name="l1_o", residual=x2, post=_post_rms(p["norm_ffn"][1:2]))
    (dx4, dx4b, loss), sv1 = _ffn_forward(x3, hf1, p["w_upT"][1], p["w_down"][1], p["ff_dw"][1], p["ff_dw_b"][1:2], 1,
                                          loss_target=target)
    dx3, dx3b, _, gf1, _ = _ffn_backward(x3, p["norm_ffn"][1:2], p["w_upT"][1], p["w_down"][1], p["ff_dw"][1], p["ff_dw_b"][1:2],
                                      sv1, dx4, dx4b, 1)
    do = _mm(dx3b, p["w_o"], mode="nt", out_dtype=MXU_DTYPE, name="l1_do")
    d_wo = _mm(outb, dx3b, mode="tn", out_dtype=F32, name="l1_dwo", tk=2048)
    deltas, dos = _attn_delta(do, outb, dils, name="l1_delta")
    dh1s, d_wqkvT, dqg, dkg = [], [], [], []
    for g, dil in enumerate(dils):
        dqkv_g, a, b_ = _attn_bwd(qkvs[g], rqks[g], dos[g], lses[g], deltas[g], p["at_q_norm"][g * H:(g + 1) * H],
                                  p["at_k_norm"][g * H:(g + 1) * H], grp=g, name=f"l1_attnbwd{g}")
        dqg.append(a)
        dkg.append(b_)
        d_wqkvT.append(_mm(dqkv_g, h1s[g], mode="tn", out_dtype=F32, name=f"l1_dwqkv{g}", tk=2048))
        dh1s.append(_mm(dqkv_g, p["w_qkvT"], mode="nn", out_dtype=F32, name=f"l1_dh{g}", b_off=g * 3 * D, b_len=3 * D))
    dx2, dx2b, dgm1, _ = _rms_bwd(x2, p["norm_mix"][1:2], dh1s[0:1], dx3, name="l1_rmsbwd",
                                  dh_subs=[(dh1s[g].reshape(dils[g], S // dils[g], D), dils[g]) for g in range(1, len(dils))])
    ex, finish = (None, None) if early_reduce is None else early_reduce(
        dict(w_qkvT=jnp.concatenate(d_wqkvT, axis=0), w_o=d_wo, w_upT=gf1["w_upT"], w_down=gf1["w_down"]))
    dx1, dx1b, cs1, gf0, carried = _ffn_backward(x1, p["norm_ffn"][0:1], p["w_upT"][0], p["w_down"][0], p["ff_dw"][0],
                                                 p["ff_dw_b"][0:1], sv0, dx2, dx2b, 0, exchange=ex)
    reduced = [] if finish is None else [finish(carried)]
    ex, finish = (None, None) if early_reduce is None else early_reduce(dict(w_upT=gf0["w_upT"], w_down=gf0["w_down"]))
    ds = _mm(dx1b, p["w_out"], mode="nt", out_dtype=F32, name="l0_ds")
    d_wout = _mm(s, dx1b, mode="tn", out_dtype=F32, name="l0_dwout", tk=2048)
    dc, d_lng, d_lnb = _cm_ln_bwd(c, ds, p["cm_ln_g"], p["cm_ln_b"], name="l0_lnbwd")
    du, d_cmdw, d_cmdwb, d_bin, *carried = _cm_conv_bwd(dc, u, p["cm_dw"], name="l0_convbwd", exchange=ex)
    if finish is not None:
        reduced.append(finish(carried))
    grad_x, _, dgm0, _ = _mm_post(du, p["w_inT"], _post_rms_bwd(x, p["norm_mix"][0:1], dx1), name="l0_dh")
    d_winT = _mm(du, h0, mode="tn", out_dtype=F32, name="l0_dwin", tk=2048)
    grads = dict(
        norm_mix=jnp.concatenate([dgm0, dgm1], axis=0),
        norm_ffn=jnp.concatenate([gf0["norm"], gf1["norm"]], axis=0),
        w_inT=d_winT, cm_b_in=d_bin, cm_dw=d_cmdw[0:CONV_KERNEL], cm_dw_b=d_cmdwb, cm_ln_g=d_lng, cm_ln_b=d_lnb,
        w_out=d_wout, cm_b_out=cs1,
        w_qkvT=jnp.concatenate(d_wqkvT, axis=0), at_q_norm=jnp.concatenate(dqg, axis=0), at_k_norm=jnp.concatenate(dkg, axis=0),
        w_o=d_wo,
        w_upT=[gf0["w_upT"], gf1["w_upT"]], w_down=[gf0["w_down"], gf1["w_down"]],
        ff_dw=jnp.stack([gf0["dw"], gf1["dw"]]), ff_dw_b=jnp.concatenate([gf0["dw_b"], gf1["dw_b"]], axis=0),
    )
    return loss, grad_x, grads, reduced


_BIG = ("cm_w_in", "cm_w_out", "at_w_qkv", "at_w_out", "ff_w_up", "ff_w_down")
_TRANSPOSED = ("cm_w_in", "at_w_qkv", "ff_w_up")
_SMALL = ("norm_mix", "norm_ffn", "cm_b_in", "cm_dw_b", "cm_ln_g", "cm_ln_b", "cm_b_out", "at_q_norm", "at_k_norm",
          "ff_dw_b", "cm_dw", "ff_dw")
_SMALL_SHARDED = ("cm_dw", "ff_dw")
_ORDER = ("norm_mix", "norm_ffn", "cm_w_in", "cm_b_in", "cm_dw", "cm_dw_b", "cm_ln_g", "cm_ln_b", "cm_w_out", "cm_b_out",
          "at_w_qkv", "at_q_norm", "at_k_norm", "at_w_out", "ff_w_up", "ff_dw", "ff_dw_b", "ff_w_down")


_UNITS = (("cm_w_in", 0), ("cm_w_out", 0), ("ff_w_up", 0), ("ff_w_down", 0),
          ("at_w_qkv", 0), ("at_w_out", 0), ("ff_w_up", 1), ("ff_w_down", 1))
_N_FIRST = 2
_N_LAYER0 = 4


def _unit_rows(t, n, l):
    return t[n].shape[2] if n in _TRANSPOSED else t[n].shape[1]


def _big_rows(t, units=_UNITS):
    return _pack_rows([t[n][l].T if n in _TRANSPOSED else t[n][l] for n, l in units])


def _big_unrows(packed, like):
    mats, off = {}, 0
    for n, l in _UNITS:
        rows = _unit_rows(like, n, l)
        m = packed[off:off + rows]
        off += rows
        mats[(n, l)] = m.T if n in _TRANSPOSED else m
    return {n: jnp.stack([mats[(n, l)] for l in range(like[n].shape[0])]) for n in _BIG}


def kernel(x, norm_mix, norm_ffn, cm_w_in, cm_b_in, cm_dw, cm_dw_b, cm_ln_g, cm_ln_b, cm_w_out, cm_b_out, at_w_qkv, at_q_norm, at_k_norm, at_w_out, ff_w_up, ff_dw, ff_dw_b, ff_w_down, loss_target, m_norm_mix, m_norm_ffn, m_cm_w_in, m_cm_b_in, m_cm_dw, m_cm_dw_b, m_cm_ln_g, m_cm_ln_b, m_cm_w_out, m_cm_b_out, m_at_w_qkv, m_at_q_norm, m_at_k_norm, m_at_w_out, m_ff_w_up, m_ff_dw, m_ff_dw_b, m_ff_w_down, v_norm_mix, v_norm_ffn, v_cm_w_in, v_cm_b_in, v_cm_dw, v_cm_dw_b, v_cm_ln_g, v_cm_ln_b, v_cm_w_out, v_cm_b_out, v_at_w_qkv, v_at_q_norm, v_at_k_norm, v_at_w_out, v_ff_w_up, v_ff_dw, v_ff_dw_b, v_ff_w_down):
    w = dict(norm_mix=norm_mix, norm_ffn=norm_ffn, cm_w_in=cm_w_in, cm_b_in=cm_b_in, cm_dw=cm_dw, cm_dw_b=cm_dw_b, cm_ln_g=cm_ln_g,
             cm_ln_b=cm_ln_b, cm_w_out=cm_w_out, cm_b_out=cm_b_out, at_w_qkv=at_w_qkv, at_q_norm=at_q_norm, at_k_norm=at_k_norm,
             at_w_out=at_w_out, ff_w_up=ff_w_up, ff_dw=ff_dw, ff_dw_b=ff_dw_b, ff_w_down=ff_w_down)
    m = dict(norm_mix=m_norm_mix, norm_ffn=m_norm_ffn, cm_w_in=m_cm_w_in, cm_b_in=m_cm_b_in, cm_dw=m_cm_dw, cm_dw_b=m_cm_dw_b,
             cm_ln_g=m_cm_ln_g, cm_ln_b=m_cm_ln_b, cm_w_out=m_cm_w_out, cm_b_out=m_cm_b_out, at_w_qkv=m_at_w_qkv,
             at_q_norm=m_at_q_norm, at_k_norm=m_at_k_norm, at_w_out=m_at_w_out, ff_w_up=m_ff_w_up, ff_dw=m_ff_dw,
             ff_dw_b=m_ff_dw_b, ff_w_down=m_ff_w_down)
    v = dict(norm_mix=v_norm_mix, norm_ffn=v_norm_ffn, cm_w_in=v_cm_w_in, cm_b_in=v_cm_b_in, cm_dw=v_cm_dw, cm_dw_b=v_cm_dw_b,
             cm_ln_g=v_cm_ln_g, cm_ln_b=v_cm_ln_b, cm_w_out=v_cm_w_out, cm_b_out=v_cm_b_out, at_w_qkv=v_at_w_qkv,
             at_q_norm=v_at_q_norm, at_k_norm=v_at_k_norm, at_w_out=v_at_w_out, ff_w_up=v_ff_w_up, ff_dw=v_ff_dw,
             ff_dw_b=v_ff_dw_b, ff_w_down=v_ff_w_down)
    S, D = x.shape[1], x.shape[2]
    F2 = ff_dw_b.shape[1]
    H3 = at_q_norm.shape[1]
    me = 4 * lax.axis_index("x") + 2 * lax.axis_index("y") + lax.axis_index("c")

    ix, iy, ic = lax.axis_index("x"), lax.axis_index("y"), lax.axis_index("c")
    chip = 2 * ix + iy
    core = jnp.stack([ic]).astype(jnp.int32)
    w_rows = _big_rows(w)
    unit_rows = [_unit_rows(w, n, l) for n, l in _UNITS]
    n_first = sum(unit_rows[:_N_FIRST])
    n_layer0 = sum(unit_rows[:_N_LAYER0])
    w_wire = w_rows.astype(MXU_DTYPE)

    def unpack(gathered, units, rows):
        full, off = {}, 0
        for (n, l), r in zip(units, rows):
            full[(n, l)] = gathered[:, off:off + r, :].reshape(N_DEV * r, D)
            off += r
        out = {}
        if ("cm_w_in", 0) in full:
            out.update(w_inT=full[("cm_w_in", 0)], w_out=full[("cm_w_out", 0)])
        if ("at_w_qkv", 0) in full:
            out.update(w_qkvT=full[("at_w_qkv", 0)], w_o=full[("at_w_out", 0)],
                       w_upT=[_interleave_rows(full[("ff_w_up", l)]) for l in range(2)],
                       w_down=[full[("ff_w_down", l)] for l in range(2)])
        return out

    first = _all_gather(w_wire[:n_first], name="gather_first")
    late_weights = (_gather_exchange(w_wire[n_first:]),
                    lambda carried: unpack(carried[0], _UNITS[_N_FIRST:], unit_rows[_N_FIRST:]))
    small_sh = _flat_pack([cm_dw, ff_dw], D)
    small_g = _all_gather(small_sh, name="gather_small")
    cm_dw_full = jnp.concatenate([_flat_unpack(small_g[j], [cm_dw.shape, ff_dw.shape])[0][0] for j in range(N_DEV)], axis=-1)
    ff_dw_full = jnp.concatenate([_flat_unpack(small_g[j], [cm_dw.shape, ff_dw.shape])[1] for j in range(N_DEV)], axis=-1)

    p = dict(
        norm_mix=norm_mix, norm_ffn=norm_ffn, cm_b_in=cm_b_in, cm_dw=cm_dw_full, cm_dw_b=cm_dw_b, cm_ln_g=cm_ln_g, cm_ln_b=cm_ln_b,
        cm_b_out=cm_b_out, at_q_norm=at_q_norm[0], at_k_norm=at_k_norm[0],
        ff_dw=jnp.stack([_interleave_cols(ff_dw_full[l]) for l in range(ff_dw_full.shape[0])]),
        ff_dw_b=_interleave_cols(ff_dw_b),
        **unpack(first, _UNITS[:_N_FIRST], unit_rows[:_N_FIRST]),
    )

    def pack(pieces):
        return jnp.concatenate([t.reshape(N_DEV, t.shape[0] // N_DEV, D) for t in pieces], axis=1)

    def reduce_start(pieces, tag):
        g_rows = pack(pieces)
        sib = _rs_sibling(g_rows.astype(WIRE_DTYPE), name=f"reduce{tag}_sibling")
        return g_rows, sib, _chip_partials(g_rows, sib, core, name=f"reduce{tag}_add")

    def early_reduce(gd):
        ffn = [_deinterleave_rows(gd["w_upT"]), gd["w_down"]]
        tag, pieces = (2, [gd["w_qkvT"], gd["w_o"]] + ffn) if "w_qkvT" in gd else (1, ffn)
        g_rows, sib, part = reduce_start(pieces, tag)
        return _chips_exchange(part), lambda carried: (g_rows, sib, carried[0])

    loss8, grad_x, g, (reduced2, reduced1) = _local_step(x[0], loss_target[0], p, late_weights, early_reduce)
    loss = lax.psum(loss8[0, 0], ("x", "y", "c"))
    g_rows0, sib0, part0 = reduce_start([g["w_inT"], g["w_out"]], 0)
    reduced0 = (g_rows0, sib0, _rs_chips(part0, name="reduce0_chips"))
    slots = jnp.stack([me, chip, 0 * me, 0 * me + 1, 0 * me + 2]).astype(jnp.int32)
    m_rows, v_rows = _big_rows(m), _big_rows(v)
    updated = []
    for tag, (g_rows, sib, recv), rows in ((0, reduced0, slice(0, n_first)), (1, reduced1, slice(n_first, n_layer0)),
                                           (2, reduced2, slice(n_layer0, None))):
        updated.append(_adamw(w_rows[rows], m_rows[rows], v_rows[rows], [g_rows, sib, recv, recv, recv], slots, name=f"adamw_big{tag}"))
    big = [_big_unrows(jnp.concatenate([u[k] for u in updated], axis=0), w) for k in range(4)]

    g_small = dict(g)
    g_small["cm_b_in"] = g["cm_b_in"]
    g_small["at_q_norm"] = g["at_q_norm"][None]
    g_small["at_k_norm"] = g["at_k_norm"][None]
    g_small["ff_dw_b"] = _deinterleave_cols(g["ff_dw_b"])
    g_small["cm_dw"] = g["cm_dw"][None]
    g_small["ff_dw"] = jnp.stack([_deinterleave_cols(g["ff_dw"][l]) for l in range(g["ff_dw"].shape[0])])
    small_shapes = [g_small[n].shape for n in _SMALL]
    gs_parts = _all_gather(_flat_pack([g_small[n] for n in _SMALL], D), name="gather_small_grads")

    def embed(t, n):
        if n not in _SMALL_SHARDED:
            return t
        full_shape = t.shape[:-1] + (t.shape[-1] * N_DEV,)
        return lax.dynamic_update_slice_in_dim(jnp.zeros(full_shape, F32), t, me * t.shape[-1], axis=t.ndim - 1)

    packs = [_flat_pack([embed(tree[n], n) for n in _SMALL], D) for tree in (w, m, v)]
    gs, ds_, ms, vs = _adamw(packs[0], packs[1], packs[2], [gs_parts] * N_DEV, jnp.arange(N_DEV, dtype=jnp.int32),
                             name="adamw_small")
    small = []
    for t in (gs, ds_, ms, vs):
        un = dict(zip(_SMALL, _flat_unpack(t, small_shapes)))
        for n in _SMALL_SHARDED:
            width = w[n].shape[-1]
            un[n] = lax.dynamic_slice_in_dim(un[n], me * width, width, axis=un[n].ndim - 1)
        small.append({n: un[n].reshape(w[n].shape) for n in _SMALL})

    outs = [loss, grad_x[None]]
    for k in range(4):
        for n in _ORDER:
            outs.append(big[k][n] if n in _BIG else small[k][n])
    return tuple(outs)
```

```python
import functools

import jax
import jax.numpy as jnp
import numpy as np
from jax import lax
from jax.experimental import pallas as pl
from jax.experimental.pallas import tpu as pltpu

F32 = jnp.float32
MXU_DTYPE = jnp.bfloat16
WIRE_DTYPE = jnp.bfloat16
EPS = 1e-6
NEG = -1e30
HEAD_DIM = 128
BLOCK = 128
DILATED_GROUPS = ((128, 1), (512, 4), (2048, 16))
ALIBI_MAX = 8.0
CONV_KERNEL = 31
CONV_HALO = 32
CONV_ROWS = 64
FFN_KERNEL = 3
FFN_HALO = 16
FFN_ROWS = 64
ADAM_LR, ADAM_B1, ADAM_B2, ADAM_EPS, ADAM_WD, ADAM_STEP = 0.001, 0.9, 0.999, 1e-08, 0.01, 10
V7X_VMEM_BYTES = 64 * 1024 * 1024
VMEM_LIMIT = V7X_VMEM_BYTES * 3 // 4
N_DEV = 8
MESH = pl.DeviceIdType.MESH


def _pick(n, target, align):
    if n <= target:
        return n
    best = None
    for t in range(align, target + 1, align):
        if n % t == 0:
            best = t
    assert best is not None, (n, target, align)
    return best


def _params(*sem, vmem=VMEM_LIMIT):
    return pltpu.CompilerParams(dimension_semantics=sem, vmem_limit_bytes=vmem)


def _sigmoid(x):
    return 1.0 / (1.0 + jnp.exp(-x))


_DIMS = {"nn": ((1,), (0,)), "nt": ((1,), (1,)), "tn": ((0,), (0,))}


def _mm(a, b, *, mode, out_dtype, name, tm=1024, tn=1024, tk=None, bias=None, residual=None, b_off=0, b_len=None,
        post=None, keep_main=True):
    if mode == "tn":
        K, M = a.shape
    else:
        M, K = a.shape
    if mode == "nt":
        N = b.shape[0] if b_len is None else b_len
    else:
        N = b.shape[1]
    if b_len is not None:
        assert mode == "nt" or (mode == "nn" and K == b_len)
    tm = _pick(M, tm, 128 if mode == "tn" else 16)
    tn = _pick(N, tn, 128)
    tk = K if tk is None else _pick(K, tk, 128 if mode != "tn" else 16)
    nk = K // tk
    unit = tn if mode == "nt" else tk
    assert b_off % unit == 0
    kb0 = b_off // unit
    if mode == "tn":
        a_spec = pl.BlockSpec((tk, tm), lambda i, j, k: (k, i))
    else:
        a_spec = pl.BlockSpec((tm, tk), lambda i, j, k: (i, k))
    if mode == "nt":
        b_spec = pl.BlockSpec((tn, tk), lambda i, j, k: (j + kb0, k))
    else:
        b_spec = pl.BlockSpec((tk, tn), lambda i, j, k: (k + kb0, j))
    in_specs = [a_spec, b_spec]
    args = [a, b]
    if bias is not None:
        in_specs.append(pl.BlockSpec((1, tn), lambda i, j, k: (0, j)))
        args.append(bias)
    if residual is not None:
        in_specs.append(pl.BlockSpec((tm, tn), lambda i, j, k: (i, j)))
        args.append(residual)
    has_bias, has_res = bias is not None, residual is not None
    kinds = {"tile": ((tm, tn), lambda i, j, k: (i, j)), "row": ((1, tn), lambda i, j, k: (0, j)),
             "lanes": ((8, 128), lambda i, j, k: (0, 0))}
    post_in = [] if post is None else post.ins
    post_out = [] if post is None else post.outs
    if post is not None:
        assert tn == N
        for arr, kind in post_in:
            in_specs.append(pl.BlockSpec(*kinds[kind]))
            args.append(arr)
    out_specs = [pl.BlockSpec((tm, tn), lambda i, j, k: (i, j))] if keep_main else []
    out_shape = [jax.ShapeDtypeStruct((M, N), out_dtype)] if keep_main else []
    for kind, dtype in post_out:
        out_specs.append(pl.BlockSpec(*kinds[kind]))
        out_shape.append(jax.ShapeDtypeStruct({"tile": (M, N), "row": (1, N), "lanes": (8, 128)}[kind], dtype))
    accumulates = any(kind != "tile" for kind, _ in post_out)

    def body(*refs):
        a_ref, b_ref = refs[0], refs[1]
        pos = 2
        bias_ref = res_ref = None
        if has_bias:
            bias_ref = refs[pos]
            pos += 1
        if has_res:
            res_ref = refs[pos]
            pos += 1
        pin_refs = refs[pos:pos + len(post_in)]
        pos += len(post_in)
        o_ref = refs[pos] if keep_main else None
        pos += 1 if keep_main else 0
        pout_refs = refs[pos:pos + len(post_out)]
        pos += len(post_out)
        acc_ref = refs[pos] if nk > 1 else None

        def finish(acc):
            if has_bias:
                acc = acc + bias_ref[...]
            if has_res:
                acc = acc + res_ref[...]
            if keep_main:
                o_ref[...] = acc.astype(o_ref.dtype)
            if post is not None:
                post.fn(acc, pin_refs, pout_refs, pl.program_id(0) == 0)

        part = lax.dot_general(a_ref[...].astype(MXU_DTYPE), b_ref[...].astype(MXU_DTYPE), (_DIMS[mode], ((), ())),
                               preferred_element_type=F32)
        if nk == 1:
            finish(part)
        else:
            k = pl.program_id(2)

            @pl.when(k == 0)
            def _():
                acc_ref[...] = part

            @pl.when(jnp.logical_and(k > 0, k < nk - 1))
            def _():
                acc_ref[...] += part

            @pl.when(k == nk - 1)
            def _():
                finish(acc_ref[...] + part)

    outs = pl.pallas_call(
        body, name=name, grid=(M // tm, N // tn, nk), in_specs=in_specs, out_specs=out_specs, out_shape=out_shape,
        scratch_shapes=[pltpu.VMEM((tm, tn), F32)] if nk > 1 else [],
        compiler_params=_params("arbitrary" if accumulates else "parallel", "parallel", "arbitrary"),
    )(*args)
    return outs[0] if post is None else outs


def _mm_post(a, b, post, *, name, tm=512, rows=256):
    M, K = a.shape
    N = b.shape[1]
    tm = _pick(M, tm, 16)
    rows = _pick(tm, rows, 16)
    kinds = {"tile": ((tm, N), lambda i: (i, 0)), "row": ((1, N), lambda i: (0, 0)), "lanes": ((8, 128), lambda i: (0, 0))}
    n_in = len(post.ins)

    def body(a_ref, b_ref, *rest):
        first = pl.program_id(0) == 0

        def product(c):
            return jnp.dot(a_ref[c * rows:(c + 1) * rows, :].astype(MXU_DTYPE), b_ref[...].astype(MXU_DTYPE),
                           preferred_element_type=F32)

        def chunk(refs, specs, c):
            return [r.at[pl.ds(c * rows, rows), :] if kind == "tile" else r for r, kind in zip(refs, specs)]

        acc = product(0)
        for c in range(tm // rows):
            nxt = product(c + 1) if c + 1 < tm // rows else None
            post.fn(acc, chunk(rest[:n_in], [k for _, k in post.ins], c), chunk(rest[n_in:], [k for k, _ in post.outs], c),
                    jnp.logical_and(first, c == 0))
            acc = nxt

    return pl.pallas_call(
        body, name=name, grid=(M // tm,),
        in_specs=[pl.BlockSpec((tm, K), lambda i: (i, 0)), pl.BlockSpec((K, N), lambda i: (0, 0), pipeline_mode=pl.Buffered(1))]
        + [pl.BlockSpec(*kinds[kind]) for _, kind in post.ins],
        out_specs=[pl.BlockSpec(*kinds[kind]) for kind, _ in post.outs],
        out_shape=[jax.ShapeDtypeStruct({"tile": (M, N), "row": (1, N), "lanes": (8, 128)}[kind], dtype) for kind, dtype in post.outs],
        compiler_params=_params("arbitrary", vmem=V7X_VMEM_BYTES * 7 // 8),
    )(a, b, *[arr for arr, _ in post.ins])


class _Post:
    def __init__(self, ins, outs, fn):
        self.ins, self.outs, self.fn = ins, outs, fn


def _accumulate(ref, value, first):
    @pl.when(first)
    def _():
        ref[...] = value

    @pl.when(jnp.logical_not(first))
    def _():
        ref[...] += value


def _post_rms(g):
    def fn(acc, ins, outs, first):
        r = lax.rsqrt(jnp.mean(acc * acc, axis=-1, keepdims=True) + EPS)
        outs[0][...] = (acc * r * ins[0][...]).astype(outs[0].dtype)

    return _Post([(g, "row")], [("tile", MXU_DTYPE)], fn)


def _post_loss(target):
    def fn(acc, ins, outs, first):
        e = acc - ins[0][...]
        dy = e * (1.0 / acc.shape[-1])
        outs[0][...] = dy
        outs[1][...] = dy.astype(outs[1].dtype)
        part = jnp.sum(jnp.sum(e * e, axis=0, keepdims=True), axis=1, keepdims=True) * (0.5 / acc.shape[-1])
        _accumulate(outs[2], jnp.broadcast_to(part, outs[2].shape), first)

    return _Post([(target, "tile")], [("tile", F32), ("tile", MXU_DTYPE), ("lanes", F32)], fn)


def _post_rms_bwd(x, g, dres):
    def fn(acc, ins, outs, first):
        xv = ins[0][...]
        r = lax.rsqrt(jnp.mean(xv * xv, axis=-1, keepdims=True) + EPS)
        xh = xv * r
        gy = acc * ins[1][...]
        dx = r * (gy - xh * jnp.mean(gy * xh, axis=-1, keepdims=True)) + ins[2][...]
        outs[0][...] = dx
        outs[1][...] = dx.astype(outs[1].dtype)
        _accumulate(outs[2], jnp.sum(acc * xh, axis=0, keepdims=True), first)
        _accumulate(outs[3], jnp.sum(dx, axis=0, keepdims=True), first)

    return _Post([(x, "tile"), (g, "row"), (dres, "tile")], [("tile", F32), ("tile", MXU_DTYPE), ("row", F32), ("row", F32)], fn)


SUB_TILE = 512


def _sub_spec(dil, ts, cols):
    return pl.BlockSpec((dil, ts // dil, cols), lambda i: (0, i, 0))


def _tok_to_sub(tok_ref, dst_ref, dil):
    nc, ts, _ = tok_ref.shape
    for c in range(nc):
        for r in range(dil):
            dst_ref[r, :, c * 128:(c + 1) * 128] = tok_ref.at[c][pl.ds(r, ts // dil, stride=dil), :].astype(dst_ref.dtype)


def _sub_to_tok(src_ref, tok_ref, dil):
    nc, ts, _ = tok_ref.shape
    for c in range(nc):
        for r in range(dil):
            tok_ref.at[c][pl.ds(r, ts // dil, stride=dil), :] = src_ref[r, :, c * 128:(c + 1) * 128].astype(F32)


def _rms_fwd(x, g, *, name, subs=()):
    S, D = x.shape
    ts = _pick(S, SUB_TILE, 16 * max(subs, default=1))
    NC = D // 128

    def body(x_ref, g_ref, h_ref, *rest):
        xv = x_ref[...]
        r = lax.rsqrt(jnp.mean(xv * xv, axis=-1, keepdims=True) + EPS)
        h = xv * r * g_ref[...]
        h_ref[...] = h.astype(h_ref.dtype)
        if subs:
            tok_ref = rest[-1]
            for c in range(NC):
                tok_ref[c] = h[:, c * 128:(c + 1) * 128]
            for dil, dst_ref in zip(subs, rest):
                _tok_to_sub(tok_ref, dst_ref, dil)

    row = pl.BlockSpec((ts, D), lambda i: (i, 0))
    outs = pl.pallas_call(
        body, name=name, grid=(S // ts,),
        in_specs=[row, pl.BlockSpec((1, D), lambda i: (0, 0))],
        out_specs=[row] + [_sub_spec(dil, ts, D) for dil in subs],
        out_shape=[jax.ShapeDtypeStruct((S, D), MXU_DTYPE)] + [jax.ShapeDtypeStruct((dil, S // dil, D), MXU_DTYPE) for dil in subs],
        scratch_shapes=[pltpu.VMEM((NC, ts, 128), F32)] if subs else [],
        compiler_params=_params("parallel"),
    )(x, g)
    return outs if subs else outs[0]


def _rms_bwd(x, g, dhs, dres, *, name, dh_subs=()):
    S, D = x.shape
    ts = _pick(S, SUB_TILE, 16 * max([dil for _, dil in dh_subs], default=1))
    n_dh, n_sub = len(dhs), len(dh_subs)
    NC = D // 128

    def body(*refs):
        x_ref, g_ref = refs[0], refs[1]
        dh_refs = refs[2:2 + n_dh]
        sub_refs = refs[2 + n_dh:2 + n_dh + n_sub]
        dres_ref, dx_ref, dxb_ref, dg_ref, cs_ref = refs[2 + n_dh + n_sub:7 + n_dh + n_sub]
        i = pl.program_id(0)
        xv = x_ref[...]
        r = lax.rsqrt(jnp.mean(xv * xv, axis=-1, keepdims=True) + EPS)
        xh = xv * r
        dhv = dh_refs[0][...].astype(F32)
        for t in dh_refs[1:]:
            dhv = dhv + t[...].astype(F32)
        for (_, dil), sub_ref in zip(dh_subs, sub_refs):
            tok_ref = refs[-1]
            _sub_to_tok(sub_ref, tok_ref, dil)
            dhv = dhv + jnp.concatenate([tok_ref[c] for c in range(NC)], axis=1)
        gy = dhv * g_ref[...]
        dx = r * (gy - xh * jnp.mean(gy * xh, axis=-1, keepdims=True)) + dres_ref[...]
        dx_ref[...] = dx
        dxb_ref[...] = dx.astype(dxb_ref.dtype)
        dg = jnp.sum(dhv * xh, axis=0, keepdims=True)
        cs = jnp.sum(dx, axis=0, keepdims=True)

        @pl.when(i == 0)
        def _():
            dg_ref[...] = dg
            cs_ref[...] = cs

        @pl.when(i > 0)
        def _():
            dg_ref[...] += dg
            cs_ref[...] += cs

    row = pl.BlockSpec((ts, D), lambda i: (i, 0))
    vec = pl.BlockSpec((1, D), lambda i: (0, 0))
    return pl.pallas_call(
        body, name=name, grid=(S // ts,),
        in_specs=[row, vec] + [row] * n_dh + [_sub_spec(dil, ts, D) for _, dil in dh_subs] + [row],
        out_specs=[row, row, vec, vec],
        out_shape=[jax.ShapeDtypeStruct((S, D), F32), jax.ShapeDtypeStruct((S, D), MXU_DTYPE),
                   jax.ShapeDtypeStruct((1, D), F32), jax.ShapeDtypeStruct((1, D), F32)],
        scratch_shapes=[pltpu.VMEM((NC, ts, 128), F32)] if n_sub else [],
        compiler_params=_params("arbitrary"),
    )(x, g, *dhs, *[a for a, _ in dh_subs], dres)


def _conv_phases(ph_ref, ts):
    n = ts + CONV_HALO - 8
    for b in range(1, 8):
        ph_ref[b, 0:n, :] = ph_ref[0, pl.ds(b, n), :]


def _phase_taps(base, step=1):
    groups = {}
    for k in range(CONV_KERNEL):
        a, b = divmod(base + step * k, 8)
        groups.setdefault(b, []).append((a, k))
    out = []
    for b in sorted(groups):
        ak = sorted(groups[b])
        assert [a for a, _ in ak] == list(range(ak[0][0], ak[0][0] + len(ak)))
        out.append((b, ak[0][0], [k for _, k in ak]))
    return out


def _cm_fwd(u, dw, dw_b, ln_g, ln_b, *, name, exchange=None):
    S, D2 = u.shape
    D = D2 // 2
    ts = _pick(S, 256, CONV_HALO)
    hb = ts // CONV_HALO
    ex_in, ex_out, ex_scr = ([], [], []) if exchange is None else (exchange.operands, exchange.out_shapes, exchange.scratch)

    def body(u_ref, up_ref, dw_ref, dwb_ref, g_ref, b_ref, *rest):
        xi = rest[:len(ex_in)]
        c_ref, s_ref = rest[len(ex_in):len(ex_in) + 2]
        xo = rest[len(ex_in) + 2:len(ex_in) + 2 + len(ex_out)]
        ext_ref = rest[len(ex_in) + 2 + len(ex_out)]
        i = pl.program_id(0)
        if exchange is not None:
            exchange.emit(i, S // ts, xi, xo, rest[len(ex_in) + 3 + len(ex_out):])
        prev = up_ref[:, :D] * _sigmoid(up_ref[:, D:])
        ext_ref[0:CONV_HALO, :] = jnp.where(i > 0, prev, 0.0)
        ext_ref[CONV_HALO:CONV_HALO + ts, :] = u_ref[:, :D] * _sigmoid(u_ref[:, D:])
        for cc in range(D // 128):
            sl = slice(cc * 128, (cc + 1) * 128)
            acc = jnp.zeros((ts, 128), F32) + dwb_ref[:, sl]
            for k in range(CONV_KERNEL):
                acc = acc + dw_ref[k:k + 1, sl] * ext_ref[pl.ds(CONV_HALO - (CONV_KERNEL - 1) + k, ts), sl]
            c_ref[:, sl] = acc
        c = c_ref[...]
        mu = jnp.mean(c, axis=-1, keepdims=True)
        xc = c - mu
        rstd = lax.rsqrt(jnp.mean(xc * xc, axis=-1, keepdims=True) + EPS)
        y = xc * rstd * g_ref[...] + b_ref[...]
        s_ref[...] = (y * _sigmoid(y)).astype(s_ref.dtype)

    vec = pl.BlockSpec((1, D), lambda i: (0, 0))
    return pl.pallas_call(
        body, name=name, grid=(S // ts,),
        in_specs=[pl.BlockSpec((ts, D2), lambda i: (i, 0)),
                  pl.BlockSpec((CONV_HALO, D2), lambda i: (jnp.maximum(i * hb - 1, 0), 0)),
                  pl.BlockSpec((CONV_KERNEL, D), lambda i: (0, 0)), vec, vec, vec] + [_ANY] * len(ex_in),
        out_specs=[pl.BlockSpec((ts, D), lambda i: (i, 0)), pl.BlockSpec((ts, D), lambda i: (i, 0))] + [_ANY] * len(ex_out),
        out_shape=[jax.ShapeDtypeStruct((S, D), F32), jax.ShapeDtypeStruct((S, D), MXU_DTYPE)] + list(ex_out),
        scratch_shapes=[pltpu.VMEM((ts + CONV_HALO, D), F32)] + list(ex_scr),
        compiler_params=_params("parallel" if exchange is None else "arbitrary"),
    )(u, u, dw, dw_b, ln_g, ln_b, *ex_in)


def _cm_ln_bwd(c, ds, ln_g, ln_b, *, name):
    S, D = c.shape
    ts = _pick(S, 512, 16)

    def body(c_ref, ds_ref, g_ref, b_ref, dc_ref, dg_ref, db_ref):
        i = pl.program_id(0)
        cv = c_ref[...]
        mu = jnp.mean(cv, axis=-1, keepdims=True)
        xc = cv - mu
        rstd = lax.rsqrt(jnp.mean(xc * xc, axis=-1, keepdims=True) + EPS)
        xh = xc * rstd
        y = xh * g_ref[...] + b_ref[...]
        sg = _sigmoid(y)
        dy = ds_ref[...].astype(F32) * (sg * (1.0 + y * (1.0 - sg)))
        gy = dy * g_ref[...]
        dc_ref[...] = rstd * (gy - jnp.mean(gy, axis=-1, keepdims=True) - xh * jnp.mean(gy * xh, axis=-1, keepdims=True))
        dg = jnp.sum(dy * xh, axis=0, keepdims=True)
        db = jnp.sum(dy, axis=0, keepdims=True)

        @pl.when(i == 0)
        def _():
            dg_ref[...] = dg
            db_ref[...] = db

        @pl.when(i > 0)
        def _():
            dg_ref[...] += dg
            db_ref[...] += db

    row = pl.BlockSpec((ts, D), lambda i: (i, 0))
    vec = pl.BlockSpec((1, D), lambda i: (0, 0))
    return pl.pallas_call(
        body, name=name, grid=(S // ts,), in_specs=[row, row, vec, vec], out_specs=[row, vec, vec],
        out_shape=[jax.ShapeDtypeStruct((S, D), F32), jax.ShapeDtypeStruct((1, D), F32), jax.ShapeDtypeStruct((1, D), F32)],
        compiler_params=_params("arbitrary"),
    )(c, ds, ln_g, ln_b)


def _cm_conv_bwd(dc, u, dw, *, name, exchange=None):
    ex_in, ex_out, ex_scr = ([], [], []) if exchange is None else (exchange.operands, exchange.out_shapes, exchange.scratch)
    S, D2 = u.shape
    D = D2 // 2
    ts = _pick(S, 256, CONV_HALO)
    hb = ts // CONV_HALO
    n_t = S // ts
    last_h = S // CONV_HALO - 1

    rc = _pick(ts, CONV_ROWS, 8)

    def fold8(v):
        out = v[0:8]
        for j in range(1, v.shape[0] // 8):
            out = out + v[8 * j:8 * j + 8]
        return out

    def body(dc_ref, dcn_ref, u_ref, up_ref, dw_ref, *rest):
        xi = rest[:len(ex_in)]
        du_ref, ddw_ref, ddwb_ref, dbin_ref = rest[len(ex_in):len(ex_in) + 4]
        xo = rest[len(ex_in) + 4:len(ex_in) + 4 + len(ex_out)]
        dph_ref, gph_ref, dgl_ref = rest[len(ex_in) + 4 + len(ex_out):len(ex_in) + 7 + len(ex_out)]
        i = pl.program_id(0)
        if exchange is not None:
            exchange.emit(i, n_t, xi, xo, rest[len(ex_in) + 7 + len(ex_out):])
        dph_ref[0, 0:ts, :] = dc_ref[...]
        dph_ref[0, ts:ts + CONV_HALO, :] = jnp.where(i < n_t - 1, dcn_ref[...], 0.0)
        prev = up_ref[:, :D] * _sigmoid(up_ref[:, D:])
        gph_ref[0, 0:CONV_HALO, :] = jnp.where(i > 0, prev, 0.0)
        gph_ref[0, CONV_HALO:CONV_HALO + ts, :] = u_ref[:, :D] * _sigmoid(u_ref[:, D:])
        _conv_phases(dph_ref, ts)
        _conv_phases(gph_ref, ts)

        @pl.when(i == 0)
        def _():
            ddw_ref[...] = jnp.zeros_like(ddw_ref)
            ddwb_ref[...] = jnp.zeros_like(ddwb_ref)
            dbin_ref[...] = jnp.zeros_like(dbin_ref)

        for cc in range(D // 128):
            sl = slice(cc * 128, (cc + 1) * 128)
            sl2 = slice(D + cc * 128, D + (cc + 1) * 128)
            wk = [dw_ref[k:k + 1, sl] for k in range(CONV_KERNEL)]
            acc_a, acc_g = jnp.zeros((8, 128), F32), jnp.zeros((8, 128), F32)
            dgl = jnp.zeros((ts, 128), F32)
            for b, a0, taps in _phase_taps(CONV_KERNEL - 1, -1):
                for j, k in enumerate(taps):
                    dgl = dgl + wk[k] * dph_ref[b, 8 * (a0 + j):8 * (a0 + j) + ts, sl]
            dgl_ref[...] = dgl
            for r0 in range(0, ts, rc):
                dglu = dgl_ref[r0:r0 + rc, :]
                av = u_ref[r0:r0 + rc, sl]
                sg = _sigmoid(u_ref[r0:r0 + rc, sl2])
                da = dglu * sg
                dg = dglu * av * sg * (1.0 - sg)
                du_ref[r0:r0 + rc, sl] = da.astype(du_ref.dtype)
                du_ref[r0:r0 + rc, sl2] = dg.astype(du_ref.dtype)
                acc_a = acc_a + fold8(da)
                acc_g = acc_g + fold8(dg)
            dbin_ref[:, sl] += jnp.sum(acc_a, axis=0, keepdims=True)
            dbin_ref[:, sl2] += jnp.sum(acc_g, axis=0, keepdims=True)
            for gi, (b, a0, taps) in enumerate(_phase_taps(CONV_HALO - (CONV_KERNEL - 1))):
                accs = [jnp.zeros((8, 128), F32) for _ in taps]
                accb = jnp.zeros((8, 128), F32)
                for r0 in range(0, ts, rc):
                    dcc = dph_ref[0, r0:r0 + rc, sl]
                    win = gph_ref[b, 8 * a0 + r0:8 * (a0 + len(taps) - 1) + r0 + rc, sl]
                    for j in range(len(taps)):
                        accs[j] = accs[j] + fold8(dcc * win[8 * j:8 * j + rc])
                    if gi == 0:
                        accb = accb + fold8(dcc)
                for j, k in enumerate(taps):
                    ddw_ref[k:k + 1, sl] += jnp.sum(accs[j], axis=0, keepdims=True)
                if gi == 0:
                    ddwb_ref[:, sl] += jnp.sum(accb, axis=0, keepdims=True)

    return pl.pallas_call(
        body, name=name, grid=(n_t,),
        in_specs=[pl.BlockSpec((ts, D), lambda i: (i, 0)),
                  pl.BlockSpec((CONV_HALO, D), lambda i: (jnp.minimum((i + 1) * hb, last_h), 0)),
                  pl.BlockSpec((ts, D2), lambda i: (i, 0)),
                  pl.BlockSpec((CONV_HALO, D2), lambda i: (jnp.maximum(i * hb - 1, 0), 0)),
                  pl.BlockSpec((CONV_KERNEL, D), lambda i: (0, 0))] + [_ANY] * len(ex_in),
        out_specs=[pl.BlockSpec((ts, D2), lambda i: (i, 0)), pl.BlockSpec((CONV_HALO, D), lambda i: (0, 0)),
                   pl.BlockSpec((1, D), lambda i: (0, 0)), pl.BlockSpec((1, D2), lambda i: (0, 0))] + [_ANY] * len(ex_out),
        out_shape=[jax.ShapeDtypeStruct((S, D2), MXU_DTYPE), jax.ShapeDtypeStruct((CONV_HALO, D), F32),
                   jax.ShapeDtypeStruct((1, D), F32), jax.ShapeDtypeStruct((1, D2), F32)] + list(ex_out),
        scratch_shapes=[pltpu.VMEM((8, ts + CONV_HALO, D), F32), pltpu.VMEM((8, ts + CONV_HALO, D), F32),
                        pltpu.VMEM((ts, 128), F32)] + list(ex_scr),
        compiler_params=_params("arbitrary"),
    )(dc, dc, u, u, dw, *ex_in)


def _ffn_cols(F2):
    return _pick(F2, 1024, 256)


def _ffn_up_act(hf, wupT, dw, dw_b, *, name):
    S, D = hf.shape
    F2 = wupT.shape[0]
    ts = _pick(S, 512, 16)
    tc = _ffn_cols(F2)
    n_ct = F2 // tc
    hb = ts // FFN_HALO
    rc = _pick(ts, FFN_ROWS, 16)

    def body(h_ref, hp_ref, wt_ref, w_ref, b_ref, up_ref, a_ref, he_ref, ext_ref):
        i = pl.program_id(0)
        he_ref[0:FFN_HALO, :] = hp_ref[...]
        he_ref[FFN_HALO:FFN_HALO + ts, :] = h_ref[...]

        def product(j):
            return lax.dot_general(he_ref[...].astype(MXU_DTYPE), wt_ref[j * tc:(j + 1) * tc, :].astype(MXU_DTYPE),
                                   (_DIMS["nt"], ((), ())), preferred_element_type=F32)

        def conv_gate(j, res):
            ext = ext_ref.at[j % 2]
            upv = res.astype(up_ref.dtype)
            up_ref[:, j * tc:(j + 1) * tc] = upv[FFN_HALO:FFN_HALO + ts]
            ext[0:FFN_HALO, :] = jnp.where(i > 0, upv[0:FFN_HALO].astype(F32), 0.0)
            ext[FFN_HALO:FFN_HALO + ts, :] = upv[FFN_HALO:FFN_HALO + ts].astype(F32)
            for q in range(tc // 256):
                sls = [slice(q * 256 + half * 128, q * 256 + half * 128 + 128) for half in range(2)]
                gls = [slice(j * tc + sl.start, j * tc + sl.stop) for sl in sls]
                wk = [[w_ref[k:k + 1, gl] for k in range(FFN_KERNEL)] for gl in gls]
                bb = [b_ref[:, gl] for gl in gls]
                for r0 in range(0, ts, rc):
                    gt, vl = [bb[h] + sum(wk[h][k] * ext[pl.ds(FFN_HALO + r0 - 2 + k, rc), sls[h]] for k in range(FFN_KERNEL))
                              for h in range(2)]
                    a_ref[r0:r0 + rc, j * (tc // 2) + q * 128:j * (tc // 2) + (q + 1) * 128] = (gt * _sigmoid(gt) * vl).astype(a_ref.dtype)

        res = product(0)
        for j in range(n_ct):
            nxt = product(j + 1) if j + 1 < n_ct else None
            conv_gate(j, res)
            res = nxt

    return pl.pallas_call(
        body, name=name, grid=(S // ts,),
        in_specs=[pl.BlockSpec((ts, D), lambda i: (i, 0)),
                  pl.BlockSpec((FFN_HALO, D), lambda i: (jnp.maximum(i * hb - 1, 0), 0)),
                  pl.BlockSpec((F2, D), lambda i: (0, 0), pipeline_mode=pl.Buffered(1)),
                  pl.BlockSpec((FFN_KERNEL, F2), lambda i: (0, 0)), pl.BlockSpec((1, F2), lambda i: (0, 0))],
        out_specs=[pl.BlockSpec((ts, F2), lambda i: (i, 0)), pl.BlockSpec((ts, F2 // 2), lambda i: (i, 0))],
        out_shape=[jax.ShapeDtypeStruct((S, F2), MXU_DTYPE), jax.ShapeDtypeStruct((S, F2 // 2), MXU_DTYPE)],
        scratch_shapes=[pltpu.VMEM((ts + FFN_HALO, D), hf.dtype), pltpu.VMEM((2, ts + FFN_HALO, tc), F32)],
        compiler_params=_params("parallel", vmem=V7X_VMEM_BYTES * 7 // 8),
    )(hf, hf, wupT, dw, dw_b)


def _ffn_act_bwd(up, dyb, wdown, dw, dw_b, *, name, exchange=None):
    ex_in, ex_out, ex_scr = ([], [], []) if exchange is None else (exchange.operands, exchange.out_shapes, exchange.scratch)
    S, F2 = up.shape
    D = dyb.shape[1]
    ts = _pick(S, 512, 16)
    tc = _ffn_cols(F2)
    hb = ts // FFN_HALO
    n_t = S // ts
    last_h = S // FFN_HALO - 1
    E = ts + FFN_HALO

    rc = _pick(ts, FFN_ROWS, 16)

    def fold8(v):
        out = v[0:8]
        for j in range(1, v.shape[0] // 8):
            out = out + v[8 * j:8 * j + 8]
        return out

    def body(u_ref, up_ref, un_ref, dy_ref, dyn_ref, wd_ref, w_ref, b_ref, *rest):
        xi = rest[:len(ex_in)]
        dup_ref, ddw_ref, ddb_ref = rest[len(ex_in):len(ex_in) + 3]
        xo = rest[len(ex_in) + 3:len(ex_in) + 3 + len(ex_out)]
        ue_ref, dcv_ref, dye_ref, da_ref = rest[len(ex_in) + 3 + len(ex_out):len(ex_in) + 7 + len(ex_out)]
        i = pl.program_id(1)
        if exchange is not None:
            exchange.emit(pl.program_id(0) * n_t + i, (F2 // tc) * n_t, xi, xo, rest[len(ex_in) + 7 + len(ex_out):])
        dye_ref[0:ts, :] = dy_ref[...]
        dye_ref[ts:E, :] = jnp.where(i < n_t - 1, dyn_ref[...], jnp.zeros_like(dyn_ref))
        da_ref[...] = lax.dot_general(dye_ref[...].astype(MXU_DTYPE), wd_ref[...].astype(MXU_DTYPE), (_DIMS["nt"], ((), ())),
                                      preferred_element_type=F32).astype(da_ref.dtype)
        ue_ref[0:FFN_HALO, :] = jnp.where(i > 0, up_ref[...].astype(F32), 0.0)
        ue_ref[FFN_HALO:FFN_HALO + ts, :] = u_ref[...].astype(F32)
        ue_ref[FFN_HALO + ts:FFN_HALO + ts + FFN_HALO, :] = jnp.where(i < n_t - 1, un_ref[...].astype(F32), 0.0)

        @pl.when(i == 0)
        def _():
            ddw_ref[...] = jnp.zeros_like(ddw_ref)
            ddb_ref[...] = jnp.zeros_like(ddb_ref)

        for q in range(tc // 256):
            sls = [slice(q * 256 + half * 128, q * 256 + half * 128 + 128) for half in range(2)]
            qs = slice(q * 128, (q + 1) * 128)
            wk = [[w_ref[k:k + 1, sl] for k in range(FFN_KERNEL)] for sl in sls]
            bb = [b_ref[:, sl] for sl in sls]
            acc = [[jnp.zeros((8, 128), F32) for _ in range(FFN_KERNEL)] for _ in range(2)]
            accb = [jnp.zeros((8, 128), F32) for _ in range(2)]
            for r0, rows in [(r, rc) for r in range(0, ts, rc)] + [(ts, FFN_HALO)]:
                xs = [[ue_ref[pl.ds(FFN_HALO + r0 - 2 + k, rows), sls[h]] for k in range(FFN_KERNEL)] for h in range(2)]
                gt, vl = [bb[h] + sum(wk[h][k] * xs[h][k] for k in range(FFN_KERNEL)) for h in range(2)]
                sg = _sigmoid(gt)
                dae = da_ref[r0:r0 + rows, qs].astype(F32)
                dcv = [dae * vl * (sg * (1.0 + gt * (1.0 - sg))), dae * (gt * sg)]
                for h in range(2):
                    dcv_ref[r0:r0 + rows, sls[h]] = dcv[h]
                    if r0 < ts:
                        for k in range(FFN_KERNEL):
                            acc[h][k] = acc[h][k] + fold8(dcv[h] * xs[h][k])
                        accb[h] = accb[h] + fold8(dcv[h])
            for r0 in range(0, ts, rc):
                for h in range(2):
                    dup = sum(wk[h][2 - j] * dcv_ref[pl.ds(r0 + j, rc), sls[h]] for j in range(FFN_KERNEL))
                    dup_ref[r0:r0 + rc, sls[h]] = dup.astype(dup_ref.dtype)
            for h in range(2):
                for k in range(FFN_KERNEL):
                    ddw_ref[k:k + 1, sls[h]] += jnp.sum(acc[h][k], axis=0, keepdims=True)
                ddb_ref[:, sls[h]] += jnp.sum(accb[h], axis=0, keepdims=True)

    return pl.pallas_call(
        body, name=name, grid=(F2 // tc, n_t),
        in_specs=[pl.BlockSpec((ts, tc), lambda j, i: (i, j)),
                  pl.BlockSpec((FFN_HALO, tc), lambda j, i: (jnp.maximum(i * hb - 1, 0), j)),
                  pl.BlockSpec((FFN_HALO, tc), lambda j, i: (jnp.minimum((i + 1) * hb, last_h), j)),
                  pl.BlockSpec((ts, D), lambda j, i: (i, 0)),
                  pl.BlockSpec((FFN_HALO, D), lambda j, i: (jnp.minimum((i + 1) * hb, last_h), 0)),
                  pl.BlockSpec((tc // 2, D), lambda j, i: (j, 0)),
                  pl.BlockSpec((FFN_KERNEL, tc), lambda j, i: (0, j)),
                  pl.BlockSpec((1, tc), lambda j, i: (0, j))] + [_ANY] * len(ex_in),
        out_specs=[pl.BlockSpec((ts, tc), lambda j, i: (i, j)), pl.BlockSpec((FFN_HALO, tc), lambda j, i: (0, j)),
                   pl.BlockSpec((1, tc), lambda j, i: (0, j))] + [_ANY] * len(ex_out),
        out_shape=[jax.ShapeDtypeStruct((S, F2), MXU_DTYPE), jax.ShapeDtypeStruct((FFN_HALO, F2), F32),
                   jax.ShapeDtypeStruct((1, F2), F32)] + list(ex_out),
        scratch_shapes=[pltpu.VMEM((ts + 2 * FFN_HALO, tc), F32), pltpu.VMEM((E, tc), F32), pltpu.VMEM((E, D), dyb.dtype),
                        pltpu.VMEM((E, tc // 2), MXU_DTYPE)] + list(ex_scr),
        compiler_params=_params("parallel" if exchange is None else "arbitrary", "arbitrary"),
    )(up, up, up, dyb, dyb, wdown, dw, dw_b, *ex_in)


def _slopes(n_heads_total):
    return np.asarray(2.0 ** (-ALIBI_MAX * (np.arange(n_heads_total, dtype=np.float32) + 1.0) / n_heads_total), np.float32)


def _qkv_proj(h, w_qkvT, *, grp, name):
    S, D = h.shape
    H = D // HEAD_DIM
    tm = _pick(S, 1024, 16)

    rows = _pick(tm, 256, 16)

    def body(a_ref, b_ref, o_ref, r_ref):
        j = pl.program_id(1)
        r_ref[...] = jnp.zeros_like(r_ref)

        def product(c):
            return lax.dot_general(a_ref[c * rows:(c + 1) * rows, :].astype(MXU_DTYPE), b_ref[...].astype(MXU_DTYPE),
                                   (_DIMS["nt"], ((), ())), preferred_element_type=F32)

        @pl.when(j < 2)
        def _():
            acc = product(0)
            for c in range(tm // rows):
                nxt = product(c + 1) if c + 1 < tm // rows else None
                rs = slice(c * rows, (c + 1) * rows)
                for hd in range(H):
                    hs = slice(hd * HEAD_DIM, (hd + 1) * HEAD_DIM)
                    xv = acc[:, hs]
                    r = lax.rsqrt(jnp.mean(xv * xv, axis=-1, keepdims=True) + EPS)
                    o_ref[rs, hs] = (xv * r).astype(o_ref.dtype)
                    r_ref[rs, hd:hd + 1] = r
                acc = nxt

        @pl.when(j == 2)
        def _():
            o_ref[...] = lax.dot_general(a_ref[...].astype(MXU_DTYPE), b_ref[...].astype(MXU_DTYPE), (_DIMS["nt"], ((), ())),
                                         preferred_element_type=F32).astype(o_ref.dtype)

    return pl.pallas_call(
        body, name=name, grid=(S // tm, 3),
        in_specs=[pl.BlockSpec((tm, D), lambda i, j: (i, 0)), pl.BlockSpec((D, D), lambda i, j: (grp * 3 + j, 0))],
        out_specs=[pl.BlockSpec((tm, D), lambda i, j: (i, j)), pl.BlockSpec((tm, HEAD_DIM), lambda i, j: (i, j))],
        out_shape=[jax.ShapeDtypeStruct((S, 3 * D), MXU_DTYPE), jax.ShapeDtypeStruct((S, 3 * HEAD_DIM), F32)],
        compiler_params=_params("parallel", "parallel"),
    )(h, w_qkvT)


def _band(b, dil):
    qi = lax.broadcasted_iota(jnp.int32, (BLOCK, 2 * BLOCK), 0)
    ki = lax.broadcasted_iota(jnp.int32, (BLOCK, 2 * BLOCK), 1)
    delta = qi + BLOCK - ki
    valid = (delta >= 0) & (delta <= BLOCK) & ((ki >= BLOCK) | (b > 0))
    return valid, (delta * dil).astype(F32)


def _attn_fwd(qkv, qg, kg, *, grp, name):
    S, W = qkv.shape
    D = W // 3
    H = D // HEAD_DIM
    dil = DILATED_GROUPS[grp][1]
    L = S // dil
    nb = L // BLOCK
    slopes = _slopes(3 * H)[grp * H:(grp + 1) * H]
    scale = HEAD_DIM ** -0.5

    def body(q_ref, kp_ref, kc_ref, vp_ref, vc_ref, qg_ref, kg_ref, o_ref, l_ref):
        b = pl.program_id(1)
        valid, dist = _band(b, dil)
        l_ref[...] = jnp.zeros_like(l_ref)
        ss = []
        for h in range(H):
            hs = slice(h * HEAD_DIM, (h + 1) * HEAD_DIM)
            qn = (q_ref[:, hs].astype(F32) * qg_ref[h:h + 1, :]).astype(MXU_DTYPE)
            kp = (kp_ref[:, hs].astype(F32) * kg_ref[h:h + 1, :]).astype(MXU_DTYPE)
            kc = (kc_ref[:, hs].astype(F32) * kg_ref[h:h + 1, :]).astype(MXU_DTYPE)
            ss.append(lax.dot_general(qn, jnp.concatenate([kp, kc], axis=0), (((1,), (1,)), ((), ())), preferred_element_type=F32))
        ps = []
        for h in range(H):
            s = jnp.where(valid, ss[h] * scale - float(slopes[h]) * dist, NEG)
            m = jnp.max(s, axis=-1, keepdims=True)
            p = jnp.exp(s - m)
            den = jnp.sum(p, axis=-1, keepdims=True)
            l_ref[:, h:h + 1] = m + jnp.log(den)
            ps.append((p.astype(MXU_DTYPE), den))
        for h in range(H):
            hs = slice(h * HEAD_DIM, (h + 1) * HEAD_DIM)
            pb, den = ps[h]
            v2 = jnp.concatenate([vp_ref[:, hs], vc_ref[:, hs]], axis=0).astype(MXU_DTYPE)
            o_ref[:, hs] = (jnp.dot(pb, v2, preferred_element_type=F32) / den).astype(o_ref.dtype)

    def cur(j):
        return lambda r, b: (r * nb + b, j)

    def prv(j):
        return lambda r, b: (r * nb + jnp.maximum(b - 1, 0), j)

    blk = (BLOCK, D)
    gain = pl.BlockSpec((H, HEAD_DIM), lambda r, b: (0, 0))
    return pl.pallas_call(
        body, name=name, grid=(dil, nb),
        in_specs=[pl.BlockSpec(blk, cur(0)), pl.BlockSpec(blk, prv(1)), pl.BlockSpec(blk, cur(1)),
                  pl.BlockSpec(blk, prv(2)), pl.BlockSpec(blk, cur(2)), gain, gain],
        out_specs=[pl.BlockSpec(blk, cur(0)), pl.BlockSpec((BLOCK, HEAD_DIM), cur(0))],
        out_shape=[jax.ShapeDtypeStruct((S, D), MXU_DTYPE), jax.ShapeDtypeStruct((S, HEAD_DIM), F32)],
        compiler_params=_params("parallel", "parallel"),
    )(qkv, qkv, qkv, qkv, qkv, qg, kg)


def _attn_merge(os_, ls_, dils, *, name):
    S, D = os_[0].shape
    H = D // HEAD_DIM
    G = len(dils)
    ts = _pick(S, SUB_TILE, 16 * max(dils))
    subs = [g for g in range(G) if dils[g] > 1]

    def body(*refs):
        o_refs, l_refs = refs[0:G], refs[G:2 * G]
        outb_ref = refs[2 * G]
        lt_refs = refs[2 * G + 1:3 * G + 1]
        scratch = refs[3 * G + 1:]
        lt_tok = scratch[0]
        o_tok = {g: scratch[1 + 2 * j] for j, g in enumerate(subs)}
        l_tok = {g: scratch[2 + 2 * j] for j, g in enumerate(subs)}
        for g in subs:
            _sub_to_tok(o_refs[g], o_tok[g], dils[g])
            _sub_to_tok(l_refs[g], l_tok[g], dils[g])
        lt_tok[0] = jnp.zeros((ts, HEAD_DIM), F32)
        for h in range(H):
            hs = slice(h * HEAD_DIM, (h + 1) * HEAD_DIM)
            ls = [l_tok[g][0][:, h:h + 1] if g in subs else l_refs[g][:, h:h + 1] for g in range(G)]
            ov = [o_tok[g][h] if g in subs else o_refs[g][:, hs].astype(F32) for g in range(G)]
            m = functools.reduce(jnp.maximum, ls)
            es = [jnp.exp(l - m) for l in ls]
            den = functools.reduce(lambda a, b: a + b, es)
            out = functools.reduce(lambda a, b: a + b, [e * o for e, o in zip(es, ov)]) / den
            outb_ref[:, hs] = out.astype(outb_ref.dtype)
            lt_tok.at[0][:, h:h + 1] = m + jnp.log(den)
        for g in range(G):
            if g in subs:
                _tok_to_sub(lt_tok, lt_refs[g], dils[g])
            else:
                lt_refs[g][...] = lt_tok[0]

    def spec(g, cols):
        return _sub_spec(dils[g], ts, cols) if g in subs else pl.BlockSpec((ts, cols), lambda i: (i, 0))

    def shape(g, cols, dtype):
        return jax.ShapeDtypeStruct((dils[g], S // dils[g], cols) if g in subs else (S, cols), dtype)

    def view(a, g):
        return a.reshape(dils[g], S // dils[g], a.shape[-1]) if g in subs else a

    outs = pl.pallas_call(
        body, name=name, grid=(S // ts,),
        in_specs=[spec(g, D) for g in range(G)] + [spec(g, HEAD_DIM) for g in range(G)],
        out_specs=[pl.BlockSpec((ts, D), lambda i: (i, 0))] + [spec(g, HEAD_DIM) for g in range(G)],
        out_shape=[jax.ShapeDtypeStruct((S, D), MXU_DTYPE)] + [shape(g, HEAD_DIM, F32) for g in range(G)],
        scratch_shapes=[pltpu.VMEM((1, ts, HEAD_DIM), F32)] + [pltpu.VMEM((H, ts, HEAD_DIM), F32), pltpu.VMEM((1, ts, HEAD_DIM), F32)] * len(subs),
        compiler_params=_params("parallel"),
    )(*[view(o, g) for g, o in enumerate(os_)], *[view(l, g) for g, l in enumerate(ls_)])
    return outs[0], [t.reshape(S, HEAD_DIM) for t in outs[1:]]


def _attn_delta(do, out, dils, *, name):
    S, D = out.shape
    H = D // HEAD_DIM
    G = len(dils)
    ts = _pick(S, SUB_TILE, 16 * max(dils))
    subs = [g for g in range(G) if dils[g] > 1]

    def body(do_ref, o_ref, *rest):
        d_refs = rest[0:G]
        dos_refs = rest[G:G + len(subs)]
        d_tok, do_tok = rest[G + len(subs):]
        d_tok[0] = jnp.zeros((ts, HEAD_DIM), F32)
        for h in range(H):
            hs = slice(h * HEAD_DIM, (h + 1) * HEAD_DIM)
            dov = do_ref[:, hs].astype(F32)
            do_tok[h] = dov
            d_tok.at[0][:, h:h + 1] = jnp.sum(dov * o_ref[:, hs].astype(F32), axis=-1, keepdims=True)
        for g in range(G):
            if g in subs:
                _tok_to_sub(d_tok, d_refs[g], dils[g])
            else:
                d_refs[g][...] = d_tok[0]
        for g, dst in zip(subs, dos_refs):
            _tok_to_sub(do_tok, dst, dils[g])

    def spec(g, cols):
        return _sub_spec(dils[g], ts, cols) if g in subs else pl.BlockSpec((ts, cols), lambda i: (i, 0))

    def shape(g, cols, dtype):
        return jax.ShapeDtypeStruct((dils[g], S // dils[g], cols) if g in subs else (S, cols), dtype)

    row = pl.BlockSpec((ts, D), lambda i: (i, 0))
    outs = pl.pallas_call(
        body, name=name, grid=(S // ts,), in_specs=[row, row],
        out_specs=[spec(g, HEAD_DIM) for g in range(G)] + [spec(g, D) for g in subs],
        out_shape=[shape(g, HEAD_DIM, F32) for g in range(G)] + [shape(g, D, do.dtype) for g in subs],
        scratch_shapes=[pltpu.VMEM((1, ts, HEAD_DIM), F32), pltpu.VMEM((H, ts, HEAD_DIM), F32)],
        compiler_params=_params("parallel"),
    )(do, out)
    deltas = [t.reshape(S, HEAD_DIM) for t in outs[0:G]]
    dos = {g: t.reshape(S, D) for g, t in zip(subs, outs[G:])}
    return deltas, [dos[g] if g in subs else do for g in range(G)]


def _attn_bwd(qkv, rqk, do, lse, delta, qg, kg, *, grp, name):
    S, W = qkv.shape
    D = W // 3
    H = D // HEAD_DIM
    dil = DILATED_GROUPS[grp][1]
    L = S // dil
    nb = L // BLOCK
    slopes = _slopes(3 * H)[grp * H:(grp + 1) * H]
    scale = HEAD_DIM ** -0.5

    def body(q_ref, qp_ref, kp_ref, kc_ref, vp_ref, vc_ref, do_ref, l_ref, dl_ref, rq_ref, rk_ref, qg_ref, kg_ref,
             out_ref, dqg_ref, dkg_ref, cq_ref, ck_ref, cv_ref, nq_ref, nk_ref, nv_ref, pk_ref, pv_ref):
        r = pl.program_id(0)
        b = pl.program_id(1)

        @pl.when(jnp.logical_and(r == 0, b == 0))
        def _():
            dqg_ref[...] = jnp.zeros_like(dqg_ref)
            dkg_ref[...] = jnp.zeros_like(dkg_ref)

        @pl.when(b < nb)
        def _():
            valid, dist = _band(b, dil)

            def operands(h):
                hs = slice(h * HEAD_DIM, (h + 1) * HEAD_DIM)
                qn = (q_ref[:, hs].astype(F32) * qg_ref[h:h + 1, :]).astype(MXU_DTYPE)
                kp = (kp_ref[:, hs].astype(F32) * kg_ref[h:h + 1, :]).astype(MXU_DTYPE)
                kc = (kc_ref[:, hs].astype(F32) * kg_ref[h:h + 1, :]).astype(MXU_DTYPE)
                k2 = jnp.concatenate([kp, kc], axis=0)
                v2 = jnp.concatenate([vp_ref[:, hs], vc_ref[:, hs]], axis=0).astype(MXU_DTYPE)
                return hs, qn, k2, v2, do_ref[:, hs].astype(MXU_DTYPE)

            sdp = []
            for h in range(H):
                hs, qn, k2, v2, doh = operands(h)
                s = lax.dot_general(qn, k2, (((1,), (1,)), ((), ())), preferred_element_type=F32)
                dp = lax.dot_general(doh, v2, (((1,), (1,)), ((), ())), preferred_element_type=F32)
                sdp.append((s, dp))
            pds = []
            for h in range(H):
                s, dp = sdp[h]
                s = jnp.where(valid, s * scale - float(slopes[h]) * dist, NEG)
                p = jnp.exp(s - l_ref[:, h:h + 1])
                pds.append((p.astype(MXU_DTYPE), (p * (dp - dl_ref[:, h:h + 1]) * scale).astype(MXU_DTYPE)))
            for h in range(H):
                hs, qn, k2, v2, doh = operands(h)
                pb, dsc = pds[h]
                nq_ref[:, hs] = jnp.dot(dsc, k2, preferred_element_type=F32)
                dk2 = lax.dot_general(dsc, qn, (((0,), (0,)), ((), ())), preferred_element_type=F32)
                dv2 = lax.dot_general(pb, doh, (((0,), (0,)), ((), ())), preferred_element_type=F32)
                pk_ref[:, hs] = dk2[0:BLOCK]
                nk_ref[:, hs] = dk2[BLOCK:2 * BLOCK]
                pv_ref[:, hs] = dv2[0:BLOCK]
                nv_ref[:, hs] = dv2[BLOCK:2 * BLOCK]

        @pl.when(b == nb)
        def _():
            pk_ref[...] = jnp.zeros_like(pk_ref)
            pv_ref[...] = jnp.zeros_like(pv_ref)

        @pl.when(b > 0)
        def _():
            for h in range(H):
                hs = slice(h * HEAD_DIM, (h + 1) * HEAD_DIM)
                for j, (xh_ref, r_ref, gain_ref, dgain_ref) in enumerate(((qp_ref, rq_ref, qg_ref, dqg_ref),
                                                                          (kp_ref, rk_ref, kg_ref, dkg_ref))):
                    dy = cq_ref[:, hs] if j == 0 else ck_ref[:, hs] + pk_ref[:, hs]
                    gain = gain_ref[h:h + 1, :]
                    xh = xh_ref[:, hs].astype(F32)
                    rr = r_ref[:, h:h + 1]
                    gy = dy * gain
                    dx = rr * (gy - xh * jnp.mean(gy * xh, axis=-1, keepdims=True))
                    out_ref[:, j * D + h * HEAD_DIM:j * D + (h + 1) * HEAD_DIM] = dx.astype(out_ref.dtype)
                    dgain_ref[h:h + 1, :] += jnp.sum(dy * xh, axis=0, keepdims=True)
                out_ref[:, 2 * D + h * HEAD_DIM:2 * D + (h + 1) * HEAD_DIM] = (cv_ref[:, hs] + pv_ref[:, hs]).astype(out_ref.dtype)

        @pl.when(b < nb)
        def _():
            cq_ref[...] = nq_ref[...]
            ck_ref[...] = nk_ref[...]
            cv_ref[...] = nv_ref[...]

    def cur(j):
        return lambda r, b: (r * nb + jnp.minimum(b, nb - 1), j)

    def prv(j):
        return lambda r, b: (r * nb + jnp.clip(b - 1, 0, nb - 1), j)

    blk = (BLOCK, D)
    lblk = pl.BlockSpec((BLOCK, HEAD_DIM), cur(0))
    gain = pl.BlockSpec((H, HEAD_DIM), lambda r, b: (0, 0))
    return pl.pallas_call(
        body, name=name, grid=(dil, nb + 1),
        in_specs=[pl.BlockSpec(blk, cur(0)), pl.BlockSpec(blk, prv(0)), pl.BlockSpec(blk, prv(1)), pl.BlockSpec(blk, cur(1)),
                  pl.BlockSpec(blk, prv(2)), pl.BlockSpec(blk, cur(2)), pl.BlockSpec(blk, cur(0)), lblk, lblk,
                  pl.BlockSpec((BLOCK, HEAD_DIM), prv(0)), pl.BlockSpec((BLOCK, HEAD_DIM), prv(1)), gain, gain],
        out_specs=[pl.BlockSpec((BLOCK, 3 * D), lambda r, b: (r * nb + jnp.maximum(b - 1, 0), 0)), gain, gain],
        out_shape=[jax.ShapeDtypeStruct((S, 3 * D), MXU_DTYPE), jax.ShapeDtypeStruct((H, HEAD_DIM), F32),
                   jax.ShapeDtypeStruct((H, HEAD_DIM), F32)],
        scratch_shapes=[pltpu.VMEM(blk, F32)] * 8,
        compiler_params=_params("arbitrary", "arbitrary"),
    )(qkv, qkv, qkv, qkv, qkv, qkv, do, lse, delta, rqk, rqk, qg, kg)


def _adamw(w, m, v, terms, slots, *, name):
    R, C = w.shape
    nt = len(terms)
    tr = _pick(R, 256, 16)
    c1 = 1.0 - ADAM_B1 ** ADAM_STEP
    c2 = 1.0 - ADAM_B2 ** ADAM_STEP

    def body(slot_ref, w_ref, m_ref, v_ref, *rest):
        t_refs = rest[:nt]
        g_ref, d_ref, nm_ref, nv_ref = rest[nt:]
        g = t_refs[0][...].astype(F32)
        for t in t_refs[1:]:
            g = g + t[...].astype(F32)
        mm = ADAM_B1 * m_ref[...] + (1.0 - ADAM_B1) * g
        vv = ADAM_B2 * v_ref[...] + (1.0 - ADAM_B2) * (g * g)
        m_hat = mm / c1
        v_hat = vv / c2
        g_ref[...] = g
        d_ref[...] = -ADAM_LR * (m_hat / (jnp.sqrt(v_hat) + ADAM_EPS) + ADAM_WD * w_ref[...])
        nm_ref[...] = mm
        nv_ref[...] = vv

    row = pl.BlockSpec((tr, C), lambda i, s: (i, 0))
    grid_spec = pltpu.PrefetchScalarGridSpec(
        num_scalar_prefetch=1, grid=(R // tr,),
        in_specs=[row, row, row] + [pl.BlockSpec((None, tr, C), lambda i, s, t=t: (s[t], i, 0)) for t in range(nt)],
        out_specs=[row] * 4)
    return pl.pallas_call(
        body, name=name, grid_spec=grid_spec, out_shape=[jax.ShapeDtypeStruct((R, C), F32)] * 4,
        compiler_params=_params("parallel"),
    )(slots, w, m, v, *terms)


def _chip_partials(g, sib, core, *, name):
    _, R, C = g.shape
    tr = _pick(R, 1200, 16)

    def body(core_ref, g_ref, s_ref, o_ref):
        o_ref[...] = (g_ref[...] + s_ref[...].astype(F32)).astype(o_ref.dtype)

    grid_spec = pltpu.PrefetchScalarGridSpec(
        num_scalar_prefetch=1, grid=(4, R // tr),
        in_specs=[pl.BlockSpec((None, tr, C), lambda k, i, c: (2 * k + c[0], i, 0)),
                  pl.BlockSpec((None, tr, C), lambda k, i, c: (k, i, 0))],
        out_specs=pl.BlockSpec((None, tr, C), lambda k, i, c: (k, i, 0)))
    return pl.pallas_call(
        body, name=name, grid_spec=grid_spec, out_shape=jax.ShapeDtypeStruct((4, R, C), sib.dtype),
        compiler_params=_params("parallel", "parallel"),
    )(core, g, sib)


_ANY = pl.BlockSpec(memory_space=pl.ANY)


def _place():
    return lax.axis_index("x"), lax.axis_index("y"), lax.axis_index("c")


class _Exchange:
    def __init__(self, operands, out_shapes, scratch, emit):
        self.operands, self.out_shapes, self.scratch, self.emit = operands, out_shapes, scratch, emit


def _run_exchange(ex, *, name):
    n_in, n_out = len(ex.operands), len(ex.out_shapes)

    def body(*refs):
        ex.emit(0, 1, refs[:n_in], refs[n_in:n_in + n_out], refs[n_in + n_out:])

    return pl.pallas_call(body, name=name, in_specs=[_ANY] * n_in, out_specs=[_ANY] * n_out, out_shape=ex.out_shapes,
                          scratch_shapes=ex.scratch)(*ex.operands)


def _gather_exchange(shard):
    R, C = shard.shape

    def emit(step, n, ins, outs, sems):
        x_ref, out_ref = ins[0], outs[0]
        send_sems, recv_sems, local_sem = sems
        x, y, c = _place()
        me, sibling = (x, y, c), (x, y, 1 - c)
        chips = [(1 - x, y), (x, 1 - y), (1 - x, 1 - y)]

        def slot(px, py, pc):
            return out_ref.at[4 * px + 2 * py + pc]

        def copy(k, block, to, src=None):
            return pltpu.make_async_remote_copy(
                src_ref=slot(*block) if src is None else src, dst_ref=slot(*block),
                send_sem=send_sems.at[k], recv_sem=recv_sems.at[k], device_id=to, device_id_type=MESH)

        mine = pltpu.make_async_copy(x_ref, slot(*me), local_sem)
        first = [copy(0, me, sibling, src=x_ref)] + [copy(1 + j, me, (*chip, c), src=x_ref) for j, chip in enumerate(chips)]
        passed = [copy(4 + j, (*chip, c), sibling) for j, chip in enumerate(chips)]

        @pl.when(step == 0)
        def _():
            mine.start()
            for cp in first:
                cp.start()

        for j, chip in enumerate(chips):
            @pl.when(step == max(n - 2 * (len(chips) - j), 0))
            def _(j=j, chip=chip):
                copy(1 + j, (*chip, c), me).wait_recv()
                passed[j].start()

        @pl.when(step == n - 1)
        def _():
            copy(0, sibling, me).wait_recv()
            for j, chip in enumerate(chips):
                copy(4 + j, (*chip, 1 - c), me).wait_recv()
            for cp in first + passed:
                cp.wait_send()
            mine.wait()

    return _Exchange([shard], [jax.ShapeDtypeStruct((N_DEV, R, C), shard.dtype)],
                     [pltpu.SemaphoreType.DMA((7,)), pltpu.SemaphoreType.DMA((7,)), pltpu.SemaphoreType.DMA], emit)


def _all_gather(shard, *, name):
    return _run_exchange(_gather_exchange(shard), name=name)[0]


def _rs_sibling(g, *, name):
    _, R, C = g.shape

    def body(g_ref, sib_ref, send_sems, recv_sems):
        x, y, c = _place()
        sends = [pltpu.make_async_remote_copy(
            src_ref=g_ref.at[2 * k + (1 - c)], dst_ref=sib_ref.at[k], send_sem=send_sems.at[k], recv_sem=recv_sems.at[k],
            device_id=(x, y, 1 - c), device_id_type=MESH) for k in range(4)]
        for cp in sends:
            cp.start()
        for cp in sends:
            cp.wait_recv()
        for cp in sends:
            cp.wait_send()

    return pl.pallas_call(
        body, name=name, in_specs=[_ANY], out_specs=_ANY, out_shape=jax.ShapeDtypeStruct((4, R, C), g.dtype),
        scratch_shapes=[pltpu.SemaphoreType.DMA((4,)), pltpu.SemaphoreType.DMA((4,))],
    )(g)


def _chips_exchange(part):
    _, R, C = part.shape

    def emit(step, n, ins, outs, sems):
        p_ref, out_ref = ins[0], outs[0]
        send_sems, recv_sems = sems
        x, y, c = _place()
        chips = [(1 - x, y), (x, 1 - y), (1 - x, 1 - y)]
        sends = [pltpu.make_async_remote_copy(
            src_ref=p_ref.at[2 * px + py], dst_ref=out_ref.at[j], send_sem=send_sems.at[j], recv_sem=recv_sems.at[j],
            device_id=(px, py, c), device_id_type=MESH) for j, (px, py) in enumerate(chips)]

        @pl.when(step == 0)
        def _():
            for cp in sends:
                cp.start()

        @pl.when(step == n - 1)
        def _():
            for cp in sends:
                cp.wait_recv()
            for cp in sends:
                cp.wait_send()

    return _Exchange([part], [jax.ShapeDtypeStruct((3, R, C), part.dtype)],
                     [pltpu.SemaphoreType.DMA((3,)), pltpu.SemaphoreType.DMA((3,))], emit)


def _rs_chips(part, *, name):
    return _run_exchange(_chips_exchange(part), name=name)[0]


def _interleave_rows(wt):
    F2, D = wt.shape
    return wt.reshape(2, F2 // 256, 128, D).transpose(1, 0, 2, 3).reshape(F2, D)


def _deinterleave_rows(wt):
    F2, D = wt.shape
    return wt.reshape(F2 // 256, 2, 128, D).transpose(1, 0, 2, 3).reshape(F2, D)


def _interleave_cols(v):
    k, F2 = v.shape
    return v.reshape(k, 2, F2 // 256, 128).transpose(0, 2, 1, 3).reshape(k, F2)


def _deinterleave_cols(v):
    k, F2 = v.shape
    return v.reshape(k, F2 // 256, 2, 128).transpose(0, 2, 1, 3).reshape(k, F2)


def _pack_rows(parts):
    return jnp.concatenate(parts, axis=0)


def _flat_pack(parts, width):
    flat = jnp.concatenate([p.reshape(-1) for p in parts])
    pad = (-flat.shape[0]) % (8 * width)
    return jnp.pad(flat, (0, pad)).reshape(-1, width)


def _flat_unpack(packed, shapes):
    flat = packed.reshape(-1)
    out, off = [], 0
    for shp in shapes:
        n = int(np.prod(shp))
        out.append(flat[off:off + n].reshape(shp))
        off += n
    return out


def _ffn_forward(x, hf, wupT, wdown, dw_i, dwb_i, tag, loss_target=None):
    up, act = _ffn_up_act(hf, wupT, dw_i, dwb_i, name=f"ffn{tag}_up")
    if loss_target is None:
        y = _mm(act, wdown, mode="nn", out_dtype=F32, name=f"ffn{tag}_down", residual=x)
    else:
        y = _mm(act, wdown, mode="nn", out_dtype=F32, name=f"ffn{tag}_down", residual=x, tm=512, post=_post_loss(loss_target),
                keep_main=False)
    return y, (hf, up, act)


def _ffn_backward(x, g_ffn, wupT, wdown, dw_i, dwb_i, saved, dy, dyb, tag, exchange=None):
    hf, up, act = saved
    d_wdown = _mm(act, dyb, mode="tn", out_dtype=F32, name=f"ffn{tag}_dwdown", tm=1408, tk=2048)
    dup, d_dw_i, d_dwb_i, *carried = _ffn_act_bwd(up, dyb, wdown, dw_i, dwb_i, name=f"ffn{tag}_actbwd", exchange=exchange)
    dx, dxb, dg, cs = _mm_post(dup, wupT, _post_rms_bwd(x, g_ffn, dy), name=f"ffn{tag}_dhf")
    d_wupT = _mm(dup, hf, mode="tn", out_dtype=F32, name=f"ffn{tag}_dwup", tm=1408, tk=2048)
    return dx, dxb, cs, dict(w_upT=d_wupT, w_down=d_wdown, dw=d_dw_i[0:FFN_KERNEL], dw_b=d_dwb_i, norm=dg), carried


def _local_step(x, target, p, late_weights=None, early_reduce=None):
    S, D = x.shape
    H = D // HEAD_DIM
    h0 = _rms_fwd(x, p["norm_mix"][0:1], name="l0_rms")
    u = _mm(h0, p["w_inT"], mode="nt", out_dtype=F32, name="l0_in", bias=p["cm_b_in"])
    c, s, *carried = _cm_fwd(u, p["cm_dw"], p["cm_dw_b"], p["cm_ln_g"], p["cm_ln_b"], name="l0_conv",
                             exchange=None if late_weights is None else late_weights[0])
    if late_weights is not None:
        p = {**p, **late_weights[1](carried)}
    x1, hf0 = _mm(s, p["w_out"], mode="nn", out_dtype=F32, name="l0_out", bias=p["cm_b_out"], residual=x,
                  post=_post_rms(p["norm_ffn"][0:1]))
    x2, sv0 = _ffn_forward(x1, hf0, p["w_upT"][0], p["w_down"][0], p["ff_dw"][0], p["ff_dw_b"][0:1], 0)
    dils = [dil for _, dil in DILATED_GROUPS]
    assert dils[0] == 1
    h1s = [t.reshape(S, D) for t in _rms_fwd(x2, p["norm_mix"][1:2], name="l1_rms", subs=tuple(dils[1:]))]
    qkvs, rqks, os_, ls_ = [], [], [], []
    for g in range(len(dils)):
        qkv_g, r_g = _qkv_proj(h1s[g], p["w_qkvT"], grp=g, name=f"l1_qkv{g}")
        qkvs.append(qkv_g)
        rqks.append(r_g)
        o, l = _attn_fwd(qkvs[g], p["at_q_norm"][g * H:(g + 1) * H], p["at_k_norm"][g * H:(g + 1) * H], grp=g, name=f"l1_attn{g}")
        os_.append(o)
        ls_.append(l)
    outb, lses = _attn_merge(os_, ls_, dils, name="l1_merge")
    x3, hf1 = _mm(outb, p["w_o"], mode="nn", out_dtype=F32, name="l1_o", residual=x2, post=_post_rms(p["norm_ffn"][1:2]))
    (dx4, dx4b, loss), sv1 = _ffn_forward(x3, hf1, p["w_upT"][1], p["w_down"][1], p["ff_dw"][1], p["ff_dw_b"][1:2], 1,
                                          loss_target=target)
    dx3, dx3b, _, gf1, _ = _ffn_backward(x3, p["norm_ffn"][1:2], p["w_upT"][1], p["w_down"][1], p["ff_dw"][1], p["ff_dw_b"][1:2],
                                      sv1, dx4, dx4b, 1)
    do = _mm(dx3b, p["w_o"], mode="nt", out_dtype=MXU_DTYPE, name="l1_do")
    d_wo = _mm(outb, dx3b, mode="tn", out_dtype=F32, name="l1_dwo", tk=2048)
    deltas, dos = _attn_delta(do, outb, dils, name="l1_delta")
    dh1s, d_wqkvT, dqg, dkg = [], [], [], []
    for g, dil in enumerate(dils):
        dqkv_g, a, b_ = _attn_bwd(qkvs[g], rqks[g], dos[g], lses[g], deltas[g], p["at_q_norm"][g * H:(g + 1) * H],
                                  p["at_k_norm"][g * H:(g + 1) * H], grp=g, name=f"l1_attnbwd{g}")
        dqg.append(a)
        dkg.append(b_)
        d_wqkvT.append(_mm(dqkv_g, h1s[g], mode="tn", out_dtype=F32, name=f"l1_dwqkv{g}", tk=2048))
        dh1s.append(_mm(dqkv_g, p["w_qkvT"], mode="nn", out_dtype=F32, name=f"l1_dh{g}", b_off=g * 3 * D, b_len=3 * D))
    dx2, dx2b, dgm1, _ = _rms_bwd(x2, p["norm_mix"][1:2], dh1s[0:1], dx3, name="l1_rmsbwd",
                                  dh_subs=[(dh1s[g].reshape(dils[g], S // dils[g], D), dils[g]) for g in range(1, len(dils))])
    ex, finish = (None, None) if early_reduce is None else early_reduce(
        dict(w_qkvT=jnp.concatenate(d_wqkvT, axis=0), w_o=d_wo, w_upT=gf1["w_upT"], w_down=gf1["w_down"]))
    dx1, dx1b, cs1, gf0, carried = _ffn_backward(x1, p["norm_ffn"][0:1], p["w_upT"][0], p["w_down"][0], p["ff_dw"][0],
                                                 p["ff_dw_b"][0:1], sv0, dx2, dx2b, 0, exchange=ex)
    reduced = [] if finish is None else [finish(carried)]
    ex, finish = (None, None) if early_reduce is None else early_reduce(dict(w_upT=gf0["w_upT"], w_down=gf0["w_down"]))
    ds = _mm(dx1b, p["w_out"], mode="nt", out_dtype=F32, name="l0_ds")
    d_wout = _mm(s, dx1b, mode="tn", out_dtype=F32, name="l0_dwout", tk=2048)
    dc, d_lng, d_lnb = _cm_ln_bwd(c, ds, p["cm_ln_g"], p["cm_ln_b"], name="l0_lnbwd")
    du, d_cmdw, d_cmdwb, d_bin, *carried = _cm_conv_bwd(dc, u, p["cm_dw"], name="l0_convbwd", exchange=ex)
    if finish is not None:
        reduced.append(finish(carried))
    grad_x, _, dgm0, _ = _mm_post(du, p["w_inT"], _post_rms_bwd(x, p["norm_mix"][0:1], dx1), name="l0_dh")
    d_winT = _mm(du, h0, mode="tn", out_dtype=F32, name="l0_dwin", tk=2048)
    grads = dict(
        norm_mix=jnp.concatenate([dgm0, dgm1], axis=0),
        norm_ffn=jnp.concatenate([gf0["norm"], gf1["norm"]], axis=0),
        w_inT=d_winT, cm_b_in=d_bin, cm_dw=d_cmdw[0:CONV_KERNEL], cm_dw_b=d_cmdwb, cm_ln_g=d_lng, cm_ln_b=d_lnb,
        w_out=d_wout, cm_b_out=cs1,
        w_qkvT=jnp.concatenate(d_wqkvT, axis=0), at_q_norm=jnp.concatenate(dqg, axis=0), at_k_norm=jnp.concatenate(dkg, axis=0),
        w_o=d_wo,
        w_upT=[gf0["w_upT"], gf1["w_upT"]], w_down=[gf0["w_down"], gf1["w_down"]],
        ff_dw=jnp.stack([gf0["dw"], gf1["dw"]]), ff_dw_b=jnp.concatenate([gf0["dw_b"], gf1["dw_b"]], axis=0),
    )
    return loss, grad_x, grads, reduced


_BIG = ("cm_w_in", "cm_w_out", "at_w_qkv", "at_w_out", "ff_w_up", "ff_w_down")
_TRANSPOSED = ("cm_w_in", "at_w_qkv", "ff_w_up")
_SMALL = ("norm_mix", "norm_ffn", "cm_b_in", "cm_dw_b", "cm_ln_g", "cm_ln_b", "cm_b_out", "at_q_norm", "at_k_norm",
          "ff_dw_b", "cm_dw", "ff_dw")
_SMALL_SHARDED = ("cm_dw", "ff_dw")
_ORDER = ("norm_mix", "norm_ffn", "cm_w_in", "cm_b_in", "cm_dw", "cm_dw_b", "cm_ln_g", "cm_ln_b", "cm_w_out", "cm_b_out",
          "at_w_qkv", "at_q_norm", "at_k_norm", "at_w_out", "ff_w_up", "ff_dw", "ff_dw_b", "ff_w_down")


_UNITS = (("cm_w_in", 0), ("cm_w_out", 0), ("ff_w_up", 0), ("ff_w_down", 0),
          ("at_w_qkv", 0), ("at_w_out", 0), ("ff_w_up", 1), ("ff_w_down", 1))
_N_FIRST = 2
_N_LAYER0 = 4


def _unit_rows(t, n, l):
    return t[n].shape[2] if n in _TRANSPOSED else t[n].shape[1]


def _big_rows(t, units=_UNITS):
    return _pack_rows([t[n][l].T if n in _TRANSPOSED else t[n][l] for n, l in units])


def _big_unrows(packed, like):
    mats, off = {}, 0
    for n, l in _UNITS:
        rows = _unit_rows(like, n, l)
        m = packed[off:off + rows]
        off += rows
        mats[(n, l)] = m.T if n in _TRANSPOSED else m
    return {n: jnp.stack([mats[(n, l)] for l in range(like[n].shape[0])]) for n in _BIG}


def kernel(x, norm_mix, norm_ffn, cm_w_in, cm_b_in, cm_dw, cm_dw_b, cm_ln_g, cm_ln_b, cm_w_out, cm_b_out, at_w_qkv, at_q_norm, at_k_norm, at_w_out, ff_w_up, ff_dw, ff_dw_b, ff_w_down, loss_target, m_norm_mix, m_norm_ffn, m_cm_w_in, m_cm_b_in, m_cm_dw, m_cm_dw_b, m_cm_ln_g, m_cm_ln_b, m_cm_w_out, m_cm_b_out, m_at_w_qkv, m_at_q_norm, m_at_k_norm, m_at_w_out, m_ff_w_up, m_ff_dw, m_ff_dw_b, m_ff_w_down, v_norm_mix, v_norm_ffn, v_cm_w_in, v_cm_b_in, v_cm_dw, v_cm_dw_b, v_cm_ln_g, v_cm_ln_b, v_cm_w_out, v_cm_b_out, v_at_w_qkv, v_at_q_norm, v_at_k_norm, v_at_w_out, v_ff_w_up, v_ff_dw, v_ff_dw_b, v_ff_w_down):
    w = dict(norm_mix=norm_mix, norm_ffn=norm_ffn, cm_w_in=cm_w_in, cm_b_in=cm_b_in, cm_dw=cm_dw, cm_dw_b=cm_dw_b, cm_ln_g=cm_ln_g,
             cm_ln_b=cm_ln_b, cm_w_out=cm_w_out, cm_b_out=cm_b_out, at_w_qkv=at_w_qkv, at_q_norm=at_q_norm, at_k_norm=at_k_norm,
             at_w_out=at_w_out, ff_w_up=ff_w_up, ff_dw=ff_dw, ff_dw_b=ff_dw_b, ff_w_down=ff_w_down)
    m = dict(norm_mix=m_norm_mix, norm_ffn=m_norm_ffn, cm_w_in=m_cm_w_in, cm_b_in=m_cm_b_in, cm_dw=m_cm_dw, cm_dw_b=m_cm_dw_b,
             cm_ln_g=m_cm_ln_g, cm_ln_b=m_cm_ln_b, cm_w_out=m_cm_w_out, cm_b_out=m_cm_b_out, at_w_qkv=m_at_w_qkv,
             at_q_norm=m_at_q_norm, at_k_norm=m_at_k_norm, at_w_out=m_at_w_out, ff_w_up=m_ff_w_up, ff_dw=m_ff_dw,
             ff_dw_b=m_ff_dw_b, ff_w_down=m_ff_w_down)
    v = dict(norm_mix=v_norm_mix, norm_ffn=v_norm_ffn, cm_w_in=v_cm_w_in, cm_b_in=v_cm_b_in, cm_dw=v_cm_dw, cm_dw_b=v_cm_dw_b,
             cm_ln_g=v_cm_ln_g, cm_ln_b=v_cm_ln_b, cm_w_out=v_cm_w_out, cm_b_out=v_cm_b_out, at_w_qkv=v_at_w_qkv,
             at_q_norm=v_at_q_norm, at_k_norm=v_at_k_norm, at_w_out=v_at_w_out, ff_w_up=v_ff_w_up, ff_dw=v_ff_dw,
             ff_dw_b=v_ff_dw_b, ff_w_down=v_ff_w_down)
    S, D = x.shape[1], x.shape[2]
    F2 = ff_dw_b.shape[1]
    H3 = at_q_norm.shape[1]
    me = 4 * lax.axis_index("x") + 2 * lax.axis_index("y") + lax.axis_index("c")

    ix, iy, ic = lax.axis_index("x"), lax.axis_index("y"), lax.axis_index("c")
    chip = 2 * ix + iy
    core = jnp.stack([ic]).astype(jnp.int32)
    w_rows = _big_rows(w)
    unit_rows = [_unit_rows(w, n, l) for n, l in _UNITS]
    n_first = sum(unit_rows[:_N_FIRST])
    n_layer0 = sum(unit_rows[:_N_LAYER0])
    w_wire = w_rows.astype(MXU_DTYPE)

    def unpack(gathered, units, rows):
        full, off = {}, 0
        for (n, l), r in zip(units, rows):
            full[(n, l)] = gathered[:, off:off + r, :].reshape(N_DEV * r, D)
            off += r
        out = {}
        if ("cm_w_in", 0) in full:
            out.update(w_inT=full[("cm_w_in", 0)], w_out=full[("cm_w_out", 0)])
        if ("at_w_qkv", 0) in full:
            out.update(w_qkvT=full[("at_w_qkv", 0)], w_o=full[("at_w_out", 0)],
                       w_upT=[_interleave_rows(full[("ff_w_up", l)]) for l in range(2)],
                       w_down=[full[("ff_w_down", l)] for l in range(2)])
        return out

    first = _all_gather(w_wire[:n_first], name="gather_first")
    late_weights = (_gather_exchange(w_wire[n_first:]),
                    lambda carried: unpack(carried[0], _UNITS[_N_FIRST:], unit_rows[_N_FIRST:]))
    small_sh = _flat_pack([cm_dw, ff_dw], D)
    small_g = _all_gather(small_sh, name="gather_small")
    cm_dw_full = jnp.concatenate([_flat_unpack(small_g[j], [cm_dw.shape, ff_dw.shape])[0][0] for j in range(N_DEV)], axis=-1)
    ff_dw_full = jnp.concatenate([_flat_unpack(small_g[j], [cm_dw.shape, ff_dw.shape])[1] for j in range(N_DEV)], axis=-1)

    p = dict(
        norm_mix=norm_mix, norm_ffn=norm_ffn, cm_b_in=cm_b_in, cm_dw=cm_dw_full, cm_dw_b=cm_dw_b, cm_ln_g=cm_ln_g, cm_ln_b=cm_ln_b,
        cm_b_out=cm_b_out, at_q_norm=at_q_norm[0], at_k_norm=at_k_norm[0],
        ff_dw=jnp.stack([_interleave_cols(ff_dw_full[l]) for l in range(ff_dw_full.shape[0])]),
        ff_dw_b=_interleave_cols(ff_dw_b),
        **unpack(first, _UNITS[:_N_FIRST], unit_rows[:_N_FIRST]),
    )

    def pack(pieces):
        return jnp.concatenate([t.reshape(N_DEV, t.shape[0] // N_DEV, D) for t in pieces], axis=1)

    def reduce_start(pieces, tag):
        g_rows = pack(pieces)
        sib = _rs_sibling(g_rows.astype(WIRE_DTYPE), name=f"reduce{tag}_sibling")
        return g_rows, sib, _chip_partials(g_rows, sib, core, name=f"reduce{tag}_add")

    def early_reduce(gd):
        ffn = [_deinterleave_rows(gd["w_upT"]), gd["w_down"]]
        tag, pieces = (2, [gd["w_qkvT"], gd["w_o"]] + ffn) if "w_qkvT" in gd else (1, ffn)
        g_rows, sib, part = reduce_start(pieces, tag)
        return _chips_exchange(part), lambda carried: (g_rows, sib, carried[0])

    loss8, grad_x, g, (reduced2, reduced1) = _local_step(x[0], loss_target[0], p, late_weights, early_reduce)
    loss = lax.psum(loss8[0, 0], ("x", "y", "c"))
    g_rows0, sib0, part0 = reduce_start([g["w_inT"], g["w_out"]], 0)
    reduced0 = (g_rows0, sib0, _rs_chips(part0, name="reduce0_chips"))
    slots = jnp.stack([me, chip, 0 * me, 0 * me + 1, 0 * me + 2]).astype(jnp.int32)
    m_rows, v_rows = _big_rows(m), _big_rows(v)
    updated = []
    for tag, (g_rows, sib, recv), rows in ((0, reduced0, slice(0, n_first)), (1, reduced1, slice(n_first, n_layer0)),
                                           (2, reduced2, slice(n_layer0, None))):
        updated.append(_adamw(w_rows[rows], m_rows[rows], v_rows[rows], [g_rows, sib, recv, recv, recv], slots, name=f"adamw_big{tag}"))
    big = [_big_unrows(jnp.concatenate([u[k] for u in updated], axis=0), w) for k in range(4)]

    g_small = dict(g)
    g_small["cm_b_in"] = g["cm_b_in"]
    g_small["at_q_norm"] = g["at_q_norm"][None]
    g_small["at_k_norm"] = g["at_k_norm"][None]
    g_small["ff_dw_b"] = _deinterleave_cols(g["ff_dw_b"])
    g_small["cm_dw"] = g["cm_dw"][None]
    g_small["ff_dw"] = jnp.stack([_deinterleave_cols(g["ff_dw"][l]) for l in range(g["ff_dw"].shape[0])])
    small_shapes = [g_small[n].shape for n in _SMALL]
    gs_parts = _all_gather(_flat_pack([g_small[n] for n in _SMALL], D), name="gather_small_grads")

    def embed(t, n):
        if n not in _SMALL_SHARDED:
            return t
        full_shape = t.shape[:-1] + (t.shape[-1] * N_DEV,)
        return lax.dynamic_update_slice_in_dim(jnp.zeros(full_shape, F32), t, me * t.shape[-1], axis=t.ndim - 1)

    packs = [_flat_pack([embed(tree[n], n) for n in _SMALL], D) for tree in (w, m, v)]
    gs, ds_, ms, vs = _adamw(packs[0], packs[1], packs[2], [gs_parts] * N_DEV, jnp.arange(N_DEV, dtype=jnp.int32),
                             name="adamw_small")
    small = []
    for t in (gs, ds_, ms, vs):
        un = dict(zip(_SMALL, _flat_unpack(t, small_shapes)))
        for n in _SMALL_SHARDED:
            width = w[n].shape[-1]
            un[n] = lax.dynamic_slice_in_dim(un[n], me * width, width, axis=un[n].ndim - 1)
        small.append({n: un[n].reshape(w[n].shape) for n in _SMALL})

    outs = [loss, grad_x[None]]
    for k in range(4):
        for n in _ORDER:
            outs.append(big[k][n] if n in _BIG else small[k][n])
    return tuple(outs)
```

```python
import functools

import jax
import jax.numpy as jnp
import numpy as np
from jax import lax
from jax.experimental import pallas as pl
from jax.experimental.pallas import tpu as pltpu

F32 = jnp.float32
MXU_DTYPE = jnp.bfloat16
WIRE_DTYPE = jnp.bfloat16
EPS = 1e-6
NEG = -1e30
HEAD_DIM = 128
BLOCK = 128
DILATED_GROUPS = ((128, 1), (512, 4), (2048, 16))
ALIBI_MAX = 8.0
CONV_KERNEL = 31
CONV_HALO = 32
CONV_ROWS = 64
FFN_KERNEL = 3
FFN_HALO = 16
FFN_ROWS = 64
ADAM_LR, ADAM_B1, ADAM_B2, ADAM_EPS, ADAM_WD, ADAM_STEP = 0.001, 0.9, 0.999, 1e-08, 0.01, 10
V7X_VMEM_BYTES = 64 * 1024 * 1024
VMEM_LIMIT = V7X_VMEM_BYTES * 3 // 4
N_DEV = 8
MESH = pl.DeviceIdType.MESH


def _pick(n, target, align):
    if n <= target:
        return n
    best = None
    for t in range(align, target + 1, align):
        if n % t == 0:
            best = t
    assert best is not None, (n, target, align)
    return best


def _params(*sem, vmem=VMEM_LIMIT):
    return pltpu.CompilerParams(dimension_semantics=sem, vmem_limit_bytes=vmem)


def _sigmoid(x):
    return 1.0 / (1.0 + jnp.exp(-x))


_DIMS = {"nn": ((1,), (0,)), "nt": ((1,), (1,)), "tn": ((0,), (0,))}


def _mm(a, b, *, mode, out_dtype, name, tm=1024, tn=1024, tk=None, bias=None, residual=None, b_off=0, b_len=None,
        post=None, keep_main=True):
    if mode == "tn":
        K, M = a.shape
    else:
        M, K = a.shape
    if mode == "nt":
        N = b.shape[0] if b_len is None else b_len
    else:
        N = b.shape[1]
    if b_len is not None:
        assert mode == "nt" or (mode == "nn" and K == b_len)
    tm = _pick(M, tm, 128 if mode == "tn" else 16)
    tn = _pick(N, tn, 128)
    tk = K if tk is None else _pick(K, tk, 128 if mode != "tn" else 16)
    nk = K // tk
    unit = tn if mode == "nt" else tk
    assert b_off % unit == 0
    kb0 = b_off // unit
    if mode == "tn":
        a_spec = pl.BlockSpec((tk, tm), lambda i, j, k: (k, i))
    else:
        a_spec = pl.BlockSpec((tm, tk), lambda i, j, k: (i, k))
    if mode == "nt":
        b_spec = pl.BlockSpec((tn, tk), lambda i, j, k: (j + kb0, k))
    else:
        b_spec = pl.BlockSpec((tk, tn), lambda i, j, k: (k + kb0, j))
    in_specs = [a_spec, b_spec]
    args = [a, b]
    if bias is not None:
        in_specs.append(pl.BlockSpec((1, tn), lambda i, j, k: (0, j)))
        args.append(bias)
    if residual is not None:
        in_specs.append(pl.BlockSpec((tm, tn), lambda i, j, k: (i, j)))
        args.append(residual)
    has_bias, has_res = bias is not None, residual is not None
    kinds = {"tile": ((tm, tn), lambda i, j, k: (i, j)), "row": ((1, tn), lambda i, j, k: (0, j)),
             "lanes": ((8, 128), lambda i, j, k: (0, 0))}
    post_in = [] if post is None else post.ins
    post_out = [] if post is None else post.outs
    if post is not None:
        assert tn == N
        for arr, kind in post_in:
            in_specs.append(pl.BlockSpec(*kinds[kind]))
            args.append(arr)
    out_specs = [pl.BlockSpec((tm, tn), lambda i, j, k: (i, j))] if keep_main else []
    out_shape = [jax.ShapeDtypeStruct((M, N), out_dtype)] if keep_main else []
    for kind, dtype in post_out:
        out_specs.append(pl.BlockSpec(*kinds[kind]))
        out_shape.append(jax.ShapeDtypeStruct({"tile": (M, N), "row": (1, N), "lanes": (8, 128)}[kind], dtype))
    accumulates = any(kind != "tile" for kind, _ in post_out)

    def body(*refs):
        a_ref, b_ref = refs[0], refs[1]
        pos = 2
        bias_ref = res_ref = None
        if has_bias:
            bias_ref = refs[pos]
            pos += 1
        if has_res:
            res_ref = refs[pos]
            pos += 1
        pin_refs = refs[pos:pos + len(post_in)]
        pos += len(post_in)
        o_ref = refs[pos] if keep_main else None
        pos += 1 if keep_main else 0
        pout_refs = refs[pos:pos + len(post_out)]
        pos += len(post_out)
        acc_ref = refs[pos] if nk > 1 else None

        def finish(acc):
            if has_bias:
                acc = acc + bias_ref[...]
            if has_res:
                acc = acc + res_ref[...]
            if keep_main:
                o_ref[...] = acc.astype(o_ref.dtype)
            if post is not None:
                post.fn(acc, pin_refs, pout_refs, pl.program_id(0) == 0)

        part = lax.dot_general(a_ref[...].astype(MXU_DTYPE), b_ref[...].astype(MXU_DTYPE), (_DIMS[mode], ((), ())),
                               preferred_element_type=F32)
        if nk == 1:
            finish(part)
        else:
            k = pl.program_id(2)

            @pl.when(k == 0)
            def _():
                acc_ref[...] = part

            @pl.when(jnp.logical_and(k > 0, k < nk - 1))
            def _():
                acc_ref[...] += part

            @pl.when(k == nk - 1)
            def _():
                finish(acc_ref[...] + part)

    outs = pl.pallas_call(
        body, name=name, grid=(M // tm, N // tn, nk), in_specs=in_specs, out_specs=out_specs, out_shape=out_shape,
        scratch_shapes=[pltpu.VMEM((tm, tn), F32)] if nk > 1 else [],
        compiler_params=_params("arbitrary" if accumulates else "parallel", "parallel", "arbitrary"),
    )(*args)
    return outs[0] if post is None else outs


def _mm_post(a, b, post, *, name, tm=512, rows=256):
    M, K = a.shape
    N = b.shape[1]
    tm = _pick(M, tm, 16)
    rows = _pick(tm, rows, 16)
    kinds = {"tile": ((tm, N), lambda i: (i, 0)), "row": ((1, N), lambda i: (0, 0)), "lanes": ((8, 128), lambda i: (0, 0))}
    n_in = len(post.ins)

    def body(a_ref, b_ref, *rest):
        first = pl.program_id(0) == 0

        def product(c):
            return jnp.dot(a_ref[c * rows:(c + 1) * rows, :].astype(MXU_DTYPE), b_ref[...].astype(MXU_DTYPE),
                           preferred_element_type=F32)

        def chunk(refs, specs, c):
            return [r.at[pl.ds(c * rows, rows), :] if kind == "tile" else r for r, kind in zip(refs, specs)]

        acc = product(0)
        for c in range(tm // rows):
            nxt = product(c + 1) if c + 1 < tm // rows else None
            post.fn(acc, chunk(rest[:n_in], [k for _, k in post.ins], c), chunk(rest[n_in:], [k for k, _ in post.outs], c),
                    jnp.logical_and(first, c == 0))
            acc = nxt

    return pl.pallas_call(
        body, name=name, grid=(M // tm,),
        in_specs=[pl.BlockSpec((tm, K), lambda i: (i, 0)), pl.BlockSpec((K, N), lambda i: (0, 0), pipeline_mode=pl.Buffered(1))]
        + [pl.BlockSpec(*kinds[kind]) for _, kind in post.ins],
        out_specs=[pl.BlockSpec(*kinds[kind]) for kind, _ in post.outs],
        out_shape=[jax.ShapeDtypeStruct({"tile": (M, N), "row": (1, N), "lanes": (8, 128)}[kind], dtype) for kind, dtype in post.outs],
        compiler_params=_params("arbitrary", vmem=V7X_VMEM_BYTES * 7 // 8),
    )(a, b, *[arr for arr, _ in post.ins])


class _Post:
    def __init__(self, ins, outs, fn):
        self.ins, self.outs, self.fn = ins, outs, fn


def _accumulate(ref, value, first):
    @pl.when(first)
    def _():
        ref[...] = value

    @pl.when(jnp.logical_not(first))
    def _():
        ref[...] += value


def _post_rms(g):
    def fn(acc, ins, outs, first):
        r = lax.rsqrt(jnp.mean(acc * acc, axis=-1, keepdims=True) + EPS)
        outs[0][...] = (acc * r * ins[0][...]).astype(outs[0].dtype)

    return _Post([(g, "row")], [("tile", MXU_DTYPE)], fn)


def _post_loss(target):
    def fn(acc, ins, outs, first):
        e = acc - ins[0][...]
        dy = e * (1.0 / acc.shape[-1])
        outs[0][...] = dy
        outs[1][...] = dy.astype(outs[1].dtype)
        part = jnp.sum(jnp.sum(e * e, axis=0, keepdims=True), axis=1, keepdims=True) * (0.5 / acc.shape[-1])
        _accumulate(outs[2], jnp.broadcast_to(part, outs[2].shape), first)

    return _Post([(target, "tile")], [("tile", F32), ("tile", MXU_DTYPE), ("lanes", F32)], fn)


def _post_rms_bwd(x, g, dres):
    def fn(acc, ins, outs, first):
        xv = ins[0][...]
        r = lax.rsqrt(jnp.mean(xv * xv, axis=-1, keepdims=True) + EPS)
        xh = xv * r
        gy = acc * ins[1][...]
        dx = r * (gy - xh * jnp.mean(gy * xh, axis=-1, keepdims=True)) + ins[2][...]
        outs[0][...] = dx
        outs[1][...] = dx.astype(outs[1].dtype)
        _accumulate(outs[2], jnp.sum(acc * xh, axis=0, keepdims=True), first)
        _accumulate(outs[3], jnp.sum(dx, axis=0, keepdims=True), first)

    return _Post([(x, "tile"), (g, "row"), (dres, "tile")], [("tile", F32), ("tile", MXU_DTYPE), ("row", F32), ("row", F32)], fn)


SUB_TILE = 512


def _sub_spec(dil, ts, cols):
    return pl.BlockSpec((dil, ts // dil, cols), lambda i: (0, i, 0))


def _tok_to_sub(tok_ref, dst_ref, dil):
    nc, ts, _ = tok_ref.shape
    for c in range(nc):
        for r in range(dil):
            dst_ref[r, :, c * 128:(c + 1) * 128] = tok_ref.at[c][pl.ds(r, ts // dil, stride=dil), :].astype(dst_ref.dtype)


def _sub_to_tok(src_ref, tok_ref, dil):
    nc, ts, _ = tok_ref.shape
    for c in range(nc):
        for r in range(dil):
            tok_ref.at[c][pl.ds(r, ts // dil, stride=dil), :] = src_ref[r, :, c * 128:(c + 1) * 128].astype(F32)


def _rms_fwd(x, g, *, name, subs=()):
    S, D = x.shape
    ts = _pick(S, SUB_TILE, 16 * max(subs, default=1))
    NC = D // 128

    def body(x_ref, g_ref, h_ref, *rest):
        xv = x_ref[...]
        r = lax.rsqrt(jnp.mean(xv * xv, axis=-1, keepdims=True) + EPS)
        h = xv * r * g_ref[...]
        h_ref[...] = h.astype(h_ref.dtype)
        if subs:
            tok_ref = rest[-1]
            for c in range(NC):
                tok_ref[c] = h[:, c * 128:(c + 1) * 128]
            for dil, dst_ref in zip(subs, rest):
                _tok_to_sub(tok_ref, dst_ref, dil)

    row = pl.BlockSpec((ts, D), lambda i: (i, 0))
    outs = pl.pallas_call(
        body, name=name, grid=(S // ts,),
        in_specs=[row, pl.BlockSpec((1, D), lambda i: (0, 0))],
        out_specs=[row] + [_sub_spec(dil, ts, D) for dil in subs],
        out_shape=[jax.ShapeDtypeStruct((S, D), MXU_DTYPE)] + [jax.ShapeDtypeStruct((dil, S // dil, D), MXU_DTYPE) for dil in subs],
        scratch_shapes=[pltpu.VMEM((NC, ts, 128), F32)] if subs else [],
        compiler_params=_params("parallel"),
    )(x, g)
    return outs if subs else outs[0]


def _rms_bwd(x, g, dhs, dres, *, name, dh_subs=()):
    S, D = x.shape
    ts = _pick(S, SUB_TILE, 16 * max([dil for _, dil in dh_subs], default=1))
    n_dh, n_sub = len(dhs), len(dh_subs)
    NC = D // 128

    def body(*refs):
        x_ref, g_ref = refs[0], refs[1]
        dh_refs = refs[2:2 + n_dh]
        sub_refs = refs[2 + n_dh:2 + n_dh + n_sub]
        dres_ref, dx_ref, dxb_ref, dg_ref, cs_ref = refs[2 + n_dh + n_sub:7 + n_dh + n_sub]
        i = pl.program_id(0)
        xv = x_ref[...]
        r = lax.rsqrt(jnp.mean(xv * xv, axis=-1, keepdims=True) + EPS)
        xh = xv * r
        dhv = dh_refs[0][...].astype(F32)
        for t in dh_refs[1:]:
            dhv = dhv + t[...].astype(F32)
        for (_, dil), sub_ref in zip(dh_subs, sub_refs):
            tok_ref = refs[-1]
            _sub_to_tok(sub_ref, tok_ref, dil)
            dhv = dhv + jnp.concatenate([tok_ref[c] for c in range(NC)], axis=1)
        gy = dhv * g_ref[...]
        dx = r * (gy - xh * jnp.mean(gy * xh, axis=-1, keepdims=True)) + dres_ref[...]
        dx_ref[...] = dx
        dxb_ref[...] = dx.astype(dxb_ref.dtype)
        dg = jnp.sum(dhv * xh, axis=0, keepdims=True)
        cs = jnp.sum(dx, axis=0, keepdims=True)

        @pl.when(i == 0)
        def _():
            dg_ref[...] = dg
            cs_ref[...] = cs

        @pl.when(i > 0)
        def _():
            dg_ref[...] += dg
            cs_ref[...] += cs

    row = pl.BlockSpec((ts, D), lambda i: (i, 0))
    vec = pl.BlockSpec((1, D), lambda i: (0, 0))
    return pl.pallas_call(
        body, name=name, grid=(S // ts,),
        in_specs=[row, vec] + [row] * n_dh + [_sub_spec(dil, ts, D) for _, dil in dh_subs] + [row],
        out_specs=[row, row, vec, vec],
        out_shape=[jax.ShapeDtypeStruct((S, D), F32), jax.ShapeDtypeStruct((S, D), MXU_DTYPE),
                   jax.ShapeDtypeStruct((1, D), F32), jax.ShapeDtypeStruct((1, D), F32)],
        scratch_shapes=[pltpu.VMEM((NC, ts, 128), F32)] if n_sub else [],
        compiler_params=_params("arbitrary"),
    )(x, g, *dhs, *[a for a, _ in dh_subs], dres)


def _conv_phases(ph_ref, ts):
    n = ts + CONV_HALO - 8
    for b in range(1, 8):
        ph_ref[b, 0:n, :] = ph_ref[0, pl.ds(b, n), :]


def _phase_taps(base, step=1):
    groups = {}
    for k in range(CONV_KERNEL):
        a, b = divmod(base + step * k, 8)
        groups.setdefault(b, []).append((a, k))
    out = []
    for b in sorted(groups):
        ak = sorted(groups[b])
        assert [a for a, _ in ak] == list(range(ak[0][0], ak[0][0] + len(ak)))
        out.append((b, ak[0][0], [k for _, k in ak]))
    return out


def _cm_fwd(u, dw, dw_b, ln_g, ln_b, *, name, exchange=None):
    S, D2 = u.shape
    D = D2 // 2
    ts = _pick(S, 256, CONV_HALO)
    hb = ts // CONV_HALO
    ex_in, ex_out, ex_scr = ([], [], []) if exchange is None else (exchange.operands, exchange.out_shapes, exchange.scratch)

    def body(u_ref, up_ref, dw_ref, dwb_ref, g_ref, b_ref, *rest):
        xi = rest[:len(ex_in)]
        c_ref, s_ref = rest[len(ex_in):len(ex_in) + 2]
        xo = rest[len(ex_in) + 2:len(ex_in) + 2 + len(ex_out)]
        ext_ref = rest[len(ex_in) + 2 + len(ex_out)]
        i = pl.program_id(0)
        if exchange is not None:
            exchange.emit(i, S // ts, xi, xo, rest[len(ex_in) + 3 + len(ex_out):])
        prev = up_ref[:, :D] * _sigmoid(up_ref[:, D:])
        ext_ref[0:CONV_HALO, :] = jnp.where(i > 0, prev, 0.0)
        ext_ref[CONV_HALO:CONV_HALO + ts, :] = u_ref[:, :D] * _sigmoid(u_ref[:, D:])
        for cc in range(D // 128):
            sl = slice(cc * 128, (cc + 1) * 128)
            acc = jnp.zeros((ts, 128), F32) + dwb_ref[:, sl]
            for k in range(CONV_KERNEL):
                acc = acc + dw_ref[k:k + 1, sl] * ext_ref[pl.ds(CONV_HALO - (CONV_KERNEL - 1) + k, ts), sl]
            c_ref[:, sl] = acc
        c = c_ref[...]
        mu = jnp.mean(c, axis=-1, keepdims=True)
        xc = c - mu
        rstd = lax.rsqrt(jnp.mean(xc * xc, axis=-1, keepdims=True) + EPS)
        y = xc * rstd * g_ref[...] + b_ref[...]
        s_ref[...] = (y * _sigmoid(y)).astype(s_ref.dtype)

    vec = pl.BlockSpec((1, D), lambda i: (0, 0))
    return pl.pallas_call(
        body, name=name, grid=(S // ts,),
        in_specs=[pl.BlockSpec((ts, D2), lambda i: (i, 0)),
                  pl.BlockSpec((CONV_HALO, D2), lambda i: (jnp.maximum(i * hb - 1, 0), 0)),
                  pl.BlockSpec((CONV_KERNEL, D), lambda i: (0, 0)), vec, vec, vec] + [_ANY] * len(ex_in),
        out_specs=[pl.BlockSpec((ts, D), lambda i: (i, 0)), pl.BlockSpec((ts, D), lambda i: (i, 0))] + [_ANY] * len(ex_out),
        out_shape=[jax.ShapeDtypeStruct((S, D), F32), jax.ShapeDtypeStruct((S, D), MXU_DTYPE)] + list(ex_out),
        scratch_shapes=[pltpu.VMEM((ts + CONV_HALO, D), F32)] + list(ex_scr),
        compiler_params=_params("parallel" if exchange is None else "arbitrary"),
    )(u, u, dw, dw_b, ln_g, ln_b, *ex_in)


def _cm_ln_bwd(c, ds, ln_g, ln_b, *, name):
    S, D = c.shape
    ts = _pick(S, 512, 16)

    def body(c_ref, ds_ref, g_ref, b_ref, dc_ref, dg_ref, db_ref):
        i = pl.program_id(0)
        cv = c_ref[...]
        mu = jnp.mean(cv, axis=-1, keepdims=True)
        xc = cv - mu
        rstd = lax.rsqrt(jnp.mean(xc * xc, axis=-1, keepdims=True) + EPS)
        xh = xc * rstd
        y = xh * g_ref[...] + b_ref[...]
        sg = _sigmoid(y)
        dy = ds_ref[...].astype(F32) * (sg * (1.0 + y * (1.0 - sg)))
        gy = dy * g_ref[...]
        dc_ref[...] = rstd * (gy - jnp.mean(gy, axis=-1, keepdims=True) - xh * jnp.mean(gy * xh, axis=-1, keepdims=True))
        dg = jnp.sum(dy * xh, axis=0, keepdims=True)
        db = jnp.sum(dy, axis=0, keepdims=True)

        @pl.when(i == 0)
        def _():
            dg_ref[...] = dg
            db_ref[...] = db

        @pl.when(i > 0)
        def _():
            dg_ref[...] += dg
            db_ref[...] += db

    row = pl.BlockSpec((ts, D), lambda i: (i, 0))
    vec = pl.BlockSpec((1, D), lambda i: (0, 0))
    return pl.pallas_call(
        body, name=name, grid=(S // ts,), in_specs=[row, row, vec, vec], out_specs=[row, vec, vec],
        out_shape=[jax.ShapeDtypeStruct((S, D), F32), jax.ShapeDtypeStruct((1, D), F32), jax.ShapeDtypeStruct((1, D), F32)],
        compiler_params=_params("arbitrary"),
    )(c, ds, ln_g, ln_b)


def _cm_conv_bwd(dc, u, dw, *, name, exchange=None):
    ex_in, ex_out, ex_scr = ([], [], []) if exchange is None else (exchange.operands, exchange.out_shapes, exchange.scratch)
    S, D2 = u.shape
    D = D2 // 2
    ts = _pick(S, 256, CONV_HALO)
    hb = ts // CONV_HALO
    n_t = S // ts
    last_h = S // CONV_HALO - 1

    rc = _pick(ts, CONV_ROWS, 8)

    def fold8(v):
        out = v[0:8]
        for j in range(1, v.shape[0] // 8):
            out = out + v[8 * j:8 * j + 8]
        return out

    def body(dc_ref, dcn_ref, u_ref, up_ref, dw_ref, *rest):
        xi = rest[:len(ex_in)]
        du_ref, ddw_ref, ddwb_ref, dbin_ref = rest[len(ex_in):len(ex_in) + 4]
        xo = rest[len(ex_in) + 4:len(ex_in) + 4 + len(ex_out)]
        dph_ref, gph_ref, dgl_ref = rest[len(ex_in) + 4 + len(ex_out):len(ex_in) + 7 + len(ex_out)]
        i = pl.program_id(0)
        if exchange is not None:
            exchange.emit(i, n_t, xi, xo, rest[len(ex_in) + 7 + len(ex_out):])
        dph_ref[0, 0:ts, :] = dc_ref[...]
        dph_ref[0, ts:ts + CONV_HALO, :] = jnp.where(i < n_t - 1, dcn_ref[...], 0.0)
        prev = up_ref[:, :D] * _sigmoid(up_ref[:, D:])
        gph_ref[0, 0:CONV_HALO, :] = jnp.where(i > 0, prev, 0.0)
        gph_ref[0, CONV_HALO:CONV_HALO + ts, :] = u_ref[:, :D] * _sigmoid(u_ref[:, D:])
        _conv_phases(dph_ref, ts)
        _conv_phases(gph_ref, ts)

        @pl.when(i == 0)
        def _():
            ddw_ref[...] = jnp.zeros_like(ddw_ref)
            ddwb_ref[...] = jnp.zeros_like(ddwb_ref)
            dbin_ref[...] = jnp.zeros_like(dbin_ref)

        for cc in range(D // 128):
            sl = slice(cc * 128, (cc + 1) * 128)
            sl2 = slice(D + cc * 128, D + (cc + 1) * 128)
            wk = [dw_ref[k:k + 1, sl] for k in range(CONV_KERNEL)]
            acc_a, acc_g = jnp.zeros((8, 128), F32), jnp.zeros((8, 128), F32)
            dgl = jnp.zeros((ts, 128), F32)
            for b, a0, taps in _phase_taps(CONV_KERNEL - 1, -1):
                for j, k in enumerate(taps):
                    dgl = dgl + wk[k] * dph_ref[b, 8 * (a0 + j):8 * (a0 + j) + ts, sl]
            dgl_ref[...] = dgl
            for r0 in range(0, ts, rc):
                dglu = dgl_ref[r0:r0 + rc, :]
                av = u_ref[r0:r0 + rc, sl]
                sg = _sigmoid(u_ref[r0:r0 + rc, sl2])
                da = dglu * sg
                dg = dglu * av * sg * (1.0 - sg)
                du_ref[r0:r0 + rc, sl] = da.astype(du_ref.dtype)
                du_ref[r0:r0 + rc, sl2] = dg.astype(du_ref.dtype)
                acc_a = acc_a + fold8(da)
                acc_g = acc_g + fold8(dg)
            dbin_ref[:, sl] += jnp.sum(acc_a, axis=0, keepdims=True)
            dbin_ref[:, sl2] += jnp.sum(acc_g, axis=0, keepdims=True)
            for gi, (b, a0, taps) in enumerate(_phase_taps(CONV_HALO - (CONV_KERNEL - 1))):
                accs = [jnp.zeros((8, 128), F32) for _ in taps]
                accb = jnp.zeros((8, 128), F32)
                for r0 in range(0, ts, rc):
                    dcc = dph_ref[0, r0:r0 + rc, sl]
                    win = gph_ref[b, 8 * a0 + r0:8 * (a0 + len(taps) - 1) + r0 + rc, sl]
                    for j in range(len(taps)):
                        accs[j] = accs[j] + fold8(dcc * win[8 * j:8 * j + rc])
                    if gi == 0:
                        accb = accb + fold8(dcc)
                for j, k in enumerate(taps):
                    ddw_ref[k:k + 1, sl] += jnp.sum(accs[j], axis=0, keepdims=True)
                if gi == 0:
                    ddwb_ref[:, sl] += jnp.sum(accb, axis=0, keepdims=True)

    return pl.pallas_call(
        body, name=name, grid=(n_t,),
        in_specs=[pl.BlockSpec((ts, D), lambda i: (i, 0)),
                  pl.BlockSpec((CONV_HALO, D), lambda i: (jnp.minimum((i + 1) * hb, last_h), 0)),
                  pl.BlockSpec((ts, D2), lambda i: (i, 0)),
                  pl.BlockSpec((CONV_HALO, D2), lambda i: (jnp.maximum(i * hb - 1, 0), 0)),
                  pl.BlockSpec((CONV_KERNEL, D), lambda i: (0, 0))] + [_ANY] * len(ex_in),
        out_specs=[pl.BlockSpec((ts, D2), lambda i: (i, 0)), pl.BlockSpec((CONV_HALO, D), lambda i: (0, 0)),
                   pl.BlockSpec((1, D), lambda i: (0, 0)), pl.BlockSpec((1, D2), lambda i: (0, 0))] + [_ANY] * len(ex_out),
        out_shape=[jax.ShapeDtypeStruct((S, D2), MXU_DTYPE), jax.ShapeDtypeStruct((CONV_HALO, D), F32),
                   jax.ShapeDtypeStruct((1, D), F32), jax.ShapeDtypeStruct((1, D2), F32)] + list(ex_out),
        scratch_shapes=[pltpu.VMEM((8, ts + CONV_HALO, D), F32), pltpu.VMEM((8, ts + CONV_HALO, D), F32),
                        pltpu.VMEM((ts, 128), F32)] + list(ex_scr),
        compiler_params=_params("arbitrary"),
    )(dc, dc, u, u, dw, *ex_in)


def _ffn_cols(F2):
    return _pick(F2, 1024, 256)


def _ffn_up_act(hf, wupT, dw, dw_b, *, name):
    S, D = hf.shape
    F2 = wupT.shape[0]
    ts = _pick(S, 512, 16)
    tc = _ffn_cols(F2)
    n_ct = F2 // tc
    hb = ts // FFN_HALO
    rc = _pick(ts, FFN_ROWS, 16)

    def body(h_ref, hp_ref, wt_ref, w_ref, b_ref, up_ref, a_ref, he_ref, ext_ref):
        i = pl.program_id(0)
        he_ref[0:FFN_HALO, :] = hp_ref[...]
        he_ref[FFN_HALO:FFN_HALO + ts, :] = h_ref[...]

        def product(j):
            return lax.dot_general(he_ref[...].astype(MXU_DTYPE), wt_ref[j * tc:(j + 1) * tc, :].astype(MXU_DTYPE),
                                   (_DIMS["nt"], ((), ())), preferred_element_type=F32)

        def conv_gate(j, res):
            ext = ext_ref.at[j % 2]
            upv = res.astype(up_ref.dtype)
            up_ref[:, j * tc:(j + 1) * tc] = upv[FFN_HALO:FFN_HALO + ts]
            ext[0:FFN_HALO, :] = jnp.where(i > 0, upv[0:FFN_HALO].astype(F32), 0.0)
            ext[FFN_HALO:FFN_HALO + ts, :] = upv[FFN_HALO:FFN_HALO + ts].astype(F32)
            for q in range(tc // 256):
                sls = [slice(q * 256 + half * 128, q * 256 + half * 128 + 128) for half in range(2)]
                gls = [slice(j * tc + sl.start, j * tc + sl.stop) for sl in sls]
                wk = [[w_ref[k:k + 1, gl] for k in range(FFN_KERNEL)] for gl in gls]
                bb = [b_ref[:, gl] for gl in gls]
                for r0 in range(0, ts, rc):
                    gt, vl = [bb[h] + sum(wk[h][k] * ext[pl.ds(FFN_HALO + r0 - 2 + k, rc), sls[h]] for k in range(FFN_KERNEL))
                              for h in range(2)]
                    a_ref[r0:r0 + rc, j * (tc // 2) + q * 128:j * (tc // 2) + (q + 1) * 128] = (gt * _sigmoid(gt) * vl).astype(a_ref.dtype)

        res = product(0)
        for j in range(n_ct):
            nxt = product(j + 1) if j + 1 < n_ct else None
            conv_gate(j, res)
            res = nxt

    return pl.pallas_call(
        body, name=name, grid=(S // ts,),
        in_specs=[pl.BlockSpec((ts, D), lambda i: (i, 0)),
                  pl.BlockSpec((FFN_HALO, D), lambda i: (jnp.maximum(i * hb - 1, 0), 0)),
                  pl.BlockSpec((F2, D), lambda i: (0, 0), pipeline_mode=pl.Buffered(1)),
                  pl.BlockSpec((FFN_KERNEL, F2), lambda i: (0, 0)), pl.BlockSpec((1, F2), lambda i: (0, 0))],
        out_specs=[pl.BlockSpec((ts, F2), lambda i: (i, 0)), pl.BlockSpec((ts, F2 // 2), lambda i: (i, 0))],
        out_shape=[jax.ShapeDtypeStruct((S, F2), MXU_DTYPE), jax.ShapeDtypeStruct((S, F2 // 2), MXU_DTYPE)],
        scratch_shapes=[pltpu.VMEM((ts + FFN_HALO, D), hf.dtype), pltpu.VMEM((2, ts + FFN_HALO, tc), F32)],
        compiler_params=_params("parallel", vmem=V7X_VMEM_BYTES * 7 // 8),
    )(hf, hf, wupT, dw, dw_b)


def _ffn_act_bwd(up, dyb, wdown, dw, dw_b, *, name, exchange=None):
    ex_in, ex_out, ex_scr = ([], [], []) if exchange is None else (exchange.operands, exchange.out_shapes, exchange.scratch)
    S, F2 = up.shape
    D = dyb.shape[1]
    ts = _pick(S, 512, 16)
    tc = _ffn_cols(F2)
    n_ct = F2 // tc
    hb = ts // FFN_HALO
    n_t = S // ts
    last_h = S // FFN_HALO - 1
    E = ts + FFN_HALO

    rc = _pick(ts, FFN_ROWS, 16)

    def fold8(v):
        out = v[0:8]
        for j in range(1, v.shape[0] // 8):
            out = out + v[8 * j:8 * j + 8]
        return out

    def body(u_ref, up_ref, un_ref, dy_ref, dyn_ref, wd_ref, w_ref, b_ref, *rest):
        xi = rest[:len(ex_in)]
        dup_ref, ddw_ref, ddb_ref = rest[len(ex_in):len(ex_in) + 3]
        xo = rest[len(ex_in) + 3:len(ex_in) + 3 + len(ex_out)]
        ue2_ref, dcv2_ref, dye_ref, da2_ref = rest[len(ex_in) + 3 + len(ex_out):len(ex_in) + 7 + len(ex_out)]
        i = pl.program_id(0)
        if exchange is not None:
            exchange.emit(i, n_t, xi, xo, rest[len(ex_in) + 7 + len(ex_out):])
        dye_ref[0:ts, :] = dy_ref[...]
        dye_ref[ts:E, :] = jnp.where(i < n_t - 1, dyn_ref[...], jnp.zeros_like(dyn_ref))

        @pl.when(i == 0)
        def _():
            ddw_ref[...] = jnp.zeros_like(ddw_ref)
            ddb_ref[...] = jnp.zeros_like(ddb_ref)

        def product(j):
            return lax.dot_general(dye_ref[...].astype(MXU_DTYPE), wd_ref[j * (tc // 2):(j + 1) * (tc // 2), :].astype(MXU_DTYPE),
                                   (_DIMS["nt"], ((), ())), preferred_element_type=F32)

        def conv_bwd(j, res):
            ue_ref, dcv_ref, da_ref = ue2_ref.at[j % 2], dcv2_ref.at[j % 2], da2_ref.at[j % 2]
            cs = slice(j * tc, (j + 1) * tc)
            da_ref[...] = res.astype(da_ref.dtype)
            ue_ref[0:FFN_HALO, :] = jnp.where(i > 0, up_ref[:, cs].astype(F32), 0.0)
            ue_ref[FFN_HALO:FFN_HALO + ts, :] = u_ref[:, cs].astype(F32)
            ue_ref[FFN_HALO + ts:FFN_HALO + ts + FFN_HALO, :] = jnp.where(i < n_t - 1, un_ref[:, cs].astype(F32), 0.0)
            for q in range(tc // 256):
                sls = [slice(q * 256 + half * 128, q * 256 + half * 128 + 128) for half in range(2)]
                gls = [slice(j * tc + sl.start, j * tc + sl.stop) for sl in sls]
                qs = slice(q * 128, (q + 1) * 128)
                wk = [[w_ref[k:k + 1, gl] for k in range(FFN_KERNEL)] for gl in gls]
                bb = [b_ref[:, gl] for gl in gls]
                acc = [[jnp.zeros((8, 128), F32) for _ in range(FFN_KERNEL)] for _ in range(2)]
                accb = [jnp.zeros((8, 128), F32) for _ in range(2)]
                for r0, rows in [(r, rc) for r in range(0, ts, rc)] + [(ts, FFN_HALO)]:
                    xs = [[ue_ref[pl.ds(FFN_HALO + r0 - 2 + k, rows), sls[h]] for k in range(FFN_KERNEL)] for h in range(2)]
                    gt, vl = [bb[h] + sum(wk[h][k] * xs[h][k] for k in range(FFN_KERNEL)) for h in range(2)]
                    sg = _sigmoid(gt)
                    dae = da_ref[r0:r0 + rows, qs].astype(F32)
                    dcv = [dae * vl * (sg * (1.0 + gt * (1.0 - sg))), dae * (gt * sg)]
                    for h in range(2):
                        dcv_ref[r0:r0 + rows, sls[h]] = dcv[h]
                        if r0 < ts:
                            for k in range(FFN_KERNEL):
                                acc[h][k] = acc[h][k] + fold8(dcv[h] * xs[h][k])
                            accb[h] = accb[h] + fold8(dcv[h])
                for r0 in range(0, ts, rc):
                    for h in range(2):
                        dup = sum(wk[h][2 - jj] * dcv_ref[pl.ds(r0 + jj, rc), sls[h]] for jj in range(FFN_KERNEL))
                        dup_ref[r0:r0 + rc, gls[h]] = dup.astype(dup_ref.dtype)
                for h in range(2):
                    for k in range(FFN_KERNEL):
                        ddw_ref[k:k + 1, gls[h]] += jnp.sum(acc[h][k], axis=0, keepdims=True)
                    ddb_ref[:, gls[h]] += jnp.sum(accb[h], axis=0, keepdims=True)

        res = product(0)
        for j in range(n_ct):
            nxt = product(j + 1) if j + 1 < n_ct else None
            conv_bwd(j, res)
            res = nxt

    return pl.pallas_call(
        body, name=name, grid=(n_t,),
        in_specs=[pl.BlockSpec((ts, F2), lambda i: (i, 0)),
                  pl.BlockSpec((FFN_HALO, F2), lambda i: (jnp.maximum(i * hb - 1, 0), 0)),
                  pl.BlockSpec((FFN_HALO, F2), lambda i: (jnp.minimum((i + 1) * hb, last_h), 0)),
                  pl.BlockSpec((ts, D), lambda i: (i, 0)),
                  pl.BlockSpec((FFN_HALO, D), lambda i: (jnp.minimum((i + 1) * hb, last_h), 0)),
                  pl.BlockSpec((F2 // 2, D), lambda i: (0, 0), pipeline_mode=pl.Buffered(1)),
                  pl.BlockSpec((FFN_KERNEL, F2), lambda i: (0, 0)),
                  pl.BlockSpec((1, F2), lambda i: (0, 0))] + [_ANY] * len(ex_in),
        out_specs=[pl.BlockSpec((ts, F2), lambda i: (i, 0)), pl.BlockSpec((FFN_HALO, F2), lambda i: (0, 0)),
                   pl.BlockSpec((1, F2), lambda i: (0, 0))] + [_ANY] * len(ex_out),
        out_shape=[jax.ShapeDtypeStruct((S, F2), MXU_DTYPE), jax.ShapeDtypeStruct((FFN_HALO, F2), F32),
                   jax.ShapeDtypeStruct((1, F2), F32)] + list(ex_out),
        scratch_shapes=[pltpu.VMEM((2, ts + 2 * FFN_HALO, tc), F32), pltpu.VMEM((2, E, tc), F32), pltpu.VMEM((E, D), dyb.dtype),
                        pltpu.VMEM((2, E, tc // 2), MXU_DTYPE)] + list(ex_scr),
        compiler_params=_params("arbitrary", vmem=V7X_VMEM_BYTES * 7 // 8),
    )(up, up, up, dyb, dyb, wdown, dw, dw_b, *ex_in)


def _slopes(n_heads_total):
    return np.asarray(2.0 ** (-ALIBI_MAX * (np.arange(n_heads_total, dtype=np.float32) + 1.0) / n_heads_total), np.float32)


def _qkv_proj(h, w_qkvT, *, grp, name):
    S, D = h.shape
    H = D // HEAD_DIM
    tm = _pick(S, 1024, 16)

    rows = _pick(tm, 256, 16)

    def body(a_ref, b_ref, o_ref, r_ref):
        j = pl.program_id(1)
        r_ref[...] = jnp.zeros_like(r_ref)

        def product(c):
            return lax.dot_general(a_ref[c * rows:(c + 1) * rows, :].astype(MXU_DTYPE), b_ref[...].astype(MXU_DTYPE),
                                   (_DIMS["nt"], ((), ())), preferred_element_type=F32)

        @pl.when(j < 2)
        def _():
            acc = product(0)
            for c in range(tm // rows):
                nxt = product(c + 1) if c + 1 < tm // rows else None
                rs = slice(c * rows, (c + 1) * rows)
                for hd in range(H):
                    hs = slice(hd * HEAD_DIM, (hd + 1) * HEAD_DIM)
                    xv = acc[:, hs]
                    r = lax.rsqrt(jnp.mean(xv * xv, axis=-1, keepdims=True) + EPS)
                    o_ref[rs, hs] = (xv * r).astype(o_ref.dtype)
                    r_ref[rs, hd:hd + 1] = r
                acc = nxt

        @pl.when(j == 2)
        def _():
            o_ref[...] = lax.dot_general(a_ref[...].astype(MXU_DTYPE), b_ref[...].astype(MXU_DTYPE), (_DIMS["nt"], ((), ())),
                                         preferred_element_type=F32).astype(o_ref.dtype)

    return pl.pallas_call(
        body, name=name, grid=(S // tm, 3),
        in_specs=[pl.BlockSpec((tm, D), lambda i, j: (i, 0)), pl.BlockSpec((D, D), lambda i, j: (grp * 3 + j, 0))],
        out_specs=[pl.BlockSpec((tm, D), lambda i, j: (i, j)), pl.BlockSpec((tm, HEAD_DIM), lambda i, j: (i, j))],
        out_shape=[jax.ShapeDtypeStruct((S, 3 * D), MXU_DTYPE), jax.ShapeDtypeStruct((S, 3 * HEAD_DIM), F32)],
        compiler_params=_params("parallel", "parallel"),
    )(h, w_qkvT)


def _band(b, dil):
    qi = lax.broadcasted_iota(jnp.int32, (BLOCK, 2 * BLOCK), 0)
    ki = lax.broadcasted_iota(jnp.int32, (BLOCK, 2 * BLOCK), 1)
    delta = qi + BLOCK - ki
    valid = (delta >= 0) & (delta <= BLOCK) & ((ki >= BLOCK) | (b > 0))
    return valid, (delta * dil).astype(F32)


def _attn_fwd(qkv, qg, kg, *, grp, name):
    S, W = qkv.shape
    D = W // 3
    H = D // HEAD_DIM
    dil = DILATED_GROUPS[grp][1]
    L = S // dil
    nb = L // BLOCK
    slopes = _slopes(3 * H)[grp * H:(grp + 1) * H]
    scale = HEAD_DIM ** -0.5

    def body(q_ref, kp_ref, kc_ref, vp_ref, vc_ref, qg_ref, kg_ref, o_ref, l_ref):
        b = pl.program_id(1)
        valid, dist = _band(b, dil)
        l_ref[...] = jnp.zeros_like(l_ref)
        ss = []
        for h in range(H):
            hs = slice(h * HEAD_DIM, (h + 1) * HEAD_DIM)
            qn = (q_ref[:, hs].astype(F32) * qg_ref[h:h + 1, :]).astype(MXU_DTYPE)
            kp = (kp_ref[:, hs].astype(F32) * kg_ref[h:h + 1, :]).astype(MXU_DTYPE)
            kc = (kc_ref[:, hs].astype(F32) * kg_ref[h:h + 1, :]).astype(MXU_DTYPE)
            ss.append(lax.dot_general(qn, jnp.concatenate([kp, kc], axis=0), (((1,), (1,)), ((), ())), preferred_element_type=F32))
        ps = []
        for h in range(H):
            s = jnp.where(valid, ss[h] * scale - float(slopes[h]) * dist, NEG)
            m = jnp.max(s, axis=-1, keepdims=True)
            p = jnp.exp(s - m)
            den = jnp.sum(p, axis=-1, keepdims=True)
            l_ref[:, h:h + 1] = m + jnp.log(den)
            ps.append((p.astype(MXU_DTYPE), den))
        for h in range(H):
            hs = slice(h * HEAD_DIM, (h + 1) * HEAD_DIM)
            pb, den = ps[h]
            v2 = jnp.concatenate([vp_ref[:, hs], vc_ref[:, hs]], axis=0).astype(MXU_DTYPE)
            o_ref[:, hs] = (jnp.dot(pb, v2, preferred_element_type=F32) / den).astype(o_ref.dtype)

    def cur(j):
        return lambda r, b: (r * nb + b, j)

    def prv(j):
        return lambda r, b: (r * nb + jnp.maximum(b - 1, 0), j)

    blk = (BLOCK, D)
    gain = pl.BlockSpec((H, HEAD_DIM), lambda r, b: (0, 0))
    return pl.pallas_call(
        body, name=name, grid=(dil, nb),
        in_specs=[pl.BlockSpec(blk, cur(0)), pl.BlockSpec(blk, prv(1)), pl.BlockSpec(blk, cur(1)),
                  pl.BlockSpec(blk, prv(2)), pl.BlockSpec(blk, cur(2)), gain, gain],
        out_specs=[pl.BlockSpec(blk, cur(0)), pl.BlockSpec((BLOCK, HEAD_DIM), cur(0))],
        out_shape=[jax.ShapeDtypeStruct((S, D), MXU_DTYPE), jax.ShapeDtypeStruct((S, HEAD_DIM), F32)],
        compiler_params=_params("parallel", "parallel"),
    )(qkv, qkv, qkv, qkv, qkv, qg, kg)


def _attn_merge(os_, ls_, dils, *, name):
    S, D = os_[0].shape
    H = D // HEAD_DIM
    G = len(dils)
    ts = _pick(S, SUB_TILE, 16 * max(dils))
    subs = [g for g in range(G) if dils[g] > 1]

    def body(*refs):
        o_refs, l_refs = refs[0:G], refs[G:2 * G]
        outb_ref = refs[2 * G]
        lt_refs = refs[2 * G + 1:3 * G + 1]
        scratch = refs[3 * G + 1:]
        lt_tok = scratch[0]
        o_tok = {g: scratch[1 + 2 * j] for j, g in enumerate(subs)}
        l_tok = {g: scratch[2 + 2 * j] for j, g in enumerate(subs)}
        for g in subs:
            _sub_to_tok(o_refs[g], o_tok[g], dils[g])
            _sub_to_tok(l_refs[g], l_tok[g], dils[g])
        lt_tok[0] = jnp.zeros((ts, HEAD_DIM), F32)
        for h in range(H):
            hs = slice(h * HEAD_DIM, (h + 1) * HEAD_DIM)
            ls = [l_tok[g][0][:, h:h + 1] if g in subs else l_refs[g][:, h:h + 1] for g in range(G)]
            ov = [o_tok[g][h] if g in subs else o_refs[g][:, hs].astype(F32) for g in range(G)]
            m = functools.reduce(jnp.maximum, ls)
            es = [jnp.exp(l - m) for l in ls]
            den = functools.reduce(lambda a, b: a + b, es)
            out = functools.reduce(lambda a, b: a + b, [e * o for e, o in zip(es, ov)]) / den
            outb_ref[:, hs] = out.astype(outb_ref.dtype)
            lt_tok.at[0][:, h:h + 1] = m + jnp.log(den)
        for g in range(G):
            if g in subs:
                _tok_to_sub(lt_tok, lt_refs[g], dils[g])
            else:
                lt_refs[g][...] = lt_tok[0]

    def spec(g, cols):
        return _sub_spec(dils[g], ts, cols) if g in subs else pl.BlockSpec((ts, cols), lambda i: (i, 0))

    def shape(g, cols, dtype):
        return jax.ShapeDtypeStruct((dils[g], S // dils[g], cols) if g in subs else (S, cols), dtype)

    def view(a, g):
        return a.reshape(dils[g], S // dils[g], a.shape[-1]) if g in subs else a

    outs = pl.pallas_call(
        body, name=name, grid=(S // ts,),
        in_specs=[spec(g, D) for g in range(G)] + [spec(g, HEAD_DIM) for g in range(G)],
        out_specs=[pl.BlockSpec((ts, D), lambda i: (i, 0))] + [spec(g, HEAD_DIM) for g in range(G)],
        out_shape=[jax.ShapeDtypeStruct((S, D), MXU_DTYPE)] + [shape(g, HEAD_DIM, F32) for g in range(G)],
        scratch_shapes=[pltpu.VMEM((1, ts, HEAD_DIM), F32)] + [pltpu.VMEM((H, ts, HEAD_DIM), F32), pltpu.VMEM((1, ts, HEAD_DIM), F32)] * len(subs),
        compiler_params=_params("parallel"),
    )(*[view(o, g) for g, o in enumerate(os_)], *[view(l, g) for g, l in enumerate(ls_)])
    return outs[0], [t.reshape(S, HEAD_DIM) for t in outs[1:]]


def _attn_delta(do, out, dils, *, name):
    S, D = out.shape
    H = D // HEAD_DIM
    G = len(dils)
    ts = _pick(S, SUB_TILE, 16 * max(dils))
    subs = [g for g in range(G) if dils[g] > 1]

    def body(do_ref, o_ref, *rest):
        d_refs = rest[0:G]
        dos_refs = rest[G:G + len(subs)]
        d_tok, do_tok = rest[G + len(subs):]
        d_tok[0] = jnp.zeros((ts, HEAD_DIM), F32)
        for h in range(H):
            hs = slice(h * HEAD_DIM, (h + 1) * HEAD_DIM)
            dov = do_ref[:, hs].astype(F32)
            do_tok[h] = dov
            d_tok.at[0][:, h:h + 1] = jnp.sum(dov * o_ref[:, hs].astype(F32), axis=-1, keepdims=True)
        for g in range(G):
            if g in subs:
                _tok_to_sub(d_tok, d_refs[g], dils[g])
            else:
                d_refs[g][...] = d_tok[0]
        for g, dst in zip(subs, dos_refs):
            _tok_to_sub(do_tok, dst, dils[g])

    def spec(g, cols):
        return _sub_spec(dils[g], ts, cols) if g in subs else pl.BlockSpec((ts, cols), lambda i: (i, 0))

    def shape(g, cols, dtype):
        return jax.ShapeDtypeStruct((dils[g], S // dils[g], cols) if g in subs else (S, cols), dtype)

    row = pl.BlockSpec((ts, D), lambda i: (i, 0))
    outs = pl.pallas_call(
        body, name=name, grid=(S // ts,), in_specs=[row, row],
        out_specs=[spec(g, HEAD_DIM) for g in range(G)] + [spec(g, D) for g in subs],
        out_shape=[shape(g, HEAD_DIM, F32) for g in range(G)] + [shape(g, D, do.dtype) for g in subs],
        scratch_shapes=[pltpu.VMEM((1, ts, HEAD_DIM), F32), pltpu.VMEM((H, ts, HEAD_DIM), F32)],
        compiler_params=_params("parallel"),
    )(do, out)
    deltas = [t.reshape(S, HEAD_DIM) for t in outs[0:G]]
    dos = {g: t.reshape(S, D) for g, t in zip(subs, outs[G:])}
    return deltas, [dos[g] if g in subs else do for g in range(G)]


def _attn_bwd(qkv, rqk, do, lse, delta, qg, kg, *, grp, name):
    S, W = qkv.shape
    D = W // 3
    H = D // HEAD_DIM
    dil = DILATED_GROUPS[grp][1]
    L = S // dil
    nb = L // BLOCK
    slopes = _slopes(3 * H)[grp * H:(grp + 1) * H]
    scale = HEAD_DIM ** -0.5

    def body(q_ref, qp_ref, kp_ref, kc_ref, vp_ref, vc_ref, do_ref, l_ref, dl_ref, rq_ref, rk_ref, qg_ref, kg_ref,
             out_ref, dqg_ref, dkg_ref, cq_ref, ck_ref, cv_ref, nq_ref, nk_ref, nv_ref, pk_ref, pv_ref):
        r = pl.program_id(0)
        b = pl.program_id(1)

        @pl.when(jnp.logical_and(r == 0, b == 0))
        def _():
            dqg_ref[...] = jnp.zeros_like(dqg_ref)
            dkg_ref[...] = jnp.zeros_like(dkg_ref)

        @pl.when(b < nb)
        def _():
            valid, dist = _band(b, dil)

            def operands(h):
                hs = slice(h * HEAD_DIM, (h + 1) * HEAD_DIM)
                qn = (q_ref[:, hs].astype(F32) * qg_ref[h:h + 1, :]).astype(MXU_DTYPE)
                kp = (kp_ref[:, hs].astype(F32) * kg_ref[h:h + 1, :]).astype(MXU_DTYPE)
                kc = (kc_ref[:, hs].astype(F32) * kg_ref[h:h + 1, :]).astype(MXU_DTYPE)
                k2 = jnp.concatenate([kp, kc], axis=0)
                v2 = jnp.concatenate([vp_ref[:, hs], vc_ref[:, hs]], axis=0).astype(MXU_DTYPE)
                return hs, qn, k2, v2, do_ref[:, hs].astype(MXU_DTYPE)

            sdp = []
            for h in range(H):
                hs, qn, k2, v2, doh = operands(h)
                s = lax.dot_general(qn, k2, (((1,), (1,)), ((), ())), preferred_element_type=F32)
                dp = lax.dot_general(doh, v2, (((1,), (1,)), ((), ())), preferred_element_type=F32)
                sdp.append((s, dp))
            pds = []
            for h in range(H):
                s, dp = sdp[h]
                s = jnp.where(valid, s * scale - float(slopes[h]) * dist, NEG)
                p = jnp.exp(s - l_ref[:, h:h + 1])
                pds.append((p.astype(MXU_DTYPE), (p * (dp - dl_ref[:, h:h + 1]) * scale).astype(MXU_DTYPE)))
            for h in range(H):
                hs, qn, k2, v2, doh = operands(h)
                pb, dsc = pds[h]
                nq_ref[:, hs] = jnp.dot(dsc, k2, preferred_element_type=F32)
                dk2 = lax.dot_general(dsc, qn, (((0,), (0,)), ((), ())), preferred_element_type=F32)
                dv2 = lax.dot_general(pb, doh, (((0,), (0,)), ((), ())), preferred_element_type=F32)
                pk_ref[:, hs] = dk2[0:BLOCK]
                nk_ref[:, hs] = dk2[BLOCK:2 * BLOCK]
                pv_ref[:, hs] = dv2[0:BLOCK]
                nv_ref[:, hs] = dv2[BLOCK:2 * BLOCK]

        @pl.when(b == nb)
        def _():
            pk_ref[...] = jnp.zeros_like(pk_ref)
            pv_ref[...] = jnp.zeros_like(pv_ref)

        @pl.when(b > 0)
        def _():
            for h in range(H):
                hs = slice(h * HEAD_DIM, (h + 1) * HEAD_DIM)
                for j, (xh_ref, r_ref, gain_ref, dgain_ref) in enumerate(((qp_ref, rq_ref, qg_ref, dqg_ref),
                                                                          (kp_ref, rk_ref, kg_ref, dkg_ref))):
                    dy = cq_ref[:, hs] if j == 0 else ck_ref[:, hs] + pk_ref[:, hs]
                    gain = gain_ref[h:h + 1, :]
                    xh = xh_ref[:, hs].astype(F32)
                    rr = r_ref[:, h:h + 1]
                    gy = dy * gain
                    dx = rr * (gy - xh * jnp.mean(gy * xh, axis=-1, keepdims=True))
                    out_ref[:, j * D + h * HEAD_DIM:j * D + (h + 1) * HEAD_DIM] = dx.astype(out_ref.dtype)
                    dgain_ref[h:h + 1, :] += jnp.sum(dy * xh, axis=0, keepdims=True)
                out_ref[:, 2 * D + h * HEAD_DIM:2 * D + (h + 1) * HEAD_DIM] = (cv_ref[:, hs] + pv_ref[:, hs]).astype(out_ref.dtype)

        @pl.when(b < nb)
        def _():
            cq_ref[...] = nq_ref[...]
            ck_ref[...] = nk_ref[...]
            cv_ref[...] = nv_ref[...]

    def cur(j):
        return lambda r, b: (r * nb + jnp.minimum(b, nb - 1), j)

    def prv(j):
        return lambda r, b: (r * nb + jnp.clip(b - 1, 0, nb - 1), j)

    blk = (BLOCK, D)
    lblk = pl.BlockSpec((BLOCK, HEAD_DIM), cur(0))
    gain = pl.BlockSpec((H, HEAD_DIM), lambda r, b: (0, 0))
    return pl.pallas_call(
        body, name=name, grid=(dil, nb + 1),
        in_specs=[pl.BlockSpec(blk, cur(0)), pl.BlockSpec(blk, prv(0)), pl.BlockSpec(blk, prv(1)), pl.BlockSpec(blk, cur(1)),
                  pl.BlockSpec(blk, prv(2)), pl.BlockSpec(blk, cur(2)), pl.BlockSpec(blk, cur(0)), lblk, lblk,
                  pl.BlockSpec((BLOCK, HEAD_DIM), prv(0)), pl.BlockSpec((BLOCK, HEAD_DIM), prv(1)), gain, gain],
        out_specs=[pl.BlockSpec((BLOCK, 3 * D), lambda r, b: (r * nb + jnp.maximum(b - 1, 0), 0)), gain, gain],
        out_shape=[jax.ShapeDtypeStruct((S, 3 * D), MXU_DTYPE), jax.ShapeDtypeStruct((H, HEAD_DIM), F32),
                   jax.ShapeDtypeStruct((H, HEAD_DIM), F32)],
        scratch_shapes=[pltpu.VMEM(blk, F32)] * 8,
        compiler_params=_params("arbitrary", "arbitrary"),
    )(qkv, qkv, qkv, qkv, qkv, qkv, do, lse, delta, rqk, rqk, qg, kg)


def _adamw(w, m, v, terms, slots, *, name):
    R, C = w.shape
    nt = len(terms)
    tr = _pick(R, 256, 16)
    c1 = 1.0 - ADAM_B1 ** ADAM_STEP
    c2 = 1.0 - ADAM_B2 ** ADAM_STEP

    def body(slot_ref, w_ref, m_ref, v_ref, *rest):
        t_refs = rest[:nt]
        g_ref, d_ref, nm_ref, nv_ref = rest[nt:]
        g = t_refs[0][...].astype(F32)
        for t in t_refs[1:]:
            g = g + t[...].astype(F32)
        mm = ADAM_B1 * m_ref[...] + (1.0 - ADAM_B1) * g
        vv = ADAM_B2 * v_ref[...] + (1.0 - ADAM_B2) * (g * g)
        m_hat = mm / c1
        v_hat = vv / c2
        g_ref[...] = g
        d_ref[...] = -ADAM_LR * (m_hat / (jnp.sqrt(v_hat) + ADAM_EPS) + ADAM_WD * w_ref[...])
        nm_ref[...] = mm
        nv_ref[...] = vv

    row = pl.BlockSpec((tr, C), lambda i, s: (i, 0))
    grid_spec = pltpu.PrefetchScalarGridSpec(
        num_scalar_prefetch=1, grid=(R // tr,),
        in_specs=[row, row, row] + [pl.BlockSpec((None, tr, C), lambda i, s, t=t: (s[t], i, 0)) for t in range(nt)],
        out_specs=[row] * 4)
    return pl.pallas_call(
        body, name=name, grid_spec=grid_spec, out_shape=[jax.ShapeDtypeStruct((R, C), F32)] * 4,
        compiler_params=_params("parallel"),
    )(slots, w, m, v, *terms)


def _chip_partials(g, sib, core, *, name):
    _, R, C = g.shape
    tr = _pick(R, 1200, 16)

    def body(core_ref, g_ref, s_ref, o_ref):
        o_ref[...] = (g_ref[...] + s_ref[...].astype(F32)).astype(o_ref.dtype)

    grid_spec = pltpu.PrefetchScalarGridSpec(
        num_scalar_prefetch=1, grid=(4, R // tr),
        in_specs=[pl.BlockSpec((None, tr, C), lambda k, i, c: (2 * k + c[0], i, 0)),
                  pl.BlockSpec((None, tr, C), lambda k, i, c: (k, i, 0))],
        out_specs=pl.BlockSpec((None, tr, C), lambda k, i, c: (k, i, 0)))
    return pl.pallas_call(
        body, name=name, grid_spec=grid_spec, out_shape=jax.ShapeDtypeStruct((4, R, C), sib.dtype),
        compiler_params=_params("parallel", "parallel"),
    )(core, g, sib)


_ANY = pl.BlockSpec(memory_space=pl.ANY)


def _place():
    return lax.axis_index("x"), lax.axis_index("y"), lax.axis_index("c")


class _Exchange:
    def __init__(self, operands, out_shapes, scratch, emit):
        self.operands, self.out_shapes, self.scratch, self.emit = operands, out_shapes, scratch, emit


def _run_exchange(ex, *, name):
    n_in, n_out = len(ex.operands), len(ex.out_shapes)

    def body(*refs):
        ex.emit(0, 1, refs[:n_in], refs[n_in:n_in + n_out], refs[n_in + n_out:])

    return pl.pallas_call(body, name=name, in_specs=[_ANY] * n_in, out_specs=[_ANY] * n_out, out_shape=ex.out_shapes,
                          scratch_shapes=ex.scratch)(*ex.operands)


def _gather_exchange(shard):
    R, C = shard.shape

    def emit(step, n, ins, outs, sems):
        x_ref, out_ref = ins[0], outs[0]
        send_sems, recv_sems, local_sem = sems
        x, y, c = _place()
        me, sibling = (x, y, c), (x, y, 1 - c)
        chips = [(1 - x, y), (x, 1 - y), (1 - x, 1 - y)]

        def slot(px, py, pc):
            return out_ref.at[4 * px + 2 * py + pc]

        def copy(k, block, to, src=None):
            return pltpu.make_async_remote_copy(
                src_ref=slot(*block) if src is None else src, dst_ref=slot(*block),
                send_sem=send_sems.at[k], recv_sem=recv_sems.at[k], device_id=to, device_id_type=MESH)

        mine = pltpu.make_async_copy(x_ref, slot(*me), local_sem)
        first = [copy(0, me, sibling, src=x_ref)] + [copy(1 + j, me, (*chip, c), src=x_ref) for j, chip in enumerate(chips)]
        passed = [copy(4 + j, (*chip, c), sibling) for j, chip in enumerate(chips)]

        @pl.when(step == 0)
        def _():
            mine.start()
            for cp in first:
                cp.start()

        for j, chip in enumerate(chips):
            @pl.when(step == max(n - 2 * (len(chips) - j), 0))
            def _(j=j, chip=chip):
                copy(1 + j, (*chip, c), me).wait_recv()
                passed[j].start()

        @pl.when(step == n - 1)
        def _():
            copy(0, sibling, me).wait_recv()
            for j, chip in enumerate(chips):
                copy(4 + j, (*chip, 1 - c), me).wait_recv()
            for cp in first + passed:
                cp.wait_send()
            mine.wait()

    return _Exchange([shard], [jax.ShapeDtypeStruct((N_DEV, R, C), shard.dtype)],
                     [pltpu.SemaphoreType.DMA((7,)), pltpu.SemaphoreType.DMA((7,)), pltpu.SemaphoreType.DMA], emit)


def _all_gather(shard, *, name):
    return _run_exchange(_gather_exchange(shard), name=name)[0]


def _rs_sibling(g, *, name):
    _, R, C = g.shape

    def body(g_ref, sib_ref, send_sems, recv_sems):
        x, y, c = _place()
        sends = [pltpu.make_async_remote_copy(
            src_ref=g_ref.at[2 * k + (1 - c)], dst_ref=sib_ref.at[k], send_sem=send_sems.at[k], recv_sem=recv_sems.at[k],
            device_id=(x, y, 1 - c), device_id_type=MESH) for k in range(4)]
        for cp in sends:
            cp.start()
        for cp in sends:
            cp.wait_recv()
        for cp in sends:
            cp.wait_send()

    return pl.pallas_call(
        body, name=name, in_specs=[_ANY], out_specs=_ANY, out_shape=jax.ShapeDtypeStruct((4, R, C), g.dtype),
        scratch_shapes=[pltpu.SemaphoreType.DMA((4,)), pltpu.SemaphoreType.DMA((4,))],
    )(g)


def _chips_exchange(part):
    _, R, C = part.shape

    def emit(step, n, ins, outs, sems):
        p_ref, out_ref = ins[0], outs[0]
        send_sems, recv_sems = sems
        x, y, c = _place()
        chips = [(1 - x, y), (x, 1 - y), (1 - x, 1 - y)]
        sends = [pltpu.make_async_remote_copy(
            src_ref=p_ref.at[2 * px + py], dst_ref=out_ref.at[j], send_sem=send_sems.at[j], recv_sem=recv_sems.at[j],
            device_id=(px, py, c), device_id_type=MESH) for j, (px, py) in enumerate(chips)]

        @pl.when(step == 0)
        def _():
            for cp in sends:
                cp.start()

        @pl.when(step == n - 1)
        def _():
            for cp in sends:
                cp.wait_recv()
            for cp in sends:
                cp.wait_send()

    return _Exchange([part], [jax.ShapeDtypeStruct((3, R, C), part.dtype)],
                     [pltpu.SemaphoreType.DMA((3,)), pltpu.SemaphoreType.DMA((3,))], emit)


def _rs_chips(part, *, name):
    return _run_exchange(_chips_exchange(part), name=name)[0]


def _interleave_rows(wt):
    F2, D = wt.shape
    return wt.reshape(2, F2 // 256, 128, D).transpose(1, 0, 2, 3).reshape(F2, D)


def _deinterleave_rows(wt):
    F2, D = wt.shape
    return wt.reshape(F2 // 256, 2, 128, D).transpose(1, 0, 2, 3).reshape(F2, D)


def _interleave_cols(v):
    k, F2 = v.shape
    return v.reshape(k, 2, F2 // 256, 128).transpose(0, 2, 1, 3).reshape(k, F2)


def _deinterleave_cols(v):
    k, F2 = v.shape
    return v.reshape(k, F2 // 256, 2, 128).transpose(0, 2, 1, 3).reshape(k, F2)


def _pack_rows(parts):
    return jnp.concatenate(parts, axis=0)


def _flat_pack(parts, width):
    flat = jnp.concatenate([p.reshape(-1) for p in parts])
    pad = (-flat.shape[0]) % (8 * width)
    return jnp.pad(flat, (0, pad)).reshape(-1, width)


def _flat_unpack(packed, shapes):
    flat = packed.reshape(-1)
    out, off = [], 0
    for shp in shapes:
        n = int(np.prod(shp))
        out.append(flat[off:off + n].reshape(shp))
        off += n
    return out


def _ffn_forward(x, hf, wupT, wdown, dw_i, dwb_i, tag, loss_target=None):
    up, act = _ffn_up_act(hf, wupT, dw_i, dwb_i, name=f"ffn{tag}_up")
    if loss_target is None:
        y = _mm(act, wdown, mode="nn", out_dtype=F32, name=f"ffn{tag}_down", residual=x)
    else:
        y = _mm(act, wdown, mode="nn", out_dtype=F32, name=f"ffn{tag}_down", residual=x, tm=512, post=_post_loss(loss_target),
                keep_main=False)
    return y, (hf, up, act)


def _ffn_backward(x, g_ffn, wupT, wdown, dw_i, dwb_i, saved, dy, dyb, tag, exchange=None):
    hf, up, act = saved
    d_wdown = _mm(act, dyb, mode="tn", out_dtype=F32, name=f"ffn{tag}_dwdown", tm=1408, tk=2048)
    dup, d_dw_i, d_dwb_i, *carried = _ffn_act_bwd(up, dyb, wdown, dw_i, dwb_i, name=f"ffn{tag}_actbwd", exchange=exchange)
    dx, dxb, dg, cs = _mm_post(dup, wupT, _post_rms_bwd(x, g_ffn, dy), name=f"ffn{tag}_dhf")
    d_wupT = _mm(dup, hf, mode="tn", out_dtype=F32, name=f"ffn{tag}_dwup", tm=1408, tk=2048)
    return dx, dxb, cs, dict(w_upT=d_wupT, w_down=d_wdown, dw=d_dw_i[0:FFN_KERNEL], dw_b=d_dwb_i, norm=dg), carried


def _local_step(x, target, p, late_weights=None, early_reduce=None):
    S, D = x.shape
    H = D // HEAD_DIM
    h0 = _rms_fwd(x, p["norm_mix"][0:1], name="l0_rms")
    u = _mm(h0, p["w_inT"], mode="nt", out_dtype=F32, name="l0_in", bias=p["cm_b_in"])
    c, s, *carried = _cm_fwd(u, p["cm_dw"], p["cm_dw_b"], p["cm_ln_g"], p["cm_ln_b"], name="l0_conv",
                             exchange=None if late_weights is None else late_weights[0])
    if late_weights is not None:
        p = {**p, **late_weights[1](carried)}
    x1, hf0 = _mm(s, p["w_out"], mode="nn", out_dtype=F32, name="l0_out", bias=p["cm_b_out"], residual=x,
                  post=_post_rms(p["norm_ffn"][0:1]))
    x2, sv0 = _ffn_forward(x1, hf0, p["w_upT"][0], p["w_down"][0], p["ff_dw"][0], p["ff_dw_b"][0:1], 0)
    dils = [dil for _, dil in DILATED_GROUPS]
    assert dils[0] == 1
    h1s = [t.reshape(S, D) for t in _rms_fwd(x2, p["norm_mix"][1:2], name="l1_rms", subs=tuple(dils[1:]))]
    qkvs, rqks, os_, ls_ = [], [], [], []
    for g in range(len(dils)):
        qkv_g, r_g = _qkv_proj(h1s[g], p["w_qkvT"], grp=g, name=f"l1_qkv{g}")
        qkvs.append(qkv_g)
        rqks.append(r_g)
        o, l = _attn_fwd(qkvs[g], p["at_q_norm"][g * H:(g + 1) * H], p["at_k_norm"][g * H:(g + 1) * H], grp=g, name=f"l1_attn{g}")
        os_.append(o)
        ls_.append(l)
    outb, lses = _attn_merge(os_, ls_, dils, name="l1_merge")
    x3, hf1 = _mm(outb, p["w_o"], mode="nn", out_dtype=F32, name="l1_o", residual=x2, post=_post_rms(p["norm_ffn"][1:2]))
    (dx4, dx4b, loss), sv1 = _ffn_forward(x3, hf1, p["w_upT"][1], p["w_down"][1], p["ff_dw"][1], p["ff_dw_b"][1:2], 1,
                                          loss_target=target)
    dx3, dx3b, _, gf1, _ = _ffn_backward(x3, p["norm_ffn"][1:2], p["w_upT"][1], p["w_down"][1], p["ff_dw"][1], p["ff_dw_b"][1:2],
                                      sv1, dx4, dx4b, 1)
    do = _mm(dx3b, p["w_o"], mode="nt", out_dtype=MXU_DTYPE, name="l1_do")
    d_wo = _mm(outb, dx3b, mode="tn", out_dtype=F32, name="l1_dwo", tk=2048)
    deltas, dos = _attn_delta(do, outb, dils, name="l1_delta")
    dh1s, d_wqkvT, dqg, dkg = [], [], [], []
    for g, dil in enumerate(dils):
        dqkv_g, a, b_ = _attn_bwd(qkvs[g], rqks[g], dos[g], lses[g], deltas[g], p["at_q_norm"][g * H:(g + 1) * H],
                                  p["at_k_norm"][g * H:(g + 1) * H], grp=g, name=f"l1_attnbwd{g}")
        dqg.append(a)
        dkg.append(b_)
        d_wqkvT.append(_mm(dqkv_g, h1s[g], mode="tn", out_dtype=F32, name=f"l1_dwqkv{g}", tk=2048))
        dh1s.append(_mm(dqkv_g, p["w_qkvT"], mode="nn", out_dtype=F32, name=f"l1_dh{g}", b_off=g * 3 * D, b_len=3 * D))
    dx2, dx2b, dgm1, _ = _rms_bwd(x2, p["norm_mix"][1:2], dh1s[0:1], dx3, name="l1_rmsbwd",
                                  dh_subs=[(dh1s[g].reshape(dils[g], S // dils[g], D), dils[g]) for g in range(1, len(dils))])
    ex, finish = (None, None) if early_reduce is None else early_reduce(
        dict(w_qkvT=jnp.concatenate(d_wqkvT, axis=0), w_o=d_wo, w_upT=gf1["w_upT"], w_down=gf1["w_down"]))
    dx1, dx1b, cs1, gf0, carried = _ffn_backward(x1, p["norm_ffn"][0:1], p["w_upT"][0], p["w_down"][0], p["ff_dw"][0],
                                                 p["ff_dw_b"][0:1], sv0, dx2, dx2b, 0, exchange=ex)
    reduced = [] if finish is None else [finish(carried)]
    ex, finish = (None, None) if early_reduce is None else early_reduce(dict(w_upT=gf0["w_upT"], w_down=gf0["w_down"]))
    ds = _mm(dx1b, p["w_out"], mode="nt", out_dtype=F32, name="l0_ds")
    d_wout = _mm(s, dx1b, mode="tn", out_dtype=F32, name="l0_dwout", tk=2048)
    dc, d_lng, d_lnb = _cm_ln_bwd(c, ds, p["cm_ln_g"], p["cm_ln_b"], name="l0_lnbwd")
    du, d_cmdw, d_cmdwb, d_bin, *carried = _cm_conv_bwd(dc, u, p["cm_dw"], name="l0_convbwd", exchange=ex)
    if finish is not None:
        reduced.append(finish(carried))
    grad_x, _, dgm0, _ = _mm_post(du, p["w_inT"], _post_rms_bwd(x, p["norm_mix"][0:1], dx1), name="l0_dh")
    d_winT = _mm(du, h0, mode="tn", out_dtype=F32, name="l0_dwin", tk=2048)
    grads = dict(
        norm_mix=jnp.concatenate([dgm0, dgm1], axis=0),
        norm_ffn=jnp.concatenate([gf0["norm"], gf1["norm"]], axis=0),
        w_inT=d_winT, cm_b_in=d_bin, cm_dw=d_cmdw[0:CONV_KERNEL], cm_dw_b=d_cmdwb, cm_ln_g=d_lng, cm_ln_b=d_lnb,
        w_out=d_wout, cm_b_out=cs1,
        w_qkvT=jnp.concatenate(d_wqkvT, axis=0), at_q_norm=jnp.concatenate(dqg, axis=0), at_k_norm=jnp.concatenate(dkg, axis=0),
        w_o=d_wo,
        w_upT=[gf0["w_upT"], gf1["w_upT"]], w_down=[gf0["w_down"], gf1["w_down"]],
        ff_dw=jnp.stack([gf0["dw"], gf1["dw"]]), ff_dw_b=jnp.concatenate([gf0["dw_b"], gf1["dw_b"]], axis=0),
    )
    return loss, grad_x, grads, reduced


_BIG = ("cm_w_in", "cm_w_out", "at_w_qkv", "at_w_out", "ff_w_up", "ff_w_down")
_TRANSPOSED = ("cm_w_in", "at_w_qkv", "ff_w_up")
_SMALL = ("norm_mix", "norm_ffn", "cm_b_in", "cm_dw_b", "cm_ln_g", "cm_ln_b", "cm_b_out", "at_q_norm", "at_k_norm",
          "ff_dw_b", "cm_dw", "ff_dw")
_SMALL_SHARDED = ("cm_dw", "ff_dw")
_ORDER = ("norm_mix", "norm_ffn", "cm_w_in", "cm_b_in", "cm_dw", "cm_dw_b", "cm_ln_g", "cm_ln_b", "cm_w_out", "cm_b_out",
          "at_w_qkv", "at_q_norm", "at_k_norm", "at_w_out", "ff_w_up", "ff_dw", "ff_dw_b", "ff_w_down")


_UNITS = (("cm_w_in", 0), ("cm_w_out", 0), ("ff_w_up", 0), ("ff_w_down", 0),
          ("at_w_qkv", 0), ("at_w_out", 0), ("ff_w_up", 1), ("ff_w_down", 1))
_N_FIRST = 2
_N_LAYER0 = 4


def _unit_rows(t, n, l):
    return t[n].shape[2] if n in _TRANSPOSED else t[n].shape[1]


def _big_rows(t, units=_UNITS):
    return _pack_rows([t[n][l].T if n in _TRANSPOSED else t[n][l] for n, l in units])


def _big_unrows(packed, like):
    mats, off = {}, 0
    for n, l in _UNITS:
        rows = _unit_rows(like, n, l)
        m = packed[off:off + rows]
        off += rows
        mats[(n, l)] = m.T if n in _TRANSPOSED else m
    return {n: jnp.stack([mats[(n, l)] for l in range(like[n].shape[0])]) for n in _BIG}


def kernel(x, norm_mix, norm_ffn, cm_w_in, cm_b_in, cm_dw, cm_dw_b, cm_ln_g, cm_ln_b, cm_w_out, cm_b_out, at_w_qkv, at_q_norm, at_k_norm, at_w_out, ff_w_up, ff_dw, ff_dw_b, ff_w_down, loss_target, m_norm_mix, m_norm_ffn, m_cm_w_in, m_cm_b_in, m_cm_dw, m_cm_dw_b, m_cm_ln_g, m_cm_ln_b, m_cm_w_out, m_cm_b_out, m_at_w_qkv, m_at_q_norm, m_at_k_norm, m_at_w_out, m_ff_w_up, m_ff_dw, m_ff_dw_b, m_ff_w_down, v_norm_mix, v_norm_ffn, v_cm_w_in, v_cm_b_in, v_cm_dw, v_cm_dw_b, v_cm_ln_g, v_cm_ln_b, v_cm_w_out, v_cm_b_out, v_at_w_qkv, v_at_q_norm, v_at_k_norm, v_at_w_out, v_ff_w_up, v_ff_dw, v_ff_dw_b, v_ff_w_down):
    w = dict(norm_mix=norm_mix, norm_ffn=norm_ffn, cm_w_in=cm_w_in, cm_b_in=cm_b_in, cm_dw=cm_dw, cm_dw_b=cm_dw_b, cm_ln_g=cm_ln_g,
             cm_ln_b=cm_ln_b, cm_w_out=cm_w_out, cm_b_out=cm_b_out, at_w_qkv=at_w_qkv, at_q_norm=at_q_norm, at_k_norm=at_k_norm,
             at_w_out=at_w_out, ff_w_up=ff_w_up, ff_dw=ff_dw, ff_dw_b=ff_dw_b, ff_w_down=ff_w_down)
    m = dict(norm_mix=m_norm_mix, norm_ffn=m_norm_ffn, cm_w_in=m_cm_w_in, cm_b_in=m_cm_b_in, cm_dw=m_cm_dw, cm_dw_b=m_cm_dw_b,
             cm_ln_g=m_cm_ln_g, cm_ln_b=m_cm_ln_b, cm_w_out=m_cm_w_out, cm_b_out=m_cm_b_out, at_w_qkv=m_at_w_qkv,
             at_q_norm=m_at_q_norm, at_k_norm=m_at_k_norm, at_w_out=m_at_w_out, ff_w_up=m_ff_w_up, ff_dw=m_ff_dw,
             ff_dw_b=m_ff_dw_b, ff_w_down=m_ff_w_down)
    v = dict(norm_mix=v_norm_mix, norm_ffn=v_norm_ffn, cm_w_in=v_cm_w_in, cm_b_in=v_cm_b_in, cm_dw=v_cm_dw, cm_dw_b=v_cm_dw_b,
             cm_ln_g=v_cm_ln_g, cm_ln_b=v_cm_ln_b, cm_w_out=v_cm_w_out, cm_b_out=v_cm_b_out, at_w_qkv=v_at_w_qkv,
             at_q_norm=v_at_q_norm, at_k_norm=v_at_k_norm, at_w_out=v_at_w_out, ff_w_up=v_ff_w_up, ff_dw=v_ff_dw,
             ff_dw_b=v_ff_dw_b, ff_w_down=v_ff_w_down)
    S, D = x.shape[1], x.shape[2]
    F2 = ff_dw_b.shape[1]
    H3 = at_q_norm.shape[1]
    me = 4 * lax.axis_index("x") + 2 * lax.axis_index("y") + lax.axis_index("c")

    ix, iy, ic = lax.axis_index("x"), lax.axis_index("y"), lax.axis_index("c")
    chip = 2 * ix + iy
    core = jnp.stack([ic]).astype(jnp.int32)
    w_rows = _big_rows(w)
    unit_rows = [_unit_rows(w, n, l) for n, l in _UNITS]
    n_first = sum(unit_rows[:_N_FIRST])
    n_layer0 = sum(unit_rows[:_N_LAYER0])
    w_wire = w_rows.astype(MXU_DTYPE)

    def unpack(gathered, units, rows):
        full, off = {}, 0
        for (n, l), r in zip(units, rows):
            full[(n, l)] = gathered[:, off:off + r, :].reshape(N_DEV * r, D)
            off += r
        out = {}
        if ("cm_w_in", 0) in full:
            out.update(w_inT=full[("cm_w_in", 0)], w_out=full[("cm_w_out", 0)])
        if ("at_w_qkv", 0) in full:
            out.update(w_qkvT=full[("at_w_qkv", 0)], w_o=full[("at_w_out", 0)],
                       w_upT=[_interleave_rows(full[("ff_w_up", l)]) for l in range(2)],
                       w_down=[full[("ff_w_down", l)] for l in range(2)])
        return out

    first = _all_gather(w_wire[:n_first], name="gather_first")
    late_weights = (_gather_exchange(w_wire[n_first:]),
                    lambda carried: unpack(carried[0], _UNITS[_N_FIRST:], unit_rows[_N_FIRST:]))
    small_sh = _flat_pack([cm_dw, ff_dw], D)
    small_g = _all_gather(small_sh, name="gather_small")
    cm_dw_full = jnp.concatenate([_flat_unpack(small_g[j], [cm_dw.shape, ff_dw.shape])[0][0] for j in range(N_DEV)], axis=-1)
    ff_dw_full = jnp.concatenate([_flat_unpack(small_g[j], [cm_dw.shape, ff_dw.shape])[1] for j in range(N_DEV)], axis=-1)

    p = dict(
        norm_mix=norm_mix, norm_ffn=norm_ffn, cm_b_in=cm_b_in, cm_dw=cm_dw_full, cm_dw_b=cm_dw_b, cm_ln_g=cm_ln_g, cm_ln_b=cm_ln_b,
        cm_b_out=cm_b_out, at_q_norm=at_q_norm[0], at_k_norm=at_k_norm[0],
        ff_dw=jnp.stack([_interleave_cols(ff_dw_full[l]) for l in range(ff_dw_full.shape[0])]),
        ff_dw_b=_interleave_cols(ff_dw_b),
        **unpack(first, _UNITS[:_N_FIRST], unit_rows[:_N_FIRST]),
    )

    def pack(pieces):
        return jnp.concatenate([t.reshape(N_DEV, t.shape[0] // N_DEV, D) for t in pieces], axis=1)

    def reduce_start(pieces, tag):
        g_rows = pack(pieces)
        sib = _rs_sibling(g_rows.astype(WIRE_DTYPE), name=f"reduce{tag}_sibling")
        return g_rows, sib, _chip_partials(g_rows, sib, core, name=f"reduce{tag}_add")

    def early_reduce(gd):
        ffn = [_deinterleave_rows(gd["w_upT"]), gd["w_down"]]
        tag, pieces = (2, [gd["w_qkvT"], gd["w_o"]] + ffn) if "w_qkvT" in gd else (1, ffn)
        g_rows, sib, part = reduce_start(pieces, tag)
        return _chips_exchange(part), lambda carried: (g_rows, sib, carried[0])

    loss8, grad_x, g, (reduced2, reduced1) = _local_step(x[0], loss_target[0], p, late_weights, early_reduce)
    loss = lax.psum(loss8[0, 0], ("x", "y", "c"))
    g_rows0, sib0, part0 = reduce_start([g["w_inT"], g["w_out"]], 0)
    reduced0 = (g_rows0, sib0, _rs_chips(part0, name="reduce0_chips"))
    slots = jnp.stack([me, chip, 0 * me, 0 * me + 1, 0 * me + 2]).astype(jnp.int32)
    m_rows, v_rows = _big_rows(m), _big_rows(v)
    updated = []
    for tag, (g_rows, sib, recv), rows in ((0, reduced0, slice(0, n_first)), (1, reduced1, slice(n_first, n_layer0)),
                                           (2, reduced2, slice(n_layer0, None))):
        updated.append(_adamw(w_rows[rows], m_rows[rows], v_rows[rows], [g_rows, sib, recv, recv, recv], slots, name=f"adamw_big{tag}"))
    big = [_big_unrows(jnp.concatenate([u[k] for u in updated], axis=0), w) for k in range(4)]

    g_small = dict(g)
    g_small["cm_b_in"] = g["cm_b_in"]
    g_small["at_q_norm"] = g["at_q_norm"][None]
    g_small["at_k_norm"] = g["at_k_norm"][None]
    g_small["ff_dw_b"] = _deinterleave_cols(g["ff_dw_b"])
    g_small["cm_dw"] = g["cm_dw"][None]
    g_small["ff_dw"] = jnp.stack([_deinterleave_cols(g["ff_dw"][l]) for l in range(g["ff_dw"].shape[0])])
    small_shapes = [g_small[n].shape for n in _SMALL]
    gs_parts = _all_gather(_flat_pack([g_small[n] for n in _SMALL], D), name="gather_small_grads")

    def embed(t, n):
        if n not in _SMALL_SHARDED:
            return t
        full_shape = t.shape[:-1] + (t.shape[-1] * N_DEV,)
        return lax.dynamic_update_slice_in_dim(jnp.zeros(full_shape, F32), t, me * t.shape[-1], axis=t.ndim - 1)

    packs = [_flat_pack([embed(tree[n], n) for n in _SMALL], D) for tree in (w, m, v)]
    gs, ds_, ms, vs = _adamw(packs[0], packs[1], packs[2], [gs_parts] * N_DEV, jnp.arange(N_DEV, dtype=jnp.int32),
                             name="adamw_small")
    small = []
    for t in (gs, ds_, ms, vs):
        un = dict(zip(_SMALL, _flat_unpack(t, small_shapes)))
        for n in _SMALL_SHARDED:
            width = w[n].shape[-1]
            un[n] = lax.dynamic_slice_in_dim(un[n], me * width, width, axis=un[n].ndim - 1)
        small.append({n: un[n].reshape(w[n].shape) for n in _SMALL})

    outs = [loss, grad_x[None]]
    for k in range(4):
        for n in _ORDER:
            outs.append(big[k][n] if n in _BIG else small[k][n])
    return tuple(outs)
```

```python
import functools

import jax
import jax.numpy as jnp
import numpy as np
from jax import lax
from jax.experimental import pallas as pl
from jax.experimental.pallas import tpu as pltpu

F32 = jnp.float32
MXU_DTYPE = jnp.bfloat16
WIRE_DTYPE = jnp.bfloat16
EPS = 1e-6
NEG = -1e30
HEAD_DIM = 128
BLOCK = 128
DILATED_GROUPS = ((128, 1), (512, 4), (2048, 16))
ALIBI_MAX = 8.0
CONV_KERNEL = 31
CONV_HALO = 32
CONV_ROWS = 64
FFN_KERNEL = 3
FFN_HALO = 16
FFN_ROWS = 64
ADAM_LR, ADAM_B1, ADAM_B2, ADAM_EPS, ADAM_WD, ADAM_STEP = 0.001, 0.9, 0.999, 1e-08, 0.01, 10
V7X_VMEM_BYTES = 64 * 1024 * 1024
VMEM_LIMIT = V7X_VMEM_BYTES * 3 // 4
N_DEV = 8
MESH = pl.DeviceIdType.MESH


def _pick(n, target, align):
    if n <= target:
        return n
    best = None
    for t in range(align, target + 1, align):
        if n % t == 0:
            best = t
    assert best is not None, (n, target, align)
    return best


def _params(*sem, vmem=VMEM_LIMIT):
    return pltpu.CompilerParams(dimension_semantics=sem, vmem_limit_bytes=vmem)


def _sigmoid(x):
    return 1.0 / (1.0 + jnp.exp(-x))


_DIMS = {"nn": ((1,), (0,)), "nt": ((1,), (1,)), "tn": ((0,), (0,))}


def _mm(a, b, *, mode, out_dtype, name, tm=1024, tn=1024, tk=None, bias=None, residual=None, b_off=0, b_len=None,
        post=None, keep_main=True):
    if mode == "tn":
        K, M = a.shape
    else:
        M, K = a.shape
    if mode == "nt":
        N = b.shape[0] if b_len is None else b_len
    else:
        N = b.shape[1]
    if b_len is not None:
        assert mode == "nt" or (mode == "nn" and K == b_len)
    tm = _pick(M, tm, 128 if mode == "tn" else 16)
    tn = _pick(N, tn, 128)
    tk = K if tk is None else _pick(K, tk, 128 if mode != "tn" else 16)
    nk = K // tk
    unit = tn if mode == "nt" else tk
    assert b_off % unit == 0
    kb0 = b_off // unit
    if mode == "tn":
        a_spec = pl.BlockSpec((tk, tm), lambda i, j, k: (k, i))
    else:
        a_spec = pl.BlockSpec((tm, tk), lambda i, j, k: (i, k))
    if mode == "nt":
        b_spec = pl.BlockSpec((tn, tk), lambda i, j, k: (j + kb0, k))
    else:
        b_spec = pl.BlockSpec((tk, tn), lambda i, j, k: (k + kb0, j))
    in_specs = [a_spec, b_spec]
    args = [a, b]
    if bias is not None:
        in_specs.append(pl.BlockSpec((1, tn), lambda i, j, k: (0, j)))
        args.append(bias)
    if residual is not None:
        in_specs.append(pl.BlockSpec((tm, tn), lambda i, j, k: (i, j)))
        args.append(residual)
    has_bias, has_res = bias is not None, residual is not None
    kinds = {"tile": ((tm, tn), lambda i, j, k: (i, j)), "row": ((1, tn), lambda i, j, k: (0, j)),
             "lanes": ((8, 128), lambda i, j, k: (0, 0))}
    post_in = [] if post is None else post.ins
    post_out = [] if post is None else post.outs
    if post is not None:
        assert tn == N
        for arr, kind in post_in:
            in_specs.append(pl.BlockSpec(*kinds[kind]))
            args.append(arr)
    out_specs = [pl.BlockSpec((tm, tn), lambda i, j, k: (i, j))] if keep_main else []
    out_shape = [jax.ShapeDtypeStruct((M, N), out_dtype)] if keep_main else []
    for kind, dtype in post_out:
        out_specs.append(pl.BlockSpec(*kinds[kind]))
        out_shape.append(jax.ShapeDtypeStruct({"tile": (M, N), "row": (1, N), "lanes": (8, 128)}[kind], dtype))
    accumulates = any(kind != "tile" for kind, _ in post_out)

    def body(*refs):
        a_ref, b_ref = refs[0], refs[1]
        pos = 2
        bias_ref = res_ref = None
        if has_bias:
            bias_ref = refs[pos]
            pos += 1
        if has_res:
            res_ref = refs[pos]
            pos += 1
        pin_refs = refs[pos:pos + len(post_in)]
        pos += len(post_in)
        o_ref = refs[pos] if keep_main else None
        pos += 1 if keep_main else 0
        pout_refs = refs[pos:pos + len(post_out)]
        pos += len(post_out)
        acc_ref = refs[pos] if nk > 1 else None

        def finish(acc):
            if has_bias:
                acc = acc + bias_ref[...]
            if has_res:
                acc = acc + res_ref[...]
            if keep_main:
                o_ref[...] = acc.astype(o_ref.dtype)
            if post is not None:
                post.fn(acc, pin_refs, pout_refs, pl.program_id(0) == 0)

        part = lax.dot_general(a_ref[...].astype(MXU_DTYPE), b_ref[...].astype(MXU_DTYPE), (_DIMS[mode], ((), ())),
                               preferred_element_type=F32)
        if nk == 1:
            finish(part)
        else:
            k = pl.program_id(2)

            @pl.when(k == 0)
            def _():
                acc_ref[...] = part

            @pl.when(jnp.logical_and(k > 0, k < nk - 1))
            def _():
                acc_ref[...] += part

            @pl.when(k == nk - 1)
            def _():
                finish(acc_ref[...] + part)

    outs = pl.pallas_call(
        body, name=name, grid=(M // tm, N // tn, nk), in_specs=in_specs, out_specs=out_specs, out_shape=out_shape,
        scratch_shapes=[pltpu.VMEM((tm, tn), F32)] if nk > 1 else [],
        compiler_params=_params("arbitrary" if accumulates else "parallel", "parallel", "arbitrary"),
    )(*args)
    return outs[0] if post is None else outs


def _mm_post(a, b, post, *, name, tm=512, rows=256):
    M, K = a.shape
    N = b.shape[1]
    tm = _pick(M, tm, 16)
    rows = _pick(tm, rows, 16)
    kinds = {"tile": ((tm, N), lambda i: (i, 0)), "row": ((1, N), lambda i: (0, 0)), "lanes": ((8, 128), lambda i: (0, 0))}
    n_in = len(post.ins)

    def body(a_ref, b_ref, *rest):
        first = pl.program_id(0) == 0

        def product(c):
            return jnp.dot(a_ref[c * rows:(c + 1) * rows, :].astype(MXU_DTYPE), b_ref[...].astype(MXU_DTYPE),
                           preferred_element_type=F32)

        def chunk(refs, specs, c):
            return [r.at[pl.ds(c * rows, rows), :] if kind == "tile" else r for r, kind in zip(refs, specs)]

        acc = product(0)
        for c in range(tm // rows):
            nxt = product(c + 1) if c + 1 < tm // rows else None
            post.fn(acc, chunk(rest[:n_in], [k for _, k in post.ins], c), chunk(rest[n_in:], [k for k, _ in post.outs], c),
                    jnp.logical_and(first, c == 0))
            acc = nxt

    return pl.pallas_call(
        body, name=name, grid=(M // tm,),
        in_specs=[pl.BlockSpec((tm, K), lambda i: (i, 0)), pl.BlockSpec((K, N), lambda i: (0, 0), pipeline_mode=pl.Buffered(1))]
        + [pl.BlockSpec(*kinds[kind]) for _, kind in post.ins],
        out_specs=[pl.BlockSpec(*kinds[kind]) for kind, _ in post.outs],
        out_shape=[jax.ShapeDtypeStruct({"tile": (M, N), "row": (1, N), "lanes": (8, 128)}[kind], dtype) for kind, dtype in post.outs],
        compiler_params=_params("arbitrary", vmem=V7X_VMEM_BYTES * 7 // 8),
    )(a, b, *[arr for arr, _ in post.ins])


class _Post:
    def __init__(self, ins, outs, fn):
        self.ins, self.outs, self.fn = ins, outs, fn


def _accumulate(ref, value, first):
    @pl.when(first)
    def _():
        ref[...] = value

    @pl.when(jnp.logical_not(first))
    def _():
        ref[...] += value


def _post_rms(g):
    def fn(acc, ins, outs, first):
        r = lax.rsqrt(jnp.mean(acc * acc, axis=-1, keepdims=True) + EPS)
        outs[0][...] = (acc * r * ins[0][...]).astype(outs[0].dtype)

    return _Post([(g, "row")], [("tile", MXU_DTYPE)], fn)


def _post_loss(target):
    def fn(acc, ins, outs, first):
        e = acc - ins[0][...]
        dy = e * (1.0 / acc.shape[-1])
        outs[0][...] = dy
        outs[1][...] = dy.astype(outs[1].dtype)
        part = jnp.sum(jnp.sum(e * e, axis=0, keepdims=True), axis=1, keepdims=True) * (0.5 / acc.shape[-1])
        _accumulate(outs[2], jnp.broadcast_to(part, outs[2].shape), first)

    return _Post([(target, "tile")], [("tile", F32), ("tile", MXU_DTYPE), ("lanes", F32)], fn)


def _post_rms_bwd(x, g, dres):
    def fn(acc, ins, outs, first):
        xv = ins[0][...]
        r = lax.rsqrt(jnp.mean(xv * xv, axis=-1, keepdims=True) + EPS)
        xh = xv * r
        gy = acc * ins[1][...]
        dx = r * (gy - xh * jnp.mean(gy * xh, axis=-1, keepdims=True)) + ins[2][...]
        outs[0][...] = dx
        outs[1][...] = dx.astype(outs[1].dtype)
        _accumulate(outs[2], jnp.sum(acc * xh, axis=0, keepdims=True), first)
        _accumulate(outs[3], jnp.sum(dx, axis=0, keepdims=True), first)

    return _Post([(x, "tile"), (g, "row"), (dres, "tile")], [("tile", F32), ("tile", MXU_DTYPE), ("row", F32), ("row", F32)], fn)


SUB_TILE = 512


def _sub_spec(dil, ts, cols):
    return pl.BlockSpec((dil, ts // dil, cols), lambda i: (0, i, 0))


def _tok_to_sub(tok_ref, dst_ref, dil):
    nc, ts, _ = tok_ref.shape
    for c in range(nc):
        for r in range(dil):
            dst_ref[r, :, c * 128:(c + 1) * 128] = tok_ref.at[c][pl.ds(r, ts // dil, stride=dil), :].astype(dst_ref.dtype)


def _sub_to_tok(src_ref, tok_ref, dil):
    nc, ts, _ = tok_ref.shape
    for c in range(nc):
        for r in range(dil):
            tok_ref.at[c][pl.ds(r, ts // dil, stride=dil), :] = src_ref[r, :, c * 128:(c + 1) * 128].astype(F32)


def _rms_fwd(x, g, *, name, subs=()):
    S, D = x.shape
    ts = _pick(S, SUB_TILE, 16 * max(subs, default=1))
    NC = D // 128

    def body(x_ref, g_ref, h_ref, *rest):
        xv = x_ref[...]
        r = lax.rsqrt(jnp.mean(xv * xv, axis=-1, keepdims=True) + EPS)
        h = xv * r * g_ref[...]
        h_ref[...] = h.astype(h_ref.dtype)
        if subs:
            tok_ref = rest[-1]
            for c in range(NC):
                tok_ref[c] = h[:, c * 128:(c + 1) * 128]
            for dil, dst_ref in zip(subs, rest):
                _tok_to_sub(tok_ref, dst_ref, dil)

    row = pl.BlockSpec((ts, D), lambda i: (i, 0))
    outs = pl.pallas_call(
        body, name=name, grid=(S // ts,),
        in_specs=[row, pl.BlockSpec((1, D), lambda i: (0, 0))],
        out_specs=[row] + [_sub_spec(dil, ts, D) for dil in subs],
        out_shape=[jax.ShapeDtypeStruct((S, D), MXU_DTYPE)] + [jax.ShapeDtypeStruct((dil, S // dil, D), MXU_DTYPE) for dil in subs],
        scratch_shapes=[pltpu.VMEM((NC, ts, 128), F32)] if subs else [],
        compiler_params=_params("parallel"),
    )(x, g)
    return outs if subs else outs[0]


def _rms_bwd(x, g, dhs, dres, *, name, dh_subs=()):
    S, D = x.shape
    ts = _pick(S, SUB_TILE, 16 * max([dil for _, dil in dh_subs], default=1))
    n_dh, n_sub = len(dhs), len(dh_subs)
    NC = D // 128

    def body(*refs):
        x_ref, g_ref = refs[0], refs[1]
        dh_refs = refs[2:2 + n_dh]
        sub_refs = refs[2 + n_dh:2 + n_dh + n_sub]
        dres_ref, dx_ref, dxb_ref, dg_ref, cs_ref = refs[2 + n_dh + n_sub:7 + n_dh + n_sub]
        i = pl.program_id(0)
        xv = x_ref[...]
        r = lax.rsqrt(jnp.mean(xv * xv, axis=-1, keepdims=True) + EPS)
        xh = xv * r
        dhv = dh_refs[0][...].astype(F32)
        for t in dh_refs[1:]:
            dhv = dhv + t[...].astype(F32)
        for (_, dil), sub_ref in zip(dh_subs, sub_refs):
            tok_ref = refs[-1]
            _sub_to_tok(sub_ref, tok_ref, dil)
            dhv = dhv + jnp.concatenate([tok_ref[c] for c in range(NC)], axis=1)
        gy = dhv * g_ref[...]
        dx = r * (gy - xh * jnp.mean(gy * xh, axis=-1, keepdims=True)) + dres_ref[...]
        dx_ref[...] = dx
        dxb_ref[...] = dx.astype(dxb_ref.dtype)
        dg = jnp.sum(dhv * xh, axis=0, keepdims=True)
        cs = jnp.sum(dx, axis=0, keepdims=True)

        @pl.when(i == 0)
        def _():
            dg_ref[...] = dg
            cs_ref[...] = cs

        @pl.when(i > 0)
        def _():
            dg_ref[...] += dg
            cs_ref[...] += cs

    row = pl.BlockSpec((ts, D), lambda i: (i, 0))
    vec = pl.BlockSpec((1, D), lambda i: (0, 0))
    return pl.pallas_call(
        body, name=name, grid=(S // ts,),
        in_specs=[row, vec] + [row] * n_dh + [_sub_spec(dil, ts, D) for _, dil in dh_subs] + [row],
        out_specs=[row, row, vec, vec],
        out_shape=[jax.ShapeDtypeStruct((S, D), F32), jax.ShapeDtypeStruct((S, D), MXU_DTYPE),
                   jax.ShapeDtypeStruct((1, D), F32), jax.ShapeDtypeStruct((1, D), F32)],
        scratch_shapes=[pltpu.VMEM((NC, ts, 128), F32)] if n_sub else [],
        compiler_params=_params("arbitrary"),
    )(x, g, *dhs, *[a for a, _ in dh_subs], dres)


def _conv_phases(ph_ref, ts):
    n = ts + CONV_HALO - 8
    for b in range(1, 8):
        ph_ref[b, 0:n, :] = ph_ref[0, pl.ds(b, n), :]


def _phase_taps(base, step=1):
    groups = {}
    for k in range(CONV_KERNEL):
        a, b = divmod(base + step * k, 8)
        groups.setdefault(b, []).append((a, k))
    out = []
    for b in sorted(groups):
        ak = sorted(groups[b])
        assert [a for a, _ in ak] == list(range(ak[0][0], ak[0][0] + len(ak)))
        out.append((b, ak[0][0], [k for _, k in ak]))
    return out


def _cm_fwd(u, dw, dw_b, ln_g, ln_b, *, name, exchange=None):
    S, D2 = u.shape
    D = D2 // 2
    ts = _pick(S, 256, CONV_HALO)
    hb = ts // CONV_HALO
    ex_in, ex_out, ex_scr = ([], [], []) if exchange is None else (exchange.operands, exchange.out_shapes, exchange.scratch)

    def body(u_ref, up_ref, dw_ref, dwb_ref, g_ref, b_ref, *rest):
        xi = rest[:len(ex_in)]
        c_ref, s_ref = rest[len(ex_in):len(ex_in) + 2]
        xo = rest[len(ex_in) + 2:len(ex_in) + 2 + len(ex_out)]
        ext_ref = rest[len(ex_in) + 2 + len(ex_out)]
        i = pl.program_id(0)
        if exchange is not None:
            exchange.emit(i, S // ts, xi, xo, rest[len(ex_in) + 3 + len(ex_out):])
        prev = up_ref[:, :D] * _sigmoid(up_ref[:, D:])
        ext_ref[0:CONV_HALO, :] = jnp.where(i > 0, prev, 0.0)
        ext_ref[CONV_HALO:CONV_HALO + ts, :] = u_ref[:, :D] * _sigmoid(u_ref[:, D:])
        for cc in range(D // 128):
            sl = slice(cc * 128, (cc + 1) * 128)
            acc = jnp.zeros((ts, 128), F32) + dwb_ref[:, sl]
            for k in range(CONV_KERNEL):
                acc = acc + dw_ref[k:k + 1, sl] * ext_ref[pl.ds(CONV_HALO - (CONV_KERNEL - 1) + k, ts), sl]
            c_ref[:, sl] = acc
        c = c_ref[...]
        mu = jnp.mean(c, axis=-1, keepdims=True)
        xc = c - mu
        rstd = lax.rsqrt(jnp.mean(xc * xc, axis=-1, keepdims=True) + EPS)
        y = xc * rstd * g_ref[...] + b_ref[...]
        s_ref[...] = (y * _sigmoid(y)).astype(s_ref.dtype)

    vec = pl.BlockSpec((1, D), lambda i: (0, 0))
    return pl.pallas_call(
        body, name=name, grid=(S // ts,),
        in_specs=[pl.BlockSpec((ts, D2), lambda i: (i, 0)),
                  pl.BlockSpec((CONV_HALO, D2), lambda i: (jnp.maximum(i * hb - 1, 0), 0)),
                  pl.BlockSpec((CONV_KERNEL, D), lambda i: (0, 0)), vec, vec, vec] + [_ANY] * len(ex_in),
        out_specs=[pl.BlockSpec((ts, D), lambda i: (i, 0)), pl.BlockSpec((ts, D), lambda i: (i, 0))] + [_ANY] * len(ex_out),
        out_shape=[jax.ShapeDtypeStruct((S, D), F32), jax.ShapeDtypeStruct((S, D), MXU_DTYPE)] + list(ex_out),
        scratch_shapes=[pltpu.VMEM((ts + CONV_HALO, D), F32)] + list(ex_scr),
        compiler_params=_params("parallel" if exchange is None else "arbitrary"),
    )(u, u, dw, dw_b, ln_g, ln_b, *ex_in)


def _cm_ln_bwd(c, ds, ln_g, ln_b, *, name):
    S, D = c.shape
    ts = _pick(S, 512, 16)

    def body(c_ref, ds_ref, g_ref, b_ref, dc_ref, dg_ref, db_ref):
        i = pl.program_id(0)
        cv = c_ref[...]
        mu = jnp.mean(cv, axis=-1, keepdims=True)
        xc = cv - mu
        rstd = lax.rsqrt(jnp.mean(xc * xc, axis=-1, keepdims=True) + EPS)
        xh = xc * rstd
        y = xh * g_ref[...] + b_ref[...]
        sg = _sigmoid(y)
        dy = ds_ref[...].astype(F32) * (sg * (1.0 + y * (1.0 - sg)))
        gy = dy * g_ref[...]
        dc_ref[...] = rstd * (gy - jnp.mean(gy, axis=-1, keepdims=True) - xh * jnp.mean(gy * xh, axis=-1, keepdims=True))
        dg = jnp.sum(dy * xh, axis=0, keepdims=True)
        db = jnp.sum(dy, axis=0, keepdims=True)

        @pl.when(i == 0)
        def _():
            dg_ref[...] = dg
            db_ref[...] = db

        @pl.when(i > 0)
        def _():
            dg_ref[...] += dg
            db_ref[...] += db

    row = pl.BlockSpec((ts, D), lambda i: (i, 0))
    vec = pl.BlockSpec((1, D), lambda i: (0, 0))
    return pl.pallas_call(
        body, name=name, grid=(S // ts,), in_specs=[row, row, vec, vec], out_specs=[row, vec, vec],
        out_shape=[jax.ShapeDtypeStruct((S, D), F32), jax.ShapeDtypeStruct((1, D), F32), jax.ShapeDtypeStruct((1, D), F32)],
        compiler_params=_params("arbitrary"),
    )(c, ds, ln_g, ln_b)


def _cm_conv_bwd(dc, u, dw, *, name, exchange=None):
    ex_in, ex_out, ex_scr = ([], [], []) if exchange is None else (exchange.operands, exchange.out_shapes, exchange.scratch)
    S, D2 = u.shape
    D = D2 // 2
    ts = _pick(S, 256, CONV_HALO)
    hb = ts // CONV_HALO
    n_t = S // ts
    last_h = S // CONV_HALO - 1

    rc = _pick(ts, CONV_ROWS, 8)

    def fold8(v):
        out = v[0:8]
        for j in range(1, v.shape[0] // 8):
            out = out + v[8 * j:8 * j + 8]
        return out

    def body(dc_ref, dcn_ref, u_ref, up_ref, dw_ref, *rest):
        xi = rest[:len(ex_in)]
        du_ref, ddw_ref, ddwb_ref, dbin_ref = rest[len(ex_in):len(ex_in) + 4]
        xo = rest[len(ex_in) + 4:len(ex_in) + 4 + len(ex_out)]
        dph_ref, gph_ref, dgl_ref = rest[len(ex_in) + 4 + len(ex_out):len(ex_in) + 7 + len(ex_out)]
        i = pl.program_id(0)
        if exchange is not None:
            exchange.emit(i, n_t, xi, xo, rest[len(ex_in) + 7 + len(ex_out):])
        dph_ref[0, 0:ts, :] = dc_ref[...]
        dph_ref[0, ts:ts + CONV_HALO, :] = jnp.where(i < n_t - 1, dcn_ref[...], 0.0)
        prev = up_ref[:, :D] * _sigmoid(up_ref[:, D:])
        gph_ref[0, 0:CONV_HALO, :] = jnp.where(i > 0, prev, 0.0)
        gph_ref[0, CONV_HALO:CONV_HALO + ts, :] = u_ref[:, :D] * _sigmoid(u_ref[:, D:])
        _conv_phases(dph_ref, ts)
        _conv_phases(gph_ref, ts)

        @pl.when(i == 0)
        def _():
            ddw_ref[...] = jnp.zeros_like(ddw_ref)
            ddwb_ref[...] = jnp.zeros_like(ddwb_ref)
            dbin_ref[...] = jnp.zeros_like(dbin_ref)

        for cc in range(D // 128):
            sl = slice(cc * 128, (cc + 1) * 128)
            sl2 = slice(D + cc * 128, D + (cc + 1) * 128)
            wk = [dw_ref[k:k + 1, sl] for k in range(CONV_KERNEL)]
            acc_a, acc_g = jnp.zeros((8, 128), F32), jnp.zeros((8, 128), F32)
            dgl = jnp.zeros((ts, 128), F32)
            for b, a0, taps in _phase_taps(CONV_KERNEL - 1, -1):
                for j, k in enumerate(taps):
                    dgl = dgl + wk[k] * dph_ref[b, 8 * (a0 + j):8 * (a0 + j) + ts, sl]
            dgl_ref[...] = dgl
            for r0 in range(0, ts, rc):
                dglu = dgl_ref[r0:r0 + rc, :]
                av = u_ref[r0:r0 + rc, sl]
                sg = _sigmoid(u_ref[r0:r0 + rc, sl2])
                da = dglu * sg
                dg = dglu * av * sg * (1.0 - sg)
                du_ref[r0:r0 + rc, sl] = da.astype(du_ref.dtype)
                du_ref[r0:r0 + rc, sl2] = dg.astype(du_ref.dtype)
                acc_a = acc_a + fold8(da)
                acc_g = acc_g + fold8(dg)
            dbin_ref[:, sl] += jnp.sum(acc_a, axis=0, keepdims=True)
            dbin_ref[:, sl2] += jnp.sum(acc_g, axis=0, keepdims=True)
            for gi, (b, a0, taps) in enumerate(_phase_taps(CONV_HALO - (CONV_KERNEL - 1))):
                accs = [jnp.zeros((8, 128), F32) for _ in taps]
                accb = jnp.zeros((8, 128), F32)
                for r0 in range(0, ts, rc):
                    dcc = dph_ref[0, r0:r0 + rc, sl]
                    win = gph_ref[b, 8 * a0 + r0:8 * (a0 + len(taps) - 1) + r0 + rc, sl]
                    for j in range(len(taps)):
                        accs[j] = accs[j] + fold8(dcc * win[8 * j:8 * j + rc])
                    if gi == 0:
                        accb = accb + fold8(dcc)
                for j, k in enumerate(taps):
                    ddw_ref[k:k + 1, sl] += jnp.sum(accs[j], axis=0, keepdims=True)
                if gi == 0:
                    ddwb_ref[:, sl] += jnp.sum(accb, axis=0, keepdims=True)

    return pl.pallas_call(
        body, name=name, grid=(n_t,),
        in_specs=[pl.BlockSpec((ts, D), lambda i: (i, 0)),
                  pl.BlockSpec((CONV_HALO, D), lambda i: (jnp.minimum((i + 1) * hb, last_h), 0)),
                  pl.BlockSpec((ts, D2), lambda i: (i, 0)),
                  pl.BlockSpec((CONV_HALO, D2), lambda i: (jnp.maximum(i * hb - 1, 0), 0)),
                  pl.BlockSpec((CONV_KERNEL, D), lambda i: (0, 0))] + [_ANY] * len(ex_in),
        out_specs=[pl.BlockSpec((ts, D2), lambda i: (i, 0)), pl.BlockSpec((CONV_HALO, D), lambda i: (0, 0)),
                   pl.BlockSpec((1, D), lambda i: (0, 0)), pl.BlockSpec((1, D2), lambda i: (0, 0))] + [_ANY] * len(ex_out),
        out_shape=[jax.ShapeDtypeStruct((S, D2), MXU_DTYPE), jax.ShapeDtypeStruct((CONV_HALO, D), F32),
                   jax.ShapeDtypeStruct((1, D), F32), jax.ShapeDtypeStruct((1, D2), F32)] + list(ex_out),
        scratch_shapes=[pltpu.VMEM((8, ts + CONV_HALO, D), F32), pltpu.VMEM((8, ts + CONV_HALO, D), F32),
                        pltpu.VMEM((ts, 128), F32)] + list(ex_scr),
        compiler_params=_params("arbitrary"),
    )(dc, dc, u, u, dw, *ex_in)


def _ffn_cols(F2):
    return _pick(F2, 1024, 256)


def _ffn_up_act(hf, wupT, dw, dw_b, *, name):
    S, D = hf.shape
    F2 = wupT.shape[0]
    ts = _pick(S, 512, 16)
    tc = _ffn_cols(F2)
    n_ct = F2 // tc
    hb = ts // FFN_HALO
    rc = _pick(ts, FFN_ROWS, 16)

    def body(h_ref, hp_ref, wt_ref, w_ref, b_ref, up_ref, a_ref, he_ref, ext_ref):
        i = pl.program_id(0)
        he_ref[0:FFN_HALO, :] = hp_ref[...]
        he_ref[FFN_HALO:FFN_HALO + ts, :] = h_ref[...]

        def product(j):
            return lax.dot_general(he_ref[...].astype(MXU_DTYPE), wt_ref[j * tc:(j + 1) * tc, :].astype(MXU_DTYPE),
                                   (_DIMS["nt"], ((), ())), preferred_element_type=F32)

        def conv_gate(j, res):
            ext = ext_ref.at[j % 2]
            upv = res.astype(up_ref.dtype)
            up_ref[:, j * tc:(j + 1) * tc] = upv[FFN_HALO:FFN_HALO + ts]
            ext[0:FFN_HALO, :] = jnp.where(i > 0, upv[0:FFN_HALO].astype(F32), 0.0)
            ext[FFN_HALO:FFN_HALO + ts, :] = upv[FFN_HALO:FFN_HALO + ts].astype(F32)
            for q in range(tc // 256):
                sls = [slice(q * 256 + half * 128, q * 256 + half * 128 + 128) for half in range(2)]
                gls = [slice(j * tc + sl.start, j * tc + sl.stop) for sl in sls]
                wk = [[w_ref[k:k + 1, gl] for k in range(FFN_KERNEL)] for gl in gls]
                bb = [b_ref[:, gl] for gl in gls]
                for r0 in range(0, ts, rc):
                    gt, vl = [bb[h] + sum(wk[h][k] * ext[pl.ds(FFN_HALO + r0 - 2 + k, rc), sls[h]] for k in range(FFN_KERNEL))
                              for h in range(2)]
                    a_ref[r0:r0 + rc, j * (tc // 2) + q * 128:j * (tc // 2) + (q + 1) * 128] = (gt * _sigmoid(gt) * vl).astype(a_ref.dtype)

        res = product(0)
        for j in range(n_ct):
            nxt = product(j + 1) if j + 1 < n_ct else None
            conv_gate(j, res)
            res = nxt

    return pl.pallas_call(
        body, name=name, grid=(S // ts,),
        in_specs=[pl.BlockSpec((ts, D), lambda i: (i, 0)),
                  pl.BlockSpec((FFN_HALO, D), lambda i: (jnp.maximum(i * hb - 1, 0), 0)),
                  pl.BlockSpec((F2, D), lambda i: (0, 0), pipeline_mode=pl.Buffered(1)),
                  pl.BlockSpec((FFN_KERNEL, F2), lambda i: (0, 0)), pl.BlockSpec((1, F2), lambda i: (0, 0))],
        out_specs=[pl.BlockSpec((ts, F2), lambda i: (i, 0)), pl.BlockSpec((ts, F2 // 2), lambda i: (i, 0))],
        out_shape=[jax.ShapeDtypeStruct((S, F2), MXU_DTYPE), jax.ShapeDtypeStruct((S, F2 // 2), MXU_DTYPE)],
        scratch_shapes=[pltpu.VMEM((ts + FFN_HALO, D), hf.dtype), pltpu.VMEM((2, ts + FFN_HALO, tc), F32)],
        compiler_params=_params("parallel", vmem=V7X_VMEM_BYTES * 7 // 8),
    )(hf, hf, wupT, dw, dw_b)


def _ffn_act_bwd(up, dact, dw, dw_b, *, name, exchange=None):
    ex_in, ex_out, ex_scr = ([], [], []) if exchange is None else (exchange.operands, exchange.out_shapes, exchange.scratch)
    S, F2 = up.shape
    ts = _pick(S, 512, 16)
    tc = _ffn_cols(F2)
    hb = ts // FFN_HALO
    n_t = S // ts
    last_h = S // FFN_HALO - 1
    E = ts + FFN_HALO

    rc = _pick(ts, FFN_ROWS, 16)

    def fold8(v):
        out = v[0:8]
        for j in range(1, v.shape[0] // 8):
            out = out + v[8 * j:8 * j + 8]
        return out

    def body(u_ref, up_ref, un_ref, da_ref, dan_ref, w_ref, b_ref, *rest):
        xi = rest[:len(ex_in)]
        dup_ref, ddw_ref, ddb_ref = rest[len(ex_in):len(ex_in) + 3]
        xo = rest[len(ex_in) + 3:len(ex_in) + 3 + len(ex_out)]
        ue_ref, dcv_ref = rest[len(ex_in) + 3 + len(ex_out):len(ex_in) + 5 + len(ex_out)]
        i = pl.program_id(1)
        if exchange is not None:
            exchange.emit(pl.program_id(0) * n_t + i, (F2 // tc) * n_t, xi, xo, rest[len(ex_in) + 5 + len(ex_out):])
        ue_ref[0:FFN_HALO, :] = jnp.where(i > 0, up_ref[...].astype(F32), 0.0)
        ue_ref[FFN_HALO:FFN_HALO + ts, :] = u_ref[...].astype(F32)
        ue_ref[FFN_HALO + ts:FFN_HALO + ts + FFN_HALO, :] = jnp.where(i < n_t - 1, un_ref[...].astype(F32), 0.0)

        @pl.when(i == 0)
        def _():
            ddw_ref[...] = jnp.zeros_like(ddw_ref)
            ddb_ref[...] = jnp.zeros_like(ddb_ref)

        for q in range(tc // 256):
            sls = [slice(q * 256 + half * 128, q * 256 + half * 128 + 128) for half in range(2)]
            qs = slice(q * 128, (q + 1) * 128)
            wk = [[w_ref[k:k + 1, sl] for k in range(FFN_KERNEL)] for sl in sls]
            bb = [b_ref[:, sl] for sl in sls]
            acc = [[jnp.zeros((8, 128), F32) for _ in range(FFN_KERNEL)] for _ in range(2)]
            accb = [jnp.zeros((8, 128), F32) for _ in range(2)]
            for r0, rows in [(r, rc) for r in range(0, ts, rc)] + [(ts, FFN_HALO)]:
                xs = [[ue_ref[pl.ds(FFN_HALO + r0 - 2 + k, rows), sls[h]] for k in range(FFN_KERNEL)] for h in range(2)]
                gt, vl = [bb[h] + sum(wk[h][k] * xs[h][k] for k in range(FFN_KERNEL)) for h in range(2)]
                sg = _sigmoid(gt)
                if r0 < ts:
                    dae = da_ref[r0:r0 + rows, qs].astype(F32)
                else:
                    dae = jnp.where(i < n_t - 1, dan_ref[:, qs].astype(F32), 0.0)
                dcv = [dae * vl * (sg * (1.0 + gt * (1.0 - sg))), dae * (gt * sg)]
                for h in range(2):
                    dcv_ref[r0:r0 + rows, sls[h]] = dcv[h]
                    if r0 < ts:
                        for k in range(FFN_KERNEL):
                            acc[h][k] = acc[h][k] + fold8(dcv[h] * xs[h][k])
                        accb[h] = accb[h] + fold8(dcv[h])
            for r0 in range(0, ts, rc):
                for h in range(2):
                    dup = sum(wk[h][2 - j] * dcv_ref[pl.ds(r0 + j, rc), sls[h]] for j in range(FFN_KERNEL))
                    dup_ref[r0:r0 + rc, sls[h]] = dup.astype(dup_ref.dtype)
            for h in range(2):
                for k in range(FFN_KERNEL):
                    ddw_ref[k:k + 1, sls[h]] += jnp.sum(acc[h][k], axis=0, keepdims=True)
                ddb_ref[:, sls[h]] += jnp.sum(accb[h], axis=0, keepdims=True)

    return pl.pallas_call(
        body, name=name, grid=(F2 // tc, n_t),
        in_specs=[pl.BlockSpec((ts, tc), lambda j, i: (i, j)),
                  pl.BlockSpec((FFN_HALO, tc), lambda j, i: (jnp.maximum(i * hb - 1, 0), j)),
                  pl.BlockSpec((FFN_HALO, tc), lambda j, i: (jnp.minimum((i + 1) * hb, last_h), j)),
                  pl.BlockSpec((ts, tc // 2), lambda j, i: (i, j)),
                  pl.BlockSpec((FFN_HALO, tc // 2), lambda j, i: (jnp.minimum((i + 1) * hb, last_h), j)),
                  pl.BlockSpec((FFN_KERNEL, tc), lambda j, i: (0, j)),
                  pl.BlockSpec((1, tc), lambda j, i: (0, j))] + [_ANY] * len(ex_in),
        out_specs=[pl.BlockSpec((ts, tc), lambda j, i: (i, j)), pl.BlockSpec((FFN_HALO, tc), lambda j, i: (0, j)),
                   pl.BlockSpec((1, tc), lambda j, i: (0, j))] + [_ANY] * len(ex_out),
        out_shape=[jax.ShapeDtypeStruct((S, F2), MXU_DTYPE), jax.ShapeDtypeStruct((FFN_HALO, F2), F32),
                   jax.ShapeDtypeStruct((1, F2), F32)] + list(ex_out),
        scratch_shapes=[pltpu.VMEM((ts + 2 * FFN_HALO, tc), F32), pltpu.VMEM((E, tc), F32)] + list(ex_scr),
        compiler_params=_params("parallel" if exchange is None else "arbitrary", "arbitrary"),
    )(up, up, up, dact, dact, dw, dw_b, *ex_in)


def _slopes(n_heads_total):
    return np.asarray(2.0 ** (-ALIBI_MAX * (np.arange(n_heads_total, dtype=np.float32) + 1.0) / n_heads_total), np.float32)


def _qkv_proj(h, w_qkvT, *, grp, name):
    S, D = h.shape
    H = D // HEAD_DIM
    tm = _pick(S, 1024, 16)

    rows = _pick(tm, 256, 16)

    def body(a_ref, b_ref, o_ref, r_ref):
        j = pl.program_id(1)
        r_ref[...] = jnp.zeros_like(r_ref)

        def product(c):
            return lax.dot_general(a_ref[c * rows:(c + 1) * rows, :].astype(MXU_DTYPE), b_ref[...].astype(MXU_DTYPE),
                                   (_DIMS["nt"], ((), ())), preferred_element_type=F32)

        @pl.when(j < 2)
        def _():
            acc = product(0)
            for c in range(tm // rows):
                nxt = product(c + 1) if c + 1 < tm // rows else None
                rs = slice(c * rows, (c + 1) * rows)
                for hd in range(H):
                    hs = slice(hd * HEAD_DIM, (hd + 1) * HEAD_DIM)
                    xv = acc[:, hs]
                    r = lax.rsqrt(jnp.mean(xv * xv, axis=-1, keepdims=True) + EPS)
                    o_ref[rs, hs] = (xv * r).astype(o_ref.dtype)
                    r_ref[rs, hd:hd + 1] = r
                acc = nxt

        @pl.when(j == 2)
        def _():
            o_ref[...] = lax.dot_general(a_ref[...].astype(MXU_DTYPE), b_ref[...].astype(MXU_DTYPE), (_DIMS["nt"], ((), ())),
                                         preferred_element_type=F32).astype(o_ref.dtype)

    return pl.pallas_call(
        body, name=name, grid=(S // tm, 3),
        in_specs=[pl.BlockSpec((tm, D), lambda i, j: (i, 0)), pl.BlockSpec((D, D), lambda i, j: (grp * 3 + j, 0))],
        out_specs=[pl.BlockSpec((tm, D), lambda i, j: (i, j)), pl.BlockSpec((tm, HEAD_DIM), lambda i, j: (i, j))],
        out_shape=[jax.ShapeDtypeStruct((S, 3 * D), MXU_DTYPE), jax.ShapeDtypeStruct((S, 3 * HEAD_DIM), F32)],
        compiler_params=_params("parallel", "parallel"),
    )(h, w_qkvT)


def _band(b, dil):
    qi = lax.broadcasted_iota(jnp.int32, (BLOCK, 2 * BLOCK), 0)
    ki = lax.broadcasted_iota(jnp.int32, (BLOCK, 2 * BLOCK), 1)
    delta = qi + BLOCK - ki
    valid = (delta >= 0) & (delta <= BLOCK) & ((ki >= BLOCK) | (b > 0))
    return valid, (delta * dil).astype(F32)


def _attn_fwd(qkv, qg, kg, *, grp, name):
    S, W = qkv.shape
    D = W // 3
    H = D // HEAD_DIM
    dil = DILATED_GROUPS[grp][1]
    L = S // dil
    nb = L // BLOCK
    slopes = _slopes(3 * H)[grp * H:(grp + 1) * H]
    scale = HEAD_DIM ** -0.5

    def body(q_ref, kp_ref, kc_ref, vp_ref, vc_ref, qg_ref, kg_ref, o_ref, l_ref):
        b = pl.program_id(1)
        valid, dist = _band(b, dil)
        l_ref[...] = jnp.zeros_like(l_ref)
        ss = []
        for h in range(H):
            hs = slice(h * HEAD_DIM, (h + 1) * HEAD_DIM)
            qn = (q_ref[:, hs].astype(F32) * qg_ref[h:h + 1, :]).astype(MXU_DTYPE)
            kp = (kp_ref[:, hs].astype(F32) * kg_ref[h:h + 1, :]).astype(MXU_DTYPE)
            kc = (kc_ref[:, hs].astype(F32) * kg_ref[h:h + 1, :]).astype(MXU_DTYPE)
            ss.append(lax.dot_general(qn, jnp.concatenate([kp, kc], axis=0), (((1,), (1,)), ((), ())), preferred_element_type=F32))
        ps = []
        for h in range(H):
            s = jnp.where(valid, ss[h] * scale - float(slopes[h]) * dist, NEG)
            m = jnp.max(s, axis=-1, keepdims=True)
            p = jnp.exp(s - m)
            den = jnp.sum(p, axis=-1, keepdims=True)
            l_ref[:, h:h + 1] = m + jnp.log(den)
            ps.append((p.astype(MXU_DTYPE), den))
        for h in range(H):
            hs = slice(h * HEAD_DIM, (h + 1) * HEAD_DIM)
            pb, den = ps[h]
            v2 = jnp.concatenate([vp_ref[:, hs], vc_ref[:, hs]], axis=0).astype(MXU_DTYPE)
            o_ref[:, hs] = (jnp.dot(pb, v2, preferred_element_type=F32) / den).astype(o_ref.dtype)

    def cur(j):
        return lambda r, b: (r * nb + b, j)

    def prv(j):
        return lambda r, b: (r * nb + jnp.maximum(b - 1, 0), j)

    blk = (BLOCK, D)
    gain = pl.BlockSpec((H, HEAD_DIM), lambda r, b: (0, 0))
    return pl.pallas_call(
        body, name=name, grid=(dil, nb),
        in_specs=[pl.BlockSpec(blk, cur(0)), pl.BlockSpec(blk, prv(1)), pl.BlockSpec(blk, cur(1)),
                  pl.BlockSpec(blk, prv(2)), pl.BlockSpec(blk, cur(2)), gain, gain],
        out_specs=[pl.BlockSpec(blk, cur(0)), pl.BlockSpec((BLOCK, HEAD_DIM), cur(0))],
        out_shape=[jax.ShapeDtypeStruct((S, D), MXU_DTYPE), jax.ShapeDtypeStruct((S, HEAD_DIM), F32)],
        compiler_params=_params("parallel", "parallel"),
    )(qkv, qkv, qkv, qkv, qkv, qg, kg)


def _attn_merge(os_, ls_, dils, *, name):
    S, D = os_[0].shape
    H = D // HEAD_DIM
    G = len(dils)
    ts = _pick(S, SUB_TILE, 16 * max(dils))
    subs = [g for g in range(G) if dils[g] > 1]

    def body(*refs):
        o_refs, l_refs = refs[0:G], refs[G:2 * G]
        outb_ref = refs[2 * G]
        lt_refs = refs[2 * G + 1:3 * G + 1]
        scratch = refs[3 * G + 1:]
        lt_tok = scratch[0]
        o_tok = {g: scratch[1 + 2 * j] for j, g in enumerate(subs)}
        l_tok = {g: scratch[2 + 2 * j] for j, g in enumerate(subs)}
        for g in subs:
            _sub_to_tok(o_refs[g], o_tok[g], dils[g])
            _sub_to_tok(l_refs[g], l_tok[g], dils[g])
        lt_tok[0] = jnp.zeros((ts, HEAD_DIM), F32)
        for h in range(H):
            hs = slice(h * HEAD_DIM, (h + 1) * HEAD_DIM)
            ls = [l_tok[g][0][:, h:h + 1] if g in subs else l_refs[g][:, h:h + 1] for g in range(G)]
            ov = [o_tok[g][h] if g in subs else o_refs[g][:, hs].astype(F32) for g in range(G)]
            m = functools.reduce(jnp.maximum, ls)
            es = [jnp.exp(l - m) for l in ls]
            den = functools.reduce(lambda a, b: a + b, es)
            out = functools.reduce(lambda a, b: a + b, [e * o for e, o in zip(es, ov)]) / den
            outb_ref[:, hs] = out.astype(outb_ref.dtype)
            lt_tok.at[0][:, h:h + 1] = m + jnp.log(den)
        for g in range(G):
            if g in subs:
                _tok_to_sub(lt_tok, lt_refs[g], dils[g])
            else:
                lt_refs[g][...] = lt_tok[0]

    def spec(g, cols):
        return _sub_spec(dils[g], ts, cols) if g in subs else pl.BlockSpec((ts, cols), lambda i: (i, 0))

    def shape(g, cols, dtype):
        return jax.ShapeDtypeStruct((dils[g], S // dils[g], cols) if g in subs else (S, cols), dtype)

    def view(a, g):
        return a.reshape(dils[g], S // dils[g], a.shape[-1]) if g in subs else a

    outs = pl.pallas_call(
        body, name=name, grid=(S // ts,),
        in_specs=[spec(g, D) for g in range(G)] + [spec(g, HEAD_DIM) for g in range(G)],
        out_specs=[pl.BlockSpec((ts, D), lambda i: (i, 0))] + [spec(g, HEAD_DIM) for g in range(G)],
        out_shape=[jax.ShapeDtypeStruct((S, D), MXU_DTYPE)] + [shape(g, HEAD_DIM, F32) for g in range(G)],
        scratch_shapes=[pltpu.VMEM((1, ts, HEAD_DIM), F32)] + [pltpu.VMEM((H, ts, HEAD_DIM), F32), pltpu.VMEM((1, ts, HEAD_DIM), F32)] * len(subs),
        compiler_params=_params("parallel"),
    )(*[view(o, g) for g, o in enumerate(os_)], *[view(l, g) for g, l in enumerate(ls_)])
    return outs[0], [t.reshape(S, HEAD_DIM) for t in outs[1:]]


def _attn_delta(do, out, dils, *, name):
    S, D = out.shape
    H = D // HEAD_DIM
    G = len(dils)
    ts = _pick(S, SUB_TILE, 16 * max(dils))
    subs = [g for g in range(G) if dils[g] > 1]

    def body(do_ref, o_ref, *rest):
        d_refs = rest[0:G]
        dos_refs = rest[G:G + len(subs)]
        d_tok, do_tok = rest[G + len(subs):]
        d_tok[0] = jnp.zeros((ts, HEAD_DIM), F32)
        for h in range(H):
            hs = slice(h * HEAD_DIM, (h + 1) * HEAD_DIM)
            dov = do_ref[:, hs].astype(F32)
            do_tok[h] = dov
            d_tok.at[0][:, h:h + 1] = jnp.sum(dov * o_ref[:, hs].astype(F32), axis=-1, keepdims=True)
        for g in range(G):
            if g in subs:
                _tok_to_sub(d_tok, d_refs[g], dils[g])
            else:
                d_refs[g][...] = d_tok[0]
        for g, dst in zip(subs, dos_refs):
            _tok_to_sub(do_tok, dst, dils[g])

    def spec(g, cols):
        return _sub_spec(dils[g], ts, cols) if g in subs else pl.BlockSpec((ts, cols), lambda i: (i, 0))

    def shape(g, cols, dtype):
        return jax.ShapeDtypeStruct((dils[g], S // dils[g], cols) if g in subs else (S, cols), dtype)

    row = pl.BlockSpec((ts, D), lambda i: (i, 0))
    outs = pl.pallas_call(
        body, name=name, grid=(S // ts,), in_specs=[row, row],
        out_specs=[spec(g, HEAD_DIM) for g in range(G)] + [spec(g, D) for g in subs],
        out_shape=[shape(g, HEAD_DIM, F32) for g in range(G)] + [shape(g, D, do.dtype) for g in subs],
        scratch_shapes=[pltpu.VMEM((1, ts, HEAD_DIM), F32), pltpu.VMEM((H, ts, HEAD_DIM), F32)],
        compiler_params=_params("parallel"),
    )(do, out)
    deltas = [t.reshape(S, HEAD_DIM) for t in outs[0:G]]
    dos = {g: t.reshape(S, D) for g, t in zip(subs, outs[G:])}
    return deltas, [dos[g] if g in subs else do for g in range(G)]


def _attn_bwd(qkv, rqk, do, lse, delta, qg, kg, *, grp, name):
    S, W = qkv.shape
    D = W // 3
    H = D // HEAD_DIM
    dil = DILATED_GROUPS[grp][1]
    L = S // dil
    nb = L // BLOCK
    slopes = _slopes(3 * H)[grp * H:(grp + 1) * H]
    scale = HEAD_DIM ** -0.5

    def body(q_ref, qp_ref, kp_ref, kc_ref, vp_ref, vc_ref, do_ref, l_ref, dl_ref, rq_ref, rk_ref, qg_ref, kg_ref,
             out_ref, dqg_ref, dkg_ref, cq_ref, ck_ref, cv_ref, nq_ref, nk_ref, nv_ref, pk_ref, pv_ref):
        r = pl.program_id(0)
        b = pl.program_id(1)

        @pl.when(jnp.logical_and(r == 0, b == 0))
        def _():
            dqg_ref[...] = jnp.zeros_like(dqg_ref)
            dkg_ref[...] = jnp.zeros_like(dkg_ref)

        @pl.when(b < nb)
        def _():
            valid, dist = _band(b, dil)

            def operands(h):
                hs = slice(h * HEAD_DIM, (h + 1) * HEAD_DIM)
                qn = (q_ref[:, hs].astype(F32) * qg_ref[h:h + 1, :]).astype(MXU_DTYPE)
                kp = (kp_ref[:, hs].astype(F32) * kg_ref[h:h + 1, :]).astype(MXU_DTYPE)
                kc = (kc_ref[:, hs].astype(F32) * kg_ref[h:h + 1, :]).astype(MXU_DTYPE)
                k2 = jnp.concatenate([kp, kc], axis=0)
                v2 = jnp.concatenate([vp_ref[:, hs], vc_ref[:, hs]], axis=0).astype(MXU_DTYPE)
                return hs, qn, k2, v2, do_ref[:, hs].astype(MXU_DTYPE)

            sdp = []
            for h in range(H):
                hs, qn, k2, v2, doh = operands(h)
                s = lax.dot_general(qn, k2, (((1,), (1,)), ((), ())), preferred_element_type=F32)
                dp = lax.dot_general(doh, v2, (((1,), (1,)), ((), ())), preferred_element_type=F32)
                sdp.append((s, dp))
            pds = []
            for h in range(H):
                s, dp = sdp[h]
                s = jnp.where(valid, s * scale - float(slopes[h]) * dist, NEG)
                p = jnp.exp(s - l_ref[:, h:h + 1])
                pds.append((p.astype(MXU_DTYPE), (p * (dp - dl_ref[:, h:h + 1]) * scale).astype(MXU_DTYPE)))
            for h in range(H):
                hs, qn, k2, v2, doh = operands(h)
                pb, dsc = pds[h]
                nq_ref[:, hs] = jnp.dot(dsc, k2, preferred_element_type=F32)
                dk2 = lax.dot_general(dsc, qn, (((0,), (0,)), ((), ())), preferred_element_type=F32)
                dv2 = lax.dot_general(pb, doh, (((0,), (0,)), ((), ())), preferred_element_type=F32)
                pk_ref[:, hs] = dk2[0:BLOCK]
                nk_ref[:, hs] = dk2[BLOCK:2 * BLOCK]
                pv_ref[:, hs] = dv2[0:BLOCK]
                nv_ref[:, hs] = dv2[BLOCK:2 * BLOCK]

        @pl.when(b == nb)
        def _():
            pk_ref[...] = jnp.zeros_like(pk_ref)
            pv_ref[...] = jnp.zeros_like(pv_ref)

        @pl.when(b > 0)
        def _():
            for h in range(H):
                hs = slice(h * HEAD_DIM, (h + 1) * HEAD_DIM)
                for j, (xh_ref, r_ref, gain_ref, dgain_ref) in enumerate(((qp_ref, rq_ref, qg_ref, dqg_ref),
                                                                          (kp_ref, rk_ref, kg_ref, dkg_ref))):
                    dy = cq_ref[:, hs] if j == 0 else ck_ref[:, hs] + pk_ref[:, hs]
                    gain = gain_ref[h:h + 1, :]
                    xh = xh_ref[:, hs].astype(F32)
                    rr = r_ref[:, h:h + 1]
                    gy = dy * gain
                    dx = rr * (gy - xh * jnp.mean(gy * xh, axis=-1, keepdims=True))
                    out_ref[:, j * D + h * HEAD_DIM:j * D + (h + 1) * HEAD_DIM] = dx.astype(out_ref.dtype)
                    dgain_ref[h:h + 1, :] += jnp.sum(dy * xh, axis=0, keepdims=True)
                out_ref[:, 2 * D + h * HEAD_DIM:2 * D + (h + 1) * HEAD_DIM] = (cv_ref[:, hs] + pv_ref[:, hs]).astype(out_ref.dtype)

        @pl.when(b < nb)
        def _():
            cq_ref[...] = nq_ref[...]
            ck_ref[...] = nk_ref[...]
            cv_ref[...] = nv_ref[...]

    def cur(j):
        return lambda r, b: (r * nb + jnp.minimum(b, nb - 1), j)

    def prv(j):
        return lambda r, b: (r * nb + jnp.clip(b - 1, 0, nb - 1), j)

    blk = (BLOCK, D)
    lblk = pl.BlockSpec((BLOCK, HEAD_DIM), cur(0))
    gain = pl.BlockSpec((H, HEAD_DIM), lambda r, b: (0, 0))
    return pl.pallas_call(
        body, name=name, grid=(dil, nb + 1),
        in_specs=[pl.BlockSpec(blk, cur(0)), pl.BlockSpec(blk, prv(0)), pl.BlockSpec(blk, prv(1)), pl.BlockSpec(blk, cur(1)),
                  pl.BlockSpec(blk, prv(2)), pl.BlockSpec(blk, cur(2)), pl.BlockSpec(blk, cur(0)), lblk, lblk,
                  pl.BlockSpec((BLOCK, HEAD_DIM), prv(0)), pl.BlockSpec((BLOCK, HEAD_DIM), prv(1)), gain, gain],
        out_specs=[pl.BlockSpec((BLOCK, 3 * D), lambda r, b: (r * nb + jnp.maximum(b - 1, 0), 0)), gain, gain],
        out_shape=[jax.ShapeDtypeStruct((S, 3 * D), MXU_DTYPE), jax.ShapeDtypeStruct((H, HEAD_DIM), F32),
                   jax.ShapeDtypeStruct((H, HEAD_DIM), F32)],
        scratch_shapes=[pltpu.VMEM(blk, F32)] * 8,
        compiler_params=_params("arbitrary", "arbitrary"),
    )(qkv, qkv, qkv, qkv, qkv, qkv, do, lse, delta, rqk, rqk, qg, kg)


def _adamw(w, m, v, terms, slots, *, name):
    R, C = w.shape
    nt = len(terms)
    tr = _pick(R, 256, 16)
    c1 = 1.0 - ADAM_B1 ** ADAM_STEP
    c2 = 1.0 - ADAM_B2 ** ADAM_STEP

    def body(slot_ref, w_ref, m_ref, v_ref, *rest):
        t_refs = rest[:nt]
        g_ref, d_ref, nm_ref, nv_ref = rest[nt:]
        g = t_refs[0][...].astype(F32)
        for t in t_refs[1:]:
            g = g + t[...].astype(F32)
        mm = ADAM_B1 * m_ref[...] + (1.0 - ADAM_B1) * g
        vv = ADAM_B2 * v_ref[...] + (1.0 - ADAM_B2) * (g * g)
        m_hat = mm / c1
        v_hat = vv / c2
        g_ref[...] = g
        d_ref[...] = -ADAM_LR * (m_hat / (jnp.sqrt(v_hat) + ADAM_EPS) + ADAM_WD * w_ref[...])
        nm_ref[...] = mm
        nv_ref[...] = vv

    row = pl.BlockSpec((tr, C), lambda i, s: (i, 0))
    grid_spec = pltpu.PrefetchScalarGridSpec(
        num_scalar_prefetch=1, grid=(R // tr,),
        in_specs=[row, row, row] + [pl.BlockSpec((None, tr, C), lambda i, s, t=t: (s[t], i, 0)) for t in range(nt)],
        out_specs=[row] * 4)
    return pl.pallas_call(
        body, name=name, grid_spec=grid_spec, out_shape=[jax.ShapeDtypeStruct((R, C), F32)] * 4,
        compiler_params=_params("parallel"),
    )(slots, w, m, v, *terms)


def _chip_partials(g, sib, core, *, name):
    _, R, C = g.shape
    tr = _pick(R, 1200, 16)

    def body(core_ref, g_ref, s_ref, o_ref):
        o_ref[...] = (g_ref[...] + s_ref[...].astype(F32)).astype(o_ref.dtype)

    grid_spec = pltpu.PrefetchScalarGridSpec(
        num_scalar_prefetch=1, grid=(4, R // tr),
        in_specs=[pl.BlockSpec((None, tr, C), lambda k, i, c: (2 * k + c[0], i, 0)),
                  pl.BlockSpec((None, tr, C), lambda k, i, c: (k, i, 0))],
        out_specs=pl.BlockSpec((None, tr, C), lambda k, i, c: (k, i, 0)))
    return pl.pallas_call(
        body, name=name, grid_spec=grid_spec, out_shape=jax.ShapeDtypeStruct((4, R, C), sib.dtype),
        compiler_params=_params("parallel", "parallel"),
    )(core, g, sib)


_ANY = pl.BlockSpec(memory_space=pl.ANY)


def _place():
    return lax.axis_index("x"), lax.axis_index("y"), lax.axis_index("c")


class _Exchange:
    def __init__(self, operands, out_shapes, scratch, emit):
        self.operands, self.out_shapes, self.scratch, self.emit = operands, out_shapes, scratch, emit


def _run_exchange(ex, *, name):
    n_in, n_out = len(ex.operands), len(ex.out_shapes)

    def body(*refs):
        ex.emit(0, 1, refs[:n_in], refs[n_in:n_in + n_out], refs[n_in + n_out:])

    return pl.pallas_call(body, name=name, in_specs=[_ANY] * n_in, out_specs=[_ANY] * n_out, out_shape=ex.out_shapes,
                          scratch_shapes=ex.scratch)(*ex.operands)


def _gather_exchange(shard):
    R, C = shard.shape

    def emit(step, n, ins, outs, sems):
        x_ref, out_ref = ins[0], outs[0]
        send_sems, recv_sems, local_sem = sems
        x, y, c = _place()
        me, sibling = (x, y, c), (x, y, 1 - c)
        chips = [(1 - x, y), (x, 1 - y), (1 - x, 1 - y)]

        def slot(px, py, pc):
            return out_ref.at[4 * px + 2 * py + pc]

        def copy(k, block, to, src=None):
            return pltpu.make_async_remote_copy(
                src_ref=slot(*block) if src is None else src, dst_ref=slot(*block),
                send_sem=send_sems.at[k], recv_sem=recv_sems.at[k], device_id=to, device_id_type=MESH)

        mine = pltpu.make_async_copy(x_ref, slot(*me), local_sem)
        first = [copy(0, me, sibling, src=x_ref)] + [copy(1 + j, me, (*chip, c), src=x_ref) for j, chip in enumerate(chips)]
        passed = [copy(4 + j, (*chip, c), sibling) for j, chip in enumerate(chips)]

        @pl.when(step == 0)
        def _():
            mine.start()
            for cp in first:
                cp.start()

        for j, chip in enumerate(chips):
            @pl.when(step == max(n - 2 * (len(chips) - j), 0))
            def _(j=j, chip=chip):
                copy(1 + j, (*chip, c), me).wait_recv()
                passed[j].start()

        @pl.when(step == n - 1)
        def _():
            copy(0, sibling, me).wait_recv()
            for j, chip in enumerate(chips):
                copy(4 + j, (*chip, 1 - c), me).wait_recv()
            for cp in first + passed:
                cp.wait_send()
            mine.wait()

    return _Exchange([shard], [jax.ShapeDtypeStruct((N_DEV, R, C), shard.dtype)],
                     [pltpu.SemaphoreType.DMA((7,)), pltpu.SemaphoreType.DMA((7,)), pltpu.SemaphoreType.DMA], emit)


def _all_gather(shard, *, name):
    return _run_exchange(_gather_exchange(shard), name=name)[0]


def _rs_sibling(g, *, name):
    _, R, C = g.shape

    def body(g_ref, sib_ref, send_sems, recv_sems):
        x, y, c = _place()
        sends = [pltpu.make_async_remote_copy(
            src_ref=g_ref.at[2 * k + (1 - c)], dst_ref=sib_ref.at[k], send_sem=send_sems.at[k], recv_sem=recv_sems.at[k],
            device_id=(x, y, 1 - c), device_id_type=MESH) for k in range(4)]
        for cp in sends:
            cp.start()
        for cp in sends:
            cp.wait_recv()
        for cp in sends:
            cp.wait_send()

    return pl.pallas_call(
        body, name=name, in_specs=[_ANY], out_specs=_ANY, out_shape=jax.ShapeDtypeStruct((4, R, C), g.dtype),
        scratch_shapes=[pltpu.SemaphoreType.DMA((4,)), pltpu.SemaphoreType.DMA((4,))],
    )(g)


def _chips_exchange(part):
    _, R, C = part.shape

    def emit(step, n, ins, outs, sems):
        p_ref, out_ref = ins[0], outs[0]
        send_sems, recv_sems = sems
        x, y, c = _place()
        chips = [(1 - x, y), (x, 1 - y), (1 - x, 1 - y)]
        sends = [pltpu.make_async_remote_copy(
            src_ref=p_ref.at[2 * px + py], dst_ref=out_ref.at[j], send_sem=send_sems.at[j], recv_sem=recv_sems.at[j],
            device_id=(px, py, c), device_id_type=MESH) for j, (px, py) in enumerate(chips)]

        @pl.when(step == 0)
        def _():
            for cp in sends:
                cp.start()

        @pl.when(step == n - 1)
        def _():
            for cp in sends:
                cp.wait_recv()
            for cp in sends:
                cp.wait_send()

    return _Exchange([part], [jax.ShapeDtypeStruct((3, R, C), part.dtype)],
                     [pltpu.SemaphoreType.DMA((3,)), pltpu.SemaphoreType.DMA((3,))], emit)


def _rs_chips(part, *, name):
    return _run_exchange(_chips_exchange(part), name=name)[0]


def _interleave_rows(wt):
    F2, D = wt.shape
    return wt.reshape(2, F2 // 256, 128, D).transpose(1, 0, 2, 3).reshape(F2, D)


def _deinterleave_rows(wt):
    F2, D = wt.shape
    return wt.reshape(F2 // 256, 2, 128, D).transpose(1, 0, 2, 3).reshape(F2, D)


def _interleave_cols(v):
    k, F2 = v.shape
    return v.reshape(k, 2, F2 // 256, 128).transpose(0, 2, 1, 3).reshape(k, F2)


def _deinterleave_cols(v):
    k, F2 = v.shape
    return v.reshape(k, F2 // 256, 2, 128).transpose(0, 2, 1, 3).reshape(k, F2)


def _pack_rows(parts):
    return jnp.concatenate(parts, axis=0)


def _flat_pack(parts, width):
    flat = jnp.concatenate([p.reshape(-1) for p in parts])
    pad = (-flat.shape[0]) % (8 * width)
    return jnp.pad(flat, (0, pad)).reshape(-1, width)


def _flat_unpack(packed, shapes):
    flat = packed.reshape(-1)
    out, off = [], 0
    for shp in shapes:
        n = int(np.prod(shp))
        out.append(flat[off:off + n].reshape(shp))
        off += n
    return out


def _ffn_forward(x, hf, wupT, wdown, dw_i, dwb_i, tag, loss_target=None):
    up, act = _ffn_up_act(hf, wupT, dw_i, dwb_i, name=f"ffn{tag}_up")
    if loss_target is None:
        y = _mm(act, wdown, mode="nn", out_dtype=F32, name=f"ffn{tag}_down", residual=x)
    else:
        y = _mm(act, wdown, mode="nn", out_dtype=F32, name=f"ffn{tag}_down", residual=x, tm=512, post=_post_loss(loss_target),
                keep_main=False)
    return y, (hf, up, act)


def _ffn_backward(x, g_ffn, wupT, wdown, dw_i, dwb_i, saved, dy, dyb, tag, exchange=None):
    hf, up, act = saved
    dact = _mm(dyb, wdown, mode="nt", out_dtype=MXU_DTYPE, name=f"ffn{tag}_dact", tm=1024, tn=1408)
    d_wdown = _mm(act, dyb, mode="tn", out_dtype=F32, name=f"ffn{tag}_dwdown", tm=1408, tk=2048)
    dup, d_dw_i, d_dwb_i, *carried = _ffn_act_bwd(up, dact, dw_i, dwb_i, name=f"ffn{tag}_actbwd", exchange=exchange)
    dx, dxb, dg, cs = _mm_post(dup, wupT, _post_rms_bwd(x, g_ffn, dy), name=f"ffn{tag}_dhf")
    d_wupT = _mm(dup, hf, mode="tn", out_dtype=F32, name=f"ffn{tag}_dwup", tm=1408, tk=2048)
    return dx, dxb, cs, dict(w_upT=d_wupT, w_down=d_wdown, dw=d_dw_i[0:FFN_KERNEL], dw_b=d_dwb_i, norm=dg), carried


def _local_step(x, target, p, late_weights=None, early_reduce=None):
    S, D = x.shape
    H = D // HEAD_DIM
    h0 = _rms_fwd(x, p["norm_mix"][0:1], name="l0_rms")
    u = _mm(h0, p["w_inT"], mode="nt", out_dtype=F32, name="l0_in", bias=p["cm_b_in"])
    c, s, *carried = _cm_fwd(u, p["cm_dw"], p["cm_dw_b"], p["cm_ln_g"], p["cm_ln_b"], name="l0_conv",
                             exchange=None if late_weights is None else late_weights[0])
    if late_weights is not None:
        p = {**p, **late_weights[1](carried)}
    x1, hf0 = _mm(s, p["w_out"], mode="nn", out_dtype=F32, name="l0_out", bias=p["cm_b_out"], residual=x,
                  post=_post_rms(p["norm_ffn"][0:1]))
    x2, sv0 = _ffn_forward(x1, hf0, p["w_upT"][0], p["w_down"][0], p["ff_dw"][0], p["ff_dw_b"][0:1], 0)
    dils = [dil for _, dil in DILATED_GROUPS]
    assert dils[0] == 1
    h1s = [t.reshape(S, D) for t in _rms_fwd(x2, p["norm_mix"][1:2], name="l1_rms", subs=tuple(dils[1:]))]
    qkvs, rqks, os_, ls_ = [], [], [], []
    for g in range(len(dils)):
        qkv_g, r_g = _qkv_proj(h1s[g], p["w_qkvT"], grp=g, name=f"l1_qkv{g}")
        qkvs.append(qkv_g)
        rqks.append(r_g)
        o, l = _attn_fwd(qkvs[g], p["at_q_norm"][g * H:(g + 1) * H], p["at_k_norm"][g * H:(g + 1) * H], grp=g, name=f"l1_attn{g}")
        os_.append(o)
        ls_.append(l)
    outb, lses = _attn_merge(os_, ls_, dils, name="l1_merge")
    x3, hf1 = _mm(outb, p["w_o"], mode="nn", out_dtype=F32, name="l1_o", residual=x2, post=_post_rms(p["norm_ffn"][1:2]))
    (dx4, dx4b, loss), sv1 = _ffn_forward(x3, hf1, p["w_upT"][1], p["w_down"][1], p["ff_dw"][1], p["ff_dw_b"][1:2], 1,
                                          loss_target=target)
    dx3, dx3b, _, gf1, _ = _ffn_backward(x3, p["norm_ffn"][1:2], p["w_upT"][1], p["w_down"][1], p["ff_dw"][1], p["ff_dw_b"][1:2],
                                      sv1, dx4, dx4b, 1)
    do = _mm(dx3b, p["w_o"], mode="nt", out_dtype=MXU_DTYPE, name="l1_do")
    d_wo = _mm(outb, dx3b, mode="tn", out_dtype=F32, name="l1_dwo", tk=2048)
    deltas, dos = _attn_delta(do, outb, dils, name="l1_delta")
    dh1s, d_wqkvT, dqg, dkg = [], [], [], []
    for g, dil in enumerate(dils):
        dqkv_g, a, b_ = _attn_bwd(qkvs[g], rqks[g], dos[g], lses[g], deltas[g], p["at_q_norm"][g * H:(g + 1) * H],
                                  p["at_k_norm"][g * H:(g + 1) * H], grp=g, name=f"l1_attnbwd{g}")
        dqg.append(a)
        dkg.append(b_)
        d_wqkvT.append(_mm(dqkv_g, h1s[g], mode="tn", out_dtype=F32, name=f"l1_dwqkv{g}", tk=2048))
        dh1s.append(_mm(dqkv_g, p["w_qkvT"], mode="nn", out_dtype=F32, name=f"l1_dh{g}", b_off=g * 3 * D, b_len=3 * D))
    dx2, dx2b, dgm1, _ = _rms_bwd(x2, p["norm_mix"][1:2], dh1s[0:1], dx3, name="l1_rmsbwd",
                                  dh_subs=[(dh1s[g].reshape(dils[g], S // dils[g], D), dils[g]) for g in range(1, len(dils))])
    ex, finish = (None, None) if early_reduce is None else early_reduce(
        dict(w_qkvT=jnp.concatenate(d_wqkvT, axis=0), w_o=d_wo, w_upT=gf1["w_upT"], w_down=gf1["w_down"]))
    dx1, dx1b, cs1, gf0, carried = _ffn_backward(x1, p["norm_ffn"][0:1], p["w_upT"][0], p["w_down"][0], p["ff_dw"][0],
                                                 p["ff_dw_b"][0:1], sv0, dx2, dx2b, 0, exchange=ex)
    reduced = [] if finish is None else [finish(carried)]
    ex, finish = (None, None) if early_reduce is None else early_reduce(dict(w_upT=gf0["w_upT"], w_down=gf0["w_down"]))
    ds = _mm(dx1b, p["w_out"], mode="nt", out_dtype=F32, name="l0_ds")
    d_wout = _mm(s, dx1b, mode="tn", out_dtype=F32, name="l0_dwout", tk=2048)
    dc, d_lng, d_lnb = _cm_ln_bwd(c, ds, p["cm_ln_g"], p["cm_ln_b"], name="l0_lnbwd")
    du, d_cmdw, d_cmdwb, d_bin, *carried = _cm_conv_bwd(dc, u, p["cm_dw"], name="l0_convbwd", exchange=ex)
    if finish is not None:
        reduced.append(finish(carried))
    grad_x, _, dgm0, _ = _mm_post(du, p["w_inT"], _post_rms_bwd(x, p["norm_mix"][0:1], dx1), name="l0_dh")
    d_winT = _mm(du, h0, mode="tn", out_dtype=F32, name="l0_dwin", tk=2048)
    grads = dict(
        norm_mix=jnp.concatenate([dgm0, dgm1], axis=0),
        norm_ffn=jnp.concatenate([gf0["norm"], gf1["norm"]], axis=0),
        w_inT=d_winT, cm_b_in=d_bin, cm_dw=d_cmdw[0:CONV_KERNEL], cm_dw_b=d_cmdwb, cm_ln_g=d_lng, cm_ln_b=d_lnb,
        w_out=d_wout, cm_b_out=cs1,
        w_qkvT=jnp.concatenate(d_wqkvT, axis=0), at_q_norm=jnp.concatenate(dqg, axis=0), at_k_norm=jnp.concatenate(dkg, axis=0),
        w_o=d_wo,
        w_upT=[gf0["w_upT"], gf1["w_upT"]], w_down=[gf0["w_down"], gf1["w_down"]],
        ff_dw=jnp.stack([gf0["dw"], gf1["dw"]]), ff_dw_b=jnp.concatenate([gf0["dw_b"], gf1["dw_b"]], axis=0),
    )
    return loss, grad_x, grads, reduced


_BIG = ("cm_w_in", "cm_w_out", "at_w_qkv", "at_w_out", "ff_w_up", "ff_w_down")
_TRANSPOSED = ("cm_w_in", "at_w_qkv", "ff_w_up")
_SMALL = ("norm_mix", "norm_ffn", "cm_b_in", "cm_dw_b", "cm_ln_g", "cm_ln_b", "cm_b_out", "at_q_norm", "at_k_norm",
          "ff_dw_b", "cm_dw", "ff_dw")
_SMALL_SHARDED = ("cm_dw", "ff_dw")
_ORDER = ("norm_mix", "norm_ffn", "cm_w_in", "cm_b_in", "cm_dw", "cm_dw_b", "cm_ln_g", "cm_ln_b", "cm_w_out", "cm_b_out",
          "at_w_qkv", "at_q_norm", "at_k_norm", "at_w_out", "ff_w_up", "ff_dw", "ff_dw_b", "ff_w_down")


_UNITS = (("cm_w_in", 0), ("cm_w_out", 0), ("ff_w_up", 0), ("ff_w_down", 0),
          ("at_w_qkv", 0), ("at_w_out", 0), ("ff_w_up", 1), ("ff_w_down", 1))
_N_FIRST = 2
_N_LAYER0 = 4


def _unit_rows(t, n, l):
    return t[n].shape[2] if n in _TRANSPOSED else t[n].shape[1]


def _big_rows(t, units=_UNITS):
    return _pack_rows([t[n][l].T if n in _TRANSPOSED else t[n][l] for n, l in units])


def _big_unrows(packed, like):
    mats, off = {}, 0
    for n, l in _UNITS:
        rows = _unit_rows(like, n, l)
        m = packed[off:off + rows]
        off += rows
        mats[(n, l)] = m.T if n in _TRANSPOSED else m
    return {n: jnp.stack([mats[(n, l)] for l in range(like[n].shape[0])]) for n in _BIG}


def kernel(x, norm_mix, norm_ffn, cm_w_in, cm_b_in, cm_dw, cm_dw_b, cm_ln_g, cm_ln_b, cm_w_out, cm_b_out, at_w_qkv, at_q_norm, at_k_norm, at_w_out, ff_w_up, ff_dw, ff_dw_b, ff_w_down, loss_target, m_norm_mix, m_norm_ffn, m_cm_w_in, m_cm_b_in, m_cm_dw, m_cm_dw_b, m_cm_ln_g, m_cm_ln_b, m_cm_w_out, m_cm_b_out, m_at_w_qkv, m_at_q_norm, m_at_k_norm, m_at_w_out, m_ff_w_up, m_ff_dw, m_ff_dw_b, m_ff_w_down, v_norm_mix, v_norm_ffn, v_cm_w_in, v_cm_b_in, v_cm_dw, v_cm_dw_b, v_cm_ln_g, v_cm_ln_b, v_cm_w_out, v_cm_b_out, v_at_w_qkv, v_at_q_norm, v_at_k_norm, v_at_w_out, v_ff_w_up, v_ff_dw, v_ff_dw_b, v_ff_w_down):
    w = dict(norm_mix=norm_mix, norm_ffn=norm_ffn, cm_w_in=cm_w_in, cm_b_in=cm_b_in, cm_dw=cm_dw, cm_dw_b=cm_dw_b, cm_ln_g=cm_ln_g,
             cm_ln_b=cm_ln_b, cm_w_out=cm_w_out, cm_b_out=cm_b_out, at_w_qkv=at_w_qkv, at_q_norm=at_q_norm, at_k_norm=at_k_norm,
             at_w_out=at_w_out, ff_w_up=ff_w_up, ff_dw=ff_dw, ff_dw_b=ff_dw_b, ff_w_down=ff_w_down)
    m = dict(norm_mix=m_norm_mix, norm_ffn=m_norm_ffn, cm_w_in=m_cm_w_in, cm_b_in=m_cm_b_in, cm_dw=m_cm_dw, cm_dw_b=m_cm_dw_b,
             cm_ln_g=m_cm_ln_g, cm_ln_b=m_cm_ln_b, cm_w_out=m_cm_w_out, cm_b_out=m_cm_b_out, at_w_qkv=m_at_w_qkv,
             at_q_norm=m_at_q_norm, at_k_norm=m_at_k_norm, at_w_out=m_at_w_out, ff_w_up=m_ff_w_up, ff_dw=m_ff_dw,
             ff_dw_b=m_ff_dw_b, ff_w_down=m_ff_w_down)
    v = dict(norm_mix=v_norm_mix, norm_ffn=v_norm_ffn, cm_w_in=v_cm_w_in, cm_b_in=v_cm_b_in, cm_dw=v_cm_dw, cm_dw_b=v_cm_dw_b,
             cm_ln_g=v_cm_ln_g, cm_ln_b=v_cm_ln_b, cm_w_out=v_cm_w_out, cm_b_out=v_cm_b_out, at_w_qkv=v_at_w_qkv,
             at_q_norm=v_at_q_norm, at_k_norm=v_at_k_norm, at_w_out=v_at_w_out, ff_w_up=v_ff_w_up, ff_dw=v_ff_dw,
             ff_dw_b=v_ff_dw_b, ff_w_down=v_ff_w_down)
    S, D = x.shape[1], x.shape[2]
    F2 = ff_dw_b.shape[1]
    H3 = at_q_norm.shape[1]
    me = 4 * lax.axis_index("x") + 2 * lax.axis_index("y") + lax.axis_index("c")

    ix, iy, ic = lax.axis_index("x"), lax.axis_index("y"), lax.axis_index("c")
    chip = 2 * ix + iy
    core = jnp.stack([ic]).astype(jnp.int32)
    w_rows = _big_rows(w)
    unit_rows = [_unit_rows(w, n, l) for n, l in _UNITS]
    n_first = sum(unit_rows[:_N_FIRST])
    n_layer0 = sum(unit_rows[:_N_LAYER0])
    w_wire = w_rows.astype(MXU_DTYPE)

    def unpack(gathered, units, rows):
        full, off = {}, 0
        for (n, l), r in zip(units, rows):
            full[(n, l)] = gathered[:, off:off + r, :].reshape(N_DEV * r, D)
            off += r
        out = {}
        if ("cm_w_in", 0) in full:
            out.update(w_inT=full[("cm_w_in", 0)], w_out=full[("cm_w_out", 0)])
        if ("at_w_qkv", 0) in full:
            out.update(w_qkvT=full[("at_w_qkv", 0)], w_o=full[("at_w_out", 0)],
                       w_upT=[_interleave_rows(full[("ff_w_up", l)]) for l in range(2)],
                       w_down=[full[("ff_w_down", l)] for l in range(2)])
        return out

    first = _all_gather(w_wire[:n_first], name="gather_first")
    late_weights = (_gather_exchange(w_wire[n_first:]),
                    lambda carried: unpack(carried[0], _UNITS[_N_FIRST:], unit_rows[_N_FIRST:]))
    small_sh = _flat_pack([cm_dw, ff_dw], D)
    small_g = _all_gather(small_sh, name="gather_small")
    cm_dw_full = jnp.concatenate([_flat_unpack(small_g[j], [cm_dw.shape, ff_dw.shape])[0][0] for j in range(N_DEV)], axis=-1)
    ff_dw_full = jnp.concatenate([_flat_unpack(small_g[j], [cm_dw.shape, ff_dw.shape])[1] for j in range(N_DEV)], axis=-1)

    p = dict(
        norm_mix=norm_mix, norm_ffn=norm_ffn, cm_b_in=cm_b_in, cm_dw=cm_dw_full, cm_dw_b=cm_dw_b, cm_ln_g=cm_ln_g, cm_ln_b=cm_ln_b,
        cm_b_out=cm_b_out, at_q_norm=at_q_norm[0], at_k_norm=at_k_norm[0],
        ff_dw=jnp.stack([_interleave_cols(ff_dw_full[l]) for l in range(ff_dw_full.shape[0])]),
        ff_dw_b=_interleave_cols(ff_dw_b),
        **unpack(first, _UNITS[:_N_FIRST], unit_rows[:_N_FIRST]),
    )

    def pack(pieces):
        return jnp.concatenate([t.reshape(N_DEV, t.shape[0] // N_DEV, D) for t in pieces], axis=1)

    def reduce_start(pieces, tag):
        g_rows = pack(pieces)
        sib = _rs_sibling(g_rows.astype(WIRE_DTYPE), name=f"reduce{tag}_sibling")
        return g_rows, sib, _chip_partials(g_rows, sib, core, name=f"reduce{tag}_add")

    def early_reduce(gd):
        ffn = [_deinterleave_rows(gd["w_upT"]), gd["w_down"]]
        tag, pieces = (2, [gd["w_qkvT"], gd["w_o"]] + ffn) if "w_qkvT" in gd else (1, ffn)
        g_rows, sib, part = reduce_start(pieces, tag)
        return _chips_exchange(part), lambda carried: (g_rows, sib, carried[0])

    loss8, grad_x, g, (reduced2, reduced1) = _local_step(x[0], loss_target[0], p, late_weights, early_reduce)
    loss = lax.psum(loss8[0, 0], ("x", "y", "c"))
    g_rows0, sib0, part0 = reduce_start([g["w_inT"], g["w_out"]], 0)
    reduced0 = (g_rows0, sib0, _rs_chips(part0, name="reduce0_chips"))
    slots = jnp.stack([me, chip, 0 * me, 0 * me + 1, 0 * me + 2]).astype(jnp.int32)
    m_rows, v_rows = _big_rows(m), _big_rows(v)
    updated = []
    for tag, (g_rows, sib, recv), rows in ((0, reduced0, slice(0, n_first)), (1, reduced1, slice(n_first, n_layer0)),
                                           (2, reduced2, slice(n_layer0, None))):
        updated.append(_adamw(w_rows[rows], m_rows[rows], v_rows[rows], [g_rows, sib, recv, recv, recv], slots, name=f"adamw_big{tag}"))
    big = [_big_unrows(jnp.concatenate([u[k] for u in updated], axis=0), w) for k in range(4)]

    g_small = dict(g)
    g_small["cm_b_in"] = g["cm_b_in"]
    g_small["at_q_norm"] = g["at_q_norm"][None]
    g_small["at_k_norm"] = g["at_k_norm"][None]
    g_small["ff_dw_b"] = _deinterleave_cols(g["ff_dw_b"])
    g_small["cm_dw"] = g["cm_dw"][None]
    g_small["ff_dw"] = jnp.stack([_deinterleave_cols(g["ff_dw"][l]) for l in range(g["ff_dw"].shape[0])])
    small_shapes = [g_small[n].shape for n in _SMALL]
    gs_parts = _all_gather(_flat_pack([g_small[n] for n in _SMALL], D), name="gather_small_grads")

    def embed(t, n):
        if n not in _SMALL_SHARDED:
            return t
        full_shape = t.shape[:-1] + (t.shape[-1] * N_DEV,)
        return lax.dynamic_update_slice_in_dim(jnp.zeros(full_shape, F32), t, me * t.shape[-1], axis=t.ndim - 1)

    packs = [_flat_pack([embed(tree[n], n) for n in _SMALL], D) for tree in (w, m, v)]
    gs, ds_, ms, vs = _adamw(packs[0], packs[1], packs[2], [gs_parts] * N_DEV, jnp.arange(N_DEV, dtype=jnp.int32),
                             name="adamw_small")
    small = []
    for t in (gs, ds_, ms, vs):
        un = dict(zip(_SMALL, _flat_unpack(t, small_shapes)))
        for n in _SMALL_SHARDED:
            width = w[n].shape[-1]
            un[n] = lax.dynamic_slice_in_dim(un[n], me * width, width, axis=un[n].ndim - 1)
        small.append({n: un[n].reshape(w[n].shape) for n in _SMALL})

    outs = [loss, grad_x[None]]
    for k in range(4):
        for n in _ORDER:
            outs.append(big[k][n] if n in _BIG else small[k][n])
    return tuple(outs)
```

```python
import functools

import jax
import jax.numpy as jnp
import numpy as np
from jax import lax
from jax.experimental import pallas as pl
from jax.experimental.pallas import tpu as pltpu

F32 = jnp.float32
MXU_DTYPE = jnp.bfloat16
WIRE_DTYPE = jnp.bfloat16
EPS = 1e-6
NEG = -1e30
HEAD_DIM = 128
BLOCK = 128
DILATED_GROUPS = ((128, 1), (512, 4), (2048, 16))
ALIBI_MAX = 8.0
CONV_KERNEL = 31
CONV_HALO = 32
CONV_ROWS = 64
FFN_KERNEL = 3
FFN_HALO = 16
FFN_ROWS = 64
ADAM_LR, ADAM_B1, ADAM_B2, ADAM_EPS, ADAM_WD, ADAM_STEP = 0.001, 0.9, 0.999, 1e-08, 0.01, 10
V7X_VMEM_BYTES = 64 * 1024 * 1024
VMEM_LIMIT = V7X_VMEM_BYTES * 3 // 4
N_DEV = 8
MESH = pl.DeviceIdType.MESH


def _pick(n, target, align):
    if n <= target:
        return n
    best = None
    for t in range(align, target + 1, align):
        if n % t == 0:
            best = t
    assert best is not None, (n, target, align)
    return best


def _params(*sem, vmem=VMEM_LIMIT):
    return pltpu.CompilerParams(dimension_semantics=sem, vmem_limit_bytes=vmem)


def _sigmoid(x):
    return 1.0 / (1.0 + jnp.exp(-x))


_DIMS = {"nn": ((1,), (0,)), "nt": ((1,), (1,)), "tn": ((0,), (0,))}


def _mm(a, b, *, mode, out_dtype, name, tm=1024, tn=1024, tk=None, bias=None, residual=None, b_off=0, b_len=None,
        post=None, keep_main=True):
    if mode == "tn":
        K, M = a.shape
    else:
        M, K = a.shape
    if mode == "nt":
        N = b.shape[0] if b_len is None else b_len
    else:
        N = b.shape[1]
    if b_len is not None:
        assert mode == "nt" or (mode == "nn" and K == b_len)
    tm = _pick(M, tm, 128 if mode == "tn" else 16)
    tn = _pick(N, tn, 128)
    tk = K if tk is None else _pick(K, tk, 128 if mode != "tn" else 16)
    nk = K // tk
    unit = tn if mode == "nt" else tk
    assert b_off % unit == 0
    kb0 = b_off // unit
    if mode == "tn":
        a_spec = pl.BlockSpec((tk, tm), lambda i, j, k: (k, i))
    else:
        a_spec = pl.BlockSpec((tm, tk), lambda i, j, k: (i, k))
    if mode == "nt":
        b_spec = pl.BlockSpec((tn, tk), lambda i, j, k: (j + kb0, k))
    else:
        b_spec = pl.BlockSpec((tk, tn), lambda i, j, k: (k + kb0, j))
    in_specs = [a_spec, b_spec]
    args = [a, b]
    if bias is not None:
        in_specs.append(pl.BlockSpec((1, tn), lambda i, j, k: (0, j)))
        args.append(bias)
    if residual is not None:
        in_specs.append(pl.BlockSpec((tm, tn), lambda i, j, k: (i, j)))
        args.append(residual)
    has_bias, has_res = bias is not None, residual is not None
    kinds = {"tile": ((tm, tn), lambda i, j, k: (i, j)), "row": ((1, tn), lambda i, j, k: (0, j)),
             "lanes": ((8, 128), lambda i, j, k: (0, 0))}
    post_in = [] if post is None else post.ins
    post_out = [] if post is None else post.outs
    if post is not None:
        assert tn == N
        for arr, kind in post_in:
            in_specs.append(pl.BlockSpec(*kinds[kind]))
            args.append(arr)
    out_specs = [pl.BlockSpec((tm, tn), lambda i, j, k: (i, j))] if keep_main else []
    out_shape = [jax.ShapeDtypeStruct((M, N), out_dtype)] if keep_main else []
    for kind, dtype in post_out:
        out_specs.append(pl.BlockSpec(*kinds[kind]))
        out_shape.append(jax.ShapeDtypeStruct({"tile": (M, N), "row": (1, N), "lanes": (8, 128)}[kind], dtype))
    accumulates = any(kind != "tile" for kind, _ in post_out)

    def body(*refs):
        a_ref, b_ref = refs[0], refs[1]
        pos = 2
        bias_ref = res_ref = None
        if has_bias:
            bias_ref = refs[pos]
            pos += 1
        if has_res:
            res_ref = refs[pos]
            pos += 1
        pin_refs = refs[pos:pos + len(post_in)]
        pos += len(post_in)
        o_ref = refs[pos] if keep_main else None
        pos += 1 if keep_main else 0
        pout_refs = refs[pos:pos + len(post_out)]
        pos += len(post_out)
        acc_ref = refs[pos] if nk > 1 else None

        def finish(acc):
            if has_bias:
                acc = acc + bias_ref[...]
            if has_res:
                acc = acc + res_ref[...]
            if keep_main:
                o_ref[...] = acc.astype(o_ref.dtype)
            if post is not None:
                post.fn(acc, pin_refs, pout_refs, pl.program_id(0) == 0)

        part = lax.dot_general(a_ref[...].astype(MXU_DTYPE), b_ref[...].astype(MXU_DTYPE), (_DIMS[mode], ((), ())),
                               preferred_element_type=F32)
        if nk == 1:
            finish(part)
        else:
            k = pl.program_id(2)

            @pl.when(k == 0)
            def _():
                acc_ref[...] = part

            @pl.when(jnp.logical_and(k > 0, k < nk - 1))
            def _():
                acc_ref[...] += part

            @pl.when(k == nk - 1)
            def _():
                finish(acc_ref[...] + part)

    outs = pl.pallas_call(
        body, name=name, grid=(M // tm, N // tn, nk), in_specs=in_specs, out_specs=out_specs, out_shape=out_shape,
        scratch_shapes=[pltpu.VMEM((tm, tn), F32)] if nk > 1 else [],
        compiler_params=_params("arbitrary" if accumulates else "parallel", "parallel", "arbitrary"),
    )(*args)
    return outs[0] if post is None else outs


def _mm_post(a, b, post, *, name, tm=512, rows=256):
    M, K = a.shape
    N = b.shape[1]
    tm = _pick(M, tm, 16)
    rows = _pick(tm, rows, 16)
    kinds = {"tile": ((tm, N), lambda i: (i, 0)), "row": ((1, N), lambda i: (0, 0)), "lanes": ((8, 128), lambda i: (0, 0))}
    n_in = len(post.ins)

    def body(a_ref, b_ref, *rest):
        first = pl.program_id(0) == 0

        def product(c):
            return jnp.dot(a_ref[c * rows:(c + 1) * rows, :].astype(MXU_DTYPE), b_ref[...].astype(MXU_DTYPE),
                           preferred_element_type=F32)

        def chunk(refs, specs, c):
            return [r.at[pl.ds(c * rows, rows), :] if kind == "tile" else r for r, kind in zip(refs, specs)]

        acc = product(0)
        for c in range(tm // rows):
            nxt = product(c + 1) if c + 1 < tm // rows else None
            post.fn(acc, chunk(rest[:n_in], [k for _, k in post.ins], c), chunk(rest[n_in:], [k for k, _ in post.outs], c),
                    jnp.logical_and(first, c == 0))
            acc = nxt

    return pl.pallas_call(
        body, name=name, grid=(M // tm,),
        in_specs=[pl.BlockSpec((tm, K), lambda i: (i, 0)), pl.BlockSpec((K, N), lambda i: (0, 0), pipeline_mode=pl.Buffered(1))]
        + [pl.BlockSpec(*kinds[kind]) for _, kind in post.ins],
        out_specs=[pl.BlockSpec(*kinds[kind]) for kind, _ in post.outs],
        out_shape=[jax.ShapeDtypeStruct({"tile": (M, N), "row": (1, N), "lanes": (8, 128)}[kind], dtype) for kind, dtype in post.outs],
        compiler_params=_params("arbitrary", vmem=V7X_VMEM_BYTES * 7 // 8),
    )(a, b, *[arr for arr, _ in post.ins])


class _Post:
    def __init__(self, ins, outs, fn):
        self.ins, self.outs, self.fn = ins, outs, fn


def _accumulate(ref, value, first):
    @pl.when(first)
    def _():
        ref[...] = value

    @pl.when(jnp.logical_not(first))
    def _():
        ref[...] += value


def _post_rms(g):
    def fn(acc, ins, outs, first):
        r = lax.rsqrt(jnp.mean(acc * acc, axis=-1, keepdims=True) + EPS)
        outs[0][...] = (acc * r * ins[0][...]).astype(outs[0].dtype)

    return _Post([(g, "row")], [("tile", MXU_DTYPE)], fn)


def _post_loss(target):
    def fn(acc, ins, outs, first):
        e = acc - ins[0][...]
        dy = e * (1.0 / acc.shape[-1])
        outs[0][...] = dy
        outs[1][...] = dy.astype(outs[1].dtype)
        part = jnp.sum(jnp.sum(e * e, axis=0, keepdims=True), axis=1, keepdims=True) * (0.5 / acc.shape[-1])
        _accumulate(outs[2], jnp.broadcast_to(part, outs[2].shape), first)

    return _Post([(target, "tile")], [("tile", F32), ("tile", MXU_DTYPE), ("lanes", F32)], fn)


def _post_rms_bwd(x, g, dres):
    def fn(acc, ins, outs, first):
        xv = ins[0][...]
        r = lax.rsqrt(jnp.mean(xv * xv, axis=-1, keepdims=True) + EPS)
        xh = xv * r
        gy = acc * ins[1][...]
        dx = r * (gy - xh * jnp.mean(gy * xh, axis=-1, keepdims=True)) + ins[2][...]
        outs[0][...] = dx
        outs[1][...] = dx.astype(outs[1].dtype)
        _accumulate(outs[2], jnp.sum(acc * xh, axis=0, keepdims=True), first)
        _accumulate(outs[3], jnp.sum(dx, axis=0, keepdims=True), first)

    return _Post([(x, "tile"), (g, "row"), (dres, "tile")], [("tile", F32), ("tile", MXU_DTYPE), ("row", F32), ("row", F32)], fn)


SUB_TILE = 512


def _sub_spec(dil, ts, cols):
    return pl.BlockSpec((dil, ts // dil, cols), lambda i: (0, i, 0))


def _tok_to_sub(tok_ref, dst_ref, dil):
    nc, ts, _ = tok_ref.shape
    for c in range(nc):
        for r in range(dil):
            dst_ref[r, :, c * 128:(c + 1) * 128] = tok_ref.at[c][pl.ds(r, ts // dil, stride=dil), :].astype(dst_ref.dtype)


def _sub_to_tok(src_ref, tok_ref, dil):
    nc, ts, _ = tok_ref.shape
    for c in range(nc):
        for r in range(dil):
            tok_ref.at[c][pl.ds(r, ts // dil, stride=dil), :] = src_ref[r, :, c * 128:(c + 1) * 128].astype(F32)


def _rms_fwd(x, g, *, name, subs=()):
    S, D = x.shape
    ts = _pick(S, SUB_TILE, 16 * max(subs, default=1))
    NC = D // 128

    def body(x_ref, g_ref, h_ref, *rest):
        xv = x_ref[...]
        r = lax.rsqrt(jnp.mean(xv * xv, axis=-1, keepdims=True) + EPS)
        h = xv * r * g_ref[...]
        h_ref[...] = h.astype(h_ref.dtype)
        if subs:
            tok_ref = rest[-1]
            for c in range(NC):
                tok_ref[c] = h[:, c * 128:(c + 1) * 128]
            for dil, dst_ref in zip(subs, rest):
                _tok_to_sub(tok_ref, dst_ref, dil)

    row = pl.BlockSpec((ts, D), lambda i: (i, 0))
    outs = pl.pallas_call(
        body, name=name, grid=(S // ts,),
        in_specs=[row, pl.BlockSpec((1, D), lambda i: (0, 0))],
        out_specs=[row] + [_sub_spec(dil, ts, D) for dil in subs],
        out_shape=[jax.ShapeDtypeStruct((S, D), MXU_DTYPE)] + [jax.ShapeDtypeStruct((dil, S // dil, D), MXU_DTYPE) for dil in subs],
        scratch_shapes=[pltpu.VMEM((NC, ts, 128), F32)] if subs else [],
        compiler_params=_params("parallel"),
    )(x, g)
    return outs if subs else outs[0]


def _rms_bwd(x, g, dhs, dres, *, name, dh_subs=()):
    S, D = x.shape
    ts = _pick(S, SUB_TILE, 16 * max([dil for _, dil in dh_subs], default=1))
    n_dh, n_sub = len(dhs), len(dh_subs)
    NC = D // 128

    def body(*refs):
        x_ref, g_ref = refs[0], refs[1]
        dh_refs = refs[2:2 + n_dh]
        sub_refs = refs[2 + n_dh:2 + n_dh + n_sub]
        dres_ref, dx_ref, dxb_ref, dg_ref, cs_ref = refs[2 + n_dh + n_sub:7 + n_dh + n_sub]
        i = pl.program_id(0)
        xv = x_ref[...]
        r = lax.rsqrt(jnp.mean(xv * xv, axis=-1, keepdims=True) + EPS)
        xh = xv * r
        dhv = dh_refs[0][...].astype(F32)
        for t in dh_refs[1:]:
            dhv = dhv + t[...].astype(F32)
        for (_, dil), sub_ref in zip(dh_subs, sub_refs):
            tok_ref = refs[-1]
            _sub_to_tok(sub_ref, tok_ref, dil)
            dhv = dhv + jnp.concatenate([tok_ref[c] for c in range(NC)], axis=1)
        gy = dhv * g_ref[...]
        dx = r * (gy - xh * jnp.mean(gy * xh, axis=-1, keepdims=True)) + dres_ref[...]
        dx_ref[...] = dx
        dxb_ref[...] = dx.astype(dxb_ref.dtype)
        dg = jnp.sum(dhv * xh, axis=0, keepdims=True)
        cs = jnp.sum(dx, axis=0, keepdims=True)

        @pl.when(i == 0)
        def _():
            dg_ref[...] = dg
            cs_ref[...] = cs

        @pl.when(i > 0)
        def _():
            dg_ref[...] += dg
            cs_ref[...] += cs

    row = pl.BlockSpec((ts, D), lambda i: (i, 0))
    vec = pl.BlockSpec((1, D), lambda i: (0, 0))
    return pl.pallas_call(
        body, name=name, grid=(S // ts,),
        in_specs=[row, vec] + [row] * n_dh + [_sub_spec(dil, ts, D) for _, dil in dh_subs] + [row],
        out_specs=[row, row, vec, vec],
        out_shape=[jax.ShapeDtypeStruct((S, D), F32), jax.ShapeDtypeStruct((S, D), MXU_DTYPE),
                   jax.ShapeDtypeStruct((1, D), F32), jax.ShapeDtypeStruct((1, D), F32)],
        scratch_shapes=[pltpu.VMEM((NC, ts, 128), F32)] if n_sub else [],
        compiler_params=_params("arbitrary"),
    )(x, g, *dhs, *[a for a, _ in dh_subs], dres)


def _conv_phases(ph_ref, ts):
    n = ts + CONV_HALO - 8
    for b in range(1, 8):
        ph_ref[b, 0:n, :] = ph_ref[0, pl.ds(b, n), :]


def _phase_taps(base, step=1):
    groups = {}
    for k in range(CONV_KERNEL):
        a, b = divmod(base + step * k, 8)
        groups.setdefault(b, []).append((a, k))
    out = []
    for b in sorted(groups):
        ak = sorted(groups[b])
        assert [a for a, _ in ak] == list(range(ak[0][0], ak[0][0] + len(ak)))
        out.append((b, ak[0][0], [k for _, k in ak]))
    return out


def _cm_fwd(u, dw, dw_b, ln_g, ln_b, *, name, exchange=None):
    S, D2 = u.shape
    D = D2 // 2
    ts = _pick(S, 256, CONV_HALO)
    hb = ts // CONV_HALO
    ex_in, ex_out, ex_scr = ([], [], []) if exchange is None else (exchange.operands, exchange.out_shapes, exchange.scratch)

    def body(u_ref, up_ref, dw_ref, dwb_ref, g_ref, b_ref, *rest):
        xi = rest[:len(ex_in)]
        c_ref, s_ref = rest[len(ex_in):len(ex_in) + 2]
        xo = rest[len(ex_in) + 2:len(ex_in) + 2 + len(ex_out)]
        ext_ref = rest[len(ex_in) + 2 + len(ex_out)]
        i = pl.program_id(0)
        if exchange is not None:
            exchange.emit(i, S // ts, xi, xo, rest[len(ex_in) + 3 + len(ex_out):])
        prev = up_ref[:, :D] * _sigmoid(up_ref[:, D:])
        ext_ref[0:CONV_HALO, :] = jnp.where(i > 0, prev, 0.0)
        ext_ref[CONV_HALO:CONV_HALO + ts, :] = u_ref[:, :D] * _sigmoid(u_ref[:, D:])
        for cc in range(D // 128):
            sl = slice(cc * 128, (cc + 1) * 128)
            acc = jnp.zeros((ts, 128), F32) + dwb_ref[:, sl]
            for k in range(CONV_KERNEL):
                acc = acc + dw_ref[k:k + 1, sl] * ext_ref[pl.ds(CONV_HALO - (CONV_KERNEL - 1) + k, ts), sl]
            c_ref[:, sl] = acc
        c = c_ref[...]
        mu = jnp.mean(c, axis=-1, keepdims=True)
        xc = c - mu
        rstd = lax.rsqrt(jnp.mean(xc * xc, axis=-1, keepdims=True) + EPS)
        y = xc * rstd * g_ref[...] + b_ref[...]
        s_ref[...] = (y * _sigmoid(y)).astype(s_ref.dtype)

    vec = pl.BlockSpec((1, D), lambda i: (0, 0))
    return pl.pallas_call(
        body, name=name, grid=(S // ts,),
        in_specs=[pl.BlockSpec((ts, D2), lambda i: (i, 0)),
                  pl.BlockSpec((CONV_HALO, D2), lambda i: (jnp.maximum(i * hb - 1, 0), 0)),
                  pl.BlockSpec((CONV_KERNEL, D), lambda i: (0, 0)), vec, vec, vec] + [_ANY] * len(ex_in),
        out_specs=[pl.BlockSpec((ts, D), lambda i: (i, 0)), pl.BlockSpec((ts, D), lambda i: (i, 0))] + [_ANY] * len(ex_out),
        out_shape=[jax.ShapeDtypeStruct((S, D), F32), jax.ShapeDtypeStruct((S, D), MXU_DTYPE)] + list(ex_out),
        scratch_shapes=[pltpu.VMEM((ts + CONV_HALO, D), F32)] + list(ex_scr),
        compiler_params=_params("parallel" if exchange is None else "arbitrary"),
    )(u, u, dw, dw_b, ln_g, ln_b, *ex_in)


def _cm_ln_bwd(c, ds, ln_g, ln_b, *, name):
    S, D = c.shape
    ts = _pick(S, 512, 16)

    def body(c_ref, ds_ref, g_ref, b_ref, dc_ref, dg_ref, db_ref):
        i = pl.program_id(0)
        cv = c_ref[...]
        mu = jnp.mean(cv, axis=-1, keepdims=True)
        xc = cv - mu
        rstd = lax.rsqrt(jnp.mean(xc * xc, axis=-1, keepdims=True) + EPS)
        xh = xc * rstd
        y = xh * g_ref[...] + b_ref[...]
        sg = _sigmoid(y)
        dy = ds_ref[...].astype(F32) * (sg * (1.0 + y * (1.0 - sg)))
        gy = dy * g_ref[...]
        dc_ref[...] = rstd * (gy - jnp.mean(gy, axis=-1, keepdims=True) - xh * jnp.mean(gy * xh, axis=-1, keepdims=True))
        dg = jnp.sum(dy * xh, axis=0, keepdims=True)
        db = jnp.sum(dy, axis=0, keepdims=True)

        @pl.when(i == 0)
        def _():
            dg_ref[...] = dg
            db_ref[...] = db

        @pl.when(i > 0)
        def _():
            dg_ref[...] += dg
            db_ref[...] += db

    row = pl.BlockSpec((ts, D), lambda i: (i, 0))
    vec = pl.BlockSpec((1, D), lambda i: (0, 0))
    return pl.pallas_call(
        body, name=name, grid=(S // ts,), in_specs=[row, row, vec, vec], out_specs=[row, vec, vec],
        out_shape=[jax.ShapeDtypeStruct((S, D), F32), jax.ShapeDtypeStruct((1, D), F32), jax.ShapeDtypeStruct((1, D), F32)],
        compiler_params=_params("arbitrary"),
    )(c, ds, ln_g, ln_b)


def _cm_conv_bwd(dc, u, dw, *, name, exchange=None):
    ex_in, ex_out, ex_scr = ([], [], []) if exchange is None else (exchange.operands, exchange.out_shapes, exchange.scratch)
    S, D2 = u.shape
    D = D2 // 2
    ts = _pick(S, 256, CONV_HALO)
    hb = ts // CONV_HALO
    n_t = S // ts
    last_h = S // CONV_HALO - 1

    rc = _pick(ts, CONV_ROWS, 8)

    def fold8(v):
        out = v[0:8]
        for j in range(1, v.shape[0] // 8):
            out = out + v[8 * j:8 * j + 8]
        return out

    def body(dc_ref, dcn_ref, u_ref, up_ref, dw_ref, *rest):
        xi = rest[:len(ex_in)]
        du_ref, ddw_ref, ddwb_ref, dbin_ref = rest[len(ex_in):len(ex_in) + 4]
        xo = rest[len(ex_in) + 4:len(ex_in) + 4 + len(ex_out)]
        dph_ref, gph_ref, dgl_ref = rest[len(ex_in) + 4 + len(ex_out):len(ex_in) + 7 + len(ex_out)]
        i = pl.program_id(0)
        if exchange is not None:
            exchange.emit(i, n_t, xi, xo, rest[len(ex_in) + 7 + len(ex_out):])
        dph_ref[0, 0:ts, :] = dc_ref[...]
        dph_ref[0, ts:ts + CONV_HALO, :] = jnp.where(i < n_t - 1, dcn_ref[...], 0.0)
        prev = up_ref[:, :D] * _sigmoid(up_ref[:, D:])
        gph_ref[0, 0:CONV_HALO, :] = jnp.where(i > 0, prev, 0.0)
        gph_ref[0, CONV_HALO:CONV_HALO + ts, :] = u_ref[:, :D] * _sigmoid(u_ref[:, D:])
        _conv_phases(dph_ref, ts)
        _conv_phases(gph_ref, ts)

        @pl.when(i == 0)
        def _():
            ddw_ref[...] = jnp.zeros_like(ddw_ref)
            ddwb_ref[...] = jnp.zeros_like(ddwb_ref)
            dbin_ref[...] = jnp.zeros_like(dbin_ref)

        for cc in range(D // 128):
            sl = slice(cc * 128, (cc + 1) * 128)
            sl2 = slice(D + cc * 128, D + (cc + 1) * 128)
            wk = [dw_ref[k:k + 1, sl] for k in range(CONV_KERNEL)]
            acc_a, acc_g = jnp.zeros((8, 128), F32), jnp.zeros((8, 128), F32)
            dgl = jnp.zeros((ts, 128), F32)
            for b, a0, taps in _phase_taps(CONV_KERNEL - 1, -1):
                for j, k in enumerate(taps):
                    dgl = dgl + wk[k] * dph_ref[b, 8 * (a0 + j):8 * (a0 + j) + ts, sl]
            dgl_ref[...] = dgl
            for r0 in range(0, ts, rc):
                dglu = dgl_ref[r0:r0 + rc, :]
                av = u_ref[r0:r0 + rc, sl]
                sg = _sigmoid(u_ref[r0:r0 + rc, sl2])
                da = dglu * sg
                dg = dglu * av * sg * (1.0 - sg)
                du_ref[r0:r0 + rc, sl] = da.astype(du_ref.dtype)
                du_ref[r0:r0 + rc, sl2] = dg.astype(du_ref.dtype)
                acc_a = acc_a + fold8(da)
                acc_g = acc_g + fold8(dg)
            dbin_ref[:, sl] += jnp.sum(acc_a, axis=0, keepdims=True)
            dbin_ref[:, sl2] += jnp.sum(acc_g, axis=0, keepdims=True)
            for gi, (b, a0, taps) in enumerate(_phase_taps(CONV_HALO - (CONV_KERNEL - 1))):
                accs = [jnp.zeros((8, 128), F32) for _ in taps]
                accb = jnp.zeros((8, 128), F32)
                for r0 in range(0, ts, rc):
                    dcc = dph_ref[0, r0:r0 + rc, sl]
                    win = gph_ref[b, 8 * a0 + r0:8 * (a0 + len(taps) - 1) + r0 + rc, sl]
                    for j in range(len(taps)):
                        accs[j] = accs[j] + fold8(dcc * win[8 * j:8 * j + rc])
                    if gi == 0:
                        accb = accb + fold8(dcc)
                for j, k in enumerate(taps):
                    ddw_ref[k:k + 1, sl] += jnp.sum(accs[j], axis=0, keepdims=True)
                if gi == 0:
                    ddwb_ref[:, sl] += jnp.sum(accb, axis=0, keepdims=True)

    return pl.pallas_call(
        body, name=name, grid=(n_t,),
        in_specs=[pl.BlockSpec((ts, D), lambda i: (i, 0)),
                  pl.BlockSpec((CONV_HALO, D), lambda i: (jnp.minimum((i + 1) * hb, last_h), 0)),
                  pl.BlockSpec((ts, D2), lambda i: (i, 0)),
                  pl.BlockSpec((CONV_HALO, D2), lambda i: (jnp.maximum(i * hb - 1, 0), 0)),
                  pl.BlockSpec((CONV_KERNEL, D), lambda i: (0, 0))] + [_ANY] * len(ex_in),
        out_specs=[pl.BlockSpec((ts, D2), lambda i: (i, 0)), pl.BlockSpec((CONV_HALO, D), lambda i: (0, 0)),
                   pl.BlockSpec((1, D), lambda i: (0, 0)), pl.BlockSpec((1, D2), lambda i: (0, 0))] + [_ANY] * len(ex_out),
        out_shape=[jax.ShapeDtypeStruct((S, D2), MXU_DTYPE), jax.ShapeDtypeStruct((CONV_HALO, D), F32),
                   jax.ShapeDtypeStruct((1, D), F32), jax.ShapeDtypeStruct((1, D2), F32)] + list(ex_out),
        scratch_shapes=[pltpu.VMEM((8, ts + CONV_HALO, D), F32), pltpu.VMEM((8, ts + CONV_HALO, D), F32),
                        pltpu.VMEM((ts, 128), F32)] + list(ex_scr),
        compiler_params=_params("arbitrary"),
    )(dc, dc, u, u, dw, *ex_in)


def _ffn_cols(F2):
    return _pick(F2, 1024, 256)


def _ffn_up_act(hf, wupT, dw, dw_b, *, name):
    S, D = hf.shape
    F2 = wupT.shape[0]
    ts = _pick(S, 512, 16)
    tc = _ffn_cols(F2)
    n_ct = F2 // tc
    hb = ts // FFN_HALO
    rc = _pick(ts, FFN_ROWS, 16)

    def body(h_ref, hp_ref, wt_ref, w_ref, b_ref, up_ref, a_ref, he_ref, ext_ref):
        i = pl.program_id(0)
        he_ref[0:FFN_HALO, :] = hp_ref[...]
        he_ref[FFN_HALO:FFN_HALO + ts, :] = h_ref[...]

        def product(j):
            return lax.dot_general(he_ref[...].astype(MXU_DTYPE), wt_ref[j * tc:(j + 1) * tc, :].astype(MXU_DTYPE),
                                   (_DIMS["nt"], ((), ())), preferred_element_type=F32)

        def conv_gate(j, res):
            ext = ext_ref.at[j % 2]
            upv = res.astype(up_ref.dtype)
            up_ref[:, j * tc:(j + 1) * tc] = upv[FFN_HALO:FFN_HALO + ts]
            ext[0:FFN_HALO, :] = jnp.where(i > 0, upv[0:FFN_HALO].astype(F32), 0.0)
            ext[FFN_HALO:FFN_HALO + ts, :] = upv[FFN_HALO:FFN_HALO + ts].astype(F32)
            for q in range(tc // 256):
                sls = [slice(q * 256 + half * 128, q * 256 + half * 128 + 128) for half in range(2)]
                gls = [slice(j * tc + sl.start, j * tc + sl.stop) for sl in sls]
                wk = [[w_ref[k:k + 1, gl] for k in range(FFN_KERNEL)] for gl in gls]
                bb = [b_ref[:, gl] for gl in gls]
                for r0 in range(0, ts, rc):
                    gt, vl = [bb[h] + sum(wk[h][k] * ext[pl.ds(FFN_HALO + r0 - 2 + k, rc), sls[h]] for k in range(FFN_KERNEL))
                              for h in range(2)]
                    a_ref[r0:r0 + rc, j * (tc // 2) + q * 128:j * (tc // 2) + (q + 1) * 128] = (gt * _sigmoid(gt) * vl).astype(a_ref.dtype)

        res = product(0)
        for j in range(n_ct):
            nxt = product(j + 1) if j + 1 < n_ct else None
            conv_gate(j, res)
            res = nxt

    return pl.pallas_call(
        body, name=name, grid=(S // ts,),
        in_specs=[pl.BlockSpec((ts, D), lambda i: (i, 0)),
                  pl.BlockSpec((FFN_HALO, D), lambda i: (jnp.maximum(i * hb - 1, 0), 0)),
                  pl.BlockSpec((F2, D), lambda i: (0, 0), pipeline_mode=pl.Buffered(1)),
                  pl.BlockSpec((FFN_KERNEL, F2), lambda i: (0, 0)), pl.BlockSpec((1, F2), lambda i: (0, 0))],
        out_specs=[pl.BlockSpec((ts, F2), lambda i: (i, 0)), pl.BlockSpec((ts, F2 // 2), lambda i: (i, 0))],
        out_shape=[jax.ShapeDtypeStruct((S, F2), MXU_DTYPE), jax.ShapeDtypeStruct((S, F2 // 2), MXU_DTYPE)],
        scratch_shapes=[pltpu.VMEM((ts + FFN_HALO, D), hf.dtype), pltpu.VMEM((2, ts + FFN_HALO, tc), F32)],
        compiler_params=_params("parallel", vmem=V7X_VMEM_BYTES * 7 // 8),
    )(hf, hf, wupT, dw, dw_b)


def _ffn_act_bwd(up, dact, dw, dw_b, *, name, exchange=None):
    ex_in, ex_out, ex_scr = ([], [], []) if exchange is None else (exchange.operands, exchange.out_shapes, exchange.scratch)
    S, F2 = up.shape
    ts = _pick(S, 512, 16)
    tc = _ffn_cols(F2)
    hb = ts // FFN_HALO
    n_t = S // ts
    last_h = S // FFN_HALO - 1
    E = ts + FFN_HALO

    rc = _pick(ts, FFN_ROWS, 16)

    def fold8(v):
        out = v[0:8]
        for j in range(1, v.shape[0] // 8):
            out = out + v[8 * j:8 * j + 8]
        return out

    def body(u_ref, up_ref, un_ref, da_ref, dan_ref, w_ref, b_ref, *rest):
        xi = rest[:len(ex_in)]
        dup_ref, ddw_ref, ddb_ref = rest[len(ex_in):len(ex_in) + 3]
        xo = rest[len(ex_in) + 3:len(ex_in) + 3 + len(ex_out)]
        ue_ref, dcv_ref = rest[len(ex_in) + 3 + len(ex_out):len(ex_in) + 5 + len(ex_out)]
        i = pl.program_id(1)
        if exchange is not None:
            exchange.emit(pl.program_id(0) * n_t + i, (F2 // tc) * n_t, xi, xo, rest[len(ex_in) + 5 + len(ex_out):])
        ue_ref[0:FFN_HALO, :] = jnp.where(i > 0, up_ref[...].astype(F32), 0.0)
        ue_ref[FFN_HALO:FFN_HALO + ts, :] = u_ref[...].astype(F32)
        ue_ref[FFN_HALO + ts:FFN_HALO + ts + FFN_HALO, :] = jnp.where(i < n_t - 1, un_ref[...].astype(F32), 0.0)

        @pl.when(i == 0)
        def _():
            ddw_ref[...] = jnp.zeros_like(ddw_ref)
            ddb_ref[...] = jnp.zeros_like(ddb_ref)

        for q in range(tc // 256):
            sls = [slice(q * 256 + half * 128, q * 256 + half * 128 + 128) for half in range(2)]
            qs = slice(q * 128, (q + 1) * 128)
            wk = [[w_ref[k:k + 1, sl] for k in range(FFN_KERNEL)] for sl in sls]
            bb = [b_ref[:, sl] for sl in sls]
            acc = [[jnp.zeros((8, 128), F32) for _ in range(FFN_KERNEL)] for _ in range(2)]
            accb = [jnp.zeros((8, 128), F32) for _ in range(2)]
            for r0, rows in [(r, rc) for r in range(0, ts, rc)] + [(ts, FFN_HALO)]:
                xs = [[ue_ref[pl.ds(FFN_HALO + r0 - 2 + k, rows), sls[h]] for k in range(FFN_KERNEL)] for h in range(2)]
                gt, vl = [bb[h] + sum(wk[h][k] * xs[h][k] for k in range(FFN_KERNEL)) for h in range(2)]
                sg = _sigmoid(gt)
                if r0 < ts:
                    dae = da_ref[r0:r0 + rows, qs].astype(F32)
                else:
                    dae = jnp.where(i < n_t - 1, dan_ref[:, qs].astype(F32), 0.0)
                dcv = [dae * vl * (sg * (1.0 + gt * (1.0 - sg))), dae * (gt * sg)]
                for h in range(2):
                    dcv_ref[r0:r0 + rows, sls[h]] = dcv[h]
                    if r0 < ts:
                        for k in range(FFN_KERNEL):
                            acc[h][k] = acc[h][k] + fold8(dcv[h] * xs[h][k])
                        accb[h] = accb[h] + fold8(dcv[h])
            for r0 in range(0, ts, rc):
                for h in range(2):
                    dup = sum(wk[h][2 - j] * dcv_ref[pl.ds(r0 + j, rc), sls[h]] for j in range(FFN_KERNEL))
                    dup_ref[r0:r0 + rc, sls[h]] = dup.astype(dup_ref.dtype)
            for h in range(2):
                for k in range(FFN_KERNEL):
                    ddw_ref[k:k + 1, sls[h]] += jnp.sum(acc[h][k], axis=0, keepdims=True)
                ddb_ref[:, sls[h]] += jnp.sum(accb[h], axis=0, keepdims=True)

    return pl.pallas_call(
        body, name=name, grid=(F2 // tc, n_t),
        in_specs=[pl.BlockSpec((ts, tc), lambda j, i: (i, j)),
                  pl.BlockSpec((FFN_HALO, tc), lambda j, i: (jnp.maximum(i * hb - 1, 0), j)),
                  pl.BlockSpec((FFN_HALO, tc), lambda j, i: (jnp.minimum((i + 1) * hb, last_h), j)),
                  pl.BlockSpec((ts, tc // 2), lambda j, i: (i, j)),
                  pl.BlockSpec((FFN_HALO, tc // 2), lambda j, i: (jnp.minimum((i + 1) * hb, last_h), j)),
                  pl.BlockSpec((FFN_KERNEL, tc), lambda j, i: (0, j)),
                  pl.BlockSpec((1, tc), lambda j, i: (0, j))] + [_ANY] * len(ex_in),
        out_specs=[pl.BlockSpec((ts, tc), lambda j, i: (i, j)), pl.BlockSpec((FFN_HALO, tc), lambda j, i: (0, j)),
                   pl.BlockSpec((1, tc), lambda j, i: (0, j))] + [_ANY] * len(ex_out),
        out_shape=[jax.ShapeDtypeStruct((S, F2), MXU_DTYPE), jax.ShapeDtypeStruct((FFN_HALO, F2), F32),
                   jax.ShapeDtypeStruct((1, F2), F32)] + list(ex_out),
        scratch_shapes=[pltpu.VMEM((ts + 2 * FFN_HALO, tc), F32), pltpu.VMEM((E, tc), F32)] + list(ex_scr),
        compiler_params=_params("parallel" if exchange is None else "arbitrary", "arbitrary"),
    )(up, up, up, dact, dact, dw, dw_b, *ex_in)


def _slopes(n_heads_total):
    return np.asarray(2.0 ** (-ALIBI_MAX * (np.arange(n_heads_total, dtype=np.float32) + 1.0) / n_heads_total), np.float32)


def _qkv_proj(h, w_qkvT, *, grp, name):
    S, D = h.shape
    H = D // HEAD_DIM
    tm = _pick(S, 1024, 16)

    rows = _pick(tm, 256, 16)

    def body(a_ref, b_ref, o_ref, r_ref):
        j = pl.program_id(1)
        r_ref[...] = jnp.zeros_like(r_ref)

        def product(c):
            return lax.dot_general(a_ref[c * rows:(c + 1) * rows, :].astype(MXU_DTYPE), b_ref[...].astype(MXU_DTYPE),
                                   (_DIMS["nt"], ((), ())), preferred_element_type=F32)

        @pl.when(j < 2)
        def _():
            acc = product(0)
            for c in range(tm // rows):
                nxt = product(c + 1) if c + 1 < tm // rows else None
                rs = slice(c * rows, (c + 1) * rows)
                for hd in range(H):
                    hs = slice(hd * HEAD_DIM, (hd + 1) * HEAD_DIM)
                    xv = acc[:, hs]
                    r = lax.rsqrt(jnp.mean(xv * xv, axis=-1, keepdims=True) + EPS)
                    o_ref[rs, hs] = (xv * r).astype(o_ref.dtype)
                    r_ref[rs, hd:hd + 1] = r
                acc = nxt

        @pl.when(j == 2)
        def _():
            o_ref[...] = lax.dot_general(a_ref[...].astype(MXU_DTYPE), b_ref[...].astype(MXU_DTYPE), (_DIMS["nt"], ((), ())),
                                         preferred_element_type=F32).astype(o_ref.dtype)

    return pl.pallas_call(
        body, name=name, grid=(S // tm, 3),
        in_specs=[pl.BlockSpec((tm, D), lambda i, j: (i, 0)), pl.BlockSpec((D, D), lambda i, j: (grp * 3 + j, 0))],
        out_specs=[pl.BlockSpec((tm, D), lambda i, j: (i, j)), pl.BlockSpec((tm, HEAD_DIM), lambda i, j: (i, j))],
        out_shape=[jax.ShapeDtypeStruct((S, 3 * D), MXU_DTYPE), jax.ShapeDtypeStruct((S, 3 * HEAD_DIM), F32)],
        compiler_params=_params("parallel", "parallel"),
    )(h, w_qkvT)


def _band(b, dil):
    qi = lax.broadcasted_iota(jnp.int32, (BLOCK, 2 * BLOCK), 0)
    ki = lax.broadcasted_iota(jnp.int32, (BLOCK, 2 * BLOCK), 1)
    delta = qi + BLOCK - ki
    valid = (delta >= 0) & (delta <= BLOCK) & ((ki >= BLOCK) | (b > 0))
    return valid, (delta * dil).astype(F32)


def _attn_fwd(qkv, qg, kg, *, grp, name):
    S, W = qkv.shape
    D = W // 3
    H = D // HEAD_DIM
    dil = DILATED_GROUPS[grp][1]
    L = S // dil
    nb = L // BLOCK
    slopes = _slopes(3 * H)[grp * H:(grp + 1) * H]
    scale = HEAD_DIM ** -0.5

    def body(q_ref, kp_ref, kc_ref, vp_ref, vc_ref, qg_ref, kg_ref, o_ref, l_ref):
        b = pl.program_id(1)
        valid, dist = _band(b, dil)
        l_ref[...] = jnp.zeros_like(l_ref)
        ss = []
        for h in range(H):
            hs = slice(h * HEAD_DIM, (h + 1) * HEAD_DIM)
            qn = (q_ref[:, hs].astype(F32) * qg_ref[h:h + 1, :]).astype(MXU_DTYPE)
            kp = (kp_ref[:, hs].astype(F32) * kg_ref[h:h + 1, :]).astype(MXU_DTYPE)
            kc = (kc_ref[:, hs].astype(F32) * kg_ref[h:h + 1, :]).astype(MXU_DTYPE)
            ss.append(lax.dot_general(qn, jnp.concatenate([kp, kc], axis=0), (((1,), (1,)), ((), ())), preferred_element_type=F32))
        ps = []
        for h in range(H):
            s = jnp.where(valid, ss[h] * scale - float(slopes[h]) * dist, NEG)
            m = jnp.max(s, axis=-1, keepdims=True)
            p = jnp.exp(s - m)
            den = jnp.sum(p, axis=-1, keepdims=True)
            l_ref[:, h:h + 1] = m + jnp.log(den)
            ps.append((p.astype(MXU_DTYPE), den))
        for h in range(H):
            hs = slice(h * HEAD_DIM, (h + 1) * HEAD_DIM)
            pb, den = ps[h]
            v2 = jnp.concatenate([vp_ref[:, hs], vc_ref[:, hs]], axis=0).astype(MXU_DTYPE)
            o_ref[:, hs] = (jnp.dot(pb, v2, preferred_element_type=F32) / den).astype(o_ref.dtype)

    def cur(j):
        return lambda r, b: (r * nb + b, j)

    def prv(j):
        return lambda r, b: (r * nb + jnp.maximum(b - 1, 0), j)

    blk = (BLOCK, D)
    gain = pl.BlockSpec((H, HEAD_DIM), lambda r, b: (0, 0))
    return pl.pallas_call(
        body, name=name, grid=(dil, nb),
        in_specs=[pl.BlockSpec(blk, cur(0)), pl.BlockSpec(blk, prv(1)), pl.BlockSpec(blk, cur(1)),
                  pl.BlockSpec(blk, prv(2)), pl.BlockSpec(blk, cur(2)), gain, gain],
        out_specs=[pl.BlockSpec(blk, cur(0)), pl.BlockSpec((BLOCK, HEAD_DIM), cur(0))],
        out_shape=[jax.ShapeDtypeStruct((S, D), MXU_DTYPE), jax.ShapeDtypeStruct((S, HEAD_DIM), F32)],
        compiler_params=_params("parallel", "parallel"),
    )(qkv, qkv, qkv, qkv, qkv, qg, kg)


def _attn_merge(os_, ls_, dils, *, name):
    S, D = os_[0].shape
    H = D // HEAD_DIM
    G = len(dils)
    ts = _pick(S, SUB_TILE, 16 * max(dils))
    subs = [g for g in range(G) if dils[g] > 1]

    def body(*refs):
        o_refs, l_refs = refs[0:G], refs[G:2 * G]
        outb_ref = refs[2 * G]
        lt_refs = refs[2 * G + 1:3 * G + 1]
        scratch = refs[3 * G + 1:]
        lt_tok = scratch[0]
        o_tok = {g: scratch[1 + 2 * j] for j, g in enumerate(subs)}
        l_tok = {g: scratch[2 + 2 * j] for j, g in enumerate(subs)}
        for g in subs:
            _sub_to_tok(o_refs[g], o_tok[g], dils[g])
            _sub_to_tok(l_refs[g], l_tok[g], dils[g])
        lt_tok[0] = jnp.zeros((ts, HEAD_DIM), F32)
        for h in range(H):
            hs = slice(h * HEAD_DIM, (h + 1) * HEAD_DIM)
            ls = [l_tok[g][0][:, h:h + 1] if g in subs else l_refs[g][:, h:h + 1] for g in range(G)]
            ov = [o_tok[g][h] if g in subs else o_refs[g][:, hs].astype(F32) for g in range(G)]
            m = functools.reduce(jnp.maximum, ls)
            es = [jnp.exp(l - m) for l in ls]
            den = functools.reduce(lambda a, b: a + b, es)
            out = functools.reduce(lambda a, b: a + b, [e * o for e, o in zip(es, ov)]) / den
            outb_ref[:, hs] = out.astype(outb_ref.dtype)
            lt_tok.at[0][:, h:h + 1] = m + jnp.log(den)
        for g in range(G):
            if g in subs:
                _tok_to_sub(lt_tok, lt_refs[g], dils[g])
            else:
                lt_refs[g][...] = lt_tok[0]

    def spec(g, cols):
        return _sub_spec(dils[g], ts, cols) if g in subs else pl.BlockSpec((ts, cols), lambda i: (i, 0))

    def shape(g, cols, dtype):
        return jax.ShapeDtypeStruct((dils[g], S // dils[g], cols) if g in subs else (S, cols), dtype)

    def view(a, g):
        return a.reshape(dils[g], S // dils[g], a.shape[-1]) if g in subs else a

    outs = pl.pallas_call(
        body, name=name, grid=(S // ts,),
        in_specs=[spec(g, D) for g in range(G)] + [spec(g, HEAD_DIM) for g in range(G)],
        out_specs=[pl.BlockSpec((ts, D), lambda i: (i, 0))] + [spec(g, HEAD_DIM) for g in range(G)],
        out_shape=[jax.ShapeDtypeStruct((S, D), MXU_DTYPE)] + [shape(g, HEAD_DIM, F32) for g in range(G)],
        scratch_shapes=[pltpu.VMEM((1, ts, HEAD_DIM), F32)] + [pltpu.VMEM((H, ts, HEAD_DIM), F32), pltpu.VMEM((1, ts, HEAD_DIM), F32)] * len(subs),
        compiler_params=_params("parallel"),
    )(*[view(o, g) for g, o in enumerate(os_)], *[view(l, g) for g, l in enumerate(ls_)])
    return outs[0], [t.reshape(S, HEAD_DIM) for t in outs[1:]]


def _attn_delta(do, out, dils, *, name):
    S, D = out.shape
    H = D // HEAD_DIM
    G = len(dils)
    ts = _pick(S, SUB_TILE, 16 * max(dils))
    subs = [g for g in range(G) if dils[g] > 1]

    def body(do_ref, o_ref, *rest):
        d_refs = rest[0:G]
        dos_refs = rest[G:G + len(subs)]
        d_tok, do_tok = rest[G + len(subs):]
        d_tok[0] = jnp.zeros((ts, HEAD_DIM), F32)
        for h in range(H):
            hs = slice(h * HEAD_DIM, (h + 1) * HEAD_DIM)
            dov = do_ref[:, hs].astype(F32)
            do_tok[h] = dov
            d_tok.at[0][:, h:h + 1] = jnp.sum(dov * o_ref[:, hs].astype(F32), axis=-1, keepdims=True)
        for g in range(G):
            if g in subs:
                _tok_to_sub(d_tok, d_refs[g], dils[g])
            else:
                d_refs[g][...] = d_tok[0]
        for g, dst in zip(subs, dos_refs):
            _tok_to_sub(do_tok, dst, dils[g])

    def spec(g, cols):
        return _sub_spec(dils[g], ts, cols) if g in subs else pl.BlockSpec((ts, cols), lambda i: (i, 0))

    def shape(g, cols, dtype):
        return jax.ShapeDtypeStruct((dils[g], S // dils[g], cols) if g in subs else (S, cols), dtype)

    row = pl.BlockSpec((ts, D), lambda i: (i, 0))
    outs = pl.pallas_call(
        body, name=name, grid=(S // ts,), in_specs=[row, row],
        out_specs=[spec(g, HEAD_DIM) for g in range(G)] + [spec(g, D) for g in subs],
        out_shape=[shape(g, HEAD_DIM, F32) for g in range(G)] + [shape(g, D, do.dtype) for g in subs],
        scratch_shapes=[pltpu.VMEM((1, ts, HEAD_DIM), F32), pltpu.VMEM((H, ts, HEAD_DIM), F32)],
        compiler_params=_params("parallel"),
    )(do, out)
    deltas = [t.reshape(S, HEAD_DIM) for t in outs[0:G]]
    dos = {g: t.reshape(S, D) for g, t in zip(subs, outs[G:])}
    return deltas, [dos[g] if g in subs else do for g in range(G)]


def _attn_bwd(qkv, rqk, do, lse, delta, qg, kg, *, grp, name):
    S, W = qkv.shape
    D = W // 3
    H = D // HEAD_DIM
    dil = DILATED_GROUPS[grp][1]
    L = S // dil
    nb = L // BLOCK
    slopes = _slopes(3 * H)[grp * H:(grp + 1) * H]
    scale = HEAD_DIM ** -0.5

    def body(q_ref, qp_ref, kp_ref, kc_ref, vp_ref, vc_ref, do_ref, l_ref, dl_ref, rq_ref, rk_ref, qg_ref, kg_ref,
             out_ref, dqg_ref, dkg_ref, cq_ref, ck_ref, cv_ref, nq_ref, nk_ref, nv_ref, pk_ref, pv_ref):
        r = pl.program_id(0)
        b = pl.program_id(1)

        @pl.when(jnp.logical_and(r == 0, b == 0))
        def _():
            dqg_ref[...] = jnp.zeros_like(dqg_ref)
            dkg_ref[...] = jnp.zeros_like(dkg_ref)

        @pl.when(b < nb)
        def _():
            valid, dist = _band(b, dil)

            def operands(h):
                hs = slice(h * HEAD_DIM, (h + 1) * HEAD_DIM)
                qn = (q_ref[:, hs].astype(F32) * qg_ref[h:h + 1, :]).astype(MXU_DTYPE)
                kp = (kp_ref[:, hs].astype(F32) * kg_ref[h:h + 1, :]).astype(MXU_DTYPE)
                kc = (kc_ref[:, hs].astype(F32) * kg_ref[h:h + 1, :]).astype(MXU_DTYPE)
                k2 = jnp.concatenate([kp, kc], axis=0)
                v2 = jnp.concatenate([vp_ref[:, hs], vc_ref[:, hs]], axis=0).astype(MXU_DTYPE)
                return hs, qn, k2, v2, do_ref[:, hs].astype(MXU_DTYPE)

            sdp = []
            for h in range(H):
                hs, qn, k2, v2, doh = operands(h)
                s = lax.dot_general(qn, k2, (((1,), (1,)), ((), ())), preferred_element_type=F32)
                dp = lax.dot_general(doh, v2, (((1,), (1,)), ((), ())), preferred_element_type=F32)
                sdp.append((s, dp))
            pds = []
            for h in range(H):
                s, dp = sdp[h]
                s = jnp.where(valid, s * scale - float(slopes[h]) * dist, NEG)
                p = jnp.exp(s - l_ref[:, h:h + 1])
                pds.append((p.astype(MXU_DTYPE), (p * (dp - dl_ref[:, h:h + 1]) * scale).astype(MXU_DTYPE)))
            for h in range(H):
                hs, qn, k2, v2, doh = operands(h)
                pb, dsc = pds[h]
                nq_ref[:, hs] = jnp.dot(dsc, k2, preferred_element_type=F32)
                dk2 = lax.dot_general(dsc, qn, (((0,), (0,)), ((), ())), preferred_element_type=F32)
                dv2 = lax.dot_general(pb, doh, (((0,), (0,)), ((), ())), preferred_element_type=F32)
                pk_ref[:, hs] = dk2[0:BLOCK]
                nk_ref[:, hs] = dk2[BLOCK:2 * BLOCK]
                pv_ref[:, hs] = dv2[0:BLOCK]
                nv_ref[:, hs] = dv2[BLOCK:2 * BLOCK]

        @pl.when(b == nb)
        def _():
            pk_ref[...] = jnp.zeros_like(pk_ref)
            pv_ref[...] = jnp.zeros_like(pv_ref)

        @pl.when(b > 0)
        def _():
            for h in range(H):
                hs = slice(h * HEAD_DIM, (h + 1) * HEAD_DIM)
                for j, (xh_ref, r_ref, gain_ref, dgain_ref) in enumerate(((qp_ref, rq_ref, qg_ref, dqg_ref),
                                                                          (kp_ref, rk_ref, kg_ref, dkg_ref))):
                    dy = cq_ref[:, hs] if j == 0 else ck_ref[:, hs] + pk_ref[:, hs]
                    gain = gain_ref[h:h + 1, :]
                    xh = xh_ref[:, hs].astype(F32)
                    rr = r_ref[:, h:h + 1]
                    gy = dy * gain
                    dx = rr * (gy - xh * jnp.mean(gy * xh, axis=-1, keepdims=True))
                    out_ref[:, j * D + h * HEAD_DIM:j * D + (h + 1) * HEAD_DIM] = dx.astype(out_ref.dtype)
                    dgain_ref[h:h + 1, :] += jnp.sum(dy * xh, axis=0, keepdims=True)
                out_ref[:, 2 * D + h * HEAD_DIM:2 * D + (h + 1) * HEAD_DIM] = (cv_ref[:, hs] + pv_ref[:, hs]).astype(out_ref.dtype)

        @pl.when(b < nb)
        def _():
            cq_ref[...] = nq_ref[...]
            ck_ref[...] = nk_ref[...]
            cv_ref[...] = nv_ref[...]

    def cur(j):
        return lambda r, b: (r * nb + jnp.minimum(b, nb - 1), j)

    def prv(j):
        return lambda r, b: (r * nb + jnp.clip(b - 1, 0, nb - 1), j)

    blk = (BLOCK, D)
    lblk = pl.BlockSpec((BLOCK, HEAD_DIM), cur(0))
    gain = pl.BlockSpec((H, HEAD_DIM), lambda r, b: (0, 0))
    return pl.pallas_call(
        body, name=name, grid=(dil, nb + 1),
        in_specs=[pl.BlockSpec(blk, cur(0)), pl.BlockSpec(blk, prv(0)), pl.BlockSpec(blk, prv(1)), pl.BlockSpec(blk, cur(1)),
                  pl.BlockSpec(blk, prv(2)), pl.BlockSpec(blk, cur(2)), pl.BlockSpec(blk, cur(0)), lblk, lblk,
                  pl.BlockSpec((BLOCK, HEAD_DIM), prv(0)), pl.BlockSpec((BLOCK, HEAD_DIM), prv(1)), gain, gain],
        out_specs=[pl.BlockSpec((BLOCK, 3 * D), lambda r, b: (r * nb + jnp.maximum(b - 1, 0), 0)), gain, gain],
        out_shape=[jax.ShapeDtypeStruct((S, 3 * D), MXU_DTYPE), jax.ShapeDtypeStruct((H, HEAD_DIM), F32),
                   jax.ShapeDtypeStruct((H, HEAD_DIM), F32)],
        scratch_shapes=[pltpu.VMEM(blk, F32)] * 8,
        compiler_params=_params("arbitrary", "arbitrary"),
    )(qkv, qkv, qkv, qkv, qkv, qkv, do, lse, delta, rqk, rqk, qg, kg)


def _adamw(w, m, v, terms, slots, *, name):
    R, C = w.shape
    nt = len(terms)
    tr = _pick(R, 256, 16)
    c1 = 1.0 - ADAM_B1 ** ADAM_STEP
    c2 = 1.0 - ADAM_B2 ** ADAM_STEP

    def body(slot_ref, w_ref, m_ref, v_ref, *rest):
        t_refs = rest[:nt]
        g_ref, d_ref, nm_ref, nv_ref = rest[nt:]
        g = t_refs[0][...].astype(F32)
        for t in t_refs[1:]:
            g = g + t[...].astype(F32)
        mm = ADAM_B1 * m_ref[...] + (1.0 - ADAM_B1) * g
        vv = ADAM_B2 * v_ref[...] + (1.0 - ADAM_B2) * (g * g)
        m_hat = mm / c1
        v_hat = vv / c2
        g_ref[...] = g
        d_ref[...] = -ADAM_LR * (m_hat / (jnp.sqrt(v_hat) + ADAM_EPS) + ADAM_WD * w_ref[...])
        nm_ref[...] = mm
        nv_ref[...] = vv

    row = pl.BlockSpec((tr, C), lambda i, s: (i, 0))
    grid_spec = pltpu.PrefetchScalarGridSpec(
        num_scalar_prefetch=1, grid=(R // tr,),
        in_specs=[row, row, row] + [pl.BlockSpec((None, tr, C), lambda i, s, t=t: (s[t], i, 0)) for t in range(nt)],
        out_specs=[row] * 4)
    return pl.pallas_call(
        body, name=name, grid_spec=grid_spec, out_shape=[jax.ShapeDtypeStruct((R, C), F32)] * 4,
        compiler_params=_params("parallel"),
    )(slots, w, m, v, *terms)


def _chip_partials(g, sib, core, *, name):
    _, R, C = g.shape
    tr = _pick(R, 1200, 16)

    def body(core_ref, g_ref, s_ref, o_ref):
        o_ref[...] = (g_ref[...] + s_ref[...].astype(F32)).astype(o_ref.dtype)

    grid_spec = pltpu.PrefetchScalarGridSpec(
        num_scalar_prefetch=1, grid=(4, R // tr),
        in_specs=[pl.BlockSpec((None, tr, C), lambda k, i, c: (2 * k + c[0], i, 0)),
                  pl.BlockSpec((None, tr, C), lambda k, i, c: (k, i, 0))],
        out_specs=pl.BlockSpec((None, tr, C), lambda k, i, c: (k, i, 0)))
    return pl.pallas_call(
        body, name=name, grid_spec=grid_spec, out_shape=jax.ShapeDtypeStruct((4, R, C), sib.dtype),
        compiler_params=_params("parallel", "parallel"),
    )(core, g, sib)


_ANY = pl.BlockSpec(memory_space=pl.ANY)


def _place():
    return lax.axis_index("x"), lax.axis_index("y"), lax.axis_index("c")


class _Exchange:
    def __init__(self, operands, out_shapes, scratch, emit):
        self.operands, self.out_shapes, self.scratch, self.emit = operands, out_shapes, scratch, emit


def _run_exchange(ex, *, name):
    n_in, n_out = len(ex.operands), len(ex.out_shapes)

    def body(*refs):
        ex.emit(0, 1, refs[:n_in], refs[n_in:n_in + n_out], refs[n_in + n_out:])

    return pl.pallas_call(body, name=name, in_specs=[_ANY] * n_in, out_specs=[_ANY] * n_out, out_shape=ex.out_shapes,
                          scratch_shapes=ex.scratch)(*ex.operands)


def _gather_exchange(shard):
    R, C = shard.shape

    def emit(step, n, ins, outs, sems):
        x_ref, out_ref = ins[0], outs[0]
        send_sems, recv_sems, local_sem = sems
        x, y, c = _place()
        me, sibling = (x, y, c), (x, y, 1 - c)
        chips = [(1 - x, y), (x, 1 - y), (1 - x, 1 - y)]

        def slot(px, py, pc):
            return out_ref.at[4 * px + 2 * py + pc]

        def copy(k, block, to, src=None):
            return pltpu.make_async_remote_copy(
                src_ref=slot(*block) if src is None else src, dst_ref=slot(*block),
                send_sem=send_sems.at[k], recv_sem=recv_sems.at[k], device_id=to, device_id_type=MESH)

        mine = pltpu.make_async_copy(x_ref, slot(*me), local_sem)
        first = [copy(0, me, sibling, src=x_ref)] + [copy(1 + j, me, (*chip, c), src=x_ref) for j, chip in enumerate(chips)]
        passed = [copy(4 + j, (*chip, c), sibling) for j, chip in enumerate(chips)]

        @pl.when(step == 0)
        def _():
            mine.start()
            for cp in first:
                cp.start()

        for j, chip in enumerate(chips):
            @pl.when(step == max(n - 2 * (len(chips) - j), 0))
            def _(j=j, chip=chip):
                copy(1 + j, (*chip, c), me).wait_recv()
                passed[j].start()

        @pl.when(step == n - 1)
        def _():
            copy(0, sibling, me).wait_recv()
            for j, chip in enumerate(chips):
                copy(4 + j, (*chip, 1 - c), me).wait_recv()
            for cp in first + passed:
                cp.wait_send()
            mine.wait()

    return _Exchange([shard], [jax.ShapeDtypeStruct((N_DEV, R, C), shard.dtype)],
                     [pltpu.SemaphoreType.DMA((7,)), pltpu.SemaphoreType.DMA((7,)), pltpu.SemaphoreType.DMA], emit)


def _all_gather(shard, *, name):
    return _run_exchange(_gather_exchange(shard), name=name)[0]


def _rs_sibling(g, *, name):
    _, R, C = g.shape

    def body(g_ref, sib_ref, send_sems, recv_sems):
        x, y, c = _place()
        sends = [pltpu.make_async_remote_copy(
            src_ref=g_ref.at[2 * k + (1 - c)], dst_ref=sib_ref.at[k], send_sem=send_sems.at[k], recv_sem=recv_sems.at[k],
            device_id=(x, y, 1 - c), device_id_type=MESH) for k in range(4)]
        for cp in sends:
            cp.start()
        for cp in sends:
            cp.wait_recv()
        for cp in sends:
            cp.wait_send()

    return pl.pallas_call(
        body, name=name, in_specs=[_ANY], out_specs=_ANY, out_shape=jax.ShapeDtypeStruct((4, R, C), g.dtype),
        scratch_shapes=[pltpu.SemaphoreType.DMA((4,)), pltpu.SemaphoreType.DMA((4,))],
    )(g)


def _chips_exchange(part):
    _, R, C = part.shape

    def emit(step, n, ins, outs, sems):
        p_ref, out_ref = ins[0], outs[0]
        send_sems, recv_sems = sems
        x, y, c = _place()
        chips = [(1 - x, y), (x, 1 - y), (1 - x, 1 - y)]
        sends = [pltpu.make_async_remote_copy(
            src_ref=p_ref.at[2 * px + py], dst_ref=out_ref.at[j], send_sem=send_sems.at[j], recv_sem=recv_sems.at[j],
            device_id=(px, py, c), device_id_type=MESH) for j, (px, py) in enumerate(chips)]

        @pl.when(step == 0)
        def _():
            for cp in sends:
                cp.start()

        @pl.when(step == n - 1)
        def _():
            for cp in sends:
                cp.wait_recv()
            for cp in sends:
                cp.wait_send()

    return _Exchange([part], [jax.ShapeDtypeStruct((3, R, C), part.dtype)],
                     [pltpu.SemaphoreType.DMA((3,)), pltpu.SemaphoreType.DMA((3,))], emit)


def _rs_chips(part, *, name):
    return _run_exchange(_chips_exchange(part), name=name)[0]


def _interleave_rows(wt):
    F2, D = wt.shape
    return wt.reshape(2, F2 // 256, 128, D).transpose(1, 0, 2, 3).reshape(F2, D)


def _deinterleave_rows(wt):
    F2, D = wt.shape
    return wt.reshape(F2 // 256, 2, 128, D).transpose(1, 0, 2, 3).reshape(F2, D)


def _interleave_cols(v):
    k, F2 = v.shape
    return v.reshape(k, 2, F2 // 256, 128).transpose(0, 2, 1, 3).reshape(k, F2)


def _deinterleave_cols(v):
    k, F2 = v.shape
    return v.reshape(k, F2 // 256, 2, 128).transpose(0, 2, 1, 3).reshape(k, F2)


def _pack_rows(parts):
    return jnp.concatenate(parts, axis=0)


def _flat_pack(parts, width):
    flat = jnp.concatenate([p.reshape(-1) for p in parts])
    pad = (-flat.shape[0]) % (8 * width)
    return jnp.pad(flat, (0, pad)).reshape(-1, width)


def _flat_unpack(packed, shapes):
    flat = packed.reshape(-1)
    out, off = [], 0
    for shp in shapes:
        n = int(np.prod(shp))
        out.append(flat[off:off + n].reshape(shp))
        off += n
    return out


def _ffn_forward(x, hf, wupT, wdown, dw_i, dwb_i, tag, loss_target=None):
    up, act = _ffn_up_act(hf, wupT, dw_i, dwb_i, name=f"ffn{tag}_up")
    if loss_target is None:
        y = _mm(act, wdown, mode="nn", out_dtype=F32, name=f"ffn{tag}_down", residual=x)
    else:
        y = _mm(act, wdown, mode="nn", out_dtype=F32, name=f"ffn{tag}_down", residual=x, tm=512, post=_post_loss(loss_target),
                keep_main=False)
    return y, (hf, up, act)


def _ffn_backward(x, g_ffn, wupT, wdown, dw_i, dwb_i, saved, dy, dyb, tag, exchange=None):
    hf, up, act = saved
    dact = _mm(dyb, wdown, mode="nt", out_dtype=MXU_DTYPE, name=f"ffn{tag}_dact", tm=1024, tn=1408)
    d_wdown = _mm(act, dyb, mode="tn", out_dtype=F32, name=f"ffn{tag}_dwdown", tm=1408, tk=2048)
    dup, d_dw_i, d_dwb_i, *carried = _ffn_act_bwd(up, dact, dw_i, dwb_i, name=f"ffn{tag}_actbwd", exchange=exchange)
    dx, dxb, dg, cs = _mm_post(dup, wupT, _post_rms_bwd(x, g_ffn, dy), name=f"ffn{tag}_dhf")
    d_wupT = _mm(dup, hf, mode="tn", out_dtype=F32, name=f"ffn{tag}_dwup", tm=1408, tk=2048)
    return dx, dxb, cs, dict(w_upT=d_wupT, w_down=d_wdown, dw=d_dw_i[0:FFN_KERNEL], dw_b=d_dwb_i, norm=dg), carried


def _local_step(x, target, p, late_weights=None, early_reduce=None):
    S, D = x.shape
    H = D // HEAD_DIM
    h0 = _rms_fwd(x, p["norm_mix"][0:1], name="l0_rms")
    u = _mm(h0, p["w_inT"], mode="nt", out_dtype=F32, name="l0_in", bias=p["cm_b_in"])
    c, s, *carried = _cm_fwd(u, p["cm_dw"], p["cm_dw_b"], p["cm_ln_g"], p["cm_ln_b"], name="l0_conv",
                             exchange=None if late_weights is None else late_weights[0])
    if late_weights is not None:
        p = {**p, **late_weights[1](carried)}
    x1, hf0 = _mm(s, p["w_out"], mode="nn", out_dtype=F32, name="l0_out", bias=p["cm_b_out"], residual=x,
                  post=_post_rms(p["norm_ffn"][0:1]))
    x2, sv0 = _ffn_forward(x1, hf0, p["w_upT"][0], p["w_down"][0], p["ff_dw"][0], p["ff_dw_b"][0:1], 0)
    dils = [dil for _, dil in DILATED_GROUPS]
    assert dils[0] == 1
    h1s = [t.reshape(S, D) for t in _rms_fwd(x2, p["norm_mix"][1:2], name="l1_rms", subs=tuple(dils[1:]))]
    qkvs, rqks, os_, ls_ = [], [], [], []
    for g in range(len(dils)):
        qkv_g, r_g = _qkv_proj(h1s[g], p["w_qkvT"], grp=g, name=f"l1_qkv{g}")
        qkvs.append(qkv_g)
        rqks.append(r_g)
        o, l = _attn_fwd(qkvs[g], p["at_q_norm"][g * H:(g + 1) * H], p["at_k_norm"][g * H:(g + 1) * H], grp=g, name=f"l1_attn{g}")
        os_.append(o)
        ls_.append(l)
    outb, lses = _attn_merge(os_, ls_, dils, name="l1_merge")
    x3, hf1 = _mm(outb, p["w_o"], mode="nn", out_dtype=F32, name="l1_o", residual=x2, post=_post_rms(p["norm_ffn"][1:2]))
    (dx4, dx4b, loss), sv1 = _ffn_forward(x3, hf1, p["w_upT"][1], p["w_down"][1], p["ff_dw"][1], p["ff_dw_b"][1:2], 1,
                                          loss_target=target)
    dx3, dx3b, _, gf1, _ = _ffn_backward(x3, p["norm_ffn"][1:2], p["w_upT"][1], p["w_down"][1], p["ff_dw"][1], p["ff_dw_b"][1:2],
                                      sv1, dx4, dx4b, 1)
    do = _mm(dx3b, p["w_o"], mode="nt", out_dtype=MXU_DTYPE, name="l1_do")
    d_wo = _mm(outb, dx3b, mode="tn", out_dtype=F32, name="l1_dwo", tk=2048)
    deltas, dos = _attn_delta(do, outb, dils, name="l1_delta")
    dh1s, d_wqkvT, dqg, dkg = [], [], [], []
    for g, dil in enumerate(dils):
        dqkv_g, a, b_ = _attn_bwd(qkvs[g], rqks[g], dos[g], lses[g], deltas[g], p["at_q_norm"][g * H:(g + 1) * H],
                                  p["at_k_norm"][g * H:(g + 1) * H], grp=g, name=f"l1_attnbwd{g}")
        dqg.append(a)
        dkg.append(b_)
        d_wqkvT.append(_mm(dqkv_g, h1s[g], mode="tn", out_dtype=F32, name=f"l1_dwqkv{g}", tk=2048))
        dh1s.append(_mm(dqkv_g, p["w_qkvT"], mode="nn", out_dtype=MXU_DTYPE, name=f"l1_dh{g}", b_off=g * 3 * D, b_len=3 * D))
    dx2, dx2b, dgm1, _ = _rms_bwd(x2, p["norm_mix"][1:2], dh1s[0:1], dx3, name="l1_rmsbwd",
                                  dh_subs=[(dh1s[g].reshape(dils[g], S // dils[g], D), dils[g]) for g in range(1, len(dils))])
    ex, finish = (None, None) if early_reduce is None else early_reduce(
        dict(w_qkvT=jnp.concatenate(d_wqkvT, axis=0), w_o=d_wo, w_upT=gf1["w_upT"], w_down=gf1["w_down"]))
    dx1, dx1b, cs1, gf0, carried = _ffn_backward(x1, p["norm_ffn"][0:1], p["w_upT"][0], p["w_down"][0], p["ff_dw"][0],
                                                 p["ff_dw_b"][0:1], sv0, dx2, dx2b, 0, exchange=ex)
    reduced = [] if finish is None else [finish(carried)]
    ex, finish = (None, None) if early_reduce is None else early_reduce(dict(w_upT=gf0["w_upT"], w_down=gf0["w_down"]))
    ds = _mm(dx1b, p["w_out"], mode="nt", out_dtype=F32, name="l0_ds")
    d_wout = _mm(s, dx1b, mode="tn", out_dtype=F32, name="l0_dwout", tk=2048)
    dc, d_lng, d_lnb = _cm_ln_bwd(c, ds, p["cm_ln_g"], p["cm_ln_b"], name="l0_lnbwd")
    du, d_cmdw, d_cmdwb, d_bin, *carried = _cm_conv_bwd(dc, u, p["cm_dw"], name="l0_convbwd", exchange=ex)
    if finish is not None:
        reduced.append(finish(carried))
    grad_x, _, dgm0, _ = _mm_post(du, p["w_inT"], _post_rms_bwd(x, p["norm_mix"][0:1], dx1), name="l0_dh")
    d_winT = _mm(du, h0, mode="tn", out_dtype=F32, name="l0_dwin", tk=2048)
    grads = dict(
        norm_mix=jnp.concatenate([dgm0, dgm1], axis=0),
        norm_ffn=jnp.concatenate([gf0["norm"], gf1["norm"]], axis=0),
        w_inT=d_winT, cm_b_in=d_bin, cm_dw=d_cmdw[0:CONV_KERNEL], cm_dw_b=d_cmdwb, cm_ln_g=d_lng, cm_ln_b=d_lnb,
        w_out=d_wout, cm_b_out=cs1,
        w_qkvT=jnp.concatenate(d_wqkvT, axis=0), at_q_norm=jnp.concatenate(dqg, axis=0), at_k_norm=jnp.concatenate(dkg, axis=0),
        w_o=d_wo,
        w_upT=[gf0["w_upT"], gf1["w_upT"]], w_down=[gf0["w_down"], gf1["w_down"]],
        ff_dw=jnp.stack([gf0["dw"], gf1["dw"]]), ff_dw_b=jnp.concatenate([gf0["dw_b"], gf1["dw_b"]], axis=0),
    )
    return loss, grad_x, grads, reduced


_BIG = ("cm_w_in", "cm_w_out", "at_w_qkv", "at_w_out", "ff_w_up", "ff_w_down")
_TRANSPOSED = ("cm_w_in", "at_w_qkv", "ff_w_up")
_SMALL = ("norm_mix", "norm_ffn", "cm_b_in", "cm_dw_b", "cm_ln_g", "cm_ln_b", "cm_b_out", "at_q_norm", "at_k_norm",
          "ff_dw_b", "cm_dw", "ff_dw")
_SMALL_SHARDED = ("cm_dw", "ff_dw")
_ORDER = ("norm_mix", "norm_ffn", "cm_w_in", "cm_b_in", "cm_dw", "cm_dw_b", "cm_ln_g", "cm_ln_b", "cm_w_out", "cm_b_out",
          "at_w_qkv", "at_q_norm", "at_k_norm", "at_w_out", "ff_w_up", "ff_dw", "ff_dw_b", "ff_w_down")


_UNITS = (("cm_w_in", 0), ("cm_w_out", 0), ("ff_w_up", 0), ("ff_w_down", 0),
          ("at_w_qkv", 0), ("at_w_out", 0), ("ff_w_up", 1), ("ff_w_down", 1))
_N_FIRST = 2
_N_LAYER0 = 4


def _unit_rows(t, n, l):
    return t[n].shape[2] if n in _TRANSPOSED else t[n].shape[1]


def _big_rows(t, units=_UNITS):
    return _pack_rows([t[n][l].T if n in _TRANSPOSED else t[n][l] for n, l in units])


def _big_unrows(packed, like):
    mats, off = {}, 0
    for n, l in _UNITS:
        rows = _unit_rows(like, n, l)
        m = packed[off:off + rows]
        off += rows
        mats[(n, l)] = m.T if n in _TRANSPOSED else m
    return {n: jnp.stack([mats[(n, l)] for l in range(like[n].shape[0])]) for n in _BIG}


def kernel(x, norm_mix, norm_ffn, cm_w_in, cm_b_in, cm_dw, cm_dw_b, cm_ln_g, cm_ln_b, cm_w_out, cm_b_out, at_w_qkv, at_q_norm, at_k_norm, at_w_out, ff_w_up, ff_dw, ff_dw_b, ff_w_down, loss_target, m_norm_mix, m_norm_ffn, m_cm_w_in, m_cm_b_in, m_cm_dw, m_cm_dw_b, m_cm_ln_g, m_cm_ln_b, m_cm_w_out, m_cm_b_out, m_at_w_qkv, m_at_q_norm, m_at_k_norm, m_at_w_out, m_ff_w_up, m_ff_dw, m_ff_dw_b, m_ff_w_down, v_norm_mix, v_norm_ffn, v_cm_w_in, v_cm_b_in, v_cm_dw, v_cm_dw_b, v_cm_ln_g, v_cm_ln_b, v_cm_w_out, v_cm_b_out, v_at_w_qkv, v_at_q_norm, v_at_k_norm, v_at_w_out, v_ff_w_up, v_ff_dw, v_ff_dw_b, v_ff_w_down):
    w = dict(norm_mix=norm_mix, norm_ffn=norm_ffn, cm_w_in=cm_w_in, cm_b_in=cm_b_in, cm_dw=cm_dw, cm_dw_b=cm_dw_b, cm_ln_g=cm_ln_g,
             cm_ln_b=cm_ln_b, cm_w_out=cm_w_out, cm_b_out=cm_b_out, at_w_qkv=at_w_qkv, at_q_norm=at_q_norm, at_k_norm=at_k_norm,
             at_w_out=at_w_out, ff_w_up=ff_w_up, ff_dw=ff_dw, ff_dw_b=ff_dw_b, ff_w_down=ff_w_down)
    m = dict(norm_mix=m_norm_mix, norm_ffn=m_norm_ffn, cm_w_in=m_cm_w_in, cm_b_in=m_cm_b_in, cm_dw=m_cm_dw, cm_dw_b=m_cm_dw_b,
             cm_ln_g=m_cm_ln_g, cm_ln_b=m_cm_ln_b, cm_w_out=m_cm_w_out, cm_b_out=m_cm_b_out, at_w_qkv=m_at_w_qkv,
             at_q_norm=m_at_q_norm, at_k_norm=m_at_k_norm, at_w_out=m_at_w_out, ff_w_up=m_ff_w_up, ff_dw=m_ff_dw,
             ff_dw_b=m_ff_dw_b, ff_w_down=m_ff_w_down)
    v = dict(norm_mix=v_norm_mix, norm_ffn=v_norm_ffn, cm_w_in=v_cm_w_in, cm_b_in=v_cm_b_in, cm_dw=v_cm_dw, cm_dw_b=v_cm_dw_b,
             cm_ln_g=v_cm_ln_g, cm_ln_b=v_cm_ln_b, cm_w_out=v_cm_w_out, cm_b_out=v_cm_b_out, at_w_qkv=v_at_w_qkv,
             at_q_norm=v_at_q_norm, at_k_norm=v_at_k_norm, at_w_out=v_at_w_out, ff_w_up=v_ff_w_up, ff_dw=v_ff_dw,
             ff_dw_b=v_ff_dw_b, ff_w_down=v_ff_w_down)
    S, D = x.shape[1], x.shape[2]
    F2 = ff_dw_b.shape[1]
    H3 = at_q_norm.shape[1]
    me = 4 * lax.axis_index("x") + 2 * lax.axis_index("y") + lax.axis_index("c")

    ix, iy, ic = lax.axis_index("x"), lax.axis_index("y"), lax.axis_index("c")
    chip = 2 * ix + iy
    core = jnp.stack([ic]).astype(jnp.int32)
    w_rows = _big_rows(w)
    unit_rows = [_unit_rows(w, n, l) for n, l in _UNITS]
    n_first = sum(unit_rows[:_N_FIRST])
    n_layer0 = sum(unit_rows[:_N_LAYER0])
    w_wire = w_rows.astype(MXU_DTYPE)

    def unpack(gathered, units, rows):
        full, off = {}, 0
        for (n, l), r in zip(units, rows):
            full[(n, l)] = gathered[:, off:off + r, :].reshape(N_DEV * r, D)
            off += r
        out = {}
        if ("cm_w_in", 0) in full:
            out.update(w_inT=full[("cm_w_in", 0)], w_out=full[("cm_w_out", 0)])
        if ("at_w_qkv", 0) in full:
            out.update(w_qkvT=full[("at_w_qkv", 0)], w_o=full[("at_w_out", 0)],
                       w_upT=[_interleave_rows(full[("ff_w_up", l)]) for l in range(2)],
                       w_down=[full[("ff_w_down", l)] for l in range(2)])
        return out

    first = _all_gather(w_wire[:n_first], name="gather_first")
    late_weights = (_gather_exchange(w_wire[n_first:]),
                    lambda carried: unpack(carried[0], _UNITS[_N_FIRST:], unit_rows[_N_FIRST:]))
    small_sh = _flat_pack([cm_dw, ff_dw], D)
    small_g = _all_gather(small_sh, name="gather_small")
    cm_dw_full = jnp.concatenate([_flat_unpack(small_g[j], [cm_dw.shape, ff_dw.shape])[0][0] for j in range(N_DEV)], axis=-1)
    ff_dw_full = jnp.concatenate([_flat_unpack(small_g[j], [cm_dw.shape, ff_dw.shape])[1] for j in range(N_DEV)], axis=-1)

    p = dict(
        norm_mix=norm_mix, norm_ffn=norm_ffn, cm_b_in=cm_b_in, cm_dw=cm_dw_full, cm_dw_b=cm_dw_b, cm_ln_g=cm_ln_g, cm_ln_b=cm_ln_b,
        cm_b_out=cm_b_out, at_q_norm=at_q_norm[0], at_k_norm=at_k_norm[0],
        ff_dw=jnp.stack([_interleave_cols(ff_dw_full[l]) for l in range(ff_dw_full.shape[0])]),
        ff_dw_b=_interleave_cols(ff_dw_b),
        **unpack(first, _UNITS[:_N_FIRST], unit_rows[:_N_FIRST]),
    )

    def pack(pieces):
        return jnp.concatenate([t.reshape(N_DEV, t.shape[0] // N_DEV, D) for t in pieces], axis=1)

    def reduce_start(pieces, tag):
        g_rows = pack(pieces)
        sib = _rs_sibling(g_rows.astype(WIRE_DTYPE), name=f"reduce{tag}_sibling")
        return g_rows, sib, _chip_partials(g_rows, sib, core, name=f"reduce{tag}_add")

    def early_reduce(gd):
        ffn = [_deinterleave_rows(gd["w_upT"]), gd["w_down"]]
        tag, pieces = (2, [gd["w_qkvT"], gd["w_o"]] + ffn) if "w_qkvT" in gd else (1, ffn)
        g_rows, sib, part = reduce_start(pieces, tag)
        return _chips_exchange(part), lambda carried: (g_rows, sib, carried[0])

    loss8, grad_x, g, (reduced2, reduced1) = _local_step(x[0], loss_target[0], p, late_weights, early_reduce)
    loss = lax.psum(loss8[0, 0], ("x", "y", "c"))
    g_rows0, sib0, part0 = reduce_start([g["w_inT"], g["w_out"]], 0)
    reduced0 = (g_rows0, sib0, _rs_chips(part0, name="reduce0_chips"))
    slots = jnp.stack([me, chip, 0 * me, 0 * me + 1, 0 * me + 2]).astype(jnp.int32)
    m_rows, v_rows = _big_rows(m), _big_rows(v)
    updated = []
    for tag, (g_rows, sib, recv), rows in ((0, reduced0, slice(0, n_first)), (1, reduced1, slice(n_first, n_layer0)),
                                           (2, reduced2, slice(n_layer0, None))):
        updated.append(_adamw(w_rows[rows], m_rows[rows], v_rows[rows], [g_rows, sib, recv, recv, recv], slots, name=f"adamw_big{tag}"))
    big = [_big_unrows(jnp.concatenate([u[k] for u in updated], axis=0), w) for k in range(4)]

    g_small = dict(g)
    g_small["cm_b_in"] = g["cm_b_in"]
    g_small["at_q_norm"] = g["at_q_norm"][None]
    g_small["at_k_norm"] = g["at_k_norm"][None]
    g_small["ff_dw_b"] = _deinterleave_cols(g["ff_dw_b"])
    g_small["cm_dw"] = g["cm_dw"][None]
    g_small["ff_dw"] = jnp.stack([_deinterleave_cols(g["ff_dw"][l]) for l in range(g["ff_dw"].shape[0])])
    small_shapes = [g_small[n].shape for n in _SMALL]
    gs_parts = _all_gather(_flat_pack([g_small[n] for n in _SMALL], D), name="gather_small_grads")

    def embed(t, n):
        if n not in _SMALL_SHARDED:
            return t
        full_shape = t.shape[:-1] + (t.shape[-1] * N_DEV,)
        return lax.dynamic_update_slice_in_dim(jnp.zeros(full_shape, F32), t, me * t.shape[-1], axis=t.ndim - 1)

    packs = [_flat_pack([embed(tree[n], n) for n in _SMALL], D) for tree in (w, m, v)]
    gs, ds_, ms, vs = _adamw(packs[0], packs[1], packs[2], [gs_parts] * N_DEV, jnp.arange(N_DEV, dtype=jnp.int32),
                             name="adamw_small")
    small = []
    for t in (gs, ds_, ms, vs):
        un = dict(zip(_SMALL, _flat_unpack(t, small_shapes)))
        for n in _SMALL_SHARDED:
            width = w[n].shape[-1]
            un[n] = lax.dynamic_slice_in_dim(un[n], me * width, width, axis=un[n].ndim - 1)
        small.append({n: un[n].reshape(w[n].shape) for n in _SMALL})

    outs = [loss, grad_x[None]]
    for k in range(4):
        for n in _ORDER:
            outs.append(big[k][n] if n in _BIG else small[k][n])
    return tuple(outs)
```

```python
import functools

import jax
import jax.numpy as jnp
import numpy as np
from jax import lax
from jax.experimental import pallas as pl
from jax.experimental.pallas import tpu as pltpu

F32 = jnp.float32
MXU_DTYPE = jnp.bfloat16
WIRE_DTYPE = jnp.bfloat16
EPS = 1e-6
NEG = -1e30
HEAD_DIM = 128
BLOCK = 128
DILATED_GROUPS = ((128, 1), (512, 4), (2048, 16))
ALIBI_MAX = 8.0
CONV_KERNEL = 31
CONV_HALO = 32
CONV_ROWS = 64
FFN_KERNEL = 3
FFN_HALO = 16
FFN_ROWS = 64
ADAM_LR, ADAM_B1, ADAM_B2, ADAM_EPS, ADAM_WD, ADAM_STEP = 0.001, 0.9, 0.999, 1e-08, 0.01, 10
V7X_VMEM_BYTES = 64 * 1024 * 1024
VMEM_LIMIT = V7X_VMEM_BYTES * 3 // 4
N_DEV = 8
MESH = pl.DeviceIdType.MESH


def _pick(n, target, align):
    if n <= target:
        return n
    best = None
    for t in range(align, target + 1, align):
        if n % t == 0:
            best = t
    assert best is not None, (n, target, align)
    return best


def _params(*sem, vmem=VMEM_LIMIT):
    return pltpu.CompilerParams(dimension_semantics=sem, vmem_limit_bytes=vmem)


def _sigmoid(x):
    return 1.0 / (1.0 + jnp.exp(-x))


_DIMS = {"nn": ((1,), (0,)), "nt": ((1,), (1,)), "tn": ((0,), (0,))}


def _mm(a, b, *, mode, out_dtype, name, tm=1024, tn=1024, tk=None, bias=None, residual=None, b_off=0, b_len=None,
        post=None, keep_main=True):
    if mode == "tn":
        K, M = a.shape
    else:
        M, K = a.shape
    if mode == "nt":
        N = b.shape[0] if b_len is None else b_len
    else:
        N = b.shape[1]
    if b_len is not None:
        assert mode == "nt" or (mode == "nn" and K == b_len)
    tm = _pick(M, tm, 128 if mode == "tn" else 16)
    tn = _pick(N, tn, 128)
    tk = K if tk is None else _pick(K, tk, 128 if mode != "tn" else 16)
    nk = K // tk
    unit = tn if mode == "nt" else tk
    assert b_off % unit == 0
    kb0 = b_off // unit
    if mode == "tn":
        a_spec = pl.BlockSpec((tk, tm), lambda i, j, k: (k, i))
    else:
        a_spec = pl.BlockSpec((tm, tk), lambda i, j, k: (i, k))
    if mode == "nt":
        b_spec = pl.BlockSpec((tn, tk), lambda i, j, k: (j + kb0, k))
    else:
        b_spec = pl.BlockSpec((tk, tn), lambda i, j, k: (k + kb0, j))
    in_specs = [a_spec, b_spec]
    args = [a, b]
    if bias is not None:
        in_specs.append(pl.BlockSpec((1, tn), lambda i, j, k: (0, j)))
        args.append(bias)
    if residual is not None:
        in_specs.append(pl.BlockSpec((tm, tn), lambda i, j, k: (i, j)))
        args.append(residual)
    has_bias, has_res = bias is not None, residual is not None
    kinds = {"tile": ((tm, tn), lambda i, j, k: (i, j)), "row": ((1, tn), lambda i, j, k: (0, j)),
             "lanes": ((8, 128), lambda i, j, k: (0, 0))}
    post_in = [] if post is None else post.ins
    post_out = [] if post is None else post.outs
    if post is not None:
        assert tn == N
        for arr, kind in post_in:
            in_specs.append(pl.BlockSpec(*kinds[kind]))
            args.append(arr)
    out_specs = [pl.BlockSpec((tm, tn), lambda i, j, k: (i, j))] if keep_main else []
    out_shape = [jax.ShapeDtypeStruct((M, N), out_dtype)] if keep_main else []
    for kind, dtype in post_out:
        out_specs.append(pl.BlockSpec(*kinds[kind]))
        out_shape.append(jax.ShapeDtypeStruct({"tile": (M, N), "row": (1, N), "lanes": (8, 128)}[kind], dtype))
    accumulates = any(kind != "tile" for kind, _ in post_out)

    def body(*refs):
        a_ref, b_ref = refs[0], refs[1]
        pos = 2
        bias_ref = res_ref = None
        if has_bias:
            bias_ref = refs[pos]
            pos += 1
        if has_res:
            res_ref = refs[pos]
            pos += 1
        pin_refs = refs[pos:pos + len(post_in)]
        pos += len(post_in)
        o_ref = refs[pos] if keep_main else None
        pos += 1 if keep_main else 0
        pout_refs = refs[pos:pos + len(post_out)]
        pos += len(post_out)
        acc_ref = refs[pos] if nk > 1 else None

        def finish(acc):
            if has_bias:
                acc = acc + bias_ref[...]
            if has_res:
                acc = acc + res_ref[...]
            if keep_main:
                o_ref[...] = acc.astype(o_ref.dtype)
            if post is not None:
                post.fn(acc, pin_refs, pout_refs, pl.program_id(0) == 0)

        part = lax.dot_general(a_ref[...].astype(MXU_DTYPE), b_ref[...].astype(MXU_DTYPE), (_DIMS[mode], ((), ())),
                               preferred_element_type=F32)
        if nk == 1:
            finish(part)
        else:
            k = pl.program_id(2)

            @pl.when(k == 0)
            def _():
                acc_ref[...] = part

            @pl.when(jnp.logical_and(k > 0, k < nk - 1))
            def _():
                acc_ref[...] += part

            @pl.when(k == nk - 1)
            def _():
                finish(acc_ref[...] + part)

    outs = pl.pallas_call(
        body, name=name, grid=(M // tm, N // tn, nk), in_specs=in_specs, out_specs=out_specs, out_shape=out_shape,
        scratch_shapes=[pltpu.VMEM((tm, tn), F32)] if nk > 1 else [],
        compiler_params=_params("arbitrary" if accumulates else "parallel", "parallel", "arbitrary"),
    )(*args)
    return outs[0] if post is None else outs


def _mm_post(a, b, post, *, name, tm=512, rows=256):
    M, K = a.shape
    N = b.shape[1]
    tm = _pick(M, tm, 16)
    rows = _pick(tm, rows, 16)
    kinds = {"tile": ((tm, N), lambda i: (i, 0)), "row": ((1, N), lambda i: (0, 0)), "lanes": ((8, 128), lambda i: (0, 0))}
    n_in = len(post.ins)

    def body(a_ref, b_ref, *rest):
        first = pl.program_id(0) == 0

        def product(c):
            return jnp.dot(a_ref[c * rows:(c + 1) * rows, :].astype(MXU_DTYPE), b_ref[...].astype(MXU_DTYPE),
                           preferred_element_type=F32)

        def chunk(refs, specs, c):
            return [r.at[pl.ds(c * rows, rows), :] if kind == "tile" else r for r, kind in zip(refs, specs)]

        acc = product(0)
        for c in range(tm // rows):
            nxt = product(c + 1) if c + 1 < tm // rows else None
            post.fn(acc, chunk(rest[:n_in], [k for _, k in post.ins], c), chunk(rest[n_in:], [k for k, _ in post.outs], c),
                    jnp.logical_and(first, c == 0))
            acc = nxt

    return pl.pallas_call(
        body, name=name, grid=(M // tm,),
        in_specs=[pl.BlockSpec((tm, K), lambda i: (i, 0)), pl.BlockSpec((K, N), lambda i: (0, 0), pipeline_mode=pl.Buffered(1))]
        + [pl.BlockSpec(*kinds[kind]) for _, kind in post.ins],
        out_specs=[pl.BlockSpec(*kinds[kind]) for kind, _ in post.outs],
        out_shape=[jax.ShapeDtypeStruct({"tile": (M, N), "row": (1, N), "lanes": (8, 128)}[kind], dtype) for kind, dtype in post.outs],
        compiler_params=_params("arbitrary", vmem=V7X_VMEM_BYTES * 7 // 8),
    )(a, b, *[arr for arr, _ in post.ins])


class _Post:
    def __init__(self, ins, outs, fn):
        self.ins, self.outs, self.fn = ins, outs, fn


def _accumulate(ref, value, first):
    @pl.when(first)
    def _():
        ref[...] = value

    @pl.when(jnp.logical_not(first))
    def _():
        ref[...] += value


def _post_rms(g):
    def fn(acc, ins, outs, first):
        r = lax.rsqrt(jnp.mean(acc * acc, axis=-1, keepdims=True) + EPS)
        outs[0][...] = (acc * r * ins[0][...]).astype(outs[0].dtype)

    return _Post([(g, "row")], [("tile", MXU_DTYPE)], fn)


def _post_loss(target):
    def fn(acc, ins, outs, first):
        e = acc - ins[0][...]
        dy = e * (1.0 / acc.shape[-1])
        outs[0][...] = dy
        outs[1][...] = dy.astype(outs[1].dtype)
        part = jnp.sum(jnp.sum(e * e, axis=0, keepdims=True), axis=1, keepdims=True) * (0.5 / acc.shape[-1])
        _accumulate(outs[2], jnp.broadcast_to(part, outs[2].shape), first)

    return _Post([(target, "tile")], [("tile", F32), ("tile", MXU_DTYPE), ("lanes", F32)], fn)


def _post_ln_silu_bwd(c, ln_g, ln_b):
    def fn(acc, ins, outs, first):
        cv = ins[0][...]
        xc = cv - jnp.mean(cv, axis=-1, keepdims=True)
        rstd = lax.rsqrt(jnp.mean(xc * xc, axis=-1, keepdims=True) + EPS)
        xh = xc * rstd
        y = xh * ins[1][...] + ins[2][...]
        sg = _sigmoid(y)
        dy = acc * (sg * (1.0 + y * (1.0 - sg)))
        gy = dy * ins[1][...]
        outs[0][...] = rstd * (gy - jnp.mean(gy, axis=-1, keepdims=True) - xh * jnp.mean(gy * xh, axis=-1, keepdims=True))
        _accumulate(outs[1], jnp.sum(dy * xh, axis=0, keepdims=True), first)
        _accumulate(outs[2], jnp.sum(dy, axis=0, keepdims=True), first)

    return _Post([(c, "tile"), (ln_g, "row"), (ln_b, "row")], [("tile", F32), ("row", F32), ("row", F32)], fn)


def _post_rms_bwd(x, g, dres):
    def fn(acc, ins, outs, first):
        xv = ins[0][...]
        r = lax.rsqrt(jnp.mean(xv * xv, axis=-1, keepdims=True) + EPS)
        xh = xv * r
        gy = acc * ins[1][...]
        dx = r * (gy - xh * jnp.mean(gy * xh, axis=-1, keepdims=True)) + ins[2][...]
        outs[0][...] = dx
        outs[1][...] = dx.astype(outs[1].dtype)
        _accumulate(outs[2], jnp.sum(acc * xh, axis=0, keepdims=True), first)
        _accumulate(outs[3], jnp.sum(dx, axis=0, keepdims=True), first)

    return _Post([(x, "tile"), (g, "row"), (dres, "tile")], [("tile", F32), ("tile", MXU_DTYPE), ("row", F32), ("row", F32)], fn)


SUB_TILE = 512


def _sub_spec(dil, ts, cols):
    return pl.BlockSpec((dil, ts // dil, cols), lambda i: (0, i, 0))


def _tok_to_sub(tok_ref, dst_ref, dil):
    nc, ts, _ = tok_ref.shape
    for c in range(nc):
        for r in range(dil):
            dst_ref[r, :, c * 128:(c + 1) * 128] = tok_ref.at[c][pl.ds(r, ts // dil, stride=dil), :].astype(dst_ref.dtype)


def _sub_to_tok(src_ref, tok_ref, dil):
    nc, ts, _ = tok_ref.shape
    for c in range(nc):
        for r in range(dil):
            tok_ref.at[c][pl.ds(r, ts // dil, stride=dil), :] = src_ref[r, :, c * 128:(c + 1) * 128].astype(F32)


def _rms_fwd(x, g, *, name, subs=()):
    S, D = x.shape
    ts = _pick(S, SUB_TILE, 16 * max(subs, default=1))
    NC = D // 128

    def body(x_ref, g_ref, h_ref, *rest):
        xv = x_ref[...]
        r = lax.rsqrt(jnp.mean(xv * xv, axis=-1, keepdims=True) + EPS)
        h = xv * r * g_ref[...]
        h_ref[...] = h.astype(h_ref.dtype)
        if subs:
            tok_ref = rest[-1]
            for c in range(NC):
                tok_ref[c] = h[:, c * 128:(c + 1) * 128]
            for dil, dst_ref in zip(subs, rest):
                _tok_to_sub(tok_ref, dst_ref, dil)

    row = pl.BlockSpec((ts, D), lambda i: (i, 0))
    outs = pl.pallas_call(
        body, name=name, grid=(S // ts,),
        in_specs=[row, pl.BlockSpec((1, D), lambda i: (0, 0))],
        out_specs=[row] + [_sub_spec(dil, ts, D) for dil in subs],
        out_shape=[jax.ShapeDtypeStruct((S, D), MXU_DTYPE)] + [jax.ShapeDtypeStruct((dil, S // dil, D), MXU_DTYPE) for dil in subs],
        scratch_shapes=[pltpu.VMEM((NC, ts, 128), F32)] if subs else [],
        compiler_params=_params("parallel"),
    )(x, g)
    return outs if subs else outs[0]


def _rms_bwd(x, g, dhs, dres, *, name, dh_subs=()):
    S, D = x.shape
    ts = _pick(S, SUB_TILE, 16 * max([dil for _, dil in dh_subs], default=1))
    n_dh, n_sub = len(dhs), len(dh_subs)
    NC = D // 128

    def body(*refs):
        x_ref, g_ref = refs[0], refs[1]
        dh_refs = refs[2:2 + n_dh]
        sub_refs = refs[2 + n_dh:2 + n_dh + n_sub]
        dres_ref, dx_ref, dxb_ref, dg_ref, cs_ref = refs[2 + n_dh + n_sub:7 + n_dh + n_sub]
        i = pl.program_id(0)
        xv = x_ref[...]
        r = lax.rsqrt(jnp.mean(xv * xv, axis=-1, keepdims=True) + EPS)
        xh = xv * r
        dhv = dh_refs[0][...].astype(F32)
        for t in dh_refs[1:]:
            dhv = dhv + t[...].astype(F32)
        for (_, dil), sub_ref in zip(dh_subs, sub_refs):
            tok_ref = refs[-1]
            _sub_to_tok(sub_ref, tok_ref, dil)
            dhv = dhv + jnp.concatenate([tok_ref[c] for c in range(NC)], axis=1)
        gy = dhv * g_ref[...]
        dx = r * (gy - xh * jnp.mean(gy * xh, axis=-1, keepdims=True)) + dres_ref[...]
        dx_ref[...] = dx
        dxb_ref[...] = dx.astype(dxb_ref.dtype)
        dg = jnp.sum(dhv * xh, axis=0, keepdims=True)
        cs = jnp.sum(dx, axis=0, keepdims=True)

        @pl.when(i == 0)
        def _():
            dg_ref[...] = dg
            cs_ref[...] = cs

        @pl.when(i > 0)
        def _():
            dg_ref[...] += dg
            cs_ref[...] += cs

    row = pl.BlockSpec((ts, D), lambda i: (i, 0))
    vec = pl.BlockSpec((1, D), lambda i: (0, 0))
    return pl.pallas_call(
        body, name=name, grid=(S // ts,),
        in_specs=[row, vec] + [row] * n_dh + [_sub_spec(dil, ts, D) for _, dil in dh_subs] + [row],
        out_specs=[row, row, vec, vec],
        out_shape=[jax.ShapeDtypeStruct((S, D), F32), jax.ShapeDtypeStruct((S, D), MXU_DTYPE),
                   jax.ShapeDtypeStruct((1, D), F32), jax.ShapeDtypeStruct((1, D), F32)],
        scratch_shapes=[pltpu.VMEM((NC, ts, 128), F32)] if n_sub else [],
        compiler_params=_params("arbitrary"),
    )(x, g, *dhs, *[a for a, _ in dh_subs], dres)


def _conv_phases(ph_ref, ts):
    n = ts + CONV_HALO - 8
    for b in range(1, 8):
        ph_ref[b, 0:n, :] = ph_ref[0, pl.ds(b, n), :]


def _phase_taps(base, step=1):
    groups = {}
    for k in range(CONV_KERNEL):
        a, b = divmod(base + step * k, 8)
        groups.setdefault(b, []).append((a, k))
    out = []
    for b in sorted(groups):
        ak = sorted(groups[b])
        assert [a for a, _ in ak] == list(range(ak[0][0], ak[0][0] + len(ak)))
        out.append((b, ak[0][0], [k for _, k in ak]))
    return out


def _cm_fwd(u, dw, dw_b, ln_g, ln_b, *, name, exchange=None):
    S, D2 = u.shape
    D = D2 // 2
    ts = _pick(S, 256, CONV_HALO)
    hb = ts // CONV_HALO
    ex_in, ex_out, ex_scr = ([], [], []) if exchange is None else (exchange.operands, exchange.out_shapes, exchange.scratch)

    def body(u_ref, up_ref, dw_ref, dwb_ref, g_ref, b_ref, *rest):
        xi = rest[:len(ex_in)]
        c_ref, s_ref = rest[len(ex_in):len(ex_in) + 2]
        xo = rest[len(ex_in) + 2:len(ex_in) + 2 + len(ex_out)]
        ext_ref = rest[len(ex_in) + 2 + len(ex_out)]
        i = pl.program_id(0)
        if exchange is not None:
            exchange.emit(i, S // ts, xi, xo, rest[len(ex_in) + 3 + len(ex_out):])
        prev = up_ref[:, :D] * _sigmoid(up_ref[:, D:])
        ext_ref[0:CONV_HALO, :] = jnp.where(i > 0, prev, 0.0)
        ext_ref[CONV_HALO:CONV_HALO + ts, :] = u_ref[:, :D] * _sigmoid(u_ref[:, D:])
        for cc in range(D // 128):
            sl = slice(cc * 128, (cc + 1) * 128)
            acc = jnp.zeros((ts, 128), F32) + dwb_ref[:, sl]
            for k in range(CONV_KERNEL):
                acc = acc + dw_ref[k:k + 1, sl] * ext_ref[pl.ds(CONV_HALO - (CONV_KERNEL - 1) + k, ts), sl]
            c_ref[:, sl] = acc
        c = c_ref[...]
        mu = jnp.mean(c, axis=-1, keepdims=True)
        xc = c - mu
        rstd = lax.rsqrt(jnp.mean(xc * xc, axis=-1, keepdims=True) + EPS)
        y = xc * rstd * g_ref[...] + b_ref[...]
        s_ref[...] = (y * _sigmoid(y)).astype(s_ref.dtype)

    vec = pl.BlockSpec((1, D), lambda i: (0, 0))
    return pl.pallas_call(
        body, name=name, grid=(S // ts,),
        in_specs=[pl.BlockSpec((ts, D2), lambda i: (i, 0)),
                  pl.BlockSpec((CONV_HALO, D2), lambda i: (jnp.maximum(i * hb - 1, 0), 0)),
                  pl.BlockSpec((CONV_KERNEL, D), lambda i: (0, 0)), vec, vec, vec] + [_ANY] * len(ex_in),
        out_specs=[pl.BlockSpec((ts, D), lambda i: (i, 0)), pl.BlockSpec((ts, D), lambda i: (i, 0))] + [_ANY] * len(ex_out),
        out_shape=[jax.ShapeDtypeStruct((S, D), F32), jax.ShapeDtypeStruct((S, D), MXU_DTYPE)] + list(ex_out),
        scratch_shapes=[pltpu.VMEM((ts + CONV_HALO, D), F32)] + list(ex_scr),
        compiler_params=_params("parallel" if exchange is None else "arbitrary"),
    )(u, u, dw, dw_b, ln_g, ln_b, *ex_in)


def _cm_conv_bwd(dc, u, dw, *, name, exchange=None):
    ex_in, ex_out, ex_scr = ([], [], []) if exchange is None else (exchange.operands, exchange.out_shapes, exchange.scratch)
    S, D2 = u.shape
    D = D2 // 2
    ts = _pick(S, 256, CONV_HALO)
    hb = ts // CONV_HALO
    n_t = S // ts
    last_h = S // CONV_HALO - 1

    rc = _pick(ts, CONV_ROWS, 8)

    def fold8(v):
        out = v[0:8]
        for j in range(1, v.shape[0] // 8):
            out = out + v[8 * j:8 * j + 8]
        return out

    def body(dc_ref, dcn_ref, u_ref, up_ref, dw_ref, *rest):
        xi = rest[:len(ex_in)]
        du_ref, ddw_ref, ddwb_ref, dbin_ref = rest[len(ex_in):len(ex_in) + 4]
        xo = rest[len(ex_in) + 4:len(ex_in) + 4 + len(ex_out)]
        dph_ref, gph_ref, dgl_ref = rest[len(ex_in) + 4 + len(ex_out):len(ex_in) + 7 + len(ex_out)]
        i = pl.program_id(0)
        if exchange is not None:
            exchange.emit(i, n_t, xi, xo, rest[len(ex_in) + 7 + len(ex_out):])
        dph_ref[0, 0:ts, :] = dc_ref[...]
        dph_ref[0, ts:ts + CONV_HALO, :] = jnp.where(i < n_t - 1, dcn_ref[...], 0.0)
        prev = up_ref[:, :D] * _sigmoid(up_ref[:, D:])
        gph_ref[0, 0:CONV_HALO, :] = jnp.where(i > 0, prev, 0.0)
        gph_ref[0, CONV_HALO:CONV_HALO + ts, :] = u_ref[:, :D] * _sigmoid(u_ref[:, D:])
        _conv_phases(dph_ref, ts)
        _conv_phases(gph_ref, ts)

        @pl.when(i == 0)
        def _():
            ddw_ref[...] = jnp.zeros_like(ddw_ref)
            ddwb_ref[...] = jnp.zeros_like(ddwb_ref)
            dbin_ref[...] = jnp.zeros_like(dbin_ref)

        for cc in range(D // 128):
            sl = slice(cc * 128, (cc + 1) * 128)
            sl2 = slice(D + cc * 128, D + (cc + 1) * 128)
            wk = [dw_ref[k:k + 1, sl] for k in range(CONV_KERNEL)]
            acc_a, acc_g = jnp.zeros((8, 128), F32), jnp.zeros((8, 128), F32)
            dgl = jnp.zeros((ts, 128), F32)
            for b, a0, taps in _phase_taps(CONV_KERNEL - 1, -1):
                for j, k in enumerate(taps):
                    dgl = dgl + wk[k] * dph_ref[b, 8 * (a0 + j):8 * (a0 + j) + ts, sl]
            dgl_ref[...] = dgl
            for r0 in range(0, ts, rc):
                dglu = dgl_ref[r0:r0 + rc, :]
                av = u_ref[r0:r0 + rc, sl]
                sg = _sigmoid(u_ref[r0:r0 + rc, sl2])
                da = dglu * sg
                dg = dglu * av * sg * (1.0 - sg)
                du_ref[r0:r0 + rc, sl] = da.astype(du_ref.dtype)
                du_ref[r0:r0 + rc, sl2] = dg.astype(du_ref.dtype)
                acc_a = acc_a + fold8(da)
                acc_g = acc_g + fold8(dg)
            dbin_ref[:, sl] += jnp.sum(acc_a, axis=0, keepdims=True)
            dbin_ref[:, sl2] += jnp.sum(acc_g, axis=0, keepdims=True)
            for gi, (b, a0, taps) in enumerate(_phase_taps(CONV_HALO - (CONV_KERNEL - 1))):
                accs = [jnp.zeros((8, 128), F32) for _ in taps]
                accb = jnp.zeros((8, 128), F32)
                for r0 in range(0, ts, rc):
                    dcc = dph_ref[0, r0:r0 + rc, sl]
                    win = gph_ref[b, 8 * a0 + r0:8 * (a0 + len(taps) - 1) + r0 + rc, sl]
                    for j in range(len(taps)):
                        accs[j] = accs[j] + fold8(dcc * win[8 * j:8 * j + rc])
                    if gi == 0:
                        accb = accb + fold8(dcc)
                for j, k in enumerate(taps):
                    ddw_ref[k:k + 1, sl] += jnp.sum(accs[j], axis=0, keepdims=True)
                if gi == 0:
                    ddwb_ref[:, sl] += jnp.sum(accb, axis=0, keepdims=True)

    return pl.pallas_call(
        body, name=name, grid=(n_t,),
        in_specs=[pl.BlockSpec((ts, D), lambda i: (i, 0)),
                  pl.BlockSpec((CONV_HALO, D), lambda i: (jnp.minimum((i + 1) * hb, last_h), 0)),
                  pl.BlockSpec((ts, D2), lambda i: (i, 0)),
                  pl.BlockSpec((CONV_HALO, D2), lambda i: (jnp.maximum(i * hb - 1, 0), 0)),
                  pl.BlockSpec((CONV_KERNEL, D), lambda i: (0, 0))] + [_ANY] * len(ex_in),
        out_specs=[pl.BlockSpec((ts, D2), lambda i: (i, 0)), pl.BlockSpec((CONV_HALO, D), lambda i: (0, 0)),
                   pl.BlockSpec((1, D), lambda i: (0, 0)), pl.BlockSpec((1, D2), lambda i: (0, 0))] + [_ANY] * len(ex_out),
        out_shape=[jax.ShapeDtypeStruct((S, D2), MXU_DTYPE), jax.ShapeDtypeStruct((CONV_HALO, D), F32),
                   jax.ShapeDtypeStruct((1, D), F32), jax.ShapeDtypeStruct((1, D2), F32)] + list(ex_out),
        scratch_shapes=[pltpu.VMEM((8, ts + CONV_HALO, D), F32), pltpu.VMEM((8, ts + CONV_HALO, D), F32),
                        pltpu.VMEM((ts, 128), F32)] + list(ex_scr),
        compiler_params=_params("arbitrary"),
    )(dc, dc, u, u, dw, *ex_in)


def _ffn_cols(F2):
    return _pick(F2, 1024, 256)


def _ffn_up_act(hf, wupT, dw, dw_b, *, name):
    S, D = hf.shape
    F2 = wupT.shape[0]
    ts = _pick(S, 512, 16)
    tc = _ffn_cols(F2)
    n_ct = F2 // tc
    hb = ts // FFN_HALO
    rc = _pick(ts, FFN_ROWS, 16)

    def body(h_ref, hp_ref, wt_ref, w_ref, b_ref, up_ref, a_ref, he_ref, ext_ref):
        i = pl.program_id(0)
        he_ref[0:FFN_HALO, :] = hp_ref[...]
        he_ref[FFN_HALO:FFN_HALO + ts, :] = h_ref[...]

        def product(j):
            return lax.dot_general(he_ref[...].astype(MXU_DTYPE), wt_ref[j * tc:(j + 1) * tc, :].astype(MXU_DTYPE),
                                   (_DIMS["nt"], ((), ())), preferred_element_type=F32)

        def conv_gate(j, res):
            ext = ext_ref.at[j % 2]
            upv = res.astype(up_ref.dtype)
            up_ref[:, j * tc:(j + 1) * tc] = upv[FFN_HALO:FFN_HALO + ts]
            ext[0:FFN_HALO, :] = jnp.where(i > 0, upv[0:FFN_HALO].astype(F32), 0.0)
            ext[FFN_HALO:FFN_HALO + ts, :] = upv[FFN_HALO:FFN_HALO + ts].astype(F32)
            for q in range(tc // 256):
                sls = [slice(q * 256 + half * 128, q * 256 + half * 128 + 128) for half in range(2)]
                gls = [slice(j * tc + sl.start, j * tc + sl.stop) for sl in sls]
                wk = [[w_ref[k:k + 1, gl] for k in range(FFN_KERNEL)] for gl in gls]
                bb = [b_ref[:, gl] for gl in gls]
                for r0 in range(0, ts, rc):
                    gt, vl = [bb[h] + sum(wk[h][k] * ext[pl.ds(FFN_HALO + r0 - 2 + k, rc), sls[h]] for k in range(FFN_KERNEL))
                              for h in range(2)]
                    a_ref[r0:r0 + rc, j * (tc // 2) + q * 128:j * (tc // 2) + (q + 1) * 128] = (gt * _sigmoid(gt) * vl).astype(a_ref.dtype)

        res = product(0)
        for j in range(n_ct):
            nxt = product(j + 1) if j + 1 < n_ct else None
            conv_gate(j, res)
            res = nxt

    return pl.pallas_call(
        body, name=name, grid=(S // ts,),
        in_specs=[pl.BlockSpec((ts, D), lambda i: (i, 0)),
                  pl.BlockSpec((FFN_HALO, D), lambda i: (jnp.maximum(i * hb - 1, 0), 0)),
                  pl.BlockSpec((F2, D), lambda i: (0, 0), pipeline_mode=pl.Buffered(1)),
                  pl.BlockSpec((FFN_KERNEL, F2), lambda i: (0, 0)), pl.BlockSpec((1, F2), lambda i: (0, 0))],
        out_specs=[pl.BlockSpec((ts, F2), lambda i: (i, 0)), pl.BlockSpec((ts, F2 // 2), lambda i: (i, 0))],
        out_shape=[jax.ShapeDtypeStruct((S, F2), MXU_DTYPE), jax.ShapeDtypeStruct((S, F2 // 2), MXU_DTYPE)],
        scratch_shapes=[pltpu.VMEM((ts + FFN_HALO, D), hf.dtype), pltpu.VMEM((2, ts + FFN_HALO, tc), F32)],
        compiler_params=_params("parallel", vmem=V7X_VMEM_BYTES * 7 // 8),
    )(hf, hf, wupT, dw, dw_b)


def _ffn_act_bwd(up, dact, dw, dw_b, *, name, exchange=None):
    ex_in, ex_out, ex_scr = ([], [], []) if exchange is None else (exchange.operands, exchange.out_shapes, exchange.scratch)
    S, F2 = up.shape
    ts = _pick(S, 512, 16)
    tc = _ffn_cols(F2)
    hb = ts // FFN_HALO
    n_t = S // ts
    last_h = S // FFN_HALO - 1
    E = ts + FFN_HALO

    rc = _pick(ts, FFN_ROWS, 16)

    def fold8(v):
        out = v[0:8]
        for j in range(1, v.shape[0] // 8):
            out = out + v[8 * j:8 * j + 8]
        return out

    def body(u_ref, up_ref, un_ref, da_ref, dan_ref, w_ref, b_ref, *rest):
        xi = rest[:len(ex_in)]
        dup_ref, ddw_ref, ddb_ref = rest[len(ex_in):len(ex_in) + 3]
        xo = rest[len(ex_in) + 3:len(ex_in) + 3 + len(ex_out)]
        ue_ref, dcv_ref = rest[len(ex_in) + 3 + len(ex_out):len(ex_in) + 5 + len(ex_out)]
        i = pl.program_id(1)
        if exchange is not None:
            exchange.emit(pl.program_id(0) * n_t + i, (F2 // tc) * n_t, xi, xo, rest[len(ex_in) + 5 + len(ex_out):])
        ue_ref[0:FFN_HALO, :] = jnp.where(i > 0, up_ref[...].astype(F32), 0.0)
        ue_ref[FFN_HALO:FFN_HALO + ts, :] = u_ref[...].astype(F32)
        ue_ref[FFN_HALO + ts:FFN_HALO + ts + FFN_HALO, :] = jnp.where(i < n_t - 1, un_ref[...].astype(F32), 0.0)

        @pl.when(i == 0)
        def _():
            ddw_ref[...] = jnp.zeros_like(ddw_ref)
            ddb_ref[...] = jnp.zeros_like(ddb_ref)

        for q in range(tc // 256):
            sls = [slice(q * 256 + half * 128, q * 256 + half * 128 + 128) for half in range(2)]
            qs = slice(q * 128, (q + 1) * 128)
            wk = [[w_ref[k:k + 1, sl] for k in range(FFN_KERNEL)] for sl in sls]
            bb = [b_ref[:, sl] for sl in sls]
            acc = [[jnp.zeros((8, 128), F32) for _ in range(FFN_KERNEL)] for _ in range(2)]
            accb = [jnp.zeros((8, 128), F32) for _ in range(2)]
            for r0, rows in [(r, rc) for r in range(0, ts, rc)] + [(ts, FFN_HALO)]:
                xs = [[ue_ref[pl.ds(FFN_HALO + r0 - 2 + k, rows), sls[h]] for k in range(FFN_KERNEL)] for h in range(2)]
                gt, vl = [bb[h] + sum(wk[h][k] * xs[h][k] for k in range(FFN_KERNEL)) for h in range(2)]
                sg = _sigmoid(gt)
                if r0 < ts:
                    dae = da_ref[r0:r0 + rows, qs].astype(F32)
                else:
                    dae = jnp.where(i < n_t - 1, dan_ref[:, qs].astype(F32), 0.0)
                dcv = [dae * vl * (sg * (1.0 + gt * (1.0 - sg))), dae * (gt * sg)]
                for h in range(2):
                    dcv_ref[r0:r0 + rows, sls[h]] = dcv[h]
                    if r0 < ts:
                        for k in range(FFN_KERNEL):
                            acc[h][k] = acc[h][k] + fold8(dcv[h] * xs[h][k])
                        accb[h] = accb[h] + fold8(dcv[h])
            for r0 in range(0, ts, rc):
                for h in range(2):
                    dup = sum(wk[h][2 - j] * dcv_ref[pl.ds(r0 + j, rc), sls[h]] for j in range(FFN_KERNEL))
                    dup_ref[r0:r0 + rc, sls[h]] = dup.astype(dup_ref.dtype)
            for h in range(2):
                for k in range(FFN_KERNEL):
                    ddw_ref[k:k + 1, sls[h]] += jnp.sum(acc[h][k], axis=0, keepdims=True)
                ddb_ref[:, sls[h]] += jnp.sum(accb[h], axis=0, keepdims=True)

    return pl.pallas_call(
        body, name=name, grid=(F2 // tc, n_t),
        in_specs=[pl.BlockSpec((ts, tc), lambda j, i: (i, j)),
                  pl.BlockSpec((FFN_HALO, tc), lambda j, i: (jnp.maximum(i * hb - 1, 0), j)),
                  pl.BlockSpec((FFN_HALO, tc), lambda j, i: (jnp.minimum((i + 1) * hb, last_h), j)),
                  pl.BlockSpec((ts, tc // 2), lambda j, i: (i, j)),
                  pl.BlockSpec((FFN_HALO, tc // 2), lambda j, i: (jnp.minimum((i + 1) * hb, last_h), j)),
                  pl.BlockSpec((FFN_KERNEL, tc), lambda j, i: (0, j)),
                  pl.BlockSpec((1, tc), lambda j, i: (0, j))] + [_ANY] * len(ex_in),
        out_specs=[pl.BlockSpec((ts, tc), lambda j, i: (i, j)), pl.BlockSpec((FFN_HALO, tc), lambda j, i: (0, j)),
                   pl.BlockSpec((1, tc), lambda j, i: (0, j))] + [_ANY] * len(ex_out),
        out_shape=[jax.ShapeDtypeStruct((S, F2), MXU_DTYPE), jax.ShapeDtypeStruct((FFN_HALO, F2), F32),
                   jax.ShapeDtypeStruct((1, F2), F32)] + list(ex_out),
        scratch_shapes=[pltpu.VMEM((ts + 2 * FFN_HALO, tc), F32), pltpu.VMEM((E, tc), F32)] + list(ex_scr),
        compiler_params=_params("parallel" if exchange is None else "arbitrary", "arbitrary"),
    )(up, up, up, dact, dact, dw, dw_b, *ex_in)


def _slopes(n_heads_total):
    return np.asarray(2.0 ** (-ALIBI_MAX * (np.arange(n_heads_total, dtype=np.float32) + 1.0) / n_heads_total), np.float32)


def _qkv_proj(h, w_qkvT, *, grp, name):
    S, D = h.shape
    H = D // HEAD_DIM
    tm = _pick(S, 1024, 16)

    rows = _pick(tm, 256, 16)

    def body(a_ref, b_ref, o_ref, r_ref):
        j = pl.program_id(1)
        r_ref[...] = jnp.zeros_like(r_ref)

        def product(c):
            return lax.dot_general(a_ref[c * rows:(c + 1) * rows, :].astype(MXU_DTYPE), b_ref[...].astype(MXU_DTYPE),
                                   (_DIMS["nt"], ((), ())), preferred_element_type=F32)

        @pl.when(j < 2)
        def _():
            acc = product(0)
            for c in range(tm // rows):
                nxt = product(c + 1) if c + 1 < tm // rows else None
                rs = slice(c * rows, (c + 1) * rows)
                for hd in range(H):
                    hs = slice(hd * HEAD_DIM, (hd + 1) * HEAD_DIM)
                    xv = acc[:, hs]
                    r = lax.rsqrt(jnp.mean(xv * xv, axis=-1, keepdims=True) + EPS)
                    o_ref[rs, hs] = (xv * r).astype(o_ref.dtype)
                    r_ref[rs, hd:hd + 1] = r
                acc = nxt

        @pl.when(j == 2)
        def _():
            o_ref[...] = lax.dot_general(a_ref[...].astype(MXU_DTYPE), b_ref[...].astype(MXU_DTYPE), (_DIMS["nt"], ((), ())),
                                         preferred_element_type=F32).astype(o_ref.dtype)

    return pl.pallas_call(
        body, name=name, grid=(S // tm, 3),
        in_specs=[pl.BlockSpec((tm, D), lambda i, j: (i, 0)), pl.BlockSpec((D, D), lambda i, j: (grp * 3 + j, 0))],
        out_specs=[pl.BlockSpec((tm, D), lambda i, j: (i, j)), pl.BlockSpec((tm, HEAD_DIM), lambda i, j: (i, j))],
        out_shape=[jax.ShapeDtypeStruct((S, 3 * D), MXU_DTYPE), jax.ShapeDtypeStruct((S, 3 * HEAD_DIM), F32)],
        compiler_params=_params("parallel", "parallel"),
    )(h, w_qkvT)


def _band(b, dil):
    qi = lax.broadcasted_iota(jnp.int32, (BLOCK, 2 * BLOCK), 0)
    ki = lax.broadcasted_iota(jnp.int32, (BLOCK, 2 * BLOCK), 1)
    delta = qi + BLOCK - ki
    valid = (delta >= 0) & (delta <= BLOCK) & ((ki >= BLOCK) | (b > 0))
    return valid, (delta * dil).astype(F32)


def _attn_fwd(qkv, qg, kg, *, grp, name):
    S, W = qkv.shape
    D = W // 3
    H = D // HEAD_DIM
    dil = DILATED_GROUPS[grp][1]
    L = S // dil
    nb = L // BLOCK
    slopes = _slopes(3 * H)[grp * H:(grp + 1) * H]
    scale = HEAD_DIM ** -0.5

    def body(q_ref, kp_ref, kc_ref, vp_ref, vc_ref, qg_ref, kg_ref, o_ref, l_ref):
        b = pl.program_id(1)
        valid, dist = _band(b, dil)
        l_ref[...] = jnp.zeros_like(l_ref)
        ss = []
        for h in range(H):
            hs = slice(h * HEAD_DIM, (h + 1) * HEAD_DIM)
            qn = (q_ref[:, hs].astype(F32) * qg_ref[h:h + 1, :]).astype(MXU_DTYPE)
            kp = (kp_ref[:, hs].astype(F32) * kg_ref[h:h + 1, :]).astype(MXU_DTYPE)
            kc = (kc_ref[:, hs].astype(F32) * kg_ref[h:h + 1, :]).astype(MXU_DTYPE)
            ss.append(lax.dot_general(qn, jnp.concatenate([kp, kc], axis=0), (((1,), (1,)), ((), ())), preferred_element_type=F32))
        ps = []
        for h in range(H):
            s = jnp.where(valid, ss[h] * scale - float(slopes[h]) * dist, NEG)
            m = jnp.max(s, axis=-1, keepdims=True)
            p = jnp.exp(s - m)
            den = jnp.sum(p, axis=-1, keepdims=True)
            l_ref[:, h:h + 1] = m + jnp.log(den)
            ps.append((p.astype(MXU_DTYPE), den))
        for h in range(H):
            hs = slice(h * HEAD_DIM, (h + 1) * HEAD_DIM)
            pb, den = ps[h]
            v2 = jnp.concatenate([vp_ref[:, hs], vc_ref[:, hs]], axis=0).astype(MXU_DTYPE)
            o_ref[:, hs] = (jnp.dot(pb, v2, preferred_element_type=F32) / den).astype(o_ref.dtype)

    def cur(j):
        return lambda r, b: (r * nb + b, j)

    def prv(j):
        return lambda r, b: (r * nb + jnp.maximum(b - 1, 0), j)

    blk = (BLOCK, D)
    gain = pl.BlockSpec((H, HEAD_DIM), lambda r, b: (0, 0))
    return pl.pallas_call(
        body, name=name, grid=(dil, nb),
        in_specs=[pl.BlockSpec(blk, cur(0)), pl.BlockSpec(blk, prv(1)), pl.BlockSpec(blk, cur(1)),
                  pl.BlockSpec(blk, prv(2)), pl.BlockSpec(blk, cur(2)), gain, gain],
        out_specs=[pl.BlockSpec(blk, cur(0)), pl.BlockSpec((BLOCK, HEAD_DIM), cur(0))],
        out_shape=[jax.ShapeDtypeStruct((S, D), MXU_DTYPE), jax.ShapeDtypeStruct((S, HEAD_DIM), F32)],
        compiler_params=_params("parallel", "parallel"),
    )(qkv, qkv, qkv, qkv, qkv, qg, kg)


def _attn_merge(os_, ls_, dils, *, name):
    S, D = os_[0].shape
    H = D // HEAD_DIM
    G = len(dils)
    ts = _pick(S, SUB_TILE, 16 * max(dils))
    subs = [g for g in range(G) if dils[g] > 1]

    def body(*refs):
        o_refs, l_refs = refs[0:G], refs[G:2 * G]
        outb_ref = refs[2 * G]
        lt_refs = refs[2 * G + 1:3 * G + 1]
        scratch = refs[3 * G + 1:]
        lt_tok = scratch[0]
        o_tok = {g: scratch[1 + 2 * j] for j, g in enumerate(subs)}
        l_tok = {g: scratch[2 + 2 * j] for j, g in enumerate(subs)}
        for g in subs:
            _sub_to_tok(o_refs[g], o_tok[g], dils[g])
            _sub_to_tok(l_refs[g], l_tok[g], dils[g])
        lt_tok[0] = jnp.zeros((ts, HEAD_DIM), F32)
        for h in range(H):
            hs = slice(h * HEAD_DIM, (h + 1) * HEAD_DIM)
            ls = [l_tok[g][0][:, h:h + 1] if g in subs else l_refs[g][:, h:h + 1] for g in range(G)]
            ov = [o_tok[g][h] if g in subs else o_refs[g][:, hs].astype(F32) for g in range(G)]
            m = functools.reduce(jnp.maximum, ls)
            es = [jnp.exp(l - m) for l in ls]
            den = functools.reduce(lambda a, b: a + b, es)
            out = functools.reduce(lambda a, b: a + b, [e * o for e, o in zip(es, ov)]) / den
            outb_ref[:, hs] = out.astype(outb_ref.dtype)
            lt_tok.at[0][:, h:h + 1] = m + jnp.log(den)
        for g in range(G):
            if g in subs:
                _tok_to_sub(lt_tok, lt_refs[g], dils[g])
            else:
                lt_refs[g][...] = lt_tok[0]

    def spec(g, cols):
        return _sub_spec(dils[g], ts, cols) if g in subs else pl.BlockSpec((ts, cols), lambda i: (i, 0))

    def shape(g, cols, dtype):
        return jax.ShapeDtypeStruct((dils[g], S // dils[g], cols) if g in subs else (S, cols), dtype)

    def view(a, g):
        return a.reshape(dils[g], S // dils[g], a.shape[-1]) if g in subs else a

    outs = pl.pallas_call(
        body, name=name, grid=(S // ts,),
        in_specs=[spec(g, D) for g in range(G)] + [spec(g, HEAD_DIM) for g in range(G)],
        out_specs=[pl.BlockSpec((ts, D), lambda i: (i, 0))] + [spec(g, HEAD_DIM) for g in range(G)],
        out_shape=[jax.ShapeDtypeStruct((S, D), MXU_DTYPE)] + [shape(g, HEAD_DIM, F32) for g in range(G)],
        scratch_shapes=[pltpu.VMEM((1, ts, HEAD_DIM), F32)] + [pltpu.VMEM((H, ts, HEAD_DIM), F32), pltpu.VMEM((1, ts, HEAD_DIM), F32)] * len(subs),
        compiler_params=_params("parallel"),
    )(*[view(o, g) for g, o in enumerate(os_)], *[view(l, g) for g, l in enumerate(ls_)])
    return outs[0], [t.reshape(S, HEAD_DIM) for t in outs[1:]]


def _attn_delta(do, out, dils, *, name):
    S, D = out.shape
    H = D // HEAD_DIM
    G = len(dils)
    ts = _pick(S, SUB_TILE, 16 * max(dils))
    subs = [g for g in range(G) if dils[g] > 1]

    def body(do_ref, o_ref, *rest):
        d_refs = rest[0:G]
        dos_refs = rest[G:G + len(subs)]
        d_tok, do_tok = rest[G + len(subs):]
        d_tok[0] = jnp.zeros((ts, HEAD_DIM), F32)
        for h in range(H):
            hs = slice(h * HEAD_DIM, (h + 1) * HEAD_DIM)
            dov = do_ref[:, hs].astype(F32)
            do_tok[h] = dov
            d_tok.at[0][:, h:h + 1] = jnp.sum(dov * o_ref[:, hs].astype(F32), axis=-1, keepdims=True)
        for g in range(G):
            if g in subs:
                _tok_to_sub(d_tok, d_refs[g], dils[g])
            else:
                d_refs[g][...] = d_tok[0]
        for g, dst in zip(subs, dos_refs):
            _tok_to_sub(do_tok, dst, dils[g])

    def spec(g, cols):
        return _sub_spec(dils[g], ts, cols) if g in subs else pl.BlockSpec((ts, cols), lambda i: (i, 0))

    def shape(g, cols, dtype):
        return jax.ShapeDtypeStruct((dils[g], S // dils[g], cols) if g in subs else (S, cols), dtype)

    row = pl.BlockSpec((ts, D), lambda i: (i, 0))
    outs = pl.pallas_call(
        body, name=name, grid=(S // ts,), in_specs=[row, row],
        out_specs=[spec(g, HEAD_DIM) for g in range(G)] + [spec(g, D) for g in subs],
        out_shape=[shape(g, HEAD_DIM, F32) for g in range(G)] + [shape(g, D, do.dtype) for g in subs],
        scratch_shapes=[pltpu.VMEM((1, ts, HEAD_DIM), F32), pltpu.VMEM((H, ts, HEAD_DIM), F32)],
        compiler_params=_params("parallel"),
    )(do, out)
    deltas = [t.reshape(S, HEAD_DIM) for t in outs[0:G]]
    dos = {g: t.reshape(S, D) for g, t in zip(subs, outs[G:])}
    return deltas, [dos[g] if g in subs else do for g in range(G)]


def _attn_bwd(qkv, rqk, do, lse, delta, qg, kg, *, grp, name):
    S, W = qkv.shape
    D = W // 3
    H = D // HEAD_DIM
    dil = DILATED_GROUPS[grp][1]
    L = S // dil
    nb = L // BLOCK
    slopes = _slopes(3 * H)[grp * H:(grp + 1) * H]
    scale = HEAD_DIM ** -0.5

    def body(q_ref, qp_ref, kp_ref, kc_ref, vp_ref, vc_ref, do_ref, l_ref, dl_ref, rq_ref, rk_ref, qg_ref, kg_ref,
             out_ref, dqg_ref, dkg_ref, cq_ref, ck_ref, cv_ref, nq_ref, nk_ref, nv_ref, pk_ref, pv_ref):
        r = pl.program_id(0)
        b = pl.program_id(1)

        @pl.when(jnp.logical_and(r == 0, b == 0))
        def _():
            dqg_ref[...] = jnp.zeros_like(dqg_ref)
            dkg_ref[...] = jnp.zeros_like(dkg_ref)

        @pl.when(b < nb)
        def _():
            valid, dist = _band(b, dil)

            def operands(h):
                hs = slice(h * HEAD_DIM, (h + 1) * HEAD_DIM)
                qn = (q_ref[:, hs].astype(F32) * qg_ref[h:h + 1, :]).astype(MXU_DTYPE)
                kp = (kp_ref[:, hs].astype(F32) * kg_ref[h:h + 1, :]).astype(MXU_DTYPE)
                kc = (kc_ref[:, hs].astype(F32) * kg_ref[h:h + 1, :]).astype(MXU_DTYPE)
                k2 = jnp.concatenate([kp, kc], axis=0)
                v2 = jnp.concatenate([vp_ref[:, hs], vc_ref[:, hs]], axis=0).astype(MXU_DTYPE)
                return hs, qn, k2, v2, do_ref[:, hs].astype(MXU_DTYPE)

            sdp = []
            for h in range(H):
                hs, qn, k2, v2, doh = operands(h)
                s = lax.dot_general(qn, k2, (((1,), (1,)), ((), ())), preferred_element_type=F32)
                dp = lax.dot_general(doh, v2, (((1,), (1,)), ((), ())), preferred_element_type=F32)
                sdp.append((s, dp))
            pds = []
            for h in range(H):
                s, dp = sdp[h]
                s = jnp.where(valid, s * scale - float(slopes[h]) * dist, NEG)
                p = jnp.exp(s - l_ref[:, h:h + 1])
                pds.append((p.astype(MXU_DTYPE), (p * (dp - dl_ref[:, h:h + 1]) * scale).astype(MXU_DTYPE)))
            for h in range(H):
                hs, qn, k2, v2, doh = operands(h)
                pb, dsc = pds[h]
                nq_ref[:, hs] = jnp.dot(dsc, k2, preferred_element_type=F32)
                dk2 = lax.dot_general(dsc, qn, (((0,), (0,)), ((), ())), preferred_element_type=F32)
                dv2 = lax.dot_general(pb, doh, (((0,), (0,)), ((), ())), preferred_element_type=F32)
                pk_ref[:, hs] = dk2[0:BLOCK]
                nk_ref[:, hs] = dk2[BLOCK:2 * BLOCK]
                pv_ref[:, hs] = dv2[0:BLOCK]
                nv_ref[:, hs] = dv2[BLOCK:2 * BLOCK]

        @pl.when(b == nb)
        def _():
            pk_ref[...] = jnp.zeros_like(pk_ref)
            pv_ref[...] = jnp.zeros_like(pv_ref)

        @pl.when(b > 0)
        def _():
            for h in range(H):
                hs = slice(h * HEAD_DIM, (h + 1) * HEAD_DIM)
                for j, (xh_ref, r_ref, gain_ref, dgain_ref) in enumerate(((qp_ref, rq_ref, qg_ref, dqg_ref),
                                                                          (kp_ref, rk_ref, kg_ref, dkg_ref))):
                    dy = cq_ref[:, hs] if j == 0 else ck_ref[:, hs] + pk_ref[:, hs]
                    gain = gain_ref[h:h + 1, :]
                    xh = xh_ref[:, hs].astype(F32)
                    rr = r_ref[:, h:h + 1]
                    gy = dy * gain
                    dx = rr * (gy - xh * jnp.mean(gy * xh, axis=-1, keepdims=True))
                    out_ref[:, j * D + h * HEAD_DIM:j * D + (h + 1) * HEAD_DIM] = dx.astype(out_ref.dtype)
                    dgain_ref[h:h + 1, :] += jnp.sum(dy * xh, axis=0, keepdims=True)
                out_ref[:, 2 * D + h * HEAD_DIM:2 * D + (h + 1) * HEAD_DIM] = (cv_ref[:, hs] + pv_ref[:, hs]).astype(out_ref.dtype)

        @pl.when(b < nb)
        def _():
            cq_ref[...] = nq_ref[...]
            ck_ref[...] = nk_ref[...]
            cv_ref[...] = nv_ref[...]

    def cur(j):
        return lambda r, b: (r * nb + jnp.minimum(b, nb - 1), j)

    def prv(j):
        return lambda r, b: (r * nb + jnp.clip(b - 1, 0, nb - 1), j)

    blk = (BLOCK, D)
    lblk = pl.BlockSpec((BLOCK, HEAD_DIM), cur(0))
    gain = pl.BlockSpec((H, HEAD_DIM), lambda r, b: (0, 0))
    return pl.pallas_call(
        body, name=name, grid=(dil, nb + 1),
        in_specs=[pl.BlockSpec(blk, cur(0)), pl.BlockSpec(blk, prv(0)), pl.BlockSpec(blk, prv(1)), pl.BlockSpec(blk, cur(1)),
                  pl.BlockSpec(blk, prv(2)), pl.BlockSpec(blk, cur(2)), pl.BlockSpec(blk, cur(0)), lblk, lblk,
                  pl.BlockSpec((BLOCK, HEAD_DIM), prv(0)), pl.BlockSpec((BLOCK, HEAD_DIM), prv(1)), gain, gain],
        out_specs=[pl.BlockSpec((BLOCK, 3 * D), lambda r, b: (r * nb + jnp.maximum(b - 1, 0), 0)), gain, gain],
        out_shape=[jax.ShapeDtypeStruct((S, 3 * D), MXU_DTYPE), jax.ShapeDtypeStruct((H, HEAD_DIM), F32),
                   jax.ShapeDtypeStruct((H, HEAD_DIM), F32)],
        scratch_shapes=[pltpu.VMEM(blk, F32)] * 8,
        compiler_params=_params("arbitrary", "arbitrary"),
    )(qkv, qkv, qkv, qkv, qkv, qkv, do, lse, delta, rqk, rqk, qg, kg)


def _adamw(w, m, v, terms, slots, *, name):
    R, C = w.shape
    nt = len(terms)
    tr = _pick(R, 256, 16)
    c1 = 1.0 - ADAM_B1 ** ADAM_STEP
    c2 = 1.0 - ADAM_B2 ** ADAM_STEP

    def body(slot_ref, w_ref, m_ref, v_ref, *rest):
        t_refs = rest[:nt]
        g_ref, d_ref, nm_ref, nv_ref = rest[nt:]
        g = t_refs[0][...].astype(F32)
        for t in t_refs[1:]:
            g = g + t[...].astype(F32)
        mm = ADAM_B1 * m_ref[...] + (1.0 - ADAM_B1) * g
        vv = ADAM_B2 * v_ref[...] + (1.0 - ADAM_B2) * (g * g)
        m_hat = mm / c1
        v_hat = vv / c2
        g_ref[...] = g
        d_ref[...] = -ADAM_LR * (m_hat / (jnp.sqrt(v_hat) + ADAM_EPS) + ADAM_WD * w_ref[...])
        nm_ref[...] = mm
        nv_ref[...] = vv

    row = pl.BlockSpec((tr, C), lambda i, s: (i, 0))
    grid_spec = pltpu.PrefetchScalarGridSpec(
        num_scalar_prefetch=1, grid=(R // tr,),
        in_specs=[row, row, row] + [pl.BlockSpec((None, tr, C), lambda i, s, t=t: (s[t], i, 0)) for t in range(nt)],
        out_specs=[row] * 4)
    return pl.pallas_call(
        body, name=name, grid_spec=grid_spec, out_shape=[jax.ShapeDtypeStruct((R, C), F32)] * 4,
        compiler_params=_params("parallel"),
    )(slots, w, m, v, *terms)


def _chip_partials(g, sib, core, *, name):
    _, R, C = g.shape
    tr = _pick(R, 1200, 16)

    def body(core_ref, g_ref, s_ref, o_ref):
        o_ref[...] = (g_ref[...] + s_ref[...].astype(F32)).astype(o_ref.dtype)

    grid_spec = pltpu.PrefetchScalarGridSpec(
        num_scalar_prefetch=1, grid=(4, R // tr),
        in_specs=[pl.BlockSpec((None, tr, C), lambda k, i, c: (2 * k + c[0], i, 0)),
                  pl.BlockSpec((None, tr, C), lambda k, i, c: (k, i, 0))],
        out_specs=pl.BlockSpec((None, tr, C), lambda k, i, c: (k, i, 0)))
    return pl.pallas_call(
        body, name=name, grid_spec=grid_spec, out_shape=jax.ShapeDtypeStruct((4, R, C), sib.dtype),
        compiler_params=_params("parallel", "parallel"),
    )(core, g, sib)


_ANY = pl.BlockSpec(memory_space=pl.ANY)


def _place():
    return lax.axis_index("x"), lax.axis_index("y"), lax.axis_index("c")


class _Exchange:
    def __init__(self, operands, out_shapes, scratch, emit):
        self.operands, self.out_shapes, self.scratch, self.emit = operands, out_shapes, scratch, emit


def _run_exchange(ex, *, name):
    n_in, n_out = len(ex.operands), len(ex.out_shapes)

    def body(*refs):
        ex.emit(0, 1, refs[:n_in], refs[n_in:n_in + n_out], refs[n_in + n_out:])

    return pl.pallas_call(body, name=name, in_specs=[_ANY] * n_in, out_specs=[_ANY] * n_out, out_shape=ex.out_shapes,
                          scratch_shapes=ex.scratch)(*ex.operands)


def _gather_exchange(shard):
    R, C = shard.shape

    def emit(step, n, ins, outs, sems):
        x_ref, out_ref = ins[0], outs[0]
        send_sems, recv_sems, local_sem = sems
        x, y, c = _place()
        me, sibling = (x, y, c), (x, y, 1 - c)
        chips = [(1 - x, y), (x, 1 - y), (1 - x, 1 - y)]

        def slot(px, py, pc):
            return out_ref.at[4 * px + 2 * py + pc]

        def copy(k, block, to, src=None):
            return pltpu.make_async_remote_copy(
                src_ref=slot(*block) if src is None else src, dst_ref=slot(*block),
                send_sem=send_sems.at[k], recv_sem=recv_sems.at[k], device_id=to, device_id_type=MESH)

        mine = pltpu.make_async_copy(x_ref, slot(*me), local_sem)
        first = [copy(0, me, sibling, src=x_ref)] + [copy(1 + j, me, (*chip, c), src=x_ref) for j, chip in enumerate(chips)]
        passed = [copy(4 + j, (*chip, c), sibling) for j, chip in enumerate(chips)]

        @pl.when(step == 0)
        def _():
            mine.start()
            for cp in first:
                cp.start()

        for j, chip in enumerate(chips):
            @pl.when(step == max(n - 2 * (len(chips) - j), 0))
            def _(j=j, chip=chip):
                copy(1 + j, (*chip, c), me).wait_recv()
                passed[j].start()

        @pl.when(step == n - 1)
        def _():
            copy(0, sibling, me).wait_recv()
            for j, chip in enumerate(chips):
                copy(4 + j, (*chip, 1 - c), me).wait_recv()
            for cp in first + passed:
                cp.wait_send()
            mine.wait()

    return _Exchange([shard], [jax.ShapeDtypeStruct((N_DEV, R, C), shard.dtype)],
                     [pltpu.SemaphoreType.DMA((7,)), pltpu.SemaphoreType.DMA((7,)), pltpu.SemaphoreType.DMA], emit)


def _all_gather(shard, *, name):
    return _run_exchange(_gather_exchange(shard), name=name)[0]


def _rs_sibling(g, *, name):
    _, R, C = g.shape

    def body(g_ref, sib_ref, send_sems, recv_sems):
        x, y, c = _place()
        sends = [pltpu.make_async_remote_copy(
            src_ref=g_ref.at[2 * k + (1 - c)], dst_ref=sib_ref.at[k], send_sem=send_sems.at[k], recv_sem=recv_sems.at[k],
            device_id=(x, y, 1 - c), device_id_type=MESH) for k in range(4)]
        for cp in sends:
            cp.start()
        for cp in sends:
            cp.wait_recv()
        for cp in sends:
            cp.wait_send()

    return pl.pallas_call(
        body, name=name, in_specs=[_ANY], out_specs=_ANY, out_shape=jax.ShapeDtypeStruct((4, R, C), g.dtype),
        scratch_shapes=[pltpu.SemaphoreType.DMA((4,)), pltpu.SemaphoreType.DMA((4,))],
    )(g)


def _chips_exchange(part):
    _, R, C = part.shape

    def emit(step, n, ins, outs, sems):
        p_ref, out_ref = ins[0], outs[0]
        send_sems, recv_sems = sems
        x, y, c = _place()
        chips = [(1 - x, y), (x, 1 - y), (1 - x, 1 - y)]
        sends = [pltpu.make_async_remote_copy(
            src_ref=p_ref.at[2 * px + py], dst_ref=out_ref.at[j], send_sem=send_sems.at[j], recv_sem=recv_sems.at[j],
            device_id=(px, py, c), device_id_type=MESH) for j, (px, py) in enumerate(chips)]

        @pl.when(step == 0)
        def _():
            for cp in sends:
                cp.start()

        @pl.when(step == n - 1)
        def _():
            for cp in sends:
                cp.wait_recv()
            for cp in sends:
                cp.wait_send()

    return _Exchange([part], [jax.ShapeDtypeStruct((3, R, C), part.dtype)],
                     [pltpu.SemaphoreType.DMA((3,)), pltpu.SemaphoreType.DMA((3,))], emit)


def _rs_chips(part, *, name):
    return _run_exchange(_chips_exchange(part), name=name)[0]


def _interleave_rows(wt):
    F2, D = wt.shape
    return wt.reshape(2, F2 // 256, 128, D).transpose(1, 0, 2, 3).reshape(F2, D)


def _deinterleave_rows(wt):
    F2, D = wt.shape
    return wt.reshape(F2 // 256, 2, 128, D).transpose(1, 0, 2, 3).reshape(F2, D)


def _interleave_cols(v):
    k, F2 = v.shape
    return v.reshape(k, 2, F2 // 256, 128).transpose(0, 2, 1, 3).reshape(k, F2)


def _deinterleave_cols(v):
    k, F2 = v.shape
    return v.reshape(k, F2 // 256, 2, 128).transpose(0, 2, 1, 3).reshape(k, F2)


def _pack_rows(parts):
    return jnp.concatenate(parts, axis=0)


def _flat_pack(parts, width):
    flat = jnp.concatenate([p.reshape(-1) for p in parts])
    pad = (-flat.shape[0]) % (8 * width)
    return jnp.pad(flat, (0, pad)).reshape(-1, width)


def _flat_unpack(packed, shapes):
    flat = packed.reshape(-1)
    out, off = [], 0
    for shp in shapes:
        n = int(np.prod(shp))
        out.append(flat[off:off + n].reshape(shp))
        off += n
    return out


def _ffn_forward(x, hf, wupT, wdown, dw_i, dwb_i, tag, loss_target=None):
    up, act = _ffn_up_act(hf, wupT, dw_i, dwb_i, name=f"ffn{tag}_up")
    if loss_target is None:
        y = _mm(act, wdown, mode="nn", out_dtype=F32, name=f"ffn{tag}_down", residual=x)
    else:
        y = _mm(act, wdown, mode="nn", out_dtype=F32, name=f"ffn{tag}_down", residual=x, tm=512, post=_post_loss(loss_target),
                keep_main=False)
    return y, (hf, up, act)


def _ffn_backward(x, g_ffn, wupT, wdown, dw_i, dwb_i, saved, dy, dyb, tag, exchange=None):
    hf, up, act = saved
    dact = _mm(dyb, wdown, mode="nt", out_dtype=MXU_DTYPE, name=f"ffn{tag}_dact", tm=1024, tn=1408)
    d_wdown = _mm(act, dyb, mode="tn", out_dtype=F32, name=f"ffn{tag}_dwdown", tm=1408, tk=2048)
    dup, d_dw_i, d_dwb_i, *carried = _ffn_act_bwd(up, dact, dw_i, dwb_i, name=f"ffn{tag}_actbwd", exchange=exchange)
    dx, dxb, dg, cs = _mm_post(dup, wupT, _post_rms_bwd(x, g_ffn, dy), name=f"ffn{tag}_dhf")
    d_wupT = _mm(dup, hf, mode="tn", out_dtype=F32, name=f"ffn{tag}_dwup", tm=1408, tk=2048)
    return dx, dxb, cs, dict(w_upT=d_wupT, w_down=d_wdown, dw=d_dw_i[0:FFN_KERNEL], dw_b=d_dwb_i, norm=dg), carried


def _local_step(x, target, p, late_weights=None, early_reduce=None):
    S, D = x.shape
    H = D // HEAD_DIM
    h0 = _rms_fwd(x, p["norm_mix"][0:1], name="l0_rms")
    u = _mm(h0, p["w_inT"], mode="nt", out_dtype=F32, name="l0_in", bias=p["cm_b_in"])
    c, s, *carried = _cm_fwd(u, p["cm_dw"], p["cm_dw_b"], p["cm_ln_g"], p["cm_ln_b"], name="l0_conv",
                             exchange=None if late_weights is None else late_weights[0])
    if late_weights is not None:
        p = {**p, **late_weights[1](carried)}
    x1, hf0 = _mm(s, p["w_out"], mode="nn", out_dtype=F32, name="l0_out", bias=p["cm_b_out"], residual=x,
                  post=_post_rms(p["norm_ffn"][0:1]))
    x2, sv0 = _ffn_forward(x1, hf0, p["w_upT"][0], p["w_down"][0], p["ff_dw"][0], p["ff_dw_b"][0:1], 0)
    dils = [dil for _, dil in DILATED_GROUPS]
    assert dils[0] == 1
    h1s = [t.reshape(S, D) for t in _rms_fwd(x2, p["norm_mix"][1:2], name="l1_rms", subs=tuple(dils[1:]))]
    qkvs, rqks, os_, ls_ = [], [], [], []
    for g in range(len(dils)):
        qkv_g, r_g = _qkv_proj(h1s[g], p["w_qkvT"], grp=g, name=f"l1_qkv{g}")
        qkvs.append(qkv_g)
        rqks.append(r_g)
        o, l = _attn_fwd(qkvs[g], p["at_q_norm"][g * H:(g + 1) * H], p["at_k_norm"][g * H:(g + 1) * H], grp=g, name=f"l1_attn{g}")
        os_.append(o)
        ls_.append(l)
    outb, lses = _attn_merge(os_, ls_, dils, name="l1_merge")
    x3, hf1 = _mm(outb, p["w_o"], mode="nn", out_dtype=F32, name="l1_o", residual=x2, post=_post_rms(p["norm_ffn"][1:2]))
    (dx4, dx4b, loss), sv1 = _ffn_forward(x3, hf1, p["w_upT"][1], p["w_down"][1], p["ff_dw"][1], p["ff_dw_b"][1:2], 1,
                                          loss_target=target)
    dx3, dx3b, _, gf1, _ = _ffn_backward(x3, p["norm_ffn"][1:2], p["w_upT"][1], p["w_down"][1], p["ff_dw"][1], p["ff_dw_b"][1:2],
                                      sv1, dx4, dx4b, 1)
    do = _mm(dx3b, p["w_o"], mode="nt", out_dtype=MXU_DTYPE, name="l1_do")
    d_wo = _mm(outb, dx3b, mode="tn", out_dtype=F32, name="l1_dwo", tk=2048)
    deltas, dos = _attn_delta(do, outb, dils, name="l1_delta")
    dh1s, d_wqkvT, dqg, dkg = [], [], [], []
    for g, dil in enumerate(dils):
        dqkv_g, a, b_ = _attn_bwd(qkvs[g], rqks[g], dos[g], lses[g], deltas[g], p["at_q_norm"][g * H:(g + 1) * H],
                                  p["at_k_norm"][g * H:(g + 1) * H], grp=g, name=f"l1_attnbwd{g}")
        dqg.append(a)
        dkg.append(b_)
        d_wqkvT.append(_mm(dqkv_g, h1s[g], mode="tn", out_dtype=F32, name=f"l1_dwqkv{g}", tk=2048))
        dh1s.append(_mm(dqkv_g, p["w_qkvT"], mode="nn", out_dtype=MXU_DTYPE, name=f"l1_dh{g}", b_off=g * 3 * D, b_len=3 * D))
    dx2, dx2b, dgm1, _ = _rms_bwd(x2, p["norm_mix"][1:2], dh1s[0:1], dx3, name="l1_rmsbwd",
                                  dh_subs=[(dh1s[g].reshape(dils[g], S // dils[g], D), dils[g]) for g in range(1, len(dils))])
    ex, finish = (None, None) if early_reduce is None else early_reduce(
        dict(w_qkvT=jnp.concatenate(d_wqkvT, axis=0), w_o=d_wo, w_upT=gf1["w_upT"], w_down=gf1["w_down"]))
    dx1, dx1b, cs1, gf0, carried = _ffn_backward(x1, p["norm_ffn"][0:1], p["w_upT"][0], p["w_down"][0], p["ff_dw"][0],
                                                 p["ff_dw_b"][0:1], sv0, dx2, dx2b, 0, exchange=ex)
    reduced = [] if finish is None else [finish(carried)]
    ex, finish = (None, None) if early_reduce is None else early_reduce(dict(w_upT=gf0["w_upT"], w_down=gf0["w_down"]))
    dc, d_lng, d_lnb = _mm_post(dx1b, p["w_out"].T, _post_ln_silu_bwd(c, p["cm_ln_g"], p["cm_ln_b"]), name="l0_ds")
    d_wout = _mm(s, dx1b, mode="tn", out_dtype=F32, name="l0_dwout", tk=2048)
    du, d_cmdw, d_cmdwb, d_bin, *carried = _cm_conv_bwd(dc, u, p["cm_dw"], name="l0_convbwd", exchange=ex)
    if finish is not None:
        reduced.append(finish(carried))
    grad_x, _, dgm0, _ = _mm_post(du, p["w_inT"], _post_rms_bwd(x, p["norm_mix"][0:1], dx1), name="l0_dh")
    d_winT = _mm(du, h0, mode="tn", out_dtype=F32, name="l0_dwin", tk=2048)
    grads = dict(
        norm_mix=jnp.concatenate([dgm0, dgm1], axis=0),
        norm_ffn=jnp.concatenate([gf0["norm"], gf1["norm"]], axis=0),
        w_inT=d_winT, cm_b_in=d_bin, cm_dw=d_cmdw[0:CONV_KERNEL], cm_dw_b=d_cmdwb, cm_ln_g=d_lng, cm_ln_b=d_lnb,
        w_out=d_wout, cm_b_out=cs1,
        w_qkvT=jnp.concatenate(d_wqkvT, axis=0), at_q_norm=jnp.concatenate(dqg, axis=0), at_k_norm=jnp.concatenate(dkg, axis=0),
        w_o=d_wo,
        w_upT=[gf0["w_upT"], gf1["w_upT"]], w_down=[gf0["w_down"], gf1["w_down"]],
        ff_dw=jnp.stack([gf0["dw"], gf1["dw"]]), ff_dw_b=jnp.concatenate([gf0["dw_b"], gf1["dw_b"]], axis=0),
    )
    return loss, grad_x, grads, reduced


_BIG = ("cm_w_in", "cm_w_out", "at_w_qkv", "at_w_out", "ff_w_up", "ff_w_down")
_TRANSPOSED = ("cm_w_in", "at_w_qkv", "ff_w_up")
_SMALL = ("norm_mix", "norm_ffn", "cm_b_in", "cm_dw_b", "cm_ln_g", "cm_ln_b", "cm_b_out", "at_q_norm", "at_k_norm",
          "ff_dw_b", "cm_dw", "ff_dw")
_SMALL_SHARDED = ("cm_dw", "ff_dw")
_ORDER = ("norm_mix", "norm_ffn", "cm_w_in", "cm_b_in", "cm_dw", "cm_dw_b", "cm_ln_g", "cm_ln_b", "cm_w_out", "cm_b_out",
          "at_w_qkv", "at_q_norm", "at_k_norm", "at_w_out", "ff_w_up", "ff_dw", "ff_dw_b", "ff_w_down")


_UNITS = (("cm_w_in", 0), ("cm_w_out", 0), ("ff_w_up", 0), ("ff_w_down", 0),
          ("at_w_qkv", 0), ("at_w_out", 0), ("ff_w_up", 1), ("ff_w_down", 1))
_N_FIRST = 2
_N_LAYER0 = 4


def _unit_rows(t, n, l):
    return t[n].shape[2] if n in _TRANSPOSED else t[n].shape[1]


def _big_rows(t, units=_UNITS):
    return _pack_rows([t[n][l].T if n in _TRANSPOSED else t[n][l] for n, l in units])


def _big_unrows(packed, like):
    mats, off = {}, 0
    for n, l in _UNITS:
        rows = _unit_rows(like, n, l)
        m = packed[off:off + rows]
        off += rows
        mats[(n, l)] = m.T if n in _TRANSPOSED else m
    return {n: jnp.stack([mats[(n, l)] for l in range(like[n].shape[0])]) for n in _BIG}


def kernel(x, norm_mix, norm_ffn, cm_w_in, cm_b_in, cm_dw, cm_dw_b, cm_ln_g, cm_ln_b, cm_w_out, cm_b_out, at_w_qkv, at_q_norm, at_k_norm, at_w_out, ff_w_up, ff_dw, ff_dw_b, ff_w_down, loss_target, m_norm_mix, m_norm_ffn, m_cm_w_in, m_cm_b_in, m_cm_dw, m_cm_dw_b, m_cm_ln_g, m_cm_ln_b, m_cm_w_out, m_cm_b_out, m_at_w_qkv, m_at_q_norm, m_at_k_norm, m_at_w_out, m_ff_w_up, m_ff_dw, m_ff_dw_b, m_ff_w_down, v_norm_mix, v_norm_ffn, v_cm_w_in, v_cm_b_in, v_cm_dw, v_cm_dw_b, v_cm_ln_g, v_cm_ln_b, v_cm_w_out, v_cm_b_out, v_at_w_qkv, v_at_q_norm, v_at_k_norm, v_at_w_out, v_ff_w_up, v_ff_dw, v_ff_dw_b, v_ff_w_down):
    w = dict(norm_mix=norm_mix, norm_ffn=norm_ffn, cm_w_in=cm_w_in, cm_b_in=cm_b_in, cm_dw=cm_dw, cm_dw_b=cm_dw_b, cm_ln_g=cm_ln_g,
             cm_ln_b=cm_ln_b, cm_w_out=cm_w_out, cm_b_out=cm_b_out, at_w_qkv=at_w_qkv, at_q_norm=at_q_norm, at_k_norm=at_k_norm,
             at_w_out=at_w_out, ff_w_up=ff_w_up, ff_dw=ff_dw, ff_dw_b=ff_dw_b, ff_w_down=ff_w_down)
    m = dict(norm_mix=m_norm_mix, norm_ffn=m_norm_ffn, cm_w_in=m_cm_w_in, cm_b_in=m_cm_b_in, cm_dw=m_cm_dw, cm_dw_b=m_cm_dw_b,
             cm_ln_g=m_cm_ln_g, cm_ln_b=m_cm_ln_b, cm_w_out=m_cm_w_out, cm_b_out=m_cm_b_out, at_w_qkv=m_at_w_qkv,
             at_q_norm=m_at_q_norm, at_k_norm=m_at_k_norm, at_w_out=m_at_w_out, ff_w_up=m_ff_w_up, ff_dw=m_ff_dw,
             ff_dw_b=m_ff_dw_b, ff_w_down=m_ff_w_down)
    v = dict(norm_mix=v_norm_mix, norm_ffn=v_norm_ffn, cm_w_in=v_cm_w_in, cm_b_in=v_cm_b_in, cm_dw=v_cm_dw, cm_dw_b=v_cm_dw_b,
             cm_ln_g=v_cm_ln_g, cm_ln_b=v_cm_ln_b, cm_w_out=v_cm_w_out, cm_b_out=v_cm_b_out, at_w_qkv=v_at_w_qkv,
             at_q_norm=v_at_q_norm, at_k_norm=v_at_k_norm, at_w_out=v_at_w_out, ff_w_up=v_ff_w_up, ff_dw=v_ff_dw,
             ff_dw_b=v_ff_dw_b, ff_w_down=v_ff_w_down)
    S, D = x.shape[1], x.shape[2]
    F2 = ff_dw_b.shape[1]
    H3 = at_q_norm.shape[1]
    me = 4 * lax.axis_index("x") + 2 * lax.axis_index("y") + lax.axis_index("c")

    ix, iy, ic = lax.axis_index("x"), lax.axis_index("y"), lax.axis_index("c")
    chip = 2 * ix + iy
    core = jnp.stack([ic]).astype(jnp.int32)
    w_rows = _big_rows(w)
    unit_rows = [_unit_rows(w, n, l) for n, l in _UNITS]
    n_first = sum(unit_rows[:_N_FIRST])
    n_layer0 = sum(unit_rows[:_N_LAYER0])
    w_wire = w_rows.astype(MXU_DTYPE)

    def unpack(gathered, units, rows):
        full, off = {}, 0
        for (n, l), r in zip(units, rows):
            full[(n, l)] = gathered[:, off:off + r, :].reshape(N_DEV * r, D)
            off += r
        out = {}
        if ("cm_w_in", 0) in full:
            out.update(w_inT=full[("cm_w_in", 0)], w_out=full[("cm_w_out", 0)])
        if ("at_w_qkv", 0) in full:
            out.update(w_qkvT=full[("at_w_qkv", 0)], w_o=full[("at_w_out", 0)],
                       w_upT=[_interleave_rows(full[("ff_w_up", l)]) for l in range(2)],
                       w_down=[full[("ff_w_down", l)] for l in range(2)])
        return out

    first = _all_gather(w_wire[:n_first], name="gather_first")
    late_weights = (_gather_exchange(w_wire[n_first:]),
                    lambda carried: unpack(carried[0], _UNITS[_N_FIRST:], unit_rows[_N_FIRST:]))
    small_sh = _flat_pack([cm_dw, ff_dw], D)
    small_g = _all_gather(small_sh, name="gather_small")
    cm_dw_full = jnp.concatenate([_flat_unpack(small_g[j], [cm_dw.shape, ff_dw.shape])[0][0] for j in range(N_DEV)], axis=-1)
    ff_dw_full = jnp.concatenate([_flat_unpack(small_g[j], [cm_dw.shape, ff_dw.shape])[1] for j in range(N_DEV)], axis=-1)

    p = dict(
        norm_mix=norm_mix, norm_ffn=norm_ffn, cm_b_in=cm_b_in, cm_dw=cm_dw_full, cm_dw_b=cm_dw_b, cm_ln_g=cm_ln_g, cm_ln_b=cm_ln_b,
        cm_b_out=cm_b_out, at_q_norm=at_q_norm[0], at_k_norm=at_k_norm[0],
        ff_dw=jnp.stack([_interleave_cols(ff_dw_full[l]) for l in range(ff_dw_full.shape[0])]),
        ff_dw_b=_interleave_cols(ff_dw_b),
        **unpack(first, _UNITS[:_N_FIRST], unit_rows[:_N_FIRST]),
    )

    def pack(pieces):
        return jnp.concatenate([t.reshape(N_DEV, t.shape[0] // N_DEV, D) for t in pieces], axis=1)

    def reduce_start(pieces, tag):
        g_rows = pack(pieces)
        sib = _rs_sibling(g_rows.astype(WIRE_DTYPE), name=f"reduce{tag}_sibling")
        return g_rows, sib, _chip_partials(g_rows, sib, core, name=f"reduce{tag}_add")

    def early_reduce(gd):
        ffn = [_deinterleave_rows(gd["w_upT"]), gd["w_down"]]
        tag, pieces = (2, [gd["w_qkvT"], gd["w_o"]] + ffn) if "w_qkvT" in gd else (1, ffn)
        g_rows, sib, part = reduce_start(pieces, tag)
        return _chips_exchange(part), lambda carried: (g_rows, sib, carried[0])

    loss8, grad_x, g, (reduced2, reduced1) = _local_step(x[0], loss_target[0], p, late_weights, early_reduce)
    loss = lax.psum(loss8[0, 0], ("x", "y", "c"))
    g_rows0, sib0, part0 = reduce_start([g["w_inT"], g["w_out"]], 0)
    reduced0 = (g_rows0, sib0, _rs_chips(part0, name="reduce0_chips"))
    slots = jnp.stack([me, chip, 0 * me, 0 * me + 1, 0 * me + 2]).astype(jnp.int32)
    m_rows, v_rows = _big_rows(m), _big_rows(v)
    updated = []
    for tag, (g_rows, sib, recv), rows in ((0, reduced0, slice(0, n_first)), (1, reduced1, slice(n_first, n_layer0)),
                                           (2, reduced2, slice(n_layer0, None))):
        updated.append(_adamw(w_rows[rows], m_rows[rows], v_rows[rows], [g_rows, sib, recv, recv, recv], slots, name=f"adamw_big{tag}"))
    big = [_big_unrows(jnp.concatenate([u[k] for u in updated], axis=0), w) for k in range(4)]

    g_small = dict(g)
    g_small["cm_b_in"] = g["cm_b_in"]
    g_small["at_q_norm"] = g["at_q_norm"][None]
    g_small["at_k_norm"] = g["at_k_norm"][None]
    g_small["ff_dw_b"] = _deinterleave_cols(g["ff_dw_b"])
    g_small["cm_dw"] = g["cm_dw"][None]
    g_small["ff_dw"] = jnp.stack([_deinterleave_cols(g["ff_dw"][l]) for l in range(g["ff_dw"].shape[0])])
    small_shapes = [g_small[n].shape for n in _SMALL]
    gs_parts = _all_gather(_flat_pack([g_small[n] for n in _SMALL], D), name="gather_small_grads")

    def embed(t, n):
        if n not in _SMALL_SHARDED:
            return t
        full_shape = t.shape[:-1] + (t.shape[-1] * N_DEV,)
        return lax.dynamic_update_slice_in_dim(jnp.zeros(full_shape, F32), t, me * t.shape[-1], axis=t.ndim - 1)

    packs = [_flat_pack([embed(tree[n], n) for n in _SMALL], D) for tree in (w, m, v)]
    gs, ds_, ms, vs = _adamw(packs[0], packs[1], packs[2], [gs_parts] * N_DEV, jnp.arange(N_DEV, dtype=jnp.int32),
                             name="adamw_small")
    small = []
    for t in (gs, ds_, ms, vs):
        un = dict(zip(_SMALL, _flat_unpack(t, small_shapes)))
        for n in _SMALL_SHARDED:
            width = w[n].shape[-1]
            un[n] = lax.dynamic_slice_in_dim(un[n], me * width, width, axis=un[n].ndim - 1)
        small.append({n: un[n].reshape(w[n].shape) for n in _SMALL})

    outs = [loss, grad_x[None]]
    for k in range(4):
        for n in _ORDER:
            outs.append(big[k][n] if n in _BIG else small[k][n])
    return tuple(outs)
```
